```python
import math
import jax
import jax.numpy as jnp
from jax import lax
import numpy as np

D_MODEL = 1024
BATCH = 32
SEQ = 256
DEPTH = 2
DEC_BATCH = 8
DEC_SEQ = 1024
PAST_LEN = 256

GRID_W = 64
N_EVEN = (DEPTH + 1) // 2
N_ODD = DEPTH // 2
EPS = 1e-6
NEG_INF = -1e30
Q_BLOCK = 128
NA_HEADS = 8
HEAD_DIM = 64
D_A = NA_HEADS * HEAD_DIM
WIN_ROWS_MAX = 8
WIN_COLS = 16
D_B = D_MODEL - D_A
HY_ORDER = 2
HY_SHORT = 3
HY_EMB = 33
HY_BANDS = (HY_EMB - 1) // 2
HY_FFN = 64
HY_DECAY_TARGET = 1e-2
HY_FAST_PCT = 0.3
HY_SLOW_PCT = 1.5
D_RNN = D_MODEL
RG_BLOCKS = 16
RG_BLOCK = D_RNN // RG_BLOCKS
RG_CONV = 4
RG_C = 8.0
N_EXPERTS = 16
N_GROUPS = 4
EXPERTS_PER_GROUP = N_EXPERTS // N_GROUPS
TOP_K = 2
D_EXPERT = 512

kernel_name = 'hybrid_natten_hyena_rglru_moe_step'


def rmsnorm(x, g):
    xf = x.astype(jnp.float32)
    y = xf * lax.rsqrt(jnp.mean(xf * xf, axis=-1, keepdims=True) + EPS)
    return (y * g.astype(jnp.float32)).astype(x.dtype)


def dwconv_centred(x, w, b):
    K = w.shape[0]
    L = x.shape[1]
    left = K // 2
    xp = jnp.pad(x, ((0, 0), (left, K - 1 - left), (0, 0)))
    y = xp[:, 0:L, :] * w[0]
    for j in range(1, K):
        y = y + xp[:, j:j + L, :] * w[j]
    return y + b


def context_attention(q, k, v):
    B, Lc, H, Dh = q.shape
    nblk = Lc // Q_BLOCK
    qb = q.reshape(B, nblk, Q_BLOCK, H, Dh).transpose(1, 0, 2, 3, 4)
    scale = Dh ** -0.5

    def block(q_b):
        s = jnp.einsum('bqhd,bkhd->bhqk', q_b, k).astype(jnp.float32) * scale
        p = jax.nn.softmax(s, axis=-1).astype(v.dtype)
        return jnp.einsum('bhqk,bkhd->bqhd', p, v)

    o = lax.map(block, qb)
    return o.transpose(1, 0, 2, 3, 4).reshape(B, Lc, H * Dh)


def neighbourhood_attention(q, k, v, ctx_k, ctx_v, rpb):
    B, L, H, Dh = q.shape
    rows = L // GRID_W
    kr = min(WIN_ROWS_MAX, rows)
    cols = np.arange(GRID_W)
    col_start = np.clip(cols - WIN_COLS // 2, 0, GRID_W - WIN_COLS)
    col_in = (cols[None, :] >= col_start[:, None]) & (cols[None, :] < col_start[:, None] + WIN_COLS)
    dc = np.clip(cols[None, :] - cols[:, None], 1 - WIN_COLS, WIN_COLS - 1) + WIN_COLS - 1
    row_start = jnp.asarray(np.clip(np.arange(rows) - kr // 2, 0, rows - kr), jnp.int32)
    qg = q.reshape(B, rows, GRID_W, H, Dh).transpose(1, 0, 3, 2, 4)
    kg = k.reshape(B, rows, GRID_W, H, Dh)
    vg = v.reshape(B, rows, GRID_W, H, Dh)
    scale = Dh ** -0.5
    n_lat = kr * GRID_W

    def row_block(args):
        r, start, q_r = args
        k_r = lax.dynamic_slice_in_dim(kg, start, kr, axis=1)
        v_r = lax.dynamic_slice_in_dim(vg, start, kr, axis=1)
        dr = start + jnp.arange(kr) - r + WIN_ROWS_MAX - 1
        bias = rpb[:, dr[None, :, None], dc[:, None, :]].astype(jnp.float32)
        s_lat = jnp.einsum('bhqd,bjkhd->bhqjk', q_r, k_r).astype(jnp.float32) * scale + bias
        s_lat = jnp.where(col_in[:, None, :], s_lat, NEG_INF)
        s_ctx = jnp.einsum('bhqd,bhcd->bhqc', q_r, ctx_k).astype(jnp.float32) * scale
        s = jnp.concatenate([s_lat.reshape(B, H, GRID_W, n_lat), s_ctx], axis=-1)
        p = jax.nn.softmax(s, axis=-1).astype(v.dtype)
        p_lat = p[..., :n_lat].reshape(B, H, GRID_W, kr, GRID_W)
        o = jnp.einsum('bhqjk,bjkhd->bqhd', p_lat, v_r)
        return o + jnp.einsum('bhqc,bhcd->bqhd', p[..., n_lat:], ctx_v)

    o = lax.map(row_block, (jnp.arange(rows, dtype=jnp.int32), row_start, qg))
    return o.transpose(1, 0, 2, 3, 4).reshape(B, L, H * Dh)


def hyena_filters(L, w1, b1, w2, b2, w3, freq):
    t = jnp.linspace(0.0, 1.0, L, dtype=jnp.float32)[:, None]
    w = 2.0 * math.pi * jnp.arange(L, dtype=jnp.float32)[:, None] / L
    f = jnp.linspace(1e-4, HY_BANDS - 1, HY_BANDS, dtype=jnp.float32)[None, :]
    z = jnp.concatenate([t, jnp.cos(f * w), -jnp.sin(f * w)], axis=-1)
    h = jnp.sin(freq[0].astype(jnp.float32) * (z @ w1.astype(jnp.float32) + b1.astype(jnp.float32)))
    h = jnp.sin(freq[1].astype(jnp.float32) * (h @ w2.astype(jnp.float32) + b2.astype(jnp.float32)))
    h = h @ w3.astype(jnp.float32)
    max_decay = math.log(HY_DECAY_TARGET) / HY_FAST_PCT
    min_decay = math.log(HY_DECAY_TARGET) / HY_SLOW_PCT
    deltas = jnp.linspace(min_decay, max_decay, D_B, dtype=jnp.float32)
    decay = jnp.exp(-t * jnp.abs(deltas))
    return h.reshape(L, 2 * HY_ORDER, D_B) * decay[:, None, :]


def causal_fft_conv(u, h):
    L = u.shape[1]
    n = 2 * L
    uf = jnp.fft.rfft(u, n=n, axis=1)
    hf = jnp.fft.rfft(h, n=n, axis=0)
    return jnp.fft.irfft(uf * hf, n=n, axis=1)[:, :L]


def bidir_long_conv(u, h_fwd, h_bwd, d):
    y_f = causal_fft_conv(u, h_fwd)
    y_b = jnp.flip(causal_fft_conv(jnp.flip(u, axis=1), h_bwd), axis=1)
    return y_f + y_b + u * d


def hyena_mixer(u, filt, short_w, short_b, bias_d):
    u = dwconv_centred(u, short_w, short_b)
    parts = jnp.split(u, HY_ORDER + 1, axis=-1)
    z = parts[0].astype(jnp.float32)
    for n in range(HY_ORDER):
        z = parts[n + 1].astype(jnp.float32) * bidir_long_conv(
            z, filt[:, 2 * n], filt[:, 2 * n + 1], bias_d[n].astype(jnp.float32))
    return z.astype(u.dtype)


def lin_combine(e1, e2):
    a1, b1 = e1
    a2, b2 = e2
    return a1 * a2, a2 * b1 + b2


def rglru_scan(x, wa, ba, wx, bx, lam, h0, reverse):
    B, L, _ = x.shape
    xf = x.astype(jnp.float32)
    xb = xf.reshape(B, L, RG_BLOCKS, RG_BLOCK)
    r = jax.nn.sigmoid(jnp.einsum('blnd,nde->blne', xb, wa.astype(jnp.float32)).reshape(B, L, D_RNN)
                       + ba.astype(jnp.float32))
    i = jax.nn.sigmoid(jnp.einsum('blnd,nde->blne', xb, wx.astype(jnp.float32)).reshape(B, L, D_RNN)
                       + bx.astype(jnp.float32))
    log_a = -RG_C * r * jax.nn.softplus(-lam.astype(jnp.float32))
    a = jnp.exp(log_a)
    b = jnp.sqrt(jnp.maximum(-jnp.expm1(2.0 * log_a), 0.0)) * (i * xf)
    first = L - 1 if reverse else 0
    b = b.at[:, first].add(a[:, first] * h0.astype(jnp.float32))
    _, h = lax.associative_scan(lin_combine, (a, b), reverse=reverse, axis=1)
    return h


def moe(h, router_w, router_b, w_gate, w_up, w_down):
    B, L, _ = h.shape
    scores = jax.nn.sigmoid((h @ router_w).astype(jnp.float32))
    sel = scores + router_b.astype(jnp.float32)
    group_score = lax.top_k(sel.reshape(B, L, N_GROUPS, EXPERTS_PER_GROUP), TOP_K)[0].sum(-1)
    best = jnp.argmax(group_score, axis=-1)
    in_group = jnp.repeat(best[..., None] == jnp.arange(N_GROUPS), EXPERTS_PER_GROUP, axis=-1)
    _, idx = lax.top_k(jnp.where(in_group, sel, -jnp.inf), TOP_K)
    w = jnp.take_along_axis(scores, idx, axis=-1)
    w = w / jnp.sum(w, axis=-1, keepdims=True)
    gates = jnp.sum(jax.nn.one_hot(idx, N_EXPERTS, dtype=jnp.float32) * w[..., None], axis=-2)
    hid = jnp.einsum('bld,edf->blef', h, w_gate)
    up = jnp.einsum('bld,edf->blef', h, w_up)
    act = jax.nn.silu(hid) * up * gates[..., None].astype(h.dtype)
    return jnp.einsum('blef,efd->bld', act, w_down)


def setup_inputs(seed: int = 0) -> dict:
    key = jax.random.key(seed)
    ks = iter(jax.random.split(key, 48))

    def nrm(shape, s):
        return jax.random.normal(next(ks), shape, jnp.float32) * s

    D = D_MODEL
    x_prompt = nrm((BATCH, SEQ, D), 1.0)
    x_sample = nrm((DEC_BATCH, DEC_SEQ, D), 1.0)
    cache_k = nrm((DEC_BATCH, N_EVEN, NA_HEADS, PAST_LEN, HEAD_DIM), 1.0)
    cache_v = nrm((DEC_BATCH, N_EVEN, NA_HEADS, PAST_LEN, HEAD_DIM), 1.0)
    state_h = nrm((DEC_BATCH, N_ODD, 2, D_RNN), 0.5)
    c = nrm((DEC_BATCH, D), 1.0)
    c_ctx = nrm((D,), 1.0)
    norm_g = 1.0 + nrm((DEPTH, 2, D), 0.1)
    ada_w = nrm((DEPTH, D, 6 * D), 0.5 * D ** -0.5)
    ada_b = nrm((DEPTH, 6 * D), 0.02)
    final_g = 1.0 + nrm((D,), 0.1)
    a_in_w = nrm((N_EVEN, D, 3 * D_A + 3 * D_B), D ** -0.5)
    a_out_w = nrm((N_EVEN, D_A + D_B, D), (D_A + D_B) ** -0.5)
    na_rpb = nrm((N_EVEN, NA_HEADS, 2 * WIN_ROWS_MAX - 1, 2 * WIN_COLS - 1), 0.1)
    hy_short_w = nrm((N_EVEN, HY_SHORT, 3 * D_B), HY_SHORT ** -0.5)
    hy_short_b = nrm((N_EVEN, 3 * D_B), 0.02)
    hy_w1 = nrm((N_EVEN, HY_EMB, HY_FFN), HY_EMB ** -0.5)
    hy_b1 = nrm((N_EVEN, HY_FFN), 0.1)
    hy_w2 = nrm((N_EVEN, HY_FFN, HY_FFN), HY_FFN ** -0.5)
    hy_b2 = nrm((N_EVEN, HY_FFN), 0.1)
    hy_w3 = nrm((N_EVEN, HY_FFN, 2 * HY_ORDER * D_B), 0.1 * HY_FFN ** -0.5)
    hy_freq = 1.0 + nrm((N_EVEN, 2, HY_FFN), 0.1)
    hy_d = nrm((N_EVEN, HY_ORDER, D_B), 0.1)
    c_in_w = nrm((N_ODD, D, 2 * D_RNN), D ** -0.5)
    c_out_w = nrm((N_ODD, D_RNN, D), D_RNN ** -0.5)
    rg_conv_w = nrm((N_ODD, RG_CONV, D_RNN), RG_CONV ** -0.5)
    rg_conv_b = nrm((N_ODD, D_RNN), 0.02)
    rg_wa = nrm((N_ODD, 2, RG_BLOCKS, RG_BLOCK, RG_BLOCK), RG_BLOCK ** -0.5)
    rg_ba = nrm((N_ODD, 2, D_RNN), 0.1)
    rg_wx = nrm((N_ODD, 2, RG_BLOCKS, RG_BLOCK, RG_BLOCK), RG_BLOCK ** -0.5)
    rg_bx = nrm((N_ODD, 2, D_RNN), 0.1)
    a0 = jax.random.uniform(next(ks), (N_ODD, 2, D_RNN), jnp.float32,
                            minval=0.9 ** (1.0 / RG_C), maxval=0.999 ** (1.0 / RG_C))
    rg_lam = jnp.log(a0) - jnp.log1p(-a0)
    router_w = nrm((D, N_EXPERTS), D ** -0.5)
    router_b = nrm((N_EXPERTS,), 0.01)
    moe_w_gate = nrm((DEPTH, N_EXPERTS, D, D_EXPERT), D ** -0.5)
    moe_w_up = nrm((DEPTH, N_EXPERTS, D, D_EXPERT), D ** -0.5)
    moe_w_down = nrm((DEPTH, N_EXPERTS, D_EXPERT, D), D_EXPERT ** -0.5)
    return {'x_prompt': x_prompt, 'x_sample': x_sample, 'cache_k': cache_k, 'cache_v': cache_v,
            'state_h': state_h, 'c': c, 'c_ctx': c_ctx, 'norm_g': norm_g, 'ada_w': ada_w,
            'ada_b': ada_b, 'final_g': final_g, 'a_in_w': a_in_w, 'a_out_w': a_out_w,
            'na_rpb': na_rpb, 'hy_short_w': hy_short_w, 'hy_short_b': hy_short_b,
            'hy_w1': hy_w1, 'hy_b1': hy_b1, 'hy_w2': hy_w2, 'hy_b2': hy_b2, 'hy_w3': hy_w3,
            'hy_freq': hy_freq, 'hy_d': hy_d, 'c_in_w': c_in_w, 'c_out_w': c_out_w,
            'rg_conv_w': rg_conv_w, 'rg_conv_b': rg_conv_b, 'rg_wa': rg_wa, 'rg_ba': rg_ba,
            'rg_wx': rg_wx, 'rg_bx': rg_bx, 'rg_lam': rg_lam, 'router_w': router_w,
            'router_b': router_b, 'moe_w_gate': moe_w_gate, 'moe_w_up': moe_w_up,
            'moe_w_down': moe_w_down}


def reference(x_prompt, x_sample, cache_k, cache_v, state_h, c, c_ctx, norm_g, ada_w, ada_b,
              final_g, a_in_w, a_out_w, na_rpb, hy_short_w, hy_short_b, hy_w1, hy_b1, hy_w2,
              hy_b2, hy_w3, hy_freq, hy_d, c_in_w, c_out_w, rg_conv_w, rg_conv_b, rg_wa, rg_ba,
              rg_wx, rg_bx, rg_lam, router_w, router_b, moe_w_gate, moe_w_up, moe_w_down):

    def modulation(cond, l):
        m = jax.nn.silu(cond) @ ada_w[l] + ada_b[l]
        return [t[:, None, :] for t in jnp.split(m, 6, axis=-1)]

    def even_mixer(h, e, ctx_k, ctx_v):
        B, L, _ = h.shape
        u = h @ a_in_w[e]
        q, k, v, hy_u = jnp.split(u, [D_A, 2 * D_A, 3 * D_A], axis=-1)
        q = q.reshape(B, L, NA_HEADS, HEAD_DIM)
        k = k.reshape(B, L, NA_HEADS, HEAD_DIM)
        v = v.reshape(B, L, NA_HEADS, HEAD_DIM)
        if ctx_k is None:
            attn = context_attention(q, k, v)
            kv = (k.transpose(0, 2, 1, 3), v.transpose(0, 2, 1, 3))
        else:
            attn = neighbourhood_attention(q, k, v, ctx_k, ctx_v, na_rpb[e])
            kv = None
        filt = hyena_filters(L, hy_w1[e], hy_b1[e], hy_w2[e], hy_b2[e], hy_w3[e], hy_freq[e])
        hy = hyena_mixer(hy_u, filt, hy_short_w[e], hy_short_b[e], hy_d[e])
        return jnp.concatenate([attn, hy], axis=-1) @ a_out_w[e], kv

    def odd_mixer(h, o, h0):
        u = h @ c_in_w[o]
        gate_u, xr = jnp.split(u, 2, axis=-1)
        xr = dwconv_centred(xr, rg_conv_w[o], rg_conv_b[o])
        hf = rglru_scan(xr, rg_wa[o, 0], rg_ba[o, 0], rg_wx[o, 0], rg_bx[o, 0], rg_lam[o, 0], h0[:, 0], False)
        hb = rglru_scan(xr, rg_wa[o, 1], rg_ba[o, 1], rg_wx[o, 1], rg_bx[o, 1], rg_lam[o, 1], h0[:, 1], True)
        y = ((hf + hb) * jax.nn.gelu(gate_u.astype(jnp.float32))).astype(h.dtype)
        final = jnp.stack([hf[:, -1], hb[:, 0]], axis=1).astype(h.dtype)
        return y @ c_out_w[o], final

    x = x_prompt
    cond = c_ctx[None, :]
    k_list, v_list, h_list = [], [], []
    for l in range(DEPTH):
        sh1, sc1, g1, sh2, sc2, g2 = modulation(cond, l)
        h = rmsnorm(x, norm_g[l, 0]) * (1 + sc1) + sh1
        if l % 2 == 0:
            out, kv = even_mixer(h, l // 2, None, None)
            k_list.append(kv[0])
            v_list.append(kv[1])
        else:
            h0 = jnp.zeros((x.shape[0], 2, D_RNN), jnp.float32)
            out, fin = odd_mixer(h, l // 2, h0)
            h_list.append(fin)
        x = x + g1 * out
        h = rmsnorm(x, norm_g[l, 1]) * (1 + sc2) + sh2
        x = x + g2 * moe(h, router_w, router_b, moe_w_gate[l], moe_w_up[l], moe_w_down[l])
    y_prompt = rmsnorm(x, final_g)
    new_k = jnp.stack(k_list, axis=1)
    new_v = jnp.stack(v_list, axis=1)
    new_h = jnp.stack(h_list, axis=1)

    x = x_sample
    for l in range(DEPTH):
        sh1, sc1, g1, sh2, sc2, g2 = modulation(c, l)
        h = rmsnorm(x, norm_g[l, 0]) * (1 + sc1) + sh1
        if l % 2 == 0:
            out, _ = even_mixer(h, l // 2, cache_k[:, l // 2], cache_v[:, l // 2])
        else:
            out, _ = odd_mixer(h, l // 2, state_h[:, l // 2])
        x = x + g1 * out
        h = rmsnorm(x, norm_g[l, 1]) * (1 + sc2) + sh2
        x = x + g2 * moe(h, router_w, router_b, moe_w_gate[l], moe_w_up[l], moe_w_down[l])
    y_sample = rmsnorm(x, final_g)
    return (y_prompt, y_sample, new_k, new_v, new_h)
```

```python
import functools
import math

import numpy as np
import jax
import jax.numpy as jnp
from jax import lax
from jax.experimental import pallas as pl
from jax.experimental.pallas import tpu as pltpu

F32 = jnp.float32
BF16 = jnp.bfloat16

D_MODEL = 1024
DEPTH = 2
GRID_W = 64
EPS = 1e-6
NEG_INF = -1e30
NA_HEADS = 8
HEAD_DIM = 64
D_A = NA_HEADS * HEAD_DIM
WIN_ROWS = 8
WIN_COLS = 16
D_B = D_MODEL - D_A
HY_ORDER = 2
HY_EMB = 33
HY_BANDS = (HY_EMB - 1) // 2
HY_FFN = 64
HY_DECAY_TARGET = 1e-2
HY_FAST_PCT = 0.3
HY_SLOW_PCT = 1.5
D_RNN = D_MODEL
RG_BLOCK = 64
RG_C = 8.0
N_EXPERTS = 16
N_GROUPS = 4
EXPERTS_PER_GROUP = N_EXPERTS // N_GROUPS
D_EXPERT = 512

LANES = 128
VMEM_LIMIT = 56 * 1024 * 1024
N_MOD = 6
MOD_ROWS = 16


def _cparams(*sem):
    return pltpu.CompilerParams(dimension_semantics=sem, vmem_limit_bytes=VMEM_LIMIT)


def _dot(a, b):
    return jnp.dot(a, b, preferred_element_type=F32)


def _dot_nt(a, b):
    return lax.dot_general(a, b, (((1,), (1,)), ((), ())), preferred_element_type=F32)


def _normmod(x, g, sc, sh):
    ms = jnp.mean(x * x, axis=-1, keepdims=True)
    return (x * lax.rsqrt(ms + EPS) * g) * (1.0 + sc) + sh


def _mod_spec(chunk, tm, seq_len, per_seq):
    if per_seq:
        return pl.BlockSpec((1, 1, D_MODEL), lambda i, *_: ((1 + (i * tm) // seq_len) * N_MOD + chunk, 0, 0))
    return pl.BlockSpec((1, 1, D_MODEL), lambda i, *_: (chunk, 0, 0))


def _mod_kernel(c_ref, w_ref, b_ref, o_ref):
    s = c_ref[...]
    s = s * jax.nn.sigmoid(s)
    o_ref[0] = _dot(s.astype(BF16), w_ref[0].astype(BF16)) + b_ref[0]


def modulation(cond, ada_w, ada_b):
    tn = 1536
    n = ada_w.shape[-1]
    return pl.pallas_call(
        _mod_kernel,
        grid=(DEPTH, n // tn),
        in_specs=[pl.BlockSpec((MOD_ROWS, D_MODEL), lambda l, j: (0, 0)),
                  pl.BlockSpec((1, D_MODEL, tn), lambda l, j: (l, 0, j)),
                  pl.BlockSpec((1, 1, tn), lambda l, j: (l, 0, j))],
        out_specs=pl.BlockSpec((1, MOD_ROWS, tn), lambda l, j: (l, 0, j)),
        out_shape=jax.ShapeDtypeStruct((DEPTH, MOD_ROWS, n), F32),
        compiler_params=_cparams("arbitrary", "arbitrary"),
        name="modulation",
    )(cond, ada_w, ada_b.reshape(DEPTH, 1, n))


def _nm_matmul_kernel(x_ref, g_ref, sc_ref, sh_ref, w_ref, o_ref):
    h = _normmod(x_ref[...], g_ref[...], sc_ref[0], sh_ref[0])
    o_ref[...] = _dot(h.astype(BF16), w_ref[...])


def nm_matmul(x, g, mod, w, seq_len, per_seq, tm=256):
    t, d = x.shape
    n = w.shape[1]
    return pl.pallas_call(
        _nm_matmul_kernel,
        grid=(t // tm,),
        in_specs=[pl.BlockSpec((tm, d), lambda i: (i, 0)),
                  pl.BlockSpec((1, d), lambda i: (0, 0)),
                  _mod_spec(1, tm, seq_len, per_seq),
                  _mod_spec(0, tm, seq_len, per_seq),
                  pl.BlockSpec((d, n), lambda i: (0, 0))],
        out_specs=pl.BlockSpec((tm, n), lambda i: (i, 0)),
        out_shape=jax.ShapeDtypeStruct((t, n), F32),
        compiler_params=_cparams("arbitrary"),
        name="norm_mod_proj",
    )(x, g.reshape(1, d), mod, mod, w)


def _proj_res_kernel(n_act, *refs):
    acts = refs[:n_act]
    ws = refs[n_act:2 * n_act]
    x_ref, g_ref, o_ref = refs[2 * n_act:]
    acc = _dot(acts[0][...].astype(BF16), ws[0][...])
    for a, w in zip(acts[1:], ws[1:]):
        acc += _dot(a[...].astype(BF16), w[...])
    o_ref[...] = x_ref[...] + g_ref[0] * acc


def proj_residual(acts, ws, x, mod, seq_len, per_seq, tm=512):
    t, d = x.shape
    in_specs = [pl.BlockSpec((tm, a.shape[1]), lambda i: (i, 0)) for a in acts]
    in_specs += [pl.BlockSpec(w.shape, lambda i: (0, 0)) for w in ws]
    in_specs += [pl.BlockSpec((tm, d), lambda i: (i, 0)), _mod_spec(2, tm, seq_len, per_seq)]
    return pl.pallas_call(
        functools.partial(_proj_res_kernel, len(acts)),
        grid=(t // tm,),
        in_specs=in_specs,
        out_specs=pl.BlockSpec((tm, d), lambda i: (i, 0)),
        out_shape=jax.ShapeDtypeStruct((t, d), F32),
        compiler_params=_cparams("arbitrary"),
        name="proj_residual",
    )(*acts, *ws, x, mod)


def _ctx_attn_kernel(q_ref, k_ref, v_ref, o_ref, nk_ref, nv_ref):
    scale = HEAD_DIM ** -0.5
    for h in range(NA_HEADS):
        sl = slice(h * HEAD_DIM, (h + 1) * HEAD_DIM)
        q = q_ref[:, sl]
        k = k_ref[:, sl]
        v = v_ref[:, sl]
        nk_ref[0, h] = k
        nv_ref[0, h] = v
        s = _dot_nt(q.astype(BF16), k.astype(BF16)) * scale
        m = jnp.max(s, axis=-1, keepdims=True)
        p = jnp.exp(s - m)
        den = jnp.sum(p, axis=-1, keepdims=True)
        o_ref[:, sl] = _dot(p.astype(BF16), v.astype(BF16)) / den


def context_attention(u, nseq, seq_len):
    t = u.shape[0]
    kv_shape = jax.ShapeDtypeStruct((nseq, NA_HEADS, seq_len, HEAD_DIM), F32)
    kv_spec = pl.BlockSpec((1, NA_HEADS, seq_len, HEAD_DIM), lambda b: (b, 0, 0, 0))
    return pl.pallas_call(
        _ctx_attn_kernel,
        grid=(nseq,),
        in_specs=[pl.BlockSpec((seq_len, D_A), lambda b: (b, 0)),
                  pl.BlockSpec((seq_len, D_A), lambda b: (b, 1)),
                  pl.BlockSpec((seq_len, D_A), lambda b: (b, 2))],
        out_specs=[pl.BlockSpec((seq_len, D_A), lambda b: (b, 0)), kv_spec, kv_spec],
        out_shape=[jax.ShapeDtypeStruct((t, D_A), F32), kv_shape, kv_shape],
        compiler_params=_cparams("arbitrary"),
        name="context_attention",
    )(u, u, u)


def _na_bias_table(rpb):
    cols = np.arange(GRID_W)
    col_start = np.clip(cols - WIN_COLS // 2, 0, GRID_W - WIN_COLS)
    col_in = (cols[None, :] >= col_start[:, None]) & (cols[None, :] < col_start[:, None] + WIN_COLS)
    dc = np.clip(cols[None, :] - cols[:, None], 1 - WIN_COLS, WIN_COLS - 1) + WIN_COLS - 1
    dr = np.arange(WIN_ROWS)[:, None] + np.arange(WIN_ROWS)[None, :]
    tbl = rpb[:, dr[:, None, :, None], dc[None, :, None, :]]
    tbl = jnp.where(col_in[None, None, :, None, :], tbl.astype(F32), NEG_INF)
    return tbl.reshape(NA_HEADS, WIN_ROWS, GRID_W, WIN_ROWS * GRID_W)


def _na_kernel(rows, q_ref, k_ref, v_ref, ck_ref, cv_ref, bias_ref, o_ref):
    scale = HEAD_DIM ** -0.5
    n_lat = WIN_ROWS * GRID_W
    for h in range(NA_HEADS):
        sl = slice(h * HEAD_DIM, (h + 1) * HEAD_DIM)
        ck = ck_ref[0, h].astype(BF16)
        cv = cv_ref[0, h].astype(BF16)

        def row_body(r, carry):
            start = jnp.clip(r - WIN_ROWS // 2, 0, rows - WIN_ROWS)
            off = start - r + WIN_ROWS - 1
            q = q_ref[pl.ds(pl.multiple_of(r * GRID_W, GRID_W), GRID_W), sl].astype(BF16)
            kw = k_ref[pl.ds(pl.multiple_of(start * GRID_W, GRID_W), n_lat), sl].astype(BF16)
            vw = v_ref[pl.ds(pl.multiple_of(start * GRID_W, GRID_W), n_lat), sl].astype(BF16)
            s_lat = _dot_nt(q, kw) * scale + bias_ref[h, off]
            s_ctx = _dot_nt(q, ck) * scale
            m = jnp.maximum(jnp.max(s_lat, axis=-1, keepdims=True), jnp.max(s_ctx, axis=-1, keepdims=True))
            p_lat = jnp.exp(s_lat - m)
            p_ctx = jnp.exp(s_ctx - m)
            den = jnp.sum(p_lat, axis=-1, keepdims=True) + jnp.sum(p_ctx, axis=-1, keepdims=True)
            o = _dot(p_lat.astype(BF16), vw) + _dot(p_ctx.astype(BF16), cv)
            o_ref[pl.ds(pl.multiple_of(r * GRID_W, GRID_W), GRID_W), sl] = o / den
            return carry

        lax.fori_loop(0, rows, row_body, 0)


def neighbourhood_attention(u, ctx_k, ctx_v, rpb, nseq, seq_len):
    t = u.shape[0]
    rows = seq_len // GRID_W
    assert rows >= WIN_ROWS
    past = ctx_k.shape[2]
    bias = _na_bias_table(rpb)
    ctx_spec = pl.BlockSpec((1, NA_HEADS, past, HEAD_DIM), lambda b: (b, 0, 0, 0))
    return pl.pallas_call(
        functools.partial(_na_kernel, rows),
        grid=(nseq,),
        in_specs=[pl.BlockSpec((seq_len, D_A), lambda b: (b, 0)),
                  pl.BlockSpec((seq_len, D_A), lambda b: (b, 1)),
                  pl.BlockSpec((seq_len, D_A), lambda b: (b, 2)),
                  ctx_spec, ctx_spec,
                  pl.BlockSpec(bias.shape, lambda b: (0, 0, 0, 0))],
        out_specs=pl.BlockSpec((seq_len, D_A), lambda b: (b, 0)),
        out_shape=jax.ShapeDtypeStruct((t, D_A), F32),
        compiler_params=_cparams("arbitrary"),
        name="neighbourhood_attention",
    )(u, u, u, ctx_k, ctx_v, bias)


def _dft_tables(seq_len):
    n = 2 * seq_len
    f = np.arange(seq_len, dtype=np.int64)
    ang = (np.outer(f, f) % n).astype(np.float64) * (math.pi / seq_len)
    cos, sin = np.cos(ang), np.sin(ang)
    alt = np.where(f % 2 == 0, 1.0, -1.0)
    s_fwd = -sin
    s_fwd[0, :] = alt
    fwd = np.concatenate([cos, s_fwd], axis=0)
    wf = np.where(f == 0, 1.0, 2.0) / n
    ci = cos.T * wf[None, :]
    si = -sin.T * wf[None, :]
    si[:, 0] = alt / n
    inv = np.concatenate([ci, si], axis=1)
    return fwd.astype(np.float32), inv.astype(np.float32)


def _hyena_feats(seq_len):
    t = np.linspace(0.0, 1.0, seq_len, dtype=np.float32)[:, None]
    w = (2.0 * math.pi * np.arange(seq_len, dtype=np.float32)[:, None] / seq_len).astype(np.float32)
    f = np.linspace(1e-4, HY_BANDS - 1, HY_BANDS, dtype=np.float32)[None, :]
    z = np.concatenate([t, np.cos(f * w), -np.sin(f * w)], axis=-1).astype(np.float32)
    max_decay = math.log(HY_DECAY_TARGET) / HY_FAST_PCT
    min_decay = math.log(HY_DECAY_TARGET) / HY_SLOW_PCT
    deltas = np.abs(np.linspace(min_decay, max_decay, D_B, dtype=np.float32))[None, :]
    return z, t, deltas


def _hy_filter_kernel(seq_len, z_ref, t_ref, dl_ref, w1_ref, b1_ref, w2_ref, b2_ref, w3_ref, fr_ref,
                      d_ref, fwd_ref, g_ref):
    hp = lax.Precision.HIGHEST
    h = jnp.sin(fr_ref[0:1, :] * (jnp.dot(z_ref[...], w1_ref[...], precision=hp) + b1_ref[...]))
    h = jnp.sin(fr_ref[1:2, :] * (jnp.dot(h, w2_ref[...], precision=hp) + b2_ref[...]))
    h = jnp.dot(h, w3_ref[...], precision=hp)
    decay = jnp.exp(-t_ref[...] * dl_ref[...])
    row0 = lax.broadcasted_iota(jnp.int32, (seq_len, D_B), 0) == 0
    sums, diffs = [], []
    for n in range(HY_ORDER):
        hf = h[:, (2 * n) * D_B:(2 * n + 1) * D_B] * decay
        hb = h[:, (2 * n + 1) * D_B:(2 * n + 2) * D_B] * decay
        gp = jnp.where(row0, hf + hb + d_ref[n:n + 1, :], hf)
        gm = jnp.where(row0, 0.0, hb)
        sums.append(gp + gm)
        diffs.append(gp - gm)
    rhs = jnp.concatenate(sums + diffs, axis=1).astype(BF16)
    spec = _dot(fwd_ref[...], rhs)
    for n in range(HY_ORDER):
        a = spec[:, n * D_B:(n + 1) * D_B]
        b = spec[:, (HY_ORDER + n) * D_B:(HY_ORDER + n + 1) * D_B]
        g_ref[n, 0:seq_len, :] = a[0:seq_len]
        g_ref[n, seq_len:, :] = jnp.where(row0, a[seq_len:], b[seq_len:])


def hyena_spectrum(seq_len, w1, b1, w2, b2, w3, freq, d, fwd):
    z, t, deltas = _hyena_feats(seq_len)
    return pl.pallas_call(
        functools.partial(_hy_filter_kernel, seq_len),
        out_shape=jax.ShapeDtypeStruct((HY_ORDER, 2 * seq_len, D_B), F32),
        compiler_params=pltpu.CompilerParams(vmem_limit_bytes=VMEM_LIMIT),
        name="hyena_spectrum",
    )(jnp.asarray(z), jnp.asarray(t), jnp.asarray(deltas), w1, b1.reshape(1, -1), w2, b2.reshape(1, -1),
      w3, freq, d, fwd)


def _hyena_kernel(seq_len, u_ref, sw_ref, sb_ref, g_ref, fwd_ref, inv_ref, o_ref):
    u = u_ref[...]
    t_idx = lax.broadcasted_iota(jnp.int32, u.shape, 0)
    prev = jnp.where(t_idx == 0, 0.0, pltpu.roll(u, 1, axis=0))
    nxt = jnp.where(t_idx == seq_len - 1, 0.0, pltpu.roll(u, seq_len - 1, axis=0))
    u = prev * sw_ref[0:1, :] + u * sw_ref[1:2, :] + nxt * sw_ref[2:3, :] + sb_ref[...]
    row0 = lax.broadcasted_iota(jnp.int32, (seq_len, D_B), 0) == 0
    z = u[:, 0:D_B]
    for n in range(HY_ORDER):
        spec = _dot(fwd_ref[...], z.astype(BF16))
        ure, uim = spec[0:seq_len], spec[seq_len:]
        gre, gim = g_ref[n, 0:seq_len, :], g_ref[n, seq_len:, :]
        pim = uim * gim
        yre = ure * gre - jnp.where(row0, 0.0, pim)
        yim = jnp.where(row0, pim, ure * gim + uim * gre)
        y = jnp.concatenate([yre, yim], axis=0).astype(BF16)
        z = u[:, (n + 1) * D_B:(n + 2) * D_B] * _dot(inv_ref[...], y)
    o_ref[...] = z


def hyena_mixer(u, nseq, seq_len, short_w, short_b, spectrum, fwd, inv):
    t = u.shape[0]
    width = (HY_ORDER + 1) * D_B
    col_block = (3 * D_A) // width
    assert col_block * width == 3 * D_A
    return pl.pallas_call(
        functools.partial(_hyena_kernel, seq_len),
        grid=(nseq,),
        in_specs=[pl.BlockSpec((seq_len, width), lambda b: (b, col_block)),
                  pl.BlockSpec(short_w.shape, lambda b: (0, 0)),
                  pl.BlockSpec((1, width), lambda b: (0, 0)),
                  pl.BlockSpec(spectrum.shape, lambda b: (0, 0, 0)),
                  pl.BlockSpec(fwd.shape, lambda b: (0, 0)),
                  pl.BlockSpec(inv.shape, lambda b: (0, 0))],
        out_specs=pl.BlockSpec((seq_len, D_B), lambda b: (b, 0)),
        out_shape=jax.ShapeDtypeStruct((t, D_B), F32),
        compiler_params=_cparams("arbitrary"),
        name="hyena_mixer",
    )(u, short_w, short_b.reshape(1, width), spectrum, fwd, inv)


RG_CB = LANES
RG_PAD = 8
RG_CHUNK = 512


def _rglru_kernel(nseq, seq_len, gate_ref, xr_ref, cw_ref, cb_ref, wg_ref, bg_ref, lam_ref, h0_ref,
                  y_ref, fin_ref, xp_ref, a_f, b_f, a_b, b_b):
    t_tot = nseq * seq_len
    c = RG_CB
    xp_ref[0:RG_PAD, :] = jnp.zeros((RG_PAD, c), F32)
    xp_ref[RG_PAD + t_tot:, :] = jnp.zeros((RG_PAD, c), F32)
    xp_ref[RG_PAD:RG_PAD + t_tot, :] = xr_ref[...]
    nl = -lam_ref[...]
    sp = jnp.maximum(nl, 0.0) + jnp.log1p(jnp.exp(-jnp.abs(nl)))

    def gate_chunk(ci, carry):
        r0 = pl.multiple_of(ci * RG_CHUNK, RG_CHUNK)
        t_idx = (r0 + lax.broadcasted_iota(jnp.int32, (RG_CHUNK, c), 0)) % seq_len
        xm2 = jnp.where(t_idx >= 2, xp_ref[pl.ds(r0 + RG_PAD - 2, RG_CHUNK), :], 0.0)
        xm1 = jnp.where(t_idx >= 1, xp_ref[pl.ds(r0 + RG_PAD - 1, RG_CHUNK), :], 0.0)
        x0 = xp_ref[pl.ds(r0 + RG_PAD, RG_CHUNK), :]
        xp1 = jnp.where(t_idx <= seq_len - 2, xp_ref[pl.ds(r0 + RG_PAD + 1, RG_CHUNK), :], 0.0)
        xc = (xm2 * cw_ref[0:1, :] + xm1 * cw_ref[1:2, :] + x0 * cw_ref[2:3, :] + xp1 * cw_ref[3:4, :]
              + cb_ref[...])
        gts = _dot(xc.astype(BF16), wg_ref[0]) + bg_ref[...]
        for d, (a_ref, b_ref) in enumerate(((a_f, b_f), (a_b, b_b))):
            r = jax.nn.sigmoid(gts[:, (2 * d) * c:(2 * d + 1) * c])
            i = jax.nn.sigmoid(gts[:, (2 * d + 1) * c:(2 * d + 2) * c])
            log_a = -RG_C * r * sp[d:d + 1, :]
            a_ref[pl.ds(r0, RG_CHUNK), :] = jnp.exp(log_a)
            th = jnp.tanh(log_a)
            one_minus_a2 = -2.0 * th / (1.0 - th)
            b_ref[pl.ds(r0, RG_CHUNK), :] = jnp.sqrt(jnp.maximum(one_minus_a2, 0.0)) * (i * xc)
        return carry

    lax.fori_loop(0, t_tot // RG_CHUNK, gate_chunk, 0)

    def scan_step(t, carry):
        hf, hb = carry
        rows_f = pl.ds(t, nseq, stride=seq_len)
        rows_b = pl.ds(seq_len - 1 - t, nseq, stride=seq_len)
        hf = a_f[rows_f, :] * hf + b_f[rows_f, :]
        hb = a_b[rows_b, :] * hb + b_b[rows_b, :]
        b_f[rows_f, :] = hf
        b_b[rows_b, :] = hb
        return hf, hb

    hf, hb = lax.fori_loop(0, seq_len, scan_step, (h0_ref[0], h0_ref[1]))
    fin_ref[0] = hf
    fin_ref[1] = hb

    def out_chunk(ci, carry):
        rs = pl.ds(pl.multiple_of(ci * RG_CHUNK, RG_CHUNK), RG_CHUNK)
        y_ref[rs, :] = (b_f[rs, :] + b_b[rs, :]) * jax.nn.gelu(gate_ref[rs, :])
        return carry

    lax.fori_loop(0, t_tot // RG_CHUNK, out_chunk, 0)


def _rg_gate_weights(wa, wx):
    per_step = RG_CB // RG_BLOCK
    steps = D_RNN // RG_CB
    mats = []
    for d in range(2):
        for w in (wa[d], wx[d]):
            w = w.reshape(steps, per_step, RG_BLOCK, RG_BLOCK)
            eye = jnp.eye(per_step, dtype=w.dtype)
            m = jnp.einsum('spde,pq->spdqe', w, eye).reshape(steps, RG_CB, RG_CB)
            mats.append(m)
    return jnp.concatenate(mats, axis=-1).astype(BF16)


def rglru_block(u, nseq, seq_len, conv_w, conv_b, wa, ba, wx, bx, lam, h0):
    t = u.shape[0]
    c = RG_CB
    steps = D_RNN // c
    wg = _rg_gate_weights(wa, wx)
    bg = jnp.stack([ba[0], bx[0], ba[1], bx[1]], axis=0).reshape(4, steps, c)
    bg = bg.transpose(1, 0, 2).reshape(steps, 1, 4 * c)
    y, fin = pl.pallas_call(
        functools.partial(_rglru_kernel, nseq, seq_len),
        grid=(steps,),
        in_specs=[pl.BlockSpec((t, c), lambda j: (0, j)),
                  pl.BlockSpec((t, c), lambda j: (0, steps + j)),
                  pl.BlockSpec((conv_w.shape[0], c), lambda j: (0, j)),
                  pl.BlockSpec((1, c), lambda j: (0, j)),
                  pl.BlockSpec((1, c, 4 * c), lambda j: (j, 0, 0)),
                  pl.BlockSpec((None, 1, 4 * c), lambda j: (j, 0, 0)),
                  pl.BlockSpec((2, c), lambda j: (0, j)),
                  pl.BlockSpec((2, nseq, c), lambda j: (0, 0, j))],
        out_specs=[pl.BlockSpec((t, c), lambda j: (0, j)),
                   pl.BlockSpec((2, nseq, c), lambda j: (0, 0, j))],
        out_shape=[jax.ShapeDtypeStruct((t, D_RNN), F32),
                   jax.ShapeDtypeStruct((2, nseq, D_RNN), F32)],
        scratch_shapes=[pltpu.VMEM((t + 2 * RG_PAD, c), F32)] + [pltpu.VMEM((t, c), F32)] * 4,
        compiler_params=_cparams("arbitrary"),
        name="rglru_block",
    )(u, u, conv_w, conv_b.reshape(1, -1), wg, bg, lam, h0)
    return y, fin


def _router_kernel(x_ref, g_ref, sc_ref, sh_ref, rw_ref, rb_ref, o_ref):
    h = _normmod(x_ref[...], g_ref[...], sc_ref[0], sh_ref[0])
    logits = lax.dot_general(rw_ref[...], h, (((1,), (1,)), ((), ())),
                             precision=lax.Precision.HIGHEST, preferred_element_type=F32)
    scores = jax.nn.sigmoid(logits)
    sel = scores + rb_ref[...]
    row = [sel[e:e + 1, :] for e in range(N_EXPERTS)]
    gs = []
    for g in range(N_GROUPS):
        r = row[g * EXPERTS_PER_GROUP:(g + 1) * EXPERTS_PER_GROUP]
        best_pair = None
        for i in range(EXPERTS_PER_GROUP):
            for j in range(i + 1, EXPERTS_PER_GROUP):
                s = r[i] + r[j]
                best_pair = s if best_pair is None else jnp.maximum(best_pair, s)
        gs.append(best_pair)
    best = jnp.zeros_like(gs[0], dtype=jnp.int32)
    top = gs[0]
    for g in range(1, N_GROUPS):
        better = gs[g] > top
        best = jnp.where(better, g, best)
        top = jnp.where(better, gs[g], top)
    picked = []
    for e in range(N_EXPERTS):
        g = e // EXPERTS_PER_GROUP
        rank = jnp.zeros_like(best)
        for o in range(g * EXPERTS_PER_GROUP, (g + 1) * EXPERTS_PER_GROUP):
            if o == e:
                continue
            ahead = (row[o] > row[e]) | ((row[o] == row[e]) & (o < e))
            rank = rank + ahead.astype(jnp.int32)
        picked.append((best == g) & (rank < 2))
    den = jnp.zeros_like(gs[0])
    for e in range(N_EXPERTS):
        den = den + jnp.where(picked[e], scores[e:e + 1, :], 0.0)
    for e in range(N_EXPERTS):
        o_ref[e:e + 1, :] = jnp.where(picked[e], scores[e:e + 1, :] / den, 0.0)


def router_gates(x, g, mod, router_w, router_b, seq_len, per_seq, tm=512):
    t, d = x.shape
    return pl.pallas_call(
        _router_kernel,
        grid=(t // tm,),
        in_specs=[pl.BlockSpec((tm, d), lambda i: (i, 0)),
                  pl.BlockSpec((1, d), lambda i: (0, 0)),
                  _mod_spec(4, tm, seq_len, per_seq),
                  _mod_spec(3, tm, seq_len, per_seq),
                  pl.BlockSpec((N_EXPERTS, d), lambda i: (0, 0)),
                  pl.BlockSpec((N_EXPERTS, 1), lambda i: (0, 0))],
        out_specs=pl.BlockSpec((N_EXPERTS, tm), lambda i: (0, i)),
        out_shape=jax.ShapeDtypeStruct((N_EXPERTS, t), F32),
        compiler_params=_cparams("arbitrary"),
        name="moe_router",
    )(x, g.reshape(1, d), mod, mod, router_w.T, router_b.reshape(N_EXPERTS, 1))


def _moe_kernel(final, x_ref, g_ref, sc_ref, sh_ref, g2_ref, gates_ref, wg_ref, wu_ref, wd_ref, fg_ref,
                o_ref, h_scr, acc):
    e = pl.program_id(1)

    @pl.when(e == 0)
    def _():
        h_scr[...] = _normmod(x_ref[...], g_ref[...], sc_ref[0], sh_ref[0]).astype(BF16)
        acc[...] = jnp.zeros_like(acc)

    h = h_scr[...]
    hid = _dot(h, wg_ref[0].astype(BF16))
    up = _dot(h, wu_ref[0].astype(BF16))
    lane = lax.broadcasted_iota(jnp.int32, gates_ref.shape, 1)
    gate = jnp.sum(jnp.where(lane == e, gates_ref[...], 0.0), axis=1, keepdims=True)
    act = (hid * jax.nn.sigmoid(hid)) * up * gate
    acc[...] += _dot(act.astype(BF16), wd_ref[0].astype(BF16))

    @pl.when(e == N_EXPERTS - 1)
    def _():
        y = x_ref[...] + g2_ref[0] * acc[...]
        if final:
            ms = jnp.mean(y * y, axis=-1, keepdims=True)
            y = y * lax.rsqrt(ms + EPS) * fg_ref[...]
        o_ref[...] = y


def moe_layer(x, g, mod, gates, w_gate, w_up, w_down, final_g, final, seq_len, per_seq, tm=512):
    t, d = x.shape
    return pl.pallas_call(
        functools.partial(_moe_kernel, final),
        grid=(t // tm, N_EXPERTS),
        in_specs=[pl.BlockSpec((tm, d), lambda i, e: (i, 0)),
                  pl.BlockSpec((1, d), lambda i, e: (0, 0)),
                  _mod_spec(4, tm, seq_len, per_seq),
                  _mod_spec(3, tm, seq_len, per_seq),
                  _mod_spec(5, tm, seq_len, per_seq),
                  pl.BlockSpec((tm, N_EXPERTS), lambda i, e: (i, 0)),
                  pl.BlockSpec((1, d, D_EXPERT), lambda i, e: (e, 0, 0)),
                  pl.BlockSpec((1, d, D_EXPERT), lambda i, e: (e, 0, 0)),
                  pl.BlockSpec((1, D_EXPERT, d), lambda i, e: (e, 0, 0)),
                  pl.BlockSpec((1, d), lambda i, e: (0, 0))],
        out_specs=pl.BlockSpec((tm, d), lambda i, e: (i, 0)),
        out_shape=jax.ShapeDtypeStruct((t, d), F32),
        scratch_shapes=[pltpu.VMEM((tm, d), BF16), pltpu.VMEM((tm, d), F32)],
        compiler_params=_cparams("arbitrary", "arbitrary"),
        name="moe_experts",
    )(x, g.reshape(1, d), mod, mod, mod, gates, w_gate, w_up, w_down, final_g.reshape(1, d))


def _trunk(x, per_seq, nseq, seq_len, mods, ctx_k, ctx_v, h0, p, hy_tables):
    new_kv = None
    new_h = None
    for l in range(DEPTH):
        mod = mods[l]
        if l % 2 == 0:
            e = l // 2
            u = nm_matmul(x, p['norm_g'][l, 0], mod, p['a_in_w'][e].astype(BF16), seq_len, per_seq)
            if ctx_k is None:
                attn, nk, nv = context_attention(u, nseq, seq_len)
                new_kv = (nk, nv)
            else:
                attn = neighbourhood_attention(u, ctx_k[:, e], ctx_v[:, e], p['na_rpb'][e], nseq, seq_len)
            fwd, inv = hy_tables
            spectrum = hyena_spectrum(seq_len, p['hy_w1'][e], p['hy_b1'][e], p['hy_w2'][e], p['hy_b2'][e],
                                      p['hy_w3'][e], p['hy_freq'][e], p['hy_d'][e], fwd)
            hy = hyena_mixer(u, nseq, seq_len, p['hy_short_w'][e], p['hy_short_b'][e], spectrum, fwd, inv)
            w_out = p['a_out_w'][e].astype(BF16)
            x = proj_residual([attn, hy], [w_out[:D_A], w_out[D_A:]], x, mod, seq_len, per_seq)
        else:
            o = l // 2
            u = nm_matmul(x, p['norm_g'][l, 0], mod, p['c_in_w'][o].astype(BF16), seq_len, per_seq)
            y, fin = rglru_block(u, nseq, seq_len, p['rg_conv_w'][o], p['rg_conv_b'][o], p['rg_wa'][o],
                                 p['rg_ba'][o], p['rg_wx'][o], p['rg_bx'][o], p['rg_lam'][o], h0)
            new_h = fin
            x = proj_residual([y], [p['c_out_w'][o].astype(BF16)], x, mod, seq_len, per_seq)
        gates = router_gates(x, p['norm_g'][l, 1], mod, p['router_w'], p['router_b'], seq_len, per_seq)
        x = moe_layer(x, p['norm_g'][l, 1], mod, gates.T, p['moe_w_gate'][l], p['moe_w_up'][l],
                      p['moe_w_down'][l], p['final_g'], l == DEPTH - 1, seq_len, per_seq)
    return x, new_kv, new_h


def kernel(x_prompt, x_sample, cache_k, cache_v, state_h, c, c_ctx, norm_g, ada_w, ada_b, final_g, a_in_w, a_out_w, na_rpb, hy_short_w, hy_short_b, hy_w1, hy_b1, hy_w2, hy_b2, hy_w3, hy_freq, hy_d, c_in_w, c_out_w, rg_conv_w, rg_conv_b, rg_wa, rg_ba, rg_wx, rg_bx, rg_lam, router_w, router_b, moe_w_gate, moe_w_up, moe_w_down):
    p = dict(norm_g=norm_g, final_g=final_g, a_in_w=a_in_w, a_out_w=a_out_w, na_rpb=na_rpb,
             hy_short_w=hy_short_w, hy_short_b=hy_short_b, hy_w1=hy_w1, hy_b1=hy_b1, hy_w2=hy_w2,
             hy_b2=hy_b2, hy_w3=hy_w3, hy_freq=hy_freq, hy_d=hy_d, c_in_w=c_in_w, c_out_w=c_out_w,
             rg_conv_w=rg_conv_w, rg_conv_b=rg_conv_b, rg_wa=rg_wa, rg_ba=rg_ba, rg_wx=rg_wx, rg_bx=rg_bx,
             rg_lam=rg_lam, router_w=router_w, router_b=router_b, moe_w_gate=moe_w_gate,
             moe_w_up=moe_w_up, moe_w_down=moe_w_down)
    batch, seq, d = x_prompt.shape
    dec_batch, dec_seq, _ = x_sample.shape
    assert 1 + dec_batch <= MOD_ROWS

    cond = jnp.concatenate([c_ctx[None, :], c, jnp.zeros((MOD_ROWS - 1 - dec_batch, d), F32)], axis=0)
    m = modulation(cond, ada_w, ada_b)
    mods = [m[l].reshape(MOD_ROWS * N_MOD, 1, d) for l in range(DEPTH)]

    tables = {}
    for sl in (seq, dec_seq):
        fwd, inv = _dft_tables(sl)
        tables[sl] = (jnp.asarray(fwd).astype(BF16), jnp.asarray(inv).astype(BF16))

    h0 = jnp.zeros((2, batch, D_RNN), F32)
    y_p, (new_k, new_v), fin = _trunk(x_prompt.reshape(batch * seq, d), False, batch, seq, mods,
                                      None, None, h0, p, tables[seq])
    h0 = state_h[:, 0].transpose(1, 0, 2) if DEPTH // 2 == 1 else None
    y_s, _, _ = _trunk(x_sample.reshape(dec_batch * dec_seq, d), True, dec_batch, dec_seq, mods,
                       cache_k, cache_v, h0, p, tables[dec_seq])
    new_h = fin.transpose(1, 0, 2)[:, None]
    return (y_p.reshape(batch, seq, d), y_s.reshape(dec_batch, dec_seq, d),
            new_k[:, None], new_v[:, None], new_h)
```

```python
import functools
import math

import numpy as np
import jax
import jax.numpy as jnp
from jax import lax
from jax.experimental import pallas as pl
from jax.experimental.pallas import tpu as pltpu

F32 = jnp.float32
BF16 = jnp.bfloat16

D_MODEL = 1024
DEPTH = 2
GRID_W = 64
EPS = 1e-6
NEG_INF = -1e30
NA_HEADS = 8
HEAD_DIM = 64
D_A = NA_HEADS * HEAD_DIM
WIN_ROWS = 8
WIN_COLS = 16
D_B = D_MODEL - D_A
HY_ORDER = 2
HY_EMB = 33
HY_BANDS = (HY_EMB - 1) // 2
HY_FFN = 64
HY_DECAY_TARGET = 1e-2
HY_FAST_PCT = 0.3
HY_SLOW_PCT = 1.5
D_RNN = D_MODEL
RG_BLOCK = 64
RG_C = 8.0
N_EXPERTS = 16
N_GROUPS = 4
EXPERTS_PER_GROUP = N_EXPERTS // N_GROUPS
D_EXPERT = 512

LANES = 128
VMEM_LIMIT = 56 * 1024 * 1024
N_MOD = 6
MOD_ROWS = 16


def _cparams(*sem):
    return pltpu.CompilerParams(dimension_semantics=sem, vmem_limit_bytes=VMEM_LIMIT)


def _dot(a, b):
    return jnp.dot(a, b, preferred_element_type=F32)


def _dot_nt(a, b):
    return lax.dot_general(a, b, (((1,), (1,)), ((), ())), preferred_element_type=F32)


def _sigmoid(x):
    return 0.5 * jnp.tanh(0.5 * x) + 0.5


def _normmod(x, g, sc, sh):
    ms = jnp.mean(x * x, axis=-1, keepdims=True)
    return (x * lax.rsqrt(ms + EPS) * g) * (1.0 + sc) + sh


def _mod_spec(chunk, tm, seq_len, per_seq):
    if per_seq:
        return pl.BlockSpec((1, 1, D_MODEL), lambda i, *_: ((1 + (i * tm) // seq_len) * N_MOD + chunk, 0, 0))
    return pl.BlockSpec((1, 1, D_MODEL), lambda i, *_: (chunk, 0, 0))


def _mod_kernel(c_ref, w_ref, b_ref, o_ref):
    s = c_ref[...]
    s = s * jax.nn.sigmoid(s)
    o_ref[0] = _dot(s.astype(BF16), w_ref[0].astype(BF16)) + b_ref[0]


def modulation(cond, ada_w, ada_b):
    tn = 1536
    n = ada_w.shape[-1]
    return pl.pallas_call(
        _mod_kernel,
        grid=(DEPTH, n // tn),
        in_specs=[pl.BlockSpec((MOD_ROWS, D_MODEL), lambda l, j: (0, 0)),
                  pl.BlockSpec((1, D_MODEL, tn), lambda l, j: (l, 0, j)),
                  pl.BlockSpec((1, 1, tn), lambda l, j: (l, 0, j))],
        out_specs=pl.BlockSpec((1, MOD_ROWS, tn), lambda l, j: (l, 0, j)),
        out_shape=jax.ShapeDtypeStruct((DEPTH, MOD_ROWS, n), F32),
        compiler_params=_cparams("arbitrary", "arbitrary"),
        name="modulation",
    )(cond, ada_w, ada_b.reshape(DEPTH, 1, n))


def _nm_matmul_kernel(x_ref, g_ref, sc_ref, sh_ref, w_ref, o_ref):
    h = _normmod(x_ref[...], g_ref[...], sc_ref[0], sh_ref[0])
    o_ref[...] = _dot(h.astype(BF16), w_ref[...])


def nm_matmul(x, g, mod, w, seq_len, per_seq, tm=256):
    t, d = x.shape
    n = w.shape[1]
    return pl.pallas_call(
        _nm_matmul_kernel,
        grid=(t // tm,),
        in_specs=[pl.BlockSpec((tm, d), lambda i: (i, 0)),
                  pl.BlockSpec((1, d), lambda i: (0, 0)),
                  _mod_spec(1, tm, seq_len, per_seq),
                  _mod_spec(0, tm, seq_len, per_seq),
                  pl.BlockSpec((d, n), lambda i: (0, 0))],
        out_specs=pl.BlockSpec((tm, n), lambda i: (i, 0)),
        out_shape=jax.ShapeDtypeStruct((t, n), F32),
        compiler_params=_cparams("arbitrary"),
        name="norm_mod_proj",
    )(x, g.reshape(1, d), mod, mod, w)


def _proj_res_kernel(n_act, *refs):
    acts = refs[:n_act]
    ws = refs[n_act:2 * n_act]
    x_ref, g_ref, o_ref = refs[2 * n_act:]
    acc = _dot(acts[0][...].astype(BF16), ws[0][...])
    for a, w in zip(acts[1:], ws[1:]):
        acc += _dot(a[...].astype(BF16), w[...])
    o_ref[...] = x_ref[...] + g_ref[0] * acc


def proj_residual(acts, ws, x, mod, seq_len, per_seq, tm=512):
    t, d = x.shape
    in_specs = [pl.BlockSpec((tm, a.shape[1]), lambda i: (i, 0)) for a in acts]
    in_specs += [pl.BlockSpec(w.shape, lambda i: (0, 0)) for w in ws]
    in_specs += [pl.BlockSpec((tm, d), lambda i: (i, 0)), _mod_spec(2, tm, seq_len, per_seq)]
    return pl.pallas_call(
        functools.partial(_proj_res_kernel, len(acts)),
        grid=(t // tm,),
        in_specs=in_specs,
        out_specs=pl.BlockSpec((tm, d), lambda i: (i, 0)),
        out_shape=jax.ShapeDtypeStruct((t, d), F32),
        compiler_params=_cparams("arbitrary"),
        name="proj_residual",
    )(*acts, *ws, x, mod)


def _ctx_attn_kernel(q_ref, k_ref, v_ref, o_ref, nk_ref, nv_ref):
    scale = HEAD_DIM ** -0.5
    for h in range(NA_HEADS):
        sl = slice(h * HEAD_DIM, (h + 1) * HEAD_DIM)
        q = q_ref[:, sl]
        k = k_ref[:, sl]
        v = v_ref[:, sl]
        nk_ref[0, h] = k
        nv_ref[0, h] = v
        s = _dot_nt(q.astype(BF16), k.astype(BF16)) * scale
        m = jnp.max(s, axis=-1, keepdims=True)
        p = jnp.exp(s - m)
        den = jnp.sum(p, axis=-1, keepdims=True)
        o_ref[:, sl] = _dot(p.astype(BF16), v.astype(BF16)) / den


def context_attention(u, nseq, seq_len):
    t = u.shape[0]
    kv_shape = jax.ShapeDtypeStruct((nseq, NA_HEADS, seq_len, HEAD_DIM), F32)
    kv_spec = pl.BlockSpec((1, NA_HEADS, seq_len, HEAD_DIM), lambda b: (b, 0, 0, 0))
    return pl.pallas_call(
        _ctx_attn_kernel,
        grid=(nseq,),
        in_specs=[pl.BlockSpec((seq_len, D_A), lambda b: (b, 0)),
                  pl.BlockSpec((seq_len, D_A), lambda b: (b, 1)),
                  pl.BlockSpec((seq_len, D_A), lambda b: (b, 2))],
        out_specs=[pl.BlockSpec((seq_len, D_A), lambda b: (b, 0)), kv_spec, kv_spec],
        out_shape=[jax.ShapeDtypeStruct((t, D_A), F32), kv_shape, kv_shape],
        compiler_params=_cparams("arbitrary"),
        name="context_attention",
    )(u, u, u)


N_DR = 2 * WIN_ROWS - 1
N_DC = 2 * WIN_COLS - 1


def _na_col_tables():
    cols = np.arange(GRID_W)
    col_start = np.clip(cols - WIN_COLS // 2, 0, GRID_W - WIN_COLS)
    col_in = (cols[None, :] >= col_start[:, None]) & (cols[None, :] < col_start[:, None] + WIN_COLS)
    dc = np.clip(cols[None, :] - cols[:, None], 1 - WIN_COLS, WIN_COLS - 1) + WIN_COLS - 1
    onehot = (dc.reshape(1, -1) == np.arange(32)[:, None]).astype(np.float32)
    return onehot, col_in.reshape(1, -1).astype(np.float32)


def _na_bias_kernel(r_ref, e_ref, m_ref, o_ref):
    t = jnp.dot(r_ref[...], e_ref[...], precision=lax.Precision.HIGHEST, preferred_element_type=F32)
    o_ref[...] = jnp.where(m_ref[...] > 0.0, t, NEG_INF)


def na_bias_table(rpb):
    onehot, col_in = _na_col_tables()
    n_rows = NA_HEADS * N_DR
    r = jnp.zeros((LANES, 32), F32).at[:n_rows, :N_DC].set(rpb.reshape(n_rows, N_DC).astype(F32))
    t = pl.pallas_call(
        _na_bias_kernel,
        out_shape=jax.ShapeDtypeStruct((LANES, GRID_W * GRID_W), F32),
        name="na_bias_table",
    )(r, jnp.asarray(onehot), jnp.asarray(col_in))
    t = t[:n_rows].reshape(NA_HEADS, N_DR, GRID_W, GRID_W)
    return jnp.concatenate([t[:, :-1], t[:, 1:]], axis=-1)


def _na_kernel(rows, q_ref, k_ref, v_ref, ck_ref, cv_ref, bias_ref, o_ref, q_s, k_s, v_s, ck_s, cv_s):
    scale = HEAD_DIM ** -0.5
    n_lat = WIN_ROWS * GRID_W
    q_s[...] = q_ref[...].astype(BF16)
    k_s[...] = k_ref[...].astype(BF16)
    v_s[...] = v_ref[...].astype(BF16)
    ck_s[...] = ck_ref[0].astype(BF16)
    cv_s[...] = cv_ref[0].astype(BF16)

    def row_body(r, carry):
        start = jnp.clip(r - WIN_ROWS // 2, 0, rows - WIN_ROWS)
        off = start - r + WIN_ROWS - 1
        q_rows = pl.ds(pl.multiple_of(r * GRID_W, GRID_W), GRID_W)
        w_rows = pl.ds(pl.multiple_of(start * GRID_W, GRID_W), n_lat)
        for h in range(NA_HEADS):
            sl = slice(h * HEAD_DIM, (h + 1) * HEAD_DIM)
            q = q_s[q_rows, sl]
            bias = jnp.concatenate([bias_ref[h, off + 2 * i] for i in range(WIN_ROWS // 2)], axis=1)
            s_lat = _dot_nt(q, k_s[w_rows, sl]) * scale + bias
            s_ctx = _dot_nt(q, ck_s[h]) * scale
            m = jnp.maximum(jnp.max(s_lat, axis=-1, keepdims=True), jnp.max(s_ctx, axis=-1, keepdims=True))
            p_lat = jnp.exp(s_lat - m)
            p_ctx = jnp.exp(s_ctx - m)
            den = jnp.sum(p_lat, axis=-1, keepdims=True) + jnp.sum(p_ctx, axis=-1, keepdims=True)
            o = _dot(p_lat.astype(BF16), v_s[w_rows, sl]) + _dot(p_ctx.astype(BF16), cv_s[h])
            o_ref[q_rows, sl] = o / den
        return carry

    lax.fori_loop(0, rows, row_body, 0)


def neighbourhood_attention(u, ctx_k, ctx_v, rpb, nseq, seq_len):
    t = u.shape[0]
    rows = seq_len // GRID_W
    assert rows >= WIN_ROWS and WIN_ROWS % 2 == 0
    past = ctx_k.shape[2]
    bias = na_bias_table(rpb)
    ctx_spec = pl.BlockSpec((1, NA_HEADS, past, HEAD_DIM), lambda b: (b, 0, 0, 0))
    return pl.pallas_call(
        functools.partial(_na_kernel, rows),
        grid=(nseq,),
        in_specs=[pl.BlockSpec((seq_len, D_A), lambda b: (b, 0)),
                  pl.BlockSpec((seq_len, D_A), lambda b: (b, 1)),
                  pl.BlockSpec((seq_len, D_A), lambda b: (b, 2)),
                  ctx_spec, ctx_spec,
                  pl.BlockSpec(bias.shape, lambda b: (0, 0, 0, 0))],
        out_specs=pl.BlockSpec((seq_len, D_A), lambda b: (b, 0)),
        out_shape=jax.ShapeDtypeStruct((t, D_A), F32),
        scratch_shapes=[pltpu.VMEM((seq_len, D_A), BF16)] * 3
        + [pltpu.VMEM((NA_HEADS, past, HEAD_DIM), BF16)] * 2,
        compiler_params=_cparams("arbitrary"),
        name="neighbourhood_attention",
    )(u, u, u, ctx_k, ctx_v, bias)


def _dft_tables(seq_len):
    n = 2 * seq_len
    f = np.arange(seq_len, dtype=np.int64)
    ang = (np.outer(f, f) % n).astype(np.float64) * (math.pi / seq_len)
    cos, sin = np.cos(ang), np.sin(ang)
    alt = np.where(f % 2 == 0, 1.0, -1.0)
    s_fwd = -sin
    s_fwd[0, :] = alt
    fwd = np.concatenate([cos, s_fwd], axis=0)
    wf = np.where(f == 0, 1.0, 2.0) / n
    ci = cos.T * wf[None, :]
    si = -sin.T * wf[None, :]
    si[:, 0] = alt / n
    inv = np.concatenate([ci, si], axis=1)
    return fwd.astype(np.float32), inv.astype(np.float32)


def _hyena_feats(seq_len):
    t = np.linspace(0.0, 1.0, seq_len, dtype=np.float32)[:, None]
    w = (2.0 * math.pi * np.arange(seq_len, dtype=np.float32)[:, None] / seq_len).astype(np.float32)
    f = np.linspace(1e-4, HY_BANDS - 1, HY_BANDS, dtype=np.float32)[None, :]
    z = np.concatenate([t, np.cos(f * w), -np.sin(f * w)], axis=-1).astype(np.float32)
    max_decay = math.log(HY_DECAY_TARGET) / HY_FAST_PCT
    min_decay = math.log(HY_DECAY_TARGET) / HY_SLOW_PCT
    deltas = np.abs(np.linspace(min_decay, max_decay, D_B, dtype=np.float32))[None, :]
    return z, t, deltas


def _hy_filter_kernel(seq_len, z_ref, t_ref, dl_ref, w1_ref, b1_ref, w2_ref, b2_ref, w3_ref, fr_ref,
                      d_ref, fwd_ref, g_ref):
    hp = lax.Precision.HIGHEST
    h = jnp.sin(fr_ref[0:1, :] * (jnp.dot(z_ref[...], w1_ref[...], precision=hp) + b1_ref[...]))
    h = jnp.sin(fr_ref[1:2, :] * (jnp.dot(h, w2_ref[...], precision=hp) + b2_ref[...]))
    h = jnp.dot(h, w3_ref[...], precision=hp)
    decay = jnp.exp(-t_ref[...] * dl_ref[...])
    row0 = lax.broadcasted_iota(jnp.int32, (seq_len, D_B), 0) == 0
    sums, diffs = [], []
    for n in range(HY_ORDER):
        hf = h[:, (2 * n) * D_B:(2 * n + 1) * D_B] * decay
        hb = h[:, (2 * n + 1) * D_B:(2 * n + 2) * D_B] * decay
        gp = jnp.where(row0, hf + hb + d_ref[n:n + 1, :], hf)
        gm = jnp.where(row0, 0.0, hb)
        sums.append(gp + gm)
        diffs.append(gp - gm)
    rhs = jnp.concatenate(sums + diffs, axis=1).astype(BF16)
    spec = _dot(fwd_ref[...], rhs)
    for n in range(HY_ORDER):
        a = spec[:, n * D_B:(n + 1) * D_B]
        b = spec[:, (HY_ORDER + n) * D_B:(HY_ORDER + n + 1) * D_B]
        g_ref[n, 0:seq_len, :] = a[0:seq_len]
        g_ref[n, seq_len:, :] = jnp.where(row0, a[seq_len:], b[seq_len:])


def hyena_spectrum(seq_len, w1, b1, w2, b2, w3, freq, d, fwd):
    z, t, deltas = _hyena_feats(seq_len)
    return pl.pallas_call(
        functools.partial(_hy_filter_kernel, seq_len),
        out_shape=jax.ShapeDtypeStruct((HY_ORDER, 2 * seq_len, D_B), F32),
        compiler_params=pltpu.CompilerParams(vmem_limit_bytes=VMEM_LIMIT),
        name="hyena_spectrum",
    )(jnp.asarray(z), jnp.asarray(t), jnp.asarray(deltas), w1, b1.reshape(1, -1), w2, b2.reshape(1, -1),
      w3, freq, d, fwd)


def _hyena_kernel(seq_len, u_ref, sw_ref, sb_ref, g_ref, fwd_ref, inv_ref, o_ref):
    u = u_ref[...]
    t_idx = lax.broadcasted_iota(jnp.int32, u.shape, 0)
    prev = jnp.where(t_idx == 0, 0.0, pltpu.roll(u, 1, axis=0))
    nxt = jnp.where(t_idx == seq_len - 1, 0.0, pltpu.roll(u, seq_len - 1, axis=0))
    u = prev * sw_ref[0:1, :] + u * sw_ref[1:2, :] + nxt * sw_ref[2:3, :] + sb_ref[...]
    row0 = lax.broadcasted_iota(jnp.int32, (seq_len, D_B), 0) == 0
    z = u[:, 0:D_B]
    for n in range(HY_ORDER):
        spec = _dot(fwd_ref[...], z.astype(BF16))
        ure, uim = spec[0:seq_len], spec[seq_len:]
        gre, gim = g_ref[n, 0:seq_len, :], g_ref[n, seq_len:, :]
        pim = uim * gim
        yre = ure * gre - jnp.where(row0, 0.0, pim)
        yim = jnp.where(row0, pim, ure * gim + uim * gre)
        y = jnp.concatenate([yre, yim], axis=0).astype(BF16)
        z = u[:, (n + 1) * D_B:(n + 2) * D_B] * _dot(inv_ref[...], y)
    o_ref[...] = z


def hyena_mixer(u, nseq, seq_len, short_w, short_b, spectrum, fwd, inv):
    t = u.shape[0]
    width = (HY_ORDER + 1) * D_B
    col_block = (3 * D_A) // width
    assert col_block * width == 3 * D_A
    return pl.pallas_call(
        functools.partial(_hyena_kernel, seq_len),
        grid=(nseq,),
        in_specs=[pl.BlockSpec((seq_len, width), lambda b: (b, col_block)),
                  pl.BlockSpec(short_w.shape, lambda b: (0, 0)),
                  pl.BlockSpec((1, width), lambda b: (0, 0)),
                  pl.BlockSpec(spectrum.shape, lambda b: (0, 0, 0)),
                  pl.BlockSpec(fwd.shape, lambda b: (0, 0)),
                  pl.BlockSpec(inv.shape, lambda b: (0, 0))],
        out_specs=pl.BlockSpec((seq_len, D_B), lambda b: (b, 0)),
        out_shape=jax.ShapeDtypeStruct((t, D_B), F32),
        compiler_params=_cparams("arbitrary"),
        name="hyena_mixer",
    )(u, short_w, short_b.reshape(1, width), spectrum, fwd, inv)


RG_CB = LANES
RG_PAD = 8
RG_CHUNK = 512


def _rglru_kernel(nseq, seq_len, gate_ref, xr_ref, cw_ref, cb_ref, wg_ref, bg_ref, lam_ref, h0_ref,
                  y_ref, fin_ref, xp_ref, a_f, b_f, a_b, b_b):
    t_tot = nseq * seq_len
    c = RG_CB
    xp_ref[0:RG_PAD, :] = jnp.zeros((RG_PAD, c), F32)
    xp_ref[RG_PAD + t_tot:, :] = jnp.zeros((RG_PAD, c), F32)
    xp_ref[RG_PAD:RG_PAD + t_tot, :] = xr_ref[...]
    nl = -lam_ref[...]
    sp = jnp.maximum(nl, 0.0) + jnp.log1p(jnp.exp(-jnp.abs(nl)))

    def gate_chunk(ci, carry):
        r0 = pl.multiple_of(ci * RG_CHUNK, RG_CHUNK)
        t_idx = (r0 + lax.broadcasted_iota(jnp.int32, (RG_CHUNK, c), 0)) % seq_len
        xm2 = jnp.where(t_idx >= 2, xp_ref[pl.ds(r0 + RG_PAD - 2, RG_CHUNK), :], 0.0)
        xm1 = jnp.where(t_idx >= 1, xp_ref[pl.ds(r0 + RG_PAD - 1, RG_CHUNK), :], 0.0)
        x0 = xp_ref[pl.ds(r0 + RG_PAD, RG_CHUNK), :]
        xp1 = jnp.where(t_idx <= seq_len - 2, xp_ref[pl.ds(r0 + RG_PAD + 1, RG_CHUNK), :], 0.0)
        xc = (xm2 * cw_ref[0:1, :] + xm1 * cw_ref[1:2, :] + x0 * cw_ref[2:3, :] + xp1 * cw_ref[3:4, :]
              + cb_ref[...])
        gts = _dot(xc.astype(BF16), wg_ref[0]) + bg_ref[...]
        for d, (a_ref, b_ref) in enumerate(((a_f, b_f), (a_b, b_b))):
            r = _sigmoid(gts[:, (2 * d) * c:(2 * d + 1) * c])
            i = _sigmoid(gts[:, (2 * d + 1) * c:(2 * d + 2) * c])
            a = jnp.exp(-RG_C * r * sp[d:d + 1, :])
            a_ref[pl.ds(r0, RG_CHUNK), :] = a
            b_ref[pl.ds(r0, RG_CHUNK), :] = jnp.sqrt(jnp.maximum(1.0 - a * a, 0.0)) * (i * xc)
        return carry

    lax.fori_loop(0, t_tot // RG_CHUNK, gate_chunk, 0)

    def scan_step(t, carry):
        hf, hb = carry
        rows_f = pl.ds(t, nseq, stride=seq_len)
        rows_b = pl.ds(seq_len - 1 - t, nseq, stride=seq_len)
        hf = a_f[rows_f, :] * hf + b_f[rows_f, :]
        hb = a_b[rows_b, :] * hb + b_b[rows_b, :]
        b_f[rows_f, :] = hf
        b_b[rows_b, :] = hb
        return hf, hb

    hf, hb = lax.fori_loop(0, seq_len, scan_step, (h0_ref[0], h0_ref[1]), unroll=8)
    fin_ref[0] = hf
    fin_ref[1] = hb

    def out_chunk(ci, carry):
        rs = pl.ds(pl.multiple_of(ci * RG_CHUNK, RG_CHUNK), RG_CHUNK)
        y_ref[rs, :] = (b_f[rs, :] + b_b[rs, :]) * jax.nn.gelu(gate_ref[rs, :])
        return carry

    lax.fori_loop(0, t_tot // RG_CHUNK, out_chunk, 0)


def _rg_gate_weights(wa, wx):
    per_step = RG_CB // RG_BLOCK
    steps = D_RNN // RG_CB
    mats = []
    for d in range(2):
        for w in (wa[d], wx[d]):
            w = w.reshape(steps, per_step, RG_BLOCK, RG_BLOCK)
            eye = jnp.eye(per_step, dtype=w.dtype)
            m = jnp.einsum('spde,pq->spdqe', w, eye).reshape(steps, RG_CB, RG_CB)
            mats.append(m)
    return jnp.concatenate(mats, axis=-1).astype(BF16)


def rglru_block(u, nseq, seq_len, conv_w, conv_b, wa, ba, wx, bx, lam, h0):
    t = u.shape[0]
    c = RG_CB
    steps = D_RNN // c
    wg = _rg_gate_weights(wa, wx)
    bg = jnp.stack([ba[0], bx[0], ba[1], bx[1]], axis=0).reshape(4, steps, c)
    bg = bg.transpose(1, 0, 2).reshape(steps, 1, 4 * c)
    y, fin = pl.pallas_call(
        functools.partial(_rglru_kernel, nseq, seq_len),
        grid=(steps,),
        in_specs=[pl.BlockSpec((t, c), lambda j: (0, j)),
                  pl.BlockSpec((t, c), lambda j: (0, steps + j)),
                  pl.BlockSpec((conv_w.shape[0], c), lambda j: (0, j)),
                  pl.BlockSpec((1, c), lambda j: (0, j)),
                  pl.BlockSpec((1, c, 4 * c), lambda j: (j, 0, 0)),
                  pl.BlockSpec((None, 1, 4 * c), lambda j: (j, 0, 0)),
                  pl.BlockSpec((2, c), lambda j: (0, j)),
                  pl.BlockSpec((2, nseq, c), lambda j: (0, 0, j))],
        out_specs=[pl.BlockSpec((t, c), lambda j: (0, j)),
                   pl.BlockSpec((2, nseq, c), lambda j: (0, 0, j))],
        out_shape=[jax.ShapeDtypeStruct((t, D_RNN), F32),
                   jax.ShapeDtypeStruct((2, nseq, D_RNN), F32)],
        scratch_shapes=[pltpu.VMEM((t + 2 * RG_PAD, c), F32)] + [pltpu.VMEM((t, c), F32)] * 4,
        compiler_params=_cparams("arbitrary"),
        name="rglru_block",
    )(u, u, conv_w, conv_b.reshape(1, -1), wg, bg, lam, h0)
    return y, fin


def _router_kernel(x_ref, g_ref, sc_ref, sh_ref, rw_ref, rb_ref, o_ref):
    h = _normmod(x_ref[...], g_ref[...], sc_ref[0], sh_ref[0])
    logits = lax.dot_general(rw_ref[...], h, (((1,), (1,)), ((), ())),
                             precision=lax.Precision.HIGHEST, preferred_element_type=F32)
    scores = jax.nn.sigmoid(logits)
    sel = scores + rb_ref[...]
    row = [sel[e:e + 1, :] for e in range(N_EXPERTS)]
    gs = []
    for g in range(N_GROUPS):
        r = row[g * EXPERTS_PER_GROUP:(g + 1) * EXPERTS_PER_GROUP]
        best_pair = None
        for i in range(EXPERTS_PER_GROUP):
            for j in range(i + 1, EXPERTS_PER_GROUP):
                s = r[i] + r[j]
                best_pair = s if best_pair is None else jnp.maximum(best_pair, s)
        gs.append(best_pair)
    best = jnp.zeros_like(gs[0], dtype=jnp.int32)
    top = gs[0]
    for g in range(1, N_GROUPS):
        better = gs[g] > top
        best = jnp.where(better, g, best)
        top = jnp.where(better, gs[g], top)
    picked = []
    for e in range(N_EXPERTS):
        g = e // EXPERTS_PER_GROUP
        rank = jnp.zeros_like(best)
        for o in range(g * EXPERTS_PER_GROUP, (g + 1) * EXPERTS_PER_GROUP):
            if o == e:
                continue
            ahead = (row[o] > row[e]) | ((row[o] == row[e]) & (o < e))
            rank = rank + ahead.astype(jnp.int32)
        picked.append((best == g) & (rank < 2))
    den = jnp.zeros_like(gs[0])
    for e in range(N_EXPERTS):
        den = den + jnp.where(picked[e], scores[e:e + 1, :], 0.0)
    for e in range(N_EXPERTS):
        o_ref[e:e + 1, :] = jnp.where(picked[e], scores[e:e + 1, :] / den, 0.0)


def router_gates(x, g, mod, router_w, router_b, seq_len, per_seq, tm=512):
    t, d = x.shape
    return pl.pallas_call(
        _router_kernel,
        grid=(t // tm,),
        in_specs=[pl.BlockSpec((tm, d), lambda i: (i, 0)),
                  pl.BlockSpec((1, d), lambda i: (0, 0)),
                  _mod_spec(4, tm, seq_len, per_seq),
                  _mod_spec(3, tm, seq_len, per_seq),
                  pl.BlockSpec((N_EXPERTS, d), lambda i: (0, 0)),
                  pl.BlockSpec((N_EXPERTS, 1), lambda i: (0, 0))],
        out_specs=pl.BlockSpec((N_EXPERTS, tm), lambda i: (0, i)),
        out_shape=jax.ShapeDtypeStruct((N_EXPERTS, t), F32),
        compiler_params=_cparams("arbitrary"),
        name="moe_router",
    )(x, g.reshape(1, d), mod, mod, router_w.T, router_b.reshape(N_EXPERTS, 1))


def _moe_kernel(final, x_ref, g_ref, sc_ref, sh_ref, g2_ref, gates_ref, wg_ref, wu_ref, wd_ref, fg_ref,
                o_ref, h_scr, acc):
    e = pl.program_id(1)

    @pl.when(e == 0)
    def _():
        h_scr[...] = _normmod(x_ref[...], g_ref[...], sc_ref[0], sh_ref[0]).astype(BF16)
        acc[...] = jnp.zeros_like(acc)

    h = h_scr[...]
    hid = _dot(h, wg_ref[0].astype(BF16))
    up = _dot(h, wu_ref[0].astype(BF16))
    lane = lax.broadcasted_iota(jnp.int32, gates_ref.shape, 1)
    gate = jnp.sum(jnp.where(lane == e, gates_ref[...], 0.0), axis=1, keepdims=True)
    act = (hid * jax.nn.sigmoid(hid)) * up * gate
    acc[...] += _dot(act.astype(BF16), wd_ref[0].astype(BF16))

    @pl.when(e == N_EXPERTS - 1)
    def _():
        y = x_ref[...] + g2_ref[0] * acc[...]
        if final:
            ms = jnp.mean(y * y, axis=-1, keepdims=True)
            y = y * lax.rsqrt(ms + EPS) * fg_ref[...]
        o_ref[...] = y


def moe_layer(x, g, mod, gates, layer, w_gate, w_up, w_down, final_g, final, seq_len, per_seq, tm=512):
    t, d = x.shape
    return pl.pallas_call(
        functools.partial(_moe_kernel, final),
        grid=(t // tm, N_EXPERTS),
        in_specs=[pl.BlockSpec((tm, d), lambda i, e: (i, 0)),
                  pl.BlockSpec((1, d), lambda i, e: (0, 0)),
                  _mod_spec(4, tm, seq_len, per_seq),
                  _mod_spec(3, tm, seq_len, per_seq),
                  _mod_spec(5, tm, seq_len, per_seq),
                  pl.BlockSpec((tm, N_EXPERTS), lambda i, e: (i, 0)),
                  pl.BlockSpec((None, 1, d, D_EXPERT), lambda i, e: (layer, e, 0, 0)),
                  pl.BlockSpec((None, 1, d, D_EXPERT), lambda i, e: (layer, e, 0, 0)),
                  pl.BlockSpec((None, 1, D_EXPERT, d), lambda i, e: (layer, e, 0, 0)),
                  pl.BlockSpec((1, d), lambda i, e: (0, 0))],
        out_specs=pl.BlockSpec((tm, d), lambda i, e: (i, 0)),
        out_shape=jax.ShapeDtypeStruct((t, d), F32),
        scratch_shapes=[pltpu.VMEM((tm, d), BF16), pltpu.VMEM((tm, d), F32)],
        compiler_params=_cparams("arbitrary", "arbitrary"),
        name="moe_experts",
    )(x, g.reshape(1, d), mod, mod, mod, gates, w_gate, w_up, w_down, final_g.reshape(1, d))


def _trunk(x, per_seq, nseq, seq_len, mods, ctx_k, ctx_v, h0, p, hy_tables):
    new_kv = None
    new_h = None
    for l in range(DEPTH):
        mod = mods[l]
        if l % 2 == 0:
            e = l // 2
            u = nm_matmul(x, p['norm_g'][l, 0], mod, p['a_in_w'][e].astype(BF16), seq_len, per_seq)
            if ctx_k is None:
                attn, nk, nv = context_attention(u, nseq, seq_len)
                new_kv = (nk, nv)
            else:
                attn = neighbourhood_attention(u, ctx_k[:, e], ctx_v[:, e], p['na_rpb'][e], nseq, seq_len)
            fwd, inv = hy_tables
            spectrum = hyena_spectrum(seq_len, p['hy_w1'][e], p['hy_b1'][e], p['hy_w2'][e], p['hy_b2'][e],
                                      p['hy_w3'][e], p['hy_freq'][e], p['hy_d'][e], fwd)
            hy = hyena_mixer(u, nseq, seq_len, p['hy_short_w'][e], p['hy_short_b'][e], spectrum, fwd, inv)
            w_out = p['a_out_w'][e].astype(BF16)
            x = proj_residual([attn, hy], [w_out[:D_A], w_out[D_A:]], x, mod, seq_len, per_seq)
        else:
            o = l // 2
            u = nm_matmul(x, p['norm_g'][l, 0], mod, p['c_in_w'][o].astype(BF16), seq_len, per_seq)
            y, fin = rglru_block(u, nseq, seq_len, p['rg_conv_w'][o], p['rg_conv_b'][o], p['rg_wa'][o],
                                 p['rg_ba'][o], p['rg_wx'][o], p['rg_bx'][o], p['rg_lam'][o], h0)
            new_h = fin
            x = proj_residual([y], [p['c_out_w'][o].astype(BF16)], x, mod, seq_len, per_seq)
        gates = router_gates(x, p['norm_g'][l, 1], mod, p['router_w'], p['router_b'], seq_len, per_seq)
        x = moe_layer(x, p['norm_g'][l, 1], mod, gates.T, l, p['moe_w_gate'], p['moe_w_up'],
                      p['moe_w_down'], p['final_g'], l == DEPTH - 1, seq_len, per_seq)
    return x, new_kv, new_h


def kernel(x_prompt, x_sample, cache_k, cache_v, state_h, c, c_ctx, norm_g, ada_w, ada_b, final_g, a_in_w, a_out_w, na_rpb, hy_short_w, hy_short_b, hy_w1, hy_b1, hy_w2, hy_b2, hy_w3, hy_freq, hy_d, c_in_w, c_out_w, rg_conv_w, rg_conv_b, rg_wa, rg_ba, rg_wx, rg_bx, rg_lam, router_w, router_b, moe_w_gate, moe_w_up, moe_w_down):
    p = dict(norm_g=norm_g, final_g=final_g, a_in_w=a_in_w, a_out_w=a_out_w, na_rpb=na_rpb,
             hy_short_w=hy_short_w, hy_short_b=hy_short_b, hy_w1=hy_w1, hy_b1=hy_b1, hy_w2=hy_w2,
             hy_b2=hy_b2, hy_w3=hy_w3, hy_freq=hy_freq, hy_d=hy_d, c_in_w=c_in_w, c_out_w=c_out_w,
             rg_conv_w=rg_conv_w, rg_conv_b=rg_conv_b, rg_wa=rg_wa, rg_ba=rg_ba, rg_wx=rg_wx, rg_bx=rg_bx,
             rg_lam=rg_lam, router_w=router_w, router_b=router_b, moe_w_gate=moe_w_gate,
             moe_w_up=moe_w_up, moe_w_down=moe_w_down)
    batch, seq, d = x_prompt.shape
    dec_batch, dec_seq, _ = x_sample.shape
    assert 1 + dec_batch <= MOD_ROWS

    cond = jnp.concatenate([c_ctx[None, :], c, jnp.zeros((MOD_ROWS - 1 - dec_batch, d), F32)], axis=0)
    m = modulation(cond, ada_w, ada_b)
    mods = [m[l].reshape(MOD_ROWS * N_MOD, 1, d) for l in range(DEPTH)]

    tables = {}
    for sl in (seq, dec_seq):
        fwd, inv = _dft_tables(sl)
        tables[sl] = (jnp.asarray(fwd).astype(BF16), jnp.asarray(inv).astype(BF16))

    h0 = jnp.zeros((2, batch, D_RNN), F32)
    y_p, (new_k, new_v), fin = _trunk(x_prompt.reshape(batch * seq, d), False, batch, seq, mods,
                                      None, None, h0, p, tables[seq])
    h0 = state_h[:, 0].transpose(1, 0, 2) if DEPTH // 2 == 1 else None
    y_s, _, _ = _trunk(x_sample.reshape(dec_batch * dec_seq, d), True, dec_batch, dec_seq, mods,
                       cache_k, cache_v, h0, p, tables[dec_seq])
    new_h = fin.transpose(1, 0, 2)[:, None]
    return (y_p.reshape(batch, seq, d), y_s.reshape(dec_batch, dec_seq, d),
            new_k[:, None], new_v[:, None], new_h)
```

```python
import functools
import math

import numpy as np
import jax
import jax.numpy as jnp
from jax import lax
from jax.experimental import pallas as pl
from jax.experimental.pallas import tpu as pltpu

F32 = jnp.float32
BF16 = jnp.bfloat16

D_MODEL = 1024
DEPTH = 2
GRID_W = 64
EPS = 1e-6
NEG_INF = -1e30
NA_HEADS = 8
HEAD_DIM = 64
D_A = NA_HEADS * HEAD_DIM
WIN_ROWS = 8
WIN_COLS = 16
D_B = D_MODEL - D_A
HY_ORDER = 2
HY_EMB = 33
HY_BANDS = (HY_EMB - 1) // 2
HY_FFN = 64
HY_DECAY_TARGET = 1e-2
HY_FAST_PCT = 0.3
HY_SLOW_PCT = 1.5
D_RNN = D_MODEL
RG_BLOCK = 64
RG_C = 8.0
N_EXPERTS = 16
N_GROUPS = 4
EXPERTS_PER_GROUP = N_EXPERTS // N_GROUPS
D_EXPERT = 512

LANES = 128
VMEM_LIMIT = 56 * 1024 * 1024
N_MOD = 6
MOD_ROWS = 16


def _cparams(*sem):
    return pltpu.CompilerParams(dimension_semantics=sem, vmem_limit_bytes=VMEM_LIMIT)


def _dot(a, b):
    return jnp.dot(a, b, preferred_element_type=F32)


def _dot_nt(a, b):
    return lax.dot_general(a, b, (((1,), (1,)), ((), ())), preferred_element_type=F32)


def _sigmoid(x):
    return 0.5 * jnp.tanh(0.5 * x) + 0.5


def _normmod(x, g, sc, sh):
    ms = jnp.mean(x * x, axis=-1, keepdims=True)
    return (x * lax.rsqrt(ms + EPS) * g) * (1.0 + sc) + sh


def _mod_spec(chunk, tm, seq_len, per_seq):
    if per_seq:
        return pl.BlockSpec((1, 1, D_MODEL), lambda i, *_: ((1 + (i * tm) // seq_len) * N_MOD + chunk, 0, 0))
    return pl.BlockSpec((1, 1, D_MODEL), lambda i, *_: (chunk, 0, 0))


def _mod_kernel(c_ref, w_ref, b_ref, o_ref):
    s = c_ref[...]
    s = s * jax.nn.sigmoid(s)
    o_ref[0] = _dot(s.astype(BF16), w_ref[0].astype(BF16)) + b_ref[0]


def modulation(cond, ada_w, ada_b):
    tn = 1536
    n = ada_w.shape[-1]
    return pl.pallas_call(
        _mod_kernel,
        grid=(DEPTH, n // tn),
        in_specs=[pl.BlockSpec((MOD_ROWS, D_MODEL), lambda l, j: (0, 0)),
                  pl.BlockSpec((1, D_MODEL, tn), lambda l, j: (l, 0, j)),
                  pl.BlockSpec((1, 1, tn), lambda l, j: (l, 0, j))],
        out_specs=pl.BlockSpec((1, MOD_ROWS, tn), lambda l, j: (l, 0, j)),
        out_shape=jax.ShapeDtypeStruct((DEPTH, MOD_ROWS, n), F32),
        compiler_params=_cparams("arbitrary", "arbitrary"),
        name="modulation",
    )(cond, ada_w, ada_b.reshape(DEPTH, 1, n))


def _nm_matmul_kernel(x_ref, g_ref, sc_ref, sh_ref, w_ref, o_ref):
    h = _normmod(x_ref[...], g_ref[...], sc_ref[0], sh_ref[0])
    o_ref[...] = _dot(h.astype(BF16), w_ref[...])


def nm_matmul(x, g, mod, w, seq_len, per_seq, tm=256):
    t, d = x.shape
    n = w.shape[1]
    return pl.pallas_call(
        _nm_matmul_kernel,
        grid=(t // tm,),
        in_specs=[pl.BlockSpec((tm, d), lambda i: (i, 0)),
                  pl.BlockSpec((1, d), lambda i: (0, 0)),
                  _mod_spec(1, tm, seq_len, per_seq),
                  _mod_spec(0, tm, seq_len, per_seq),
                  pl.BlockSpec((d, n), lambda i: (0, 0))],
        out_specs=pl.BlockSpec((tm, n), lambda i: (i, 0)),
        out_shape=jax.ShapeDtypeStruct((t, n), F32),
        compiler_params=_cparams("arbitrary"),
        name="norm_mod_proj",
    )(x, g.reshape(1, d), mod, mod, w)


def _proj_res_kernel(n_act, *refs):
    acts = refs[:n_act]
    ws = refs[n_act:2 * n_act]
    x_ref, g_ref, o_ref = refs[2 * n_act:]
    acc = _dot(acts[0][...].astype(BF16), ws[0][...])
    for a, w in zip(acts[1:], ws[1:]):
        acc += _dot(a[...].astype(BF16), w[...])
    o_ref[...] = x_ref[...] + g_ref[0] * acc


def proj_residual(acts, ws, x, mod, seq_len, per_seq, tm=512):
    t, d = x.shape
    in_specs = [pl.BlockSpec((tm, a.shape[1]), lambda i: (i, 0)) for a in acts]
    in_specs += [pl.BlockSpec(w.shape, lambda i: (0, 0)) for w in ws]
    in_specs += [pl.BlockSpec((tm, d), lambda i: (i, 0)), _mod_spec(2, tm, seq_len, per_seq)]
    return pl.pallas_call(
        functools.partial(_proj_res_kernel, len(acts)),
        grid=(t // tm,),
        in_specs=in_specs,
        out_specs=pl.BlockSpec((tm, d), lambda i: (i, 0)),
        out_shape=jax.ShapeDtypeStruct((t, d), F32),
        compiler_params=_cparams("arbitrary"),
        name="proj_residual",
    )(*acts, *ws, x, mod)


def _ctx_attn_kernel(q_ref, k_ref, v_ref, o_ref, nk_ref, nv_ref):
    scale = HEAD_DIM ** -0.5
    for h in range(NA_HEADS):
        sl = slice(h * HEAD_DIM, (h + 1) * HEAD_DIM)
        q = q_ref[:, sl]
        k = k_ref[:, sl]
        v = v_ref[:, sl]
        nk_ref[0, h] = k
        nv_ref[0, h] = v
        s = _dot_nt(q.astype(BF16), k.astype(BF16)) * scale
        m = jnp.max(s, axis=-1, keepdims=True)
        p = jnp.exp(s - m)
        den = jnp.sum(p, axis=-1, keepdims=True)
        o_ref[:, sl] = _dot(p.astype(BF16), v.astype(BF16)) / den


def context_attention(u, nseq, seq_len):
    t = u.shape[0]
    kv_shape = jax.ShapeDtypeStruct((nseq, NA_HEADS, seq_len, HEAD_DIM), F32)
    kv_spec = pl.BlockSpec((1, NA_HEADS, seq_len, HEAD_DIM), lambda b: (b, 0, 0, 0))
    return pl.pallas_call(
        _ctx_attn_kernel,
        grid=(nseq,),
        in_specs=[pl.BlockSpec((seq_len, D_A), lambda b: (b, 0)),
                  pl.BlockSpec((seq_len, D_A), lambda b: (b, 1)),
                  pl.BlockSpec((seq_len, D_A), lambda b: (b, 2))],
        out_specs=[pl.BlockSpec((seq_len, D_A), lambda b: (b, 0)), kv_spec, kv_spec],
        out_shape=[jax.ShapeDtypeStruct((t, D_A), F32), kv_shape, kv_shape],
        compiler_params=_cparams("arbitrary"),
        name="context_attention",
    )(u, u, u)


N_DR = 2 * WIN_ROWS - 1
N_DC = 2 * WIN_COLS - 1


def _na_col_tables():
    cols = np.arange(GRID_W)
    col_start = np.clip(cols - WIN_COLS // 2, 0, GRID_W - WIN_COLS)
    col_in = (cols[None, :] >= col_start[:, None]) & (cols[None, :] < col_start[:, None] + WIN_COLS)
    dc = np.clip(cols[None, :] - cols[:, None], 1 - WIN_COLS, WIN_COLS - 1) + WIN_COLS - 1
    onehot = (dc.reshape(1, -1) == np.arange(32)[:, None]).astype(np.float32)
    return onehot, col_in.reshape(1, -1).astype(np.float32)


def _na_bias_kernel(r_ref, e_ref, m_ref, o_ref):
    t = jnp.dot(r_ref[...], e_ref[...], precision=lax.Precision.HIGHEST, preferred_element_type=F32)
    o_ref[...] = jnp.where(m_ref[...] > 0.0, t, NEG_INF)


def na_bias_table(rpb):
    onehot, col_in = _na_col_tables()
    n_rows = NA_HEADS * N_DR
    r = jnp.zeros((LANES, 32), F32).at[:n_rows, :N_DC].set(rpb.reshape(n_rows, N_DC).astype(F32))
    t = pl.pallas_call(
        _na_bias_kernel,
        out_shape=jax.ShapeDtypeStruct((LANES, GRID_W * GRID_W), F32),
        name="na_bias_table",
    )(r, jnp.asarray(onehot), jnp.asarray(col_in))
    t = t[:n_rows].reshape(NA_HEADS, N_DR, GRID_W, GRID_W)
    return jnp.concatenate([t[:, :-1], t[:, 1:]], axis=-1)


def _na_kernel(rows, q_ref, k_ref, v_ref, ck_ref, cv_ref, bias_ref, o_ref, q_s, k_s, v_s, ck_s, cv_s):
    scale = HEAD_DIM ** -0.5
    n_lat = WIN_ROWS * GRID_W
    q_s[...] = q_ref[...].astype(BF16)
    k_s[...] = k_ref[...].astype(BF16)
    v_s[...] = v_ref[...].astype(BF16)
    ck_s[...] = ck_ref[0].astype(BF16)
    cv_s[...] = cv_ref[0].astype(BF16)

    def row_body(r, carry):
        start = jnp.clip(r - WIN_ROWS // 2, 0, rows - WIN_ROWS)
        off = start - r + WIN_ROWS - 1
        q_rows = pl.ds(pl.multiple_of(r * GRID_W, GRID_W), GRID_W)
        w_rows = pl.ds(pl.multiple_of(start * GRID_W, GRID_W), n_lat)
        for h in range(NA_HEADS):
            sl = slice(h * HEAD_DIM, (h + 1) * HEAD_DIM)
            q = q_s[q_rows, sl]
            bias = jnp.concatenate([bias_ref[h, off + 2 * i] for i in range(WIN_ROWS // 2)], axis=1)
            s_lat = _dot_nt(q, k_s[w_rows, sl]) * scale + bias
            s_ctx = _dot_nt(q, ck_s[h]) * scale
            m = jnp.maximum(jnp.max(s_lat, axis=-1, keepdims=True), jnp.max(s_ctx, axis=-1, keepdims=True))
            p_lat = jnp.exp(s_lat - m)
            p_ctx = jnp.exp(s_ctx - m)
            den = jnp.sum(p_lat, axis=-1, keepdims=True) + jnp.sum(p_ctx, axis=-1, keepdims=True)
            o = _dot(p_lat.astype(BF16), v_s[w_rows, sl]) + _dot(p_ctx.astype(BF16), cv_s[h])
            o_ref[q_rows, sl] = o / den
        return carry

    lax.fori_loop(0, rows, row_body, 0)


def neighbourhood_attention(u, ctx_k, ctx_v, rpb, nseq, seq_len):
    t = u.shape[0]
    rows = seq_len // GRID_W
    assert rows >= WIN_ROWS and WIN_ROWS % 2 == 0
    past = ctx_k.shape[2]
    bias = na_bias_table(rpb)
    ctx_spec = pl.BlockSpec((1, NA_HEADS, past, HEAD_DIM), lambda b: (b, 0, 0, 0))
    return pl.pallas_call(
        functools.partial(_na_kernel, rows),
        grid=(nseq,),
        in_specs=[pl.BlockSpec((seq_len, D_A), lambda b: (b, 0)),
                  pl.BlockSpec((seq_len, D_A), lambda b: (b, 1)),
                  pl.BlockSpec((seq_len, D_A), lambda b: (b, 2)),
                  ctx_spec, ctx_spec,
                  pl.BlockSpec(bias.shape, lambda b: (0, 0, 0, 0))],
        out_specs=pl.BlockSpec((seq_len, D_A), lambda b: (b, 0)),
        out_shape=jax.ShapeDtypeStruct((t, D_A), F32),
        scratch_shapes=[pltpu.VMEM((seq_len, D_A), BF16)] * 3
        + [pltpu.VMEM((NA_HEADS, past, HEAD_DIM), BF16)] * 2,
        compiler_params=_cparams("arbitrary"),
        name="neighbourhood_attention",
    )(u, u, u, ctx_k, ctx_v, bias)


def _dft_tables(seq_len):
    n = 2 * seq_len
    f = np.arange(seq_len, dtype=np.int64)
    ang = (np.outer(f, f) % n).astype(np.float64) * (math.pi / seq_len)
    cos, sin = np.cos(ang), np.sin(ang)
    alt = np.where(f % 2 == 0, 1.0, -1.0)
    s_fwd = -sin
    s_fwd[0, :] = alt
    fwd = np.concatenate([cos, s_fwd], axis=0)
    wf = np.where(f == 0, 1.0, 2.0) / n
    ci = cos.T * wf[None, :]
    si = -sin.T * wf[None, :]
    si[:, 0] = alt / n
    inv = np.concatenate([ci, si], axis=1)
    return fwd.astype(np.float32), inv.astype(np.float32)


def _hyena_feats(seq_len):
    t = np.linspace(0.0, 1.0, seq_len, dtype=np.float32)[:, None]
    w = (2.0 * math.pi * np.arange(seq_len, dtype=np.float32)[:, None] / seq_len).astype(np.float32)
    f = np.linspace(1e-4, HY_BANDS - 1, HY_BANDS, dtype=np.float32)[None, :]
    z = np.concatenate([t, np.cos(f * w), -np.sin(f * w)], axis=-1).astype(np.float32)
    max_decay = math.log(HY_DECAY_TARGET) / HY_FAST_PCT
    min_decay = math.log(HY_DECAY_TARGET) / HY_SLOW_PCT
    deltas = np.abs(np.linspace(min_decay, max_decay, D_B, dtype=np.float32))[None, :]
    return z, t, deltas


def _hy_filter_kernel(seq_len, z_ref, t_ref, dl_ref, w1_ref, b1_ref, w2_ref, b2_ref, w3_ref, fr_ref,
                      d_ref, fwd_ref, g_ref):
    hp = lax.Precision.HIGHEST
    h = jnp.sin(fr_ref[0:1, :] * (jnp.dot(z_ref[...], w1_ref[...], precision=hp) + b1_ref[...]))
    h = jnp.sin(fr_ref[1:2, :] * (jnp.dot(h, w2_ref[...], precision=hp) + b2_ref[...]))
    h = jnp.dot(h, w3_ref[...], precision=hp)
    decay = jnp.exp(-t_ref[...] * dl_ref[...])
    row0 = lax.broadcasted_iota(jnp.int32, (seq_len, D_B), 0) == 0
    sums, diffs = [], []
    for n in range(HY_ORDER):
        hf = h[:, (2 * n) * D_B:(2 * n + 1) * D_B] * decay
        hb = h[:, (2 * n + 1) * D_B:(2 * n + 2) * D_B] * decay
        gp = jnp.where(row0, hf + hb + d_ref[n:n + 1, :], hf)
        gm = jnp.where(row0, 0.0, hb)
        sums.append(gp + gm)
        diffs.append(gp - gm)
    rhs = jnp.concatenate(sums + diffs, axis=1).astype(BF16)
    spec = _dot(fwd_ref[...], rhs)
    for n in range(HY_ORDER):
        a = spec[:, n * D_B:(n + 1) * D_B]
        b = spec[:, (HY_ORDER + n) * D_B:(HY_ORDER + n + 1) * D_B]
        g_ref[n, 0:seq_len, :] = a[0:seq_len]
        g_ref[n, seq_len:, :] = jnp.where(row0, a[seq_len:], b[seq_len:])


def hyena_spectrum(seq_len, w1, b1, w2, b2, w3, freq, d, fwd):
    z, t, deltas = _hyena_feats(seq_len)
    return pl.pallas_call(
        functools.partial(_hy_filter_kernel, seq_len),
        out_shape=jax.ShapeDtypeStruct((HY_ORDER, 2 * seq_len, D_B), F32),
        compiler_params=pltpu.CompilerParams(vmem_limit_bytes=VMEM_LIMIT),
        name="hyena_spectrum",
    )(jnp.asarray(z), jnp.asarray(t), jnp.asarray(deltas), w1, b1.reshape(1, -1), w2, b2.reshape(1, -1),
      w3, freq, d, fwd)


def _hyena_kernel(seq_len, u_ref, sw_ref, sb_ref, g_ref, fwd_ref, inv_ref, o_ref):
    u = u_ref[...]
    t_idx = lax.broadcasted_iota(jnp.int32, u.shape, 0)
    prev = jnp.where(t_idx == 0, 0.0, pltpu.roll(u, 1, axis=0))
    nxt = jnp.where(t_idx == seq_len - 1, 0.0, pltpu.roll(u, seq_len - 1, axis=0))
    u = prev * sw_ref[0:1, :] + u * sw_ref[1:2, :] + nxt * sw_ref[2:3, :] + sb_ref[...]
    row0 = lax.broadcasted_iota(jnp.int32, (seq_len, D_B), 0) == 0
    z = u[:, 0:D_B]
    for n in range(HY_ORDER):
        spec = _dot(fwd_ref[...], z.astype(BF16))
        ure, uim = spec[0:seq_len], spec[seq_len:]
        gre, gim = g_ref[n, 0:seq_len, :], g_ref[n, seq_len:, :]
        pim = uim * gim
        yre = ure * gre - jnp.where(row0, 0.0, pim)
        yim = jnp.where(row0, pim, ure * gim + uim * gre)
        y = jnp.concatenate([yre, yim], axis=0).astype(BF16)
        z = u[:, (n + 1) * D_B:(n + 2) * D_B] * _dot(inv_ref[...], y)
    o_ref[...] = z


def hyena_mixer(u, nseq, seq_len, short_w, short_b, spectrum, fwd, inv):
    t = u.shape[0]
    width = (HY_ORDER + 1) * D_B
    col_block = (3 * D_A) // width
    assert col_block * width == 3 * D_A
    return pl.pallas_call(
        functools.partial(_hyena_kernel, seq_len),
        grid=(nseq,),
        in_specs=[pl.BlockSpec((seq_len, width), lambda b: (b, col_block)),
                  pl.BlockSpec(short_w.shape, lambda b: (0, 0)),
                  pl.BlockSpec((1, width), lambda b: (0, 0)),
                  pl.BlockSpec(spectrum.shape, lambda b: (0, 0, 0)),
                  pl.BlockSpec(fwd.shape, lambda b: (0, 0)),
                  pl.BlockSpec(inv.shape, lambda b: (0, 0))],
        out_specs=pl.BlockSpec((seq_len, D_B), lambda b: (b, 0)),
        out_shape=jax.ShapeDtypeStruct((t, D_B), F32),
        compiler_params=_cparams("arbitrary"),
        name="hyena_mixer",
    )(u, short_w, short_b.reshape(1, width), spectrum, fwd, inv)


RG_CB = LANES
RG_PAD = 8
RG_CHUNK = 512


def _rglru_kernel(nseq, seq_len, gate_ref, xr_ref, cw_ref, cb_ref, wg_ref, bg_ref, lam_ref, h0_ref,
                  y_ref, fin_ref, xp_ref, a_f, b_f, a_b, b_b):
    t_tot = nseq * seq_len
    c = RG_CB
    xp_ref[0:RG_PAD, :] = jnp.zeros((RG_PAD, c), F32)
    xp_ref[RG_PAD + t_tot:, :] = jnp.zeros((RG_PAD, c), F32)
    xp_ref[RG_PAD:RG_PAD + t_tot, :] = xr_ref[...]
    nl = -lam_ref[...]
    sp = jnp.maximum(nl, 0.0) + jnp.log1p(jnp.exp(-jnp.abs(nl)))

    def gate_chunk(ci, carry):
        r0 = pl.multiple_of(ci * RG_CHUNK, RG_CHUNK)
        t_idx = (r0 + lax.broadcasted_iota(jnp.int32, (RG_CHUNK, c), 0)) % seq_len
        xm2 = jnp.where(t_idx >= 2, xp_ref[pl.ds(r0 + RG_PAD - 2, RG_CHUNK), :], 0.0)
        xm1 = jnp.where(t_idx >= 1, xp_ref[pl.ds(r0 + RG_PAD - 1, RG_CHUNK), :], 0.0)
        x0 = xp_ref[pl.ds(r0 + RG_PAD, RG_CHUNK), :]
        xp1 = jnp.where(t_idx <= seq_len - 2, xp_ref[pl.ds(r0 + RG_PAD + 1, RG_CHUNK), :], 0.0)
        xc = (xm2 * cw_ref[0:1, :] + xm1 * cw_ref[1:2, :] + x0 * cw_ref[2:3, :] + xp1 * cw_ref[3:4, :]
              + cb_ref[...])
        gts = _dot(xc.astype(BF16), wg_ref[0]) + bg_ref[...]
        for d, (a_ref, b_ref) in enumerate(((a_f, b_f), (a_b, b_b))):
            r = _sigmoid(gts[:, (2 * d) * c:(2 * d + 1) * c])
            i = _sigmoid(gts[:, (2 * d + 1) * c:(2 * d + 2) * c])
            a = jnp.exp(-RG_C * r * sp[d:d + 1, :])
            a_ref[pl.ds(r0, RG_CHUNK), :] = a
            b_ref[pl.ds(r0, RG_CHUNK), :] = jnp.sqrt(jnp.maximum(1.0 - a * a, 0.0)) * (i * xc)
        return carry

    lax.fori_loop(0, t_tot // RG_CHUNK, gate_chunk, 0)

    def scan_step(t, carry):
        hf, hb = carry
        rows_f = pl.ds(t, nseq, stride=seq_len)
        rows_b = pl.ds(seq_len - 1 - t, nseq, stride=seq_len)
        hf = a_f[rows_f, :] * hf + b_f[rows_f, :]
        hb = a_b[rows_b, :] * hb + b_b[rows_b, :]
        b_f[rows_f, :] = hf
        b_b[rows_b, :] = hb
        return hf, hb

    hf, hb = lax.fori_loop(0, seq_len, scan_step, (h0_ref[0], h0_ref[1]), unroll=8)
    fin_ref[0] = hf
    fin_ref[1] = hb

    def out_chunk(ci, carry):
        rs = pl.ds(pl.multiple_of(ci * RG_CHUNK, RG_CHUNK), RG_CHUNK)
        y_ref[rs, :] = (b_f[rs, :] + b_b[rs, :]) * jax.nn.gelu(gate_ref[rs, :])
        return carry

    lax.fori_loop(0, t_tot // RG_CHUNK, out_chunk, 0)


def _rg_gate_weights(wa, wx):
    per_step = RG_CB // RG_BLOCK
    steps = D_RNN // RG_CB
    mats = []
    for d in range(2):
        for w in (wa[d], wx[d]):
            w = w.reshape(steps, per_step, RG_BLOCK, RG_BLOCK)
            eye = jnp.eye(per_step, dtype=w.dtype)
            m = jnp.einsum('spde,pq->spdqe', w, eye).reshape(steps, RG_CB, RG_CB)
            mats.append(m)
    return jnp.concatenate(mats, axis=-1).astype(BF16)


def rglru_block(u, nseq, seq_len, conv_w, conv_b, wa, ba, wx, bx, lam, h0):
    t = u.shape[0]
    c = RG_CB
    steps = D_RNN // c
    wg = _rg_gate_weights(wa, wx)
    bg = jnp.stack([ba[0], bx[0], ba[1], bx[1]], axis=0).reshape(4, steps, c)
    bg = bg.transpose(1, 0, 2).reshape(steps, 1, 4 * c)
    y, fin = pl.pallas_call(
        functools.partial(_rglru_kernel, nseq, seq_len),
        grid=(steps,),
        in_specs=[pl.BlockSpec((t, c), lambda j: (0, j)),
                  pl.BlockSpec((t, c), lambda j: (0, steps + j)),
                  pl.BlockSpec((conv_w.shape[0], c), lambda j: (0, j)),
                  pl.BlockSpec((1, c), lambda j: (0, j)),
                  pl.BlockSpec((1, c, 4 * c), lambda j: (j, 0, 0)),
                  pl.BlockSpec((None, 1, 4 * c), lambda j: (j, 0, 0)),
                  pl.BlockSpec((2, c), lambda j: (0, j)),
                  pl.BlockSpec((2, nseq, c), lambda j: (0, 0, j))],
        out_specs=[pl.BlockSpec((t, c), lambda j: (0, j)),
                   pl.BlockSpec((2, nseq, c), lambda j: (0, 0, j))],
        out_shape=[jax.ShapeDtypeStruct((t, D_RNN), F32),
                   jax.ShapeDtypeStruct((2, nseq, D_RNN), F32)],
        scratch_shapes=[pltpu.VMEM((t + 2 * RG_PAD, c), F32)] + [pltpu.VMEM((t, c), F32)] * 4,
        compiler_params=_cparams("arbitrary"),
        name="rglru_block",
    )(u, u, conv_w, conv_b.reshape(1, -1), wg, bg, lam, h0)
    return y, fin


def _router_kernel(x_ref, g_ref, sc_ref, sh_ref, rw_ref, rb_ref, o_ref):
    h = _normmod(x_ref[...], g_ref[...], sc_ref[0], sh_ref[0])
    logits = lax.dot_general(rw_ref[...], h, (((1,), (1,)), ((), ())),
                             precision=lax.Precision.HIGHEST, preferred_element_type=F32)
    scores = jax.nn.sigmoid(logits)
    sel = scores + rb_ref[...]
    row = [sel[e:e + 1, :] for e in range(N_EXPERTS)]
    gs = []
    for g in range(N_GROUPS):
        r = row[g * EXPERTS_PER_GROUP:(g + 1) * EXPERTS_PER_GROUP]
        best_pair = None
        for i in range(EXPERTS_PER_GROUP):
            for j in range(i + 1, EXPERTS_PER_GROUP):
                s = r[i] + r[j]
                best_pair = s if best_pair is None else jnp.maximum(best_pair, s)
        gs.append(best_pair)
    best = jnp.zeros_like(gs[0], dtype=jnp.int32)
    top = gs[0]
    for g in range(1, N_GROUPS):
        better = gs[g] > top
        best = jnp.where(better, g, best)
        top = jnp.where(better, gs[g], top)
    picked = []
    for e in range(N_EXPERTS):
        g = e // EXPERTS_PER_GROUP
        rank = jnp.zeros_like(best)
        for o in range(g * EXPERTS_PER_GROUP, (g + 1) * EXPERTS_PER_GROUP):
            if o == e:
                continue
            ahead = (row[o] > row[e]) | ((row[o] == row[e]) & (o < e))
            rank = rank + ahead.astype(jnp.int32)
        picked.append((best == g) & (rank < 2))
    den = jnp.zeros_like(gs[0])
    for e in range(N_EXPERTS):
        den = den + jnp.where(picked[e], scores[e:e + 1, :], 0.0)
    gate = [jnp.where(picked[e], scores[e:e + 1, :] / den, 0.0) for e in range(N_EXPERTS)]
    cls = jnp.zeros_like(den)
    w_a = jnp.zeros_like(den)
    w_b = jnp.zeros_like(den)
    for g in range(N_GROUPS):
        for pi, (a, b) in enumerate(MOE_PAIRS):
            ea, eb = g * EXPERTS_PER_GROUP + a, g * EXPERTS_PER_GROUP + b
            both = picked[ea] & picked[eb]
            cls = jnp.where(both, float(g * len(MOE_PAIRS) + pi), cls)
            w_a = jnp.where(both, gate[ea], w_a)
            w_b = jnp.where(both, gate[eb], w_b)
    o_ref[...] = jnp.concatenate([cls, w_a, w_b, jnp.zeros((ROUTE_ROWS - 3, cls.shape[1]), F32)], axis=0)


MOE_PAIRS = ((0, 1), (0, 2), (0, 3), (1, 3), (1, 2), (2, 3))
N_CLS = N_GROUPS * len(MOE_PAIRS)
CLS_PAD = 32
ROUTE_ROWS = 8
MOE_TS = 256
MOE_TM = 256
SLOT_BLK = 512
ROW_W = D_MODEL + LANES


def router_gates(x, g, mod, router_w, router_b, seq_len, per_seq, tm=512):
    t, d = x.shape
    return pl.pallas_call(
        _router_kernel,
        grid=(t // tm,),
        in_specs=[pl.BlockSpec((tm, d), lambda i: (i, 0)),
                  pl.BlockSpec((1, d), lambda i: (0, 0)),
                  _mod_spec(4, tm, seq_len, per_seq),
                  _mod_spec(3, tm, seq_len, per_seq),
                  pl.BlockSpec((N_EXPERTS, d), lambda i: (0, 0)),
                  pl.BlockSpec((N_EXPERTS, 1), lambda i: (0, 0))],
        out_specs=pl.BlockSpec((ROUTE_ROWS, tm), lambda i: (0, i)),
        out_shape=jax.ShapeDtypeStruct((ROUTE_ROWS, t), F32),
        compiler_params=_cparams("arbitrary"),
        name="moe_router",
    )(x, g.reshape(1, d), mod, mod, router_w.T, router_b.reshape(N_EXPERTS, 1))


def _slots_kernel(route_ref, slot_ref, off_ref, cnt_ref, cnt_s, off_s, run_s):
    phase = pl.program_id(0)
    j = pl.program_id(1)
    cls = route_ref[0:1, :]
    cid = lax.broadcasted_iota(jnp.int32, (CLS_PAD, SLOT_BLK), 0).astype(F32)
    member = (cid == cls).astype(F32)
    n_here = jnp.sum(member, axis=1, keepdims=True)

    @pl.when((phase == 0) & (j == 0))
    def _():
        cnt_s[...] = jnp.zeros_like(cnt_s)

    @pl.when(phase == 0)
    def _():
        cnt_s[...] += n_here

    @pl.when((phase == 1) & (j == 0))
    def _():
        padded = jnp.ceil(cnt_s[...] * (1.0 / MOE_TS)) * MOE_TS
        r = lax.broadcasted_iota(jnp.int32, (CLS_PAD, CLS_PAD), 0)
        c = lax.broadcasted_iota(jnp.int32, (CLS_PAD, CLS_PAD), 1)
        before = (c < r).astype(F32)
        off_s[...] = jnp.dot(before, padded, precision=lax.Precision.HIGHEST, preferred_element_type=F32)
        run_s[...] = jnp.zeros_like(run_s)
        off_ref[...] = off_s[...]
        cnt_ref[...] = cnt_s[...]

    @pl.when(phase == 1)
    def _():
        tr = lax.broadcasted_iota(jnp.int32, (SLOT_BLK, SLOT_BLK), 0)
        tc = lax.broadcasted_iota(jnp.int32, (SLOT_BLK, SLOT_BLK), 1)
        earlier = (tr < tc).astype(BF16)
        rank = _dot(member.astype(BF16), earlier)
        base = run_s[:, 0:1] + off_s[:, 0:1]
        slot = jnp.sum(member * (rank + base), axis=0, keepdims=True)
        slot_ref[...] = slot.astype(jnp.int32)
        run_s[...] += n_here


def moe_slots(route):
    t = route.shape[1]
    n_blk = t // SLOT_BLK
    stat = jax.ShapeDtypeStruct((CLS_PAD, LANES), F32)
    stat_spec = pl.BlockSpec((CLS_PAD, LANES), lambda p, j: (0, 0))
    return pl.pallas_call(
        _slots_kernel,
        grid=(2, n_blk),
        in_specs=[pl.BlockSpec((ROUTE_ROWS, SLOT_BLK), lambda p, j: (0, j))],
        out_specs=[pl.BlockSpec((1, SLOT_BLK), lambda p, j: (0, j * p)), stat_spec, stat_spec],
        out_shape=[jax.ShapeDtypeStruct((1, t), jnp.int32), stat, stat],
        scratch_shapes=[pltpu.VMEM((CLS_PAD, LANES), F32)] * 3,
        compiler_params=_cparams("arbitrary", "arbitrary"),
        name="moe_slots",
    )(route)


def _tile_maps(off, cnt, n_tiles):
    off = off[:N_CLS, 0].astype(jnp.int32)
    cnt = cnt[:N_CLS, 0].astype(jnp.int32)
    ends = off + ((cnt + MOE_TS - 1) // MOE_TS) * MOE_TS
    n_used = ends[-1] // MOE_TS
    k = jnp.arange(n_tiles, dtype=jnp.int32)
    tix = jnp.minimum(k, n_used - 1)
    cls = jnp.sum((tix[:, None] * MOE_TS >= ends[None, :]).astype(jnp.int32), axis=1)
    pair = jnp.asarray(MOE_PAIRS, jnp.int32)
    grp = (cls // len(MOE_PAIRS)) * EXPERTS_PER_GROUP
    ea = grp + pair[cls % len(MOE_PAIRS), 0]
    eb = grp + pair[cls % len(MOE_PAIRS), 1]
    first = jnp.ones((1,), jnp.int32)
    chg_a = jnp.concatenate([first, (ea[1:] != ea[:-1]).astype(jnp.int32)])
    chg_b = jnp.concatenate([first, (eb[1:] != eb[:-1]).astype(jnp.int32)])
    return ea, eb, tix, chg_a, chg_b, n_used.reshape(1)


def _dispatch_kernel(n_steps, slots_ref, x_ref, g_ref, sc_ref, sh_ref, rt_ref, hs_in, hs_out, rowbuf, sem):
    del hs_in
    i = pl.program_id(0)
    cur = i % 2

    def wait_rows(s):
        pltpu.make_async_copy(rowbuf.at[s], rowbuf.at[s], sem.at[s]).wait()

    @pl.when(i >= 2)
    def _():
        wait_rows(cur)

    rowbuf[cur, :, 0:D_MODEL] = _normmod(x_ref[...], g_ref[...], sc_ref[0], sh_ref[0])
    rowbuf[cur, :, D_MODEL:ROW_W] = jnp.concatenate(
        [rt_ref[...], jnp.zeros((MOE_TM, LANES - ROUTE_ROWS), F32)], axis=1)

    def issue(r, carry):
        dst = slots_ref[i * MOE_TM + r]
        pltpu.make_async_copy(rowbuf.at[cur, r], hs_out.at[dst], sem.at[cur]).start()
        return carry

    lax.fori_loop(0, MOE_TM, issue, 0, unroll=8)

    @pl.when(i == n_steps - 1)
    def _():
        wait_rows(cur)
        if n_steps >= 2:
            wait_rows(1 - cur)


def moe_dispatch(x, g, mod, route_t, slots, hs, seq_len, per_seq):
    t, d = x.shape
    n_steps = t // MOE_TM
    grid_spec = pltpu.PrefetchScalarGridSpec(
        num_scalar_prefetch=1,
        grid=(n_steps,),
        in_specs=[pl.BlockSpec((MOE_TM, d), lambda i, s: (i, 0)),
                  pl.BlockSpec((1, d), lambda i, s: (0, 0)),
                  _mod_spec(4, MOE_TM, seq_len, per_seq),
                  _mod_spec(3, MOE_TM, seq_len, per_seq),
                  pl.BlockSpec((MOE_TM, ROUTE_ROWS), lambda i, s: (i, 0)),
                  pl.BlockSpec(memory_space=pl.ANY)],
        out_specs=pl.BlockSpec(memory_space=pl.ANY),
        scratch_shapes=[pltpu.VMEM((2, MOE_TM, ROW_W), F32), pltpu.SemaphoreType.DMA((2,))],
    )
    return pl.pallas_call(
        functools.partial(_dispatch_kernel, n_steps),
        grid_spec=grid_spec,
        out_shape=jax.ShapeDtypeStruct(hs.shape, F32),
        input_output_aliases={6: 0},
        compiler_params=_cparams("arbitrary"),
        name="moe_dispatch",
    )(slots, x, g.reshape(1, d), mod, mod, route_t, hs)


def _experts_kernel(ea_ref, eb_ref, tix_ref, chga_ref, chgb_ref, nused_ref, hs_ref,
                    wga_ref, wua_ref, wda_ref, wgb_ref, wub_ref, wdb_ref, ys_ref,
                    bga, bua, bda, bgb, bub, bdb):
    k = pl.program_id(0)

    @pl.when(k < nused_ref[0])
    def _():
        @pl.when(chga_ref[k] == 1)
        def _():
            bga[...] = wga_ref[0].astype(BF16)
            bua[...] = wua_ref[0].astype(BF16)
            bda[...] = wda_ref[0].astype(BF16)

        @pl.when(chgb_ref[k] == 1)
        def _():
            bgb[...] = wgb_ref[0].astype(BF16)
            bub[...] = wub_ref[0].astype(BF16)
            bdb[...] = wdb_ref[0].astype(BF16)

        h = hs_ref[:, 0:D_MODEL].astype(BF16)

        def ffn(bg, bu, bd, w):
            hid = _dot(h, bg[...])
            up = _dot(h, bu[...])
            act = (hid * _sigmoid(hid)) * up * w
            return _dot(act.astype(BF16), bd[...])

        ys_ref[...] = (ffn(bga, bua, bda, hs_ref[:, D_MODEL + 1:D_MODEL + 2])
                       + ffn(bgb, bub, bdb, hs_ref[:, D_MODEL + 2:D_MODEL + 3]))

    @pl.when(k >= nused_ref[0])
    def _():
        ys_ref[...] = jnp.zeros_like(ys_ref)


def moe_experts(hs, maps, layer, w_gate, w_up, w_down):
    n_tiles = hs.shape[0] // MOE_TS
    d = D_MODEL

    def w_spec(shape, which):
        return pl.BlockSpec((None, 1) + shape, lambda k, ea, eb, *_: (layer, (ea, eb)[which][k], 0, 0))

    grid_spec = pltpu.PrefetchScalarGridSpec(
        num_scalar_prefetch=6,
        grid=(n_tiles,),
        in_specs=[pl.BlockSpec((MOE_TS, ROW_W), lambda k, ea, eb, tix, *_: (tix[k], 0)),
                  w_spec((d, D_EXPERT), 0), w_spec((d, D_EXPERT), 0), w_spec((D_EXPERT, d), 0),
                  w_spec((d, D_EXPERT), 1), w_spec((d, D_EXPERT), 1), w_spec((D_EXPERT, d), 1)],
        out_specs=pl.BlockSpec((MOE_TS, d), lambda k, *_: (k, 0)),
        scratch_shapes=[pltpu.VMEM((d, D_EXPERT), BF16), pltpu.VMEM((d, D_EXPERT), BF16),
                        pltpu.VMEM((D_EXPERT, d), BF16)] * 2,
    )
    return pl.pallas_call(
        _experts_kernel,
        grid_spec=grid_spec,
        out_shape=jax.ShapeDtypeStruct((hs.shape[0], d), F32),
        compiler_params=_cparams("arbitrary"),
        name="moe_experts",
    )(*maps, hs, w_gate, w_up, w_down, w_gate, w_up, w_down)


def _combine_kernel(final, n_steps, slots_ref, x_ref, g2_ref, fg_ref, ys_hbm, o_ref, gbuf, sem):
    i = pl.program_id(0)
    cur = i % 2

    def issue_tile(tile, s):
        def issue(r, carry):
            src = slots_ref[tile * MOE_TM + r]
            pltpu.make_async_copy(ys_hbm.at[src], gbuf.at[s, r], sem.at[s]).start()
            return carry
        lax.fori_loop(0, MOE_TM, issue, 0, unroll=8)

    @pl.when(i == 0)
    def _():
        issue_tile(0, 0)

    @pl.when(i + 1 < n_steps)
    def _():
        issue_tile(i + 1, 1 - cur)

    pltpu.make_async_copy(gbuf.at[cur], gbuf.at[cur], sem.at[cur]).wait()
    y = x_ref[...] + g2_ref[0] * gbuf[cur]
    if final:
        ms = jnp.mean(y * y, axis=-1, keepdims=True)
        y = y * lax.rsqrt(ms + EPS) * fg_ref[...]
    o_ref[...] = y


def moe_combine(x, mod, slots, ys, final_g, final, seq_len, per_seq):
    t, d = x.shape
    n_steps = t // MOE_TM
    grid_spec = pltpu.PrefetchScalarGridSpec(
        num_scalar_prefetch=1,
        grid=(n_steps,),
        in_specs=[pl.BlockSpec((MOE_TM, d), lambda i, s: (i, 0)),
                  _mod_spec(5, MOE_TM, seq_len, per_seq),
                  pl.BlockSpec((1, d), lambda i, s: (0, 0)),
                  pl.BlockSpec(memory_space=pl.ANY)],
        out_specs=pl.BlockSpec((MOE_TM, d), lambda i, s: (i, 0)),
        scratch_shapes=[pltpu.VMEM((2, MOE_TM, d), F32), pltpu.SemaphoreType.DMA((2,))],
    )
    return pl.pallas_call(
        functools.partial(_combine_kernel, final, n_steps),
        grid_spec=grid_spec,
        out_shape=jax.ShapeDtypeStruct((t, d), F32),
        compiler_params=_cparams("arbitrary"),
        name="moe_combine",
    )(slots, x, mod, final_g.reshape(1, d), ys)


def moe_block(xs, mods_l, per_seqs, seq_lens, layer, p, final):
    g = p['norm_g'][layer, 1]
    routes = [router_gates(x, g, mods_l, p['router_w'], p['router_b'], sl, ps)
              for x, sl, ps in zip(xs, seq_lens, per_seqs)]
    slots, off, cnt = moe_slots(jnp.concatenate(routes, axis=1))
    t_all = slots.shape[1]
    n_tiles = t_all // MOE_TS + N_CLS
    maps = _tile_maps(off, cnt, n_tiles)
    hs = jnp.zeros((n_tiles * MOE_TS, ROW_W), F32)
    bounds = np.cumsum([0] + [x.shape[0] for x in xs])
    group_slots = [slots[0, bounds[i]:bounds[i + 1]] for i in range(len(xs))]
    for x, r, s, sl, ps in zip(xs, routes, group_slots, seq_lens, per_seqs):
        hs = moe_dispatch(x, g, mods_l, r.T, s, hs, sl, ps)
    ys = moe_experts(hs, maps, layer, p['moe_w_gate'], p['moe_w_up'], p['moe_w_down'])
    return [moe_combine(x, mods_l, s, ys, p['final_g'], final, sl, ps)
            for x, s, sl, ps in zip(xs, group_slots, seq_lens, per_seqs)]


def _mixer(x, group, l, mod, p, hy_tables):
    per_seq, nseq, seq_len = group['per_seq'], group['nseq'], group['seq_len']
    extras = None
    if l % 2 == 0:
        e = l // 2
        u = nm_matmul(x, p['norm_g'][l, 0], mod, p['a_in_w'][e].astype(BF16), seq_len, per_seq)
        if group['ctx_k'] is None:
            attn, nk, nv = context_attention(u, nseq, seq_len)
            extras = (nk, nv)
        else:
            attn = neighbourhood_attention(u, group['ctx_k'][:, e], group['ctx_v'][:, e], p['na_rpb'][e],
                                           nseq, seq_len)
        fwd, inv = hy_tables[seq_len]
        spectrum = hyena_spectrum(seq_len, p['hy_w1'][e], p['hy_b1'][e], p['hy_w2'][e], p['hy_b2'][e],
                                  p['hy_w3'][e], p['hy_freq'][e], p['hy_d'][e], fwd)
        hy = hyena_mixer(u, nseq, seq_len, p['hy_short_w'][e], p['hy_short_b'][e], spectrum, fwd, inv)
        w_out = p['a_out_w'][e].astype(BF16)
        x = proj_residual([attn, hy], [w_out[:D_A], w_out[D_A:]], x, mod, seq_len, per_seq)
    else:
        o = l // 2
        u = nm_matmul(x, p['norm_g'][l, 0], mod, p['c_in_w'][o].astype(BF16), seq_len, per_seq)
        y, extras = rglru_block(u, nseq, seq_len, p['rg_conv_w'][o], p['rg_conv_b'][o], p['rg_wa'][o],
                                p['rg_ba'][o], p['rg_wx'][o], p['rg_bx'][o], p['rg_lam'][o], group['h0'][o])
        x = proj_residual([y], [p['c_out_w'][o].astype(BF16)], x, mod, seq_len, per_seq)
    return x, extras


def kernel(x_prompt, x_sample, cache_k, cache_v, state_h, c, c_ctx, norm_g, ada_w, ada_b, final_g, a_in_w, a_out_w, na_rpb, hy_short_w, hy_short_b, hy_w1, hy_b1, hy_w2, hy_b2, hy_w3, hy_freq, hy_d, c_in_w, c_out_w, rg_conv_w, rg_conv_b, rg_wa, rg_ba, rg_wx, rg_bx, rg_lam, router_w, router_b, moe_w_gate, moe_w_up, moe_w_down):
    p = dict(norm_g=norm_g, final_g=final_g, a_in_w=a_in_w, a_out_w=a_out_w, na_rpb=na_rpb,
             hy_short_w=hy_short_w, hy_short_b=hy_short_b, hy_w1=hy_w1, hy_b1=hy_b1, hy_w2=hy_w2,
             hy_b2=hy_b2, hy_w3=hy_w3, hy_freq=hy_freq, hy_d=hy_d, c_in_w=c_in_w, c_out_w=c_out_w,
             rg_conv_w=rg_conv_w, rg_conv_b=rg_conv_b, rg_wa=rg_wa, rg_ba=rg_ba, rg_wx=rg_wx, rg_bx=rg_bx,
             rg_lam=rg_lam, router_w=router_w, router_b=router_b, moe_w_gate=moe_w_gate,
             moe_w_up=moe_w_up, moe_w_down=moe_w_down)
    batch, seq, d = x_prompt.shape
    dec_batch, dec_seq, _ = x_sample.shape
    n_odd = DEPTH // 2
    assert 1 + dec_batch <= MOD_ROWS

    cond = jnp.concatenate([c_ctx[None, :], c, jnp.zeros((MOD_ROWS - 1 - dec_batch, d), F32)], axis=0)
    m = modulation(cond, ada_w, ada_b)
    mods = [m[l].reshape(MOD_ROWS * N_MOD, 1, d) for l in range(DEPTH)]

    tables = {}
    for sl in (seq, dec_seq):
        fwd, inv = _dft_tables(sl)
        tables[sl] = (jnp.asarray(fwd).astype(BF16), jnp.asarray(inv).astype(BF16))

    groups = [
        dict(per_seq=False, nseq=batch, seq_len=seq, ctx_k=None, ctx_v=None,
             h0=[jnp.zeros((2, batch, D_RNN), F32)] * n_odd),
        dict(per_seq=True, nseq=dec_batch, seq_len=dec_seq, ctx_k=cache_k, ctx_v=cache_v,
             h0=[state_h[:, o].transpose(1, 0, 2) for o in range(n_odd)]),
    ]
    xs = [x_prompt.reshape(batch * seq, d), x_sample.reshape(dec_batch * dec_seq, d)]
    k_list, v_list, h_list = [], [], []
    for l in range(DEPTH):
        mixed = [_mixer(x, grp, l, mods[l], p, tables) for x, grp in zip(xs, groups)]
        if l % 2 == 0:
            k_list.append(mixed[0][1][0])
            v_list.append(mixed[0][1][1])
        else:
            h_list.append(mixed[0][1].transpose(1, 0, 2))
        xs = moe_block([mx[0] for mx in mixed], mods[l], [grp['per_seq'] for grp in groups],
                       [grp['seq_len'] for grp in groups], l, p, l == DEPTH - 1)
    new_k = jnp.stack(k_list, axis=1)
    new_v = jnp.stack(v_list, axis=1)
    new_h = jnp.stack(h_list, axis=1)
    return (xs[0].reshape(batch, seq, d), xs[1].reshape(dec_batch, dec_seq, d), new_k, new_v, new_h)
```

```python
import functools
import math

import numpy as np
import jax
import jax.numpy as jnp
from jax import lax
from jax.experimental import pallas as pl
from jax.experimental.pallas import tpu as pltpu

F32 = jnp.float32
BF16 = jnp.bfloat16

D_MODEL = 1024
DEPTH = 2
GRID_W = 64
EPS = 1e-6
NEG_INF = -1e30
NA_HEADS = 8
HEAD_DIM = 64
D_A = NA_HEADS * HEAD_DIM
WIN_ROWS = 8
WIN_COLS = 16
D_B = D_MODEL - D_A
HY_ORDER = 2
HY_EMB = 33
HY_BANDS = (HY_EMB - 1) // 2
HY_FFN = 64
HY_DECAY_TARGET = 1e-2
HY_FAST_PCT = 0.3
HY_SLOW_PCT = 1.5
D_RNN = D_MODEL
RG_BLOCK = 64
RG_C = 8.0
N_EXPERTS = 16
N_GROUPS = 4
EXPERTS_PER_GROUP = N_EXPERTS // N_GROUPS
D_EXPERT = 512

LANES = 128
VMEM_LIMIT = 56 * 1024 * 1024
N_MOD = 6
MOD_ROWS = 16


def _cparams(*sem):
    return pltpu.CompilerParams(dimension_semantics=sem, vmem_limit_bytes=VMEM_LIMIT)


def _dot(a, b):
    return jnp.dot(a, b, preferred_element_type=F32)


def _dot_nt(a, b):
    return lax.dot_general(a, b, (((1,), (1,)), ((), ())), preferred_element_type=F32)


def _sigmoid(x):
    return 0.5 * jnp.tanh(0.5 * x) + 0.5


def _normmod(x, g, sc, sh):
    ms = jnp.mean(x * x, axis=-1, keepdims=True)
    return (x * lax.rsqrt(ms + EPS) * g) * (1.0 + sc) + sh


def _mod_spec(chunk, tm, seq_len, per_seq):
    if per_seq:
        return pl.BlockSpec((1, 1, D_MODEL), lambda i, *_: ((1 + (i * tm) // seq_len) * N_MOD + chunk, 0, 0))
    return pl.BlockSpec((1, 1, D_MODEL), lambda i, *_: (chunk, 0, 0))


def _mod_kernel(c_ref, w_ref, b_ref, o_ref):
    s = c_ref[...]
    s = s * jax.nn.sigmoid(s)
    o_ref[0] = _dot(s.astype(BF16), w_ref[0].astype(BF16)) + b_ref[0]


def modulation(cond, ada_w, ada_b):
    tn = 1536
    n = ada_w.shape[-1]
    return pl.pallas_call(
        _mod_kernel,
        grid=(DEPTH, n // tn),
        in_specs=[pl.BlockSpec((MOD_ROWS, D_MODEL), lambda l, j: (0, 0)),
                  pl.BlockSpec((1, D_MODEL, tn), lambda l, j: (l, 0, j)),
                  pl.BlockSpec((1, 1, tn), lambda l, j: (l, 0, j))],
        out_specs=pl.BlockSpec((1, MOD_ROWS, tn), lambda l, j: (l, 0, j)),
        out_shape=jax.ShapeDtypeStruct((DEPTH, MOD_ROWS, n), F32),
        compiler_params=_cparams("arbitrary", "arbitrary"),
        name="modulation",
    )(cond, ada_w, ada_b.reshape(DEPTH, 1, n))


def _nm_matmul_kernel(x_ref, g_ref, sc_ref, sh_ref, w_ref, o_ref):
    h = _normmod(x_ref[...], g_ref[...], sc_ref[0], sh_ref[0])
    o_ref[...] = _dot(h.astype(BF16), w_ref[...])


def nm_matmul(x, g, mod, w, seq_len, per_seq, tm=256):
    t, d = x.shape
    n = w.shape[1]
    return pl.pallas_call(
        _nm_matmul_kernel,
        grid=(t // tm,),
        in_specs=[pl.BlockSpec((tm, d), lambda i: (i, 0)),
                  pl.BlockSpec((1, d), lambda i: (0, 0)),
                  _mod_spec(1, tm, seq_len, per_seq),
                  _mod_spec(0, tm, seq_len, per_seq),
                  pl.BlockSpec((d, n), lambda i: (0, 0))],
        out_specs=pl.BlockSpec((tm, n), lambda i: (i, 0)),
        out_shape=jax.ShapeDtypeStruct((t, n), F32),
        compiler_params=_cparams("arbitrary"),
        name="norm_mod_proj",
    )(x, g.reshape(1, d), mod, mod, w)


def _proj_res_kernel(n_act, *refs):
    acts = refs[:n_act]
    ws = refs[n_act:2 * n_act]
    x_ref, g_ref, o_ref = refs[2 * n_act:]
    acc = _dot(acts[0][...].astype(BF16), ws[0][...])
    for a, w in zip(acts[1:], ws[1:]):
        acc += _dot(a[...].astype(BF16), w[...])
    o_ref[...] = x_ref[...] + g_ref[0] * acc


def proj_residual(acts, ws, x, mod, seq_len, per_seq, tm=512):
    t, d = x.shape
    in_specs = [pl.BlockSpec((tm, a.shape[1]), lambda i: (i, 0)) for a in acts]
    in_specs += [pl.BlockSpec(w.shape, lambda i: (0, 0)) for w in ws]
    in_specs += [pl.BlockSpec((tm, d), lambda i: (i, 0)), _mod_spec(2, tm, seq_len, per_seq)]
    return pl.pallas_call(
        functools.partial(_proj_res_kernel, len(acts)),
        grid=(t // tm,),
        in_specs=in_specs,
        out_specs=pl.BlockSpec((tm, d), lambda i: (i, 0)),
        out_shape=jax.ShapeDtypeStruct((t, d), F32),
        compiler_params=_cparams("arbitrary"),
        name="proj_residual",
    )(*acts, *ws, x, mod)


def _row_permutation(nseq, steps, to_time_major):
    n = nseq * steps
    i = lax.broadcasted_iota(jnp.int32, (n, n), 0)
    j = lax.broadcasted_iota(jnp.int32, (n, n), 1)
    if to_time_major:
        src = (i % nseq) * steps + i // nseq
    else:
        src = (i % steps) * nseq + i // steps
    return (j == src).astype(BF16)


def _nm_matmul_tm_kernel(nseq, steps, x_ref, g_ref, sc_ref, sh_ref, w_ref, o_ref):
    h = _normmod(x_ref[...], g_ref[...], sc_ref[...], sh_ref[...])
    h = h.reshape(nseq * steps, h.shape[-1]).astype(BF16)
    h = _dot(_row_permutation(nseq, steps, True), h).astype(BF16)
    o_ref[...] = _dot(h, w_ref[...])


def _group_mod(mod, chunk, nseq, per_seq):
    rows = mod.reshape(MOD_ROWS, N_MOD, 1, D_MODEL)
    return rows[1:1 + nseq, chunk] if per_seq else rows[0:1, chunk]


def nm_matmul_tm(x, g, mod, w, nseq, seq_len, per_seq):
    t, d = x.shape
    n = w.shape[1]
    steps = TM_ROWS // nseq
    sc = _group_mod(mod, 1, nseq, per_seq)
    sh = _group_mod(mod, 0, nseq, per_seq)
    mod_spec = pl.BlockSpec(sc.shape, lambda i: (0, 0, 0))
    return pl.pallas_call(
        functools.partial(_nm_matmul_tm_kernel, nseq, steps),
        grid=(seq_len // steps,),
        in_specs=[pl.BlockSpec((nseq, steps, d), lambda i: (0, i, 0)),
                  pl.BlockSpec((1, d), lambda i: (0, 0)),
                  mod_spec, mod_spec,
                  pl.BlockSpec((d, n), lambda i: (0, 0))],
        out_specs=pl.BlockSpec((TM_ROWS, n), lambda i: (i, 0)),
        out_shape=jax.ShapeDtypeStruct((t, n), F32),
        compiler_params=_cparams("arbitrary"),
        name="norm_mod_proj_tm",
    )(x.reshape(nseq, seq_len, d), g.reshape(1, d), sc, sh, w)


def _proj_res_tm_kernel(nseq, steps, y_ref, w_ref, x_ref, g_ref, o_ref):
    y = _dot(_row_permutation(nseq, steps, False), y_ref[...].astype(BF16)).astype(BF16)
    acc = _dot(y, w_ref[...])
    o_ref[...] = x_ref[...] + g_ref[...] * acc.reshape(nseq, steps, acc.shape[-1])


def proj_residual_tm(y, w, x, mod, nseq, seq_len, per_seq):
    t, d = x.shape
    steps = TM_ROWS // nseq
    g1 = _group_mod(mod, 2, nseq, per_seq)
    out = pl.pallas_call(
        functools.partial(_proj_res_tm_kernel, nseq, steps),
        grid=(seq_len // steps,),
        in_specs=[pl.BlockSpec((TM_ROWS, y.shape[1]), lambda i: (i, 0)),
                  pl.BlockSpec(w.shape, lambda i: (0, 0)),
                  pl.BlockSpec((nseq, steps, d), lambda i: (0, i, 0)),
                  pl.BlockSpec(g1.shape, lambda i: (0, 0, 0))],
        out_specs=pl.BlockSpec((nseq, steps, d), lambda i: (0, i, 0)),
        out_shape=jax.ShapeDtypeStruct((nseq, seq_len, d), F32),
        compiler_params=_cparams("arbitrary"),
        name="proj_residual_tm",
    )(y, w, x.reshape(nseq, seq_len, d), g1)
    return out.reshape(t, d)


def _ctx_attn_kernel(q_ref, k_ref, v_ref, o_ref, nk_ref, nv_ref):
    scale = HEAD_DIM ** -0.5
    for h in range(NA_HEADS):
        sl = slice(h * HEAD_DIM, (h + 1) * HEAD_DIM)
        q = q_ref[:, sl]
        k = k_ref[:, sl]
        v = v_ref[:, sl]
        nk_ref[0, h] = k
        nv_ref[0, h] = v
        s = _dot_nt(q.astype(BF16), k.astype(BF16)) * scale
        m = jnp.max(s, axis=-1, keepdims=True)
        p = jnp.exp(s - m)
        den = jnp.sum(p, axis=-1, keepdims=True)
        o_ref[:, sl] = _dot(p.astype(BF16), v.astype(BF16)) / den


def context_attention(u, nseq, seq_len):
    t = u.shape[0]
    kv_shape = jax.ShapeDtypeStruct((nseq, NA_HEADS, seq_len, HEAD_DIM), F32)
    kv_spec = pl.BlockSpec((1, NA_HEADS, seq_len, HEAD_DIM), lambda b: (b, 0, 0, 0))
    return pl.pallas_call(
        _ctx_attn_kernel,
        grid=(nseq,),
        in_specs=[pl.BlockSpec((seq_len, D_A), lambda b: (b, 0)),
                  pl.BlockSpec((seq_len, D_A), lambda b: (b, 1)),
                  pl.BlockSpec((seq_len, D_A), lambda b: (b, 2))],
        out_specs=[pl.BlockSpec((seq_len, D_A), lambda b: (b, 0)), kv_spec, kv_spec],
        out_shape=[jax.ShapeDtypeStruct((t, D_A), F32), kv_shape, kv_shape],
        compiler_params=_cparams("arbitrary"),
        name="context_attention",
    )(u, u, u)


N_DR = 2 * WIN_ROWS - 1
N_DC = 2 * WIN_COLS - 1


def _na_col_tables():
    cols = np.arange(GRID_W)
    col_start = np.clip(cols - WIN_COLS // 2, 0, GRID_W - WIN_COLS)
    col_in = (cols[None, :] >= col_start[:, None]) & (cols[None, :] < col_start[:, None] + WIN_COLS)
    dc = np.clip(cols[None, :] - cols[:, None], 1 - WIN_COLS, WIN_COLS - 1) + WIN_COLS - 1
    onehot = (dc.reshape(1, -1) == np.arange(32)[:, None]).astype(np.float32)
    return onehot, col_in.reshape(1, -1).astype(np.float32)


def _na_bias_kernel(r_ref, e_ref, m_ref, o_ref):
    t = jnp.dot(r_ref[...], e_ref[...], precision=lax.Precision.HIGHEST, preferred_element_type=F32)
    o_ref[...] = jnp.where(m_ref[...] > 0.0, t, NEG_INF)


def na_bias_table(rpb):
    onehot, col_in = _na_col_tables()
    n_rows = NA_HEADS * N_DR
    r = jnp.zeros((LANES, 32), F32).at[:n_rows, :N_DC].set(rpb.reshape(n_rows, N_DC).astype(F32))
    t = pl.pallas_call(
        _na_bias_kernel,
        out_shape=jax.ShapeDtypeStruct((LANES, GRID_W * GRID_W), F32),
        name="na_bias_table",
    )(r, jnp.asarray(onehot), jnp.asarray(col_in))
    t = t[:n_rows].reshape(NA_HEADS, N_DR, GRID_W, GRID_W)
    return jnp.concatenate([t[:, :-1], t[:, 1:]], axis=-1)


def _na_kernel(rows, q_ref, k_ref, v_ref, ck_ref, cv_ref, bias_ref, o_ref,
               q_s, k_s, v_s, ck_s, cv_s, s_s, p_s, den_s, o_s):
    scale = HEAD_DIM ** -0.5
    n_lat = WIN_ROWS * GRID_W
    for h in range(NA_HEADS):
        sl = slice(h * HEAD_DIM, (h + 1) * HEAD_DIM)
        q_s[h] = (q_ref[:, sl] * scale).astype(BF16)
        k_s[h] = k_ref[:, sl].astype(BF16)
        v_s[h] = v_ref[:, sl].astype(BF16)
    ck_s[...] = ck_ref[0].astype(BF16)
    cv_s[...] = cv_ref[0].astype(BF16)

    def window(r):
        start = min(max(r - WIN_ROWS // 2, 0), rows - WIN_ROWS)
        return start, start - r + WIN_ROWS - 1

    def head_body(h, carry):
        for r in range(rows):
            start, off = window(r)
            q = q_s[h, r * GRID_W:(r + 1) * GRID_W, :]
            bias = jnp.concatenate([bias_ref[h, off + 2 * i] for i in range(WIN_ROWS // 2)], axis=1)
            s_s[r * GRID_W:(r + 1) * GRID_W, 0:n_lat] = (
                _dot_nt(q, k_s[h, start * GRID_W:start * GRID_W + n_lat, :]) + bias)
            s_s[r * GRID_W:(r + 1) * GRID_W, n_lat:] = _dot_nt(q, ck_s[h])
        for r in range(rows):
            rs = slice(r * GRID_W, (r + 1) * GRID_W)
            s = s_s[rs, :]
            p = jnp.exp(s - jnp.max(s, axis=-1, keepdims=True))
            den_s[rs, :] = jnp.sum(p, axis=-1, keepdims=True)
            p_s[rs, :] = p.astype(BF16)
        for r in range(rows):
            start, _ = window(r)
            rs = slice(r * GRID_W, (r + 1) * GRID_W)
            o = (_dot(p_s[rs, 0:n_lat], v_s[h, start * GRID_W:start * GRID_W + n_lat, :])
                 + _dot(p_s[rs, n_lat:], cv_s[h]))
            o_s[h, rs, :] = o / den_s[rs, :]
        return carry

    lax.fori_loop(0, NA_HEADS, head_body, 0)
    for h in range(NA_HEADS):
        o_ref[:, h * HEAD_DIM:(h + 1) * HEAD_DIM] = o_s[h]


def neighbourhood_attention(u, ctx_k, ctx_v, rpb, nseq, seq_len):
    t = u.shape[0]
    rows = seq_len // GRID_W
    assert rows >= WIN_ROWS and WIN_ROWS % 2 == 0
    past = ctx_k.shape[2]
    bias = na_bias_table(rpb)
    ctx_spec = pl.BlockSpec((1, NA_HEADS, past, HEAD_DIM), lambda b: (b, 0, 0, 0))
    return pl.pallas_call(
        functools.partial(_na_kernel, rows),
        grid=(nseq,),
        in_specs=[pl.BlockSpec((seq_len, D_A), lambda b: (b, 0)),
                  pl.BlockSpec((seq_len, D_A), lambda b: (b, 1)),
                  pl.BlockSpec((seq_len, D_A), lambda b: (b, 2)),
                  ctx_spec, ctx_spec,
                  pl.BlockSpec(bias.shape, lambda b: (0, 0, 0, 0))],
        out_specs=pl.BlockSpec((seq_len, D_A), lambda b: (b, 0)),
        out_shape=jax.ShapeDtypeStruct((t, D_A), F32),
        scratch_shapes=[pltpu.VMEM((NA_HEADS, seq_len, HEAD_DIM), BF16)] * 3
        + [pltpu.VMEM((NA_HEADS, past, HEAD_DIM), BF16)] * 2
        + [pltpu.VMEM((seq_len, WIN_ROWS * GRID_W + past), F32),
           pltpu.VMEM((seq_len, WIN_ROWS * GRID_W + past), BF16),
           pltpu.VMEM((seq_len, 1), F32),
           pltpu.VMEM((NA_HEADS, seq_len, HEAD_DIM), F32)],
        compiler_params=_cparams("arbitrary"),
        name="neighbourhood_attention",
    )(u, u, u, ctx_k, ctx_v, bias)


def _dft_tables(seq_len):
    n = 2 * seq_len
    f = np.arange(seq_len, dtype=np.int64)
    ang = (np.outer(f, f) % n).astype(np.float64) * (math.pi / seq_len)
    cos, sin = np.cos(ang), np.sin(ang)
    alt = np.where(f % 2 == 0, 1.0, -1.0)
    s_fwd = -sin
    s_fwd[0, :] = alt
    fwd = np.concatenate([cos, s_fwd], axis=0)
    wf = np.where(f == 0, 1.0, 2.0) / n
    ci = cos.T * wf[None, :]
    si = -sin.T * wf[None, :]
    si[:, 0] = alt / n
    inv = np.concatenate([ci, si], axis=1)
    return fwd.astype(np.float32), inv.astype(np.float32)


def _hyena_feats(seq_len):
    t = np.linspace(0.0, 1.0, seq_len, dtype=np.float32)[:, None]
    w = (2.0 * math.pi * np.arange(seq_len, dtype=np.float32)[:, None] / seq_len).astype(np.float32)
    f = np.linspace(1e-4, HY_BANDS - 1, HY_BANDS, dtype=np.float32)[None, :]
    z = np.concatenate([t, np.cos(f * w), -np.sin(f * w)], axis=-1).astype(np.float32)
    max_decay = math.log(HY_DECAY_TARGET) / HY_FAST_PCT
    min_decay = math.log(HY_DECAY_TARGET) / HY_SLOW_PCT
    deltas = np.abs(np.linspace(min_decay, max_decay, D_B, dtype=np.float32))[None, :]
    return z, t, deltas


def _hy_filter_kernel(seq_len, z_ref, t_ref, dl_ref, w1_ref, b1_ref, w2_ref, b2_ref, w3_ref, fr_ref,
                      d_ref, fwd_ref, g_ref):
    hp = lax.Precision.HIGHEST
    h = jnp.sin(fr_ref[0:1, :] * (jnp.dot(z_ref[...], w1_ref[...], precision=hp) + b1_ref[...]))
    h = jnp.sin(fr_ref[1:2, :] * (jnp.dot(h, w2_ref[...], precision=hp) + b2_ref[...]))
    h = jnp.dot(h, w3_ref[...], precision=hp)
    decay = jnp.exp(-t_ref[...] * dl_ref[...])
    row0 = lax.broadcasted_iota(jnp.int32, (seq_len, D_B), 0) == 0
    sums, diffs = [], []
    for n in range(HY_ORDER):
        hf = h[:, (2 * n) * D_B:(2 * n + 1) * D_B] * decay
        hb = h[:, (2 * n + 1) * D_B:(2 * n + 2) * D_B] * decay
        gp = jnp.where(row0, hf + hb + d_ref[n:n + 1, :], hf)
        gm = jnp.where(row0, 0.0, hb)
        sums.append(gp + gm)
        diffs.append(gp - gm)
    rhs = jnp.concatenate(sums + diffs, axis=1).astype(BF16)
    spec = _dot(fwd_ref[...], rhs)
    for n in range(HY_ORDER):
        a = spec[:, n * D_B:(n + 1) * D_B]
        b = spec[:, (HY_ORDER + n) * D_B:(HY_ORDER + n + 1) * D_B]
        g_ref[n, 0:seq_len, :] = a[0:seq_len]
        g_ref[n, seq_len:, :] = jnp.where(row0, a[seq_len:], b[seq_len:])


def hyena_spectrum(seq_len, w1, b1, w2, b2, w3, freq, d, fwd):
    z, t, deltas = _hyena_feats(seq_len)
    return pl.pallas_call(
        functools.partial(_hy_filter_kernel, seq_len),
        out_shape=jax.ShapeDtypeStruct((HY_ORDER, 2 * seq_len, D_B), F32),
        compiler_params=pltpu.CompilerParams(vmem_limit_bytes=VMEM_LIMIT),
        name="hyena_spectrum",
    )(jnp.asarray(z), jnp.asarray(t), jnp.asarray(deltas), w1, b1.reshape(1, -1), w2, b2.reshape(1, -1),
      w3, freq, d, fwd)


def _hyena_kernel(seq_len, u_ref, sw_ref, sb_ref, g_ref, fwd_ref, inv_ref, o_ref):
    u = u_ref[...]
    t_idx = lax.broadcasted_iota(jnp.int32, u.shape, 0)
    prev = jnp.where(t_idx == 0, 0.0, pltpu.roll(u, 1, axis=0))
    nxt = jnp.where(t_idx == seq_len - 1, 0.0, pltpu.roll(u, seq_len - 1, axis=0))
    u = prev * sw_ref[0:1, :] + u * sw_ref[1:2, :] + nxt * sw_ref[2:3, :] + sb_ref[...]
    row0 = lax.broadcasted_iota(jnp.int32, (seq_len, D_B), 0) == 0
    z = u[:, 0:D_B]
    for n in range(HY_ORDER):
        spec = _dot(fwd_ref[...], z.astype(BF16))
        ure, uim = spec[0:seq_len], spec[seq_len:]
        gre, gim = g_ref[n, 0:seq_len, :], g_ref[n, seq_len:, :]
        pim = uim * gim
        yre = ure * gre - jnp.where(row0, 0.0, pim)
        yim = jnp.where(row0, pim, ure * gim + uim * gre)
        y = jnp.concatenate([yre, yim], axis=0).astype(BF16)
        z = u[:, (n + 1) * D_B:(n + 2) * D_B] * _dot(inv_ref[...], y)
    o_ref[...] = z


def hyena_mixer(u, nseq, seq_len, short_w, short_b, spectrum, fwd, inv):
    t = u.shape[0]
    width = (HY_ORDER + 1) * D_B
    col_block = (3 * D_A) // width
    assert col_block * width == 3 * D_A
    return pl.pallas_call(
        functools.partial(_hyena_kernel, seq_len),
        grid=(nseq,),
        in_specs=[pl.BlockSpec((seq_len, width), lambda b: (b, col_block)),
                  pl.BlockSpec(short_w.shape, lambda b: (0, 0)),
                  pl.BlockSpec((1, width), lambda b: (0, 0)),
                  pl.BlockSpec(spectrum.shape, lambda b: (0, 0, 0)),
                  pl.BlockSpec(fwd.shape, lambda b: (0, 0)),
                  pl.BlockSpec(inv.shape, lambda b: (0, 0))],
        out_specs=pl.BlockSpec((seq_len, D_B), lambda b: (b, 0)),
        out_shape=jax.ShapeDtypeStruct((t, D_B), F32),
        compiler_params=_cparams("arbitrary"),
        name="hyena_mixer",
    )(u, short_w, short_b.reshape(1, width), spectrum, fwd, inv)


RG_CB = LANES
RG_CHUNK = 512
TM_ROWS = 256


def _rglru_kernel(nseq, seq_len, gate_ref, xr_ref, cw_ref, cb_ref, wg_ref, bg_ref, lam_ref, h0_ref,
                  y_ref, fin_ref, xp_ref, a_f, b_f, a_b, b_b):
    t_tot = nseq * seq_len
    c = RG_CB
    pad = 2 * nseq
    xp_ref[0:pad, :] = jnp.zeros((pad, c), F32)
    xp_ref[pad + t_tot:, :] = jnp.zeros((pad, c), F32)
    xp_ref[pad:pad + t_tot, :] = xr_ref[...]
    nl = -lam_ref[...]
    sp = jnp.maximum(nl, 0.0) + jnp.log1p(jnp.exp(-jnp.abs(nl)))

    def gate_chunk(ci, carry):
        r0 = pl.multiple_of(ci * RG_CHUNK, RG_CHUNK)
        xc = xp_ref[pl.ds(r0, RG_CHUNK), :] * cw_ref[0:1, :]
        for j in range(1, cw_ref.shape[0]):
            xc = xc + xp_ref[pl.ds(r0 + j * nseq, RG_CHUNK), :] * cw_ref[j:j + 1, :]
        xc = xc + cb_ref[...]
        gts = _dot(xc.astype(BF16), wg_ref[0]) + bg_ref[...]
        for d, (a_ref, b_ref) in enumerate(((a_f, b_f), (a_b, b_b))):
            r = _sigmoid(gts[:, (2 * d) * c:(2 * d + 1) * c])
            i = _sigmoid(gts[:, (2 * d + 1) * c:(2 * d + 2) * c])
            a = jnp.exp(-RG_C * r * sp[d:d + 1, :])
            a_ref[pl.ds(r0, RG_CHUNK), :] = a
            b_ref[pl.ds(r0, RG_CHUNK), :] = jnp.sqrt(jnp.maximum(1.0 - a * a, 0.0)) * (i * xc)
        return carry

    lax.fori_loop(0, t_tot // RG_CHUNK, gate_chunk, 0)

    def scan_step(t, carry):
        hf, hb = carry
        rows_f = pl.ds(pl.multiple_of(t * nseq, nseq), nseq)
        rows_b = pl.ds(pl.multiple_of((seq_len - 1 - t) * nseq, nseq), nseq)
        hf = a_f[rows_f, :] * hf + b_f[rows_f, :]
        hb = a_b[rows_b, :] * hb + b_b[rows_b, :]
        b_f[rows_f, :] = hf
        b_b[rows_b, :] = hb
        return hf, hb

    hf, hb = lax.fori_loop(0, seq_len, scan_step, (h0_ref[0], h0_ref[1]), unroll=8)
    fin_ref[0] = hf
    fin_ref[1] = hb

    def out_chunk(ci, carry):
        rs = pl.ds(pl.multiple_of(ci * RG_CHUNK, RG_CHUNK), RG_CHUNK)
        y_ref[rs, :] = (b_f[rs, :] + b_b[rs, :]) * jax.nn.gelu(gate_ref[rs, :])
        return carry

    lax.fori_loop(0, t_tot // RG_CHUNK, out_chunk, 0)


def _rg_gate_weights(wa, wx):
    per_step = RG_CB // RG_BLOCK
    steps = D_RNN // RG_CB
    mats = []
    for d in range(2):
        for w in (wa[d], wx[d]):
            w = w.reshape(steps, per_step, RG_BLOCK, RG_BLOCK)
            eye = jnp.eye(per_step, dtype=w.dtype)
            m = jnp.einsum('spde,pq->spdqe', w, eye).reshape(steps, RG_CB, RG_CB)
            mats.append(m)
    return jnp.concatenate(mats, axis=-1).astype(BF16)


def rglru_block(u, nseq, seq_len, conv_w, conv_b, wa, ba, wx, bx, lam, h0):
    t = u.shape[0]
    c = RG_CB
    steps = D_RNN // c
    wg = _rg_gate_weights(wa, wx)
    bg = jnp.stack([ba[0], bx[0], ba[1], bx[1]], axis=0).reshape(4, steps, c)
    bg = bg.transpose(1, 0, 2).reshape(steps, 1, 4 * c)
    y, fin = pl.pallas_call(
        functools.partial(_rglru_kernel, nseq, seq_len),
        grid=(steps,),
        in_specs=[pl.BlockSpec((t, c), lambda j: (0, j)),
                  pl.BlockSpec((t, c), lambda j: (0, steps + j)),
                  pl.BlockSpec((conv_w.shape[0], c), lambda j: (0, j)),
                  pl.BlockSpec((1, c), lambda j: (0, j)),
                  pl.BlockSpec((1, c, 4 * c), lambda j: (j, 0, 0)),
                  pl.BlockSpec((None, 1, 4 * c), lambda j: (j, 0, 0)),
                  pl.BlockSpec((2, c), lambda j: (0, j)),
                  pl.BlockSpec((2, nseq, c), lambda j: (0, 0, j))],
        out_specs=[pl.BlockSpec((t, c), lambda j: (0, j)),
                   pl.BlockSpec((2, nseq, c), lambda j: (0, 0, j))],
        out_shape=[jax.ShapeDtypeStruct((t, D_RNN), F32),
                   jax.ShapeDtypeStruct((2, nseq, D_RNN), F32)],
        scratch_shapes=[pltpu.VMEM((t + 4 * nseq, c), F32)] + [pltpu.VMEM((t, c), F32)] * 4,
        compiler_params=_cparams("arbitrary"),
        name="rglru_block",
    )(u, u, conv_w, conv_b.reshape(1, -1), wg, bg, lam, h0)
    return y, fin


def _router_kernel(x_ref, g_ref, sc_ref, sh_ref, rw_ref, rb_ref, o_ref):
    h = _normmod(x_ref[...], g_ref[...], sc_ref[0], sh_ref[0])
    logits = lax.dot_general(rw_ref[...], h, (((1,), (1,)), ((), ())),
                             precision=lax.Precision.HIGHEST, preferred_element_type=F32)
    scores = jax.nn.sigmoid(logits)
    sel = scores + rb_ref[...]
    row = [sel[e:e + 1, :] for e in range(N_EXPERTS)]
    gs = []
    for g in range(N_GROUPS):
        r = row[g * EXPERTS_PER_GROUP:(g + 1) * EXPERTS_PER_GROUP]
        best_pair = None
        for i in range(EXPERTS_PER_GROUP):
            for j in range(i + 1, EXPERTS_PER_GROUP):
                s = r[i] + r[j]
                best_pair = s if best_pair is None else jnp.maximum(best_pair, s)
        gs.append(best_pair)
    best = jnp.zeros_like(gs[0], dtype=jnp.int32)
    top = gs[0]
    for g in range(1, N_GROUPS):
        better = gs[g] > top
        best = jnp.where(better, g, best)
        top = jnp.where(better, gs[g], top)
    picked = []
    for e in range(N_EXPERTS):
        g = e // EXPERTS_PER_GROUP
        rank = jnp.zeros_like(best)
        for o in range(g * EXPERTS_PER_GROUP, (g + 1) * EXPERTS_PER_GROUP):
            if o == e:
                continue
            ahead = (row[o] > row[e]) | ((row[o] == row[e]) & (o < e))
            rank = rank + ahead.astype(jnp.int32)
        picked.append((best == g) & (rank < 2))
    den = jnp.zeros_like(gs[0])
    for e in range(N_EXPERTS):
        den = den + jnp.where(picked[e], scores[e:e + 1, :], 0.0)
    gate = [jnp.where(picked[e], scores[e:e + 1, :] / den, 0.0) for e in range(N_EXPERTS)]
    cls = jnp.zeros_like(den)
    w_a = jnp.zeros_like(den)
    w_b = jnp.zeros_like(den)
    for g in range(N_GROUPS):
        for pi, (a, b) in enumerate(MOE_PAIRS):
            ea, eb = g * EXPERTS_PER_GROUP + a, g * EXPERTS_PER_GROUP + b
            both = picked[ea] & picked[eb]
            cls = jnp.where(both, float(g * len(MOE_PAIRS) + pi), cls)
            w_a = jnp.where(both, gate[ea], w_a)
            w_b = jnp.where(both, gate[eb], w_b)
    o_ref[...] = jnp.concatenate([cls, w_a, w_b, jnp.zeros((ROUTE_ROWS - 3, cls.shape[1]), F32)], axis=0)


MOE_PAIRS = ((0, 1), (0, 2), (0, 3), (1, 3), (1, 2), (2, 3))
N_CLS = N_GROUPS * len(MOE_PAIRS)
CLS_PAD = 32
ROUTE_ROWS = 8
MOE_TS = 256
MOE_TM = 256
SLOT_BLK = 512
ROW_W = D_MODEL + LANES


def router_gates(x, g, mod, router_w, router_b, seq_len, per_seq, tm=512):
    t, d = x.shape
    return pl.pallas_call(
        _router_kernel,
        grid=(t // tm,),
        in_specs=[pl.BlockSpec((tm, d), lambda i: (i, 0)),
                  pl.BlockSpec((1, d), lambda i: (0, 0)),
                  _mod_spec(4, tm, seq_len, per_seq),
                  _mod_spec(3, tm, seq_len, per_seq),
                  pl.BlockSpec((N_EXPERTS, d), lambda i: (0, 0)),
                  pl.BlockSpec((N_EXPERTS, 1), lambda i: (0, 0))],
        out_specs=pl.BlockSpec((ROUTE_ROWS, tm), lambda i: (0, i)),
        out_shape=jax.ShapeDtypeStruct((ROUTE_ROWS, t), F32),
        compiler_params=_cparams("arbitrary"),
        name="moe_router",
    )(x, g.reshape(1, d), mod, mod, router_w.T, router_b.reshape(N_EXPERTS, 1))


def _slots_kernel(route_ref, slot_ref, off_ref, cnt_ref, cnt_s, off_s, run_s):
    phase = pl.program_id(0)
    j = pl.program_id(1)
    cls = route_ref[0:1, :]
    cid = lax.broadcasted_iota(jnp.int32, (CLS_PAD, SLOT_BLK), 0).astype(F32)
    member = (cid == cls).astype(F32)
    n_here = jnp.sum(member, axis=1, keepdims=True)

    @pl.when((phase == 0) & (j == 0))
    def _():
        cnt_s[...] = jnp.zeros_like(cnt_s)

    @pl.when(phase == 0)
    def _():
        cnt_s[...] += n_here

    @pl.when((phase == 1) & (j == 0))
    def _():
        padded = jnp.ceil(cnt_s[...] * (1.0 / MOE_TS)) * MOE_TS
        r = lax.broadcasted_iota(jnp.int32, (CLS_PAD, CLS_PAD), 0)
        c = lax.broadcasted_iota(jnp.int32, (CLS_PAD, CLS_PAD), 1)
        before = (c < r).astype(F32)
        off_s[...] = jnp.dot(before, padded, precision=lax.Precision.HIGHEST, preferred_element_type=F32)
        run_s[...] = jnp.zeros_like(run_s)
        off_ref[...] = off_s[...]
        cnt_ref[...] = cnt_s[...]

    @pl.when(phase == 1)
    def _():
        tr = lax.broadcasted_iota(jnp.int32, (SLOT_BLK, SLOT_BLK), 0)
        tc = lax.broadcasted_iota(jnp.int32, (SLOT_BLK, SLOT_BLK), 1)
        earlier = (tr < tc).astype(BF16)
        rank = _dot(member.astype(BF16), earlier)
        base = run_s[:, 0:1] + off_s[:, 0:1]
        slot = jnp.sum(member * (rank + base), axis=0, keepdims=True)
        slot_ref[...] = slot.astype(jnp.int32)
        run_s[...] += n_here


def moe_slots(route):
    t = route.shape[1]
    n_blk = t // SLOT_BLK
    stat = jax.ShapeDtypeStruct((CLS_PAD, LANES), F32)
    stat_spec = pl.BlockSpec((CLS_PAD, LANES), lambda p, j: (0, 0))
    return pl.pallas_call(
        _slots_kernel,
        grid=(2, n_blk),
        in_specs=[pl.BlockSpec((ROUTE_ROWS, SLOT_BLK), lambda p, j: (0, j))],
        out_specs=[pl.BlockSpec((1, SLOT_BLK), lambda p, j: (0, j * p)), stat_spec, stat_spec],
        out_shape=[jax.ShapeDtypeStruct((1, t), jnp.int32), stat, stat],
        scratch_shapes=[pltpu.VMEM((CLS_PAD, LANES), F32)] * 3,
        compiler_params=_cparams("arbitrary", "arbitrary"),
        name="moe_slots",
    )(route)


def _tile_maps(off, cnt, n_tiles):
    off = off[:N_CLS, 0].astype(jnp.int32)
    cnt = cnt[:N_CLS, 0].astype(jnp.int32)
    ends = off + ((cnt + MOE_TS - 1) // MOE_TS) * MOE_TS
    n_used = ends[-1] // MOE_TS
    k = jnp.arange(n_tiles, dtype=jnp.int32)
    tix = jnp.minimum(k, n_used - 1)
    cls = jnp.sum((tix[:, None] * MOE_TS >= ends[None, :]).astype(jnp.int32), axis=1)
    pair = jnp.asarray(MOE_PAIRS, jnp.int32)
    grp = (cls // len(MOE_PAIRS)) * EXPERTS_PER_GROUP
    ea = grp + pair[cls % len(MOE_PAIRS), 0]
    eb = grp + pair[cls % len(MOE_PAIRS), 1]
    first = jnp.ones((1,), jnp.int32)
    chg_a = jnp.concatenate([first, (ea[1:] != ea[:-1]).astype(jnp.int32)])
    chg_b = jnp.concatenate([first, (eb[1:] != eb[:-1]).astype(jnp.int32)])
    return ea, eb, tix, chg_a, chg_b, n_used.reshape(1)


def _dispatch_kernel(n_steps, slots_ref, x_ref, g_ref, sc_ref, sh_ref, rt_ref, hs_in, hs_out, rowbuf, sem):
    del hs_in
    i = pl.program_id(0)
    cur = i % 2

    def wait_rows(s):
        pltpu.make_async_copy(rowbuf.at[s], rowbuf.at[s], sem.at[s]).wait()

    @pl.when(i >= 2)
    def _():
        wait_rows(cur)

    rowbuf[cur, :, 0:D_MODEL] = _normmod(x_ref[...], g_ref[...], sc_ref[0], sh_ref[0])
    rowbuf[cur, :, D_MODEL:ROW_W] = jnp.concatenate(
        [rt_ref[...], jnp.zeros((MOE_TM, LANES - ROUTE_ROWS), F32)], axis=1)

    def issue(r, carry):
        dst = slots_ref[i * MOE_TM + r]
        pltpu.make_async_copy(rowbuf.at[cur, r], hs_out.at[dst], sem.at[cur]).start()
        return carry

    lax.fori_loop(0, MOE_TM, issue, 0, unroll=8)

    @pl.when(i == n_steps - 1)
    def _():
        wait_rows(cur)
        if n_steps >= 2:
            wait_rows(1 - cur)


def moe_dispatch(x, g, mod, route_t, slots, hs, seq_len, per_seq):
    t, d = x.shape
    n_steps = t // MOE_TM
    grid_spec = pltpu.PrefetchScalarGridSpec(
        num_scalar_prefetch=1,
        grid=(n_steps,),
        in_specs=[pl.BlockSpec((MOE_TM, d), lambda i, s: (i, 0)),
                  pl.BlockSpec((1, d), lambda i, s: (0, 0)),
                  _mod_spec(4, MOE_TM, seq_len, per_seq),
                  _mod_spec(3, MOE_TM, seq_len, per_seq),
                  pl.BlockSpec((MOE_TM, ROUTE_ROWS), lambda i, s: (i, 0)),
                  pl.BlockSpec(memory_space=pl.ANY)],
        out_specs=pl.BlockSpec(memory_space=pl.ANY),
        scratch_shapes=[pltpu.VMEM((2, MOE_TM, ROW_W), F32), pltpu.SemaphoreType.DMA((2,))],
    )
    return pl.pallas_call(
        functools.partial(_dispatch_kernel, n_steps),
        grid_spec=grid_spec,
        out_shape=jax.ShapeDtypeStruct(hs.shape, F32),
        input_output_aliases={6: 0},
        compiler_params=_cparams("arbitrary"),
        name="moe_dispatch",
    )(slots, x, g.reshape(1, d), mod, mod, route_t, hs)


def _experts_kernel(ea_ref, eb_ref, tix_ref, chga_ref, chgb_ref, nused_ref, hs_ref,
                    wga_ref, wua_ref, wda_ref, wgb_ref, wub_ref, wdb_ref, ys_ref,
                    bga, bua, bda, bgb, bub, bdb):
    k = pl.program_id(0)

    @pl.when(k < nused_ref[0])
    def _():
        @pl.when(chga_ref[k] == 1)
        def _():
            bga[...] = wga_ref[0].astype(BF16)
            bua[...] = wua_ref[0].astype(BF16)
            bda[...] = wda_ref[0].astype(BF16)

        @pl.when(chgb_ref[k] == 1)
        def _():
            bgb[...] = wgb_ref[0].astype(BF16)
            bub[...] = wub_ref[0].astype(BF16)
            bdb[...] = wdb_ref[0].astype(BF16)

        h = hs_ref[:, 0:D_MODEL].astype(BF16)

        def ffn(bg, bu, bd, w):
            hid = _dot(h, bg[...])
            up = _dot(h, bu[...])
            act = (hid * _sigmoid(hid)) * up * w
            return _dot(act.astype(BF16), bd[...])

        ys_ref[...] = (ffn(bga, bua, bda, hs_ref[:, D_MODEL + 1:D_MODEL + 2])
                       + ffn(bgb, bub, bdb, hs_ref[:, D_MODEL + 2:D_MODEL + 3]))

    @pl.when(k >= nused_ref[0])
    def _():
        ys_ref[...] = jnp.zeros_like(ys_ref)


def moe_experts(hs, maps, layer, w_gate, w_up, w_down):
    n_tiles = hs.shape[0] // MOE_TS
    d = D_MODEL

    def w_spec(shape, which):
        return pl.BlockSpec((None, 1) + shape, lambda k, ea, eb, *_: (layer, (ea, eb)[which][k], 0, 0))

    grid_spec = pltpu.PrefetchScalarGridSpec(
        num_scalar_prefetch=6,
        grid=(n_tiles,),
        in_specs=[pl.BlockSpec((MOE_TS, ROW_W), lambda k, ea, eb, tix, *_: (tix[k], 0)),
                  w_spec((d, D_EXPERT), 0), w_spec((d, D_EXPERT), 0), w_spec((D_EXPERT, d), 0),
                  w_spec((d, D_EXPERT), 1), w_spec((d, D_EXPERT), 1), w_spec((D_EXPERT, d), 1)],
        out_specs=pl.BlockSpec((MOE_TS, d), lambda k, *_: (k, 0)),
        scratch_shapes=[pltpu.VMEM((d, D_EXPERT), BF16), pltpu.VMEM((d, D_EXPERT), BF16),
                        pltpu.VMEM((D_EXPERT, d), BF16)] * 2,
    )
    return pl.pallas_call(
        _experts_kernel,
        grid_spec=grid_spec,
        out_shape=jax.ShapeDtypeStruct((hs.shape[0], d), F32),
        compiler_params=_cparams("arbitrary"),
        name="moe_experts",
    )(*maps, hs, w_gate, w_up, w_down, w_gate, w_up, w_down)


def _combine_kernel(final, n_steps, slots_ref, x_ref, g2_ref, fg_ref, ys_hbm, o_ref, gbuf, sem):
    i = pl.program_id(0)
    cur = i % 2

    def issue_tile(tile, s):
        def issue(r, carry):
            src = slots_ref[tile * MOE_TM + r]
            pltpu.make_async_copy(ys_hbm.at[src], gbuf.at[s, r], sem.at[s]).start()
            return carry
        lax.fori_loop(0, MOE_TM, issue, 0, unroll=8)

    @pl.when(i == 0)
    def _():
        issue_tile(0, 0)

    @pl.when(i + 1 < n_steps)
    def _():
        issue_tile(i + 1, 1 - cur)

    pltpu.make_async_copy(gbuf.at[cur], gbuf.at[cur], sem.at[cur]).wait()
    y = x_ref[...] + g2_ref[0] * gbuf[cur]
    if final:
        ms = jnp.mean(y * y, axis=-1, keepdims=True)
        y = y * lax.rsqrt(ms + EPS) * fg_ref[...]
    o_ref[...] = y


def moe_combine(x, mod, slots, ys, final_g, final, seq_len, per_seq):
    t, d = x.shape
    n_steps = t // MOE_TM
    grid_spec = pltpu.PrefetchScalarGridSpec(
        num_scalar_prefetch=1,
        grid=(n_steps,),
        in_specs=[pl.BlockSpec((MOE_TM, d), lambda i, s: (i, 0)),
                  _mod_spec(5, MOE_TM, seq_len, per_seq),
                  pl.BlockSpec((1, d), lambda i, s: (0, 0)),
                  pl.BlockSpec(memory_space=pl.ANY)],
        out_specs=pl.BlockSpec((MOE_TM, d), lambda i, s: (i, 0)),
        scratch_shapes=[pltpu.VMEM((2, MOE_TM, d), F32), pltpu.SemaphoreType.DMA((2,))],
    )
    return pl.pallas_call(
        functools.partial(_combine_kernel, final, n_steps),
        grid_spec=grid_spec,
        out_shape=jax.ShapeDtypeStruct((t, d), F32),
        compiler_params=_cparams("arbitrary"),
        name="moe_combine",
    )(slots, x, mod, final_g.reshape(1, d), ys)


def moe_block(xs, mods_l, per_seqs, seq_lens, layer, p, final):
    g = p['norm_g'][layer, 1]
    routes = [router_gates(x, g, mods_l, p['router_w'], p['router_b'], sl, ps)
              for x, sl, ps in zip(xs, seq_lens, per_seqs)]
    slots, off, cnt = moe_slots(jnp.concatenate(routes, axis=1))
    t_all = slots.shape[1]
    n_tiles = t_all // MOE_TS + N_CLS
    maps = _tile_maps(off, cnt, n_tiles)
    hs = jnp.zeros((n_tiles * MOE_TS, ROW_W), F32)
    bounds = np.cumsum([0] + [x.shape[0] for x in xs])
    group_slots = [slots[0, bounds[i]:bounds[i + 1]] for i in range(len(xs))]
    for x, r, s, sl, ps in zip(xs, routes, group_slots, seq_lens, per_seqs):
        hs = moe_dispatch(x, g, mods_l, r.T, s, hs, sl, ps)
    ys = moe_experts(hs, maps, layer, p['moe_w_gate'], p['moe_w_up'], p['moe_w_down'])
    return [moe_combine(x, mods_l, s, ys, p['final_g'], final, sl, ps)
            for x, s, sl, ps in zip(xs, group_slots, seq_lens, per_seqs)]


def _mixer(x, group, l, mod, p, hy_tables):
    per_seq, nseq, seq_len = group['per_seq'], group['nseq'], group['seq_len']
    extras = None
    if l % 2 == 0:
        e = l // 2
        u = nm_matmul(x, p['norm_g'][l, 0], mod, p['a_in_w'][e].astype(BF16), seq_len, per_seq)
        if group['ctx_k'] is None:
            attn, nk, nv = context_attention(u, nseq, seq_len)
            extras = (nk, nv)
        else:
            attn = neighbourhood_attention(u, group['ctx_k'][:, e], group['ctx_v'][:, e], p['na_rpb'][e],
                                           nseq, seq_len)
        fwd, inv = hy_tables[seq_len]
        spectrum = hyena_spectrum(seq_len, p['hy_w1'][e], p['hy_b1'][e], p['hy_w2'][e], p['hy_b2'][e],
                                  p['hy_w3'][e], p['hy_freq'][e], p['hy_d'][e], fwd)
        hy = hyena_mixer(u, nseq, seq_len, p['hy_short_w'][e], p['hy_short_b'][e], spectrum, fwd, inv)
        w_out = p['a_out_w'][e].astype(BF16)
        x = proj_residual([attn, hy], [w_out[:D_A], w_out[D_A:]], x, mod, seq_len, per_seq)
    else:
        o = l // 2
        u = nm_matmul_tm(x, p['norm_g'][l, 0], mod, p['c_in_w'][o].astype(BF16), nseq, seq_len, per_seq)
        y, extras = rglru_block(u, nseq, seq_len, p['rg_conv_w'][o], p['rg_conv_b'][o], p['rg_wa'][o],
                                p['rg_ba'][o], p['rg_wx'][o], p['rg_bx'][o], p['rg_lam'][o], group['h0'][o])
        x = proj_residual_tm(y, p['c_out_w'][o].astype(BF16), x, mod, nseq, seq_len, per_seq)
    return x, extras


def kernel(x_prompt, x_sample, cache_k, cache_v, state_h, c, c_ctx, norm_g, ada_w, ada_b, final_g, a_in_w, a_out_w, na_rpb, hy_short_w, hy_short_b, hy_w1, hy_b1, hy_w2, hy_b2, hy_w3, hy_freq, hy_d, c_in_w, c_out_w, rg_conv_w, rg_conv_b, rg_wa, rg_ba, rg_wx, rg_bx, rg_lam, router_w, router_b, moe_w_gate, moe_w_up, moe_w_down):
    p = dict(norm_g=norm_g, final_g=final_g, a_in_w=a_in_w, a_out_w=a_out_w, na_rpb=na_rpb,
             hy_short_w=hy_short_w, hy_short_b=hy_short_b, hy_w1=hy_w1, hy_b1=hy_b1, hy_w2=hy_w2,
             hy_b2=hy_b2, hy_w3=hy_w3, hy_freq=hy_freq, hy_d=hy_d, c_in_w=c_in_w, c_out_w=c_out_w,
             rg_conv_w=rg_conv_w, rg_conv_b=rg_conv_b, rg_wa=rg_wa, rg_ba=rg_ba, rg_wx=rg_wx, rg_bx=rg_bx,
             rg_lam=rg_lam, router_w=router_w, router_b=router_b, moe_w_gate=moe_w_gate,
             moe_w_up=moe_w_up, moe_w_down=moe_w_down)
    batch, seq, d = x_prompt.shape
    dec_batch, dec_seq, _ = x_sample.shape
    n_odd = DEPTH // 2
    assert 1 + dec_batch <= MOD_ROWS

    cond = jnp.concatenate([c_ctx[None, :], c, jnp.zeros((MOD_ROWS - 1 - dec_batch, d), F32)], axis=0)
    m = modulation(cond, ada_w, ada_b)
    mods = [m[l].reshape(MOD_ROWS * N_MOD, 1, d) for l in range(DEPTH)]

    tables = {}
    for sl in (seq, dec_seq):
        fwd, inv = _dft_tables(sl)
        tables[sl] = (jnp.asarray(fwd).astype(BF16), jnp.asarray(inv).astype(BF16))

    groups = [
        dict(per_seq=False, nseq=batch, seq_len=seq, ctx_k=None, ctx_v=None,
             h0=[jnp.zeros((2, batch, D_RNN), F32)] * n_odd),
        dict(per_seq=True, nseq=dec_batch, seq_len=dec_seq, ctx_k=cache_k, ctx_v=cache_v,
             h0=[state_h[:, o].transpose(1, 0, 2) for o in range(n_odd)]),
    ]
    xs = [x_prompt.reshape(batch * seq, d), x_sample.reshape(dec_batch * dec_seq, d)]
    k_list, v_list, h_list = [], [], []
    for l in range(DEPTH):
        mixed = [_mixer(x, grp, l, mods[l], p, tables) for x, grp in zip(xs, groups)]
        if l % 2 == 0:
            k_list.append(mixed[0][1][0])
            v_list.append(mixed[0][1][1])
        else:
            h_list.append(mixed[0][1].transpose(1, 0, 2))
        xs = moe_block([mx[0] for mx in mixed], mods[l], [grp['per_seq'] for grp in groups],
                       [grp['seq_len'] for grp in groups], l, p, l == DEPTH - 1)
    new_k = jnp.stack(k_list, axis=1)
    new_v = jnp.stack(v_list, axis=1)
    new_h = jnp.stack(h_list, axis=1)
    return (xs[0].reshape(batch, seq, d), xs[1].reshape(dec_batch, dec_seq, d), new_k, new_v, new_h)
```

```python
import functools
import math

import numpy as np
import jax
import jax.numpy as jnp
from jax import lax
from jax.experimental import pallas as pl
from jax.experimental.pallas import tpu as pltpu

F32 = jnp.float32
BF16 = jnp.bfloat16

D_MODEL = 1024
DEPTH = 2
GRID_W = 64
EPS = 1e-6
NEG_INF = -1e30
NA_HEADS = 8
HEAD_DIM = 64
D_A = NA_HEADS * HEAD_DIM
WIN_ROWS = 8
WIN_COLS = 16
D_B = D_MODEL - D_A
HY_ORDER = 2
HY_EMB = 33
HY_BANDS = (HY_EMB - 1) // 2
HY_FFN = 64
HY_DECAY_TARGET = 1e-2
HY_FAST_PCT = 0.3
HY_SLOW_PCT = 1.5
D_RNN = D_MODEL
RG_BLOCK = 64
RG_C = 8.0
N_EXPERTS = 16
N_GROUPS = 4
EXPERTS_PER_GROUP = N_EXPERTS // N_GROUPS
D_EXPERT = 512

LANES = 128
VMEM_LIMIT = 56 * 1024 * 1024
N_MOD = 6
MOD_ROWS = 16


def _cparams(*sem):
    return pltpu.CompilerParams(dimension_semantics=sem, vmem_limit_bytes=VMEM_LIMIT)


def _dot(a, b):
    return jnp.dot(a, b, preferred_element_type=F32)


def _dot_nt(a, b):
    return lax.dot_general(a, b, (((1,), (1,)), ((), ())), preferred_element_type=F32)


def _sigmoid(x):
    return 0.5 * jnp.tanh(0.5 * x) + 0.5


def _normmod(x, g, sc, sh):
    ms = jnp.mean(x * x, axis=-1, keepdims=True)
    return (x * lax.rsqrt(ms + EPS) * g) * (1.0 + sc) + sh


def _mod_spec(chunk, tm, seq_len, per_seq):
    if per_seq:
        return pl.BlockSpec((1, 1, D_MODEL), lambda i, *_: ((1 + (i * tm) // seq_len) * N_MOD + chunk, 0, 0))
    return pl.BlockSpec((1, 1, D_MODEL), lambda i, *_: (chunk, 0, 0))


def _mod_kernel(c_ref, w_ref, b_ref, o_ref):
    s = c_ref[...]
    s = s * jax.nn.sigmoid(s)
    o_ref[0] = _dot(s.astype(BF16), w_ref[0].astype(BF16)) + b_ref[0]


def modulation(cond, ada_w, ada_b):
    tn = 1536
    n = ada_w.shape[-1]
    return pl.pallas_call(
        _mod_kernel,
        grid=(DEPTH, n // tn),
        in_specs=[pl.BlockSpec((MOD_ROWS, D_MODEL), lambda l, j: (0, 0)),
                  pl.BlockSpec((1, D_MODEL, tn), lambda l, j: (l, 0, j)),
                  pl.BlockSpec((1, 1, tn), lambda l, j: (l, 0, j))],
        out_specs=pl.BlockSpec((1, MOD_ROWS, tn), lambda l, j: (l, 0, j)),
        out_shape=jax.ShapeDtypeStruct((DEPTH, MOD_ROWS, n), F32),
        compiler_params=_cparams("arbitrary", "arbitrary"),
        name="modulation",
    )(cond, ada_w, ada_b.reshape(DEPTH, 1, n))


def _nm_matmul_kernel(x_ref, g_ref, sc_ref, sh_ref, w_ref, o_ref):
    h = _normmod(x_ref[...], g_ref[...], sc_ref[0], sh_ref[0])
    o_ref[...] = _dot(h.astype(BF16), w_ref[...])


def nm_matmul(x, g, mod, w, seq_len, per_seq, tm=512):
    t, d = x.shape
    n = w.shape[1]
    return pl.pallas_call(
        _nm_matmul_kernel,
        grid=(t // tm,),
        in_specs=[pl.BlockSpec((tm, d), lambda i: (i, 0)),
                  pl.BlockSpec((1, d), lambda i: (0, 0)),
                  _mod_spec(1, tm, seq_len, per_seq),
                  _mod_spec(0, tm, seq_len, per_seq),
                  pl.BlockSpec((d, n), lambda i: (0, 0))],
        out_specs=pl.BlockSpec((tm, n), lambda i: (i, 0)),
        out_shape=jax.ShapeDtypeStruct((t, n), F32),
        compiler_params=_cparams("arbitrary"),
        name="norm_mod_proj",
    )(x, g.reshape(1, d), mod, mod, w)


def _proj_res_kernel(n_act, *refs):
    acts = refs[:n_act]
    ws = refs[n_act:2 * n_act]
    x_ref, g_ref, o_ref = refs[2 * n_act:]
    acc = _dot(acts[0][...].astype(BF16), ws[0][...])
    for a, w in zip(acts[1:], ws[1:]):
        acc += _dot(a[...].astype(BF16), w[...])
    o_ref[...] = x_ref[...] + g_ref[0] * acc


def proj_residual(acts, ws, x, mod, seq_len, per_seq, tm=512):
    t, d = x.shape
    in_specs = [pl.BlockSpec((tm, a.shape[1]), lambda i: (i, 0)) for a in acts]
    in_specs += [pl.BlockSpec(w.shape, lambda i: (0, 0)) for w in ws]
    in_specs += [pl.BlockSpec((tm, d), lambda i: (i, 0)), _mod_spec(2, tm, seq_len, per_seq)]
    return pl.pallas_call(
        functools.partial(_proj_res_kernel, len(acts)),
        grid=(t // tm,),
        in_specs=in_specs,
        out_specs=pl.BlockSpec((tm, d), lambda i: (i, 0)),
        out_shape=jax.ShapeDtypeStruct((t, d), F32),
        compiler_params=_cparams("arbitrary"),
        name="proj_residual",
    )(*acts, *ws, x, mod)


def _row_permutation(nseq, steps, to_time_major):
    n = nseq * steps
    i = lax.broadcasted_iota(jnp.int32, (n, n), 0)
    j = lax.broadcasted_iota(jnp.int32, (n, n), 1)
    if to_time_major:
        src = (i % nseq) * steps + i // nseq
    else:
        src = (i % steps) * nseq + i // steps
    return (j == src).astype(BF16)


def _nm_matmul_tm_kernel(nseq, steps, x_ref, g_ref, sc_ref, sh_ref, w_ref, o_ref):
    h = _normmod(x_ref[...], g_ref[...], sc_ref[...], sh_ref[...])
    h = h.reshape(nseq * steps, h.shape[-1]).astype(BF16)
    h = _dot(_row_permutation(nseq, steps, True), h).astype(BF16)
    o_ref[...] = _dot(h, w_ref[...])


def _group_mod(mod, chunk, nseq, per_seq):
    rows = mod.reshape(MOD_ROWS, N_MOD, 1, D_MODEL)
    return rows[1:1 + nseq, chunk] if per_seq else rows[0:1, chunk]


def nm_matmul_tm(x, g, mod, w, nseq, seq_len, per_seq):
    t, d = x.shape
    n = w.shape[1]
    steps = TM_ROWS // nseq
    sc = _group_mod(mod, 1, nseq, per_seq)
    sh = _group_mod(mod, 0, nseq, per_seq)
    mod_spec = pl.BlockSpec(sc.shape, lambda i: (0, 0, 0))
    return pl.pallas_call(
        functools.partial(_nm_matmul_tm_kernel, nseq, steps),
        grid=(seq_len // steps,),
        in_specs=[pl.BlockSpec((nseq, steps, d), lambda i: (0, i, 0)),
                  pl.BlockSpec((1, d), lambda i: (0, 0)),
                  mod_spec, mod_spec,
                  pl.BlockSpec((d, n), lambda i: (0, 0))],
        out_specs=pl.BlockSpec((TM_ROWS, n), lambda i: (i, 0)),
        out_shape=jax.ShapeDtypeStruct((t, n), F32),
        compiler_params=_cparams("arbitrary"),
        name="norm_mod_proj_tm",
    )(x.reshape(nseq, seq_len, d), g.reshape(1, d), sc, sh, w)


def _proj_res_tm_kernel(nseq, steps, y_ref, w_ref, x_ref, g_ref, o_ref):
    y = _dot(_row_permutation(nseq, steps, False), y_ref[...].astype(BF16)).astype(BF16)
    acc = _dot(y, w_ref[...])
    o_ref[...] = x_ref[...] + g_ref[...] * acc.reshape(nseq, steps, acc.shape[-1])


def proj_residual_tm(y, w, x, mod, nseq, seq_len, per_seq):
    t, d = x.shape
    steps = TM_ROWS // nseq
    g1 = _group_mod(mod, 2, nseq, per_seq)
    out = pl.pallas_call(
        functools.partial(_proj_res_tm_kernel, nseq, steps),
        grid=(seq_len // steps,),
        in_specs=[pl.BlockSpec((TM_ROWS, y.shape[1]), lambda i: (i, 0)),
                  pl.BlockSpec(w.shape, lambda i: (0, 0)),
                  pl.BlockSpec((nseq, steps, d), lambda i: (0, i, 0)),
                  pl.BlockSpec(g1.shape, lambda i: (0, 0, 0))],
        out_specs=pl.BlockSpec((nseq, steps, d), lambda i: (0, i, 0)),
        out_shape=jax.ShapeDtypeStruct((nseq, seq_len, d), F32),
        compiler_params=_cparams("arbitrary"),
        name="proj_residual_tm",
    )(y, w, x.reshape(nseq, seq_len, d), g1)
    return out.reshape(t, d)


def _ctx_attn_kernel(q_ref, k_ref, v_ref, o_ref, nk_ref, nv_ref):
    scale = HEAD_DIM ** -0.5
    for h in range(NA_HEADS):
        sl = slice(h * HEAD_DIM, (h + 1) * HEAD_DIM)
        q = q_ref[:, sl]
        k = k_ref[:, sl]
        v = v_ref[:, sl]
        nk_ref[0, h] = k
        nv_ref[0, h] = v
        s = _dot_nt(q.astype(BF16), k.astype(BF16)) * scale
        m = jnp.max(s, axis=-1, keepdims=True)
        p = jnp.exp(s - m)
        den = jnp.sum(p, axis=-1, keepdims=True)
        o_ref[:, sl] = _dot(p.astype(BF16), v.astype(BF16)) / den


def context_attention(u, nseq, seq_len):
    t = u.shape[0]
    kv_shape = jax.ShapeDtypeStruct((nseq, NA_HEADS, seq_len, HEAD_DIM), F32)
    kv_spec = pl.BlockSpec((1, NA_HEADS, seq_len, HEAD_DIM), lambda b: (b, 0, 0, 0))
    return pl.pallas_call(
        _ctx_attn_kernel,
        grid=(nseq,),
        in_specs=[pl.BlockSpec((seq_len, D_A), lambda b: (b, 0)),
                  pl.BlockSpec((seq_len, D_A), lambda b: (b, 1)),
                  pl.BlockSpec((seq_len, D_A), lambda b: (b, 2))],
        out_specs=[pl.BlockSpec((seq_len, D_A), lambda b: (b, 0)), kv_spec, kv_spec],
        out_shape=[jax.ShapeDtypeStruct((t, D_A), F32), kv_shape, kv_shape],
        compiler_params=_cparams("arbitrary"),
        name="context_attention",
    )(u, u, u)


N_DR = 2 * WIN_ROWS - 1
N_DC = 2 * WIN_COLS - 1


def _na_col_tables():
    cols = np.arange(GRID_W)
    col_start = np.clip(cols - WIN_COLS // 2, 0, GRID_W - WIN_COLS)
    col_in = (cols[None, :] >= col_start[:, None]) & (cols[None, :] < col_start[:, None] + WIN_COLS)
    dc = np.clip(cols[None, :] - cols[:, None], 1 - WIN_COLS, WIN_COLS - 1) + WIN_COLS - 1
    onehot = (dc.reshape(1, -1) == np.arange(32)[:, None]).astype(np.float32)
    return onehot, col_in.reshape(1, -1).astype(np.float32)


def _na_bias_kernel(r_ref, e_ref, m_ref, o_ref):
    t = jnp.dot(r_ref[...], e_ref[...], precision=lax.Precision.HIGHEST, preferred_element_type=F32)
    o_ref[...] = jnp.where(m_ref[...] > 0.0, t, NEG_INF)


def na_bias_table(rpb):
    onehot, col_in = _na_col_tables()
    n_rows = NA_HEADS * N_DR
    r = jnp.zeros((LANES, 32), F32).at[:n_rows, :N_DC].set(rpb.reshape(n_rows, N_DC).astype(F32))
    t = pl.pallas_call(
        _na_bias_kernel,
        out_shape=jax.ShapeDtypeStruct((LANES, GRID_W * GRID_W), F32),
        name="na_bias_table",
    )(r, jnp.asarray(onehot), jnp.asarray(col_in))
    t = t[:n_rows].reshape(NA_HEADS, N_DR, GRID_W, GRID_W)
    return jnp.concatenate([t[:, :-1], t[:, 1:]], axis=-1)


def _na_kernel(rows, q_ref, k_ref, v_ref, ck_ref, cv_ref, bias_ref, o_ref,
               q_s, k_s, v_s, ck_s, cv_s, s_s, p_s, den_s, o_s):
    scale = HEAD_DIM ** -0.5
    n_lat = WIN_ROWS * GRID_W
    for h in range(NA_HEADS):
        sl = slice(h * HEAD_DIM, (h + 1) * HEAD_DIM)
        q_s[h] = (q_ref[:, sl] * scale).astype(BF16)
        k_s[h] = k_ref[:, sl].astype(BF16)
        v_s[h] = v_ref[:, sl].astype(BF16)
    ck_s[...] = ck_ref[0].astype(BF16)
    cv_s[...] = cv_ref[0].astype(BF16)

    def window(r):
        start = min(max(r - WIN_ROWS // 2, 0), rows - WIN_ROWS)
        return start, start - r + WIN_ROWS - 1

    def head_body(h, carry):
        for r in range(rows):
            start, off = window(r)
            q = q_s[h, r * GRID_W:(r + 1) * GRID_W, :]
            bias = jnp.concatenate([bias_ref[h, off + 2 * i] for i in range(WIN_ROWS // 2)], axis=1)
            s_s[r * GRID_W:(r + 1) * GRID_W, 0:n_lat] = (
                _dot_nt(q, k_s[h, start * GRID_W:start * GRID_W + n_lat, :]) + bias)
            s_s[r * GRID_W:(r + 1) * GRID_W, n_lat:] = _dot_nt(q, ck_s[h])
        for r in range(rows):
            rs = slice(r * GRID_W, (r + 1) * GRID_W)
            s = s_s[rs, :]
            p = jnp.exp(s - jnp.max(s, axis=-1, keepdims=True))
            den_s[rs, :] = jnp.sum(p, axis=-1, keepdims=True)
            p_s[rs, :] = p.astype(BF16)
        for r in range(rows):
            start, _ = window(r)
            rs = slice(r * GRID_W, (r + 1) * GRID_W)
            o = (_dot(p_s[rs, 0:n_lat], v_s[h, start * GRID_W:start * GRID_W + n_lat, :])
                 + _dot(p_s[rs, n_lat:], cv_s[h]))
            o_s[h, rs, :] = o / den_s[rs, :]
        return carry

    lax.fori_loop(0, NA_HEADS, head_body, 0)
    for h in range(NA_HEADS):
        o_ref[:, h * HEAD_DIM:(h + 1) * HEAD_DIM] = o_s[h]


def neighbourhood_attention(u, ctx_k, ctx_v, rpb, nseq, seq_len):
    t = u.shape[0]
    rows = seq_len // GRID_W
    assert rows >= WIN_ROWS and WIN_ROWS % 2 == 0
    past = ctx_k.shape[2]
    bias = na_bias_table(rpb)
    ctx_spec = pl.BlockSpec((1, NA_HEADS, past, HEAD_DIM), lambda b: (b, 0, 0, 0))
    return pl.pallas_call(
        functools.partial(_na_kernel, rows),
        grid=(nseq,),
        in_specs=[pl.BlockSpec((seq_len, D_A), lambda b: (b, 0)),
                  pl.BlockSpec((seq_len, D_A), lambda b: (b, 1)),
                  pl.BlockSpec((seq_len, D_A), lambda b: (b, 2)),
                  ctx_spec, ctx_spec,
                  pl.BlockSpec(bias.shape, lambda b: (0, 0, 0, 0))],
        out_specs=pl.BlockSpec((seq_len, D_A), lambda b: (b, 0)),
        out_shape=jax.ShapeDtypeStruct((t, D_A), F32),
        scratch_shapes=[pltpu.VMEM((NA_HEADS, seq_len, HEAD_DIM), BF16)] * 3
        + [pltpu.VMEM((NA_HEADS, past, HEAD_DIM), BF16)] * 2
        + [pltpu.VMEM((seq_len, WIN_ROWS * GRID_W + past), F32),
           pltpu.VMEM((seq_len, WIN_ROWS * GRID_W + past), BF16),
           pltpu.VMEM((seq_len, 1), F32),
           pltpu.VMEM((NA_HEADS, seq_len, HEAD_DIM), F32)],
        compiler_params=_cparams("arbitrary"),
        name="neighbourhood_attention",
    )(u, u, u, ctx_k, ctx_v, bias)


def _dft_tables(seq_len):
    n = 2 * seq_len
    f = np.arange(seq_len, dtype=np.int64)
    ang = (np.outer(f, f) % n).astype(np.float64) * (math.pi / seq_len)
    cos, sin = np.cos(ang), np.sin(ang)
    alt = np.where(f % 2 == 0, 1.0, -1.0)
    s_fwd = -sin
    s_fwd[0, :] = alt
    fwd = np.concatenate([cos, s_fwd], axis=0)
    wf = np.where(f == 0, 1.0, 2.0) / n
    ci = cos.T * wf[None, :]
    si = -sin.T * wf[None, :]
    si[:, 0] = alt / n
    inv = np.concatenate([ci, si], axis=1)
    return fwd.astype(np.float32), inv.astype(np.float32)


def _hyena_feats(seq_len):
    t = np.linspace(0.0, 1.0, seq_len, dtype=np.float32)[:, None]
    w = (2.0 * math.pi * np.arange(seq_len, dtype=np.float32)[:, None] / seq_len).astype(np.float32)
    f = np.linspace(1e-4, HY_BANDS - 1, HY_BANDS, dtype=np.float32)[None, :]
    z = np.concatenate([t, np.cos(f * w), -np.sin(f * w)], axis=-1).astype(np.float32)
    max_decay = math.log(HY_DECAY_TARGET) / HY_FAST_PCT
    min_decay = math.log(HY_DECAY_TARGET) / HY_SLOW_PCT
    deltas = np.abs(np.linspace(min_decay, max_decay, D_B, dtype=np.float32))[None, :]
    return z, t, deltas


def _hy_filter_kernel(seq_len, z_ref, t_ref, dl_ref, w1_ref, b1_ref, w2_ref, b2_ref, w3_ref, fr_ref,
                      d_ref, fwd_ref, g_ref):
    hp = lax.Precision.HIGHEST
    h = jnp.sin(fr_ref[0:1, :] * (jnp.dot(z_ref[...], w1_ref[...], precision=hp) + b1_ref[...]))
    h = jnp.sin(fr_ref[1:2, :] * (jnp.dot(h, w2_ref[...], precision=hp) + b2_ref[...]))
    h = jnp.dot(h, w3_ref[...], precision=hp)
    decay = jnp.exp(-t_ref[...] * dl_ref[...])
    row0 = lax.broadcasted_iota(jnp.int32, (seq_len, D_B), 0) == 0
    sums, diffs = [], []
    for n in range(HY_ORDER):
        hf = h[:, (2 * n) * D_B:(2 * n + 1) * D_B] * decay
        hb = h[:, (2 * n + 1) * D_B:(2 * n + 2) * D_B] * decay
        gp = jnp.where(row0, hf + hb + d_ref[n:n + 1, :], hf)
        gm = jnp.where(row0, 0.0, hb)
        sums.append(gp + gm)
        diffs.append(gp - gm)
    rhs = jnp.concatenate(sums + diffs, axis=1).astype(BF16)
    spec = _dot(fwd_ref[...], rhs)
    for n in range(HY_ORDER):
        a = spec[:, n * D_B:(n + 1) * D_B]
        b = spec[:, (HY_ORDER + n) * D_B:(HY_ORDER + n + 1) * D_B]
        g_ref[n, 0:seq_len, :] = a[0:seq_len]
        g_ref[n, seq_len:, :] = jnp.where(row0, a[seq_len:], b[seq_len:])


def hyena_spectrum(seq_len, w1, b1, w2, b2, w3, freq, d, fwd):
    z, t, deltas = _hyena_feats(seq_len)
    return pl.pallas_call(
        functools.partial(_hy_filter_kernel, seq_len),
        out_shape=jax.ShapeDtypeStruct((HY_ORDER, 2 * seq_len, D_B), F32),
        compiler_params=pltpu.CompilerParams(vmem_limit_bytes=VMEM_LIMIT),
        name="hyena_spectrum",
    )(jnp.asarray(z), jnp.asarray(t), jnp.asarray(deltas), w1, b1.reshape(1, -1), w2, b2.reshape(1, -1),
      w3, freq, d, fwd)


def _hyena_kernel(seq_len, u_ref, sw_ref, sb_ref, g_ref, fwd_ref, inv_ref, o_ref):
    u = u_ref[...]
    t_idx = lax.broadcasted_iota(jnp.int32, u.shape, 0)
    prev = jnp.where(t_idx == 0, 0.0, pltpu.roll(u, 1, axis=0))
    nxt = jnp.where(t_idx == seq_len - 1, 0.0, pltpu.roll(u, seq_len - 1, axis=0))
    u = prev * sw_ref[0:1, :] + u * sw_ref[1:2, :] + nxt * sw_ref[2:3, :] + sb_ref[...]
    row0 = lax.broadcasted_iota(jnp.int32, (seq_len, D_B), 0) == 0
    z = u[:, 0:D_B]
    for n in range(HY_ORDER):
        spec = _dot(fwd_ref[...], z.astype(BF16))
        ure, uim = spec[0:seq_len], spec[seq_len:]
        gre, gim = g_ref[n, 0:seq_len, :], g_ref[n, seq_len:, :]
        pim = uim * gim
        yre = ure * gre - jnp.where(row0, 0.0, pim)
        yim = jnp.where(row0, pim, ure * gim + uim * gre)
        y = jnp.concatenate([yre, yim], axis=0).astype(BF16)
        z = u[:, (n + 1) * D_B:(n + 2) * D_B] * _dot(inv_ref[...], y)
    o_ref[...] = z


def hyena_mixer(u, nseq, seq_len, short_w, short_b, spectrum, fwd, inv):
    t = u.shape[0]
    width = (HY_ORDER + 1) * D_B
    col_block = (3 * D_A) // width
    assert col_block * width == 3 * D_A
    return pl.pallas_call(
        functools.partial(_hyena_kernel, seq_len),
        grid=(nseq,),
        in_specs=[pl.BlockSpec((seq_len, width), lambda b: (b, col_block)),
                  pl.BlockSpec(short_w.shape, lambda b: (0, 0)),
                  pl.BlockSpec((1, width), lambda b: (0, 0)),
                  pl.BlockSpec(spectrum.shape, lambda b: (0, 0, 0)),
                  pl.BlockSpec(fwd.shape, lambda b: (0, 0)),
                  pl.BlockSpec(inv.shape, lambda b: (0, 0))],
        out_specs=pl.BlockSpec((seq_len, D_B), lambda b: (b, 0)),
        out_shape=jax.ShapeDtypeStruct((t, D_B), F32),
        compiler_params=_cparams("arbitrary"),
        name="hyena_mixer",
    )(u, short_w, short_b.reshape(1, width), spectrum, fwd, inv)


RG_CB = LANES
RG_CHUNK = 512
TM_ROWS = 256


def _rglru_kernel(nseq, seq_len, gate_ref, xr_ref, cw_ref, cb_ref, wg_ref, bg_ref, lam_ref, h0_ref,
                  y_ref, fin_ref, xp_ref, a_f, b_f, a_b, b_b):
    t_tot = nseq * seq_len
    c = RG_CB
    pad = 2 * nseq
    xp_ref[0:pad, :] = jnp.zeros((pad, c), F32)
    xp_ref[pad + t_tot:, :] = jnp.zeros((pad, c), F32)
    xp_ref[pad:pad + t_tot, :] = xr_ref[...]
    nl = -lam_ref[...]
    sp = jnp.maximum(nl, 0.0) + jnp.log1p(jnp.exp(-jnp.abs(nl)))

    def gate_chunk(ci, carry):
        r0 = pl.multiple_of(ci * RG_CHUNK, RG_CHUNK)
        xc = xp_ref[pl.ds(r0, RG_CHUNK), :] * cw_ref[0:1, :]
        for j in range(1, cw_ref.shape[0]):
            xc = xc + xp_ref[pl.ds(r0 + j * nseq, RG_CHUNK), :] * cw_ref[j:j + 1, :]
        xc = xc + cb_ref[...]
        gts = _dot(xc.astype(BF16), wg_ref[0]) + bg_ref[...]
        for d, (a_ref, b_ref) in enumerate(((a_f, b_f), (a_b, b_b))):
            r = _sigmoid(gts[:, (2 * d) * c:(2 * d + 1) * c])
            i = _sigmoid(gts[:, (2 * d + 1) * c:(2 * d + 2) * c])
            a = jnp.exp(-RG_C * r * sp[d:d + 1, :])
            a_ref[pl.ds(r0, RG_CHUNK), :] = a
            b_ref[pl.ds(r0, RG_CHUNK), :] = jnp.sqrt(jnp.maximum(1.0 - a * a, 0.0)) * (i * xc)
        return carry

    lax.fori_loop(0, t_tot // RG_CHUNK, gate_chunk, 0)

    def scan_step(t, carry):
        hf, hb = carry
        rows_f = pl.ds(pl.multiple_of(t * nseq, nseq), nseq)
        rows_b = pl.ds(pl.multiple_of((seq_len - 1 - t) * nseq, nseq), nseq)
        hf = a_f[rows_f, :] * hf + b_f[rows_f, :]
        hb = a_b[rows_b, :] * hb + b_b[rows_b, :]
        b_f[rows_f, :] = hf
        b_b[rows_b, :] = hb
        return hf, hb

    hf, hb = lax.fori_loop(0, seq_len, scan_step, (h0_ref[0], h0_ref[1]), unroll=8)
    fin_ref[0] = hf
    fin_ref[1] = hb

    def out_chunk(ci, carry):
        rs = pl.ds(pl.multiple_of(ci * RG_CHUNK, RG_CHUNK), RG_CHUNK)
        y_ref[rs, :] = (b_f[rs, :] + b_b[rs, :]) * jax.nn.gelu(gate_ref[rs, :])
        return carry

    lax.fori_loop(0, t_tot // RG_CHUNK, out_chunk, 0)


def _rg_gate_weights(wa, wx):
    per_step = RG_CB // RG_BLOCK
    steps = D_RNN // RG_CB
    mats = []
    for d in range(2):
        for w in (wa[d], wx[d]):
            w = w.reshape(steps, per_step, RG_BLOCK, RG_BLOCK)
            eye = jnp.eye(per_step, dtype=w.dtype)
            m = jnp.einsum('spde,pq->spdqe', w, eye).reshape(steps, RG_CB, RG_CB)
            mats.append(m)
    return jnp.concatenate(mats, axis=-1).astype(BF16)


def rglru_block(u, nseq, seq_len, conv_w, conv_b, wa, ba, wx, bx, lam, h0):
    t = u.shape[0]
    c = RG_CB
    steps = D_RNN // c
    wg = _rg_gate_weights(wa, wx)
    bg = jnp.stack([ba[0], bx[0], ba[1], bx[1]], axis=0).reshape(4, steps, c)
    bg = bg.transpose(1, 0, 2).reshape(steps, 1, 4 * c)
    y, fin = pl.pallas_call(
        functools.partial(_rglru_kernel, nseq, seq_len),
        grid=(steps,),
        in_specs=[pl.BlockSpec((t, c), lambda j: (0, j)),
                  pl.BlockSpec((t, c), lambda j: (0, steps + j)),
                  pl.BlockSpec((conv_w.shape[0], c), lambda j: (0, j)),
                  pl.BlockSpec((1, c), lambda j: (0, j)),
                  pl.BlockSpec((1, c, 4 * c), lambda j: (j, 0, 0)),
                  pl.BlockSpec((None, 1, 4 * c), lambda j: (j, 0, 0)),
                  pl.BlockSpec((2, c), lambda j: (0, j)),
                  pl.BlockSpec((2, nseq, c), lambda j: (0, 0, j))],
        out_specs=[pl.BlockSpec((t, c), lambda j: (0, j)),
                   pl.BlockSpec((2, nseq, c), lambda j: (0, 0, j))],
        out_shape=[jax.ShapeDtypeStruct((t, D_RNN), F32),
                   jax.ShapeDtypeStruct((2, nseq, D_RNN), F32)],
        scratch_shapes=[pltpu.VMEM((t + 4 * nseq, c), F32)] + [pltpu.VMEM((t, c), F32)] * 4,
        compiler_params=_cparams("arbitrary"),
        name="rglru_block",
    )(u, u, conv_w, conv_b.reshape(1, -1), wg, bg, lam, h0)
    return y, fin


def _router_kernel(x_ref, g_ref, sc_ref, sh_ref, rw_ref, rb_ref, o_ref):
    h = _normmod(x_ref[...], g_ref[...], sc_ref[0], sh_ref[0])
    logits = lax.dot_general(rw_ref[...], h, (((1,), (1,)), ((), ())),
                             precision=lax.Precision.HIGHEST, preferred_element_type=F32)
    scores = jax.nn.sigmoid(logits)
    sel = scores + rb_ref[...]
    row = [sel[e:e + 1, :] for e in range(N_EXPERTS)]
    gs = []
    for g in range(N_GROUPS):
        r = row[g * EXPERTS_PER_GROUP:(g + 1) * EXPERTS_PER_GROUP]
        best_pair = None
        for i in range(EXPERTS_PER_GROUP):
            for j in range(i + 1, EXPERTS_PER_GROUP):
                s = r[i] + r[j]
                best_pair = s if best_pair is None else jnp.maximum(best_pair, s)
        gs.append(best_pair)
    best = jnp.zeros_like(gs[0], dtype=jnp.int32)
    top = gs[0]
    for g in range(1, N_GROUPS):
        better = gs[g] > top
        best = jnp.where(better, g, best)
        top = jnp.where(better, gs[g], top)
    picked = []
    for e in range(N_EXPERTS):
        g = e // EXPERTS_PER_GROUP
        rank = jnp.zeros_like(best)
        for o in range(g * EXPERTS_PER_GROUP, (g + 1) * EXPERTS_PER_GROUP):
            if o == e:
                continue
            ahead = (row[o] > row[e]) | ((row[o] == row[e]) & (o < e))
            rank = rank + ahead.astype(jnp.int32)
        picked.append((best == g) & (rank < 2))
    den = jnp.zeros_like(gs[0])
    for e in range(N_EXPERTS):
        den = den + jnp.where(picked[e], scores[e:e + 1, :], 0.0)
    gate = [jnp.where(picked[e], scores[e:e + 1, :] / den, 0.0) for e in range(N_EXPERTS)]
    cls = jnp.zeros_like(den)
    w_a = jnp.zeros_like(den)
    w_b = jnp.zeros_like(den)
    for g in range(N_GROUPS):
        for pi, (a, b) in enumerate(MOE_PAIRS):
            ea, eb = g * EXPERTS_PER_GROUP + a, g * EXPERTS_PER_GROUP + b
            both = picked[ea] & picked[eb]
            cls = jnp.where(both, float(g * len(MOE_PAIRS) + pi), cls)
            w_a = jnp.where(both, gate[ea], w_a)
            w_b = jnp.where(both, gate[eb], w_b)
    o_ref[...] = jnp.concatenate([cls, w_a, w_b, jnp.zeros((ROUTE_ROWS - 3, cls.shape[1]), F32)], axis=0)


MOE_PAIRS = ((0, 1), (0, 2), (0, 3), (1, 3), (1, 2), (2, 3))
N_CLS = N_GROUPS * len(MOE_PAIRS)
CLS_PAD = 32
ROUTE_ROWS = 8
MOE_TS = 256
MOE_TM = 256
SLOT_BLK = 512
ROW_W = D_MODEL + LANES


def router_gates(x, g, mod, router_w, router_b, seq_len, per_seq, tm=512):
    t, d = x.shape
    return pl.pallas_call(
        _router_kernel,
        grid=(t // tm,),
        in_specs=[pl.BlockSpec((tm, d), lambda i: (i, 0)),
                  pl.BlockSpec((1, d), lambda i: (0, 0)),
                  _mod_spec(4, tm, seq_len, per_seq),
                  _mod_spec(3, tm, seq_len, per_seq),
                  pl.BlockSpec((N_EXPERTS, d), lambda i: (0, 0)),
                  pl.BlockSpec((N_EXPERTS, 1), lambda i: (0, 0))],
        out_specs=pl.BlockSpec((ROUTE_ROWS, tm), lambda i: (0, i)),
        out_shape=jax.ShapeDtypeStruct((ROUTE_ROWS, t), F32),
        compiler_params=_cparams("arbitrary"),
        name="moe_router",
    )(x, g.reshape(1, d), mod, mod, router_w.T, router_b.reshape(N_EXPERTS, 1))


def _slots_kernel(route_ref, slot_ref, off_ref, cnt_ref, cnt_s, off_s, run_s, earlier_s):
    phase = pl.program_id(0)
    j = pl.program_id(1)
    cls = route_ref[0:1, :]
    cid = lax.broadcasted_iota(jnp.int32, (CLS_PAD, SLOT_BLK), 0).astype(F32)
    member = (cid == cls).astype(F32)
    n_here = jnp.sum(member, axis=1, keepdims=True)

    @pl.when((phase == 0) & (j == 0))
    def _():
        cnt_s[...] = jnp.zeros_like(cnt_s)

    @pl.when(phase == 0)
    def _():
        cnt_s[...] += n_here

    @pl.when((phase == 1) & (j == 0))
    def _():
        padded = jnp.ceil(cnt_s[...] * (1.0 / MOE_TS)) * MOE_TS
        r = lax.broadcasted_iota(jnp.int32, (CLS_PAD, CLS_PAD), 0)
        c = lax.broadcasted_iota(jnp.int32, (CLS_PAD, CLS_PAD), 1)
        before = (c < r).astype(F32)
        off_s[...] = jnp.dot(before, padded, precision=lax.Precision.HIGHEST, preferred_element_type=F32)
        run_s[...] = jnp.zeros_like(run_s)
        off_ref[...] = off_s[...]
        cnt_ref[...] = cnt_s[...]
        tr = lax.broadcasted_iota(jnp.int32, (SLOT_BLK, SLOT_BLK), 0)
        tc = lax.broadcasted_iota(jnp.int32, (SLOT_BLK, SLOT_BLK), 1)
        earlier_s[...] = (tr < tc).astype(BF16)

    @pl.when(phase == 1)
    def _():
        rank = _dot(member.astype(BF16), earlier_s[...])
        base = run_s[:, 0:1] + off_s[:, 0:1]
        slot = jnp.sum(member * (rank + base), axis=0, keepdims=True)
        slot_ref[...] = slot.astype(jnp.int32)
        run_s[...] += n_here


def moe_slots(route):
    t = route.shape[1]
    n_blk = t // SLOT_BLK
    stat = jax.ShapeDtypeStruct((CLS_PAD, LANES), F32)
    stat_spec = pl.BlockSpec((CLS_PAD, LANES), lambda p, j: (0, 0))
    return pl.pallas_call(
        _slots_kernel,
        grid=(2, n_blk),
        in_specs=[pl.BlockSpec((ROUTE_ROWS, SLOT_BLK), lambda p, j: (0, j))],
        out_specs=[pl.BlockSpec((1, SLOT_BLK), lambda p, j: (0, j * p)), stat_spec, stat_spec],
        out_shape=[jax.ShapeDtypeStruct((1, t), jnp.int32), stat, stat],
        scratch_shapes=[pltpu.VMEM((CLS_PAD, LANES), F32)] * 3 + [pltpu.VMEM((SLOT_BLK, SLOT_BLK), BF16)],
        compiler_params=_cparams("arbitrary", "arbitrary"),
        name="moe_slots",
    )(route)


def _tile_maps(off, cnt, n_tiles):
    off = off[:N_CLS, 0].astype(jnp.int32)
    cnt = cnt[:N_CLS, 0].astype(jnp.int32)
    ends = off + ((cnt + MOE_TS - 1) // MOE_TS) * MOE_TS
    n_used = ends[-1] // MOE_TS
    k = jnp.arange(n_tiles, dtype=jnp.int32)
    tix = jnp.minimum(k, n_used - 1)
    cls = jnp.sum((tix[:, None] * MOE_TS >= ends[None, :]).astype(jnp.int32), axis=1)
    pair = jnp.asarray(MOE_PAIRS, jnp.int32)
    grp = (cls // len(MOE_PAIRS)) * EXPERTS_PER_GROUP
    ea = grp + pair[cls % len(MOE_PAIRS), 0]
    eb = grp + pair[cls % len(MOE_PAIRS), 1]
    first = jnp.ones((1,), jnp.int32)
    chg_a = jnp.concatenate([first, (ea[1:] != ea[:-1]).astype(jnp.int32)])
    chg_b = jnp.concatenate([first, (eb[1:] != eb[:-1]).astype(jnp.int32)])
    return ea, eb, tix, chg_a, chg_b, n_used.reshape(1)


def _dispatch_kernel(n_steps, slots_ref, x_ref, g_ref, sc_ref, sh_ref, rt_ref, hs_in, hs_out, rowbuf, sem):
    del hs_in
    i = pl.program_id(0)
    cur = i % 2

    def wait_rows(s):
        pltpu.make_async_copy(rowbuf.at[s], rowbuf.at[s], sem.at[s]).wait()

    @pl.when(i >= 2)
    def _():
        wait_rows(cur)

    rowbuf[cur, :, 0:D_MODEL] = _normmod(x_ref[...], g_ref[...], sc_ref[0], sh_ref[0])
    rowbuf[cur, :, D_MODEL:ROW_W] = jnp.concatenate(
        [rt_ref[...], jnp.zeros((MOE_TM, LANES - ROUTE_ROWS), F32)], axis=1)

    for s in range(2):
        @pl.when(cur == s)
        def _():
            for r in range(MOE_TM):
                dst = slots_ref[i * MOE_TM + r]
                pltpu.make_async_copy(rowbuf.at[s, r], hs_out.at[dst], sem.at[s]).start()

    @pl.when(i == n_steps - 1)
    def _():
        wait_rows(cur)
        if n_steps >= 2:
            wait_rows(1 - cur)


def moe_dispatch(x, g, mod, route_t, slots, hs, seq_len, per_seq):
    t, d = x.shape
    n_steps = t // MOE_TM
    grid_spec = pltpu.PrefetchScalarGridSpec(
        num_scalar_prefetch=1,
        grid=(n_steps,),
        in_specs=[pl.BlockSpec((MOE_TM, d), lambda i, s: (i, 0)),
                  pl.BlockSpec((1, d), lambda i, s: (0, 0)),
                  _mod_spec(4, MOE_TM, seq_len, per_seq),
                  _mod_spec(3, MOE_TM, seq_len, per_seq),
                  pl.BlockSpec((MOE_TM, ROUTE_ROWS), lambda i, s: (i, 0)),
                  pl.BlockSpec(memory_space=pl.ANY)],
        out_specs=pl.BlockSpec(memory_space=pl.ANY),
        scratch_shapes=[pltpu.VMEM((2, MOE_TM, ROW_W), F32), pltpu.SemaphoreType.DMA((2,))],
    )
    return pl.pallas_call(
        functools.partial(_dispatch_kernel, n_steps),
        grid_spec=grid_spec,
        out_shape=jax.ShapeDtypeStruct(hs.shape, F32),
        input_output_aliases={6: 0},
        compiler_params=_cparams("arbitrary"),
        name="moe_dispatch",
    )(slots, x, g.reshape(1, d), mod, mod, route_t, hs)


def _experts_kernel(ea_ref, eb_ref, tix_ref, chga_ref, chgb_ref, nused_ref, hs_ref,
                    wga_ref, wua_ref, wda_ref, wgb_ref, wub_ref, wdb_ref, ys_ref,
                    bga, bua, bda, bgb, bub, bdb):
    k = pl.program_id(0)

    @pl.when(k < nused_ref[0])
    def _():
        @pl.when(chga_ref[k] == 1)
        def _():
            bga[...] = wga_ref[0].astype(BF16)
            bua[...] = wua_ref[0].astype(BF16)
            bda[...] = wda_ref[0].astype(BF16)

        @pl.when(chgb_ref[k] == 1)
        def _():
            bgb[...] = wgb_ref[0].astype(BF16)
            bub[...] = wub_ref[0].astype(BF16)
            bdb[...] = wdb_ref[0].astype(BF16)

        h = hs_ref[:, 0:D_MODEL].astype(BF16)

        def ffn(bg, bu, bd, w):
            hid = _dot(h, bg[...])
            up = _dot(h, bu[...])
            act = (hid * _sigmoid(hid)) * up * w
            return _dot(act.astype(BF16), bd[...])

        ys_ref[...] = (ffn(bga, bua, bda, hs_ref[:, D_MODEL + 1:D_MODEL + 2])
                       + ffn(bgb, bub, bdb, hs_ref[:, D_MODEL + 2:D_MODEL + 3]))

    @pl.when(k >= nused_ref[0])
    def _():
        ys_ref[...] = jnp.zeros_like(ys_ref)


def moe_experts(hs, maps, layer, w_gate, w_up, w_down):
    n_tiles = hs.shape[0] // MOE_TS
    d = D_MODEL

    def w_spec(shape, which):
        return pl.BlockSpec((None, 1) + shape, lambda k, ea, eb, *_: (layer, (ea, eb)[which][k], 0, 0))

    grid_spec = pltpu.PrefetchScalarGridSpec(
        num_scalar_prefetch=6,
        grid=(n_tiles,),
        in_specs=[pl.BlockSpec((MOE_TS, ROW_W), lambda k, ea, eb, tix, *_: (tix[k], 0)),
                  w_spec((d, D_EXPERT), 0), w_spec((d, D_EXPERT), 0), w_spec((D_EXPERT, d), 0),
                  w_spec((d, D_EXPERT), 1), w_spec((d, D_EXPERT), 1), w_spec((D_EXPERT, d), 1)],
        out_specs=pl.BlockSpec((MOE_TS, d), lambda k, *_: (k, 0)),
        scratch_shapes=[pltpu.VMEM((d, D_EXPERT), BF16), pltpu.VMEM((d, D_EXPERT), BF16),
                        pltpu.VMEM((D_EXPERT, d), BF16)] * 2,
    )
    return pl.pallas_call(
        _experts_kernel,
        grid_spec=grid_spec,
        out_shape=jax.ShapeDtypeStruct((hs.shape[0], d), F32),
        compiler_params=_cparams("arbitrary"),
        name="moe_experts",
    )(*maps, hs, w_gate, w_up, w_down, w_gate, w_up, w_down)


def _combine_kernel(final, n_steps, slots_ref, x_ref, g2_ref, fg_ref, ys_hbm, o_ref, gbuf, sem):
    i = pl.program_id(0)
    cur = i % 2

    def issue_tile(tile, s):
        for r in range(MOE_TM):
            src = slots_ref[tile * MOE_TM + r]
            pltpu.make_async_copy(ys_hbm.at[src], gbuf.at[s, r], sem.at[s]).start()

    @pl.when(i == 0)
    def _():
        issue_tile(0, 0)

    for s in range(2):
        @pl.when((i + 1 < n_steps) & (1 - cur == s))
        def _():
            issue_tile(i + 1, s)

    pltpu.make_async_copy(gbuf.at[cur], gbuf.at[cur], sem.at[cur]).wait()
    y = x_ref[...] + g2_ref[0] * gbuf[cur]
    if final:
        ms = jnp.mean(y * y, axis=-1, keepdims=True)
        y = y * lax.rsqrt(ms + EPS) * fg_ref[...]
    o_ref[...] = y


def moe_combine(x, mod, slots, ys, final_g, final, seq_len, per_seq):
    t, d = x.shape
    n_steps = t // MOE_TM
    grid_spec = pltpu.PrefetchScalarGridSpec(
        num_scalar_prefetch=1,
        grid=(n_steps,),
        in_specs=[pl.BlockSpec((MOE_TM, d), lambda i, s: (i, 0)),
                  _mod_spec(5, MOE_TM, seq_len, per_seq),
                  pl.BlockSpec((1, d), lambda i, s: (0, 0)),
                  pl.BlockSpec(memory_space=pl.ANY)],
        out_specs=pl.BlockSpec((MOE_TM, d), lambda i, s: (i, 0)),
        scratch_shapes=[pltpu.VMEM((2, MOE_TM, d), F32), pltpu.SemaphoreType.DMA((2,))],
    )
    return pl.pallas_call(
        functools.partial(_combine_kernel, final, n_steps),
        grid_spec=grid_spec,
        out_shape=jax.ShapeDtypeStruct((t, d), F32),
        compiler_params=_cparams("arbitrary"),
        name="moe_combine",
    )(slots, x, mod, final_g.reshape(1, d), ys)


def moe_block(xs, mods_l, per_seqs, seq_lens, layer, p, final):
    g = p['norm_g'][layer, 1]
    routes = [router_gates(x, g, mods_l, p['router_w'], p['router_b'], sl, ps)
              for x, sl, ps in zip(xs, seq_lens, per_seqs)]
    slots, off, cnt = moe_slots(jnp.concatenate(routes, axis=1))
    t_all = slots.shape[1]
    n_tiles = t_all // MOE_TS + N_CLS
    maps = _tile_maps(off, cnt, n_tiles)
    hs = jnp.zeros((n_tiles * MOE_TS, ROW_W), F32)
    bounds = np.cumsum([0] + [x.shape[0] for x in xs])
    group_slots = [slots[0, bounds[i]:bounds[i + 1]] for i in range(len(xs))]
    for x, r, s, sl, ps in zip(xs, routes, group_slots, seq_lens, per_seqs):
        hs = moe_dispatch(x, g, mods_l, r.T, s, hs, sl, ps)
    ys = moe_experts(hs, maps, layer, p['moe_w_gate'], p['moe_w_up'], p['moe_w_down'])
    return [moe_combine(x, mods_l, s, ys, p['final_g'], final, sl, ps)
            for x, s, sl, ps in zip(xs, group_slots, seq_lens, per_seqs)]


def _mixer(x, group, l, mod, p, hy_tables):
    per_seq, nseq, seq_len = group['per_seq'], group['nseq'], group['seq_len']
    extras = None
    if l % 2 == 0:
        e = l // 2
        u = nm_matmul(x, p['norm_g'][l, 0], mod, p['a_in_w'][e].astype(BF16), seq_len, per_seq)
        if group['ctx_k'] is None:
            attn, nk, nv = context_attention(u, nseq, seq_len)
            extras = (nk, nv)
        else:
            attn = neighbourhood_attention(u, group['ctx_k'][:, e], group['ctx_v'][:, e], p['na_rpb'][e],
                                           nseq, seq_len)
        fwd, inv = hy_tables[seq_len]
        spectrum = hyena_spectrum(seq_len, p['hy_w1'][e], p['hy_b1'][e], p['hy_w2'][e], p['hy_b2'][e],
                                  p['hy_w3'][e], p['hy_freq'][e], p['hy_d'][e], fwd)
        hy = hyena_mixer(u, nseq, seq_len, p['hy_short_w'][e], p['hy_short_b'][e], spectrum, fwd, inv)
        w_out = p['a_out_w'][e].astype(BF16)
        x = proj_residual([attn, hy], [w_out[:D_A], w_out[D_A:]], x, mod, seq_len, per_seq)
    else:
        o = l // 2
        u = nm_matmul_tm(x, p['norm_g'][l, 0], mod, p['c_in_w'][o].astype(BF16), nseq, seq_len, per_seq)
        y, extras = rglru_block(u, nseq, seq_len, p['rg_conv_w'][o], p['rg_conv_b'][o], p['rg_wa'][o],
                                p['rg_ba'][o], p['rg_wx'][o], p['rg_bx'][o], p['rg_lam'][o], group['h0'][o])
        x = proj_residual_tm(y, p['c_out_w'][o].astype(BF16), x, mod, nseq, seq_len, per_seq)
    return x, extras


def kernel(x_prompt, x_sample, cache_k, cache_v, state_h, c, c_ctx, norm_g, ada_w, ada_b, final_g, a_in_w, a_out_w, na_rpb, hy_short_w, hy_short_b, hy_w1, hy_b1, hy_w2, hy_b2, hy_w3, hy_freq, hy_d, c_in_w, c_out_w, rg_conv_w, rg_conv_b, rg_wa, rg_ba, rg_wx, rg_bx, rg_lam, router_w, router_b, moe_w_gate, moe_w_up, moe_w_down):
    p = dict(norm_g=norm_g, final_g=final_g, a_in_w=a_in_w, a_out_w=a_out_w, na_rpb=na_rpb,
             hy_short_w=hy_short_w, hy_short_b=hy_short_b, hy_w1=hy_w1, hy_b1=hy_b1, hy_w2=hy_w2,
             hy_b2=hy_b2, hy_w3=hy_w3, hy_freq=hy_freq, hy_d=hy_d, c_in_w=c_in_w, c_out_w=c_out_w,
             rg_conv_w=rg_conv_w, rg_conv_b=rg_conv_b, rg_wa=rg_wa, rg_ba=rg_ba, rg_wx=rg_wx, rg_bx=rg_bx,
             rg_lam=rg_lam, router_w=router_w, router_b=router_b, moe_w_gate=moe_w_gate,
             moe_w_up=moe_w_up, moe_w_down=moe_w_down)
    batch, seq, d = x_prompt.shape
    dec_batch, dec_seq, _ = x_sample.shape
    n_odd = DEPTH // 2
    assert 1 + dec_batch <= MOD_ROWS

    cond = jnp.concatenate([c_ctx[None, :], c, jnp.zeros((MOD_ROWS - 1 - dec_batch, d), F32)], axis=0)
    m = modulation(cond, ada_w, ada_b)
    mods = [m[l].reshape(MOD_ROWS * N_MOD, 1, d) for l in range(DEPTH)]

    tables = {}
    for sl in (seq, dec_seq):
        fwd, inv = _dft_tables(sl)
        tables[sl] = (jnp.asarray(fwd).astype(BF16), jnp.asarray(inv).astype(BF16))

    groups = [
        dict(per_seq=False, nseq=batch, seq_len=seq, ctx_k=None, ctx_v=None,
             h0=[jnp.zeros((2, batch, D_RNN), F32)] * n_odd),
        dict(per_seq=True, nseq=dec_batch, seq_len=dec_seq, ctx_k=cache_k, ctx_v=cache_v,
             h0=[state_h[:, o].transpose(1, 0, 2) for o in range(n_odd)]),
    ]
    xs = [x_prompt.reshape(batch * seq, d), x_sample.reshape(dec_batch * dec_seq, d)]
    k_list, v_list, h_list = [], [], []
    for l in range(DEPTH):
        mixed = [_mixer(x, grp, l, mods[l], p, tables) for x, grp in zip(xs, groups)]
        if l % 2 == 0:
            k_list.append(mixed[0][1][0])
            v_list.append(mixed[0][1][1])
        else:
            h_list.append(mixed[0][1].transpose(1, 0, 2))
        xs = moe_block([mx[0] for mx in mixed], mods[l], [grp['per_seq'] for grp in groups],
                       [grp['seq_len'] for grp in groups], l, p, l == DEPTH - 1)
    new_k = jnp.stack(k_list, axis=1)
    new_v = jnp.stack(v_list, axis=1)
    new_h = jnp.stack(h_list, axis=1)
    return (xs[0].reshape(batch, seq, d), xs[1].reshape(dec_batch, dec_seq, d), new_k, new_v, new_h)
```

```python
import functools
import math

import numpy as np
import jax
import jax.numpy as jnp
from jax import lax
from jax.experimental import pallas as pl
from jax.experimental.pallas import tpu as pltpu

F32 = jnp.float32
BF16 = jnp.bfloat16

D_MODEL = 1024
DEPTH = 2
GRID_W = 64
EPS = 1e-6
NEG_INF = -1e30
NA_HEADS = 8
HEAD_DIM = 64
D_A = NA_HEADS * HEAD_DIM
WIN_ROWS = 8
WIN_COLS = 16
D_B = D_MODEL - D_A
HY_ORDER = 2
HY_EMB = 33
HY_BANDS = (HY_EMB - 1) // 2
HY_FFN = 64
HY_DECAY_TARGET = 1e-2
HY_FAST_PCT = 0.3
HY_SLOW_PCT = 1.5
D_RNN = D_MODEL
RG_BLOCK = 64
RG_C = 8.0
N_EXPERTS = 16
N_GROUPS = 4
EXPERTS_PER_GROUP = N_EXPERTS // N_GROUPS
D_EXPERT = 512

LANES = 128
VMEM_LIMIT = 56 * 1024 * 1024
N_MOD = 6
MOD_ROWS = 16


def _cparams(*sem):
    return pltpu.CompilerParams(dimension_semantics=sem, vmem_limit_bytes=VMEM_LIMIT)


def _dot(a, b):
    return jnp.dot(a, b, preferred_element_type=F32)


def _dot_nt(a, b):
    return lax.dot_general(a, b, (((1,), (1,)), ((), ())), preferred_element_type=F32)


def _sigmoid(x):
    return 0.5 * jnp.tanh(0.5 * x) + 0.5


def _normmod(x, g, sc, sh):
    ms = jnp.mean(x * x, axis=-1, keepdims=True)
    return (x * lax.rsqrt(ms + EPS) * g) * (1.0 + sc) + sh


def _mod_spec(chunk, tm, seq_len, per_seq):
    if per_seq:
        return pl.BlockSpec((1, 1, D_MODEL), lambda i, *_: ((1 + (i * tm) // seq_len) * N_MOD + chunk, 0, 0))
    return pl.BlockSpec((1, 1, D_MODEL), lambda i, *_: (chunk, 0, 0))


def _mod_kernel(c_ref, w_ref, b_ref, o_ref):
    s = c_ref[...]
    s = s * jax.nn.sigmoid(s)
    o_ref[0] = _dot(s.astype(BF16), w_ref[0].astype(BF16)) + b_ref[0]


def modulation(cond, ada_w, ada_b):
    tn = 1536
    n = ada_w.shape[-1]
    return pl.pallas_call(
        _mod_kernel,
        grid=(DEPTH, n // tn),
        in_specs=[pl.BlockSpec((MOD_ROWS, D_MODEL), lambda l, j: (0, 0)),
                  pl.BlockSpec((1, D_MODEL, tn), lambda l, j: (l, 0, j)),
                  pl.BlockSpec((1, 1, tn), lambda l, j: (l, 0, j))],
        out_specs=pl.BlockSpec((1, MOD_ROWS, tn), lambda l, j: (l, 0, j)),
        out_shape=jax.ShapeDtypeStruct((DEPTH, MOD_ROWS, n), F32),
        compiler_params=_cparams("arbitrary", "arbitrary"),
        name="modulation",
    )(cond, ada_w, ada_b.reshape(DEPTH, 1, n))


def _nm_matmul_kernel(x_ref, g_ref, sc_ref, sh_ref, w_ref, o_ref):
    h = _normmod(x_ref[...], g_ref[...], sc_ref[0], sh_ref[0])
    o_ref[...] = _dot(h.astype(BF16), w_ref[...])


def nm_matmul(x, g, mod, w, seq_len, per_seq, tm=512):
    t, d = x.shape
    n = w.shape[1]
    return pl.pallas_call(
        _nm_matmul_kernel,
        grid=(t // tm,),
        in_specs=[pl.BlockSpec((tm, d), lambda i: (i, 0)),
                  pl.BlockSpec((1, d), lambda i: (0, 0)),
                  _mod_spec(1, tm, seq_len, per_seq),
                  _mod_spec(0, tm, seq_len, per_seq),
                  pl.BlockSpec((d, n), lambda i: (0, 0))],
        out_specs=pl.BlockSpec((tm, n), lambda i: (i, 0)),
        out_shape=jax.ShapeDtypeStruct((t, n), F32),
        compiler_params=_cparams("arbitrary"),
        name="norm_mod_proj",
    )(x, g.reshape(1, d), mod, mod, w)


def _proj_res_kernel(n_act, *refs):
    acts = refs[:n_act]
    ws = refs[n_act:2 * n_act]
    x_ref, g_ref, o_ref = refs[2 * n_act:]
    acc = _dot(acts[0][...].astype(BF16), ws[0][...])
    for a, w in zip(acts[1:], ws[1:]):
        acc += _dot(a[...].astype(BF16), w[...])
    o_ref[...] = x_ref[...] + g_ref[0] * acc


def proj_residual(acts, ws, x, mod, seq_len, per_seq, tm=512):
    t, d = x.shape
    in_specs = [pl.BlockSpec((tm, a.shape[1]), lambda i: (i, 0)) for a in acts]
    in_specs += [pl.BlockSpec(w.shape, lambda i: (0, 0)) for w in ws]
    in_specs += [pl.BlockSpec((tm, d), lambda i: (i, 0)), _mod_spec(2, tm, seq_len, per_seq)]
    return pl.pallas_call(
        functools.partial(_proj_res_kernel, len(acts)),
        grid=(t // tm,),
        in_specs=in_specs,
        out_specs=pl.BlockSpec((tm, d), lambda i: (i, 0)),
        out_shape=jax.ShapeDtypeStruct((t, d), F32),
        compiler_params=_cparams("arbitrary"),
        name="proj_residual",
    )(*acts, *ws, x, mod)


def _row_permutation(nseq, steps, to_time_major):
    n = nseq * steps
    i = lax.broadcasted_iota(jnp.int32, (n, n), 0)
    j = lax.broadcasted_iota(jnp.int32, (n, n), 1)
    if to_time_major:
        src = (i % nseq) * steps + i // nseq
    else:
        src = (i % steps) * nseq + i // steps
    return (j == src).astype(BF16)


def _nm_matmul_tm_kernel(nseq, steps, x_ref, g_ref, sc_ref, sh_ref, w_ref, o_ref):
    h = _normmod(x_ref[...], g_ref[...], sc_ref[...], sh_ref[...])
    h = h.reshape(nseq * steps, h.shape[-1]).astype(BF16)
    h = _dot(_row_permutation(nseq, steps, True), h).astype(BF16)
    o_ref[...] = _dot(h, w_ref[...])


def _group_mod(mod, chunk, nseq, per_seq):
    rows = mod.reshape(MOD_ROWS, N_MOD, 1, D_MODEL)
    return rows[1:1 + nseq, chunk] if per_seq else rows[0:1, chunk]


def nm_matmul_tm(x, g, mod, w, nseq, seq_len, per_seq):
    t, d = x.shape
    n = w.shape[1]
    steps = TM_ROWS // nseq
    sc = _group_mod(mod, 1, nseq, per_seq)
    sh = _group_mod(mod, 0, nseq, per_seq)
    mod_spec = pl.BlockSpec(sc.shape, lambda i: (0, 0, 0))
    return pl.pallas_call(
        functools.partial(_nm_matmul_tm_kernel, nseq, steps),
        grid=(seq_len // steps,),
        in_specs=[pl.BlockSpec((nseq, steps, d), lambda i: (0, i, 0)),
                  pl.BlockSpec((1, d), lambda i: (0, 0)),
                  mod_spec, mod_spec,
                  pl.BlockSpec((d, n), lambda i: (0, 0))],
        out_specs=pl.BlockSpec((TM_ROWS, n), lambda i: (i, 0)),
        out_shape=jax.ShapeDtypeStruct((t, n), F32),
        compiler_params=_cparams("arbitrary"),
        name="norm_mod_proj_tm",
    )(x.reshape(nseq, seq_len, d), g.reshape(1, d), sc, sh, w)


def _proj_res_tm_kernel(nseq, steps, y_ref, w_ref, x_ref, g_ref, o_ref):
    y = _dot(_row_permutation(nseq, steps, False), y_ref[...].astype(BF16)).astype(BF16)
    acc = _dot(y, w_ref[...])
    o_ref[...] = x_ref[...] + g_ref[...] * acc.reshape(nseq, steps, acc.shape[-1])


def proj_residual_tm(y, w, x, mod, nseq, seq_len, per_seq):
    t, d = x.shape
    steps = TM_ROWS // nseq
    g1 = _group_mod(mod, 2, nseq, per_seq)
    out = pl.pallas_call(
        functools.partial(_proj_res_tm_kernel, nseq, steps),
        grid=(seq_len // steps,),
        in_specs=[pl.BlockSpec((TM_ROWS, y.shape[1]), lambda i: (i, 0)),
                  pl.BlockSpec(w.shape, lambda i: (0, 0)),
                  pl.BlockSpec((nseq, steps, d), lambda i: (0, i, 0)),
                  pl.BlockSpec(g1.shape, lambda i: (0, 0, 0))],
        out_specs=pl.BlockSpec((nseq, steps, d), lambda i: (0, i, 0)),
        out_shape=jax.ShapeDtypeStruct((nseq, seq_len, d), F32),
        compiler_params=_cparams("arbitrary"),
        name="proj_residual_tm",
    )(y, w, x.reshape(nseq, seq_len, d), g1)
    return out.reshape(t, d)


def _ctx_attn_kernel(q_ref, k_ref, v_ref, o_ref, nk_ref, nv_ref):
    scale = HEAD_DIM ** -0.5
    for h in range(NA_HEADS):
        sl = slice(h * HEAD_DIM, (h + 1) * HEAD_DIM)
        q = q_ref[:, sl]
        k = k_ref[:, sl]
        v = v_ref[:, sl]
        nk_ref[0, h] = k
        nv_ref[0, h] = v
        s = _dot_nt(q.astype(BF16), k.astype(BF16)) * scale
        m = jnp.max(s, axis=-1, keepdims=True)
        p = jnp.exp(s - m)
        den = jnp.sum(p, axis=-1, keepdims=True)
        o_ref[:, sl] = _dot(p.astype(BF16), v.astype(BF16)) / den


def context_attention(u, nseq, seq_len):
    t = u.shape[0]
    kv_shape = jax.ShapeDtypeStruct((nseq, NA_HEADS, seq_len, HEAD_DIM), F32)
    kv_spec = pl.BlockSpec((1, NA_HEADS, seq_len, HEAD_DIM), lambda b: (b, 0, 0, 0))
    return pl.pallas_call(
        _ctx_attn_kernel,
        grid=(nseq,),
        in_specs=[pl.BlockSpec((seq_len, D_A), lambda b: (b, 0)),
                  pl.BlockSpec((seq_len, D_A), lambda b: (b, 1)),
                  pl.BlockSpec((seq_len, D_A), lambda b: (b, 2))],
        out_specs=[pl.BlockSpec((seq_len, D_A), lambda b: (b, 0)), kv_spec, kv_spec],
        out_shape=[jax.ShapeDtypeStruct((t, D_A), F32), kv_shape, kv_shape],
        compiler_params=_cparams("arbitrary"),
        name="context_attention",
    )(u, u, u)


N_DR = 2 * WIN_ROWS - 1
N_DC = 2 * WIN_COLS - 1


def _na_col_tables():
    cols = np.arange(GRID_W)
    col_start = np.clip(cols - WIN_COLS // 2, 0, GRID_W - WIN_COLS)
    col_in = (cols[None, :] >= col_start[:, None]) & (cols[None, :] < col_start[:, None] + WIN_COLS)
    dc = np.clip(cols[None, :] - cols[:, None], 1 - WIN_COLS, WIN_COLS - 1) + WIN_COLS - 1
    onehot = (dc.reshape(1, -1) == np.arange(32)[:, None]).astype(np.float32)
    return onehot, col_in.reshape(1, -1).astype(np.float32)


def _na_bias_kernel(r_ref, e_ref, m_ref, o_ref):
    t = jnp.dot(r_ref[...], e_ref[...], precision=lax.Precision.HIGHEST, preferred_element_type=F32)
    o_ref[...] = jnp.where(m_ref[...] > 0.0, t, NEG_INF)


def na_bias_table(rpb):
    onehot, col_in = _na_col_tables()
    n_rows = NA_HEADS * N_DR
    r = jnp.zeros((LANES, 32), F32).at[:n_rows, :N_DC].set(rpb.reshape(n_rows, N_DC).astype(F32))
    t = pl.pallas_call(
        _na_bias_kernel,
        out_shape=jax.ShapeDtypeStruct((LANES, GRID_W * GRID_W), F32),
        name="na_bias_table",
    )(r, jnp.asarray(onehot), jnp.asarray(col_in))
    t = t[:n_rows].reshape(NA_HEADS, N_DR, GRID_W, GRID_W)
    return jnp.concatenate([t[:, :-1], t[:, 1:]], axis=-1)


def _na_kernel(rows, q_ref, k_ref, v_ref, ck_ref, cv_ref, bias_ref, o_ref,
               q_s, k_s, v_s, ck_s, cv_s, s_s, p_s, den_s, o_s):
    scale = HEAD_DIM ** -0.5
    n_lat = WIN_ROWS * GRID_W
    for h in range(NA_HEADS):
        sl = slice(h * HEAD_DIM, (h + 1) * HEAD_DIM)
        q_s[h] = (q_ref[:, sl] * scale).astype(BF16)
        k_s[h] = k_ref[:, sl].astype(BF16)
        v_s[h] = v_ref[:, sl].astype(BF16)
    ck_s[...] = ck_ref[0].astype(BF16)
    cv_s[...] = cv_ref[0].astype(BF16)

    def window(r):
        start = min(max(r - WIN_ROWS // 2, 0), rows - WIN_ROWS)
        return start, start - r + WIN_ROWS - 1

    def head_body(h, carry):
        for r in range(rows):
            start, off = window(r)
            q = q_s[h, r * GRID_W:(r + 1) * GRID_W, :]
            bias = jnp.concatenate([bias_ref[h, off + 2 * i] for i in range(WIN_ROWS // 2)], axis=1)
            s_s[r * GRID_W:(r + 1) * GRID_W, 0:n_lat] = (
                _dot_nt(q, k_s[h, start * GRID_W:start * GRID_W + n_lat, :]) + bias)
            s_s[r * GRID_W:(r + 1) * GRID_W, n_lat:] = _dot_nt(q, ck_s[h])
        for r in range(rows):
            rs = slice(r * GRID_W, (r + 1) * GRID_W)
            s = s_s[rs, :]
            p = jnp.exp(s - jnp.max(s, axis=-1, keepdims=True))
            den_s[rs, :] = jnp.sum(p, axis=-1, keepdims=True)
            p_s[rs, :] = p.astype(BF16)
        for r in range(rows):
            start, _ = window(r)
            rs = slice(r * GRID_W, (r + 1) * GRID_W)
            o = (_dot(p_s[rs, 0:n_lat], v_s[h, start * GRID_W:start * GRID_W + n_lat, :])
                 + _dot(p_s[rs, n_lat:], cv_s[h]))
            o_s[h, rs, :] = o / den_s[rs, :]
        return carry

    lax.fori_loop(0, NA_HEADS, head_body, 0)
    for h in range(NA_HEADS):
        o_ref[:, h * HEAD_DIM:(h + 1) * HEAD_DIM] = o_s[h]


def neighbourhood_attention(u, ctx_k, ctx_v, rpb, nseq, seq_len):
    t = u.shape[0]
    rows = seq_len // GRID_W
    assert rows >= WIN_ROWS and WIN_ROWS % 2 == 0
    past = ctx_k.shape[2]
    bias = na_bias_table(rpb)
    ctx_spec = pl.BlockSpec((1, NA_HEADS, past, HEAD_DIM), lambda b: (b, 0, 0, 0))
    return pl.pallas_call(
        functools.partial(_na_kernel, rows),
        grid=(nseq,),
        in_specs=[pl.BlockSpec((seq_len, D_A), lambda b: (b, 0)),
                  pl.BlockSpec((seq_len, D_A), lambda b: (b, 1)),
                  pl.BlockSpec((seq_len, D_A), lambda b: (b, 2)),
                  ctx_spec, ctx_spec,
                  pl.BlockSpec(bias.shape, lambda b: (0, 0, 0, 0))],
        out_specs=pl.BlockSpec((seq_len, D_A), lambda b: (b, 0)),
        out_shape=jax.ShapeDtypeStruct((t, D_A), F32),
        scratch_shapes=[pltpu.VMEM((NA_HEADS, seq_len, HEAD_DIM), BF16)] * 3
        + [pltpu.VMEM((NA_HEADS, past, HEAD_DIM), BF16)] * 2
        + [pltpu.VMEM((seq_len, WIN_ROWS * GRID_W + past), F32),
           pltpu.VMEM((seq_len, WIN_ROWS * GRID_W + past), BF16),
           pltpu.VMEM((seq_len, 1), F32),
           pltpu.VMEM((NA_HEADS, seq_len, HEAD_DIM), F32)],
        compiler_params=_cparams("arbitrary"),
        name="neighbourhood_attention",
    )(u, u, u, ctx_k, ctx_v, bias)


def _dft_tables(seq_len):
    n = 2 * seq_len
    f = np.arange(seq_len, dtype=np.int64)
    ang = (np.outer(f, f) % n).astype(np.float64) * (math.pi / seq_len)
    cos, sin = np.cos(ang), np.sin(ang)
    alt = np.where(f % 2 == 0, 1.0, -1.0)
    s_fwd = -sin
    s_fwd[0, :] = alt
    fwd = np.concatenate([cos, s_fwd], axis=0)
    wf = np.where(f == 0, 1.0, 2.0) / n
    ci = cos.T * wf[None, :]
    si = -sin.T * wf[None, :]
    si[:, 0] = alt / n
    inv = np.concatenate([ci, si], axis=1)
    return fwd.astype(np.float32), inv.astype(np.float32)


def _hyena_feats(seq_len):
    t = np.linspace(0.0, 1.0, seq_len, dtype=np.float32)[:, None]
    w = (2.0 * math.pi * np.arange(seq_len, dtype=np.float32)[:, None] / seq_len).astype(np.float32)
    f = np.linspace(1e-4, HY_BANDS - 1, HY_BANDS, dtype=np.float32)[None, :]
    z = np.concatenate([t, np.cos(f * w), -np.sin(f * w)], axis=-1).astype(np.float32)
    max_decay = math.log(HY_DECAY_TARGET) / HY_FAST_PCT
    min_decay = math.log(HY_DECAY_TARGET) / HY_SLOW_PCT
    deltas = np.abs(np.linspace(min_decay, max_decay, D_B, dtype=np.float32))[None, :]
    return z, t, deltas


def _hy_filter_kernel(seq_len, z_ref, t_ref, dl_ref, w1_ref, b1_ref, w2_ref, b2_ref, w3_ref, fr_ref,
                      d_ref, fwd_ref, g_ref):
    hp = lax.Precision.HIGHEST
    h = jnp.sin(fr_ref[0:1, :] * (jnp.dot(z_ref[...], w1_ref[...], precision=hp) + b1_ref[...]))
    h = jnp.sin(fr_ref[1:2, :] * (jnp.dot(h, w2_ref[...], precision=hp) + b2_ref[...]))
    h = jnp.dot(h, w3_ref[...], precision=hp)
    decay = jnp.exp(-t_ref[...] * dl_ref[...])
    row0 = lax.broadcasted_iota(jnp.int32, (seq_len, D_B), 0) == 0
    sums, diffs = [], []
    for n in range(HY_ORDER):
        hf = h[:, (2 * n) * D_B:(2 * n + 1) * D_B] * decay
        hb = h[:, (2 * n + 1) * D_B:(2 * n + 2) * D_B] * decay
        gp = jnp.where(row0, hf + hb + d_ref[n:n + 1, :], hf)
        gm = jnp.where(row0, 0.0, hb)
        sums.append(gp + gm)
        diffs.append(gp - gm)
    rhs = jnp.concatenate(sums + diffs, axis=1).astype(BF16)
    spec = _dot(fwd_ref[...], rhs)
    for n in range(HY_ORDER):
        a = spec[:, n * D_B:(n + 1) * D_B]
        b = spec[:, (HY_ORDER + n) * D_B:(HY_ORDER + n + 1) * D_B]
        g_ref[n, 0:seq_len, :] = a[0:seq_len]
        g_ref[n, seq_len:, :] = jnp.where(row0, a[seq_len:], b[seq_len:])


def hyena_spectrum(seq_len, w1, b1, w2, b2, w3, freq, d, fwd):
    z, t, deltas = _hyena_feats(seq_len)
    return pl.pallas_call(
        functools.partial(_hy_filter_kernel, seq_len),
        out_shape=jax.ShapeDtypeStruct((HY_ORDER, 2 * seq_len, D_B), F32),
        compiler_params=pltpu.CompilerParams(vmem_limit_bytes=VMEM_LIMIT),
        name="hyena_spectrum",
    )(jnp.asarray(z), jnp.asarray(t), jnp.asarray(deltas), w1, b1.reshape(1, -1), w2, b2.reshape(1, -1),
      w3, freq, d, fwd)


def _hyena_kernel(seq_len, u_ref, sw_ref, sb_ref, g_ref, fwd_ref, inv_ref, o_ref):
    u = u_ref[...]
    t_idx = lax.broadcasted_iota(jnp.int32, u.shape, 0)
    prev = jnp.where(t_idx == 0, 0.0, pltpu.roll(u, 1, axis=0))
    nxt = jnp.where(t_idx == seq_len - 1, 0.0, pltpu.roll(u, seq_len - 1, axis=0))
    u = prev * sw_ref[0:1, :] + u * sw_ref[1:2, :] + nxt * sw_ref[2:3, :] + sb_ref[...]
    row0 = lax.broadcasted_iota(jnp.int32, (seq_len, D_B), 0) == 0
    z = u[:, 0:D_B]
    for n in range(HY_ORDER):
        spec = _dot(fwd_ref[...], z.astype(BF16))
        ure, uim = spec[0:seq_len], spec[seq_len:]
        gre, gim = g_ref[n, 0:seq_len, :], g_ref[n, seq_len:, :]
        pim = uim * gim
        yre = ure * gre - jnp.where(row0, 0.0, pim)
        yim = jnp.where(row0, pim, ure * gim + uim * gre)
        y = jnp.concatenate([yre, yim], axis=0).astype(BF16)
        z = u[:, (n + 1) * D_B:(n + 2) * D_B] * _dot(inv_ref[...], y)
    o_ref[...] = z


def hyena_mixer(u, nseq, seq_len, short_w, short_b, spectrum, fwd, inv):
    t = u.shape[0]
    width = (HY_ORDER + 1) * D_B
    col_block = (3 * D_A) // width
    assert col_block * width == 3 * D_A
    return pl.pallas_call(
        functools.partial(_hyena_kernel, seq_len),
        grid=(nseq,),
        in_specs=[pl.BlockSpec((seq_len, width), lambda b: (b, col_block)),
                  pl.BlockSpec(short_w.shape, lambda b: (0, 0)),
                  pl.BlockSpec((1, width), lambda b: (0, 0)),
                  pl.BlockSpec(spectrum.shape, lambda b: (0, 0, 0)),
                  pl.BlockSpec(fwd.shape, lambda b: (0, 0)),
                  pl.BlockSpec(inv.shape, lambda b: (0, 0))],
        out_specs=pl.BlockSpec((seq_len, D_B), lambda b: (b, 0)),
        out_shape=jax.ShapeDtypeStruct((t, D_B), F32),
        compiler_params=_cparams("arbitrary"),
        name="hyena_mixer",
    )(u, short_w, short_b.reshape(1, width), spectrum, fwd, inv)


RG_CB = LANES
RG_CHUNK = 512
TM_ROWS = 256


def _rglru_kernel(nseq, seq_len, gate_ref, xr_ref, cw_ref, cb_ref, wg_ref, bg_ref, lam_ref, h0_ref,
                  y_ref, fin_ref, xp_ref, a_f, b_f, a_b, b_b):
    t_tot = nseq * seq_len
    c = RG_CB
    pad = 2 * nseq
    xp_ref[0:pad, :] = jnp.zeros((pad, c), F32)
    xp_ref[pad + t_tot:, :] = jnp.zeros((pad, c), F32)
    xp_ref[pad:pad + t_tot, :] = xr_ref[...]
    nl = -lam_ref[...]
    sp = jnp.maximum(nl, 0.0) + jnp.log1p(jnp.exp(-jnp.abs(nl)))

    def gate_chunk(ci, carry):
        r0 = pl.multiple_of(ci * RG_CHUNK, RG_CHUNK)
        xc = xp_ref[pl.ds(r0, RG_CHUNK), :] * cw_ref[0:1, :]
        for j in range(1, cw_ref.shape[0]):
            xc = xc + xp_ref[pl.ds(r0 + j * nseq, RG_CHUNK), :] * cw_ref[j:j + 1, :]
        xc = xc + cb_ref[...]
        gts = _dot(xc.astype(BF16), wg_ref[0]) + bg_ref[...]
        for d, (a_ref, b_ref) in enumerate(((a_f, b_f), (a_b, b_b))):
            r = _sigmoid(gts[:, (2 * d) * c:(2 * d + 1) * c])
            i = _sigmoid(gts[:, (2 * d + 1) * c:(2 * d + 2) * c])
            a = jnp.exp(-RG_C * r * sp[d:d + 1, :])
            a_ref[pl.ds(r0, RG_CHUNK), :] = a
            b_ref[pl.ds(r0, RG_CHUNK), :] = jnp.sqrt(jnp.maximum(1.0 - a * a, 0.0)) * (i * xc)
        return carry

    lax.fori_loop(0, t_tot // RG_CHUNK, gate_chunk, 0)

    def scan_step(t, carry):
        hf, hb = carry
        rows_f = pl.ds(pl.multiple_of(t * nseq, nseq), nseq)
        rows_b = pl.ds(pl.multiple_of((seq_len - 1 - t) * nseq, nseq), nseq)
        hf = a_f[rows_f, :] * hf + b_f[rows_f, :]
        hb = a_b[rows_b, :] * hb + b_b[rows_b, :]
        b_f[rows_f, :] = hf
        b_b[rows_b, :] = hb
        return hf, hb

    hf, hb = lax.fori_loop(0, seq_len, scan_step, (h0_ref[0], h0_ref[1]), unroll=8)
    fin_ref[0] = hf
    fin_ref[1] = hb

    def out_chunk(ci, carry):
        rs = pl.ds(pl.multiple_of(ci * RG_CHUNK, RG_CHUNK), RG_CHUNK)
        y_ref[rs, :] = (b_f[rs, :] + b_b[rs, :]) * jax.nn.gelu(gate_ref[rs, :])
        return carry

    lax.fori_loop(0, t_tot // RG_CHUNK, out_chunk, 0)


def _rg_gate_weights(wa, wx):
    per_step = RG_CB // RG_BLOCK
    steps = D_RNN // RG_CB
    mats = []
    for d in range(2):
        for w in (wa[d], wx[d]):
            w = w.reshape(steps, per_step, RG_BLOCK, RG_BLOCK)
            eye = jnp.eye(per_step, dtype=w.dtype)
            m = jnp.einsum('spde,pq->spdqe', w, eye).reshape(steps, RG_CB, RG_CB)
            mats.append(m)
    return jnp.concatenate(mats, axis=-1).astype(BF16)


def rglru_block(u, nseq, seq_len, conv_w, conv_b, wa, ba, wx, bx, lam, h0):
    t = u.shape[0]
    c = RG_CB
    steps = D_RNN // c
    wg = _rg_gate_weights(wa, wx)
    bg = jnp.stack([ba[0], bx[0], ba[1], bx[1]], axis=0).reshape(4, steps, c)
    bg = bg.transpose(1, 0, 2).reshape(steps, 1, 4 * c)
    y, fin = pl.pallas_call(
        functools.partial(_rglru_kernel, nseq, seq_len),
        grid=(steps,),
        in_specs=[pl.BlockSpec((t, c), lambda j: (0, j)),
                  pl.BlockSpec((t, c), lambda j: (0, steps + j)),
                  pl.BlockSpec((conv_w.shape[0], c), lambda j: (0, j)),
                  pl.BlockSpec((1, c), lambda j: (0, j)),
                  pl.BlockSpec((1, c, 4 * c), lambda j: (j, 0, 0)),
                  pl.BlockSpec((None, 1, 4 * c), lambda j: (j, 0, 0)),
                  pl.BlockSpec((2, c), lambda j: (0, j)),
                  pl.BlockSpec((2, nseq, c), lambda j: (0, 0, j))],
        out_specs=[pl.BlockSpec((t, c), lambda j: (0, j)),
                   pl.BlockSpec((2, nseq, c), lambda j: (0, 0, j))],
        out_shape=[jax.ShapeDtypeStruct((t, D_RNN), F32),
                   jax.ShapeDtypeStruct((2, nseq, D_RNN), F32)],
        scratch_shapes=[pltpu.VMEM((t + 4 * nseq, c), F32)] + [pltpu.VMEM((t, c), F32)] * 4,
        compiler_params=_cparams("arbitrary"),
        name="rglru_block",
    )(u, u, conv_w, conv_b.reshape(1, -1), wg, bg, lam, h0)
    return y, fin


def _router_kernel(x_ref, g_ref, sc_ref, sh_ref, rw_ref, rb_ref, o_ref):
    h = _normmod(x_ref[...], g_ref[...], sc_ref[0], sh_ref[0])
    logits = lax.dot_general(rw_ref[...], h, (((1,), (1,)), ((), ())),
                             precision=lax.Precision.HIGHEST, preferred_element_type=F32)
    scores = jax.nn.sigmoid(logits)
    sel = scores + rb_ref[...]
    row = [sel[e:e + 1, :] for e in range(N_EXPERTS)]
    gs = []
    for g in range(N_GROUPS):
        r = row[g * EXPERTS_PER_GROUP:(g + 1) * EXPERTS_PER_GROUP]
        best_pair = None
        for i in range(EXPERTS_PER_GROUP):
            for j in range(i + 1, EXPERTS_PER_GROUP):
                s = r[i] + r[j]
                best_pair = s if best_pair is None else jnp.maximum(best_pair, s)
        gs.append(best_pair)
    best = jnp.zeros_like(gs[0], dtype=jnp.int32)
    top = gs[0]
    for g in range(1, N_GROUPS):
        better = gs[g] > top
        best = jnp.where(better, g, best)
        top = jnp.where(better, gs[g], top)
    picked = []
    for e in range(N_EXPERTS):
        g = e // EXPERTS_PER_GROUP
        rank = jnp.zeros_like(best)
        for o in range(g * EXPERTS_PER_GROUP, (g + 1) * EXPERTS_PER_GROUP):
            if o == e:
                continue
            ahead = (row[o] > row[e]) | ((row[o] == row[e]) & (o < e))
            rank = rank + ahead.astype(jnp.int32)
        picked.append((best == g) & (rank < 2))
    den = jnp.zeros_like(gs[0])
    for e in range(N_EXPERTS):
        den = den + jnp.where(picked[e], scores[e:e + 1, :], 0.0)
    gate = [jnp.where(picked[e], scores[e:e + 1, :] / den, 0.0) for e in range(N_EXPERTS)]
    cls = jnp.zeros_like(den)
    w_a = jnp.zeros_like(den)
    w_b = jnp.zeros_like(den)
    for g in range(N_GROUPS):
        for pi, (a, b) in enumerate(MOE_PAIRS):
            ea, eb = g * EXPERTS_PER_GROUP + a, g * EXPERTS_PER_GROUP + b
            both = picked[ea] & picked[eb]
            cls = jnp.where(both, float(g * len(MOE_PAIRS) + pi), cls)
            w_a = jnp.where(both, gate[ea], w_a)
            w_b = jnp.where(both, gate[eb], w_b)
    o_ref[...] = jnp.concatenate([cls, w_a, w_b, jnp.zeros((ROUTE_ROWS - 3, cls.shape[1]), F32)], axis=0)


MOE_PAIRS = ((0, 1), (0, 2), (0, 3), (1, 3), (1, 2), (2, 3))
N_CLS = N_GROUPS * len(MOE_PAIRS)
CLS_PAD = 32
ROUTE_ROWS = 8
MOE_TS = 256
MOE_TM = 256
SLOT_BLK = 512
ROW_W = D_MODEL + LANES


def router_gates(x, g, mod, router_w, router_b, seq_len, per_seq, tm=512):
    t, d = x.shape
    return pl.pallas_call(
        _router_kernel,
        grid=(t // tm,),
        in_specs=[pl.BlockSpec((tm, d), lambda i: (i, 0)),
                  pl.BlockSpec((1, d), lambda i: (0, 0)),
                  _mod_spec(4, tm, seq_len, per_seq),
                  _mod_spec(3, tm, seq_len, per_seq),
                  pl.BlockSpec((N_EXPERTS, d), lambda i: (0, 0)),
                  pl.BlockSpec((N_EXPERTS, 1), lambda i: (0, 0))],
        out_specs=pl.BlockSpec((ROUTE_ROWS, tm), lambda i: (0, i)),
        out_shape=jax.ShapeDtypeStruct((ROUTE_ROWS, t), F32),
        compiler_params=_cparams("arbitrary"),
        name="moe_router",
    )(x, g.reshape(1, d), mod, mod, router_w.T, router_b.reshape(N_EXPERTS, 1))


def _slots_kernel(n_blk, route_ref, slot_ref, off_ref, cnt_ref):
    cid = lax.broadcasted_iota(jnp.int32, (CLS_PAD, SLOT_BLK), 0).astype(F32)

    def members(j):
        cls = route_ref[0:1, pl.ds(pl.multiple_of(j * SLOT_BLK, SLOT_BLK), SLOT_BLK)]
        return (cid == cls).astype(F32)

    def count(j, cnt):
        return cnt + jnp.sum(members(j), axis=1, keepdims=True)

    cnt = lax.fori_loop(0, n_blk, count, jnp.zeros((CLS_PAD, 1), F32))
    cnt = jnp.broadcast_to(cnt, (CLS_PAD, LANES))
    padded = jnp.ceil(cnt * (1.0 / MOE_TS)) * MOE_TS
    r = lax.broadcasted_iota(jnp.int32, (CLS_PAD, CLS_PAD), 0)
    c = lax.broadcasted_iota(jnp.int32, (CLS_PAD, CLS_PAD), 1)
    off = jnp.dot((c < r).astype(F32), padded, precision=lax.Precision.HIGHEST, preferred_element_type=F32)
    off_ref[...] = off
    cnt_ref[...] = cnt
    tr = lax.broadcasted_iota(jnp.int32, (SLOT_BLK, SLOT_BLK), 0)
    tc = lax.broadcasted_iota(jnp.int32, (SLOT_BLK, SLOT_BLK), 1)
    earlier = (tr < tc).astype(BF16)

    def assign(j, base):
        member = members(j)
        rank = _dot(member.astype(BF16), earlier)
        slot = jnp.sum(member * (rank + base), axis=0, keepdims=True)
        slot_ref[0:1, pl.ds(pl.multiple_of(j * SLOT_BLK, SLOT_BLK), SLOT_BLK)] = slot.astype(jnp.int32)
        return base + jnp.sum(member, axis=1, keepdims=True)

    lax.fori_loop(0, n_blk, assign, off[:, 0:1])


def moe_slots(route):
    t = route.shape[1]
    stat = jax.ShapeDtypeStruct((CLS_PAD, LANES), F32)
    return pl.pallas_call(
        functools.partial(_slots_kernel, t // SLOT_BLK),
        out_shape=[jax.ShapeDtypeStruct((1, t), jnp.int32), stat, stat],
        compiler_params=pltpu.CompilerParams(vmem_limit_bytes=VMEM_LIMIT),
        name="moe_slots",
    )(route)


def _tile_maps(off, cnt, n_tiles):
    off = off[:N_CLS, 0].astype(jnp.int32)
    cnt = cnt[:N_CLS, 0].astype(jnp.int32)
    ends = off + ((cnt + MOE_TS - 1) // MOE_TS) * MOE_TS
    n_used = ends[-1] // MOE_TS
    k = jnp.arange(n_tiles, dtype=jnp.int32)
    tix = jnp.minimum(k, n_used - 1)
    cls = jnp.sum((tix[:, None] * MOE_TS >= ends[None, :]).astype(jnp.int32), axis=1)
    pair = jnp.asarray(MOE_PAIRS, jnp.int32)
    grp = (cls // len(MOE_PAIRS)) * EXPERTS_PER_GROUP
    ea = grp + pair[cls % len(MOE_PAIRS), 0]
    eb = grp + pair[cls % len(MOE_PAIRS), 1]
    n = jnp.int32(n_tiles)

    def slot_plan(e):
        chg = jnp.concatenate([jnp.ones((1,), jnp.int32), (e[1:] != e[:-1]).astype(jnp.int32)])
        at = jnp.where(chg == 1, k, n)
        nxt_at = jnp.concatenate([lax.cummin(at[::-1])[::-1][1:], n.reshape(1)])
        more = (nxt_at < n).astype(jnp.int32)
        nxt = e[jnp.minimum(nxt_at, n - 1)]
        par = (jnp.cumsum(chg) - 1) % 2
        return chg, nxt, more, par.astype(jnp.int32)

    plan_a, plan_b = slot_plan(ea), slot_plan(eb)
    chg, nxt, more, par = (jnp.stack([pa, pb]) for pa, pb in zip(plan_a, plan_b))
    return ea, eb, tix, chg, nxt, more, par, n_used.reshape(1)


def _dispatch_kernel(n_steps, slots_ref, x_ref, g_ref, sc_ref, sh_ref, rt_ref, hs_in, hs_out, rowbuf, sem):
    del hs_in
    i = pl.program_id(0)
    cur = i % 2

    def wait_rows(s):
        pltpu.make_async_copy(rowbuf.at[s], rowbuf.at[s], sem.at[s]).wait()

    @pl.when(i >= 2)
    def _():
        wait_rows(cur)

    rowbuf[cur, :, 0:D_MODEL] = _normmod(x_ref[...], g_ref[...], sc_ref[0], sh_ref[0])
    rowbuf[cur, :, D_MODEL:ROW_W] = jnp.concatenate(
        [rt_ref[...], jnp.zeros((MOE_TM, LANES - ROUTE_ROWS), F32)], axis=1)

    for s in range(2):
        @pl.when(cur == s)
        def _():
            for r in range(MOE_TM):
                dst = slots_ref[i * MOE_TM + r]
                pltpu.make_async_copy(rowbuf.at[s, r], hs_out.at[dst], sem.at[s]).start()

    @pl.when(i == n_steps - 1)
    def _():
        wait_rows(cur)
        if n_steps >= 2:
            wait_rows(1 - cur)


def moe_dispatch(x, g, mod, route_t, slots, hs, seq_len, per_seq):
    t, d = x.shape
    n_steps = t // MOE_TM
    grid_spec = pltpu.PrefetchScalarGridSpec(
        num_scalar_prefetch=1,
        grid=(n_steps,),
        in_specs=[pl.BlockSpec((MOE_TM, d), lambda i, s: (i, 0)),
                  pl.BlockSpec((1, d), lambda i, s: (0, 0)),
                  _mod_spec(4, MOE_TM, seq_len, per_seq),
                  _mod_spec(3, MOE_TM, seq_len, per_seq),
                  pl.BlockSpec((MOE_TM, ROUTE_ROWS), lambda i, s: (i, 0)),
                  pl.BlockSpec(memory_space=pl.ANY)],
        out_specs=pl.BlockSpec(memory_space=pl.ANY),
        scratch_shapes=[pltpu.VMEM((2, MOE_TM, ROW_W), F32), pltpu.SemaphoreType.DMA((2,))],
    )
    return pl.pallas_call(
        functools.partial(_dispatch_kernel, n_steps),
        grid_spec=grid_spec,
        out_shape=jax.ShapeDtypeStruct(hs.shape, F32),
        input_output_aliases={6: 0},
        compiler_params=_cparams("arbitrary"),
        name="moe_dispatch",
    )(slots, x, g.reshape(1, d), mod, mod, route_t, hs)


def _experts_kernel(layer, ea_ref, eb_ref, tix_ref, chg_ref, nxt_ref, more_ref, par_ref, nused_ref,
                    hs_ref, wg_hbm, wu_hbm, wd_hbm, ys_ref, fg, fu, fd, bg, bu, bd, sem):
    k = pl.program_id(0)

    def weight_copies(slot, expert, par):
        return [pltpu.make_async_copy(src.at[layer, expert], dst.at[slot, par], sem.at[slot, par])
                for src, dst in ((wg_hbm, fg), (wu_hbm, fu), (wd_hbm, fd))]

    @pl.when(k < nused_ref[0])
    def _():
        for slot, e_ref in enumerate((ea_ref, eb_ref)):
            @pl.when(chg_ref[slot, k] == 1)
            def _():
                par = par_ref[slot, k]

                @pl.when(k == 0)
                def _():
                    for cp in weight_copies(slot, e_ref[0], par):
                        cp.start()

                for cp in weight_copies(slot, e_ref[k], par):
                    cp.wait()
                bg[slot] = fg[slot, par].astype(BF16)
                bu[slot] = fu[slot, par].astype(BF16)
                bd[slot] = fd[slot, par].astype(BF16)

                @pl.when(more_ref[slot, k] == 1)
                def _():
                    for cp in weight_copies(slot, nxt_ref[slot, k], 1 - par):
                        cp.start()

        h = hs_ref[:, 0:D_MODEL].astype(BF16)

        def ffn(slot):
            hid = _dot(h, bg[slot])
            up = _dot(h, bu[slot])
            w = hs_ref[:, D_MODEL + 1 + slot:D_MODEL + 2 + slot]
            act = (hid * _sigmoid(hid)) * up * w
            return _dot(act.astype(BF16), bd[slot])

        ys_ref[...] = ffn(0) + ffn(1)

    @pl.when(k >= nused_ref[0])
    def _():
        ys_ref[...] = jnp.zeros_like(ys_ref)


def moe_experts(hs, maps, layer, w_gate, w_up, w_down):
    n_tiles = hs.shape[0] // MOE_TS
    d = D_MODEL

    grid_spec = pltpu.PrefetchScalarGridSpec(
        num_scalar_prefetch=8,
        grid=(n_tiles,),
        in_specs=[pl.BlockSpec((MOE_TS, ROW_W), lambda k, ea, eb, tix, *_: (tix[k], 0)),
                  pl.BlockSpec(memory_space=pl.ANY), pl.BlockSpec(memory_space=pl.ANY),
                  pl.BlockSpec(memory_space=pl.ANY)],
        out_specs=pl.BlockSpec((MOE_TS, d), lambda k, *_: (k, 0)),
        scratch_shapes=[pltpu.VMEM((2, 2, d, D_EXPERT), F32), pltpu.VMEM((2, 2, d, D_EXPERT), F32),
                        pltpu.VMEM((2, 2, D_EXPERT, d), F32),
                        pltpu.VMEM((2, d, D_EXPERT), BF16), pltpu.VMEM((2, d, D_EXPERT), BF16),
                        pltpu.VMEM((2, D_EXPERT, d), BF16),
                        pltpu.SemaphoreType.DMA((2, 2))],
    )
    return pl.pallas_call(
        functools.partial(_experts_kernel, layer),
        grid_spec=grid_spec,
        out_shape=jax.ShapeDtypeStruct((hs.shape[0], d), F32),
        compiler_params=_cparams("arbitrary"),
        name="moe_experts",
    )(*maps, hs, w_gate, w_up, w_down)


def _combine_kernel(final, n_steps, slots_ref, x_ref, g2_ref, fg_ref, ys_hbm, o_ref, gbuf, sem):
    i = pl.program_id(0)
    cur = i % 2

    def issue_tile(tile, s):
        for r in range(MOE_TM):
            src = slots_ref[tile * MOE_TM + r]
            pltpu.make_async_copy(ys_hbm.at[src], gbuf.at[s, r], sem.at[s]).start()

    @pl.when(i == 0)
    def _():
        issue_tile(0, 0)

    for s in range(2):
        @pl.when((i + 1 < n_steps) & (1 - cur == s))
        def _():
            issue_tile(i + 1, s)

    pltpu.make_async_copy(gbuf.at[cur], gbuf.at[cur], sem.at[cur]).wait()
    y = x_ref[...] + g2_ref[0] * gbuf[cur]
    if final:
        ms = jnp.mean(y * y, axis=-1, keepdims=True)
        y = y * lax.rsqrt(ms + EPS) * fg_ref[...]
    o_ref[...] = y


def moe_combine(x, mod, slots, ys, final_g, final, seq_len, per_seq):
    t, d = x.shape
    n_steps = t // MOE_TM
    grid_spec = pltpu.PrefetchScalarGridSpec(
        num_scalar_prefetch=1,
        grid=(n_steps,),
        in_specs=[pl.BlockSpec((MOE_TM, d), lambda i, s: (i, 0)),
                  _mod_spec(5, MOE_TM, seq_len, per_seq),
                  pl.BlockSpec((1, d), lambda i, s: (0, 0)),
                  pl.BlockSpec(memory_space=pl.ANY)],
        out_specs=pl.BlockSpec((MOE_TM, d), lambda i, s: (i, 0)),
        scratch_shapes=[pltpu.VMEM((2, MOE_TM, d), F32), pltpu.SemaphoreType.DMA((2,))],
    )
    return pl.pallas_call(
        functools.partial(_combine_kernel, final, n_steps),
        grid_spec=grid_spec,
        out_shape=jax.ShapeDtypeStruct((t, d), F32),
        compiler_params=_cparams("arbitrary"),
        name="moe_combine",
    )(slots, x, mod, final_g.reshape(1, d), ys)


def moe_block(xs, mods_l, per_seqs, seq_lens, layer, p, final):
    g = p['norm_g'][layer, 1]
    routes = [router_gates(x, g, mods_l, p['router_w'], p['router_b'], sl, ps)
              for x, sl, ps in zip(xs, seq_lens, per_seqs)]
    slots, off, cnt = moe_slots(jnp.concatenate(routes, axis=1))
    t_all = slots.shape[1]
    n_tiles = t_all // MOE_TS + N_CLS
    maps = _tile_maps(off, cnt, n_tiles)
    hs = jnp.zeros((n_tiles * MOE_TS, ROW_W), F32)
    bounds = np.cumsum([0] + [x.shape[0] for x in xs])
    group_slots = [slots[0, bounds[i]:bounds[i + 1]] for i in range(len(xs))]
    for x, r, s, sl, ps in zip(xs, routes, group_slots, seq_lens, per_seqs):
        hs = moe_dispatch(x, g, mods_l, r.T, s, hs, sl, ps)
    ys = moe_experts(hs, maps, layer, p['moe_w_gate'], p['moe_w_up'], p['moe_w_down'])
    return [moe_combine(x, mods_l, s, ys, p['final_g'], final, sl, ps)
            for x, s, sl, ps in zip(xs, group_slots, seq_lens, per_seqs)]


def _mixer(x, group, l, mod, p, hy_tables):
    per_seq, nseq, seq_len = group['per_seq'], group['nseq'], group['seq_len']
    extras = None
    if l % 2 == 0:
        e = l // 2
        u = nm_matmul(x, p['norm_g'][l, 0], mod, p['a_in_w'][e].astype(BF16), seq_len, per_seq)
        if group['ctx_k'] is None:
            attn, nk, nv = context_attention(u, nseq, seq_len)
            extras = (nk, nv)
        else:
            attn = neighbourhood_attention(u, group['ctx_k'][:, e], group['ctx_v'][:, e], p['na_rpb'][e],
                                           nseq, seq_len)
        fwd, inv = hy_tables[seq_len]
        spectrum = hyena_spectrum(seq_len, p['hy_w1'][e], p['hy_b1'][e], p['hy_w2'][e], p['hy_b2'][e],
                                  p['hy_w3'][e], p['hy_freq'][e], p['hy_d'][e], fwd)
        hy = hyena_mixer(u, nseq, seq_len, p['hy_short_w'][e], p['hy_short_b'][e], spectrum, fwd, inv)
        w_out = p['a_out_w'][e].astype(BF16)
        x = proj_residual([attn, hy], [w_out[:D_A], w_out[D_A:]], x, mod, seq_len, per_seq)
    else:
        o = l // 2
        u = nm_matmul_tm(x, p['norm_g'][l, 0], mod, p['c_in_w'][o].astype(BF16), nseq, seq_len, per_seq)
        y, extras = rglru_block(u, nseq, seq_len, p['rg_conv_w'][o], p['rg_conv_b'][o], p['rg_wa'][o],
                                p['rg_ba'][o], p['rg_wx'][o], p['rg_bx'][o], p['rg_lam'][o], group['h0'][o])
        x = proj_residual_tm(y, p['c_out_w'][o].astype(BF16), x, mod, nseq, seq_len, per_seq)
    return x, extras


def kernel(x_prompt, x_sample, cache_k, cache_v, state_h, c, c_ctx, norm_g, ada_w, ada_b, final_g, a_in_w, a_out_w, na_rpb, hy_short_w, hy_short_b, hy_w1, hy_b1, hy_w2, hy_b2, hy_w3, hy_freq, hy_d, c_in_w, c_out_w, rg_conv_w, rg_conv_b, rg_wa, rg_ba, rg_wx, rg_bx, rg_lam, router_w, router_b, moe_w_gate, moe_w_up, moe_w_down):
    p = dict(norm_g=norm_g, final_g=final_g, a_in_w=a_in_w, a_out_w=a_out_w, na_rpb=na_rpb,
             hy_short_w=hy_short_w, hy_short_b=hy_short_b, hy_w1=hy_w1, hy_b1=hy_b1, hy_w2=hy_w2,
             hy_b2=hy_b2, hy_w3=hy_w3, hy_freq=hy_freq, hy_d=hy_d, c_in_w=c_in_w, c_out_w=c_out_w,
             rg_conv_w=rg_conv_w, rg_conv_b=rg_conv_b, rg_wa=rg_wa, rg_ba=rg_ba, rg_wx=rg_wx, rg_bx=rg_bx,
             rg_lam=rg_lam, router_w=router_w, router_b=router_b, moe_w_gate=moe_w_gate,
             moe_w_up=moe_w_up, moe_w_down=moe_w_down)
    batch, seq, d = x_prompt.shape
    dec_batch, dec_seq, _ = x_sample.shape
    n_odd = DEPTH // 2
    assert 1 + dec_batch <= MOD_ROWS

    cond = jnp.concatenate([c_ctx[None, :], c, jnp.zeros((MOD_ROWS - 1 - dec_batch, d), F32)], axis=0)
    m = modulation(cond, ada_w, ada_b)
    mods = [m[l].reshape(MOD_ROWS * N_MOD, 1, d) for l in range(DEPTH)]

    tables = {}
    for sl in (seq, dec_seq):
        fwd, inv = _dft_tables(sl)
        tables[sl] = (jnp.asarray(fwd).astype(BF16), jnp.asarray(inv).astype(BF16))

    groups = [
        dict(per_seq=False, nseq=batch, seq_len=seq, ctx_k=None, ctx_v=None,
             h0=[jnp.zeros((2, batch, D_RNN), F32)] * n_odd),
        dict(per_seq=True, nseq=dec_batch, seq_len=dec_seq, ctx_k=cache_k, ctx_v=cache_v,
             h0=[state_h[:, o].transpose(1, 0, 2) for o in range(n_odd)]),
    ]
    xs = [x_prompt.reshape(batch * seq, d), x_sample.reshape(dec_batch * dec_seq, d)]
    k_list, v_list, h_list = [], [], []
    for l in range(DEPTH):
        mixed = [_mixer(x, grp, l, mods[l], p, tables) for x, grp in zip(xs, groups)]
        if l % 2 == 0:
            k_list.append(mixed[0][1][0])
            v_list.append(mixed[0][1][1])
        else:
            h_list.append(mixed[0][1].transpose(1, 0, 2))
        xs = moe_block([mx[0] for mx in mixed], mods[l], [grp['per_seq'] for grp in groups],
                       [grp['seq_len'] for grp in groups], l, p, l == DEPTH - 1)
    new_k = jnp.stack(k_list, axis=1)
    new_v = jnp.stack(v_list, axis=1)
    new_h = jnp.stack(h_list, axis=1)
    return (xs[0].reshape(batch, seq, d), xs[1].reshape(dec_batch, dec_seq, d), new_k, new_v, new_h)
```

```python
import functools
import math

import numpy as np
import jax
import jax.numpy as jnp
from jax import lax
from jax.experimental import pallas as pl
from jax.experimental.pallas import tpu as pltpu

F32 = jnp.float32
BF16 = jnp.bfloat16

D_MODEL = 1024
DEPTH = 2
GRID_W = 64
EPS = 1e-6
NEG_INF = -1e30
NA_HEADS = 8
HEAD_DIM = 64
D_A = NA_HEADS * HEAD_DIM
WIN_ROWS = 8
WIN_COLS = 16
D_B = D_MODEL - D_A
HY_ORDER = 2
HY_EMB = 33
HY_BANDS = (HY_EMB - 1) // 2
HY_FFN = 64
HY_DECAY_TARGET = 1e-2
HY_FAST_PCT = 0.3
HY_SLOW_PCT = 1.5
D_RNN = D_MODEL
RG_BLOCK = 64
RG_C = 8.0
N_EXPERTS = 16
N_GROUPS = 4
EXPERTS_PER_GROUP = N_EXPERTS // N_GROUPS
D_EXPERT = 512

LANES = 128
VMEM_LIMIT = 56 * 1024 * 1024
N_MOD = 6
MOD_ROWS = 16


def _cparams(*sem):
    return pltpu.CompilerParams(dimension_semantics=sem, vmem_limit_bytes=VMEM_LIMIT)


def _dot(a, b):
    return jnp.dot(a, b, preferred_element_type=F32)


def _dot_nt(a, b):
    return lax.dot_general(a, b, (((1,), (1,)), ((), ())), preferred_element_type=F32)


def _sigmoid(x):
    return 0.5 * jnp.tanh(0.5 * x) + 0.5


def _normmod(x, g, sc, sh):
    ms = jnp.mean(x * x, axis=-1, keepdims=True)
    return (x * lax.rsqrt(ms + EPS) * g) * (1.0 + sc) + sh


def _mod_spec(chunk, tm, seq_len, per_seq):
    if per_seq:
        return pl.BlockSpec((1, 1, D_MODEL), lambda i, *_: ((1 + (i * tm) // seq_len) * N_MOD + chunk, 0, 0))
    return pl.BlockSpec((1, 1, D_MODEL), lambda i, *_: (chunk, 0, 0))


def _mod_kernel(c_ref, w_ref, b_ref, o_ref):
    s = c_ref[...]
    s = s * jax.nn.sigmoid(s)
    o_ref[0] = _dot(s.astype(BF16), w_ref[0].astype(BF16)) + b_ref[0]


def modulation(cond, ada_w, ada_b):
    tn = 1536
    n = ada_w.shape[-1]
    return pl.pallas_call(
        _mod_kernel,
        grid=(DEPTH, n // tn),
        in_specs=[pl.BlockSpec((MOD_ROWS, D_MODEL), lambda l, j: (0, 0)),
                  pl.BlockSpec((1, D_MODEL, tn), lambda l, j: (l, 0, j)),
                  pl.BlockSpec((1, 1, tn), lambda l, j: (l, 0, j))],
        out_specs=pl.BlockSpec((1, MOD_ROWS, tn), lambda l, j: (l, 0, j)),
        out_shape=jax.ShapeDtypeStruct((DEPTH, MOD_ROWS, n), F32),
        compiler_params=_cparams("arbitrary", "arbitrary"),
        name="modulation",
    )(cond, ada_w, ada_b.reshape(DEPTH, 1, n))


def _nm_matmul_kernel(x_ref, g_ref, sc_ref, sh_ref, w_ref, o_ref):
    h = _normmod(x_ref[...], g_ref[...], sc_ref[0], sh_ref[0])
    o_ref[...] = _dot(h.astype(BF16), w_ref[...])


def nm_matmul(x, g, mod, w, seq_len, per_seq, tm=512):
    t, d = x.shape
    n = w.shape[1]
    return pl.pallas_call(
        _nm_matmul_kernel,
        grid=(t // tm,),
        in_specs=[pl.BlockSpec((tm, d), lambda i: (i, 0)),
                  pl.BlockSpec((1, d), lambda i: (0, 0)),
                  _mod_spec(1, tm, seq_len, per_seq),
                  _mod_spec(0, tm, seq_len, per_seq),
                  pl.BlockSpec((d, n), lambda i: (0, 0))],
        out_specs=pl.BlockSpec((tm, n), lambda i: (i, 0)),
        out_shape=jax.ShapeDtypeStruct((t, n), F32),
        compiler_params=_cparams("arbitrary"),
        name="norm_mod_proj",
    )(x, g.reshape(1, d), mod, mod, w)


def _proj_res_kernel(n_act, *refs):
    acts = refs[:n_act]
    ws = refs[n_act:2 * n_act]
    x_ref, g_ref, o_ref = refs[2 * n_act:]
    acc = _dot(acts[0][...].astype(BF16), ws[0][...])
    for a, w in zip(acts[1:], ws[1:]):
        acc += _dot(a[...].astype(BF16), w[...])
    o_ref[...] = x_ref[...] + g_ref[0] * acc


def proj_residual(acts, ws, x, mod, seq_len, per_seq, tm=512):
    t, d = x.shape
    in_specs = [pl.BlockSpec((tm, a.shape[1]), lambda i: (i, 0)) for a in acts]
    in_specs += [pl.BlockSpec(w.shape, lambda i: (0, 0)) for w in ws]
    in_specs += [pl.BlockSpec((tm, d), lambda i: (i, 0)), _mod_spec(2, tm, seq_len, per_seq)]
    return pl.pallas_call(
        functools.partial(_proj_res_kernel, len(acts)),
        grid=(t // tm,),
        in_specs=in_specs,
        out_specs=pl.BlockSpec((tm, d), lambda i: (i, 0)),
        out_shape=jax.ShapeDtypeStruct((t, d), F32),
        compiler_params=_cparams("arbitrary"),
        name="proj_residual",
    )(*acts, *ws, x, mod)


def _row_permutation(nseq, steps, to_time_major):
    n = nseq * steps
    i = lax.broadcasted_iota(jnp.int32, (n, n), 0)
    j = lax.broadcasted_iota(jnp.int32, (n, n), 1)
    if to_time_major:
        src = (i % nseq) * steps + i // nseq
    else:
        src = (i % steps) * nseq + i // steps
    return (j == src).astype(BF16)


def _nm_matmul_tm_kernel(nseq, steps, x_ref, g_ref, sc_ref, sh_ref, w_ref, o_ref):
    h = _normmod(x_ref[...], g_ref[...], sc_ref[...], sh_ref[...])
    h = h.reshape(nseq * steps, h.shape[-1]).astype(BF16)
    h = _dot(_row_permutation(nseq, steps, True), h).astype(BF16)
    o_ref[...] = _dot(h, w_ref[...])


def _group_mod(mod, chunk, nseq, per_seq):
    rows = mod.reshape(MOD_ROWS, N_MOD, 1, D_MODEL)
    return rows[1:1 + nseq, chunk] if per_seq else rows[0:1, chunk]


def nm_matmul_tm(x, g, mod, w, nseq, seq_len, per_seq):
    t, d = x.shape
    n = w.shape[1]
    steps = TM_ROWS // nseq
    sc = _group_mod(mod, 1, nseq, per_seq)
    sh = _group_mod(mod, 0, nseq, per_seq)
    mod_spec = pl.BlockSpec(sc.shape, lambda i: (0, 0, 0))
    return pl.pallas_call(
        functools.partial(_nm_matmul_tm_kernel, nseq, steps),
        grid=(seq_len // steps,),
        in_specs=[pl.BlockSpec((nseq, steps, d), lambda i: (0, i, 0)),
                  pl.BlockSpec((1, d), lambda i: (0, 0)),
                  mod_spec, mod_spec,
                  pl.BlockSpec((d, n), lambda i: (0, 0))],
        out_specs=pl.BlockSpec((TM_ROWS, n), lambda i: (i, 0)),
        out_shape=jax.ShapeDtypeStruct((t, n), F32),
        compiler_params=_cparams("arbitrary"),
        name="norm_mod_proj_tm",
    )(x.reshape(nseq, seq_len, d), g.reshape(1, d), sc, sh, w)


def _proj_res_tm_kernel(nseq, steps, y_ref, w_ref, x_ref, g_ref, o_ref):
    y = _dot(_row_permutation(nseq, steps, False), y_ref[...].astype(BF16)).astype(BF16)
    acc = _dot(y, w_ref[...])
    o_ref[...] = x_ref[...] + g_ref[...] * acc.reshape(nseq, steps, acc.shape[-1])


def proj_residual_tm(y, w, x, mod, nseq, seq_len, per_seq):
    t, d = x.shape
    steps = TM_ROWS // nseq
    g1 = _group_mod(mod, 2, nseq, per_seq)
    out = pl.pallas_call(
        functools.partial(_proj_res_tm_kernel, nseq, steps),
        grid=(seq_len // steps,),
        in_specs=[pl.BlockSpec((TM_ROWS, y.shape[1]), lambda i: (i, 0)),
                  pl.BlockSpec(w.shape, lambda i: (0, 0)),
                  pl.BlockSpec((nseq, steps, d), lambda i: (0, i, 0)),
                  pl.BlockSpec(g1.shape, lambda i: (0, 0, 0))],
        out_specs=pl.BlockSpec((nseq, steps, d), lambda i: (0, i, 0)),
        out_shape=jax.ShapeDtypeStruct((nseq, seq_len, d), F32),
        compiler_params=_cparams("arbitrary"),
        name="proj_residual_tm",
    )(y, w, x.reshape(nseq, seq_len, d), g1)
    return out.reshape(t, d)


def _ctx_attn_kernel(q_ref, k_ref, v_ref, o_ref, nk_ref, nv_ref):
    scale = HEAD_DIM ** -0.5
    for h in range(NA_HEADS):
        sl = slice(h * HEAD_DIM, (h + 1) * HEAD_DIM)
        q = q_ref[:, sl]
        k = k_ref[:, sl]
        v = v_ref[:, sl]
        nk_ref[0, h] = k
        nv_ref[0, h] = v
        s = _dot_nt(q.astype(BF16), k.astype(BF16)) * scale
        m = jnp.max(s, axis=-1, keepdims=True)
        p = jnp.exp(s - m)
        den = jnp.sum(p, axis=-1, keepdims=True)
        o_ref[:, sl] = _dot(p.astype(BF16), v.astype(BF16)) / den


def context_attention(u, nseq, seq_len):
    t = u.shape[0]
    kv_shape = jax.ShapeDtypeStruct((nseq, NA_HEADS, seq_len, HEAD_DIM), F32)
    kv_spec = pl.BlockSpec((1, NA_HEADS, seq_len, HEAD_DIM), lambda b: (b, 0, 0, 0))
    return pl.pallas_call(
        _ctx_attn_kernel,
        grid=(nseq,),
        in_specs=[pl.BlockSpec((seq_len, D_A), lambda b: (b, 0)),
                  pl.BlockSpec((seq_len, D_A), lambda b: (b, 1)),
                  pl.BlockSpec((seq_len, D_A), lambda b: (b, 2))],
        out_specs=[pl.BlockSpec((seq_len, D_A), lambda b: (b, 0)), kv_spec, kv_spec],
        out_shape=[jax.ShapeDtypeStruct((t, D_A), F32), kv_shape, kv_shape],
        compiler_params=_cparams("arbitrary"),
        name="context_attention",
    )(u, u, u)


N_DR = 2 * WIN_ROWS - 1
N_DC = 2 * WIN_COLS - 1


def _na_col_tables():
    cols = np.arange(GRID_W)
    col_start = np.clip(cols - WIN_COLS // 2, 0, GRID_W - WIN_COLS)
    col_in = (cols[None, :] >= col_start[:, None]) & (cols[None, :] < col_start[:, None] + WIN_COLS)
    dc = np.clip(cols[None, :] - cols[:, None], 1 - WIN_COLS, WIN_COLS - 1) + WIN_COLS - 1
    onehot = (dc.reshape(1, -1) == np.arange(32)[:, None]).astype(np.float32)
    return onehot, col_in.reshape(1, -1).astype(np.float32)


def _na_bias_kernel(r_ref, e_ref, m_ref, o_ref):
    t = jnp.dot(r_ref[...], e_ref[...], precision=lax.Precision.HIGHEST, preferred_element_type=F32)
    o_ref[...] = jnp.where(m_ref[...] > 0.0, t, NEG_INF)


def na_bias_table(rpb):
    onehot, col_in = _na_col_tables()
    n_rows = NA_HEADS * N_DR
    r = jnp.zeros((LANES, 32), F32).at[:n_rows, :N_DC].set(rpb.reshape(n_rows, N_DC).astype(F32))
    t = pl.pallas_call(
        _na_bias_kernel,
        out_shape=jax.ShapeDtypeStruct((LANES, GRID_W * GRID_W), F32),
        name="na_bias_table",
    )(r, jnp.asarray(onehot), jnp.asarray(col_in))
    t = t[:n_rows].reshape(NA_HEADS, N_DR, GRID_W, GRID_W)
    return jnp.concatenate([t[:, :-1], t[:, 1:]], axis=-1)


def _na_kernel(rows, q_ref, k_ref, v_ref, ck_ref, cv_ref, bias_ref, o_ref,
               q_s, k_s, v_s, ck_s, cv_s, s_s, p_s, den_s, o_s):
    scale = HEAD_DIM ** -0.5
    n_lat = WIN_ROWS * GRID_W
    for h in range(NA_HEADS):
        sl = slice(h * HEAD_DIM, (h + 1) * HEAD_DIM)
        q_s[h] = (q_ref[:, sl] * scale).astype(BF16)
        k_s[h] = k_ref[:, sl].astype(BF16)
        v_s[h] = v_ref[:, sl].astype(BF16)
    ck_s[...] = ck_ref[0].astype(BF16)
    cv_s[...] = cv_ref[0].astype(BF16)

    def window(r):
        start = min(max(r - WIN_ROWS // 2, 0), rows - WIN_ROWS)
        return start, start - r + WIN_ROWS - 1

    def head_body(h, carry):
        for r in range(rows):
            start, off = window(r)
            q = q_s[h, r * GRID_W:(r + 1) * GRID_W, :]
            bias = jnp.concatenate([bias_ref[h, off + 2 * i] for i in range(WIN_ROWS // 2)], axis=1)
            s_s[r * GRID_W:(r + 1) * GRID_W, 0:n_lat] = (
                _dot_nt(q, k_s[h, start * GRID_W:start * GRID_W + n_lat, :]) + bias)
            s_s[r * GRID_W:(r + 1) * GRID_W, n_lat:] = _dot_nt(q, ck_s[h])
        for r in range(rows):
            rs = slice(r * GRID_W, (r + 1) * GRID_W)
            s = s_s[rs, :]
            p = jnp.exp(s - jnp.max(s, axis=-1, keepdims=True))
            den_s[rs, :] = jnp.sum(p, axis=-1, keepdims=True)
            p_s[rs, :] = p.astype(BF16)
        for r in range(rows):
            start, _ = window(r)
            rs = slice(r * GRID_W, (r + 1) * GRID_W)
            o = (_dot(p_s[rs, 0:n_lat], v_s[h, start * GRID_W:start * GRID_W + n_lat, :])
                 + _dot(p_s[rs, n_lat:], cv_s[h]))
            o_s[h, rs, :] = o / den_s[rs, :]
        return carry

    lax.fori_loop(0, NA_HEADS, head_body, 0)
    for h in range(NA_HEADS):
        o_ref[:, h * HEAD_DIM:(h + 1) * HEAD_DIM] = o_s[h]


def neighbourhood_attention(u, ctx_k, ctx_v, rpb, nseq, seq_len):
    t = u.shape[0]
    rows = seq_len // GRID_W
    assert rows >= WIN_ROWS and WIN_ROWS % 2 == 0
    past = ctx_k.shape[2]
    bias = na_bias_table(rpb)
    ctx_spec = pl.BlockSpec((1, NA_HEADS, past, HEAD_DIM), lambda b: (b, 0, 0, 0))
    return pl.pallas_call(
        functools.partial(_na_kernel, rows),
        grid=(nseq,),
        in_specs=[pl.BlockSpec((seq_len, D_A), lambda b: (b, 0)),
                  pl.BlockSpec((seq_len, D_A), lambda b: (b, 1)),
                  pl.BlockSpec((seq_len, D_A), lambda b: (b, 2)),
                  ctx_spec, ctx_spec,
                  pl.BlockSpec(bias.shape, lambda b: (0, 0, 0, 0))],
        out_specs=pl.BlockSpec((seq_len, D_A), lambda b: (b, 0)),
        out_shape=jax.ShapeDtypeStruct((t, D_A), F32),
        scratch_shapes=[pltpu.VMEM((NA_HEADS, seq_len, HEAD_DIM), BF16)] * 3
        + [pltpu.VMEM((NA_HEADS, past, HEAD_DIM), BF16)] * 2
        + [pltpu.VMEM((seq_len, WIN_ROWS * GRID_W + past), F32),
           pltpu.VMEM((seq_len, WIN_ROWS * GRID_W + past), BF16),
           pltpu.VMEM((seq_len, 1), F32),
           pltpu.VMEM((NA_HEADS, seq_len, HEAD_DIM), F32)],
        compiler_params=_cparams("arbitrary"),
        name="neighbourhood_attention",
    )(u, u, u, ctx_k, ctx_v, bias)


def _dft_tables(seq_len):
    n = 2 * seq_len
    f = np.arange(seq_len, dtype=np.int64)
    ang = (np.outer(f, f) % n).astype(np.float64) * (math.pi / seq_len)
    cos, sin = np.cos(ang), np.sin(ang)
    alt = np.where(f % 2 == 0, 1.0, -1.0)
    s_fwd = -sin
    s_fwd[0, :] = alt
    fwd = np.concatenate([cos, s_fwd], axis=0)
    wf = np.where(f == 0, 1.0, 2.0) / n
    ci = cos.T * wf[None, :]
    si = -sin.T * wf[None, :]
    si[:, 0] = alt / n
    inv = np.concatenate([ci, si], axis=1)
    return fwd.astype(np.float32), inv.astype(np.float32)


def _hyena_feats(seq_len):
    t = np.linspace(0.0, 1.0, seq_len, dtype=np.float32)[:, None]
    w = (2.0 * math.pi * np.arange(seq_len, dtype=np.float32)[:, None] / seq_len).astype(np.float32)
    f = np.linspace(1e-4, HY_BANDS - 1, HY_BANDS, dtype=np.float32)[None, :]
    z = np.concatenate([t, np.cos(f * w), -np.sin(f * w)], axis=-1).astype(np.float32)
    max_decay = math.log(HY_DECAY_TARGET) / HY_FAST_PCT
    min_decay = math.log(HY_DECAY_TARGET) / HY_SLOW_PCT
    deltas = np.abs(np.linspace(min_decay, max_decay, D_B, dtype=np.float32))[None, :]
    return z, t, deltas


def _hy_filter_kernel(seq_len, z_ref, t_ref, dl_ref, w1_ref, b1_ref, w2_ref, b2_ref, w3_ref, fr_ref,
                      d_ref, fwd_ref, g_ref):
    hp = lax.Precision.HIGHEST
    h = jnp.sin(fr_ref[0:1, :] * (jnp.dot(z_ref[...], w1_ref[...], precision=hp) + b1_ref[...]))
    h = jnp.sin(fr_ref[1:2, :] * (jnp.dot(h, w2_ref[...], precision=hp) + b2_ref[...]))
    h = jnp.dot(h, w3_ref[...], precision=hp)
    decay = jnp.exp(-t_ref[...] * dl_ref[...])
    row0 = lax.broadcasted_iota(jnp.int32, (seq_len, D_B), 0) == 0
    sums, diffs = [], []
    for n in range(HY_ORDER):
        hf = h[:, (2 * n) * D_B:(2 * n + 1) * D_B] * decay
        hb = h[:, (2 * n + 1) * D_B:(2 * n + 2) * D_B] * decay
        gp = jnp.where(row0, hf + hb + d_ref[n:n + 1, :], hf)
        gm = jnp.where(row0, 0.0, hb)
        sums.append(gp + gm)
        diffs.append(gp - gm)
    rhs = jnp.concatenate(sums + diffs, axis=1).astype(BF16)
    spec = _dot(fwd_ref[...], rhs)
    for n in range(HY_ORDER):
        a = spec[:, n * D_B:(n + 1) * D_B]
        b = spec[:, (HY_ORDER + n) * D_B:(HY_ORDER + n + 1) * D_B]
        g_ref[n, 0:seq_len, :] = a[0:seq_len]
        g_ref[n, seq_len:, :] = jnp.where(row0, a[seq_len:], b[seq_len:])


def hyena_spectrum(seq_len, w1, b1, w2, b2, w3, freq, d, fwd):
    z, t, deltas = _hyena_feats(seq_len)
    return pl.pallas_call(
        functools.partial(_hy_filter_kernel, seq_len),
        out_shape=jax.ShapeDtypeStruct((HY_ORDER, 2 * seq_len, D_B), F32),
        compiler_params=pltpu.CompilerParams(vmem_limit_bytes=VMEM_LIMIT),
        name="hyena_spectrum",
    )(jnp.asarray(z), jnp.asarray(t), jnp.asarray(deltas), w1, b1.reshape(1, -1), w2, b2.reshape(1, -1),
      w3, freq, d, fwd)


def _hyena_kernel(seq_len, u_ref, sw_ref, sb_ref, g_ref, fwd_ref, inv_ref, o_ref):
    u = u_ref[...]
    t_idx = lax.broadcasted_iota(jnp.int32, u.shape, 0)
    prev = jnp.where(t_idx == 0, 0.0, pltpu.roll(u, 1, axis=0))
    nxt = jnp.where(t_idx == seq_len - 1, 0.0, pltpu.roll(u, seq_len - 1, axis=0))
    u = prev * sw_ref[0:1, :] + u * sw_ref[1:2, :] + nxt * sw_ref[2:3, :] + sb_ref[...]
    row0 = lax.broadcasted_iota(jnp.int32, (seq_len, D_B), 0) == 0
    z = u[:, 0:D_B]
    for n in range(HY_ORDER):
        spec = _dot(fwd_ref[...], z.astype(BF16))
        ure, uim = spec[0:seq_len], spec[seq_len:]
        gre, gim = g_ref[n, 0:seq_len, :], g_ref[n, seq_len:, :]
        pim = uim * gim
        yre = ure * gre - jnp.where(row0, 0.0, pim)
        yim = jnp.where(row0, pim, ure * gim + uim * gre)
        y = jnp.concatenate([yre, yim], axis=0).astype(BF16)
        z = u[:, (n + 1) * D_B:(n + 2) * D_B] * _dot(inv_ref[...], y)
    o_ref[...] = z


def hyena_mixer(u, nseq, seq_len, short_w, short_b, spectrum, fwd, inv):
    t = u.shape[0]
    width = (HY_ORDER + 1) * D_B
    col_block = (3 * D_A) // width
    assert col_block * width == 3 * D_A
    return pl.pallas_call(
        functools.partial(_hyena_kernel, seq_len),
        grid=(nseq,),
        in_specs=[pl.BlockSpec((seq_len, width), lambda b: (b, col_block)),
                  pl.BlockSpec(short_w.shape, lambda b: (0, 0)),
                  pl.BlockSpec((1, width), lambda b: (0, 0)),
                  pl.BlockSpec(spectrum.shape, lambda b: (0, 0, 0)),
                  pl.BlockSpec(fwd.shape, lambda b: (0, 0)),
                  pl.BlockSpec(inv.shape, lambda b: (0, 0))],
        out_specs=pl.BlockSpec((seq_len, D_B), lambda b: (b, 0)),
        out_shape=jax.ShapeDtypeStruct((t, D_B), F32),
        compiler_params=_cparams("arbitrary"),
        name="hyena_mixer",
    )(u, short_w, short_b.reshape(1, width), spectrum, fwd, inv)


RG_CB = LANES
RG_CHUNK = 512
TM_ROWS = 256


def _rglru_kernel(nseq, seq_len, gate_ref, xr_ref, cw_ref, cb_ref, wg_ref, bg_ref, lam_ref, h0_ref,
                  y_ref, fin_ref, xp_ref, a_f, b_f, a_b, b_b):
    t_tot = nseq * seq_len
    c = RG_CB
    pad = 2 * nseq
    xp_ref[0:pad, :] = jnp.zeros((pad, c), F32)
    xp_ref[pad + t_tot:, :] = jnp.zeros((pad, c), F32)
    xp_ref[pad:pad + t_tot, :] = xr_ref[...]
    nl = -lam_ref[...]
    sp = jnp.maximum(nl, 0.0) + jnp.log1p(jnp.exp(-jnp.abs(nl)))

    def gate_chunk(ci, carry):
        r0 = pl.multiple_of(ci * RG_CHUNK, RG_CHUNK)
        xc = xp_ref[pl.ds(r0, RG_CHUNK), :] * cw_ref[0:1, :]
        for j in range(1, cw_ref.shape[0]):
            xc = xc + xp_ref[pl.ds(r0 + j * nseq, RG_CHUNK), :] * cw_ref[j:j + 1, :]
        xc = xc + cb_ref[...]
        gts = _dot(xc.astype(BF16), wg_ref[0]) + bg_ref[...]
        for d, (a_ref, b_ref) in enumerate(((a_f, b_f), (a_b, b_b))):
            r = _sigmoid(gts[:, (2 * d) * c:(2 * d + 1) * c])
            i = _sigmoid(gts[:, (2 * d + 1) * c:(2 * d + 2) * c])
            a = jnp.exp(-RG_C * r * sp[d:d + 1, :])
            a_ref[pl.ds(r0, RG_CHUNK), :] = a
            b_ref[pl.ds(r0, RG_CHUNK), :] = jnp.sqrt(jnp.maximum(1.0 - a * a, 0.0)) * (i * xc)
        return carry

    lax.fori_loop(0, t_tot // RG_CHUNK, gate_chunk, 0)

    def scan_step(t, carry):
        hf, hb = carry
        rows_f = pl.ds(pl.multiple_of(t * nseq, nseq), nseq)
        rows_b = pl.ds(pl.multiple_of((seq_len - 1 - t) * nseq, nseq), nseq)
        hf = a_f[rows_f, :] * hf + b_f[rows_f, :]
        hb = a_b[rows_b, :] * hb + b_b[rows_b, :]
        b_f[rows_f, :] = hf
        b_b[rows_b, :] = hb
        return hf, hb

    hf, hb = lax.fori_loop(0, seq_len, scan_step, (h0_ref[0], h0_ref[1]), unroll=8)
    fin_ref[0] = hf
    fin_ref[1] = hb

    def out_chunk(ci, carry):
        rs = pl.ds(pl.multiple_of(ci * RG_CHUNK, RG_CHUNK), RG_CHUNK)
        y_ref[rs, :] = (b_f[rs, :] + b_b[rs, :]) * jax.nn.gelu(gate_ref[rs, :])
        return carry

    lax.fori_loop(0, t_tot // RG_CHUNK, out_chunk, 0)


def _rg_gate_weights(wa, wx):
    per_step = RG_CB // RG_BLOCK
    steps = D_RNN // RG_CB
    mats = []
    for d in range(2):
        for w in (wa[d], wx[d]):
            w = w.reshape(steps, per_step, RG_BLOCK, RG_BLOCK)
            eye = jnp.eye(per_step, dtype=w.dtype)
            m = jnp.einsum('spde,pq->spdqe', w, eye).reshape(steps, RG_CB, RG_CB)
            mats.append(m)
    return jnp.concatenate(mats, axis=-1).astype(BF16)


def rglru_block(u, nseq, seq_len, conv_w, conv_b, wa, ba, wx, bx, lam, h0):
    t = u.shape[0]
    c = RG_CB
    steps = D_RNN // c
    wg = _rg_gate_weights(wa, wx)
    bg = jnp.stack([ba[0], bx[0], ba[1], bx[1]], axis=0).reshape(4, steps, c)
    bg = bg.transpose(1, 0, 2).reshape(steps, 1, 4 * c)
    y, fin = pl.pallas_call(
        functools.partial(_rglru_kernel, nseq, seq_len),
        grid=(steps,),
        in_specs=[pl.BlockSpec((t, c), lambda j: (0, j)),
                  pl.BlockSpec((t, c), lambda j: (0, steps + j)),
                  pl.BlockSpec((conv_w.shape[0], c), lambda j: (0, j)),
                  pl.BlockSpec((1, c), lambda j: (0, j)),
                  pl.BlockSpec((1, c, 4 * c), lambda j: (j, 0, 0)),
                  pl.BlockSpec((None, 1, 4 * c), lambda j: (j, 0, 0)),
                  pl.BlockSpec((2, c), lambda j: (0, j)),
                  pl.BlockSpec((2, nseq, c), lambda j: (0, 0, j))],
        out_specs=[pl.BlockSpec((t, c), lambda j: (0, j)),
                   pl.BlockSpec((2, nseq, c), lambda j: (0, 0, j))],
        out_shape=[jax.ShapeDtypeStruct((t, D_RNN), F32),
                   jax.ShapeDtypeStruct((2, nseq, D_RNN), F32)],
        scratch_shapes=[pltpu.VMEM((t + 4 * nseq, c), F32)] + [pltpu.VMEM((t, c), F32)] * 4,
        compiler_params=_cparams("arbitrary"),
        name="rglru_block",
    )(u, u, conv_w, conv_b.reshape(1, -1), wg, bg, lam, h0)
    return y, fin


def _router_kernel(x_ref, g_ref, sc_ref, sh_ref, rw_ref, rb_ref, o_ref):
    h = _normmod(x_ref[...], g_ref[...], sc_ref[0], sh_ref[0])
    logits = lax.dot_general(rw_ref[...], h, (((1,), (1,)), ((), ())),
                             precision=lax.Precision.HIGHEST, preferred_element_type=F32)
    scores = jax.nn.sigmoid(logits)
    sel = scores + rb_ref[...]
    row = [sel[e:e + 1, :] for e in range(N_EXPERTS)]
    gs = []
    for g in range(N_GROUPS):
        r = row[g * EXPERTS_PER_GROUP:(g + 1) * EXPERTS_PER_GROUP]
        best_pair = None
        for i in range(EXPERTS_PER_GROUP):
            for j in range(i + 1, EXPERTS_PER_GROUP):
                s = r[i] + r[j]
                best_pair = s if best_pair is None else jnp.maximum(best_pair, s)
        gs.append(best_pair)
    best = jnp.zeros_like(gs[0], dtype=jnp.int32)
    top = gs[0]
    for g in range(1, N_GROUPS):
        better = gs[g] > top
        best = jnp.where(better, g, best)
        top = jnp.where(better, gs[g], top)
    picked = []
    for e in range(N_EXPERTS):
        g = e // EXPERTS_PER_GROUP
        rank = jnp.zeros_like(best)
        for o in range(g * EXPERTS_PER_GROUP, (g + 1) * EXPERTS_PER_GROUP):
            if o == e:
                continue
            ahead = (row[o] > row[e]) | ((row[o] == row[e]) & (o < e))
            rank = rank + ahead.astype(jnp.int32)
        picked.append((best == g) & (rank < 2))
    den = jnp.zeros_like(gs[0])
    for e in range(N_EXPERTS):
        den = den + jnp.where(picked[e], scores[e:e + 1, :], 0.0)
    gate = [jnp.where(picked[e], scores[e:e + 1, :] / den, 0.0) for e in range(N_EXPERTS)]
    cls = jnp.zeros_like(den)
    w_a = jnp.zeros_like(den)
    w_b = jnp.zeros_like(den)
    for g in range(N_GROUPS):
        for pi, (a, b) in enumerate(MOE_PAIRS):
            ea, eb = g * EXPERTS_PER_GROUP + a, g * EXPERTS_PER_GROUP + b
            both = picked[ea] & picked[eb]
            cls = jnp.where(both, float(g * len(MOE_PAIRS) + pi), cls)
            w_a = jnp.where(both, gate[ea], w_a)
            w_b = jnp.where(both, gate[eb], w_b)
    o_ref[...] = jnp.concatenate([cls, w_a, w_b, jnp.zeros((ROUTE_ROWS - 3, cls.shape[1]), F32)], axis=0)


MOE_PAIRS = ((0, 1), (0, 2), (0, 3), (1, 3), (1, 2), (2, 3))
N_CLS = N_GROUPS * len(MOE_PAIRS)
CLS_PAD = 32
ROUTE_ROWS = 8
MOE_TS = 256
MOE_TM = 256
SLOT_BLK = 512
ROW_W = D_MODEL + LANES


def router_gates(x, g, mod, router_w, router_b, seq_len, per_seq, tm=512):
    t, d = x.shape
    return pl.pallas_call(
        _router_kernel,
        grid=(t // tm,),
        in_specs=[pl.BlockSpec((tm, d), lambda i: (i, 0)),
                  pl.BlockSpec((1, d), lambda i: (0, 0)),
                  _mod_spec(4, tm, seq_len, per_seq),
                  _mod_spec(3, tm, seq_len, per_seq),
                  pl.BlockSpec((N_EXPERTS, d), lambda i: (0, 0)),
                  pl.BlockSpec((N_EXPERTS, 1), lambda i: (0, 0))],
        out_specs=pl.BlockSpec((ROUTE_ROWS, tm), lambda i: (0, i)),
        out_shape=jax.ShapeDtypeStruct((ROUTE_ROWS, t), F32),
        compiler_params=_cparams("arbitrary"),
        name="moe_router",
    )(x, g.reshape(1, d), mod, mod, router_w.T, router_b.reshape(N_EXPERTS, 1))


def _slots_kernel(n_blk, route_ref, slot_ref, off_ref, cnt_ref):
    cid = lax.broadcasted_iota(jnp.int32, (CLS_PAD, SLOT_BLK), 0).astype(F32)

    def members(j):
        cls = route_ref[0:1, pl.ds(pl.multiple_of(j * SLOT_BLK, SLOT_BLK), SLOT_BLK)]
        return (cid == cls).astype(F32)

    def count(j, cnt):
        return cnt + jnp.sum(members(j), axis=1, keepdims=True)

    cnt = lax.fori_loop(0, n_blk, count, jnp.zeros((CLS_PAD, 1), F32))
    cnt = jnp.broadcast_to(cnt, (CLS_PAD, LANES))
    padded = jnp.ceil(cnt * (1.0 / MOE_TS)) * MOE_TS
    r = lax.broadcasted_iota(jnp.int32, (CLS_PAD, CLS_PAD), 0)
    c = lax.broadcasted_iota(jnp.int32, (CLS_PAD, CLS_PAD), 1)
    off = jnp.dot((c < r).astype(F32), padded, precision=lax.Precision.HIGHEST, preferred_element_type=F32)
    off_ref[...] = off
    cnt_ref[...] = cnt
    tr = lax.broadcasted_iota(jnp.int32, (SLOT_BLK, SLOT_BLK), 0)
    tc = lax.broadcasted_iota(jnp.int32, (SLOT_BLK, SLOT_BLK), 1)
    earlier = (tr < tc).astype(BF16)

    def assign(j, base):
        member = members(j)
        rank = _dot(member.astype(BF16), earlier)
        slot = jnp.sum(member * (rank + base), axis=0, keepdims=True)
        slot_ref[0:1, pl.ds(pl.multiple_of(j * SLOT_BLK, SLOT_BLK), SLOT_BLK)] = slot.astype(jnp.int32)
        return base + jnp.sum(member, axis=1, keepdims=True)

    lax.fori_loop(0, n_blk, assign, off[:, 0:1])


def moe_slots(route):
    t = route.shape[1]
    stat = jax.ShapeDtypeStruct((CLS_PAD, LANES), F32)
    return pl.pallas_call(
        functools.partial(_slots_kernel, t // SLOT_BLK),
        out_shape=[jax.ShapeDtypeStruct((1, t), jnp.int32), stat, stat],
        compiler_params=pltpu.CompilerParams(vmem_limit_bytes=VMEM_LIMIT),
        name="moe_slots",
    )(route)


def _tile_maps(off, cnt, n_tiles):
    off = off[:N_CLS, 0].astype(jnp.int32)
    cnt = cnt[:N_CLS, 0].astype(jnp.int32)
    ends = off + ((cnt + MOE_TS - 1) // MOE_TS) * MOE_TS
    n_used = ends[-1] // MOE_TS
    k = jnp.arange(n_tiles, dtype=jnp.int32)
    tix = jnp.minimum(k, n_used - 1)
    cls = jnp.sum((tix[:, None] * MOE_TS >= ends[None, :]).astype(jnp.int32), axis=1)
    pair = jnp.asarray(MOE_PAIRS, jnp.int32)
    grp = (cls // len(MOE_PAIRS)) * EXPERTS_PER_GROUP
    ea = grp + pair[cls % len(MOE_PAIRS), 0]
    eb = grp + pair[cls % len(MOE_PAIRS), 1]
    n = jnp.int32(n_tiles)

    def slot_plan(e):
        chg = jnp.concatenate([jnp.ones((1,), jnp.int32), (e[1:] != e[:-1]).astype(jnp.int32)])
        at = jnp.where(chg == 1, k, n)
        nxt_at = jnp.concatenate([lax.cummin(at[::-1])[::-1][1:], n.reshape(1)])
        more = (nxt_at < n).astype(jnp.int32)
        nxt = e[jnp.minimum(nxt_at, n - 1)]
        par = (jnp.cumsum(chg) - 1) % 2
        return chg, nxt, more, par.astype(jnp.int32)

    plan_a, plan_b = slot_plan(ea), slot_plan(eb)
    chg, nxt, more, par = (jnp.stack([pa, pb]) for pa, pb in zip(plan_a, plan_b))
    return ea, eb, tix, chg, nxt, more, par, n_used.reshape(1)


def _dispatch_kernel(n_steps, slots_ref, x_ref, g_ref, sc_ref, sh_ref, rt_ref, hs_in, hs_out, rowbuf, sem):
    del hs_in
    i = pl.program_id(0)
    cur = i % 2

    def wait_rows(s):
        pltpu.make_async_copy(rowbuf.at[s], rowbuf.at[s], sem.at[s]).wait()

    @pl.when(i >= 2)
    def _():
        wait_rows(cur)

    rowbuf[cur, :, 0:D_MODEL] = _normmod(x_ref[...], g_ref[...], sc_ref[0], sh_ref[0])
    rowbuf[cur, :, D_MODEL:ROW_W] = jnp.concatenate(
        [rt_ref[...], jnp.zeros((MOE_TM, LANES - ROUTE_ROWS), F32)], axis=1)

    for s in range(2):
        @pl.when(cur == s)
        def _():
            for r in range(MOE_TM):
                dst = slots_ref[i * MOE_TM + r]
                pltpu.make_async_copy(rowbuf.at[s, r], hs_out.at[dst], sem.at[s]).start()

    @pl.when(i == n_steps - 1)
    def _():
        wait_rows(cur)
        if n_steps >= 2:
            wait_rows(1 - cur)


def moe_dispatch(x, g, mod, route_t, slots, hs, seq_len, per_seq):
    t, d = x.shape
    n_steps = t // MOE_TM
    grid_spec = pltpu.PrefetchScalarGridSpec(
        num_scalar_prefetch=1,
        grid=(n_steps,),
        in_specs=[pl.BlockSpec((MOE_TM, d), lambda i, s: (i, 0)),
                  pl.BlockSpec((1, d), lambda i, s: (0, 0)),
                  _mod_spec(4, MOE_TM, seq_len, per_seq),
                  _mod_spec(3, MOE_TM, seq_len, per_seq),
                  pl.BlockSpec((MOE_TM, ROUTE_ROWS), lambda i, s: (i, 0)),
                  pl.BlockSpec(memory_space=pl.ANY)],
        out_specs=pl.BlockSpec(memory_space=pl.ANY),
        scratch_shapes=[pltpu.VMEM((2, MOE_TM, ROW_W), F32), pltpu.SemaphoreType.DMA((2,))],
    )
    return pl.pallas_call(
        functools.partial(_dispatch_kernel, n_steps),
        grid_spec=grid_spec,
        out_shape=jax.ShapeDtypeStruct(hs.shape, F32),
        input_output_aliases={6: 0},
        compiler_params=_cparams("arbitrary"),
        name="moe_dispatch",
    )(slots, x, g.reshape(1, d), mod, mod, route_t, hs)


def _experts_kernel(layer, ea_ref, eb_ref, tix_ref, chg_ref, nxt_ref, more_ref, par_ref, nused_ref,
                    hs_ref, wg_hbm, wu_hbm, wd_hbm, ys_ref, fg, fu, fd, bg, bu, bd, sem):
    k = pl.program_id(0)

    def weight_copies(slot, expert, par):
        return [pltpu.make_async_copy(src.at[layer, expert], dst.at[slot, par], sem.at[slot, par])
                for src, dst in ((wg_hbm, fg), (wu_hbm, fu), (wd_hbm, fd))]

    @pl.when(k < nused_ref[0])
    def _():
        for slot, e_ref in enumerate((ea_ref, eb_ref)):
            @pl.when(chg_ref[slot, k] == 1)
            def _():
                par = par_ref[slot, k]

                @pl.when(k == 0)
                def _():
                    for cp in weight_copies(slot, e_ref[0], par):
                        cp.start()

                for cp in weight_copies(slot, e_ref[k], par):
                    cp.wait()
                bg[slot] = fg[slot, par].astype(BF16)
                bu[slot] = fu[slot, par].astype(BF16)
                bd[slot] = fd[slot, par].astype(BF16)

                @pl.when(more_ref[slot, k] == 1)
                def _():
                    for cp in weight_copies(slot, nxt_ref[slot, k], 1 - par):
                        cp.start()

        h = hs_ref[:, 0:D_MODEL].astype(BF16)

        def ffn(slot):
            hid = _dot(h, bg[slot])
            up = _dot(h, bu[slot])
            w = hs_ref[:, D_MODEL + 1 + slot:D_MODEL + 2 + slot]
            act = (hid * _sigmoid(hid)) * up * w
            return _dot(act.astype(BF16), bd[slot])

        ys_ref[...] = ffn(0) + ffn(1)

    @pl.when(k >= nused_ref[0])
    def _():
        ys_ref[...] = jnp.zeros_like(ys_ref)


def moe_experts(hs, maps, layer, w_gate, w_up, w_down):
    n_tiles = hs.shape[0] // MOE_TS
    d = D_MODEL

    grid_spec = pltpu.PrefetchScalarGridSpec(
        num_scalar_prefetch=8,
        grid=(n_tiles,),
        in_specs=[pl.BlockSpec((MOE_TS, ROW_W), lambda k, ea, eb, tix, *_: (tix[k], 0)),
                  pl.BlockSpec(memory_space=pl.ANY), pl.BlockSpec(memory_space=pl.ANY),
                  pl.BlockSpec(memory_space=pl.ANY)],
        out_specs=pl.BlockSpec((MOE_TS, d), lambda k, *_: (k, 0)),
        scratch_shapes=[pltpu.VMEM((2, 2, d, D_EXPERT), F32), pltpu.VMEM((2, 2, d, D_EXPERT), F32),
                        pltpu.VMEM((2, 2, D_EXPERT, d), F32),
                        pltpu.VMEM((2, d, D_EXPERT), BF16), pltpu.VMEM((2, d, D_EXPERT), BF16),
                        pltpu.VMEM((2, D_EXPERT, d), BF16),
                        pltpu.SemaphoreType.DMA((2, 2))],
    )
    return pl.pallas_call(
        functools.partial(_experts_kernel, layer),
        grid_spec=grid_spec,
        out_shape=jax.ShapeDtypeStruct((hs.shape[0], d), F32),
        compiler_params=_cparams("arbitrary"),
        name="moe_experts",
    )(*maps, hs, w_gate, w_up, w_down)


def _combine_kernel(final, n_steps, slots_ref, x_ref, g2_ref, fg_ref, ys_hbm, o_ref, gbuf, sem):
    i = pl.program_id(0)
    cur = i % 2

    def issue_tile(tile, s):
        for r in range(MOE_TM):
            src = slots_ref[tile * MOE_TM + r]
            pltpu.make_async_copy(ys_hbm.at[src], gbuf.at[s, r], sem.at[s]).start()

    @pl.when(i == 0)
    def _():
        issue_tile(0, 0)

    for s in range(2):
        @pl.when((i + 1 < n_steps) & (1 - cur == s))
        def _():
            issue_tile(i + 1, s)

    pltpu.make_async_copy(gbuf.at[cur], gbuf.at[cur], sem.at[cur]).wait()
    y = x_ref[...] + g2_ref[0] * gbuf[cur]
    if final:
        ms = jnp.mean(y * y, axis=-1, keepdims=True)
        y = y * lax.rsqrt(ms + EPS) * fg_ref[...]
    o_ref[...] = y


def moe_combine(x, mod, slots, ys, final_g, final, seq_len, per_seq):
    t, d = x.shape
    n_steps = t // MOE_TM
    grid_spec = pltpu.PrefetchScalarGridSpec(
        num_scalar_prefetch=1,
        grid=(n_steps,),
        in_specs=[pl.BlockSpec((MOE_TM, d), lambda i, s: (i, 0)),
                  _mod_spec(5, MOE_TM, seq_len, per_seq),
                  pl.BlockSpec((1, d), lambda i, s: (0, 0)),
                  pl.BlockSpec(memory_space=pl.ANY)],
        out_specs=pl.BlockSpec((MOE_TM, d), lambda i, s: (i, 0)),
        scratch_shapes=[pltpu.VMEM((2, MOE_TM, d), F32), pltpu.SemaphoreType.DMA((2,))],
    )
    return pl.pallas_call(
        functools.partial(_combine_kernel, final, n_steps),
        grid_spec=grid_spec,
        out_shape=jax.ShapeDtypeStruct((t, d), F32),
        compiler_params=_cparams("arbitrary"),
        name="moe_combine",
    )(slots, x, mod, final_g.reshape(1, d), ys)


def _combine_nm_tm_kernel(nseq, steps, seq_len, n_steps, slots_ref, x_ref, g2_ref, ys_hbm, g_ref, sc_ref,
                          sh_ref, w_ref, x2_ref, u_ref, gbuf, sem):
    i = pl.program_id(0)
    cur = i % 2

    def issue_tile(tile, s):
        for sq in range(nseq):
            for t in range(steps):
                src = slots_ref[sq * seq_len + tile * steps + t]
                pltpu.make_async_copy(ys_hbm.at[src], gbuf.at[s, sq * steps + t], sem.at[s]).start()

    @pl.when(i == 0)
    def _():
        issue_tile(0, 0)

    for s in range(2):
        @pl.when((i + 1 < n_steps) & (1 - cur == s))
        def _():
            issue_tile(i + 1, s)

    pltpu.make_async_copy(gbuf.at[cur], gbuf.at[cur], sem.at[cur]).wait()
    d = x_ref.shape[-1]
    y = x_ref[...] + g2_ref[...] * gbuf[cur].reshape(nseq, steps, d)
    x2_ref[...] = y
    h = _normmod(y, g_ref[...], sc_ref[...], sh_ref[...])
    h = h.reshape(nseq * steps, d).astype(BF16)
    h = _dot(_row_permutation(nseq, steps, True), h).astype(BF16)
    u_ref[...] = _dot(h, w_ref[...])


def combine_nm_matmul_tm(x, mod_prev, slots, ys, g, mod, w, nseq, seq_len, per_seq):
    t, d = x.shape
    n = w.shape[1]
    steps = TM_ROWS // nseq
    n_steps = seq_len // steps
    g2 = _group_mod(mod_prev, 5, nseq, per_seq)
    sc = _group_mod(mod, 1, nseq, per_seq)
    sh = _group_mod(mod, 0, nseq, per_seq)
    mod_spec = pl.BlockSpec(sc.shape, lambda i, s: (0, 0, 0))
    x_spec = pl.BlockSpec((nseq, steps, d), lambda i, s: (0, i, 0))
    grid_spec = pltpu.PrefetchScalarGridSpec(
        num_scalar_prefetch=1,
        grid=(n_steps,),
        in_specs=[x_spec, mod_spec, pl.BlockSpec(memory_space=pl.ANY),
                  pl.BlockSpec((1, d), lambda i, s: (0, 0)), mod_spec, mod_spec,
                  pl.BlockSpec((d, n), lambda i, s: (0, 0))],
        out_specs=[x_spec, pl.BlockSpec((TM_ROWS, n), lambda i, s: (i, 0))],
        scratch_shapes=[pltpu.VMEM((2, TM_ROWS, d), F32), pltpu.SemaphoreType.DMA((2,))],
    )
    x2, u = pl.pallas_call(
        functools.partial(_combine_nm_tm_kernel, nseq, steps, seq_len, n_steps),
        grid_spec=grid_spec,
        out_shape=[jax.ShapeDtypeStruct((nseq, seq_len, d), F32), jax.ShapeDtypeStruct((t, n), F32)],
        compiler_params=_cparams("arbitrary"),
        name="moe_combine_norm_mod_proj_tm",
    )(slots, x.reshape(nseq, seq_len, d), g2, ys, g.reshape(1, d), sc, sh, w)
    return x2.reshape(t, d), u


def moe_block(xs, mods_l, per_seqs, seq_lens, layer, p, final, defer_combine):
    g = p['norm_g'][layer, 1]
    routes = [router_gates(x, g, mods_l, p['router_w'], p['router_b'], sl, ps)
              for x, sl, ps in zip(xs, seq_lens, per_seqs)]
    slots, off, cnt = moe_slots(jnp.concatenate(routes, axis=1))
    t_all = slots.shape[1]
    n_tiles = t_all // MOE_TS + N_CLS
    maps = _tile_maps(off, cnt, n_tiles)
    hs = jnp.zeros((n_tiles * MOE_TS, ROW_W), F32)
    bounds = np.cumsum([0] + [x.shape[0] for x in xs])
    group_slots = [slots[0, bounds[i]:bounds[i + 1]] for i in range(len(xs))]
    for x, r, s, sl, ps in zip(xs, routes, group_slots, seq_lens, per_seqs):
        hs = moe_dispatch(x, g, mods_l, r.T, s, hs, sl, ps)
    ys = moe_experts(hs, maps, layer, p['moe_w_gate'], p['moe_w_up'], p['moe_w_down'])
    if defer_combine:
        return [(x, s, ys) for x, s in zip(xs, group_slots)]
    return [moe_combine(x, mods_l, s, ys, p['final_g'], final, sl, ps)
            for x, s, sl, ps in zip(xs, group_slots, seq_lens, per_seqs)]


def _mixer(x, group, l, mod, p, hy_tables, mod_prev):
    per_seq, nseq, seq_len = group['per_seq'], group['nseq'], group['seq_len']
    extras = None
    deferred = isinstance(x, tuple)
    if deferred and l % 2 == 0:
        x = moe_combine(x[0], mod_prev, x[1], x[2], p['final_g'], False, seq_len, per_seq)
        deferred = False
    if l % 2 == 0:
        e = l // 2
        u = nm_matmul(x, p['norm_g'][l, 0], mod, p['a_in_w'][e].astype(BF16), seq_len, per_seq)
        if group['ctx_k'] is None:
            attn, nk, nv = context_attention(u, nseq, seq_len)
            extras = (nk, nv)
        else:
            attn = neighbourhood_attention(u, group['ctx_k'][:, e], group['ctx_v'][:, e], p['na_rpb'][e],
                                           nseq, seq_len)
        fwd, inv = hy_tables[seq_len]
        spectrum = hyena_spectrum(seq_len, p['hy_w1'][e], p['hy_b1'][e], p['hy_w2'][e], p['hy_b2'][e],
                                  p['hy_w3'][e], p['hy_freq'][e], p['hy_d'][e], fwd)
        hy = hyena_mixer(u, nseq, seq_len, p['hy_short_w'][e], p['hy_short_b'][e], spectrum, fwd, inv)
        w_out = p['a_out_w'][e].astype(BF16)
        x = proj_residual([attn, hy], [w_out[:D_A], w_out[D_A:]], x, mod, seq_len, per_seq)
    else:
        o = l // 2
        w_in = p['c_in_w'][o].astype(BF16)
        if deferred:
            x, u = combine_nm_matmul_tm(x[0], mod_prev, x[1], x[2], p['norm_g'][l, 0], mod, w_in,
                                        nseq, seq_len, per_seq)
        else:
            u = nm_matmul_tm(x, p['norm_g'][l, 0], mod, w_in, nseq, seq_len, per_seq)
        y, extras = rglru_block(u, nseq, seq_len, p['rg_conv_w'][o], p['rg_conv_b'][o], p['rg_wa'][o],
                                p['rg_ba'][o], p['rg_wx'][o], p['rg_bx'][o], p['rg_lam'][o], group['h0'][o])
        x = proj_residual_tm(y, p['c_out_w'][o].astype(BF16), x, mod, nseq, seq_len, per_seq)
    return x, extras


def kernel(x_prompt, x_sample, cache_k, cache_v, state_h, c, c_ctx, norm_g, ada_w, ada_b, final_g, a_in_w, a_out_w, na_rpb, hy_short_w, hy_short_b, hy_w1, hy_b1, hy_w2, hy_b2, hy_w3, hy_freq, hy_d, c_in_w, c_out_w, rg_conv_w, rg_conv_b, rg_wa, rg_ba, rg_wx, rg_bx, rg_lam, router_w, router_b, moe_w_gate, moe_w_up, moe_w_down):
    p = dict(norm_g=norm_g, final_g=final_g, a_in_w=a_in_w, a_out_w=a_out_w, na_rpb=na_rpb,
             hy_short_w=hy_short_w, hy_short_b=hy_short_b, hy_w1=hy_w1, hy_b1=hy_b1, hy_w2=hy_w2,
             hy_b2=hy_b2, hy_w3=hy_w3, hy_freq=hy_freq, hy_d=hy_d, c_in_w=c_in_w, c_out_w=c_out_w,
             rg_conv_w=rg_conv_w, rg_conv_b=rg_conv_b, rg_wa=rg_wa, rg_ba=rg_ba, rg_wx=rg_wx, rg_bx=rg_bx,
             rg_lam=rg_lam, router_w=router_w, router_b=router_b, moe_w_gate=moe_w_gate,
             moe_w_up=moe_w_up, moe_w_down=moe_w_down)
    batch, seq, d = x_prompt.shape
    dec_batch, dec_seq, _ = x_sample.shape
    n_odd = DEPTH // 2
    assert 1 + dec_batch <= MOD_ROWS

    cond = jnp.concatenate([c_ctx[None, :], c, jnp.zeros((MOD_ROWS - 1 - dec_batch, d), F32)], axis=0)
    m = modulation(cond, ada_w, ada_b)
    mods = [m[l].reshape(MOD_ROWS * N_MOD, 1, d) for l in range(DEPTH)]

    tables = {}
    for sl in (seq, dec_seq):
        fwd, inv = _dft_tables(sl)
        tables[sl] = (jnp.asarray(fwd).astype(BF16), jnp.asarray(inv).astype(BF16))

    groups = [
        dict(per_seq=False, nseq=batch, seq_len=seq, ctx_k=None, ctx_v=None,
             h0=[jnp.zeros((2, batch, D_RNN), F32)] * n_odd),
        dict(per_seq=True, nseq=dec_batch, seq_len=dec_seq, ctx_k=cache_k, ctx_v=cache_v,
             h0=[state_h[:, o].transpose(1, 0, 2) for o in range(n_odd)]),
    ]
    xs = [x_prompt.reshape(batch * seq, d), x_sample.reshape(dec_batch * dec_seq, d)]
    k_list, v_list, h_list = [], [], []
    for l in range(DEPTH):
        mixed = [_mixer(x, grp, l, mods[l], p, tables, mods[l - 1] if l else None)
                 for x, grp in zip(xs, groups)]
        if l % 2 == 0:
            k_list.append(mixed[0][1][0])
            v_list.append(mixed[0][1][1])
        else:
            h_list.append(mixed[0][1].transpose(1, 0, 2))
        xs = moe_block([mx[0] for mx in mixed], mods[l], [grp['per_seq'] for grp in groups],
                       [grp['seq_len'] for grp in groups], l, p, l == DEPTH - 1, l < DEPTH - 1)
    new_k = jnp.stack(k_list, axis=1)
    new_v = jnp.stack(v_list, axis=1)
    new_h = jnp.stack(h_list, axis=1)
    return (xs[0].reshape(batch, seq, d), xs[1].reshape(dec_batch, dec_seq, d), new_k, new_v, new_h)
```

```python
import functools
import math

import numpy as np
import jax
import jax.numpy as jnp
from jax import lax
from jax.experimental import pallas as pl
from jax.experimental.pallas import tpu as pltpu

F32 = jnp.float32
BF16 = jnp.bfloat16

D_MODEL = 1024
DEPTH = 2
GRID_W = 64
EPS = 1e-6
NEG_INF = -1e30
NA_HEADS = 8
HEAD_DIM = 64
D_A = NA_HEADS * HEAD_DIM
WIN_ROWS = 8
WIN_COLS = 16
D_B = D_MODEL - D_A
HY_ORDER = 2
HY_EMB = 33
HY_BANDS = (HY_EMB - 1) // 2
HY_FFN = 64
HY_DECAY_TARGET = 1e-2
HY_FAST_PCT = 0.3
HY_SLOW_PCT = 1.5
D_RNN = D_MODEL
RG_BLOCK = 64
RG_C = 8.0
N_EXPERTS = 16
N_GROUPS = 4
EXPERTS_PER_GROUP = N_EXPERTS // N_GROUPS
D_EXPERT = 512

LANES = 128
VMEM_LIMIT = 56 * 1024 * 1024
N_MOD = 6
MOD_ROWS = 16


def _cparams(*sem):
    return pltpu.CompilerParams(dimension_semantics=sem, vmem_limit_bytes=VMEM_LIMIT)


def _dot(a, b):
    return jnp.dot(a, b, preferred_element_type=F32)


def _dot_nt(a, b):
    return lax.dot_general(a, b, (((1,), (1,)), ((), ())), preferred_element_type=F32)


def _sigmoid(x):
    return 0.5 * jnp.tanh(0.5 * x) + 0.5


def _normmod(x, g, sc, sh):
    ms = jnp.mean(x * x, axis=-1, keepdims=True)
    return (x * lax.rsqrt(ms + EPS) * g) * (1.0 + sc) + sh


def _mod_spec(chunk, tm, seq_len, per_seq):
    if per_seq:
        return pl.BlockSpec((1, 1, D_MODEL), lambda i, *_: ((1 + (i * tm) // seq_len) * N_MOD + chunk, 0, 0))
    return pl.BlockSpec((1, 1, D_MODEL), lambda i, *_: (chunk, 0, 0))


def _mod_kernel(c_ref, w_ref, b_ref, o_ref):
    s = c_ref[...]
    s = s * jax.nn.sigmoid(s)
    o_ref[0] = _dot(s.astype(BF16), w_ref[0].astype(BF16)) + b_ref[0]


def modulation(cond, ada_w, ada_b):
    tn = 1536
    n = ada_w.shape[-1]
    return pl.pallas_call(
        _mod_kernel,
        grid=(DEPTH, n // tn),
        in_specs=[pl.BlockSpec((MOD_ROWS, D_MODEL), lambda l, j: (0, 0)),
                  pl.BlockSpec((1, D_MODEL, tn), lambda l, j: (l, 0, j)),
                  pl.BlockSpec((1, 1, tn), lambda l, j: (l, 0, j))],
        out_specs=pl.BlockSpec((1, MOD_ROWS, tn), lambda l, j: (l, 0, j)),
        out_shape=jax.ShapeDtypeStruct((DEPTH, MOD_ROWS, n), F32),
        compiler_params=_cparams("arbitrary", "arbitrary"),
        name="modulation",
    )(cond, ada_w, ada_b.reshape(DEPTH, 1, n))


def _nm_matmul_kernel(x_ref, g_ref, sc_ref, sh_ref, w_ref, o_ref):
    h = _normmod(x_ref[...], g_ref[...], sc_ref[0], sh_ref[0])
    o_ref[...] = _dot(h.astype(BF16), w_ref[...])


def nm_matmul(x, g, mod, w, seq_len, per_seq, tm=512):
    t, d = x.shape
    n = w.shape[1]
    return pl.pallas_call(
        _nm_matmul_kernel,
        grid=(t // tm,),
        in_specs=[pl.BlockSpec((tm, d), lambda i: (i, 0)),
                  pl.BlockSpec((1, d), lambda i: (0, 0)),
                  _mod_spec(1, tm, seq_len, per_seq),
                  _mod_spec(0, tm, seq_len, per_seq),
                  pl.BlockSpec((d, n), lambda i: (0, 0))],
        out_specs=pl.BlockSpec((tm, n), lambda i: (i, 0)),
        out_shape=jax.ShapeDtypeStruct((t, n), F32),
        compiler_params=_cparams("arbitrary"),
        name="norm_mod_proj",
    )(x, g.reshape(1, d), mod, mod, w)


def _proj_res_kernel(n_act, *refs):
    acts = refs[:n_act]
    ws = refs[n_act:2 * n_act]
    x_ref, g_ref, o_ref = refs[2 * n_act:]
    acc = _dot(acts[0][...].astype(BF16), ws[0][...])
    for a, w in zip(acts[1:], ws[1:]):
        acc += _dot(a[...].astype(BF16), w[...])
    o_ref[...] = x_ref[...] + g_ref[0] * acc


def proj_residual(acts, ws, x, mod, seq_len, per_seq, tm=512):
    t, d = x.shape
    in_specs = [pl.BlockSpec((tm, a.shape[1]), lambda i: (i, 0)) for a in acts]
    in_specs += [pl.BlockSpec(w.shape, lambda i: (0, 0)) for w in ws]
    in_specs += [pl.BlockSpec((tm, d), lambda i: (i, 0)), _mod_spec(2, tm, seq_len, per_seq)]
    return pl.pallas_call(
        functools.partial(_proj_res_kernel, len(acts)),
        grid=(t // tm,),
        in_specs=in_specs,
        out_specs=pl.BlockSpec((tm, d), lambda i: (i, 0)),
        out_shape=jax.ShapeDtypeStruct((t, d), F32),
        compiler_params=_cparams("arbitrary"),
        name="proj_residual",
    )(*acts, *ws, x, mod)


def _row_permutation(nseq, steps, to_time_major):
    n = nseq * steps
    i = lax.broadcasted_iota(jnp.int32, (n, n), 0)
    j = lax.broadcasted_iota(jnp.int32, (n, n), 1)
    if to_time_major:
        src = (i % nseq) * steps + i // nseq
    else:
        src = (i % steps) * nseq + i // steps
    return (j == src).astype(BF16)


def _nm_matmul_tm_kernel(nseq, steps, x_ref, g_ref, sc_ref, sh_ref, w_ref, o_ref):
    h = _normmod(x_ref[...], g_ref[...], sc_ref[...], sh_ref[...])
    h = h.reshape(nseq * steps, h.shape[-1]).astype(BF16)
    h = _dot(_row_permutation(nseq, steps, True), h).astype(BF16)
    o_ref[...] = _dot(h, w_ref[...])


def _group_mod(mod, chunk, nseq, per_seq):
    rows = mod.reshape(MOD_ROWS, N_MOD, 1, D_MODEL)
    return rows[1:1 + nseq, chunk] if per_seq else rows[0:1, chunk]


def nm_matmul_tm(x, g, mod, w, nseq, seq_len, per_seq):
    t, d = x.shape
    n = w.shape[1]
    steps = TM_ROWS // nseq
    sc = _group_mod(mod, 1, nseq, per_seq)
    sh = _group_mod(mod, 0, nseq, per_seq)
    mod_spec = pl.BlockSpec(sc.shape, lambda i: (0, 0, 0))
    return pl.pallas_call(
        functools.partial(_nm_matmul_tm_kernel, nseq, steps),
        grid=(seq_len // steps,),
        in_specs=[pl.BlockSpec((nseq, steps, d), lambda i: (0, i, 0)),
                  pl.BlockSpec((1, d), lambda i: (0, 0)),
                  mod_spec, mod_spec,
                  pl.BlockSpec((d, n), lambda i: (0, 0))],
        out_specs=pl.BlockSpec((TM_ROWS, n), lambda i: (i, 0)),
        out_shape=jax.ShapeDtypeStruct((t, n), F32),
        compiler_params=_cparams("arbitrary"),
        name="norm_mod_proj_tm",
    )(x.reshape(nseq, seq_len, d), g.reshape(1, d), sc, sh, w)


def _proj_res_tm_kernel(nseq, steps, y_ref, w_ref, x_ref, g_ref, o_ref):
    y = _dot(_row_permutation(nseq, steps, False), y_ref[...].astype(BF16)).astype(BF16)
    acc = _dot(y, w_ref[...])
    o_ref[...] = x_ref[...] + g_ref[...] * acc.reshape(nseq, steps, acc.shape[-1])


def proj_residual_tm(y, w, x, mod, nseq, seq_len, per_seq):
    t, d = x.shape
    steps = TM_ROWS // nseq
    g1 = _group_mod(mod, 2, nseq, per_seq)
    out = pl.pallas_call(
        functools.partial(_proj_res_tm_kernel, nseq, steps),
        grid=(seq_len // steps,),
        in_specs=[pl.BlockSpec((TM_ROWS, y.shape[1]), lambda i: (i, 0)),
                  pl.BlockSpec(w.shape, lambda i: (0, 0)),
                  pl.BlockSpec((nseq, steps, d), lambda i: (0, i, 0)),
                  pl.BlockSpec(g1.shape, lambda i: (0, 0, 0))],
        out_specs=pl.BlockSpec((nseq, steps, d), lambda i: (0, i, 0)),
        out_shape=jax.ShapeDtypeStruct((nseq, seq_len, d), F32),
        compiler_params=_cparams("arbitrary"),
        name="proj_residual_tm",
    )(y, w, x.reshape(nseq, seq_len, d), g1)
    return out.reshape(t, d)


def _ctx_attn_kernel(q_ref, k_ref, v_ref, o_ref, nk_ref, nv_ref):
    scale = HEAD_DIM ** -0.5
    per_tile = LANES // HEAD_DIM
    lane = lax.broadcasted_iota(jnp.int32, (1, LANES), 1)
    for hp in range(NA_HEADS // per_tile):
        sl = slice(hp * LANES, (hp + 1) * LANES)
        q, k, v = q_ref[:, sl] * scale, k_ref[:, sl], v_ref[:, sl]
        kb, vb = k.astype(BF16), v.astype(BF16)
        out = None
        for j in range(per_tile):
            h = hp * per_tile + j
            nk_ref[0, h] = k[:, j * HEAD_DIM:(j + 1) * HEAD_DIM]
            nv_ref[0, h] = v[:, j * HEAD_DIM:(j + 1) * HEAD_DIM]
            mine = lane // HEAD_DIM == j
            s = _dot_nt(jnp.where(mine, q, 0.0).astype(BF16), kb)
            p = jnp.exp(s - jnp.max(s, axis=-1, keepdims=True))
            o = _dot(p.astype(BF16), vb) / jnp.sum(p, axis=-1, keepdims=True)
            out = o if out is None else jnp.where(mine, o, out)
        o_ref[:, sl] = out


def context_attention(u, nseq, seq_len):
    t = u.shape[0]
    kv_shape = jax.ShapeDtypeStruct((nseq, NA_HEADS, seq_len, HEAD_DIM), F32)
    kv_spec = pl.BlockSpec((1, NA_HEADS, seq_len, HEAD_DIM), lambda b: (b, 0, 0, 0))
    return pl.pallas_call(
        _ctx_attn_kernel,
        grid=(nseq,),
        in_specs=[pl.BlockSpec((seq_len, D_A), lambda b: (b, 0)),
                  pl.BlockSpec((seq_len, D_A), lambda b: (b, 1)),
                  pl.BlockSpec((seq_len, D_A), lambda b: (b, 2))],
        out_specs=[pl.BlockSpec((seq_len, D_A), lambda b: (b, 0)), kv_spec, kv_spec],
        out_shape=[jax.ShapeDtypeStruct((t, D_A), F32), kv_shape, kv_shape],
        compiler_params=_cparams("arbitrary"),
        name="context_attention",
    )(u, u, u)


N_DR = 2 * WIN_ROWS - 1
N_DC = 2 * WIN_COLS - 1


def _na_col_tables():
    cols = np.arange(GRID_W)
    col_start = np.clip(cols - WIN_COLS // 2, 0, GRID_W - WIN_COLS)
    col_in = (cols[None, :] >= col_start[:, None]) & (cols[None, :] < col_start[:, None] + WIN_COLS)
    dc = np.clip(cols[None, :] - cols[:, None], 1 - WIN_COLS, WIN_COLS - 1) + WIN_COLS - 1
    onehot = (dc.reshape(1, -1) == np.arange(32)[:, None]).astype(np.float32)
    return onehot, col_in.reshape(1, -1).astype(np.float32)


def _na_bias_kernel(r_ref, e_ref, m_ref, o_ref):
    t = jnp.dot(r_ref[...], e_ref[...], precision=lax.Precision.HIGHEST, preferred_element_type=F32)
    o_ref[...] = jnp.where(m_ref[...] > 0.0, t, NEG_INF)


def na_bias_table(rpb):
    onehot, col_in = _na_col_tables()
    n_rows = NA_HEADS * N_DR
    r = jnp.zeros((LANES, 32), F32).at[:n_rows, :N_DC].set(rpb.reshape(n_rows, N_DC).astype(F32))
    t = pl.pallas_call(
        _na_bias_kernel,
        out_shape=jax.ShapeDtypeStruct((LANES, GRID_W * GRID_W), F32),
        name="na_bias_table",
    )(r, jnp.asarray(onehot), jnp.asarray(col_in))
    t = t[:n_rows].reshape(NA_HEADS, N_DR, GRID_W, GRID_W)
    return jnp.concatenate([t[:, :-1], t[:, 1:]], axis=-1)


def _na_kernel(rows, q_ref, k_ref, v_ref, ck_ref, cv_ref, bias_ref, o_ref,
               q_s, k_s, v_s, ck_s, cv_s, s_s, p_s, den_s, o_s):
    scale = HEAD_DIM ** -0.5
    n_lat = WIN_ROWS * GRID_W
    for h in range(NA_HEADS):
        sl = slice(h * HEAD_DIM, (h + 1) * HEAD_DIM)
        q_s[h] = (q_ref[:, sl] * scale).astype(BF16)
        k_s[h] = k_ref[:, sl].astype(BF16)
        v_s[h] = v_ref[:, sl].astype(BF16)
    ck_s[...] = ck_ref[0].astype(BF16)
    cv_s[...] = cv_ref[0].astype(BF16)

    def window(r):
        start = min(max(r - WIN_ROWS // 2, 0), rows - WIN_ROWS)
        return start, start - r + WIN_ROWS - 1

    def head_body(h, carry):
        for r in range(rows):
            start, off = window(r)
            q = q_s[h, r * GRID_W:(r + 1) * GRID_W, :]
            bias = jnp.concatenate([bias_ref[h, off + 2 * i] for i in range(WIN_ROWS // 2)], axis=1)
            s_s[r * GRID_W:(r + 1) * GRID_W, 0:n_lat] = (
                _dot_nt(q, k_s[h, start * GRID_W:start * GRID_W + n_lat, :]) + bias)
            s_s[r * GRID_W:(r + 1) * GRID_W, n_lat:] = _dot_nt(q, ck_s[h])
        for r in range(rows):
            rs = slice(r * GRID_W, (r + 1) * GRID_W)
            s = s_s[rs, :]
            p = jnp.exp(s - jnp.max(s, axis=-1, keepdims=True))
            den_s[rs, :] = jnp.sum(p, axis=-1, keepdims=True)
            p_s[rs, :] = p.astype(BF16)
        for r in range(rows):
            start, _ = window(r)
            rs = slice(r * GRID_W, (r + 1) * GRID_W)
            o = (_dot(p_s[rs, 0:n_lat], v_s[h, start * GRID_W:start * GRID_W + n_lat, :])
                 + _dot(p_s[rs, n_lat:], cv_s[h]))
            o_s[h, rs, :] = o / den_s[rs, :]
        return carry

    lax.fori_loop(0, NA_HEADS, head_body, 0)
    for h in range(NA_HEADS):
        o_ref[:, h * HEAD_DIM:(h + 1) * HEAD_DIM] = o_s[h]


def neighbourhood_attention(u, ctx_k, ctx_v, rpb, nseq, seq_len):
    t = u.shape[0]
    rows = seq_len // GRID_W
    assert rows >= WIN_ROWS and WIN_ROWS % 2 == 0
    past = ctx_k.shape[2]
    bias = na_bias_table(rpb)
    ctx_spec = pl.BlockSpec((1, NA_HEADS, past, HEAD_DIM), lambda b: (b, 0, 0, 0))
    return pl.pallas_call(
        functools.partial(_na_kernel, rows),
        grid=(nseq,),
        in_specs=[pl.BlockSpec((seq_len, D_A), lambda b: (b, 0)),
                  pl.BlockSpec((seq_len, D_A), lambda b: (b, 1)),
                  pl.BlockSpec((seq_len, D_A), lambda b: (b, 2)),
                  ctx_spec, ctx_spec,
                  pl.BlockSpec(bias.shape, lambda b: (0, 0, 0, 0))],
        out_specs=pl.BlockSpec((seq_len, D_A), lambda b: (b, 0)),
        out_shape=jax.ShapeDtypeStruct((t, D_A), F32),
        scratch_shapes=[pltpu.VMEM((NA_HEADS, seq_len, HEAD_DIM), BF16)] * 3
        + [pltpu.VMEM((NA_HEADS, past, HEAD_DIM), BF16)] * 2
        + [pltpu.VMEM((seq_len, WIN_ROWS * GRID_W + past), F32),
           pltpu.VMEM((seq_len, WIN_ROWS * GRID_W + past), BF16),
           pltpu.VMEM((seq_len, 1), F32),
           pltpu.VMEM((NA_HEADS, seq_len, HEAD_DIM), F32)],
        compiler_params=_cparams("arbitrary"),
        name="neighbourhood_attention",
    )(u, u, u, ctx_k, ctx_v, bias)


def _dft_tables(seq_len):
    n = 2 * seq_len
    f = np.arange(seq_len, dtype=np.int64)
    ang = (np.outer(f, f) % n).astype(np.float64) * (math.pi / seq_len)
    cos, sin = np.cos(ang), np.sin(ang)
    alt = np.where(f % 2 == 0, 1.0, -1.0)
    s_fwd = -sin
    s_fwd[0, :] = alt
    fwd = np.concatenate([cos, s_fwd], axis=0)
    wf = np.where(f == 0, 1.0, 2.0) / n
    ci = cos.T * wf[None, :]
    si = -sin.T * wf[None, :]
    si[:, 0] = alt / n
    inv = np.concatenate([ci, si], axis=1)
    return fwd.astype(np.float32), inv.astype(np.float32)


def _hyena_feats(seq_len):
    t = np.linspace(0.0, 1.0, seq_len, dtype=np.float32)[:, None]
    w = (2.0 * math.pi * np.arange(seq_len, dtype=np.float32)[:, None] / seq_len).astype(np.float32)
    f = np.linspace(1e-4, HY_BANDS - 1, HY_BANDS, dtype=np.float32)[None, :]
    z = np.concatenate([t, np.cos(f * w), -np.sin(f * w)], axis=-1).astype(np.float32)
    max_decay = math.log(HY_DECAY_TARGET) / HY_FAST_PCT
    min_decay = math.log(HY_DECAY_TARGET) / HY_SLOW_PCT
    deltas = np.abs(np.linspace(min_decay, max_decay, D_B, dtype=np.float32))[None, :]
    return z, t, deltas


def _hy_filter_kernel(seq_len, z_ref, t_ref, dl_ref, w1_ref, b1_ref, w2_ref, b2_ref, w3_ref, fr_ref,
                      d_ref, fwd_ref, g_ref):
    hp = lax.Precision.HIGHEST
    h = jnp.sin(fr_ref[0:1, :] * (jnp.dot(z_ref[...], w1_ref[...], precision=hp) + b1_ref[...]))
    h = jnp.sin(fr_ref[1:2, :] * (jnp.dot(h, w2_ref[...], precision=hp) + b2_ref[...]))
    h = jnp.dot(h, w3_ref[...], precision=hp)
    decay = jnp.exp(-t_ref[...] * dl_ref[...])
    row0 = lax.broadcasted_iota(jnp.int32, (seq_len, D_B), 0) == 0
    sums, diffs = [], []
    for n in range(HY_ORDER):
        hf = h[:, (2 * n) * D_B:(2 * n + 1) * D_B] * decay
        hb = h[:, (2 * n + 1) * D_B:(2 * n + 2) * D_B] * decay
        gp = jnp.where(row0, hf + hb + d_ref[n:n + 1, :], hf)
        gm = jnp.where(row0, 0.0, hb)
        sums.append(gp + gm)
        diffs.append(gp - gm)
    rhs = jnp.concatenate(sums + diffs, axis=1).astype(BF16)
    spec = _dot(fwd_ref[...], rhs)
    for n in range(HY_ORDER):
        a = spec[:, n * D_B:(n + 1) * D_B]
        b = spec[:, (HY_ORDER + n) * D_B:(HY_ORDER + n + 1) * D_B]
        g_ref[n, 0:seq_len, :] = a[0:seq_len]
        g_ref[n, seq_len:, :] = jnp.where(row0, a[seq_len:], b[seq_len:])


def hyena_spectrum(seq_len, w1, b1, w2, b2, w3, freq, d, fwd):
    z, t, deltas = _hyena_feats(seq_len)
    return pl.pallas_call(
        functools.partial(_hy_filter_kernel, seq_len),
        out_shape=jax.ShapeDtypeStruct((HY_ORDER, 2 * seq_len, D_B), F32),
        compiler_params=pltpu.CompilerParams(vmem_limit_bytes=VMEM_LIMIT),
        name="hyena_spectrum",
    )(jnp.asarray(z), jnp.asarray(t), jnp.asarray(deltas), w1, b1.reshape(1, -1), w2, b2.reshape(1, -1),
      w3, freq, d, fwd)


def _hyena_kernel(seq_len, u_ref, sw_ref, sb_ref, g_ref, fwd_ref, inv_ref, o_ref):
    u = u_ref[...]
    t_idx = lax.broadcasted_iota(jnp.int32, u.shape, 0)
    prev = jnp.where(t_idx == 0, 0.0, pltpu.roll(u, 1, axis=0))
    nxt = jnp.where(t_idx == seq_len - 1, 0.0, pltpu.roll(u, seq_len - 1, axis=0))
    u = prev * sw_ref[0:1, :] + u * sw_ref[1:2, :] + nxt * sw_ref[2:3, :] + sb_ref[...]
    row0 = lax.broadcasted_iota(jnp.int32, (seq_len, D_B), 0) == 0
    z = u[:, 0:D_B]
    for n in range(HY_ORDER):
        spec = _dot(fwd_ref[...], z.astype(BF16))
        ure, uim = spec[0:seq_len], spec[seq_len:]
        gre, gim = g_ref[n, 0:seq_len, :], g_ref[n, seq_len:, :]
        pim = uim * gim
        yre = ure * gre - jnp.where(row0, 0.0, pim)
        yim = jnp.where(row0, pim, ure * gim + uim * gre)
        y = jnp.concatenate([yre, yim], axis=0).astype(BF16)
        z = u[:, (n + 1) * D_B:(n + 2) * D_B] * _dot(inv_ref[...], y)
    o_ref[...] = z


def hyena_mixer(u, nseq, seq_len, short_w, short_b, spectrum, fwd, inv):
    t = u.shape[0]
    width = (HY_ORDER + 1) * D_B
    col_block = (3 * D_A) // width
    assert col_block * width == 3 * D_A
    return pl.pallas_call(
        functools.partial(_hyena_kernel, seq_len),
        grid=(nseq,),
        in_specs=[pl.BlockSpec((seq_len, width), lambda b: (b, col_block)),
                  pl.BlockSpec(short_w.shape, lambda b: (0, 0)),
                  pl.BlockSpec((1, width), lambda b: (0, 0)),
                  pl.BlockSpec(spectrum.shape, lambda b: (0, 0, 0)),
                  pl.BlockSpec(fwd.shape, lambda b: (0, 0)),
                  pl.BlockSpec(inv.shape, lambda b: (0, 0))],
        out_specs=pl.BlockSpec((seq_len, D_B), lambda b: (b, 0)),
        out_shape=jax.ShapeDtypeStruct((t, D_B), F32),
        compiler_params=_cparams("arbitrary"),
        name="hyena_mixer",
    )(u, short_w, short_b.reshape(1, width), spectrum, fwd, inv)


RG_CB = LANES
RG_CHUNK = 512
TM_ROWS = 256


def _rglru_kernel(nseq, seq_len, gate_ref, xr_ref, cw_ref, cb_ref, wg_ref, bg_ref, lam_ref, h0_ref,
                  y_ref, fin_ref, xp_ref, a_f, b_f, a_b, b_b):
    t_tot = nseq * seq_len
    c = RG_CB
    pad = 2 * nseq
    xp_ref[0:pad, :] = jnp.zeros((pad, c), F32)
    xp_ref[pad + t_tot:, :] = jnp.zeros((pad, c), F32)
    xp_ref[pad:pad + t_tot, :] = xr_ref[...]
    nl = -lam_ref[...]
    sp = jnp.maximum(nl, 0.0) + jnp.log1p(jnp.exp(-jnp.abs(nl)))

    def gate_chunk(ci, carry):
        r0 = pl.multiple_of(ci * RG_CHUNK, RG_CHUNK)
        xc = xp_ref[pl.ds(r0, RG_CHUNK), :] * cw_ref[0:1, :]
        for j in range(1, cw_ref.shape[0]):
            xc = xc + xp_ref[pl.ds(r0 + j * nseq, RG_CHUNK), :] * cw_ref[j:j + 1, :]
        xc = xc + cb_ref[...]
        gts = _dot(xc.astype(BF16), wg_ref[0]) + bg_ref[...]
        for d, (a_ref, b_ref) in enumerate(((a_f, b_f), (a_b, b_b))):
            r = _sigmoid(gts[:, (2 * d) * c:(2 * d + 1) * c])
            i = _sigmoid(gts[:, (2 * d + 1) * c:(2 * d + 2) * c])
            a = jnp.exp(-RG_C * r * sp[d:d + 1, :])
            a_ref[pl.ds(r0, RG_CHUNK), :] = a
            b_ref[pl.ds(r0, RG_CHUNK), :] = jnp.sqrt(jnp.maximum(1.0 - a * a, 0.0)) * (i * xc)
        return carry

    lax.fori_loop(0, t_tot // RG_CHUNK, gate_chunk, 0)

    def scan_step(t, carry):
        hf, hb = carry
        rows_f = pl.ds(pl.multiple_of(t * nseq, nseq), nseq)
        rows_b = pl.ds(pl.multiple_of((seq_len - 1 - t) * nseq, nseq), nseq)
        hf = a_f[rows_f, :] * hf + b_f[rows_f, :]
        hb = a_b[rows_b, :] * hb + b_b[rows_b, :]
        b_f[rows_f, :] = hf
        b_b[rows_b, :] = hb
        return hf, hb

    hf, hb = lax.fori_loop(0, seq_len, scan_step, (h0_ref[0], h0_ref[1]), unroll=8)
    fin_ref[0] = hf
    fin_ref[1] = hb

    def out_chunk(ci, carry):
        rs = pl.ds(pl.multiple_of(ci * RG_CHUNK, RG_CHUNK), RG_CHUNK)
        y_ref[rs, :] = (b_f[rs, :] + b_b[rs, :]) * jax.nn.gelu(gate_ref[rs, :])
        return carry

    lax.fori_loop(0, t_tot // RG_CHUNK, out_chunk, 0)


def _rg_gate_weights(wa, wx):
    per_step = RG_CB // RG_BLOCK
    steps = D_RNN // RG_CB
    mats = []
    for d in range(2):
        for w in (wa[d], wx[d]):
            w = w.reshape(steps, per_step, RG_BLOCK, RG_BLOCK)
            eye = jnp.eye(per_step, dtype=w.dtype)
            m = jnp.einsum('spde,pq->spdqe', w, eye).reshape(steps, RG_CB, RG_CB)
            mats.append(m)
    return jnp.concatenate(mats, axis=-1).astype(BF16)


def rglru_block(u, nseq, seq_len, conv_w, conv_b, wa, ba, wx, bx, lam, h0):
    t = u.shape[0]
    c = RG_CB
    steps = D_RNN // c
    wg = _rg_gate_weights(wa, wx)
    bg = jnp.stack([ba[0], bx[0], ba[1], bx[1]], axis=0).reshape(4, steps, c)
    bg = bg.transpose(1, 0, 2).reshape(steps, 1, 4 * c)
    y, fin = pl.pallas_call(
        functools.partial(_rglru_kernel, nseq, seq_len),
        grid=(steps,),
        in_specs=[pl.BlockSpec((t, c), lambda j: (0, j)),
                  pl.BlockSpec((t, c), lambda j: (0, steps + j)),
                  pl.BlockSpec((conv_w.shape[0], c), lambda j: (0, j)),
                  pl.BlockSpec((1, c), lambda j: (0, j)),
                  pl.BlockSpec((1, c, 4 * c), lambda j: (j, 0, 0)),
                  pl.BlockSpec((None, 1, 4 * c), lambda j: (j, 0, 0)),
                  pl.BlockSpec((2, c), lambda j: (0, j)),
                  pl.BlockSpec((2, nseq, c), lambda j: (0, 0, j))],
        out_specs=[pl.BlockSpec((t, c), lambda j: (0, j)),
                   pl.BlockSpec((2, nseq, c), lambda j: (0, 0, j))],
        out_shape=[jax.ShapeDtypeStruct((t, D_RNN), F32),
                   jax.ShapeDtypeStruct((2, nseq, D_RNN), F32)],
        scratch_shapes=[pltpu.VMEM((t + 4 * nseq, c), F32)] + [pltpu.VMEM((t, c), F32)] * 4,
        compiler_params=_cparams("arbitrary"),
        name="rglru_block",
    )(u, u, conv_w, conv_b.reshape(1, -1), wg, bg, lam, h0)
    return y, fin


def _router_kernel(x_ref, g_ref, sc_ref, sh_ref, rw_ref, rb_ref, o_ref):
    h = _normmod(x_ref[...], g_ref[...], sc_ref[0], sh_ref[0])
    h_hi = h.astype(BF16)
    h_lo = (h - h_hi.astype(F32)).astype(BF16)
    w = rw_ref[...]
    w_hi = w.astype(BF16)
    w_lo = (w - w_hi.astype(F32)).astype(BF16)
    logits = _dot_nt(w_hi, h_hi) + (_dot_nt(w_lo, h_hi) + _dot_nt(w_hi, h_lo))
    scores = jax.nn.sigmoid(logits)
    sel = scores + rb_ref[...]
    row = [sel[e:e + 1, :] for e in range(N_EXPERTS)]
    gs = []
    for g in range(N_GROUPS):
        r = row[g * EXPERTS_PER_GROUP:(g + 1) * EXPERTS_PER_GROUP]
        best_pair = None
        for i in range(EXPERTS_PER_GROUP):
            for j in range(i + 1, EXPERTS_PER_GROUP):
                s = r[i] + r[j]
                best_pair = s if best_pair is None else jnp.maximum(best_pair, s)
        gs.append(best_pair)
    best = jnp.zeros_like(gs[0], dtype=jnp.int32)
    top = gs[0]
    for g in range(1, N_GROUPS):
        better = gs[g] > top
        best = jnp.where(better, g, best)
        top = jnp.where(better, gs[g], top)
    picked = []
    for e in range(N_EXPERTS):
        g = e // EXPERTS_PER_GROUP
        rank = jnp.zeros_like(best)
        for o in range(g * EXPERTS_PER_GROUP, (g + 1) * EXPERTS_PER_GROUP):
            if o == e:
                continue
            ahead = (row[o] > row[e]) | ((row[o] == row[e]) & (o < e))
            rank = rank + ahead.astype(jnp.int32)
        picked.append((best == g) & (rank < 2))
    den = jnp.zeros_like(gs[0])
    for e in range(N_EXPERTS):
        den = den + jnp.where(picked[e], scores[e:e + 1, :], 0.0)
    gate = [jnp.where(picked[e], scores[e:e + 1, :] / den, 0.0) for e in range(N_EXPERTS)]
    cls = jnp.zeros_like(den)
    w_a = jnp.zeros_like(den)
    w_b = jnp.zeros_like(den)
    for g in range(N_GROUPS):
        for pi, (a, b) in enumerate(MOE_PAIRS):
            ea, eb = g * EXPERTS_PER_GROUP + a, g * EXPERTS_PER_GROUP + b
            both = picked[ea] & picked[eb]
            cls = jnp.where(both, float(g * len(MOE_PAIRS) + pi), cls)
            w_a = jnp.where(both, gate[ea], w_a)
            w_b = jnp.where(both, gate[eb], w_b)
    o_ref[...] = jnp.concatenate([cls, w_a, w_b, jnp.zeros((ROUTE_ROWS - 3, cls.shape[1]), F32)], axis=0)


MOE_PAIRS = ((0, 1), (0, 2), (0, 3), (1, 3), (1, 2), (2, 3))
N_CLS = N_GROUPS * len(MOE_PAIRS)
CLS_PAD = 32
ROUTE_ROWS = 8
MOE_TS = 256
MOE_TM = 256
SLOT_BLK = 512
ROW_W = D_MODEL + LANES
N_DMA_QUEUES = 2


def router_gates(x, g, mod, router_w, router_b, seq_len, per_seq, tm=512):
    t, d = x.shape
    return pl.pallas_call(
        _router_kernel,
        grid=(t // tm,),
        in_specs=[pl.BlockSpec((tm, d), lambda i: (i, 0)),
                  pl.BlockSpec((1, d), lambda i: (0, 0)),
                  _mod_spec(4, tm, seq_len, per_seq),
                  _mod_spec(3, tm, seq_len, per_seq),
                  pl.BlockSpec((N_EXPERTS, d), lambda i: (0, 0)),
                  pl.BlockSpec((N_EXPERTS, 1), lambda i: (0, 0))],
        out_specs=pl.BlockSpec((ROUTE_ROWS, tm), lambda i: (0, i)),
        out_shape=jax.ShapeDtypeStruct((ROUTE_ROWS, t), F32),
        compiler_params=_cparams("arbitrary"),
        name="moe_router",
    )(x, g.reshape(1, d), mod, mod, router_w.T, router_b.reshape(N_EXPERTS, 1))


def _slots_kernel(n_blk, route_ref, slot_ref, off_ref, cnt_ref):
    cid = lax.broadcasted_iota(jnp.int32, (CLS_PAD, SLOT_BLK), 0).astype(F32)

    def members(j):
        cls = route_ref[0:1, pl.ds(pl.multiple_of(j * SLOT_BLK, SLOT_BLK), SLOT_BLK)]
        return (cid == cls).astype(F32)

    def count(j, cnt):
        return cnt + jnp.sum(members(j), axis=1, keepdims=True)

    cnt = lax.fori_loop(0, n_blk, count, jnp.zeros((CLS_PAD, 1), F32))
    cnt = jnp.broadcast_to(cnt, (CLS_PAD, LANES))
    padded = jnp.ceil(cnt * (1.0 / MOE_TS)) * MOE_TS
    r = lax.broadcasted_iota(jnp.int32, (CLS_PAD, CLS_PAD), 0)
    c = lax.broadcasted_iota(jnp.int32, (CLS_PAD, CLS_PAD), 1)
    off = jnp.dot((c < r).astype(F32), padded, precision=lax.Precision.HIGHEST, preferred_element_type=F32)
    off_ref[...] = off
    cnt_ref[...] = cnt
    tr = lax.broadcasted_iota(jnp.int32, (SLOT_BLK, SLOT_BLK), 0)
    tc = lax.broadcasted_iota(jnp.int32, (SLOT_BLK, SLOT_BLK), 1)
    earlier = (tr < tc).astype(BF16)

    def assign(j, base):
        member = members(j)
        rank = _dot(member.astype(BF16), earlier)
        slot = jnp.sum(member * (rank + base), axis=0, keepdims=True)
        slot_ref[0:1, pl.ds(pl.multiple_of(j * SLOT_BLK, SLOT_BLK), SLOT_BLK)] = slot.astype(jnp.int32)
        return base + jnp.sum(member, axis=1, keepdims=True)

    lax.fori_loop(0, n_blk, assign, off[:, 0:1])


def moe_slots(route):
    t = route.shape[1]
    stat = jax.ShapeDtypeStruct((CLS_PAD, LANES), F32)
    return pl.pallas_call(
        functools.partial(_slots_kernel, t // SLOT_BLK),
        out_shape=[jax.ShapeDtypeStruct((1, t), jnp.int32), stat, stat],
        compiler_params=pltpu.CompilerParams(vmem_limit_bytes=VMEM_LIMIT),
        name="moe_slots",
    )(route)


def _tile_maps(off, cnt, n_tiles):
    off = off[:N_CLS, 0].astype(jnp.int32)
    cnt = cnt[:N_CLS, 0].astype(jnp.int32)
    ends = off + ((cnt + MOE_TS - 1) // MOE_TS) * MOE_TS
    n_used = ends[-1] // MOE_TS
    k = jnp.arange(n_tiles, dtype=jnp.int32)
    tix = jnp.minimum(k, n_used - 1)
    cls = jnp.sum((tix[:, None] * MOE_TS >= ends[None, :]).astype(jnp.int32), axis=1)
    pair = jnp.asarray(MOE_PAIRS, jnp.int32)
    grp = (cls // len(MOE_PAIRS)) * EXPERTS_PER_GROUP
    ea = grp + pair[cls % len(MOE_PAIRS), 0]
    eb = grp + pair[cls % len(MOE_PAIRS), 1]
    n = jnp.int32(n_tiles)

    def slot_plan(e):
        chg = jnp.concatenate([jnp.ones((1,), jnp.int32), (e[1:] != e[:-1]).astype(jnp.int32)])
        at = jnp.where(chg == 1, k, n)
        nxt_at = jnp.concatenate([lax.cummin(at[::-1])[::-1][1:], n.reshape(1)])
        more = (nxt_at < n).astype(jnp.int32)
        nxt = e[jnp.minimum(nxt_at, n - 1)]
        par = (jnp.cumsum(chg) - 1) % 2
        return chg, nxt, more, par.astype(jnp.int32)

    plan_a, plan_b = slot_plan(ea), slot_plan(eb)
    chg, nxt, more, par = (jnp.stack([pa, pb]) for pa, pb in zip(plan_a, plan_b))
    return ea, eb, tix, chg, nxt, more, par, n_used.reshape(1)


def _dispatch_kernel(n_steps, slots_ref, x_ref, g_ref, sc_ref, sh_ref, rt_ref, hs_in, hs_out, rowbuf, sem):
    del hs_in
    i = pl.program_id(0)
    cur = i % 2

    def wait_rows(s):
        pltpu.make_async_copy(rowbuf.at[s], rowbuf.at[s], sem.at[s]).wait()

    @pl.when(i >= 2)
    def _():
        wait_rows(cur)

    rowbuf[cur, :, 0:D_MODEL] = _normmod(x_ref[...], g_ref[...], sc_ref[0], sh_ref[0])
    rowbuf[cur, :, D_MODEL:ROW_W] = jnp.concatenate(
        [rt_ref[...], jnp.zeros((MOE_TM, LANES - ROUTE_ROWS), F32)], axis=1)

    for s in range(2):
        @pl.when(cur == s)
        def _():
            for r in range(MOE_TM):
                dst = slots_ref[i * MOE_TM + r]
                pltpu.make_async_copy(rowbuf.at[s, r], hs_out.at[dst], sem.at[s]).start(
                    priority=r % N_DMA_QUEUES)

    @pl.when(i == n_steps - 1)
    def _():
        wait_rows(cur)
        if n_steps >= 2:
            wait_rows(1 - cur)


def moe_dispatch(x, g, mod, route_t, slots, hs, seq_len, per_seq):
    t, d = x.shape
    n_steps = t // MOE_TM
    grid_spec = pltpu.PrefetchScalarGridSpec(
        num_scalar_prefetch=1,
        grid=(n_steps,),
        in_specs=[pl.BlockSpec((MOE_TM, d), lambda i, s: (i, 0)),
                  pl.BlockSpec((1, d), lambda i, s: (0, 0)),
                  _mod_spec(4, MOE_TM, seq_len, per_seq),
                  _mod_spec(3, MOE_TM, seq_len, per_seq),
                  pl.BlockSpec((MOE_TM, ROUTE_ROWS), lambda i, s: (i, 0)),
                  pl.BlockSpec(memory_space=pl.ANY)],
        out_specs=pl.BlockSpec(memory_space=pl.ANY),
        scratch_shapes=[pltpu.VMEM((2, MOE_TM, ROW_W), F32), pltpu.SemaphoreType.DMA((2,))],
    )
    return pl.pallas_call(
        functools.partial(_dispatch_kernel, n_steps),
        grid_spec=grid_spec,
        out_shape=jax.ShapeDtypeStruct(hs.shape, F32),
        input_output_aliases={6: 0},
        compiler_params=_cparams("arbitrary"),
        name="moe_dispatch",
    )(slots, x, g.reshape(1, d), mod, mod, route_t, hs)


def _experts_kernel(layer, ea_ref, eb_ref, tix_ref, chg_ref, nxt_ref, more_ref, par_ref, nused_ref,
                    hs_ref, wg_hbm, wu_hbm, wd_hbm, ys_ref, fg, fu, fd, bg, bu, bd, sem):
    k = pl.program_id(0)

    def weight_copies(slot, expert, par):
        return [pltpu.make_async_copy(src.at[layer, expert], dst.at[slot, par], sem.at[slot, par])
                for src, dst in ((wg_hbm, fg), (wu_hbm, fu), (wd_hbm, fd))]

    @pl.when(k < nused_ref[0])
    def _():
        for slot, e_ref in enumerate((ea_ref, eb_ref)):
            @pl.when(chg_ref[slot, k] == 1)
            def _():
                par = par_ref[slot, k]

                @pl.when(k == 0)
                def _():
                    for cp in weight_copies(slot, e_ref[0], par):
                        cp.start()

                for cp in weight_copies(slot, e_ref[k], par):
                    cp.wait()
                bg[slot] = fg[slot, par].astype(BF16)
                bu[slot] = fu[slot, par].astype(BF16)
                bd[slot] = fd[slot, par].astype(BF16)

                @pl.when(more_ref[slot, k] == 1)
                def _():
                    for cp in weight_copies(slot, nxt_ref[slot, k], 1 - par):
                        cp.start()

        h = hs_ref[:, 0:D_MODEL].astype(BF16)

        def ffn(slot):
            hid = _dot(h, bg[slot])
            up = _dot(h, bu[slot])
            w = hs_ref[:, D_MODEL + 1 + slot:D_MODEL + 2 + slot]
            act = (hid * _sigmoid(hid)) * up * w
            return _dot(act.astype(BF16), bd[slot])

        ys_ref[...] = ffn(0) + ffn(1)

    @pl.when(k >= nused_ref[0])
    def _():
        ys_ref[...] = jnp.zeros_like(ys_ref)


def moe_experts(hs, maps, layer, w_gate, w_up, w_down):
    n_tiles = hs.shape[0] // MOE_TS
    d = D_MODEL

    grid_spec = pltpu.PrefetchScalarGridSpec(
        num_scalar_prefetch=8,
        grid=(n_tiles,),
        in_specs=[pl.BlockSpec((MOE_TS, ROW_W), lambda k, ea, eb, tix, *_: (tix[k], 0)),
                  pl.BlockSpec(memory_space=pl.ANY), pl.BlockSpec(memory_space=pl.ANY),
                  pl.BlockSpec(memory_space=pl.ANY)],
        out_specs=pl.BlockSpec((MOE_TS, d), lambda k, *_: (k, 0)),
        scratch_shapes=[pltpu.VMEM((2, 2, d, D_EXPERT), F32), pltpu.VMEM((2, 2, d, D_EXPERT), F32),
                        pltpu.VMEM((2, 2, D_EXPERT, d), F32),
                        pltpu.VMEM((2, d, D_EXPERT), BF16), pltpu.VMEM((2, d, D_EXPERT), BF16),
                        pltpu.VMEM((2, D_EXPERT, d), BF16),
                        pltpu.SemaphoreType.DMA((2, 2))],
    )
    return pl.pallas_call(
        functools.partial(_experts_kernel, layer),
        grid_spec=grid_spec,
        out_shape=jax.ShapeDtypeStruct((hs.shape[0], d), F32),
        compiler_params=_cparams("arbitrary"),
        name="moe_experts",
    )(*maps, hs, w_gate, w_up, w_down)


def _combine_kernel(final, n_steps, slots_ref, x_ref, g2_ref, fg_ref, ys_hbm, o_ref, gbuf, sem):
    i = pl.program_id(0)
    cur = i % 2

    def issue_tile(tile, s):
        for r in range(MOE_TM):
            src = slots_ref[tile * MOE_TM + r]
            pltpu.make_async_copy(ys_hbm.at[src], gbuf.at[s, r], sem.at[s]).start(priority=r % N_DMA_QUEUES)

    @pl.when(i == 0)
    def _():
        issue_tile(0, 0)

    for s in range(2):
        @pl.when((i + 1 < n_steps) & (1 - cur == s))
        def _():
            issue_tile(i + 1, s)

    pltpu.make_async_copy(gbuf.at[cur], gbuf.at[cur], sem.at[cur]).wait()
    y = x_ref[...] + g2_ref[0] * gbuf[cur]
    if final:
        ms = jnp.mean(y * y, axis=-1, keepdims=True)
        y = y * lax.rsqrt(ms + EPS) * fg_ref[...]
    o_ref[...] = y


def moe_combine(x, mod, slots, ys, final_g, final, seq_len, per_seq):
    t, d = x.shape
    n_steps = t // MOE_TM
    grid_spec = pltpu.PrefetchScalarGridSpec(
        num_scalar_prefetch=1,
        grid=(n_steps,),
        in_specs=[pl.BlockSpec((MOE_TM, d), lambda i, s: (i, 0)),
                  _mod_spec(5, MOE_TM, seq_len, per_seq),
                  pl.BlockSpec((1, d), lambda i, s: (0, 0)),
                  pl.BlockSpec(memory_space=pl.ANY)],
        out_specs=pl.BlockSpec((MOE_TM, d), lambda i, s: (i, 0)),
        scratch_shapes=[pltpu.VMEM((2, MOE_TM, d), F32), pltpu.SemaphoreType.DMA((2,))],
    )
    return pl.pallas_call(
        functools.partial(_combine_kernel, final, n_steps),
        grid_spec=grid_spec,
        out_shape=jax.ShapeDtypeStruct((t, d), F32),
        compiler_params=_cparams("arbitrary"),
        name="moe_combine",
    )(slots, x, mod, final_g.reshape(1, d), ys)


def _combine_nm_tm_kernel(nseq, steps, seq_len, n_steps, slots_ref, x_ref, g2_ref, ys_hbm, g_ref, sc_ref,
                          sh_ref, w_ref, x2_ref, u_ref, gbuf, sem):
    i = pl.program_id(0)
    cur = i % 2

    def issue_tile(tile, s):
        for sq in range(nseq):
            for t in range(steps):
                src = slots_ref[sq * seq_len + tile * steps + t]
                pltpu.make_async_copy(ys_hbm.at[src], gbuf.at[s, sq * steps + t], sem.at[s]).start(
                    priority=t % N_DMA_QUEUES)

    @pl.when(i == 0)
    def _():
        issue_tile(0, 0)

    for s in range(2):
        @pl.when((i + 1 < n_steps) & (1 - cur == s))
        def _():
            issue_tile(i + 1, s)

    pltpu.make_async_copy(gbuf.at[cur], gbuf.at[cur], sem.at[cur]).wait()
    d = x_ref.shape[-1]
    y = x_ref[...] + g2_ref[...] * gbuf[cur].reshape(nseq, steps, d)
    x2_ref[...] = y
    h = _normmod(y, g_ref[...], sc_ref[...], sh_ref[...])
    h = h.reshape(nseq * steps, d).astype(BF16)
    h = _dot(_row_permutation(nseq, steps, True), h).astype(BF16)
    u_ref[...] = _dot(h, w_ref[...])


def combine_nm_matmul_tm(x, mod_prev, slots, ys, g, mod, w, nseq, seq_len, per_seq):
    t, d = x.shape
    n = w.shape[1]
    steps = TM_ROWS // nseq
    n_steps = seq_len // steps
    g2 = _group_mod(mod_prev, 5, nseq, per_seq)
    sc = _group_mod(mod, 1, nseq, per_seq)
    sh = _group_mod(mod, 0, nseq, per_seq)
    mod_spec = pl.BlockSpec(sc.shape, lambda i, s: (0, 0, 0))
    x_spec = pl.BlockSpec((nseq, steps, d), lambda i, s: (0, i, 0))
    grid_spec = pltpu.PrefetchScalarGridSpec(
        num_scalar_prefetch=1,
        grid=(n_steps,),
        in_specs=[x_spec, mod_spec, pl.BlockSpec(memory_space=pl.ANY),
                  pl.BlockSpec((1, d), lambda i, s: (0, 0)), mod_spec, mod_spec,
                  pl.BlockSpec((d, n), lambda i, s: (0, 0))],
        out_specs=[x_spec, pl.BlockSpec((TM_ROWS, n), lambda i, s: (i, 0))],
        scratch_shapes=[pltpu.VMEM((2, TM_ROWS, d), F32), pltpu.SemaphoreType.DMA((2,))],
    )
    x2, u = pl.pallas_call(
        functools.partial(_combine_nm_tm_kernel, nseq, steps, seq_len, n_steps),
        grid_spec=grid_spec,
        out_shape=[jax.ShapeDtypeStruct((nseq, seq_len, d), F32), jax.ShapeDtypeStruct((t, n), F32)],
        compiler_params=_cparams("arbitrary"),
        name="moe_combine_norm_mod_proj_tm",
    )(slots, x.reshape(nseq, seq_len, d), g2, ys, g.reshape(1, d), sc, sh, w)
    return x2.reshape(t, d), u


def moe_block(xs, mods_l, per_seqs, seq_lens, layer, p, final, defer_combine):
    g = p['norm_g'][layer, 1]
    routes = [router_gates(x, g, mods_l, p['router_w'], p['router_b'], sl, ps)
              for x, sl, ps in zip(xs, seq_lens, per_seqs)]
    slots, off, cnt = moe_slots(jnp.concatenate(routes, axis=1))
    t_all = slots.shape[1]
    n_tiles = t_all // MOE_TS + N_CLS
    maps = _tile_maps(off, cnt, n_tiles)
    hs = jnp.zeros((n_tiles * MOE_TS, ROW_W), F32)
    bounds = np.cumsum([0] + [x.shape[0] for x in xs])
    group_slots = [slots[0, bounds[i]:bounds[i + 1]] for i in range(len(xs))]
    for x, r, s, sl, ps in zip(xs, routes, group_slots, seq_lens, per_seqs):
        hs = moe_dispatch(x, g, mods_l, r.T, s, hs, sl, ps)
    ys = moe_experts(hs, maps, layer, p['moe_w_gate'], p['moe_w_up'], p['moe_w_down'])
    if defer_combine:
        return [(x, s, ys) for x, s in zip(xs, group_slots)]
    return [moe_combine(x, mods_l, s, ys, p['final_g'], final, sl, ps)
            for x, s, sl, ps in zip(xs, group_slots, seq_lens, per_seqs)]


def _mixer(x, group, l, mod, p, hy_tables, mod_prev):
    per_seq, nseq, seq_len = group['per_seq'], group['nseq'], group['seq_len']
    extras = None
    deferred = isinstance(x, tuple)
    if deferred and l % 2 == 0:
        x = moe_combine(x[0], mod_prev, x[1], x[2], p['final_g'], False, seq_len, per_seq)
        deferred = False
    if l % 2 == 0:
        e = l // 2
        u = nm_matmul(x, p['norm_g'][l, 0], mod, p['a_in_w'][e].astype(BF16), seq_len, per_seq)
        if group['ctx_k'] is None:
            attn, nk, nv = context_attention(u, nseq, seq_len)
            extras = (nk, nv)
        else:
            attn = neighbourhood_attention(u, group['ctx_k'][:, e], group['ctx_v'][:, e], p['na_rpb'][e],
                                           nseq, seq_len)
        fwd, inv = hy_tables[seq_len]
        spectrum = hyena_spectrum(seq_len, p['hy_w1'][e], p['hy_b1'][e], p['hy_w2'][e], p['hy_b2'][e],
                                  p['hy_w3'][e], p['hy_freq'][e], p['hy_d'][e], fwd)
        hy = hyena_mixer(u, nseq, seq_len, p['hy_short_w'][e], p['hy_short_b'][e], spectrum, fwd, inv)
        w_out = p['a_out_w'][e].astype(BF16)
        x = proj_residual([attn, hy], [w_out[:D_A], w_out[D_A:]], x, mod, seq_len, per_seq)
    else:
        o = l // 2
        w_in = p['c_in_w'][o].astype(BF16)
        if deferred:
            x, u = combine_nm_matmul_tm(x[0], mod_prev, x[1], x[2], p['norm_g'][l, 0], mod, w_in,
                                        nseq, seq_len, per_seq)
        else:
            u = nm_matmul_tm(x, p['norm_g'][l, 0], mod, w_in, nseq, seq_len, per_seq)
        y, extras = rglru_block(u, nseq, seq_len, p['rg_conv_w'][o], p['rg_conv_b'][o], p['rg_wa'][o],
                                p['rg_ba'][o], p['rg_wx'][o], p['rg_bx'][o], p['rg_lam'][o], group['h0'][o])
        x = proj_residual_tm(y, p['c_out_w'][o].astype(BF16), x, mod, nseq, seq_len, per_seq)
    return x, extras


def kernel(x_prompt, x_sample, cache_k, cache_v, state_h, c, c_ctx, norm_g, ada_w, ada_b, final_g, a_in_w, a_out_w, na_rpb, hy_short_w, hy_short_b, hy_w1, hy_b1, hy_w2, hy_b2, hy_w3, hy_freq, hy_d, c_in_w, c_out_w, rg_conv_w, rg_conv_b, rg_wa, rg_ba, rg_wx, rg_bx, rg_lam, router_w, router_b, moe_w_gate, moe_w_up, moe_w_down):
    p = dict(norm_g=norm_g, final_g=final_g, a_in_w=a_in_w, a_out_w=a_out_w, na_rpb=na_rpb,
             hy_short_w=hy_short_w, hy_short_b=hy_short_b, hy_w1=hy_w1, hy_b1=hy_b1, hy_w2=hy_w2,
             hy_b2=hy_b2, hy_w3=hy_w3, hy_freq=hy_freq, hy_d=hy_d, c_in_w=c_in_w, c_out_w=c_out_w,
             rg_conv_w=rg_conv_w, rg_conv_b=rg_conv_b, rg_wa=rg_wa, rg_ba=rg_ba, rg_wx=rg_wx, rg_bx=rg_bx,
             rg_lam=rg_lam, router_w=router_w, router_b=router_b, moe_w_gate=moe_w_gate,
             moe_w_up=moe_w_up, moe_w_down=moe_w_down)
    batch, seq, d = x_prompt.shape
    dec_batch, dec_seq, _ = x_sample.shape
    n_odd = DEPTH // 2
    assert 1 + dec_batch <= MOD_ROWS

    cond = jnp.concatenate([c_ctx[None, :], c, jnp.zeros((MOD_ROWS - 1 - dec_batch, d), F32)], axis=0)
    m = modulation(cond, ada_w, ada_b)
    mods = [m[l].reshape(MOD_ROWS * N_MOD, 1, d) for l in range(DEPTH)]

    tables = {}
    for sl in (seq, dec_seq):
        fwd, inv = _dft_tables(sl)
        tables[sl] = (jnp.asarray(fwd).astype(BF16), jnp.asarray(inv).astype(BF16))

    groups = [
        dict(per_seq=False, nseq=batch, seq_len=seq, ctx_k=None, ctx_v=None,
             h0=[jnp.zeros((2, batch, D_RNN), F32)] * n_odd),
        dict(per_seq=True, nseq=dec_batch, seq_len=dec_seq, ctx_k=cache_k, ctx_v=cache_v,
             h0=[state_h[:, o].transpose(1, 0, 2) for o in range(n_odd)]),
    ]
    xs = [x_prompt.reshape(batch * seq, d), x_sample.reshape(dec_batch * dec_seq, d)]
    k_list, v_list, h_list = [], [], []
    for l in range(DEPTH):
        mixed = [_mixer(x, grp, l, mods[l], p, tables, mods[l - 1] if l else None)
                 for x, grp in zip(xs, groups)]
        if l % 2 == 0:
            k_list.append(mixed[0][1][0])
            v_list.append(mixed[0][1][1])
        else:
            h_list.append(mixed[0][1].transpose(1, 0, 2))
        xs = moe_block([mx[0] for mx in mixed], mods[l], [grp['per_seq'] for grp in groups],
                       [grp['seq_len'] for grp in groups], l, p, l == DEPTH - 1, l < DEPTH - 1)
    new_k = jnp.stack(k_list, axis=1)
    new_v = jnp.stack(v_list, axis=1)
    new_h = jnp.stack(h_list, axis=1)
    return (xs[0].reshape(batch, seq, d), xs[1].reshape(dec_batch, dec_seq, d), new_k, new_v, new_h)
```

```python
import functools
import math

import numpy as np
import jax
import jax.numpy as jnp
from jax import lax
from jax.experimental import pallas as pl
from jax.experimental.pallas import tpu as pltpu

F32 = jnp.float32
BF16 = jnp.bfloat16

D_MODEL = 1024
DEPTH = 2
GRID_W = 64
EPS = 1e-6
NEG_INF = -1e30
NA_HEADS = 8
HEAD_DIM = 64
D_A = NA_HEADS * HEAD_DIM
WIN_ROWS = 8
WIN_COLS = 16
D_B = D_MODEL - D_A
HY_ORDER = 2
HY_EMB = 33
HY_BANDS = (HY_EMB - 1) // 2
HY_FFN = 64
HY_DECAY_TARGET = 1e-2
HY_FAST_PCT = 0.3
HY_SLOW_PCT = 1.5
D_RNN = D_MODEL
RG_BLOCK = 64
RG_C = 8.0
N_EXPERTS = 16
N_GROUPS = 4
EXPERTS_PER_GROUP = N_EXPERTS // N_GROUPS
D_EXPERT = 512

LANES = 128
VMEM_LIMIT = 56 * 1024 * 1024
N_MOD = 6
MOD_ROWS = 16


def _cparams(*sem):
    return pltpu.CompilerParams(dimension_semantics=sem, vmem_limit_bytes=VMEM_LIMIT)


def _dot(a, b):
    return jnp.dot(a, b, preferred_element_type=F32)


def _dot_nt(a, b):
    return lax.dot_general(a, b, (((1,), (1,)), ((), ())), preferred_element_type=F32)


def _sigmoid(x):
    return 0.5 * jnp.tanh(0.5 * x) + 0.5


def _normmod(x, g, sc, sh):
    ms = jnp.mean(x * x, axis=-1, keepdims=True)
    return (x * lax.rsqrt(ms + EPS) * g) * (1.0 + sc) + sh


def _mod_spec(chunk, tm, seq_len, per_seq):
    if per_seq:
        return pl.BlockSpec((1, 1, D_MODEL), lambda i, *_: ((1 + (i * tm) // seq_len) * N_MOD + chunk, 0, 0))
    return pl.BlockSpec((1, 1, D_MODEL), lambda i, *_: (chunk, 0, 0))


def _mod_kernel(c_ref, w_ref, b_ref, o_ref):
    s = c_ref[...]
    s = s * jax.nn.sigmoid(s)
    o_ref[0] = _dot(s.astype(BF16), w_ref[0].astype(BF16)) + b_ref[0]


def modulation(cond, ada_w, ada_b):
    tn = 1536
    n = ada_w.shape[-1]
    return pl.pallas_call(
        _mod_kernel,
        grid=(DEPTH, n // tn),
        in_specs=[pl.BlockSpec((MOD_ROWS, D_MODEL), lambda l, j: (0, 0)),
                  pl.BlockSpec((1, D_MODEL, tn), lambda l, j: (l, 0, j)),
                  pl.BlockSpec((1, 1, tn), lambda l, j: (l, 0, j))],
        out_specs=pl.BlockSpec((1, MOD_ROWS, tn), lambda l, j: (l, 0, j)),
        out_shape=jax.ShapeDtypeStruct((DEPTH, MOD_ROWS, n), F32),
        compiler_params=_cparams("arbitrary", "arbitrary"),
        name="modulation",
    )(cond, ada_w, ada_b.reshape(DEPTH, 1, n))


def _nm_matmul_kernel(x_ref, g_ref, sc_ref, sh_ref, w_ref, o_ref):
    h = _normmod(x_ref[...], g_ref[...], sc_ref[0], sh_ref[0])
    o_ref[...] = _dot(h.astype(BF16), w_ref[...])


def nm_matmul(x, g, mod, w, seq_len, per_seq, tm=512):
    t, d = x.shape
    n = w.shape[1]
    return pl.pallas_call(
        _nm_matmul_kernel,
        grid=(t // tm,),
        in_specs=[pl.BlockSpec((tm, d), lambda i: (i, 0)),
                  pl.BlockSpec((1, d), lambda i: (0, 0)),
                  _mod_spec(1, tm, seq_len, per_seq),
                  _mod_spec(0, tm, seq_len, per_seq),
                  pl.BlockSpec((d, n), lambda i: (0, 0))],
        out_specs=pl.BlockSpec((tm, n), lambda i: (i, 0)),
        out_shape=jax.ShapeDtypeStruct((t, n), F32),
        compiler_params=_cparams("arbitrary"),
        name="norm_mod_proj",
    )(x, g.reshape(1, d), mod, mod, w)


def _proj_res_kernel(n_act, *refs):
    acts = refs[:n_act]
    ws = refs[n_act:2 * n_act]
    x_ref, g_ref, o_ref = refs[2 * n_act:]
    acc = _dot(acts[0][...].astype(BF16), ws[0][...])
    for a, w in zip(acts[1:], ws[1:]):
        acc += _dot(a[...].astype(BF16), w[...])
    o_ref[...] = x_ref[...] + g_ref[0] * acc


def proj_residual(acts, ws, x, mod, seq_len, per_seq, tm=512):
    t, d = x.shape
    in_specs = [pl.BlockSpec((tm, a.shape[1]), lambda i: (i, 0)) for a in acts]
    in_specs += [pl.BlockSpec(w.shape, lambda i: (0, 0)) for w in ws]
    in_specs += [pl.BlockSpec((tm, d), lambda i: (i, 0)), _mod_spec(2, tm, seq_len, per_seq)]
    return pl.pallas_call(
        functools.partial(_proj_res_kernel, len(acts)),
        grid=(t // tm,),
        in_specs=in_specs,
        out_specs=pl.BlockSpec((tm, d), lambda i: (i, 0)),
        out_shape=jax.ShapeDtypeStruct((t, d), F32),
        compiler_params=_cparams("arbitrary"),
        name="proj_residual",
    )(*acts, *ws, x, mod)


def _row_permutation(nseq, steps, to_time_major):
    n = nseq * steps
    i = lax.broadcasted_iota(jnp.int32, (n, n), 0)
    j = lax.broadcasted_iota(jnp.int32, (n, n), 1)
    if to_time_major:
        src = (i % nseq) * steps + i // nseq
    else:
        src = (i % steps) * nseq + i // steps
    return (j == src).astype(BF16)


def _nm_matmul_tm_kernel(nseq, steps, x_ref, g_ref, sc_ref, sh_ref, w_ref, o_ref):
    h = _normmod(x_ref[...], g_ref[...], sc_ref[...], sh_ref[...])
    h = h.reshape(nseq * steps, h.shape[-1]).astype(BF16)
    h = _dot(_row_permutation(nseq, steps, True), h).astype(BF16)
    o_ref[...] = _dot(h, w_ref[...])


def _group_mod(mod, chunk, nseq, per_seq):
    rows = mod.reshape(MOD_ROWS, N_MOD, 1, D_MODEL)
    return rows[1:1 + nseq, chunk] if per_seq else rows[0:1, chunk]


def nm_matmul_tm(x, g, mod, w, nseq, seq_len, per_seq):
    t, d = x.shape
    n = w.shape[1]
    steps = TM_ROWS // nseq
    sc = _group_mod(mod, 1, nseq, per_seq)
    sh = _group_mod(mod, 0, nseq, per_seq)
    mod_spec = pl.BlockSpec(sc.shape, lambda i: (0, 0, 0))
    return pl.pallas_call(
        functools.partial(_nm_matmul_tm_kernel, nseq, steps),
        grid=(seq_len // steps,),
        in_specs=[pl.BlockSpec((nseq, steps, d), lambda i: (0, i, 0)),
                  pl.BlockSpec((1, d), lambda i: (0, 0)),
                  mod_spec, mod_spec,
                  pl.BlockSpec((d, n), lambda i: (0, 0))],
        out_specs=pl.BlockSpec((TM_ROWS, n), lambda i: (i, 0)),
        out_shape=jax.ShapeDtypeStruct((t, n), F32),
        compiler_params=_cparams("arbitrary"),
        name="norm_mod_proj_tm",
    )(x.reshape(nseq, seq_len, d), g.reshape(1, d), sc, sh, w)


def _proj_res_tm_kernel(nseq, steps, y_ref, w_ref, x_ref, g_ref, o_ref):
    y = _dot(_row_permutation(nseq, steps, False), y_ref[...].astype(BF16)).astype(BF16)
    acc = _dot(y, w_ref[...])
    o_ref[...] = x_ref[...] + g_ref[...] * acc.reshape(nseq, steps, acc.shape[-1])


def proj_residual_tm(y, w, x, mod, nseq, seq_len, per_seq):
    t, d = x.shape
    steps = TM_ROWS // nseq
    g1 = _group_mod(mod, 2, nseq, per_seq)
    out = pl.pallas_call(
        functools.partial(_proj_res_tm_kernel, nseq, steps),
        grid=(seq_len // steps,),
        in_specs=[pl.BlockSpec((TM_ROWS, y.shape[1]), lambda i: (i, 0)),
                  pl.BlockSpec(w.shape, lambda i: (0, 0)),
                  pl.BlockSpec((nseq, steps, d), lambda i: (0, i, 0)),
                  pl.BlockSpec(g1.shape, lambda i: (0, 0, 0))],
        out_specs=pl.BlockSpec((nseq, steps, d), lambda i: (0, i, 0)),
        out_shape=jax.ShapeDtypeStruct((nseq, seq_len, d), F32),
        compiler_params=_cparams("arbitrary"),
        name="proj_residual_tm",
    )(y, w, x.reshape(nseq, seq_len, d), g1)
    return out.reshape(t, d)


def _ctx_attn_kernel(q_ref, k_ref, v_ref, o_ref, nk_ref, nv_ref):
    scale = HEAD_DIM ** -0.5
    per_tile = LANES // HEAD_DIM
    lane = lax.broadcasted_iota(jnp.int32, (1, LANES), 1)
    for hp in range(NA_HEADS // per_tile):
        sl = slice(hp * LANES, (hp + 1) * LANES)
        q, k, v = q_ref[:, sl] * scale, k_ref[:, sl], v_ref[:, sl]
        kb, vb = k.astype(BF16), v.astype(BF16)
        out = None
        for j in range(per_tile):
            h = hp * per_tile + j
            nk_ref[0, h] = k[:, j * HEAD_DIM:(j + 1) * HEAD_DIM]
            nv_ref[0, h] = v[:, j * HEAD_DIM:(j + 1) * HEAD_DIM]
            mine = lane // HEAD_DIM == j
            s = _dot_nt(jnp.where(mine, q, 0.0).astype(BF16), kb)
            p = jnp.exp(s - jnp.max(s, axis=-1, keepdims=True))
            o = _dot(p.astype(BF16), vb) / jnp.sum(p, axis=-1, keepdims=True)
            out = o if out is None else jnp.where(mine, o, out)
        o_ref[:, sl] = out.astype(o_ref.dtype)


def context_attention(u, nseq, seq_len):
    t = u.shape[0]
    kv_shape = jax.ShapeDtypeStruct((nseq, NA_HEADS, seq_len, HEAD_DIM), F32)
    kv_spec = pl.BlockSpec((1, NA_HEADS, seq_len, HEAD_DIM), lambda b: (b, 0, 0, 0))
    return pl.pallas_call(
        _ctx_attn_kernel,
        grid=(nseq,),
        in_specs=[pl.BlockSpec((seq_len, D_A), lambda b: (b, 0)),
                  pl.BlockSpec((seq_len, D_A), lambda b: (b, 1)),
                  pl.BlockSpec((seq_len, D_A), lambda b: (b, 2))],
        out_specs=[pl.BlockSpec((seq_len, D_A), lambda b: (b, 0)), kv_spec, kv_spec],
        out_shape=[jax.ShapeDtypeStruct((t, D_A), BF16), kv_shape, kv_shape],
        compiler_params=_cparams("arbitrary"),
        name="context_attention",
    )(u, u, u)


N_DR = 2 * WIN_ROWS - 1
N_DC = 2 * WIN_COLS - 1


def _na_col_tables():
    cols = np.arange(GRID_W)
    col_start = np.clip(cols - WIN_COLS // 2, 0, GRID_W - WIN_COLS)
    col_in = (cols[None, :] >= col_start[:, None]) & (cols[None, :] < col_start[:, None] + WIN_COLS)
    dc = np.clip(cols[None, :] - cols[:, None], 1 - WIN_COLS, WIN_COLS - 1) + WIN_COLS - 1
    onehot = (dc.reshape(1, -1) == np.arange(32)[:, None]).astype(np.float32)
    return onehot, col_in.reshape(1, -1).astype(np.float32)


def _na_bias_kernel(r_ref, e_ref, m_ref, o_ref):
    t = jnp.dot(r_ref[...], e_ref[...], precision=lax.Precision.HIGHEST, preferred_element_type=F32)
    o_ref[...] = jnp.where(m_ref[...] > 0.0, t, NEG_INF)


def na_bias_table(rpb):
    onehot, col_in = _na_col_tables()
    n_rows = NA_HEADS * N_DR
    r = jnp.zeros((LANES, 32), F32).at[:n_rows, :N_DC].set(rpb.reshape(n_rows, N_DC).astype(F32))
    t = pl.pallas_call(
        _na_bias_kernel,
        out_shape=jax.ShapeDtypeStruct((LANES, GRID_W * GRID_W), F32),
        name="na_bias_table",
    )(r, jnp.asarray(onehot), jnp.asarray(col_in))
    t = t[:n_rows].reshape(NA_HEADS, N_DR, GRID_W, GRID_W)
    return jnp.concatenate([t[:, :-1], t[:, 1:]], axis=-1)


def _na_kernel(rows, q_ref, k_ref, v_ref, ck_ref, cv_ref, bias_ref, o_ref,
               q_s, k_s, v_s, ck_s, cv_s, s_s, p_s, den_s, o_s):
    scale = HEAD_DIM ** -0.5
    n_lat = WIN_ROWS * GRID_W
    for h in range(NA_HEADS):
        sl = slice(h * HEAD_DIM, (h + 1) * HEAD_DIM)
        q_s[h] = (q_ref[:, sl] * scale).astype(BF16)
        k_s[h] = k_ref[:, sl].astype(BF16)
        v_s[h] = v_ref[:, sl].astype(BF16)
    ck_s[...] = ck_ref[0].astype(BF16)
    cv_s[...] = cv_ref[0].astype(BF16)

    def window(r):
        start = min(max(r - WIN_ROWS // 2, 0), rows - WIN_ROWS)
        return start, start - r + WIN_ROWS - 1

    def head_body(h, carry):
        for r in range(rows):
            start, off = window(r)
            q = q_s[h, r * GRID_W:(r + 1) * GRID_W, :]
            bias = jnp.concatenate([bias_ref[h, off + 2 * i] for i in range(WIN_ROWS // 2)], axis=1)
            s_s[r * GRID_W:(r + 1) * GRID_W, 0:n_lat] = (
                _dot_nt(q, k_s[h, start * GRID_W:start * GRID_W + n_lat, :]) + bias)
            s_s[r * GRID_W:(r + 1) * GRID_W, n_lat:] = _dot_nt(q, ck_s[h])
        for r in range(rows):
            rs = slice(r * GRID_W, (r + 1) * GRID_W)
            s = s_s[rs, :]
            p = jnp.exp(s - jnp.max(s, axis=-1, keepdims=True))
            den_s[rs, :] = jnp.sum(p, axis=-1, keepdims=True)
            p_s[rs, :] = p.astype(BF16)
        for r in range(rows):
            start, _ = window(r)
            rs = slice(r * GRID_W, (r + 1) * GRID_W)
            o = (_dot(p_s[rs, 0:n_lat], v_s[h, start * GRID_W:start * GRID_W + n_lat, :])
                 + _dot(p_s[rs, n_lat:], cv_s[h]))
            o_s[h, rs, :] = o / den_s[rs, :]
        return carry

    lax.fori_loop(0, NA_HEADS, head_body, 0)
    for h in range(NA_HEADS):
        o_ref[:, h * HEAD_DIM:(h + 1) * HEAD_DIM] = o_s[h].astype(o_ref.dtype)


def neighbourhood_attention(u, ctx_k, ctx_v, rpb, nseq, seq_len):
    t = u.shape[0]
    rows = seq_len // GRID_W
    assert rows >= WIN_ROWS and WIN_ROWS % 2 == 0
    past = ctx_k.shape[2]
    bias = na_bias_table(rpb)
    ctx_spec = pl.BlockSpec((1, NA_HEADS, past, HEAD_DIM), lambda b: (b, 0, 0, 0))
    return pl.pallas_call(
        functools.partial(_na_kernel, rows),
        grid=(nseq,),
        in_specs=[pl.BlockSpec((seq_len, D_A), lambda b: (b, 0)),
                  pl.BlockSpec((seq_len, D_A), lambda b: (b, 1)),
                  pl.BlockSpec((seq_len, D_A), lambda b: (b, 2)),
                  ctx_spec, ctx_spec,
                  pl.BlockSpec(bias.shape, lambda b: (0, 0, 0, 0))],
        out_specs=pl.BlockSpec((seq_len, D_A), lambda b: (b, 0)),
        out_shape=jax.ShapeDtypeStruct((t, D_A), BF16),
        scratch_shapes=[pltpu.VMEM((NA_HEADS, seq_len, HEAD_DIM), BF16)] * 3
        + [pltpu.VMEM((NA_HEADS, past, HEAD_DIM), BF16)] * 2
        + [pltpu.VMEM((seq_len, WIN_ROWS * GRID_W + past), F32),
           pltpu.VMEM((seq_len, WIN_ROWS * GRID_W + past), BF16),
           pltpu.VMEM((seq_len, 1), F32),
           pltpu.VMEM((NA_HEADS, seq_len, HEAD_DIM), F32)],
        compiler_params=_cparams("arbitrary"),
        name="neighbourhood_attention",
    )(u, u, u, ctx_k, ctx_v, bias)


def _dft_tables(seq_len):
    n = 2 * seq_len
    f = np.arange(seq_len, dtype=np.int64)
    ang = (np.outer(f, f) % n).astype(np.float64) * (math.pi / seq_len)
    cos, sin = np.cos(ang), np.sin(ang)
    alt = np.where(f % 2 == 0, 1.0, -1.0)
    s_fwd = -sin
    s_fwd[0, :] = alt
    fwd = np.concatenate([cos, s_fwd], axis=0)
    wf = np.where(f == 0, 1.0, 2.0) / n
    ci = cos.T * wf[None, :]
    si = -sin.T * wf[None, :]
    si[:, 0] = alt / n
    inv = np.concatenate([ci, si], axis=1)
    return fwd.astype(np.float32), inv.astype(np.float32)


def _hyena_feats(seq_len):
    t = np.linspace(0.0, 1.0, seq_len, dtype=np.float32)[:, None]
    w = (2.0 * math.pi * np.arange(seq_len, dtype=np.float32)[:, None] / seq_len).astype(np.float32)
    f = np.linspace(1e-4, HY_BANDS - 1, HY_BANDS, dtype=np.float32)[None, :]
    z = np.concatenate([t, np.cos(f * w), -np.sin(f * w)], axis=-1).astype(np.float32)
    max_decay = math.log(HY_DECAY_TARGET) / HY_FAST_PCT
    min_decay = math.log(HY_DECAY_TARGET) / HY_SLOW_PCT
    deltas = np.abs(np.linspace(min_decay, max_decay, D_B, dtype=np.float32))[None, :]
    return z, t, deltas


def _hy_filter_kernel(seq_len, z_ref, t_ref, dl_ref, w1_ref, b1_ref, w2_ref, b2_ref, w3_ref, fr_ref,
                      d_ref, fwd_ref, g_ref):
    hp = lax.Precision.HIGHEST
    h = jnp.sin(fr_ref[0:1, :] * (jnp.dot(z_ref[...], w1_ref[...], precision=hp) + b1_ref[...]))
    h = jnp.sin(fr_ref[1:2, :] * (jnp.dot(h, w2_ref[...], precision=hp) + b2_ref[...]))
    h = jnp.dot(h, w3_ref[...], precision=hp)
    decay = jnp.exp(-t_ref[...] * dl_ref[...])
    row0 = lax.broadcasted_iota(jnp.int32, (seq_len, D_B), 0) == 0
    sums, diffs = [], []
    for n in range(HY_ORDER):
        hf = h[:, (2 * n) * D_B:(2 * n + 1) * D_B] * decay
        hb = h[:, (2 * n + 1) * D_B:(2 * n + 2) * D_B] * decay
        gp = jnp.where(row0, hf + hb + d_ref[n:n + 1, :], hf)
        gm = jnp.where(row0, 0.0, hb)
        sums.append(gp + gm)
        diffs.append(gp - gm)
    rhs = jnp.concatenate(sums + diffs, axis=1).astype(BF16)
    spec = _dot(fwd_ref[...], rhs)
    for n in range(HY_ORDER):
        a = spec[:, n * D_B:(n + 1) * D_B]
        b = spec[:, (HY_ORDER + n) * D_B:(HY_ORDER + n + 1) * D_B]
        g_ref[n, 0:seq_len, :] = a[0:seq_len]
        g_ref[n, seq_len:, :] = jnp.where(row0, a[seq_len:], b[seq_len:])


def hyena_spectrum(seq_len, w1, b1, w2, b2, w3, freq, d, fwd):
    z, t, deltas = _hyena_feats(seq_len)
    return pl.pallas_call(
        functools.partial(_hy_filter_kernel, seq_len),
        out_shape=jax.ShapeDtypeStruct((HY_ORDER, 2 * seq_len, D_B), F32),
        compiler_params=pltpu.CompilerParams(vmem_limit_bytes=VMEM_LIMIT),
        name="hyena_spectrum",
    )(jnp.asarray(z), jnp.asarray(t), jnp.asarray(deltas), w1, b1.reshape(1, -1), w2, b2.reshape(1, -1),
      w3, freq, d, fwd)


def _hyena_kernel(seq_len, u_ref, sw_ref, sb_ref, g_ref, fwd_ref, inv_ref, o_ref):
    u = u_ref[...]
    t_idx = lax.broadcasted_iota(jnp.int32, u.shape, 0)
    prev = jnp.where(t_idx == 0, 0.0, pltpu.roll(u, 1, axis=0))
    nxt = jnp.where(t_idx == seq_len - 1, 0.0, pltpu.roll(u, seq_len - 1, axis=0))
    u = prev * sw_ref[0:1, :] + u * sw_ref[1:2, :] + nxt * sw_ref[2:3, :] + sb_ref[...]
    row0 = lax.broadcasted_iota(jnp.int32, (seq_len, D_B), 0) == 0
    z = u[:, 0:D_B]
    for n in range(HY_ORDER):
        spec = _dot(fwd_ref[...], z.astype(BF16))
        ure, uim = spec[0:seq_len], spec[seq_len:]
        gre, gim = g_ref[n, 0:seq_len, :], g_ref[n, seq_len:, :]
        pim = uim * gim
        yre = ure * gre - jnp.where(row0, 0.0, pim)
        yim = jnp.where(row0, pim, ure * gim + uim * gre)
        y = jnp.concatenate([yre, yim], axis=0).astype(BF16)
        z = u[:, (n + 1) * D_B:(n + 2) * D_B] * _dot(inv_ref[...], y)
    o_ref[...] = z.astype(o_ref.dtype)


def hyena_mixer(u, nseq, seq_len, short_w, short_b, spectrum, fwd, inv):
    t = u.shape[0]
    width = (HY_ORDER + 1) * D_B
    col_block = (3 * D_A) // width
    assert col_block * width == 3 * D_A
    return pl.pallas_call(
        functools.partial(_hyena_kernel, seq_len),
        grid=(nseq,),
        in_specs=[pl.BlockSpec((seq_len, width), lambda b: (b, col_block)),
                  pl.BlockSpec(short_w.shape, lambda b: (0, 0)),
                  pl.BlockSpec((1, width), lambda b: (0, 0)),
                  pl.BlockSpec(spectrum.shape, lambda b: (0, 0, 0)),
                  pl.BlockSpec(fwd.shape, lambda b: (0, 0)),
                  pl.BlockSpec(inv.shape, lambda b: (0, 0))],
        out_specs=pl.BlockSpec((seq_len, D_B), lambda b: (b, 0)),
        out_shape=jax.ShapeDtypeStruct((t, D_B), BF16),
        compiler_params=_cparams("arbitrary"),
        name="hyena_mixer",
    )(u, short_w, short_b.reshape(1, width), spectrum, fwd, inv)


RG_CB = LANES
RG_CHUNK = 512
TM_ROWS = 256


def _rglru_kernel(nseq, seq_len, gate_ref, xr_ref, cw_ref, cb_ref, wg_ref, bg_ref, lam_ref, h0_ref,
                  y_ref, fin_ref, xp_ref, a_f, b_f, a_b, b_b):
    t_tot = nseq * seq_len
    c = RG_CB
    pad = 2 * nseq
    xp_ref[0:pad, :] = jnp.zeros((pad, c), F32)
    xp_ref[pad + t_tot:, :] = jnp.zeros((pad, c), F32)
    xp_ref[pad:pad + t_tot, :] = xr_ref[...]
    nl = -lam_ref[...]
    sp = jnp.maximum(nl, 0.0) + jnp.log1p(jnp.exp(-jnp.abs(nl)))
    k2 = (-0.5 * RG_C * math.log2(math.e)) * sp

    def gate_chunk(ci, carry):
        r0 = pl.multiple_of(ci * RG_CHUNK, RG_CHUNK)
        xc = xp_ref[pl.ds(r0, RG_CHUNK), :] * cw_ref[0:1, :]
        for j in range(1, cw_ref.shape[0]):
            xc = xc + xp_ref[pl.ds(r0 + j * nseq, RG_CHUNK), :] * cw_ref[j:j + 1, :]
        xc = xc + cb_ref[...]
        gts = _dot(xc.astype(BF16), wg_ref[0]) + bg_ref[...]
        x_half = 0.5 * xc
        for d, (a_ref, b_ref) in enumerate(((a_f, b_f), (a_b, b_b))):
            t_r = jnp.tanh(gts[:, (2 * d) * c:(2 * d + 1) * c])
            t_i = jnp.tanh(gts[:, (2 * d + 1) * c:(2 * d + 2) * c])
            a = jnp.exp2(t_r * k2[d:d + 1, :] + k2[d:d + 1, :])
            a_ref[pl.ds(r0, RG_CHUNK), :] = a
            y = 1.0 - a * a
            root = jnp.where(y > 0.0, y * lax.rsqrt(y), 0.0)
            b_ref[pl.ds(r0, RG_CHUNK), :] = root * ((t_i + 1.0) * x_half)
        return carry

    lax.fori_loop(0, t_tot // RG_CHUNK, gate_chunk, 0)

    def scan_step(t, carry):
        hf, hb = carry
        rows_f = pl.ds(pl.multiple_of(t * nseq, nseq), nseq)
        rows_b = pl.ds(pl.multiple_of((seq_len - 1 - t) * nseq, nseq), nseq)
        hf = a_f[rows_f, :] * hf + b_f[rows_f, :]
        hb = a_b[rows_b, :] * hb + b_b[rows_b, :]
        b_f[rows_f, :] = hf
        b_b[rows_b, :] = hb
        return hf, hb

    hf, hb = lax.fori_loop(0, seq_len, scan_step, (h0_ref[0], h0_ref[1]), unroll=8)
    fin_ref[0] = hf
    fin_ref[1] = hb

    def out_chunk(ci, carry):
        rs = pl.ds(pl.multiple_of(ci * RG_CHUNK, RG_CHUNK), RG_CHUNK)
        y_ref[rs, :] = ((b_f[rs, :] + b_b[rs, :]) * jax.nn.gelu(gate_ref[rs, :])).astype(y_ref.dtype)
        return carry

    lax.fori_loop(0, t_tot // RG_CHUNK, out_chunk, 0)


def _rg_gate_weights(wa, wx):
    per_step = RG_CB // RG_BLOCK
    steps = D_RNN // RG_CB
    mats = []
    for d in range(2):
        for w in (wa[d], wx[d]):
            w = w.reshape(steps, per_step, RG_BLOCK, RG_BLOCK)
            eye = jnp.eye(per_step, dtype=w.dtype)
            m = jnp.einsum('spde,pq->spdqe', w, eye).reshape(steps, RG_CB, RG_CB)
            mats.append(m)
    return (0.5 * jnp.concatenate(mats, axis=-1)).astype(BF16)


def rglru_block(u, nseq, seq_len, conv_w, conv_b, wa, ba, wx, bx, lam, h0):
    t = u.shape[0]
    c = RG_CB
    steps = D_RNN // c
    wg = _rg_gate_weights(wa, wx)
    bg = jnp.stack([ba[0], bx[0], ba[1], bx[1]], axis=0).reshape(4, steps, c)
    bg = 0.5 * bg.transpose(1, 0, 2).reshape(steps, 1, 4 * c)
    y, fin = pl.pallas_call(
        functools.partial(_rglru_kernel, nseq, seq_len),
        grid=(steps,),
        in_specs=[pl.BlockSpec((t, c), lambda j: (0, j)),
                  pl.BlockSpec((t, c), lambda j: (0, steps + j)),
                  pl.BlockSpec((conv_w.shape[0], c), lambda j: (0, j)),
                  pl.BlockSpec((1, c), lambda j: (0, j)),
                  pl.BlockSpec((1, c, 4 * c), lambda j: (j, 0, 0)),
                  pl.BlockSpec((None, 1, 4 * c), lambda j: (j, 0, 0)),
                  pl.BlockSpec((2, c), lambda j: (0, j)),
                  pl.BlockSpec((2, nseq, c), lambda j: (0, 0, j))],
        out_specs=[pl.BlockSpec((t, c), lambda j: (0, j)),
                   pl.BlockSpec((2, nseq, c), lambda j: (0, 0, j))],
        out_shape=[jax.ShapeDtypeStruct((t, D_RNN), BF16),
                   jax.ShapeDtypeStruct((2, nseq, D_RNN), F32)],
        scratch_shapes=[pltpu.VMEM((t + 4 * nseq, c), F32)] + [pltpu.VMEM((t, c), F32)] * 4,
        compiler_params=_cparams("arbitrary"),
        name="rglru_block",
    )(u, u, conv_w, conv_b.reshape(1, -1), wg, bg, lam, h0)
    return y, fin


def _router_kernel(x_ref, g_ref, sc_ref, sh_ref, rw_ref, rb_ref, o_ref):
    h = _normmod(x_ref[...], g_ref[...], sc_ref[0], sh_ref[0])
    h_hi = h.astype(BF16)
    h_lo = (h - h_hi.astype(F32)).astype(BF16)
    w = rw_ref[...]
    w_hi = w.astype(BF16)
    w_lo = (w - w_hi.astype(F32)).astype(BF16)
    logits = _dot_nt(w_hi, h_hi) + (_dot_nt(w_lo, h_hi) + _dot_nt(w_hi, h_lo))
    scores = jax.nn.sigmoid(logits)
    sel = scores + rb_ref[...]
    row = [sel[e:e + 1, :] for e in range(N_EXPERTS)]
    gs = []
    for g in range(N_GROUPS):
        r = row[g * EXPERTS_PER_GROUP:(g + 1) * EXPERTS_PER_GROUP]
        best_pair = None
        for i in range(EXPERTS_PER_GROUP):
            for j in range(i + 1, EXPERTS_PER_GROUP):
                s = r[i] + r[j]
                best_pair = s if best_pair is None else jnp.maximum(best_pair, s)
        gs.append(best_pair)
    best = jnp.zeros_like(gs[0], dtype=jnp.int32)
    top = gs[0]
    for g in range(1, N_GROUPS):
        better = gs[g] > top
        best = jnp.where(better, g, best)
        top = jnp.where(better, gs[g], top)
    picked = []
    for e in range(N_EXPERTS):
        g = e // EXPERTS_PER_GROUP
        rank = jnp.zeros_like(best)
        for o in range(g * EXPERTS_PER_GROUP, (g + 1) * EXPERTS_PER_GROUP):
            if o == e:
                continue
            ahead = (row[o] > row[e]) | ((row[o] == row[e]) & (o < e))
            rank = rank + ahead.astype(jnp.int32)
        picked.append((best == g) & (rank < 2))
    den = jnp.zeros_like(gs[0])
    for e in range(N_EXPERTS):
        den = den + jnp.where(picked[e], scores[e:e + 1, :], 0.0)
    gate = [jnp.where(picked[e], scores[e:e + 1, :] / den, 0.0) for e in range(N_EXPERTS)]
    cls = jnp.zeros_like(den)
    w_a = jnp.zeros_like(den)
    w_b = jnp.zeros_like(den)
    for g in range(N_GROUPS):
        for pi, (a, b) in enumerate(MOE_PAIRS):
            ea, eb = g * EXPERTS_PER_GROUP + a, g * EXPERTS_PER_GROUP + b
            both = picked[ea] & picked[eb]
            cls = jnp.where(both, float(g * len(MOE_PAIRS) + pi), cls)
            w_a = jnp.where(both, gate[ea], w_a)
            w_b = jnp.where(both, gate[eb], w_b)
    o_ref[...] = jnp.concatenate([cls, w_a, w_b, jnp.zeros((ROUTE_ROWS - 3, cls.shape[1]), F32)], axis=0)


MOE_PAIRS = ((0, 1), (0, 2), (0, 3), (1, 3), (1, 2), (2, 3))
N_CLS = N_GROUPS * len(MOE_PAIRS)
CLS_PAD = 32
ROUTE_ROWS = 8
MOE_TS = 256
MOE_TM = 256
SLOT_BLK = 512
ROW_W = D_MODEL + LANES


def router_gates(x, g, mod, router_w, router_b, seq_len, per_seq, tm=512):
    t, d = x.shape
    return pl.pallas_call(
        _router_kernel,
        grid=(t // tm,),
        in_specs=[pl.BlockSpec((tm, d), lambda i: (i, 0)),
                  pl.BlockSpec((1, d), lambda i: (0, 0)),
                  _mod_spec(4, tm, seq_len, per_seq),
                  _mod_spec(3, tm, seq_len, per_seq),
                  pl.BlockSpec((N_EXPERTS, d), lambda i: (0, 0)),
                  pl.BlockSpec((N_EXPERTS, 1), lambda i: (0, 0))],
        out_specs=pl.BlockSpec((ROUTE_ROWS, tm), lambda i: (0, i)),
        out_shape=jax.ShapeDtypeStruct((ROUTE_ROWS, t), F32),
        compiler_params=_cparams("arbitrary"),
        name="moe_router",
    )(x, g.reshape(1, d), mod, mod, router_w.T, router_b.reshape(N_EXPERTS, 1))


def _slots_kernel(n_blk, route_ref, slot_ref, off_ref, cnt_ref):
    cid = lax.broadcasted_iota(jnp.int32, (CLS_PAD, SLOT_BLK), 0).astype(F32)

    def members(j):
        cls = route_ref[0:1, pl.ds(pl.multiple_of(j * SLOT_BLK, SLOT_BLK), SLOT_BLK)]
        return (cid == cls).astype(F32)

    def count(j, cnt):
        return cnt + jnp.sum(members(j), axis=1, keepdims=True)

    cnt = lax.fori_loop(0, n_blk, count, jnp.zeros((CLS_PAD, 1), F32))
    cnt = jnp.broadcast_to(cnt, (CLS_PAD, LANES))
    padded = jnp.ceil(cnt * (1.0 / MOE_TS)) * MOE_TS
    r = lax.broadcasted_iota(jnp.int32, (CLS_PAD, CLS_PAD), 0)
    c = lax.broadcasted_iota(jnp.int32, (CLS_PAD, CLS_PAD), 1)
    off = jnp.dot((c < r).astype(F32), padded, precision=lax.Precision.HIGHEST, preferred_element_type=F32)
    off_ref[...] = off
    cnt_ref[...] = cnt
    tr = lax.broadcasted_iota(jnp.int32, (SLOT_BLK, SLOT_BLK), 0)
    tc = lax.broadcasted_iota(jnp.int32, (SLOT_BLK, SLOT_BLK), 1)
    earlier = (tr < tc).astype(BF16)

    def assign(j, base):
        member = members(j)
        rank = _dot(member.astype(BF16), earlier)
        slot = jnp.sum(member * (rank + base), axis=0, keepdims=True)
        slot_ref[0:1, pl.ds(pl.multiple_of(j * SLOT_BLK, SLOT_BLK), SLOT_BLK)] = slot.astype(jnp.int32)
        return base + jnp.sum(member, axis=1, keepdims=True)

    lax.fori_loop(0, n_blk, assign, off[:, 0:1])


def moe_slots(route):
    t = route.shape[1]
    stat = jax.ShapeDtypeStruct((CLS_PAD, LANES), F32)
    return pl.pallas_call(
        functools.partial(_slots_kernel, t // SLOT_BLK),
        out_shape=[jax.ShapeDtypeStruct((1, t), jnp.int32), stat, stat],
        compiler_params=pltpu.CompilerParams(vmem_limit_bytes=VMEM_LIMIT),
        name="moe_slots",
    )(route)


def _tile_maps(off, cnt, n_tiles):
    off = off[:N_CLS, 0].astype(jnp.int32)
    cnt = cnt[:N_CLS, 0].astype(jnp.int32)
    ends = off + ((cnt + MOE_TS - 1) // MOE_TS) * MOE_TS
    n_used = ends[-1] // MOE_TS
    k = jnp.arange(n_tiles, dtype=jnp.int32)
    tix = jnp.minimum(k, n_used - 1)
    cls = jnp.sum((tix[:, None] * MOE_TS >= ends[None, :]).astype(jnp.int32), axis=1)
    pair = jnp.asarray(MOE_PAIRS, jnp.int32)
    grp = (cls // len(MOE_PAIRS)) * EXPERTS_PER_GROUP
    ea = grp + pair[cls % len(MOE_PAIRS), 0]
    eb = grp + pair[cls % len(MOE_PAIRS), 1]
    n = jnp.int32(n_tiles)

    def slot_plan(e):
        chg = jnp.concatenate([jnp.ones((1,), jnp.int32), (e[1:] != e[:-1]).astype(jnp.int32)])
        at = jnp.where(chg == 1, k, n)
        nxt_at = jnp.concatenate([lax.cummin(at[::-1])[::-1][1:], n.reshape(1)])
        more = (nxt_at < n).astype(jnp.int32)
        nxt = e[jnp.minimum(nxt_at, n - 1)]
        par = (jnp.cumsum(chg) - 1) % 2
        return chg, nxt, more, par.astype(jnp.int32)

    plan_a, plan_b = slot_plan(ea), slot_plan(eb)
    chg, nxt, more, par = (jnp.stack([pa, pb]) for pa, pb in zip(plan_a, plan_b))
    return ea, eb, tix, chg, nxt, more, par, n_used.reshape(1)


def _dispatch_kernel(n_steps, slots_ref, x_ref, g_ref, sc_ref, sh_ref, rt_ref, hs_in, hs_out, rowbuf, sem):
    del hs_in
    i = pl.program_id(0)
    cur = i % 2

    def wait_rows(s):
        pltpu.make_async_copy(rowbuf.at[s], rowbuf.at[s], sem.at[s]).wait()

    @pl.when(i >= 2)
    def _():
        wait_rows(cur)

    rowbuf[cur, :, 0:D_MODEL] = _normmod(x_ref[...], g_ref[...], sc_ref[0], sh_ref[0])
    rowbuf[cur, :, D_MODEL:ROW_W] = jnp.concatenate(
        [rt_ref[...], jnp.zeros((MOE_TM, LANES - ROUTE_ROWS), F32)], axis=1)

    for s in range(2):
        @pl.when(cur == s)
        def _():
            for r in range(MOE_TM):
                dst = slots_ref[i * MOE_TM + r]
                pltpu.make_async_copy(rowbuf.at[s, r], hs_out.at[dst], sem.at[s]).start()

    @pl.when(i == n_steps - 1)
    def _():
        wait_rows(cur)
        if n_steps >= 2:
            wait_rows(1 - cur)


def moe_dispatch(x, g, mod, route_t, slots, hs, seq_len, per_seq):
    t, d = x.shape
    n_steps = t // MOE_TM
    grid_spec = pltpu.PrefetchScalarGridSpec(
        num_scalar_prefetch=1,
        grid=(n_steps,),
        in_specs=[pl.BlockSpec((MOE_TM, d), lambda i, s: (i, 0)),
                  pl.BlockSpec((1, d), lambda i, s: (0, 0)),
                  _mod_spec(4, MOE_TM, seq_len, per_seq),
                  _mod_spec(3, MOE_TM, seq_len, per_seq),
                  pl.BlockSpec((MOE_TM, ROUTE_ROWS), lambda i, s: (i, 0)),
                  pl.BlockSpec(memory_space=pl.ANY)],
        out_specs=pl.BlockSpec(memory_space=pl.ANY),
        scratch_shapes=[pltpu.VMEM((2, MOE_TM, ROW_W), F32), pltpu.SemaphoreType.DMA((2,))],
    )
    return pl.pallas_call(
        functools.partial(_dispatch_kernel, n_steps),
        grid_spec=grid_spec,
        out_shape=jax.ShapeDtypeStruct(hs.shape, F32),
        input_output_aliases={6: 0},
        compiler_params=_cparams("arbitrary"),
        name="moe_dispatch",
    )(slots, x, g.reshape(1, d), mod, mod, route_t, hs)


def _experts_kernel(layer, ea_ref, eb_ref, tix_ref, chg_ref, nxt_ref, more_ref, par_ref, nused_ref,
                    hs_ref, wg_hbm, wu_hbm, wd_hbm, ys_ref, fg, fu, fd, bg, bu, bd, sem):
    k = pl.program_id(0)

    def weight_copies(slot, expert, par):
        return [pltpu.make_async_copy(src.at[layer, expert], dst.at[slot, par], sem.at[slot, par])
                for src, dst in ((wg_hbm, fg), (wu_hbm, fu), (wd_hbm, fd))]

    @pl.when(k < nused_ref[0])
    def _():
        for slot, e_ref in enumerate((ea_ref, eb_ref)):
            @pl.when(chg_ref[slot, k] == 1)
            def _():
                par = par_ref[slot, k]

                @pl.when(k == 0)
                def _():
                    for cp in weight_copies(slot, e_ref[0], par):
                        cp.start()

                for cp in weight_copies(slot, e_ref[k], par):
                    cp.wait()
                bg[slot] = fg[slot, par].astype(BF16)
                bu[slot] = fu[slot, par].astype(BF16)
                bd[slot] = fd[slot, par].astype(BF16)

                @pl.when(more_ref[slot, k] == 1)
                def _():
                    for cp in weight_copies(slot, nxt_ref[slot, k], 1 - par):
                        cp.start()

        h = hs_ref[:, 0:D_MODEL].astype(BF16)

        def ffn(slot):
            hid = _dot(h, bg[slot])
            up = _dot(h, bu[slot])
            w = hs_ref[:, D_MODEL + 1 + slot:D_MODEL + 2 + slot]
            act = (hid * _sigmoid(hid)) * up * w
            return _dot(act.astype(BF16), bd[slot])

        ys_ref[...] = ffn(0) + ffn(1)

    @pl.when(k >= nused_ref[0])
    def _():
        ys_ref[...] = jnp.zeros_like(ys_ref)


def moe_experts(hs, maps, layer, w_gate, w_up, w_down):
    n_tiles = hs.shape[0] // MOE_TS
    d = D_MODEL

    grid_spec = pltpu.PrefetchScalarGridSpec(
        num_scalar_prefetch=8,
        grid=(n_tiles,),
        in_specs=[pl.BlockSpec((MOE_TS, ROW_W), lambda k, ea, eb, tix, *_: (tix[k], 0)),
                  pl.BlockSpec(memory_space=pl.ANY), pl.BlockSpec(memory_space=pl.ANY),
                  pl.BlockSpec(memory_space=pl.ANY)],
        out_specs=pl.BlockSpec((MOE_TS, d), lambda k, *_: (k, 0)),
        scratch_shapes=[pltpu.VMEM((2, 2, d, D_EXPERT), F32), pltpu.VMEM((2, 2, d, D_EXPERT), F32),
                        pltpu.VMEM((2, 2, D_EXPERT, d), F32),
                        pltpu.VMEM((2, d, D_EXPERT), BF16), pltpu.VMEM((2, d, D_EXPERT), BF16),
                        pltpu.VMEM((2, D_EXPERT, d), BF16),
                        pltpu.SemaphoreType.DMA((2, 2))],
    )
    return pl.pallas_call(
        functools.partial(_experts_kernel, layer),
        grid_spec=grid_spec,
        out_shape=jax.ShapeDtypeStruct((hs.shape[0], d), F32),
        compiler_params=_cparams("arbitrary"),
        name="moe_experts",
    )(*maps, hs, w_gate, w_up, w_down)


def _combine_kernel(final, n_steps, slots_ref, x_ref, g2_ref, fg_ref, ys_hbm, o_ref, gbuf, sem):
    i = pl.program_id(0)
    cur = i % 2

    def issue_tile(tile, s):
        for r in range(MOE_TM):
            src = slots_ref[tile * MOE_TM + r]
            pltpu.make_async_copy(ys_hbm.at[src], gbuf.at[s, r], sem.at[s]).start()

    @pl.when(i == 0)
    def _():
        issue_tile(0, 0)

    for s in range(2):
        @pl.when((i + 1 < n_steps) & (1 - cur == s))
        def _():
            issue_tile(i + 1, s)

    pltpu.make_async_copy(gbuf.at[cur], gbuf.at[cur], sem.at[cur]).wait()
    y = x_ref[...] + g2_ref[0] * gbuf[cur]
    if final:
        ms = jnp.mean(y * y, axis=-1, keepdims=True)
        y = y * lax.rsqrt(ms + EPS) * fg_ref[...]
    o_ref[...] = y


def moe_combine(x, mod, slots, ys, final_g, final, seq_len, per_seq):
    t, d = x.shape
    n_steps = t // MOE_TM
    grid_spec = pltpu.PrefetchScalarGridSpec(
        num_scalar_prefetch=1,
        grid=(n_steps,),
        in_specs=[pl.BlockSpec((MOE_TM, d), lambda i, s: (i, 0)),
                  _mod_spec(5, MOE_TM, seq_len, per_seq),
                  pl.BlockSpec((1, d), lambda i, s: (0, 0)),
                  pl.BlockSpec(memory_space=pl.ANY)],
        out_specs=pl.BlockSpec((MOE_TM, d), lambda i, s: (i, 0)),
        scratch_shapes=[pltpu.VMEM((2, MOE_TM, d), F32), pltpu.SemaphoreType.DMA((2,))],
    )
    return pl.pallas_call(
        functools.partial(_combine_kernel, final, n_steps),
        grid_spec=grid_spec,
        out_shape=jax.ShapeDtypeStruct((t, d), F32),
        compiler_params=_cparams("arbitrary"),
        name="moe_combine",
    )(slots, x, mod, final_g.reshape(1, d), ys)


def _combine_nm_tm_kernel(nseq, steps, seq_len, n_steps, slots_ref, x_ref, g2_ref, ys_hbm, g_ref, sc_ref,
                          sh_ref, w_ref, x2_ref, u_ref, gbuf, sem):
    i = pl.program_id(0)
    cur = i % 2

    def issue_tile(tile, s):
        for sq in range(nseq):
            for t in range(steps):
                src = slots_ref[sq * seq_len + tile * steps + t]
                pltpu.make_async_copy(ys_hbm.at[src], gbuf.at[s, sq * steps + t], sem.at[s]).start()

    @pl.when(i == 0)
    def _():
        issue_tile(0, 0)

    for s in range(2):
        @pl.when((i + 1 < n_steps) & (1 - cur == s))
        def _():
            issue_tile(i + 1, s)

    pltpu.make_async_copy(gbuf.at[cur], gbuf.at[cur], sem.at[cur]).wait()
    d = x_ref.shape[-1]
    y = x_ref[...] + g2_ref[...] * gbuf[cur].reshape(nseq, steps, d)
    x2_ref[...] = y
    h = _normmod(y, g_ref[...], sc_ref[...], sh_ref[...])
    h = h.reshape(nseq * steps, d).astype(BF16)
    h = _dot(_row_permutation(nseq, steps, True), h).astype(BF16)
    u_ref[...] = _dot(h, w_ref[...])


def combine_nm_matmul_tm(x, mod_prev, slots, ys, g, mod, w, nseq, seq_len, per_seq):
    t, d = x.shape
    n = w.shape[1]
    steps = TM_ROWS // nseq
    n_steps = seq_len // steps
    g2 = _group_mod(mod_prev, 5, nseq, per_seq)
    sc = _group_mod(mod, 1, nseq, per_seq)
    sh = _group_mod(mod, 0, nseq, per_seq)
    mod_spec = pl.BlockSpec(sc.shape, lambda i, s: (0, 0, 0))
    x_spec = pl.BlockSpec((nseq, steps, d), lambda i, s: (0, i, 0))
    grid_spec = pltpu.PrefetchScalarGridSpec(
        num_scalar_prefetch=1,
        grid=(n_steps,),
        in_specs=[x_spec, mod_spec, pl.BlockSpec(memory_space=pl.ANY),
                  pl.BlockSpec((1, d), lambda i, s: (0, 0)), mod_spec, mod_spec,
                  pl.BlockSpec((d, n), lambda i, s: (0, 0))],
        out_specs=[x_spec, pl.BlockSpec((TM_ROWS, n), lambda i, s: (i, 0))],
        scratch_shapes=[pltpu.VMEM((2, TM_ROWS, d), F32), pltpu.SemaphoreType.DMA((2,))],
    )
    x2, u = pl.pallas_call(
        functools.partial(_combine_nm_tm_kernel, nseq, steps, seq_len, n_steps),
        grid_spec=grid_spec,
        out_shape=[jax.ShapeDtypeStruct((nseq, seq_len, d), F32), jax.ShapeDtypeStruct((t, n), F32)],
        compiler_params=_cparams("arbitrary"),
        name="moe_combine_norm_mod_proj_tm",
    )(slots, x.reshape(nseq, seq_len, d), g2, ys, g.reshape(1, d), sc, sh, w)
    return x2.reshape(t, d), u


def moe_block(xs, mods_l, per_seqs, seq_lens, layer, p, final, defer_combine):
    g = p['norm_g'][layer, 1]
    routes = [router_gates(x, g, mods_l, p['router_w'], p['router_b'], sl, ps)
              for x, sl, ps in zip(xs, seq_lens, per_seqs)]
    slots, off, cnt = moe_slots(jnp.concatenate(routes, axis=1))
    t_all = slots.shape[1]
    n_tiles = t_all // MOE_TS + N_CLS
    maps = _tile_maps(off, cnt, n_tiles)
    hs = jnp.zeros((n_tiles * MOE_TS, ROW_W), F32)
    bounds = np.cumsum([0] + [x.shape[0] for x in xs])
    group_slots = [slots[0, bounds[i]:bounds[i + 1]] for i in range(len(xs))]
    for x, r, s, sl, ps in zip(xs, routes, group_slots, seq_lens, per_seqs):
        hs = moe_dispatch(x, g, mods_l, r.T, s, hs, sl, ps)
    ys = moe_experts(hs, maps, layer, p['moe_w_gate'], p['moe_w_up'], p['moe_w_down'])
    if defer_combine:
        return [(x, s, ys) for x, s in zip(xs, group_slots)]
    return [moe_combine(x, mods_l, s, ys, p['final_g'], final, sl, ps)
            for x, s, sl, ps in zip(xs, group_slots, seq_lens, per_seqs)]


def _mixer(x, group, l, mod, p, hy_tables, mod_prev):
    per_seq, nseq, seq_len = group['per_seq'], group['nseq'], group['seq_len']
    extras = None
    deferred = isinstance(x, tuple)
    if deferred and l % 2 == 0:
        x = moe_combine(x[0], mod_prev, x[1], x[2], p['final_g'], False, seq_len, per_seq)
        deferred = False
    if l % 2 == 0:
        e = l // 2
        u = nm_matmul(x, p['norm_g'][l, 0], mod, p['a_in_w'][e].astype(BF16), seq_len, per_seq)
        if group['ctx_k'] is None:
            attn, nk, nv = context_attention(u, nseq, seq_len)
            extras = (nk, nv)
        else:
            attn = neighbourhood_attention(u, group['ctx_k'][:, e], group['ctx_v'][:, e], p['na_rpb'][e],
                                           nseq, seq_len)
        fwd, inv = hy_tables[seq_len]
        spectrum = hyena_spectrum(seq_len, p['hy_w1'][e], p['hy_b1'][e], p['hy_w2'][e], p['hy_b2'][e],
                                  p['hy_w3'][e], p['hy_freq'][e], p['hy_d'][e], fwd)
        hy = hyena_mixer(u, nseq, seq_len, p['hy_short_w'][e], p['hy_short_b'][e], spectrum, fwd, inv)
        w_out = p['a_out_w'][e].astype(BF16)
        x = proj_residual([attn, hy], [w_out[:D_A], w_out[D_A:]], x, mod, seq_len, per_seq)
    else:
        o = l // 2
        w_in = p['c_in_w'][o].astype(BF16)
        if deferred:
            x, u = combine_nm_matmul_tm(x[0], mod_prev, x[1], x[2], p['norm_g'][l, 0], mod, w_in,
                                        nseq, seq_len, per_seq)
        else:
            u = nm_matmul_tm(x, p['norm_g'][l, 0], mod, w_in, nseq, seq_len, per_seq)
        y, extras = rglru_block(u, nseq, seq_len, p['rg_conv_w'][o], p['rg_conv_b'][o], p['rg_wa'][o],
                                p['rg_ba'][o], p['rg_wx'][o], p['rg_bx'][o], p['rg_lam'][o], group['h0'][o])
        x = proj_residual_tm(y, p['c_out_w'][o].astype(BF16), x, mod, nseq, seq_len, per_seq)
    return x, extras


def kernel(x_prompt, x_sample, cache_k, cache_v, state_h, c, c_ctx, norm_g, ada_w, ada_b, final_g, a_in_w, a_out_w, na_rpb, hy_short_w, hy_short_b, hy_w1, hy_b1, hy_w2, hy_b2, hy_w3, hy_freq, hy_d, c_in_w, c_out_w, rg_conv_w, rg_conv_b, rg_wa, rg_ba, rg_wx, rg_bx, rg_lam, router_w, router_b, moe_w_gate, moe_w_up, moe_w_down):
    p = dict(norm_g=norm_g, final_g=final_g, a_in_w=a_in_w, a_out_w=a_out_w, na_rpb=na_rpb,
             hy_short_w=hy_short_w, hy_short_b=hy_short_b, hy_w1=hy_w1, hy_b1=hy_b1, hy_w2=hy_w2,
             hy_b2=hy_b2, hy_w3=hy_w3, hy_freq=hy_freq, hy_d=hy_d, c_in_w=c_in_w, c_out_w=c_out_w,
             rg_conv_w=rg_conv_w, rg_conv_b=rg_conv_b, rg_wa=rg_wa, rg_ba=rg_ba, rg_wx=rg_wx, rg_bx=rg_bx,
             rg_lam=rg_lam, router_w=router_w, router_b=router_b, moe_w_gate=moe_w_gate,
             moe_w_up=moe_w_up, moe_w_down=moe_w_down)
    batch, seq, d = x_prompt.shape
    dec_batch, dec_seq, _ = x_sample.shape
    n_odd = DEPTH // 2
    assert 1 + dec_batch <= MOD_ROWS

    cond = jnp.concatenate([c_ctx[None, :], c, jnp.zeros((MOD_ROWS - 1 - dec_batch, d), F32)], axis=0)
    m = modulation(cond, ada_w, ada_b)
    mods = [m[l].reshape(MOD_ROWS * N_MOD, 1, d) for l in range(DEPTH)]

    tables = {}
    for sl in (seq, dec_seq):
        fwd, inv = _dft_tables(sl)
        tables[sl] = (jnp.asarray(fwd).astype(BF16), jnp.asarray(inv).astype(BF16))

    groups = [
        dict(per_seq=False, nseq=batch, seq_len=seq, ctx_k=None, ctx_v=None,
             h0=[jnp.zeros((2, batch, D_RNN), F32)] * n_odd),
        dict(per_seq=True, nseq=dec_batch, seq_len=dec_seq, ctx_k=cache_k, ctx_v=cache_v,
             h0=[state_h[:, o].transpose(1, 0, 2) for o in range(n_odd)]),
    ]
    xs = [x_prompt.reshape(batch * seq, d), x_sample.reshape(dec_batch * dec_seq, d)]
    k_list, v_list, h_list = [], [], []
    for l in range(DEPTH):
        mixed = [_mixer(x, grp, l, mods[l], p, tables, mods[l - 1] if l else None)
                 for x, grp in zip(xs, groups)]
        if l % 2 == 0:
            k_list.append(mixed[0][1][0])
            v_list.append(mixed[0][1][1])
        else:
            h_list.append(mixed[0][1].transpose(1, 0, 2))
        xs = moe_block([mx[0] for mx in mixed], mods[l], [grp['per_seq'] for grp in groups],
                       [grp['seq_len'] for grp in groups], l, p, l == DEPTH - 1, l < DEPTH - 1)
    new_k = jnp.stack(k_list, axis=1)
    new_v = jnp.stack(v_list, axis=1)
    new_h = jnp.stack(h_list, axis=1)
    return (xs[0].reshape(batch, seq, d), xs[1].reshape(dec_batch, dec_seq, d), new_k, new_v, new_h)
```

```python
import functools
import math

import numpy as np
import jax
import jax.numpy as jnp
from jax import lax
from jax.experimental import pallas as pl
from jax.experimental.pallas import tpu as pltpu

F32 = jnp.float32
BF16 = jnp.bfloat16

D_MODEL = 1024
DEPTH = 2
GRID_W = 64
EPS = 1e-6
NEG_INF = -1e30
NA_HEADS = 8
HEAD_DIM = 64
D_A = NA_HEADS * HEAD_DIM
WIN_ROWS = 8
WIN_COLS = 16
D_B = D_MODEL - D_A
HY_ORDER = 2
HY_EMB = 33
HY_BANDS = (HY_EMB - 1) // 2
HY_FFN = 64
HY_DECAY_TARGET = 1e-2
HY_FAST_PCT = 0.3
HY_SLOW_PCT = 1.5
D_RNN = D_MODEL
RG_BLOCK = 64
RG_C = 8.0
N_EXPERTS = 16
N_GROUPS = 4
EXPERTS_PER_GROUP = N_EXPERTS // N_GROUPS
D_EXPERT = 512

LANES = 128
VMEM_LIMIT = 56 * 1024 * 1024
N_MOD = 6
MOD_ROWS = 16


def _cparams(*sem):
    return pltpu.CompilerParams(dimension_semantics=sem, vmem_limit_bytes=VMEM_LIMIT)


def _dot(a, b):
    return jnp.dot(a, b, preferred_element_type=F32)


def _dot_nt(a, b):
    return lax.dot_general(a, b, (((1,), (1,)), ((), ())), preferred_element_type=F32)


def _sigmoid(x):
    return 0.5 * jnp.tanh(0.5 * x) + 0.5


def _normmod(x, g, sc, sh):
    ms = jnp.mean(x * x, axis=-1, keepdims=True)
    return (x * lax.rsqrt(ms + EPS) * g) * (1.0 + sc) + sh


def _mod_spec(chunk, tm, seq_len, per_seq):
    if per_seq:
        return pl.BlockSpec((1, 1, D_MODEL), lambda i, *_: ((1 + (i * tm) // seq_len) * N_MOD + chunk, 0, 0))
    return pl.BlockSpec((1, 1, D_MODEL), lambda i, *_: (chunk, 0, 0))


def _mod_kernel(c_ref, w_ref, b_ref, o_ref):
    s = c_ref[...]
    s = s * jax.nn.sigmoid(s)
    o_ref[0] = _dot(s.astype(BF16), w_ref[0].astype(BF16)) + b_ref[0]


def modulation(cond, ada_w, ada_b):
    tn = 1536
    n = ada_w.shape[-1]
    return pl.pallas_call(
        _mod_kernel,
        grid=(DEPTH, n // tn),
        in_specs=[pl.BlockSpec((MOD_ROWS, D_MODEL), lambda l, j: (0, 0)),
                  pl.BlockSpec((1, D_MODEL, tn), lambda l, j: (l, 0, j)),
                  pl.BlockSpec((1, 1, tn), lambda l, j: (l, 0, j))],
        out_specs=pl.BlockSpec((1, MOD_ROWS, tn), lambda l, j: (l, 0, j)),
        out_shape=jax.ShapeDtypeStruct((DEPTH, MOD_ROWS, n), F32),
        compiler_params=_cparams("arbitrary", "arbitrary"),
        name="modulation",
    )(cond, ada_w, ada_b.reshape(DEPTH, 1, n))


def _nm_matmul_kernel(x_ref, g_ref, sc_ref, sh_ref, w_ref, o_ref):
    h = _normmod(x_ref[...], g_ref[...], sc_ref[0], sh_ref[0])
    o_ref[...] = _dot(h.astype(BF16), w_ref[...])


def nm_matmul(x, g, mod, w, seq_len, per_seq, tm=512):
    t, d = x.shape
    n = w.shape[1]
    return pl.pallas_call(
        _nm_matmul_kernel,
        grid=(t // tm,),
        in_specs=[pl.BlockSpec((tm, d), lambda i: (i, 0)),
                  pl.BlockSpec((1, d), lambda i: (0, 0)),
                  _mod_spec(1, tm, seq_len, per_seq),
                  _mod_spec(0, tm, seq_len, per_seq),
                  pl.BlockSpec((d, n), lambda i: (0, 0))],
        out_specs=pl.BlockSpec((tm, n), lambda i: (i, 0)),
        out_shape=jax.ShapeDtypeStruct((t, n), F32),
        compiler_params=_cparams("arbitrary"),
        name="norm_mod_proj",
    )(x, g.reshape(1, d), mod, mod, w)


def _proj_res_kernel(n_act, *refs):
    acts = refs[:n_act]
    ws = refs[n_act:2 * n_act]
    x_ref, g_ref, ng_ref, sc2_ref, sh2_ref, rw_ref, rb_ref, o_ref, route_ref = refs[2 * n_act:]
    acc = _dot(acts[0][...].astype(BF16), ws[0][...])
    for a, w in zip(acts[1:], ws[1:]):
        acc += _dot(a[...].astype(BF16), w[...])
    x = x_ref[...] + g_ref[0] * acc
    o_ref[...] = x
    route_ref[...] = _route_record(_normmod(x, ng_ref[...], sc2_ref[0], sh2_ref[0]), rw_ref[...], rb_ref[...])


def proj_residual(acts, ws, x, mod, norm2_g, router_w, router_b, seq_len, per_seq, tm=512):
    t, d = x.shape
    in_specs = [pl.BlockSpec((tm, a.shape[1]), lambda i: (i, 0)) for a in acts]
    in_specs += [pl.BlockSpec(w.shape, lambda i: (0, 0)) for w in ws]
    in_specs += [pl.BlockSpec((tm, d), lambda i: (i, 0)), _mod_spec(2, tm, seq_len, per_seq),
                 pl.BlockSpec((1, d), lambda i: (0, 0)),
                 _mod_spec(4, tm, seq_len, per_seq), _mod_spec(3, tm, seq_len, per_seq),
                 pl.BlockSpec((N_EXPERTS, d), lambda i: (0, 0)),
                 pl.BlockSpec((N_EXPERTS, 1), lambda i: (0, 0))]
    return pl.pallas_call(
        functools.partial(_proj_res_kernel, len(acts)),
        grid=(t // tm,),
        in_specs=in_specs,
        out_specs=[pl.BlockSpec((tm, d), lambda i: (i, 0)), pl.BlockSpec((ROUTE_ROWS, tm), lambda i: (0, i))],
        out_shape=[jax.ShapeDtypeStruct((t, d), F32), jax.ShapeDtypeStruct((ROUTE_ROWS, t), F32)],
        compiler_params=_cparams("arbitrary"),
        name="proj_residual_route",
    )(*acts, *ws, x, mod, norm2_g.reshape(1, d), mod, mod, router_w.T, router_b.reshape(N_EXPERTS, 1))


def _row_permutation(nseq, steps, to_time_major):
    n = nseq * steps
    i = lax.broadcasted_iota(jnp.int32, (n, n), 0)
    j = lax.broadcasted_iota(jnp.int32, (n, n), 1)
    if to_time_major:
        src = (i % nseq) * steps + i // nseq
    else:
        src = (i % steps) * nseq + i // steps
    return (j == src).astype(BF16)


def _nm_matmul_tm_kernel(nseq, steps, x_ref, g_ref, sc_ref, sh_ref, w_ref, o_ref):
    h = _normmod(x_ref[...], g_ref[...], sc_ref[...], sh_ref[...])
    h = h.reshape(nseq * steps, h.shape[-1]).astype(BF16)
    h = _dot(_row_permutation(nseq, steps, True), h).astype(BF16)
    o_ref[...] = _dot(h, w_ref[...])


def _group_mod(mod, chunk, nseq, per_seq):
    rows = mod.reshape(MOD_ROWS, N_MOD, 1, D_MODEL)
    return rows[1:1 + nseq, chunk] if per_seq else rows[0:1, chunk]


def nm_matmul_tm(x, g, mod, w, nseq, seq_len, per_seq):
    t, d = x.shape
    n = w.shape[1]
    steps = TM_ROWS // nseq
    sc = _group_mod(mod, 1, nseq, per_seq)
    sh = _group_mod(mod, 0, nseq, per_seq)
    mod_spec = pl.BlockSpec(sc.shape, lambda i: (0, 0, 0))
    return pl.pallas_call(
        functools.partial(_nm_matmul_tm_kernel, nseq, steps),
        grid=(seq_len // steps,),
        in_specs=[pl.BlockSpec((nseq, steps, d), lambda i: (0, i, 0)),
                  pl.BlockSpec((1, d), lambda i: (0, 0)),
                  mod_spec, mod_spec,
                  pl.BlockSpec((d, n), lambda i: (0, 0))],
        out_specs=pl.BlockSpec((TM_ROWS, n), lambda i: (i, 0)),
        out_shape=jax.ShapeDtypeStruct((t, n), F32),
        compiler_params=_cparams("arbitrary"),
        name="norm_mod_proj_tm",
    )(x.reshape(nseq, seq_len, d), g.reshape(1, d), sc, sh, w)


def _proj_res_tm_kernel(nseq, steps, y_ref, w_ref, x_ref, g_ref, ng_ref, sc2_ref, sh2_ref, rw_ref, rb_ref,
                        o_ref, route_ref):
    y = _dot(_row_permutation(nseq, steps, False), y_ref[...].astype(BF16)).astype(BF16)
    acc = _dot(y, w_ref[...])
    x = x_ref[...] + g_ref[...] * acc.reshape(nseq, steps, acc.shape[-1])
    o_ref[...] = x
    h = _normmod(x, ng_ref[...], sc2_ref[...], sh2_ref[...]).reshape(nseq * steps, x.shape[-1])
    route_ref[0] = _route_record(h, rw_ref[...], rb_ref[...])


def proj_residual_tm(y, w, x, mod, norm2_g, router_w, router_b, nseq, seq_len, per_seq):
    t, d = x.shape
    steps = TM_ROWS // nseq
    n_steps = seq_len // steps
    g1 = _group_mod(mod, 2, nseq, per_seq)
    sc2 = _group_mod(mod, 4, nseq, per_seq)
    sh2 = _group_mod(mod, 3, nseq, per_seq)
    mod_spec = pl.BlockSpec(g1.shape, lambda i: (0, 0, 0))
    out, route = pl.pallas_call(
        functools.partial(_proj_res_tm_kernel, nseq, steps),
        grid=(n_steps,),
        in_specs=[pl.BlockSpec((TM_ROWS, y.shape[1]), lambda i: (i, 0)),
                  pl.BlockSpec(w.shape, lambda i: (0, 0)),
                  pl.BlockSpec((nseq, steps, d), lambda i: (0, i, 0)),
                  mod_spec,
                  pl.BlockSpec((1, d), lambda i: (0, 0)),
                  mod_spec, mod_spec,
                  pl.BlockSpec((N_EXPERTS, d), lambda i: (0, 0)),
                  pl.BlockSpec((N_EXPERTS, 1), lambda i: (0, 0))],
        out_specs=[pl.BlockSpec((nseq, steps, d), lambda i: (0, i, 0)),
                   pl.BlockSpec((1, ROUTE_ROWS, TM_ROWS), lambda i: (i, 0, 0))],
        out_shape=[jax.ShapeDtypeStruct((nseq, seq_len, d), F32),
                   jax.ShapeDtypeStruct((n_steps, ROUTE_ROWS, TM_ROWS), F32)],
        compiler_params=_cparams("arbitrary"),
        name="proj_residual_tm_route",
    )(y, w, x.reshape(nseq, seq_len, d), g1, norm2_g.reshape(1, d), sc2, sh2, router_w.T,
      router_b.reshape(N_EXPERTS, 1))
    route = route.reshape(n_steps, ROUTE_ROWS, nseq, steps).transpose(1, 2, 0, 3).reshape(ROUTE_ROWS, t)
    return out.reshape(t, d), route


def _ctx_attn_kernel(q_ref, k_ref, v_ref, o_ref, nk_ref, nv_ref):
    scale = HEAD_DIM ** -0.5
    per_tile = LANES // HEAD_DIM
    lane = lax.broadcasted_iota(jnp.int32, (1, LANES), 1)
    for hp in range(NA_HEADS // per_tile):
        sl = slice(hp * LANES, (hp + 1) * LANES)
        q, k, v = q_ref[:, sl] * scale, k_ref[:, sl], v_ref[:, sl]
        kb, vb = k.astype(BF16), v.astype(BF16)
        out = None
        for j in range(per_tile):
            h = hp * per_tile + j
            nk_ref[0, h] = k[:, j * HEAD_DIM:(j + 1) * HEAD_DIM]
            nv_ref[0, h] = v[:, j * HEAD_DIM:(j + 1) * HEAD_DIM]
            mine = lane // HEAD_DIM == j
            s = _dot_nt(jnp.where(mine, q, 0.0).astype(BF16), kb)
            p = jnp.exp(s - jnp.max(s, axis=-1, keepdims=True))
            o = _dot(p.astype(BF16), vb) / jnp.sum(p, axis=-1, keepdims=True)
            out = o if out is None else jnp.where(mine, o, out)
        o_ref[:, sl] = out.astype(o_ref.dtype)


def context_attention(u, nseq, seq_len):
    t = u.shape[0]
    kv_shape = jax.ShapeDtypeStruct((nseq, NA_HEADS, seq_len, HEAD_DIM), F32)
    kv_spec = pl.BlockSpec((1, NA_HEADS, seq_len, HEAD_DIM), lambda b: (b, 0, 0, 0))
    return pl.pallas_call(
        _ctx_attn_kernel,
        grid=(nseq,),
        in_specs=[pl.BlockSpec((seq_len, D_A), lambda b: (b, 0)),
                  pl.BlockSpec((seq_len, D_A), lambda b: (b, 1)),
                  pl.BlockSpec((seq_len, D_A), lambda b: (b, 2))],
        out_specs=[pl.BlockSpec((seq_len, D_A), lambda b: (b, 0)), kv_spec, kv_spec],
        out_shape=[jax.ShapeDtypeStruct((t, D_A), BF16), kv_shape, kv_shape],
        compiler_params=_cparams("arbitrary"),
        name="context_attention",
    )(u, u, u)


N_DR = 2 * WIN_ROWS - 1
N_DC = 2 * WIN_COLS - 1


def _na_col_tables():
    cols = np.arange(GRID_W)
    col_start = np.clip(cols - WIN_COLS // 2, 0, GRID_W - WIN_COLS)
    col_in = (cols[None, :] >= col_start[:, None]) & (cols[None, :] < col_start[:, None] + WIN_COLS)
    dc = np.clip(cols[None, :] - cols[:, None], 1 - WIN_COLS, WIN_COLS - 1) + WIN_COLS - 1
    onehot = (dc.reshape(1, -1) == np.arange(32)[:, None]).astype(np.float32)
    return onehot, col_in.reshape(1, -1).astype(np.float32)


def _na_bias_kernel(r_ref, e_ref, m_ref, o_ref):
    t = jnp.dot(r_ref[...], e_ref[...], precision=lax.Precision.HIGHEST, preferred_element_type=F32)
    o_ref[...] = jnp.where(m_ref[...] > 0.0, t, NEG_INF)


def na_bias_table(rpb):
    onehot, col_in = _na_col_tables()
    n_rows = NA_HEADS * N_DR
    r = jnp.zeros((LANES, 32), F32).at[:n_rows, :N_DC].set(rpb.reshape(n_rows, N_DC).astype(F32))
    t = pl.pallas_call(
        _na_bias_kernel,
        out_shape=jax.ShapeDtypeStruct((LANES, GRID_W * GRID_W), F32),
        name="na_bias_table",
    )(r, jnp.asarray(onehot), jnp.asarray(col_in))
    t = t[:n_rows].reshape(NA_HEADS, N_DR, GRID_W, GRID_W)
    return jnp.concatenate([t[:, :-1], t[:, 1:]], axis=-1)


def _na_kernel(rows, q_ref, k_ref, v_ref, ck_ref, cv_ref, bias_ref, o_ref,
               q_s, k_s, v_s, ck_s, cv_s, s_s, p_s, den_s, o_s):
    scale = HEAD_DIM ** -0.5
    n_lat = WIN_ROWS * GRID_W
    for h in range(NA_HEADS):
        sl = slice(h * HEAD_DIM, (h + 1) * HEAD_DIM)
        q_s[h] = (q_ref[:, sl] * scale).astype(BF16)
        k_s[h] = k_ref[:, sl].astype(BF16)
        v_s[h] = v_ref[:, sl].astype(BF16)
    ck_s[...] = ck_ref[0].astype(BF16)
    cv_s[...] = cv_ref[0].astype(BF16)

    def window(r):
        start = min(max(r - WIN_ROWS // 2, 0), rows - WIN_ROWS)
        return start, start - r + WIN_ROWS - 1

    def head_body(h, carry):
        for r in range(rows):
            start, off = window(r)
            q = q_s[h, r * GRID_W:(r + 1) * GRID_W, :]
            bias = jnp.concatenate([bias_ref[h, off + 2 * i] for i in range(WIN_ROWS // 2)], axis=1)
            s_s[r * GRID_W:(r + 1) * GRID_W, 0:n_lat] = (
                _dot_nt(q, k_s[h, start * GRID_W:start * GRID_W + n_lat, :]) + bias)
            s_s[r * GRID_W:(r + 1) * GRID_W, n_lat:] = _dot_nt(q, ck_s[h])
        for r in range(rows):
            rs = slice(r * GRID_W, (r + 1) * GRID_W)
            s = s_s[rs, :]
            p = jnp.exp(s - jnp.max(s, axis=-1, keepdims=True))
            den_s[rs, :] = jnp.sum(p, axis=-1, keepdims=True)
            p_s[rs, :] = p.astype(BF16)
        for r in range(rows):
            start, _ = window(r)
            rs = slice(r * GRID_W, (r + 1) * GRID_W)
            o = (_dot(p_s[rs, 0:n_lat], v_s[h, start * GRID_W:start * GRID_W + n_lat, :])
                 + _dot(p_s[rs, n_lat:], cv_s[h]))
            o_s[h, rs, :] = o / den_s[rs, :]
        return carry

    lax.fori_loop(0, NA_HEADS, head_body, 0)
    for h in range(NA_HEADS):
        o_ref[:, h * HEAD_DIM:(h + 1) * HEAD_DIM] = o_s[h].astype(o_ref.dtype)


def neighbourhood_attention(u, ctx_k, ctx_v, rpb, nseq, seq_len):
    t = u.shape[0]
    rows = seq_len // GRID_W
    assert rows >= WIN_ROWS and WIN_ROWS % 2 == 0
    past = ctx_k.shape[2]
    bias = na_bias_table(rpb)
    ctx_spec = pl.BlockSpec((1, NA_HEADS, past, HEAD_DIM), lambda b: (b, 0, 0, 0))
    return pl.pallas_call(
        functools.partial(_na_kernel, rows),
        grid=(nseq,),
        in_specs=[pl.BlockSpec((seq_len, D_A), lambda b: (b, 0)),
                  pl.BlockSpec((seq_len, D_A), lambda b: (b, 1)),
                  pl.BlockSpec((seq_len, D_A), lambda b: (b, 2)),
                  ctx_spec, ctx_spec,
                  pl.BlockSpec(bias.shape, lambda b: (0, 0, 0, 0))],
        out_specs=pl.BlockSpec((seq_len, D_A), lambda b: (b, 0)),
        out_shape=jax.ShapeDtypeStruct((t, D_A), BF16),
        scratch_shapes=[pltpu.VMEM((NA_HEADS, seq_len, HEAD_DIM), BF16)] * 3
        + [pltpu.VMEM((NA_HEADS, past, HEAD_DIM), BF16)] * 2
        + [pltpu.VMEM((seq_len, WIN_ROWS * GRID_W + past), F32),
           pltpu.VMEM((seq_len, WIN_ROWS * GRID_W + past), BF16),
           pltpu.VMEM((seq_len, 1), F32),
           pltpu.VMEM((NA_HEADS, seq_len, HEAD_DIM), F32)],
        compiler_params=_cparams("arbitrary"),
        name="neighbourhood_attention",
    )(u, u, u, ctx_k, ctx_v, bias)


def _dft_tables(seq_len):
    n = 2 * seq_len
    f = np.arange(seq_len, dtype=np.int64)
    ang = (np.outer(f, f) % n).astype(np.float64) * (math.pi / seq_len)
    cos, sin = np.cos(ang), np.sin(ang)
    alt = np.where(f % 2 == 0, 1.0, -1.0)
    s_fwd = -sin
    s_fwd[0, :] = alt
    fwd = np.concatenate([cos, s_fwd], axis=0)
    wf = np.where(f == 0, 1.0, 2.0) / n
    ci = cos.T * wf[None, :]
    si = -sin.T * wf[None, :]
    si[:, 0] = alt / n
    inv = np.concatenate([ci, si], axis=1)
    return fwd.astype(np.float32), inv.astype(np.float32)


def _hyena_feats(seq_len):
    t = np.linspace(0.0, 1.0, seq_len, dtype=np.float32)[:, None]
    w = (2.0 * math.pi * np.arange(seq_len, dtype=np.float32)[:, None] / seq_len).astype(np.float32)
    f = np.linspace(1e-4, HY_BANDS - 1, HY_BANDS, dtype=np.float32)[None, :]
    z = np.concatenate([t, np.cos(f * w), -np.sin(f * w)], axis=-1).astype(np.float32)
    max_decay = math.log(HY_DECAY_TARGET) / HY_FAST_PCT
    min_decay = math.log(HY_DECAY_TARGET) / HY_SLOW_PCT
    deltas = np.abs(np.linspace(min_decay, max_decay, D_B, dtype=np.float32))[None, :]
    return z, t, deltas


def _hy_filter_kernel(seq_len, z_ref, t_ref, dl_ref, w1_ref, b1_ref, w2_ref, b2_ref, w3_ref, fr_ref,
                      d_ref, fwd_ref, g_ref):
    hp = lax.Precision.HIGHEST
    h = jnp.sin(fr_ref[0:1, :] * (jnp.dot(z_ref[...], w1_ref[...], precision=hp) + b1_ref[...]))
    h = jnp.sin(fr_ref[1:2, :] * (jnp.dot(h, w2_ref[...], precision=hp) + b2_ref[...]))
    h = jnp.dot(h, w3_ref[...], precision=hp)
    decay = jnp.exp(-t_ref[...] * dl_ref[...])
    row0 = lax.broadcasted_iota(jnp.int32, (seq_len, D_B), 0) == 0
    sums, diffs = [], []
    for n in range(HY_ORDER):
        hf = h[:, (2 * n) * D_B:(2 * n + 1) * D_B] * decay
        hb = h[:, (2 * n + 1) * D_B:(2 * n + 2) * D_B] * decay
        gp = jnp.where(row0, hf + hb + d_ref[n:n + 1, :], hf)
        gm = jnp.where(row0, 0.0, hb)
        sums.append(gp + gm)
        diffs.append(gp - gm)
    rhs = jnp.concatenate(sums + diffs, axis=1).astype(BF16)
    spec = _dot(fwd_ref[...], rhs)
    for n in range(HY_ORDER):
        a = spec[:, n * D_B:(n + 1) * D_B]
        b = spec[:, (HY_ORDER + n) * D_B:(HY_ORDER + n + 1) * D_B]
        g_ref[n, 0:seq_len, :] = a[0:seq_len]
        g_ref[n, seq_len:, :] = jnp.where(row0, a[seq_len:], b[seq_len:])


def hyena_spectrum(seq_len, w1, b1, w2, b2, w3, freq, d, fwd):
    z, t, deltas = _hyena_feats(seq_len)
    return pl.pallas_call(
        functools.partial(_hy_filter_kernel, seq_len),
        out_shape=jax.ShapeDtypeStruct((HY_ORDER, 2 * seq_len, D_B), F32),
        compiler_params=pltpu.CompilerParams(vmem_limit_bytes=VMEM_LIMIT),
        name="hyena_spectrum",
    )(jnp.asarray(z), jnp.asarray(t), jnp.asarray(deltas), w1, b1.reshape(1, -1), w2, b2.reshape(1, -1),
      w3, freq, d, fwd)


def _hyena_kernel(seq_len, u_ref, sw_ref, sb_ref, g_ref, fwd_ref, inv_ref, o_ref):
    u = u_ref[...]
    t_idx = lax.broadcasted_iota(jnp.int32, u.shape, 0)
    prev = jnp.where(t_idx == 0, 0.0, pltpu.roll(u, 1, axis=0))
    nxt = jnp.where(t_idx == seq_len - 1, 0.0, pltpu.roll(u, seq_len - 1, axis=0))
    u = prev * sw_ref[0:1, :] + u * sw_ref[1:2, :] + nxt * sw_ref[2:3, :] + sb_ref[...]
    row0 = lax.broadcasted_iota(jnp.int32, (seq_len, D_B), 0) == 0
    z = u[:, 0:D_B]
    for n in range(HY_ORDER):
        spec = _dot(fwd_ref[...], z.astype(BF16))
        ure, uim = spec[0:seq_len], spec[seq_len:]
        gre, gim = g_ref[n, 0:seq_len, :], g_ref[n, seq_len:, :]
        pim = uim * gim
        yre = ure * gre - jnp.where(row0, 0.0, pim)
        yim = jnp.where(row0, pim, ure * gim + uim * gre)
        y = jnp.concatenate([yre, yim], axis=0).astype(BF16)
        z = u[:, (n + 1) * D_B:(n + 2) * D_B] * _dot(inv_ref[...], y)
    o_ref[...] = z.astype(o_ref.dtype)


def hyena_mixer(u, nseq, seq_len, short_w, short_b, spectrum, fwd, inv):
    t = u.shape[0]
    width = (HY_ORDER + 1) * D_B
    col_block = (3 * D_A) // width
    assert col_block * width == 3 * D_A
    return pl.pallas_call(
        functools.partial(_hyena_kernel, seq_len),
        grid=(nseq,),
        in_specs=[pl.BlockSpec((seq_len, width), lambda b: (b, col_block)),
                  pl.BlockSpec(short_w.shape, lambda b: (0, 0)),
                  pl.BlockSpec((1, width), lambda b: (0, 0)),
                  pl.BlockSpec(spectrum.shape, lambda b: (0, 0, 0)),
                  pl.BlockSpec(fwd.shape, lambda b: (0, 0)),
                  pl.BlockSpec(inv.shape, lambda b: (0, 0))],
        out_specs=pl.BlockSpec((seq_len, D_B), lambda b: (b, 0)),
        out_shape=jax.ShapeDtypeStruct((t, D_B), BF16),
        compiler_params=_cparams("arbitrary"),
        name="hyena_mixer",
    )(u, short_w, short_b.reshape(1, width), spectrum, fwd, inv)


RG_CB = LANES
RG_CHUNK = 512
TM_ROWS = 256


def _rglru_kernel(nseq, seq_len, gate_ref, xr_ref, cw_ref, cb_ref, wg_ref, bg_ref, lam_ref, h0_ref,
                  y_ref, fin_ref, xp_ref, a_f, b_f, a_b, b_b):
    t_tot = nseq * seq_len
    c = RG_CB
    pad = 2 * nseq
    xp_ref[0:pad, :] = jnp.zeros((pad, c), F32)
    xp_ref[pad + t_tot:, :] = jnp.zeros((pad, c), F32)
    xp_ref[pad:pad + t_tot, :] = xr_ref[...]
    nl = -lam_ref[...]
    sp = jnp.maximum(nl, 0.0) + jnp.log1p(jnp.exp(-jnp.abs(nl)))
    k2 = (-0.5 * RG_C * math.log2(math.e)) * sp

    def gate_chunk(ci, carry):
        r0 = pl.multiple_of(ci * RG_CHUNK, RG_CHUNK)
        xc = xp_ref[pl.ds(r0, RG_CHUNK), :] * cw_ref[0:1, :]
        for j in range(1, cw_ref.shape[0]):
            xc = xc + xp_ref[pl.ds(r0 + j * nseq, RG_CHUNK), :] * cw_ref[j:j + 1, :]
        xc = xc + cb_ref[...]
        gts = _dot(xc.astype(BF16), wg_ref[0]) + bg_ref[...]
        x_half = 0.5 * xc
        for d, (a_ref, b_ref) in enumerate(((a_f, b_f), (a_b, b_b))):
            t_r = jnp.tanh(gts[:, (2 * d) * c:(2 * d + 1) * c])
            t_i = jnp.tanh(gts[:, (2 * d + 1) * c:(2 * d + 2) * c])
            a = jnp.exp2(t_r * k2[d:d + 1, :] + k2[d:d + 1, :])
            a_ref[pl.ds(r0, RG_CHUNK), :] = a
            y = 1.0 - a * a
            root = jnp.where(y > 0.0, y * lax.rsqrt(y), 0.0)
            b_ref[pl.ds(r0, RG_CHUNK), :] = root * ((t_i + 1.0) * x_half)
        return carry

    lax.fori_loop(0, t_tot // RG_CHUNK, gate_chunk, 0)

    def scan_step(t, carry):
        hf, hb = carry
        rows_f = pl.ds(pl.multiple_of(t * nseq, nseq), nseq)
        rows_b = pl.ds(pl.multiple_of((seq_len - 1 - t) * nseq, nseq), nseq)
        hf = a_f[rows_f, :] * hf + b_f[rows_f, :]
        hb = a_b[rows_b, :] * hb + b_b[rows_b, :]
        b_f[rows_f, :] = hf
        b_b[rows_b, :] = hb
        return hf, hb

    hf, hb = lax.fori_loop(0, seq_len, scan_step, (h0_ref[0], h0_ref[1]), unroll=8)
    fin_ref[0] = hf
    fin_ref[1] = hb

    def out_chunk(ci, carry):
        rs = pl.ds(pl.multiple_of(ci * RG_CHUNK, RG_CHUNK), RG_CHUNK)
        y_ref[rs, :] = ((b_f[rs, :] + b_b[rs, :]) * jax.nn.gelu(gate_ref[rs, :])).astype(y_ref.dtype)
        return carry

    lax.fori_loop(0, t_tot // RG_CHUNK, out_chunk, 0)


def _rg_gate_weights(wa, wx):
    per_step = RG_CB // RG_BLOCK
    steps = D_RNN // RG_CB
    mats = []
    for d in range(2):
        for w in (wa[d], wx[d]):
            w = w.reshape(steps, per_step, RG_BLOCK, RG_BLOCK)
            eye = jnp.eye(per_step, dtype=w.dtype)
            m = jnp.einsum('spde,pq->spdqe', w, eye).reshape(steps, RG_CB, RG_CB)
            mats.append(m)
    return (0.5 * jnp.concatenate(mats, axis=-1)).astype(BF16)


def rglru_block(u, nseq, seq_len, conv_w, conv_b, wa, ba, wx, bx, lam, h0):
    t = u.shape[0]
    c = RG_CB
    steps = D_RNN // c
    wg = _rg_gate_weights(wa, wx)
    bg = jnp.stack([ba[0], bx[0], ba[1], bx[1]], axis=0).reshape(4, steps, c)
    bg = 0.5 * bg.transpose(1, 0, 2).reshape(steps, 1, 4 * c)
    y, fin = pl.pallas_call(
        functools.partial(_rglru_kernel, nseq, seq_len),
        grid=(steps,),
        in_specs=[pl.BlockSpec((t, c), lambda j: (0, j)),
                  pl.BlockSpec((t, c), lambda j: (0, steps + j)),
                  pl.BlockSpec((conv_w.shape[0], c), lambda j: (0, j)),
                  pl.BlockSpec((1, c), lambda j: (0, j)),
                  pl.BlockSpec((1, c, 4 * c), lambda j: (j, 0, 0)),
                  pl.BlockSpec((None, 1, 4 * c), lambda j: (j, 0, 0)),
                  pl.BlockSpec((2, c), lambda j: (0, j)),
                  pl.BlockSpec((2, nseq, c), lambda j: (0, 0, j))],
        out_specs=[pl.BlockSpec((t, c), lambda j: (0, j)),
                   pl.BlockSpec((2, nseq, c), lambda j: (0, 0, j))],
        out_shape=[jax.ShapeDtypeStruct((t, D_RNN), BF16),
                   jax.ShapeDtypeStruct((2, nseq, D_RNN), F32)],
        scratch_shapes=[pltpu.VMEM((t + 4 * nseq, c), F32)] + [pltpu.VMEM((t, c), F32)] * 4,
        compiler_params=_cparams("arbitrary"),
        name="rglru_block",
    )(u, u, conv_w, conv_b.reshape(1, -1), wg, bg, lam, h0)
    return y, fin


def _route_record(h, w, rb):
    h_hi = h.astype(BF16)
    h_lo = (h - h_hi.astype(F32)).astype(BF16)
    w_hi = w.astype(BF16)
    w_lo = (w - w_hi.astype(F32)).astype(BF16)
    logits = _dot_nt(w_hi, h_hi) + (_dot_nt(w_lo, h_hi) + _dot_nt(w_hi, h_lo))
    scores = jax.nn.sigmoid(logits)
    sel = scores + rb
    row = [sel[e:e + 1, :] for e in range(N_EXPERTS)]
    gs = []
    for g in range(N_GROUPS):
        r = row[g * EXPERTS_PER_GROUP:(g + 1) * EXPERTS_PER_GROUP]
        best_pair = None
        for i in range(EXPERTS_PER_GROUP):
            for j in range(i + 1, EXPERTS_PER_GROUP):
                s = r[i] + r[j]
                best_pair = s if best_pair is None else jnp.maximum(best_pair, s)
        gs.append(best_pair)
    best = jnp.zeros_like(gs[0], dtype=jnp.int32)
    top = gs[0]
    for g in range(1, N_GROUPS):
        better = gs[g] > top
        best = jnp.where(better, g, best)
        top = jnp.where(better, gs[g], top)
    picked = []
    for e in range(N_EXPERTS):
        g = e // EXPERTS_PER_GROUP
        rank = jnp.zeros_like(best)
        for o in range(g * EXPERTS_PER_GROUP, (g + 1) * EXPERTS_PER_GROUP):
            if o == e:
                continue
            ahead = (row[o] > row[e]) | ((row[o] == row[e]) & (o < e))
            rank = rank + ahead.astype(jnp.int32)
        picked.append((best == g) & (rank < 2))
    den = jnp.zeros_like(gs[0])
    for e in range(N_EXPERTS):
        den = den + jnp.where(picked[e], scores[e:e + 1, :], 0.0)
    gate = [jnp.where(picked[e], scores[e:e + 1, :] / den, 0.0) for e in range(N_EXPERTS)]
    cls = jnp.zeros_like(den)
    w_a = jnp.zeros_like(den)
    w_b = jnp.zeros_like(den)
    for g in range(N_GROUPS):
        for pi, (a, b) in enumerate(MOE_PAIRS):
            ea, eb = g * EXPERTS_PER_GROUP + a, g * EXPERTS_PER_GROUP + b
            both = picked[ea] & picked[eb]
            cls = jnp.where(both, float(g * len(MOE_PAIRS) + pi), cls)
            w_a = jnp.where(both, gate[ea], w_a)
            w_b = jnp.where(both, gate[eb], w_b)
    return jnp.concatenate([cls, w_a, w_b, jnp.zeros((ROUTE_ROWS - 3, cls.shape[1]), F32)], axis=0)


MOE_PAIRS = ((0, 1), (0, 2), (0, 3), (1, 3), (1, 2), (2, 3))
N_CLS = N_GROUPS * len(MOE_PAIRS)
CLS_PAD = 32
ROUTE_ROWS = 8
MOE_TS = 256
MOE_TM = 256
MOE_STEP_TILES = 4
SLOT_BLK = 512
ROW_W = D_MODEL + LANES


def _slots_kernel(n_blk, route_ref, slot_ref, off_ref, cnt_ref):
    cid = lax.broadcasted_iota(jnp.int32, (CLS_PAD, SLOT_BLK), 0).astype(F32)

    def members(j):
        cls = route_ref[0:1, pl.ds(pl.multiple_of(j * SLOT_BLK, SLOT_BLK), SLOT_BLK)]
        return (cid == cls).astype(F32)

    def count(j, cnt):
        return cnt + jnp.sum(members(j), axis=1, keepdims=True)

    cnt = lax.fori_loop(0, n_blk, count, jnp.zeros((CLS_PAD, 1), F32))
    cnt = jnp.broadcast_to(cnt, (CLS_PAD, LANES))
    padded = jnp.ceil(cnt * (1.0 / MOE_TS)) * MOE_TS
    r = lax.broadcasted_iota(jnp.int32, (CLS_PAD, CLS_PAD), 0)
    c = lax.broadcasted_iota(jnp.int32, (CLS_PAD, CLS_PAD), 1)
    off = jnp.dot((c < r).astype(F32), padded, precision=lax.Precision.HIGHEST, preferred_element_type=F32)
    off_ref[...] = off
    cnt_ref[...] = cnt
    tr = lax.broadcasted_iota(jnp.int32, (SLOT_BLK, SLOT_BLK), 0)
    tc = lax.broadcasted_iota(jnp.int32, (SLOT_BLK, SLOT_BLK), 1)
    earlier = (tr < tc).astype(BF16)

    def assign(j, base):
        member = members(j)
        rank = _dot(member.astype(BF16), earlier)
        slot = jnp.sum(member * (rank + base), axis=0, keepdims=True)
        slot_ref[0:1, pl.ds(pl.multiple_of(j * SLOT_BLK, SLOT_BLK), SLOT_BLK)] = slot.astype(jnp.int32)
        return base + jnp.sum(member, axis=1, keepdims=True)

    lax.fori_loop(0, n_blk, assign, off[:, 0:1])


def moe_slots(route):
    t = route.shape[1]
    stat = jax.ShapeDtypeStruct((CLS_PAD, LANES), F32)
    return pl.pallas_call(
        functools.partial(_slots_kernel, t // SLOT_BLK),
        out_shape=[jax.ShapeDtypeStruct((1, t), jnp.int32), stat, stat],
        compiler_params=pltpu.CompilerParams(vmem_limit_bytes=VMEM_LIMIT),
        name="moe_slots",
    )(route)


def _tile_maps(off, cnt, n_tiles):
    off = off[:N_CLS, 0].astype(jnp.int32)
    cnt = cnt[:N_CLS, 0].astype(jnp.int32)
    ends = off + ((cnt + MOE_TS - 1) // MOE_TS) * MOE_TS
    n_used = ends[-1] // MOE_TS
    k = jnp.arange(n_tiles, dtype=jnp.int32)
    tix = jnp.minimum(k, n_used - 1)
    cls = jnp.sum((tix[:, None] * MOE_TS >= ends[None, :]).astype(jnp.int32), axis=1)
    pair = jnp.asarray(MOE_PAIRS, jnp.int32)
    grp = (cls // len(MOE_PAIRS)) * EXPERTS_PER_GROUP
    ea = grp + pair[cls % len(MOE_PAIRS), 0]
    eb = grp + pair[cls % len(MOE_PAIRS), 1]
    n = jnp.int32(n_tiles)

    def slot_plan(e):
        chg = jnp.concatenate([jnp.ones((1,), jnp.int32), (e[1:] != e[:-1]).astype(jnp.int32)])
        at = jnp.where(chg == 1, k, n)
        nxt_at = jnp.concatenate([lax.cummin(at[::-1])[::-1][1:], n.reshape(1)])
        more = (nxt_at < n).astype(jnp.int32)
        nxt = e[jnp.minimum(nxt_at, n - 1)]
        par = (jnp.cumsum(chg) - 1) % 2
        return chg, nxt, more, par.astype(jnp.int32)

    plan_a, plan_b = slot_plan(ea), slot_plan(eb)
    chg, nxt, more, par = (jnp.stack([pa, pb]) for pa, pb in zip(plan_a, plan_b))
    return ea, eb, chg, nxt, more, par, n_used.reshape(1)


def _dispatch_kernel(n_steps, slots_ref, x_ref, g_ref, sc_ref, sh_ref, rt_ref, hs_in, hs_out, rowbuf, sem):
    del hs_in
    i = pl.program_id(0)
    cur = i % 2

    def wait_rows(s):
        pltpu.make_async_copy(rowbuf.at[s], rowbuf.at[s], sem.at[s]).wait()

    @pl.when(i >= 2)
    def _():
        wait_rows(cur)

    rowbuf[cur, :, 0:D_MODEL] = _normmod(x_ref[...], g_ref[...], sc_ref[0], sh_ref[0])
    rowbuf[cur, :, D_MODEL:ROW_W] = jnp.concatenate(
        [rt_ref[...], jnp.zeros((MOE_TM, LANES - ROUTE_ROWS), F32)], axis=1)

    for s in range(2):
        @pl.when(cur == s)
        def _():
            for r in range(MOE_TM):
                dst = slots_ref[i * MOE_TM + r]
                pltpu.make_async_copy(rowbuf.at[s, r], hs_out.at[dst], sem.at[s]).start()

    @pl.when(i == n_steps - 1)
    def _():
        wait_rows(cur)
        if n_steps >= 2:
            wait_rows(1 - cur)


def moe_dispatch(x, g, mod, route_t, slots, hs, seq_len, per_seq):
    t, d = x.shape
    n_steps = t // MOE_TM
    grid_spec = pltpu.PrefetchScalarGridSpec(
        num_scalar_prefetch=1,
        grid=(n_steps,),
        in_specs=[pl.BlockSpec((MOE_TM, d), lambda i, s: (i, 0)),
                  pl.BlockSpec((1, d), lambda i, s: (0, 0)),
                  _mod_spec(4, MOE_TM, seq_len, per_seq),
                  _mod_spec(3, MOE_TM, seq_len, per_seq),
                  pl.BlockSpec((MOE_TM, ROUTE_ROWS), lambda i, s: (i, 0)),
                  pl.BlockSpec(memory_space=pl.ANY)],
        out_specs=pl.BlockSpec(memory_space=pl.ANY),
        scratch_shapes=[pltpu.VMEM((2, MOE_TM, ROW_W), F32), pltpu.SemaphoreType.DMA((2,))],
    )
    return pl.pallas_call(
        functools.partial(_dispatch_kernel, n_steps),
        grid_spec=grid_spec,
        out_shape=jax.ShapeDtypeStruct(hs.shape, F32),
        input_output_aliases={6: 0},
        compiler_params=_cparams("arbitrary"),
        name="moe_dispatch",
    )(slots, x, g.reshape(1, d), mod, mod, route_t, hs)


def _experts_kernel(layer, ea_ref, eb_ref, chg_ref, nxt_ref, more_ref, par_ref, nused_ref,
                    hs_ref, wg_hbm, wu_hbm, wd_hbm, ys_ref, fg, fu, fd, bg, bu, bd, sem):
    def weight_copies(slot, expert, par):
        return [pltpu.make_async_copy(src.at[layer, expert], dst.at[slot, par], sem.at[slot, par])
                for src, dst in ((wg_hbm, fg), (wu_hbm, fu), (wd_hbm, fd))]

    def tile(k, rows):
        @pl.when(k < nused_ref[0])
        def _():
            for slot, e_ref in enumerate((ea_ref, eb_ref)):
                @pl.when(chg_ref[slot, k] == 1)
                def _():
                    par = par_ref[slot, k]

                    @pl.when(k == 0)
                    def _():
                        for cp in weight_copies(slot, e_ref[0], par):
                            cp.start()

                    for cp in weight_copies(slot, e_ref[k], par):
                        cp.wait()
                    bg[slot] = fg[slot, par].astype(BF16)
                    bu[slot] = fu[slot, par].astype(BF16)
                    bd[slot] = fd[slot, par].astype(BF16)

                    @pl.when(more_ref[slot, k] == 1)
                    def _():
                        for cp in weight_copies(slot, nxt_ref[slot, k], 1 - par):
                            cp.start()

            h = hs_ref[rows, 0:D_MODEL].astype(BF16)

            def ffn(slot):
                hid = _dot(h, bg[slot])
                up = _dot(h, bu[slot])
                w = hs_ref[rows, D_MODEL + 1 + slot:D_MODEL + 2 + slot]
                act = (hid * _sigmoid(hid)) * up * w
                return _dot(act.astype(BF16), bd[slot])

            ys_ref[rows, :] = ffn(0) + ffn(1)

        @pl.when(k >= nused_ref[0])
        def _():
            ys_ref[rows, :] = jnp.zeros((MOE_TS, ys_ref.shape[1]), ys_ref.dtype)

    for j in range(MOE_STEP_TILES):
        tile(pl.program_id(0) * MOE_STEP_TILES + j, slice(j * MOE_TS, (j + 1) * MOE_TS))


def moe_experts(hs, maps, layer, w_gate, w_up, w_down):
    n_tiles = hs.shape[0] // MOE_TS
    d = D_MODEL

    grid_spec = pltpu.PrefetchScalarGridSpec(
        num_scalar_prefetch=7,
        grid=(n_tiles // MOE_STEP_TILES,),
        in_specs=[pl.BlockSpec((MOE_STEP_TILES * MOE_TS, ROW_W), lambda k, *_: (k, 0)),
                  pl.BlockSpec(memory_space=pl.ANY), pl.BlockSpec(memory_space=pl.ANY),
                  pl.BlockSpec(memory_space=pl.ANY)],
        out_specs=pl.BlockSpec((MOE_STEP_TILES * MOE_TS, d), lambda k, *_: (k, 0)),
        scratch_shapes=[pltpu.VMEM((2, 2, d, D_EXPERT), F32), pltpu.VMEM((2, 2, d, D_EXPERT), F32),
                        pltpu.VMEM((2, 2, D_EXPERT, d), F32),
                        pltpu.VMEM((2, d, D_EXPERT), BF16), pltpu.VMEM((2, d, D_EXPERT), BF16),
                        pltpu.VMEM((2, D_EXPERT, d), BF16),
                        pltpu.SemaphoreType.DMA((2, 2))],
    )
    return pl.pallas_call(
        functools.partial(_experts_kernel, layer),
        grid_spec=grid_spec,
        out_shape=jax.ShapeDtypeStruct((hs.shape[0], d), F32),
        compiler_params=_cparams("arbitrary"),
        name="moe_experts",
    )(*maps, hs, w_gate, w_up, w_down)


def _combine_kernel(final, n_steps, slots_ref, x_ref, g2_ref, fg_ref, ys_hbm, o_ref, gbuf, sem):
    i = pl.program_id(0)
    cur = i % 2

    def issue_tile(tile, s):
        for r in range(MOE_TM):
            src = slots_ref[tile * MOE_TM + r]
            pltpu.make_async_copy(ys_hbm.at[src], gbuf.at[s, r], sem.at[s]).start()

    @pl.when(i == 0)
    def _():
        issue_tile(0, 0)

    for s in range(2):
        @pl.when((i + 1 < n_steps) & (1 - cur == s))
        def _():
            issue_tile(i + 1, s)

    pltpu.make_async_copy(gbuf.at[cur], gbuf.at[cur], sem.at[cur]).wait()
    y = x_ref[...] + g2_ref[0] * gbuf[cur]
    if final:
        ms = jnp.mean(y * y, axis=-1, keepdims=True)
        y = y * lax.rsqrt(ms + EPS) * fg_ref[...]
    o_ref[...] = y


def moe_combine(x, mod, slots, ys, final_g, final, seq_len, per_seq):
    t, d = x.shape
    n_steps = t // MOE_TM
    grid_spec = pltpu.PrefetchScalarGridSpec(
        num_scalar_prefetch=1,
        grid=(n_steps,),
        in_specs=[pl.BlockSpec((MOE_TM, d), lambda i, s: (i, 0)),
                  _mod_spec(5, MOE_TM, seq_len, per_seq),
                  pl.BlockSpec((1, d), lambda i, s: (0, 0)),
                  pl.BlockSpec(memory_space=pl.ANY)],
        out_specs=pl.BlockSpec((MOE_TM, d), lambda i, s: (i, 0)),
        scratch_shapes=[pltpu.VMEM((2, MOE_TM, d), F32), pltpu.SemaphoreType.DMA((2,))],
    )
    return pl.pallas_call(
        functools.partial(_combine_kernel, final, n_steps),
        grid_spec=grid_spec,
        out_shape=jax.ShapeDtypeStruct((t, d), F32),
        compiler_params=_cparams("arbitrary"),
        name="moe_combine",
    )(slots, x, mod, final_g.reshape(1, d), ys)


def _combine_nm_tm_kernel(nseq, steps, seq_len, n_steps, slots_ref, x_ref, g2_ref, ys_hbm, g_ref, sc_ref,
                          sh_ref, w_ref, x2_ref, u_ref, gbuf, sem):
    i = pl.program_id(0)
    cur = i % 2

    def issue_tile(tile, s):
        for sq in range(nseq):
            for t in range(steps):
                src = slots_ref[sq * seq_len + tile * steps + t]
                pltpu.make_async_copy(ys_hbm.at[src], gbuf.at[s, sq * steps + t], sem.at[s]).start()

    @pl.when(i == 0)
    def _():
        issue_tile(0, 0)

    for s in range(2):
        @pl.when((i + 1 < n_steps) & (1 - cur == s))
        def _():
            issue_tile(i + 1, s)

    pltpu.make_async_copy(gbuf.at[cur], gbuf.at[cur], sem.at[cur]).wait()
    d = x_ref.shape[-1]
    y = x_ref[...] + g2_ref[...] * gbuf[cur].reshape(nseq, steps, d)
    x2_ref[...] = y
    h = _normmod(y, g_ref[...], sc_ref[...], sh_ref[...])
    h = h.reshape(nseq * steps, d).astype(BF16)
    h = _dot(_row_permutation(nseq, steps, True), h).astype(BF16)
    u_ref[...] = _dot(h, w_ref[...])


def combine_nm_matmul_tm(x, mod_prev, slots, ys, g, mod, w, nseq, seq_len, per_seq):
    t, d = x.shape
    n = w.shape[1]
    steps = TM_ROWS // nseq
    n_steps = seq_len // steps
    g2 = _group_mod(mod_prev, 5, nseq, per_seq)
    sc = _group_mod(mod, 1, nseq, per_seq)
    sh = _group_mod(mod, 0, nseq, per_seq)
    mod_spec = pl.BlockSpec(sc.shape, lambda i, s: (0, 0, 0))
    x_spec = pl.BlockSpec((nseq, steps, d), lambda i, s: (0, i, 0))
    grid_spec = pltpu.PrefetchScalarGridSpec(
        num_scalar_prefetch=1,
        grid=(n_steps,),
        in_specs=[x_spec, mod_spec, pl.BlockSpec(memory_space=pl.ANY),
                  pl.BlockSpec((1, d), lambda i, s: (0, 0)), mod_spec, mod_spec,
                  pl.BlockSpec((d, n), lambda i, s: (0, 0))],
        out_specs=[x_spec, pl.BlockSpec((TM_ROWS, n), lambda i, s: (i, 0))],
        scratch_shapes=[pltpu.VMEM((2, TM_ROWS, d), F32), pltpu.SemaphoreType.DMA((2,))],
    )
    x2, u = pl.pallas_call(
        functools.partial(_combine_nm_tm_kernel, nseq, steps, seq_len, n_steps),
        grid_spec=grid_spec,
        out_shape=[jax.ShapeDtypeStruct((nseq, seq_len, d), F32), jax.ShapeDtypeStruct((t, n), F32)],
        compiler_params=_cparams("arbitrary"),
        name="moe_combine_norm_mod_proj_tm",
    )(slots, x.reshape(nseq, seq_len, d), g2, ys, g.reshape(1, d), sc, sh, w)
    return x2.reshape(t, d), u


def moe_block(xs, routes, mods_l, per_seqs, seq_lens, layer, p, final, defer_combine):
    g = p['norm_g'][layer, 1]
    slots, off, cnt = moe_slots(jnp.concatenate(routes, axis=1))
    t_all = slots.shape[1]
    n_tiles = -(-(t_all // MOE_TS + N_CLS) // MOE_STEP_TILES) * MOE_STEP_TILES
    maps = _tile_maps(off, cnt, n_tiles)
    hs = jnp.zeros((n_tiles * MOE_TS, ROW_W), F32)
    bounds = np.cumsum([0] + [x.shape[0] for x in xs])
    group_slots = [slots[0, bounds[i]:bounds[i + 1]] for i in range(len(xs))]
    for x, r, s, sl, ps in zip(xs, routes, group_slots, seq_lens, per_seqs):
        hs = moe_dispatch(x, g, mods_l, r.T, s, hs, sl, ps)
    ys = moe_experts(hs, maps, layer, p['moe_w_gate'], p['moe_w_up'], p['moe_w_down'])
    if defer_combine:
        return [(x, s, ys) for x, s in zip(xs, group_slots)]
    return [moe_combine(x, mods_l, s, ys, p['final_g'], final, sl, ps)
            for x, s, sl, ps in zip(xs, group_slots, seq_lens, per_seqs)]


def _mixer(x, group, l, mod, p, hy_tables, mod_prev):
    per_seq, nseq, seq_len = group['per_seq'], group['nseq'], group['seq_len']
    extras = None
    deferred = isinstance(x, tuple)
    if deferred and l % 2 == 0:
        x = moe_combine(x[0], mod_prev, x[1], x[2], p['final_g'], False, seq_len, per_seq)
        deferred = False
    if l % 2 == 0:
        e = l // 2
        u = nm_matmul(x, p['norm_g'][l, 0], mod, p['a_in_w'][e].astype(BF16), seq_len, per_seq)
        if group['ctx_k'] is None:
            attn, nk, nv = context_attention(u, nseq, seq_len)
            extras = (nk, nv)
        else:
            attn = neighbourhood_attention(u, group['ctx_k'][:, e], group['ctx_v'][:, e], p['na_rpb'][e],
                                           nseq, seq_len)
        fwd, inv = hy_tables[seq_len]
        spectrum = hyena_spectrum(seq_len, p['hy_w1'][e], p['hy_b1'][e], p['hy_w2'][e], p['hy_b2'][e],
                                  p['hy_w3'][e], p['hy_freq'][e], p['hy_d'][e], fwd)
        hy = hyena_mixer(u, nseq, seq_len, p['hy_short_w'][e], p['hy_short_b'][e], spectrum, fwd, inv)
        w_out = p['a_out_w'][e].astype(BF16)
        x, route = proj_residual([attn, hy], [w_out[:D_A], w_out[D_A:]], x, mod, p['norm_g'][l, 1],
                                 p['router_w'], p['router_b'], seq_len, per_seq)
    else:
        o = l // 2
        w_in = p['c_in_w'][o].astype(BF16)
        if deferred:
            x, u = combine_nm_matmul_tm(x[0], mod_prev, x[1], x[2], p['norm_g'][l, 0], mod, w_in,
                                        nseq, seq_len, per_seq)
        else:
            u = nm_matmul_tm(x, p['norm_g'][l, 0], mod, w_in, nseq, seq_len, per_seq)
        y, extras = rglru_block(u, nseq, seq_len, p['rg_conv_w'][o], p['rg_conv_b'][o], p['rg_wa'][o],
                                p['rg_ba'][o], p['rg_wx'][o], p['rg_bx'][o], p['rg_lam'][o], group['h0'][o])
        x, route = proj_residual_tm(y, p['c_out_w'][o].astype(BF16), x, mod, p['norm_g'][l, 1],
                                    p['router_w'], p['router_b'], nseq, seq_len, per_seq)
    return x, extras, route


def kernel(x_prompt, x_sample, cache_k, cache_v, state_h, c, c_ctx, norm_g, ada_w, ada_b, final_g, a_in_w, a_out_w, na_rpb, hy_short_w, hy_short_b, hy_w1, hy_b1, hy_w2, hy_b2, hy_w3, hy_freq, hy_d, c_in_w, c_out_w, rg_conv_w, rg_conv_b, rg_wa, rg_ba, rg_wx, rg_bx, rg_lam, router_w, router_b, moe_w_gate, moe_w_up, moe_w_down):
    p = dict(norm_g=norm_g, final_g=final_g, a_in_w=a_in_w, a_out_w=a_out_w, na_rpb=na_rpb,
             hy_short_w=hy_short_w, hy_short_b=hy_short_b, hy_w1=hy_w1, hy_b1=hy_b1, hy_w2=hy_w2,
             hy_b2=hy_b2, hy_w3=hy_w3, hy_freq=hy_freq, hy_d=hy_d, c_in_w=c_in_w, c_out_w=c_out_w,
             rg_conv_w=rg_conv_w, rg_conv_b=rg_conv_b, rg_wa=rg_wa, rg_ba=rg_ba, rg_wx=rg_wx, rg_bx=rg_bx,
             rg_lam=rg_lam, router_w=router_w, router_b=router_b, moe_w_gate=moe_w_gate,
             moe_w_up=moe_w_up, moe_w_down=moe_w_down)
    batch, seq, d = x_prompt.shape
    dec_batch, dec_seq, _ = x_sample.shape
    n_odd = DEPTH // 2
    assert 1 + dec_batch <= MOD_ROWS

    cond = jnp.concatenate([c_ctx[None, :], c, jnp.zeros((MOD_ROWS - 1 - dec_batch, d), F32)], axis=0)
    m = modulation(cond, ada_w, ada_b)
    mods = [m[l].reshape(MOD_ROWS * N_MOD, 1, d) for l in range(DEPTH)]

    tables = {}
    for sl in (seq, dec_seq):
        fwd, inv = _dft_tables(sl)
        tables[sl] = (jnp.asarray(fwd).astype(BF16), jnp.asarray(inv).astype(BF16))

    groups = [
        dict(per_seq=False, nseq=batch, seq_len=seq, ctx_k=None, ctx_v=None,
             h0=[jnp.zeros((2, batch, D_RNN), F32)] * n_odd),
        dict(per_seq=True, nseq=dec_batch, seq_len=dec_seq, ctx_k=cache_k, ctx_v=cache_v,
             h0=[state_h[:, o].transpose(1, 0, 2) for o in range(n_odd)]),
    ]
    xs = [x_prompt.reshape(batch * seq, d), x_sample.reshape(dec_batch * dec_seq, d)]
    k_list, v_list, h_list = [], [], []
    for l in range(DEPTH):
        mixed = [_mixer(x, grp, l, mods[l], p, tables, mods[l - 1] if l else None)
                 for x, grp in zip(xs, groups)]
        if l % 2 == 0:
            k_list.append(mixed[0][1][0])
            v_list.append(mixed[0][1][1])
        else:
            h_list.append(mixed[0][1].transpose(1, 0, 2))
        xs = moe_block([mx[0] for mx in mixed], [mx[2] for mx in mixed], mods[l],
                       [grp['per_seq'] for grp in groups],
                       [grp['seq_len'] for grp in groups], l, p, l == DEPTH - 1, l < DEPTH - 1)
    new_k = jnp.stack(k_list, axis=1)
    new_v = jnp.stack(v_list, axis=1)
    new_h = jnp.stack(h_list, axis=1)
    return (xs[0].reshape(batch, seq, d), xs[1].reshape(dec_batch, dec_seq, d), new_k, new_v, new_h)
```

```python
import functools
import math

import numpy as np
import jax
import jax.numpy as jnp
from jax import lax
from jax.experimental import pallas as pl
from jax.experimental.pallas import tpu as pltpu

F32 = jnp.float32
BF16 = jnp.bfloat16

D_MODEL = 1024
DEPTH = 2
GRID_W = 64
EPS = 1e-6
NEG_INF = -1e30
NA_HEADS = 8
HEAD_DIM = 64
D_A = NA_HEADS * HEAD_DIM
WIN_ROWS = 8
WIN_COLS = 16
D_B = D_MODEL - D_A
HY_ORDER = 2
HY_EMB = 33
HY_BANDS = (HY_EMB - 1) // 2
HY_FFN = 64
HY_DECAY_TARGET = 1e-2
HY_FAST_PCT = 0.3
HY_SLOW_PCT = 1.5
D_RNN = D_MODEL
RG_BLOCK = 64
RG_C = 8.0
N_EXPERTS = 16
N_GROUPS = 4
EXPERTS_PER_GROUP = N_EXPERTS // N_GROUPS
D_EXPERT = 512

LANES = 128
VMEM_LIMIT = 56 * 1024 * 1024
N_MOD = 6
MOD_ROWS = 16


def _cparams(*sem):
    return pltpu.CompilerParams(dimension_semantics=sem, vmem_limit_bytes=VMEM_LIMIT)


def _dot(a, b):
    return jnp.dot(a, b, preferred_element_type=F32)


def _dot_nt(a, b):
    return lax.dot_general(a, b, (((1,), (1,)), ((), ())), preferred_element_type=F32)


def _sigmoid(x):
    return 0.5 * jnp.tanh(0.5 * x) + 0.5


def _normmod(x, g, sc, sh):
    ms = jnp.mean(x * x, axis=-1, keepdims=True)
    return (x * lax.rsqrt(ms + EPS) * g) * (1.0 + sc) + sh


def _mod_spec(chunk, tm, seq_len, per_seq):
    if per_seq:
        return pl.BlockSpec((1, 1, D_MODEL), lambda i, *_: ((1 + (i * tm) // seq_len) * N_MOD + chunk, 0, 0))
    return pl.BlockSpec((1, 1, D_MODEL), lambda i, *_: (chunk, 0, 0))


def _mod_kernel(c_ref, w_ref, b_ref, o_ref):
    s = c_ref[...]
    s = s * jax.nn.sigmoid(s)
    o_ref[0] = _dot(s.astype(BF16), w_ref[0].astype(BF16)) + b_ref[0]


def modulation(cond, ada_w, ada_b):
    tn = 1536
    n = ada_w.shape[-1]
    return pl.pallas_call(
        _mod_kernel,
        grid=(DEPTH, n // tn),
        in_specs=[pl.BlockSpec((MOD_ROWS, D_MODEL), lambda l, j: (0, 0)),
                  pl.BlockSpec((1, D_MODEL, tn), lambda l, j: (l, 0, j)),
                  pl.BlockSpec((1, 1, tn), lambda l, j: (l, 0, j))],
        out_specs=pl.BlockSpec((1, MOD_ROWS, tn), lambda l, j: (l, 0, j)),
        out_shape=jax.ShapeDtypeStruct((DEPTH, MOD_ROWS, n), F32),
        compiler_params=_cparams("arbitrary", "arbitrary"),
        name="modulation",
    )(cond, ada_w, ada_b.reshape(DEPTH, 1, n))


def _nm_matmul_kernel(x_ref, g_ref, sc_ref, sh_ref, w_ref, o_ref):
    h = _normmod(x_ref[...], g_ref[...], sc_ref[0], sh_ref[0])
    o_ref[...] = _dot(h.astype(BF16), w_ref[...])


def nm_matmul(x, g, mod, w, seq_len, per_seq, tm=512):
    t, d = x.shape
    n = w.shape[1]
    return pl.pallas_call(
        _nm_matmul_kernel,
        grid=(t // tm,),
        in_specs=[pl.BlockSpec((tm, d), lambda i: (i, 0)),
                  pl.BlockSpec((1, d), lambda i: (0, 0)),
                  _mod_spec(1, tm, seq_len, per_seq),
                  _mod_spec(0, tm, seq_len, per_seq),
                  pl.BlockSpec((d, n), lambda i: (0, 0))],
        out_specs=pl.BlockSpec((tm, n), lambda i: (i, 0)),
        out_shape=jax.ShapeDtypeStruct((t, n), F32),
        compiler_params=_cparams("arbitrary"),
        name="norm_mod_proj",
    )(x, g.reshape(1, d), mod, mod, w)


def _proj_res_kernel(n_act, *refs):
    acts = refs[:n_act]
    ws = refs[n_act:2 * n_act]
    x_ref, g_ref, ng_ref, sc2_ref, sh2_ref, rw_ref, rb_ref, o_ref, route_ref = refs[2 * n_act:]
    acc = _dot(acts[0][...].astype(BF16), ws[0][...])
    for a, w in zip(acts[1:], ws[1:]):
        acc += _dot(a[...].astype(BF16), w[...])
    x = x_ref[...] + g_ref[0] * acc
    o_ref[...] = x
    route_ref[...] = _route_record(_normmod(x, ng_ref[...], sc2_ref[0], sh2_ref[0]), rw_ref[...], rb_ref[...])


def proj_residual(acts, ws, x, mod, norm2_g, router_w, router_b, seq_len, per_seq, tm=512):
    t, d = x.shape
    in_specs = [pl.BlockSpec((tm, a.shape[1]), lambda i: (i, 0)) for a in acts]
    in_specs += [pl.BlockSpec(w.shape, lambda i: (0, 0)) for w in ws]
    in_specs += [pl.BlockSpec((tm, d), lambda i: (i, 0)), _mod_spec(2, tm, seq_len, per_seq),
                 pl.BlockSpec((1, d), lambda i: (0, 0)),
                 _mod_spec(4, tm, seq_len, per_seq), _mod_spec(3, tm, seq_len, per_seq),
                 pl.BlockSpec((N_EXPERTS, d), lambda i: (0, 0)),
                 pl.BlockSpec((N_EXPERTS, 1), lambda i: (0, 0))]
    return pl.pallas_call(
        functools.partial(_proj_res_kernel, len(acts)),
        grid=(t // tm,),
        in_specs=in_specs,
        out_specs=[pl.BlockSpec((tm, d), lambda i: (i, 0)), pl.BlockSpec((ROUTE_ROWS, tm), lambda i: (0, i))],
        out_shape=[jax.ShapeDtypeStruct((t, d), F32), jax.ShapeDtypeStruct((ROUTE_ROWS, t), F32)],
        compiler_params=_cparams("arbitrary"),
        name="proj_residual_route",
    )(*acts, *ws, x, mod, norm2_g.reshape(1, d), mod, mod, router_w.T, router_b.reshape(N_EXPERTS, 1))


def _row_permutation(nseq, steps, to_time_major):
    n = nseq * steps
    i = lax.broadcasted_iota(jnp.int32, (n, n), 0)
    j = lax.broadcasted_iota(jnp.int32, (n, n), 1)
    if to_time_major:
        src = (i % nseq) * steps + i // nseq
    else:
        src = (i % steps) * nseq + i // steps
    return (j == src).astype(BF16)


def _nm_matmul_tm_kernel(nseq, steps, x_ref, g_ref, sc_ref, sh_ref, w_ref, o_ref):
    h = _normmod(x_ref[...], g_ref[...], sc_ref[...], sh_ref[...])
    h = h.reshape(nseq * steps, h.shape[-1]).astype(BF16)
    h = _dot(_row_permutation(nseq, steps, True), h).astype(BF16)
    o_ref[...] = _dot(h, w_ref[...])


def _group_mod(mod, chunk, nseq, per_seq):
    rows = mod.reshape(MOD_ROWS, N_MOD, 1, D_MODEL)
    return rows[1:1 + nseq, chunk] if per_seq else rows[0:1, chunk]


def nm_matmul_tm(x, g, mod, w, nseq, seq_len, per_seq):
    t, d = x.shape
    n = w.shape[1]
    steps = TM_ROWS // nseq
    sc = _group_mod(mod, 1, nseq, per_seq)
    sh = _group_mod(mod, 0, nseq, per_seq)
    mod_spec = pl.BlockSpec(sc.shape, lambda i: (0, 0, 0))
    return pl.pallas_call(
        functools.partial(_nm_matmul_tm_kernel, nseq, steps),
        grid=(seq_len // steps,),
        in_specs=[pl.BlockSpec((nseq, steps, d), lambda i: (0, i, 0)),
                  pl.BlockSpec((1, d), lambda i: (0, 0)),
                  mod_spec, mod_spec,
                  pl.BlockSpec((d, n), lambda i: (0, 0))],
        out_specs=pl.BlockSpec((TM_ROWS, n), lambda i: (i, 0)),
        out_shape=jax.ShapeDtypeStruct((t, n), F32),
        compiler_params=_cparams("arbitrary"),
        name="norm_mod_proj_tm",
    )(x.reshape(nseq, seq_len, d), g.reshape(1, d), sc, sh, w)


def _proj_res_tm_kernel(nseq, steps, y_ref, w_ref, x_ref, g_ref, ng_ref, sc2_ref, sh2_ref, rw_ref, rb_ref,
                        o_ref, route_ref):
    y = _dot(_row_permutation(nseq, steps, False), y_ref[...].astype(BF16)).astype(BF16)
    acc = _dot(y, w_ref[...])
    x = x_ref[...] + g_ref[...] * acc.reshape(nseq, steps, acc.shape[-1])
    o_ref[...] = x
    h = _normmod(x, ng_ref[...], sc2_ref[...], sh2_ref[...]).reshape(nseq * steps, x.shape[-1])
    route_ref[0] = _route_record(h, rw_ref[...], rb_ref[...])


def proj_residual_tm(y, w, x, mod, norm2_g, router_w, router_b, nseq, seq_len, per_seq):
    t, d = x.shape
    steps = TM_ROWS // nseq
    n_steps = seq_len // steps
    g1 = _group_mod(mod, 2, nseq, per_seq)
    sc2 = _group_mod(mod, 4, nseq, per_seq)
    sh2 = _group_mod(mod, 3, nseq, per_seq)
    mod_spec = pl.BlockSpec(g1.shape, lambda i: (0, 0, 0))
    out, route = pl.pallas_call(
        functools.partial(_proj_res_tm_kernel, nseq, steps),
        grid=(n_steps,),
        in_specs=[pl.BlockSpec((TM_ROWS, y.shape[1]), lambda i: (i, 0)),
                  pl.BlockSpec(w.shape, lambda i: (0, 0)),
                  pl.BlockSpec((nseq, steps, d), lambda i: (0, i, 0)),
                  mod_spec,
                  pl.BlockSpec((1, d), lambda i: (0, 0)),
                  mod_spec, mod_spec,
                  pl.BlockSpec((N_EXPERTS, d), lambda i: (0, 0)),
                  pl.BlockSpec((N_EXPERTS, 1), lambda i: (0, 0))],
        out_specs=[pl.BlockSpec((nseq, steps, d), lambda i: (0, i, 0)),
                   pl.BlockSpec((1, ROUTE_ROWS, TM_ROWS), lambda i: (i, 0, 0))],
        out_shape=[jax.ShapeDtypeStruct((nseq, seq_len, d), F32),
                   jax.ShapeDtypeStruct((n_steps, ROUTE_ROWS, TM_ROWS), F32)],
        compiler_params=_cparams("arbitrary"),
        name="proj_residual_tm_route",
    )(y, w, x.reshape(nseq, seq_len, d), g1, norm2_g.reshape(1, d), sc2, sh2, router_w.T,
      router_b.reshape(N_EXPERTS, 1))
    route = route.reshape(n_steps, ROUTE_ROWS, nseq, steps).transpose(1, 2, 0, 3).reshape(ROUTE_ROWS, t)
    return out.reshape(t, d), route


def _ctx_attn_kernel(q_ref, k_ref, v_ref, o_ref, nk_ref, nv_ref):
    scale = HEAD_DIM ** -0.5
    per_tile = LANES // HEAD_DIM
    lane = lax.broadcasted_iota(jnp.int32, (1, LANES), 1)
    for hp in range(NA_HEADS // per_tile):
        sl = slice(hp * LANES, (hp + 1) * LANES)
        q, k, v = q_ref[:, sl] * scale, k_ref[:, sl], v_ref[:, sl]
        kb, vb = k.astype(BF16), v.astype(BF16)
        out = None
        for j in range(per_tile):
            h = hp * per_tile + j
            nk_ref[0, h] = k[:, j * HEAD_DIM:(j + 1) * HEAD_DIM]
            nv_ref[0, h] = v[:, j * HEAD_DIM:(j + 1) * HEAD_DIM]
            mine = lane // HEAD_DIM == j
            s = _dot_nt(jnp.where(mine, q, 0.0).astype(BF16), kb)
            p = jnp.exp(s - jnp.max(s, axis=-1, keepdims=True))
            o = _dot(p.astype(BF16), vb) / jnp.sum(p, axis=-1, keepdims=True)
            out = o if out is None else jnp.where(mine, o, out)
        o_ref[:, sl] = out.astype(o_ref.dtype)


def context_attention(u, nseq, seq_len):
    t = u.shape[0]
    kv_shape = jax.ShapeDtypeStruct((nseq, NA_HEADS, seq_len, HEAD_DIM), F32)
    kv_spec = pl.BlockSpec((1, NA_HEADS, seq_len, HEAD_DIM), lambda b: (b, 0, 0, 0))
    return pl.pallas_call(
        _ctx_attn_kernel,
        grid=(nseq,),
        in_specs=[pl.BlockSpec((seq_len, D_A), lambda b: (b, 0)),
                  pl.BlockSpec((seq_len, D_A), lambda b: (b, 1)),
                  pl.BlockSpec((seq_len, D_A), lambda b: (b, 2))],
        out_specs=[pl.BlockSpec((seq_len, D_A), lambda b: (b, 0)), kv_spec, kv_spec],
        out_shape=[jax.ShapeDtypeStruct((t, D_A), BF16), kv_shape, kv_shape],
        compiler_params=_cparams("arbitrary"),
        name="context_attention",
    )(u, u, u)


N_DR = 2 * WIN_ROWS - 1
N_DC = 2 * WIN_COLS - 1


def _na_col_tables():
    cols = np.arange(GRID_W)
    col_start = np.clip(cols - WIN_COLS // 2, 0, GRID_W - WIN_COLS)
    col_in = (cols[None, :] >= col_start[:, None]) & (cols[None, :] < col_start[:, None] + WIN_COLS)
    dc = np.clip(cols[None, :] - cols[:, None], 1 - WIN_COLS, WIN_COLS - 1) + WIN_COLS - 1
    onehot = (dc.reshape(1, -1) == np.arange(32)[:, None]).astype(np.float32)
    return onehot, col_in.reshape(1, -1).astype(np.float32)


def _na_bias_kernel(r_ref, e_ref, m_ref, o_ref):
    t = jnp.dot(r_ref[...], e_ref[...], precision=lax.Precision.HIGHEST, preferred_element_type=F32)
    o_ref[...] = jnp.where(m_ref[...] > 0.0, t, NEG_INF)


def na_bias_table(rpb):
    onehot, col_in = _na_col_tables()
    n_rows = NA_HEADS * N_DR
    r = jnp.zeros((LANES, 32), F32).at[:n_rows, :N_DC].set(rpb.reshape(n_rows, N_DC).astype(F32))
    t = pl.pallas_call(
        _na_bias_kernel,
        out_shape=jax.ShapeDtypeStruct((LANES, GRID_W * GRID_W), F32),
        name="na_bias_table",
    )(r, jnp.asarray(onehot), jnp.asarray(col_in))
    t = t[:n_rows].reshape(NA_HEADS, N_DR, GRID_W, GRID_W)
    return jnp.concatenate([t[:, :-1], t[:, 1:]], axis=-1)


def _na_kernel(rows, q_ref, k_ref, v_ref, ck_ref, cv_ref, bias_ref, o_ref,
               q_s, k_s, v_s, ck_s, cv_s, s_s, p_s, den_s, o_s):
    scale = HEAD_DIM ** -0.5
    n_lat = WIN_ROWS * GRID_W
    per_tile = LANES // HEAD_DIM
    n_pairs = NA_HEADS // per_tile
    lane = lax.broadcasted_iota(jnp.int32, (1, LANES), 1)
    for hp in range(n_pairs):
        sl = slice(hp * LANES, (hp + 1) * LANES)
        q_s[hp] = (q_ref[:, sl] * scale).astype(BF16)
        k_s[hp] = k_ref[:, sl].astype(BF16)
        v_s[hp] = v_ref[:, sl].astype(BF16)
        heads = range(hp * per_tile, (hp + 1) * per_tile)
        ck_s[hp] = jnp.concatenate([ck_ref[0, h] for h in heads], axis=1).astype(BF16)
        cv_s[hp] = jnp.concatenate([cv_ref[0, h] for h in heads], axis=1).astype(BF16)

    def window(r):
        start = min(max(r - WIN_ROWS // 2, 0), rows - WIN_ROWS)
        return start, start - r + WIN_ROWS - 1

    def pair_body(hp, carry):
        for j in range(per_tile):
            h = hp * per_tile + j
            mine = lane // HEAD_DIM == j
            for r in range(rows):
                start, off = window(r)
                rs = slice(r * GRID_W, (r + 1) * GRID_W)
                q = jnp.where(mine, q_s[hp, rs, :], 0.0).astype(BF16)
                bias = jnp.concatenate([bias_ref[h, off + 2 * i] for i in range(WIN_ROWS // 2)], axis=1)
                s_s[rs, 0:n_lat] = _dot_nt(q, k_s[hp, start * GRID_W:start * GRID_W + n_lat, :]) + bias
                s_s[rs, n_lat:] = _dot_nt(q, ck_s[hp])
            for r in range(rows):
                rs = slice(r * GRID_W, (r + 1) * GRID_W)
                s = s_s[rs, :]
                p = jnp.exp(s - jnp.max(s, axis=-1, keepdims=True))
                den_s[rs, :] = jnp.sum(p, axis=-1, keepdims=True)
                p_s[rs, :] = p.astype(BF16)
            for r in range(rows):
                start, _ = window(r)
                rs = slice(r * GRID_W, (r + 1) * GRID_W)
                o = (_dot(p_s[rs, 0:n_lat], v_s[hp, start * GRID_W:start * GRID_W + n_lat, :])
                     + _dot(p_s[rs, n_lat:], cv_s[hp])) / den_s[rs, :]
                o_s[hp, rs, :] = o if j == 0 else jnp.where(mine, o, o_s[hp, rs, :])
        return carry

    lax.fori_loop(0, n_pairs, pair_body, 0)
    for hp in range(n_pairs):
        o_ref[:, hp * LANES:(hp + 1) * LANES] = o_s[hp].astype(o_ref.dtype)


def neighbourhood_attention(u, ctx_k, ctx_v, rpb, nseq, seq_len):
    t = u.shape[0]
    rows = seq_len // GRID_W
    assert rows >= WIN_ROWS and WIN_ROWS % 2 == 0
    past = ctx_k.shape[2]
    bias = na_bias_table(rpb)
    ctx_spec = pl.BlockSpec((1, NA_HEADS, past, HEAD_DIM), lambda b: (b, 0, 0, 0))
    return pl.pallas_call(
        functools.partial(_na_kernel, rows),
        grid=(nseq,),
        in_specs=[pl.BlockSpec((seq_len, D_A), lambda b: (b, 0)),
                  pl.BlockSpec((seq_len, D_A), lambda b: (b, 1)),
                  pl.BlockSpec((seq_len, D_A), lambda b: (b, 2)),
                  ctx_spec, ctx_spec,
                  pl.BlockSpec(bias.shape, lambda b: (0, 0, 0, 0))],
        out_specs=pl.BlockSpec((seq_len, D_A), lambda b: (b, 0)),
        out_shape=jax.ShapeDtypeStruct((t, D_A), BF16),
        scratch_shapes=[pltpu.VMEM((D_A // LANES, seq_len, LANES), BF16)] * 3
        + [pltpu.VMEM((D_A // LANES, past, LANES), BF16)] * 2
        + [pltpu.VMEM((seq_len, WIN_ROWS * GRID_W + past), F32),
           pltpu.VMEM((seq_len, WIN_ROWS * GRID_W + past), BF16),
           pltpu.VMEM((seq_len, 1), F32),
           pltpu.VMEM((D_A // LANES, seq_len, LANES), F32)],
        compiler_params=_cparams("arbitrary"),
        name="neighbourhood_attention",
    )(u, u, u, ctx_k, ctx_v, bias)


def _dft_tables(seq_len):
    n = 2 * seq_len
    f = np.arange(seq_len, dtype=np.int64)
    ang = (np.outer(f, f) % n).astype(np.float64) * (math.pi / seq_len)
    cos, sin = np.cos(ang), np.sin(ang)
    alt = np.where(f % 2 == 0, 1.0, -1.0)
    s_fwd = -sin
    s_fwd[0, :] = alt
    fwd = np.concatenate([cos, s_fwd], axis=0)
    wf = np.where(f == 0, 1.0, 2.0) / n
    ci = cos.T * wf[None, :]
    si = -sin.T * wf[None, :]
    si[:, 0] = alt / n
    inv = np.concatenate([ci, si], axis=1)
    return fwd.astype(np.float32), inv.astype(np.float32)


def _hyena_feats(seq_len):
    t = np.linspace(0.0, 1.0, seq_len, dtype=np.float32)[:, None]
    w = (2.0 * math.pi * np.arange(seq_len, dtype=np.float32)[:, None] / seq_len).astype(np.float32)
    f = np.linspace(1e-4, HY_BANDS - 1, HY_BANDS, dtype=np.float32)[None, :]
    z = np.concatenate([t, np.cos(f * w), -np.sin(f * w)], axis=-1).astype(np.float32)
    max_decay = math.log(HY_DECAY_TARGET) / HY_FAST_PCT
    min_decay = math.log(HY_DECAY_TARGET) / HY_SLOW_PCT
    deltas = np.abs(np.linspace(min_decay, max_decay, D_B, dtype=np.float32))[None, :]
    return z, t, deltas


def _hy_filter_kernel(seq_len, z_ref, t_ref, dl_ref, w1_ref, b1_ref, w2_ref, b2_ref, w3_ref, fr_ref,
                      d_ref, fwd_ref, g_ref):
    hp = lax.Precision.HIGHEST
    h = jnp.sin(fr_ref[0:1, :] * (jnp.dot(z_ref[...], w1_ref[...], precision=hp) + b1_ref[...]))
    h = jnp.sin(fr_ref[1:2, :] * (jnp.dot(h, w2_ref[...], precision=hp) + b2_ref[...]))
    h = jnp.dot(h, w3_ref[...], precision=hp)
    decay = jnp.exp(-t_ref[...] * dl_ref[...])
    row0 = lax.broadcasted_iota(jnp.int32, (seq_len, D_B), 0) == 0
    sums, diffs = [], []
    for n in range(HY_ORDER):
        hf = h[:, (2 * n) * D_B:(2 * n + 1) * D_B] * decay
        hb = h[:, (2 * n + 1) * D_B:(2 * n + 2) * D_B] * decay
        gp = jnp.where(row0, hf + hb + d_ref[n:n + 1, :], hf)
        gm = jnp.where(row0, 0.0, hb)
        sums.append(gp + gm)
        diffs.append(gp - gm)
    rhs = jnp.concatenate(sums + diffs, axis=1).astype(BF16)
    spec = _dot(fwd_ref[...], rhs)
    for n in range(HY_ORDER):
        a = spec[:, n * D_B:(n + 1) * D_B]
        b = spec[:, (HY_ORDER + n) * D_B:(HY_ORDER + n + 1) * D_B]
        g_ref[n, 0:seq_len, :] = a[0:seq_len]
        g_ref[n, seq_len:, :] = jnp.where(row0, a[seq_len:], b[seq_len:])


def hyena_spectrum(seq_len, w1, b1, w2, b2, w3, freq, d, fwd):
    z, t, deltas = _hyena_feats(seq_len)
    return pl.pallas_call(
        functools.partial(_hy_filter_kernel, seq_len),
        out_shape=jax.ShapeDtypeStruct((HY_ORDER, 2 * seq_len, D_B), F32),
        compiler_params=pltpu.CompilerParams(vmem_limit_bytes=VMEM_LIMIT),
        name="hyena_spectrum",
    )(jnp.asarray(z), jnp.asarray(t), jnp.asarray(deltas), w1, b1.reshape(1, -1), w2, b2.reshape(1, -1),
      w3, freq, d, fwd)


def _hyena_kernel(seq_len, u_ref, sw_ref, sb_ref, g_ref, fwd_ref, inv_ref, o_ref):
    u = u_ref[...]
    t_idx = lax.broadcasted_iota(jnp.int32, u.shape, 0)
    prev = jnp.where(t_idx == 0, 0.0, pltpu.roll(u, 1, axis=0))
    nxt = jnp.where(t_idx == seq_len - 1, 0.0, pltpu.roll(u, seq_len - 1, axis=0))
    u = prev * sw_ref[0:1, :] + u * sw_ref[1:2, :] + nxt * sw_ref[2:3, :] + sb_ref[...]
    row0 = lax.broadcasted_iota(jnp.int32, (seq_len, D_B), 0) == 0
    z = u[:, 0:D_B]
    for n in range(HY_ORDER):
        spec = _dot(fwd_ref[...], z.astype(BF16))
        ure, uim = spec[0:seq_len], spec[seq_len:]
        gre, gim = g_ref[n, 0:seq_len, :], g_ref[n, seq_len:, :]
        pim = uim * gim
        yre = ure * gre - jnp.where(row0, 0.0, pim)
        yim = jnp.where(row0, pim, ure * gim + uim * gre)
        y = jnp.concatenate([yre, yim], axis=0).astype(BF16)
        z = u[:, (n + 1) * D_B:(n + 2) * D_B] * _dot(inv_ref[...], y)
    o_ref[...] = z.astype(o_ref.dtype)


def hyena_mixer(u, nseq, seq_len, short_w, short_b, spectrum, fwd, inv):
    t = u.shape[0]
    width = (HY_ORDER + 1) * D_B
    col_block = (3 * D_A) // width
    assert col_block * width == 3 * D_A
    return pl.pallas_call(
        functools.partial(_hyena_kernel, seq_len),
        grid=(nseq,),
        in_specs=[pl.BlockSpec((seq_len, width), lambda b: (b, col_block)),
                  pl.BlockSpec(short_w.shape, lambda b: (0, 0)),
                  pl.BlockSpec((1, width), lambda b: (0, 0)),
                  pl.BlockSpec(spectrum.shape, lambda b: (0, 0, 0)),
                  pl.BlockSpec(fwd.shape, lambda b: (0, 0)),
                  pl.BlockSpec(inv.shape, lambda b: (0, 0))],
        out_specs=pl.BlockSpec((seq_len, D_B), lambda b: (b, 0)),
        out_shape=jax.ShapeDtypeStruct((t, D_B), BF16),
        compiler_params=_cparams("arbitrary"),
        name="hyena_mixer",
    )(u, short_w, short_b.reshape(1, width), spectrum, fwd, inv)


RG_CB = LANES
RG_CHUNK = 512
TM_ROWS = 256


def _rglru_kernel(nseq, seq_len, gate_ref, xr_ref, cw_ref, cb_ref, wg_ref, bg_ref, lam_ref, h0_ref,
                  y_ref, fin_ref, xp_ref, a_f, b_f, a_b, b_b):
    t_tot = nseq * seq_len
    c = RG_CB
    pad = 2 * nseq
    xp_ref[0:pad, :] = jnp.zeros((pad, c), F32)
    xp_ref[pad + t_tot:, :] = jnp.zeros((pad, c), F32)
    xp_ref[pad:pad + t_tot, :] = xr_ref[...]
    nl = -lam_ref[...]
    sp = jnp.maximum(nl, 0.0) + jnp.log1p(jnp.exp(-jnp.abs(nl)))
    k2 = (-0.5 * RG_C * math.log2(math.e)) * sp

    def gate_chunk(ci, carry):
        r0 = pl.multiple_of(ci * RG_CHUNK, RG_CHUNK)
        xc = xp_ref[pl.ds(r0, RG_CHUNK), :] * cw_ref[0:1, :]
        for j in range(1, cw_ref.shape[0]):
            xc = xc + xp_ref[pl.ds(r0 + j * nseq, RG_CHUNK), :] * cw_ref[j:j + 1, :]
        xc = xc + cb_ref[...]
        gts = _dot(xc.astype(BF16), wg_ref[0]) + bg_ref[...]
        x_half = 0.5 * xc
        for d, (a_ref, b_ref) in enumerate(((a_f, b_f), (a_b, b_b))):
            t_r = jnp.tanh(gts[:, (2 * d) * c:(2 * d + 1) * c])
            t_i = jnp.tanh(gts[:, (2 * d + 1) * c:(2 * d + 2) * c])
            a = jnp.exp2(t_r * k2[d:d + 1, :] + k2[d:d + 1, :])
            a_ref[pl.ds(r0, RG_CHUNK), :] = a
            y = 1.0 - a * a
            root = jnp.where(y > 0.0, y * lax.rsqrt(y), 0.0)
            b_ref[pl.ds(r0, RG_CHUNK), :] = root * ((t_i + 1.0) * x_half)
        return carry

    lax.fori_loop(0, t_tot // RG_CHUNK, gate_chunk, 0)

    def scan_step(t, carry):
        hf, hb = carry
        rows_f = pl.ds(pl.multiple_of(t * nseq, nseq), nseq)
        rows_b = pl.ds(pl.multiple_of((seq_len - 1 - t) * nseq, nseq), nseq)
        hf = a_f[rows_f, :] * hf + b_f[rows_f, :]
        hb = a_b[rows_b, :] * hb + b_b[rows_b, :]
        b_f[rows_f, :] = hf
        b_b[rows_b, :] = hb
        return hf, hb

    hf, hb = lax.fori_loop(0, seq_len, scan_step, (h0_ref[0], h0_ref[1]), unroll=8)
    fin_ref[0] = hf
    fin_ref[1] = hb

    def out_chunk(ci, carry):
        rs = pl.ds(pl.multiple_of(ci * RG_CHUNK, RG_CHUNK), RG_CHUNK)
        y_ref[rs, :] = ((b_f[rs, :] + b_b[rs, :]) * jax.nn.gelu(gate_ref[rs, :])).astype(y_ref.dtype)
        return carry

    lax.fori_loop(0, t_tot // RG_CHUNK, out_chunk, 0)


def _rg_gate_weights(wa, wx):
    per_step = RG_CB // RG_BLOCK
    steps = D_RNN // RG_CB
    mats = []
    for d in range(2):
        for w in (wa[d], wx[d]):
            w = w.reshape(steps, per_step, RG_BLOCK, RG_BLOCK)
            eye = jnp.eye(per_step, dtype=w.dtype)
            m = jnp.einsum('spde,pq->spdqe', w, eye).reshape(steps, RG_CB, RG_CB)
            mats.append(m)
    return (0.5 * jnp.concatenate(mats, axis=-1)).astype(BF16)


def rglru_block(u, nseq, seq_len, conv_w, conv_b, wa, ba, wx, bx, lam, h0):
    t = u.shape[0]
    c = RG_CB
    steps = D_RNN // c
    wg = _rg_gate_weights(wa, wx)
    bg = jnp.stack([ba[0], bx[0], ba[1], bx[1]], axis=0).reshape(4, steps, c)
    bg = 0.5 * bg.transpose(1, 0, 2).reshape(steps, 1, 4 * c)
    y, fin = pl.pallas_call(
        functools.partial(_rglru_kernel, nseq, seq_len),
        grid=(steps,),
        in_specs=[pl.BlockSpec((t, c), lambda j: (0, j)),
                  pl.BlockSpec((t, c), lambda j: (0, steps + j)),
                  pl.BlockSpec((conv_w.shape[0], c), lambda j: (0, j)),
                  pl.BlockSpec((1, c), lambda j: (0, j)),
                  pl.BlockSpec((1, c, 4 * c), lambda j: (j, 0, 0)),
                  pl.BlockSpec((None, 1, 4 * c), lambda j: (j, 0, 0)),
                  pl.BlockSpec((2, c), lambda j: (0, j)),
                  pl.BlockSpec((2, nseq, c), lambda j: (0, 0, j))],
        out_specs=[pl.BlockSpec((t, c), lambda j: (0, j)),
                   pl.BlockSpec((2, nseq, c), lambda j: (0, 0, j))],
        out_shape=[jax.ShapeDtypeStruct((t, D_RNN), BF16),
                   jax.ShapeDtypeStruct((2, nseq, D_RNN), F32)],
        scratch_shapes=[pltpu.VMEM((t + 4 * nseq, c), F32)] + [pltpu.VMEM((t, c), F32)] * 4,
        compiler_params=_cparams("arbitrary"),
        name="rglru_block",
    )(u, u, conv_w, conv_b.reshape(1, -1), wg, bg, lam, h0)
    return y, fin


def _route_record(h, w, rb):
    h_hi = h.astype(BF16)
    h_lo = (h - h_hi.astype(F32)).astype(BF16)
    w_hi = w.astype(BF16)
    w_lo = (w - w_hi.astype(F32)).astype(BF16)
    logits = _dot_nt(w_hi, h_hi) + (_dot_nt(w_lo, h_hi) + _dot_nt(w_hi, h_lo))
    scores = jax.nn.sigmoid(logits)
    sel = scores + rb
    row = [sel[e:e + 1, :] for e in range(N_EXPERTS)]
    gs = []
    for g in range(N_GROUPS):
        r = row[g * EXPERTS_PER_GROUP:(g + 1) * EXPERTS_PER_GROUP]
        best_pair = None
        for i in range(EXPERTS_PER_GROUP):
            for j in range(i + 1, EXPERTS_PER_GROUP):
                s = r[i] + r[j]
                best_pair = s if best_pair is None else jnp.maximum(best_pair, s)
        gs.append(best_pair)
    best = jnp.zeros_like(gs[0], dtype=jnp.int32)
    top = gs[0]
    for g in range(1, N_GROUPS):
        better = gs[g] > top
        best = jnp.where(better, g, best)
        top = jnp.where(better, gs[g], top)
    picked = []
    for e in range(N_EXPERTS):
        g = e // EXPERTS_PER_GROUP
        rank = jnp.zeros_like(best)
        for o in range(g * EXPERTS_PER_GROUP, (g + 1) * EXPERTS_PER_GROUP):
            if o == e:
                continue
            ahead = (row[o] > row[e]) | ((row[o] == row[e]) & (o < e))
            rank = rank + ahead.astype(jnp.int32)
        picked.append((best == g) & (rank < 2))
    den = jnp.zeros_like(gs[0])
    for e in range(N_EXPERTS):
        den = den + jnp.where(picked[e], scores[e:e + 1, :], 0.0)
    gate = [jnp.where(picked[e], scores[e:e + 1, :] / den, 0.0) for e in range(N_EXPERTS)]
    cls = jnp.zeros_like(den)
    w_a = jnp.zeros_like(den)
    w_b = jnp.zeros_like(den)
    for g in range(N_GROUPS):
        for pi, (a, b) in enumerate(MOE_PAIRS):
            ea, eb = g * EXPERTS_PER_GROUP + a, g * EXPERTS_PER_GROUP + b
            both = picked[ea] & picked[eb]
            cls = jnp.where(both, float(g * len(MOE_PAIRS) + pi), cls)
            w_a = jnp.where(both, gate[ea], w_a)
            w_b = jnp.where(both, gate[eb], w_b)
    return jnp.concatenate([cls, w_a, w_b, jnp.zeros((ROUTE_ROWS - 3, cls.shape[1]), F32)], axis=0)


MOE_PAIRS = ((0, 1), (0, 2), (0, 3), (1, 3), (1, 2), (2, 3))
N_CLS = N_GROUPS * len(MOE_PAIRS)
CLS_PAD = 32
ROUTE_ROWS = 8
MOE_TS = 256
MOE_TM = 256
MOE_STEP_TILES = 4
SLOT_BLK = 512
ROW_W = D_MODEL + LANES


def _slots_kernel(n_blk, route_ref, slot_ref, off_ref, cnt_ref):
    cid = lax.broadcasted_iota(jnp.int32, (CLS_PAD, SLOT_BLK), 0).astype(F32)

    def members(j):
        cls = route_ref[0:1, pl.ds(pl.multiple_of(j * SLOT_BLK, SLOT_BLK), SLOT_BLK)]
        return (cid == cls).astype(F32)

    def count(j, cnt):
        return cnt + jnp.sum(members(j), axis=1, keepdims=True)

    cnt = lax.fori_loop(0, n_blk, count, jnp.zeros((CLS_PAD, 1), F32))
    cnt = jnp.broadcast_to(cnt, (CLS_PAD, LANES))
    padded = jnp.ceil(cnt * (1.0 / MOE_TS)) * MOE_TS
    r = lax.broadcasted_iota(jnp.int32, (CLS_PAD, CLS_PAD), 0)
    c = lax.broadcasted_iota(jnp.int32, (CLS_PAD, CLS_PAD), 1)
    off = jnp.dot((c < r).astype(F32), padded, precision=lax.Precision.HIGHEST, preferred_element_type=F32)
    off_ref[...] = off
    cnt_ref[...] = cnt
    tr = lax.broadcasted_iota(jnp.int32, (SLOT_BLK, SLOT_BLK), 0)
    tc = lax.broadcasted_iota(jnp.int32, (SLOT_BLK, SLOT_BLK), 1)
    earlier = (tr < tc).astype(BF16)

    def assign(j, base):
        member = members(j)
        rank = _dot(member.astype(BF16), earlier)
        slot = jnp.sum(member * (rank + base), axis=0, keepdims=True)
        slot_ref[0:1, pl.ds(pl.multiple_of(j * SLOT_BLK, SLOT_BLK), SLOT_BLK)] = slot.astype(jnp.int32)
        return base + jnp.sum(member, axis=1, keepdims=True)

    lax.fori_loop(0, n_blk, assign, off[:, 0:1])


def moe_slots(route):
    t = route.shape[1]
    stat = jax.ShapeDtypeStruct((CLS_PAD, LANES), F32)
    return pl.pallas_call(
        functools.partial(_slots_kernel, t // SLOT_BLK),
        out_shape=[jax.ShapeDtypeStruct((1, t), jnp.int32), stat, stat],
        compiler_params=pltpu.CompilerParams(vmem_limit_bytes=VMEM_LIMIT),
        name="moe_slots",
    )(route)


def _tile_maps(off, cnt, n_tiles):
    off = off[:N_CLS, 0].astype(jnp.int32)
    cnt = cnt[:N_CLS, 0].astype(jnp.int32)
    ends = off + ((cnt + MOE_TS - 1) // MOE_TS) * MOE_TS
    n_used = ends[-1] // MOE_TS
    k = jnp.arange(n_tiles, dtype=jnp.int32)
    tix = jnp.minimum(k, n_used - 1)
    cls = jnp.sum((tix[:, None] * MOE_TS >= ends[None, :]).astype(jnp.int32), axis=1)
    pair = jnp.asarray(MOE_PAIRS, jnp.int32)
    grp = (cls // len(MOE_PAIRS)) * EXPERTS_PER_GROUP
    ea = grp + pair[cls % len(MOE_PAIRS), 0]
    eb = grp + pair[cls % len(MOE_PAIRS), 1]
    n = jnp.int32(n_tiles)

    def slot_plan(e):
        chg = jnp.concatenate([jnp.ones((1,), jnp.int32), (e[1:] != e[:-1]).astype(jnp.int32)])
        at = jnp.where(chg == 1, k, n)
        nxt_at = jnp.concatenate([lax.cummin(at[::-1])[::-1][1:], n.reshape(1)])
        more = (nxt_at < n).astype(jnp.int32)
        nxt = e[jnp.minimum(nxt_at, n - 1)]
        par = (jnp.cumsum(chg) - 1) % 2
        return chg, nxt, more, par.astype(jnp.int32)

    plan_a, plan_b = slot_plan(ea), slot_plan(eb)
    chg, nxt, more, par = (jnp.stack([pa, pb]) for pa, pb in zip(plan_a, plan_b))
    return ea, eb, chg, nxt, more, par, n_used.reshape(1)


def _dispatch_kernel(n_steps, slots_ref, x_ref, g_ref, sc_ref, sh_ref, rt_ref, hs_in, hs_out, rowbuf, sem):
    del hs_in
    i = pl.program_id(0)
    cur = i % 2

    def wait_rows(s):
        pltpu.make_async_copy(rowbuf.at[s], rowbuf.at[s], sem.at[s]).wait()

    @pl.when(i >= 2)
    def _():
        wait_rows(cur)

    rowbuf[cur, :, 0:D_MODEL] = _normmod(x_ref[...], g_ref[...], sc_ref[0], sh_ref[0])
    rowbuf[cur, :, D_MODEL:ROW_W] = jnp.concatenate(
        [rt_ref[...], jnp.zeros((MOE_TM, LANES - ROUTE_ROWS), F32)], axis=1)

    for s in range(2):
        @pl.when(cur == s)
        def _():
            for r in range(MOE_TM):
                dst = slots_ref[i * MOE_TM + r]
                pltpu.make_async_copy(rowbuf.at[s, r], hs_out.at[dst], sem.at[s]).start()

    @pl.when(i == n_steps - 1)
    def _():
        wait_rows(cur)
        if n_steps >= 2:
            wait_rows(1 - cur)


def moe_dispatch(x, g, mod, route_t, slots, hs, seq_len, per_seq):
    t, d = x.shape
    n_steps = t // MOE_TM
    grid_spec = pltpu.PrefetchScalarGridSpec(
        num_scalar_prefetch=1,
        grid=(n_steps,),
        in_specs=[pl.BlockSpec((MOE_TM, d), lambda i, s: (i, 0)),
                  pl.BlockSpec((1, d), lambda i, s: (0, 0)),
                  _mod_spec(4, MOE_TM, seq_len, per_seq),
                  _mod_spec(3, MOE_TM, seq_len, per_seq),
                  pl.BlockSpec((MOE_TM, ROUTE_ROWS), lambda i, s: (i, 0)),
                  pl.BlockSpec(memory_space=pl.ANY)],
        out_specs=pl.BlockSpec(memory_space=pl.ANY),
        scratch_shapes=[pltpu.VMEM((2, MOE_TM, ROW_W), F32), pltpu.SemaphoreType.DMA((2,))],
    )
    return pl.pallas_call(
        functools.partial(_dispatch_kernel, n_steps),
        grid_spec=grid_spec,
        out_shape=jax.ShapeDtypeStruct(hs.shape, F32),
        input_output_aliases={6: 0},
        compiler_params=_cparams("arbitrary"),
        name="moe_dispatch",
    )(slots, x, g.reshape(1, d), mod, mod, route_t, hs)


def _experts_kernel(layer, ea_ref, eb_ref, chg_ref, nxt_ref, more_ref, par_ref, nused_ref,
                    hs_ref, wg_hbm, wu_hbm, wd_hbm, ys_ref, fg, fu, fd, bg, bu, bd, sem):
    def weight_copies(slot, expert, par):
        return [pltpu.make_async_copy(src.at[layer, expert], dst.at[slot, par], sem.at[slot, par])
                for src, dst in ((wg_hbm, fg), (wu_hbm, fu), (wd_hbm, fd))]

    def tile(k, rows):
        @pl.when(k < nused_ref[0])
        def _():
            for slot, e_ref in enumerate((ea_ref, eb_ref)):
                @pl.when(chg_ref[slot, k] == 1)
                def _():
                    par = par_ref[slot, k]

                    @pl.when(k == 0)
                    def _():
                        for cp in weight_copies(slot, e_ref[0], par):
                            cp.start()

                    for cp in weight_copies(slot, e_ref[k], par):
                        cp.wait()
                    bg[slot] = fg[slot, par].astype(BF16)
                    bu[slot] = fu[slot, par].astype(BF16)
                    bd[slot] = fd[slot, par].astype(BF16)

                    @pl.when(more_ref[slot, k] == 1)
                    def _():
                        for cp in weight_copies(slot, nxt_ref[slot, k], 1 - par):
                            cp.start()

            h = hs_ref[rows, 0:D_MODEL].astype(BF16)

            def ffn(slot):
                hid = _dot(h, bg[slot])
                up = _dot(h, bu[slot])
                w = hs_ref[rows, D_MODEL + 1 + slot:D_MODEL + 2 + slot]
                act = (hid * _sigmoid(hid)) * up * w
                return _dot(act.astype(BF16), bd[slot])

            ys_ref[rows, :] = ffn(0) + ffn(1)

        @pl.when(k >= nused_ref[0])
        def _():
            ys_ref[rows, :] = jnp.zeros((MOE_TS, ys_ref.shape[1]), ys_ref.dtype)

    for j in range(MOE_STEP_TILES):
        tile(pl.program_id(0) * MOE_STEP_TILES + j, slice(j * MOE_TS, (j + 1) * MOE_TS))


def moe_experts(hs, maps, layer, w_gate, w_up, w_down):
    n_tiles = hs.shape[0] // MOE_TS
    d = D_MODEL

    grid_spec = pltpu.PrefetchScalarGridSpec(
        num_scalar_prefetch=7,
        grid=(n_tiles // MOE_STEP_TILES,),
        in_specs=[pl.BlockSpec((MOE_STEP_TILES * MOE_TS, ROW_W), lambda k, *_: (k, 0)),
                  pl.BlockSpec(memory_space=pl.ANY), pl.BlockSpec(memory_space=pl.ANY),
                  pl.BlockSpec(memory_space=pl.ANY)],
        out_specs=pl.BlockSpec((MOE_STEP_TILES * MOE_TS, d), lambda k, *_: (k, 0)),
        scratch_shapes=[pltpu.VMEM((2, 2, d, D_EXPERT), F32), pltpu.VMEM((2, 2, d, D_EXPERT), F32),
                        pltpu.VMEM((2, 2, D_EXPERT, d), F32),
                        pltpu.VMEM((2, d, D_EXPERT), BF16), pltpu.VMEM((2, d, D_EXPERT), BF16),
                        pltpu.VMEM((2, D_EXPERT, d), BF16),
                        pltpu.SemaphoreType.DMA((2, 2))],
    )
    return pl.pallas_call(
        functools.partial(_experts_kernel, layer),
        grid_spec=grid_spec,
        out_shape=jax.ShapeDtypeStruct((hs.shape[0], d), F32),
        compiler_params=_cparams("arbitrary"),
        name="moe_experts",
    )(*maps, hs, w_gate, w_up, w_down)


def _combine_kernel(final, n_steps, slots_ref, x_ref, g2_ref, fg_ref, ys_hbm, o_ref, gbuf, sem):
    i = pl.program_id(0)
    cur = i % 2

    def issue_tile(tile, s):
        for r in range(MOE_TM):
            src = slots_ref[tile * MOE_TM + r]
            pltpu.make_async_copy(ys_hbm.at[src], gbuf.at[s, r], sem.at[s]).start()

    @pl.when(i == 0)
    def _():
        issue_tile(0, 0)

    for s in range(2):
        @pl.when((i + 1 < n_steps) & (1 - cur == s))
        def _():
            issue_tile(i + 1, s)

    pltpu.make_async_copy(gbuf.at[cur], gbuf.at[cur], sem.at[cur]).wait()
    y = x_ref[...] + g2_ref[0] * gbuf[cur]
    if final:
        ms = jnp.mean(y * y, axis=-1, keepdims=True)
        y = y * lax.rsqrt(ms + EPS) * fg_ref[...]
    o_ref[...] = y


def moe_combine(x, mod, slots, ys, final_g, final, seq_len, per_seq):
    t, d = x.shape
    n_steps = t // MOE_TM
    grid_spec = pltpu.PrefetchScalarGridSpec(
        num_scalar_prefetch=1,
        grid=(n_steps,),
        in_specs=[pl.BlockSpec((MOE_TM, d), lambda i, s: (i, 0)),
                  _mod_spec(5, MOE_TM, seq_len, per_seq),
                  pl.BlockSpec((1, d), lambda i, s: (0, 0)),
                  pl.BlockSpec(memory_space=pl.ANY)],
        out_specs=pl.BlockSpec((MOE_TM, d), lambda i, s: (i, 0)),
        scratch_shapes=[pltpu.VMEM((2, MOE_TM, d), F32), pltpu.SemaphoreType.DMA((2,))],
    )
    return pl.pallas_call(
        functools.partial(_combine_kernel, final, n_steps),
        grid_spec=grid_spec,
        out_shape=jax.ShapeDtypeStruct((t, d), F32),
        compiler_params=_cparams("arbitrary"),
        name="moe_combine",
    )(slots, x, mod, final_g.reshape(1, d), ys)


def _combine_nm_tm_kernel(nseq, steps, seq_len, n_steps, slots_ref, x_ref, g2_ref, ys_hbm, g_ref, sc_ref,
                          sh_ref, w_ref, x2_ref, u_ref, gbuf, sem):
    i = pl.program_id(0)
    cur = i % 2

    def issue_tile(tile, s):
        for sq in range(nseq):
            for t in range(steps):
                src = slots_ref[sq * seq_len + tile * steps + t]
                pltpu.make_async_copy(ys_hbm.at[src], gbuf.at[s, sq * steps + t], sem.at[s]).start()

    @pl.when(i == 0)
    def _():
        issue_tile(0, 0)

    for s in range(2):
        @pl.when((i + 1 < n_steps) & (1 - cur == s))
        def _():
            issue_tile(i + 1, s)

    pltpu.make_async_copy(gbuf.at[cur], gbuf.at[cur], sem.at[cur]).wait()
    d = x_ref.shape[-1]
    y = x_ref[...] + g2_ref[...] * gbuf[cur].reshape(nseq, steps, d)
    x2_ref[...] = y
    h = _normmod(y, g_ref[...], sc_ref[...], sh_ref[...])
    h = h.reshape(nseq * steps, d).astype(BF16)
    h = _dot(_row_permutation(nseq, steps, True), h).astype(BF16)
    u_ref[...] = _dot(h, w_ref[...])


def combine_nm_matmul_tm(x, mod_prev, slots, ys, g, mod, w, nseq, seq_len, per_seq):
    t, d = x.shape
    n = w.shape[1]
    steps = TM_ROWS // nseq
    n_steps = seq_len // steps
    g2 = _group_mod(mod_prev, 5, nseq, per_seq)
    sc = _group_mod(mod, 1, nseq, per_seq)
    sh = _group_mod(mod, 0, nseq, per_seq)
    mod_spec = pl.BlockSpec(sc.shape, lambda i, s: (0, 0, 0))
    x_spec = pl.BlockSpec((nseq, steps, d), lambda i, s: (0, i, 0))
    grid_spec = pltpu.PrefetchScalarGridSpec(
        num_scalar_prefetch=1,
        grid=(n_steps,),
        in_specs=[x_spec, mod_spec, pl.BlockSpec(memory_space=pl.ANY),
                  pl.BlockSpec((1, d), lambda i, s: (0, 0)), mod_spec, mod_spec,
                  pl.BlockSpec((d, n), lambda i, s: (0, 0))],
        out_specs=[x_spec, pl.BlockSpec((TM_ROWS, n), lambda i, s: (i, 0))],
        scratch_shapes=[pltpu.VMEM((2, TM_ROWS, d), F32), pltpu.SemaphoreType.DMA((2,))],
    )
    x2, u = pl.pallas_call(
        functools.partial(_combine_nm_tm_kernel, nseq, steps, seq_len, n_steps),
        grid_spec=grid_spec,
        out_shape=[jax.ShapeDtypeStruct((nseq, seq_len, d), F32), jax.ShapeDtypeStruct((t, n), F32)],
        compiler_params=_cparams("arbitrary"),
        name="moe_combine_norm_mod_proj_tm",
    )(slots, x.reshape(nseq, seq_len, d), g2, ys, g.reshape(1, d), sc, sh, w)
    return x2.reshape(t, d), u


def moe_block(xs, routes, mods_l, per_seqs, seq_lens, layer, p, final, defer_combine, hs):
    g = p['norm_g'][layer, 1]
    slots, off, cnt = moe_slots(jnp.concatenate(routes, axis=1))
    t_all = slots.shape[1]
    n_tiles = -(-(t_all // MOE_TS + N_CLS) // MOE_STEP_TILES) * MOE_STEP_TILES
    maps = _tile_maps(off, cnt, n_tiles)
    if hs is None:
        hs = jnp.zeros((n_tiles * MOE_TS, ROW_W), F32)
    bounds = np.cumsum([0] + [x.shape[0] for x in xs])
    group_slots = [slots[0, bounds[i]:bounds[i + 1]] for i in range(len(xs))]
    for x, r, s, sl, ps in zip(xs, routes, group_slots, seq_lens, per_seqs):
        hs = moe_dispatch(x, g, mods_l, r.T, s, hs, sl, ps)
    ys = moe_experts(hs, maps, layer, p['moe_w_gate'], p['moe_w_up'], p['moe_w_down'])
    if defer_combine:
        return [(x, s, ys) for x, s in zip(xs, group_slots)], hs
    return [moe_combine(x, mods_l, s, ys, p['final_g'], final, sl, ps)
            for x, s, sl, ps in zip(xs, group_slots, seq_lens, per_seqs)], hs


def _mixer(x, group, l, mod, p, hy_tables, mod_prev):
    per_seq, nseq, seq_len = group['per_seq'], group['nseq'], group['seq_len']
    extras = None
    deferred = isinstance(x, tuple)
    if deferred and l % 2 == 0:
        x = moe_combine(x[0], mod_prev, x[1], x[2], p['final_g'], False, seq_len, per_seq)
        deferred = False
    if l % 2 == 0:
        e = l // 2
        u = nm_matmul(x, p['norm_g'][l, 0], mod, p['a_in_w'][e].astype(BF16), seq_len, per_seq)
        if group['ctx_k'] is None:
            attn, nk, nv = context_attention(u, nseq, seq_len)
            extras = (nk, nv)
        else:
            attn = neighbourhood_attention(u, group['ctx_k'][:, e], group['ctx_v'][:, e], p['na_rpb'][e],
                                           nseq, seq_len)
        fwd, inv = hy_tables[seq_len]
        spectrum = hyena_spectrum(seq_len, p['hy_w1'][e], p['hy_b1'][e], p['hy_w2'][e], p['hy_b2'][e],
                                  p['hy_w3'][e], p['hy_freq'][e], p['hy_d'][e], fwd)
        hy = hyena_mixer(u, nseq, seq_len, p['hy_short_w'][e], p['hy_short_b'][e], spectrum, fwd, inv)
        w_out = p['a_out_w'][e].astype(BF16)
        x, route = proj_residual([attn, hy], [w_out[:D_A], w_out[D_A:]], x, mod, p['norm_g'][l, 1],
                                 p['router_w'], p['router_b'], seq_len, per_seq)
    else:
        o = l // 2
        w_in = p['c_in_w'][o].astype(BF16)
        if deferred:
            x, u = combine_nm_matmul_tm(x[0], mod_prev, x[1], x[2], p['norm_g'][l, 0], mod, w_in,
                                        nseq, seq_len, per_seq)
        else:
            u = nm_matmul_tm(x, p['norm_g'][l, 0], mod, w_in, nseq, seq_len, per_seq)
        y, extras = rglru_block(u, nseq, seq_len, p['rg_conv_w'][o], p['rg_conv_b'][o], p['rg_wa'][o],
                                p['rg_ba'][o], p['rg_wx'][o], p['rg_bx'][o], p['rg_lam'][o], group['h0'][o])
        x, route = proj_residual_tm(y, p['c_out_w'][o].astype(BF16), x, mod, p['norm_g'][l, 1],
                                    p['router_w'], p['router_b'], nseq, seq_len, per_seq)
    return x, extras, route


def kernel(x_prompt, x_sample, cache_k, cache_v, state_h, c, c_ctx, norm_g, ada_w, ada_b, final_g, a_in_w, a_out_w, na_rpb, hy_short_w, hy_short_b, hy_w1, hy_b1, hy_w2, hy_b2, hy_w3, hy_freq, hy_d, c_in_w, c_out_w, rg_conv_w, rg_conv_b, rg_wa, rg_ba, rg_wx, rg_bx, rg_lam, router_w, router_b, moe_w_gate, moe_w_up, moe_w_down):
    p = dict(norm_g=norm_g, final_g=final_g, a_in_w=a_in_w, a_out_w=a_out_w, na_rpb=na_rpb,
             hy_short_w=hy_short_w, hy_short_b=hy_short_b, hy_w1=hy_w1, hy_b1=hy_b1, hy_w2=hy_w2,
             hy_b2=hy_b2, hy_w3=hy_w3, hy_freq=hy_freq, hy_d=hy_d, c_in_w=c_in_w, c_out_w=c_out_w,
             rg_conv_w=rg_conv_w, rg_conv_b=rg_conv_b, rg_wa=rg_wa, rg_ba=rg_ba, rg_wx=rg_wx, rg_bx=rg_bx,
             rg_lam=rg_lam, router_w=router_w, router_b=router_b, moe_w_gate=moe_w_gate,
             moe_w_up=moe_w_up, moe_w_down=moe_w_down)
    batch, seq, d = x_prompt.shape
    dec_batch, dec_seq, _ = x_sample.shape
    n_odd = DEPTH // 2
    assert 1 + dec_batch <= MOD_ROWS

    cond = jnp.concatenate([c_ctx[None, :], c, jnp.zeros((MOD_ROWS - 1 - dec_batch, d), F32)], axis=0)
    m = modulation(cond, ada_w, ada_b)
    mods = [m[l].reshape(MOD_ROWS * N_MOD, 1, d) for l in range(DEPTH)]

    tables = {}
    for sl in (seq, dec_seq):
        fwd, inv = _dft_tables(sl)
        tables[sl] = (jnp.asarray(fwd).astype(BF16), jnp.asarray(inv).astype(BF16))

    groups = [
        dict(per_seq=False, nseq=batch, seq_len=seq, ctx_k=None, ctx_v=None,
             h0=[jnp.zeros((2, batch, D_RNN), F32)] * n_odd),
        dict(per_seq=True, nseq=dec_batch, seq_len=dec_seq, ctx_k=cache_k, ctx_v=cache_v,
             h0=[state_h[:, o].transpose(1, 0, 2) for o in range(n_odd)]),
    ]
    xs = [x_prompt.reshape(batch * seq, d), x_sample.reshape(dec_batch * dec_seq, d)]
    k_list, v_list, h_list = [], [], []
    hs = None
    for l in range(DEPTH):
        mixed = [_mixer(x, grp, l, mods[l], p, tables, mods[l - 1] if l else None)
                 for x, grp in zip(xs, groups)]
        if l % 2 == 0:
            k_list.append(mixed[0][1][0])
            v_list.append(mixed[0][1][1])
        else:
            h_list.append(mixed[0][1].transpose(1, 0, 2))
        xs, hs = moe_block([mx[0] for mx in mixed], [mx[2] for mx in mixed], mods[l],
                           [grp['per_seq'] for grp in groups], [grp['seq_len'] for grp in groups],
                           l, p, l == DEPTH - 1, l < DEPTH - 1, hs)
    new_k = jnp.stack(k_list, axis=1)
    new_v = jnp.stack(v_list, axis=1)
    new_h = jnp.stack(h_list, axis=1)
    return (xs[0].reshape(batch, seq, d), xs[1].reshape(dec_batch, dec_seq, d), new_k, new_v, new_h)
```

```python
import functools
import math

import numpy as np
import jax
import jax.numpy as jnp
from jax import lax
from jax.experimental import pallas as pl
from jax.experimental.pallas import tpu as pltpu

F32 = jnp.float32
BF16 = jnp.bfloat16

D_MODEL = 1024
DEPTH = 2
GRID_W = 64
EPS = 1e-6
NEG_INF = -1e30
NA_HEADS = 8
HEAD_DIM = 64
D_A = NA_HEADS * HEAD_DIM
WIN_ROWS = 8
WIN_COLS = 16
D_B = D_MODEL - D_A
HY_ORDER = 2
HY_EMB = 33
HY_BANDS = (HY_EMB - 1) // 2
HY_FFN = 64
HY_DECAY_TARGET = 1e-2
HY_FAST_PCT = 0.3
HY_SLOW_PCT = 1.5
D_RNN = D_MODEL
RG_BLOCK = 64
RG_C = 8.0
N_EXPERTS = 16
N_GROUPS = 4
EXPERTS_PER_GROUP = N_EXPERTS // N_GROUPS
D_EXPERT = 512

LANES = 128
VMEM_LIMIT = 56 * 1024 * 1024
N_MOD = 6
MOD_ROWS = 16


def _cparams(*sem):
    return pltpu.CompilerParams(dimension_semantics=sem, vmem_limit_bytes=VMEM_LIMIT)


def _dot(a, b):
    return jnp.dot(a, b, preferred_element_type=F32)


def _dot_nt(a, b):
    return lax.dot_general(a, b, (((1,), (1,)), ((), ())), preferred_element_type=F32)


def _sigmoid(x):
    return 0.5 * jnp.tanh(0.5 * x) + 0.5


def _normmod(x, g, sc, sh):
    ms = jnp.mean(x * x, axis=-1, keepdims=True)
    return (x * lax.rsqrt(ms + EPS) * g) * (1.0 + sc) + sh


def _mod_spec(chunk, tm, seq_len, per_seq):
    if per_seq:
        return pl.BlockSpec((1, 1, D_MODEL), lambda i, *_: ((1 + (i * tm) // seq_len) * N_MOD + chunk, 0, 0))
    return pl.BlockSpec((1, 1, D_MODEL), lambda i, *_: (chunk, 0, 0))


def _mod_kernel(c_ref, w_ref, b_ref, o_ref):
    s = c_ref[...]
    s = s * jax.nn.sigmoid(s)
    o_ref[0] = _dot(s.astype(BF16), w_ref[0].astype(BF16)) + b_ref[0]


def modulation(cond, ada_w, ada_b):
    tn = 1536
    n = ada_w.shape[-1]
    return pl.pallas_call(
        _mod_kernel,
        grid=(DEPTH, n // tn),
        in_specs=[pl.BlockSpec((MOD_ROWS, D_MODEL), lambda l, j: (0, 0)),
                  pl.BlockSpec((1, D_MODEL, tn), lambda l, j: (l, 0, j)),
                  pl.BlockSpec((1, 1, tn), lambda l, j: (l, 0, j))],
        out_specs=pl.BlockSpec((1, MOD_ROWS, tn), lambda l, j: (l, 0, j)),
        out_shape=jax.ShapeDtypeStruct((DEPTH, MOD_ROWS, n), F32),
        compiler_params=_cparams("arbitrary", "arbitrary"),
        name="modulation",
    )(cond, ada_w, ada_b.reshape(DEPTH, 1, n))


def _nm_matmul_kernel(x_ref, g_ref, sc_ref, sh_ref, w_ref, o_ref):
    h = _normmod(x_ref[...], g_ref[...], sc_ref[0], sh_ref[0])
    o_ref[...] = _dot(h.astype(BF16), w_ref[...])


def nm_matmul(x, g, mod, w, seq_len, per_seq, tm=512):
    t, d = x.shape
    n = w.shape[1]
    return pl.pallas_call(
        _nm_matmul_kernel,
        grid=(t // tm,),
        in_specs=[pl.BlockSpec((tm, d), lambda i: (i, 0)),
                  pl.BlockSpec((1, d), lambda i: (0, 0)),
                  _mod_spec(1, tm, seq_len, per_seq),
                  _mod_spec(0, tm, seq_len, per_seq),
                  pl.BlockSpec((d, n), lambda i: (0, 0))],
        out_specs=pl.BlockSpec((tm, n), lambda i: (i, 0)),
        out_shape=jax.ShapeDtypeStruct((t, n), F32),
        compiler_params=_cparams("arbitrary"),
        name="norm_mod_proj",
    )(x, g.reshape(1, d), mod, mod, w)


def _proj_res_kernel(n_act, *refs):
    acts = refs[:n_act]
    ws = refs[n_act:2 * n_act]
    x_ref, g_ref, ng_ref, sc2_ref, sh2_ref, rw_ref, rb_ref, o_ref, route_ref = refs[2 * n_act:]
    acc = _dot(acts[0][...].astype(BF16), ws[0][...])
    for a, w in zip(acts[1:], ws[1:]):
        acc += _dot(a[...].astype(BF16), w[...])
    x = x_ref[...] + g_ref[0] * acc
    o_ref[...] = x
    route_ref[...] = _route_record(_normmod(x, ng_ref[...], sc2_ref[0], sh2_ref[0]), rw_ref[...], rb_ref[...])


def proj_residual(acts, ws, x, mod, norm2_g, router_w, router_b, seq_len, per_seq, tm=512):
    t, d = x.shape
    in_specs = [pl.BlockSpec((tm, a.shape[1]), lambda i: (i, 0)) for a in acts]
    in_specs += [pl.BlockSpec(w.shape, lambda i: (0, 0)) for w in ws]
    in_specs += [pl.BlockSpec((tm, d), lambda i: (i, 0)), _mod_spec(2, tm, seq_len, per_seq),
                 pl.BlockSpec((1, d), lambda i: (0, 0)),
                 _mod_spec(4, tm, seq_len, per_seq), _mod_spec(3, tm, seq_len, per_seq),
                 pl.BlockSpec((N_EXPERTS, d), lambda i: (0, 0)),
                 pl.BlockSpec((N_EXPERTS, 1), lambda i: (0, 0))]
    return pl.pallas_call(
        functools.partial(_proj_res_kernel, len(acts)),
        grid=(t // tm,),
        in_specs=in_specs,
        out_specs=[pl.BlockSpec((tm, d), lambda i: (i, 0)), pl.BlockSpec((ROUTE_ROWS, tm), lambda i: (0, i))],
        out_shape=[jax.ShapeDtypeStruct((t, d), F32), jax.ShapeDtypeStruct((ROUTE_ROWS, t), F32)],
        compiler_params=_cparams("arbitrary"),
        name="proj_residual_route",
    )(*acts, *ws, x, mod, norm2_g.reshape(1, d), mod, mod, router_w.T, router_b.reshape(N_EXPERTS, 1))


def _row_permutation(nseq, steps, to_time_major):
    n = nseq * steps
    i = lax.broadcasted_iota(jnp.int32, (n, n), 0)
    j = lax.broadcasted_iota(jnp.int32, (n, n), 1)
    if to_time_major:
        src = (i % nseq) * steps + i // nseq
    else:
        src = (i % steps) * nseq + i // steps
    return (j == src).astype(BF16)


def _nm_matmul_tm_kernel(nseq, steps, x_ref, g_ref, sc_ref, sh_ref, w_ref, o_ref):
    h = _normmod(x_ref[...], g_ref[...], sc_ref[...], sh_ref[...])
    h = h.reshape(nseq * steps, h.shape[-1]).astype(BF16)
    h = _dot(_row_permutation(nseq, steps, True), h).astype(BF16)
    o_ref[...] = _dot(h, w_ref[...])


def _group_mod(mod, chunk, nseq, per_seq):
    rows = mod.reshape(MOD_ROWS, N_MOD, 1, D_MODEL)
    return rows[1:1 + nseq, chunk] if per_seq else rows[0:1, chunk]


def nm_matmul_tm(x, g, mod, w, nseq, seq_len, per_seq):
    t, d = x.shape
    n = w.shape[1]
    steps = TM_ROWS // nseq
    sc = _group_mod(mod, 1, nseq, per_seq)
    sh = _group_mod(mod, 0, nseq, per_seq)
    mod_spec = pl.BlockSpec(sc.shape, lambda i: (0, 0, 0))
    return pl.pallas_call(
        functools.partial(_nm_matmul_tm_kernel, nseq, steps),
        grid=(seq_len // steps,),
        in_specs=[pl.BlockSpec((nseq, steps, d), lambda i: (0, i, 0)),
                  pl.BlockSpec((1, d), lambda i: (0, 0)),
                  mod_spec, mod_spec,
                  pl.BlockSpec((d, n), lambda i: (0, 0))],
        out_specs=pl.BlockSpec((TM_ROWS, n), lambda i: (i, 0)),
        out_shape=jax.ShapeDtypeStruct((t, n), F32),
        compiler_params=_cparams("arbitrary"),
        name="norm_mod_proj_tm",
    )(x.reshape(nseq, seq_len, d), g.reshape(1, d), sc, sh, w)


def _proj_res_tm_kernel(nseq, steps, y_ref, w_ref, x_ref, g_ref, ng_ref, sc2_ref, sh2_ref, rw_ref, rb_ref,
                        o_ref, route_ref):
    y = _dot(_row_permutation(nseq, steps, False), y_ref[...].astype(BF16)).astype(BF16)
    acc = _dot(y, w_ref[...])
    x = x_ref[...] + g_ref[...] * acc.reshape(nseq, steps, acc.shape[-1])
    o_ref[...] = x
    h = _normmod(x, ng_ref[...], sc2_ref[...], sh2_ref[...]).reshape(nseq * steps, x.shape[-1])
    route_ref[0] = _route_record(h, rw_ref[...], rb_ref[...])


def proj_residual_tm(y, w, x, mod, norm2_g, router_w, router_b, nseq, seq_len, per_seq):
    t, d = x.shape
    steps = TM_ROUTE_ROWS // nseq
    n_steps = seq_len // steps
    g1 = _group_mod(mod, 2, nseq, per_seq)
    sc2 = _group_mod(mod, 4, nseq, per_seq)
    sh2 = _group_mod(mod, 3, nseq, per_seq)
    mod_spec = pl.BlockSpec(g1.shape, lambda i: (0, 0, 0))
    out, route = pl.pallas_call(
        functools.partial(_proj_res_tm_kernel, nseq, steps),
        grid=(n_steps,),
        in_specs=[pl.BlockSpec((TM_ROUTE_ROWS, y.shape[1]), lambda i: (i, 0)),
                  pl.BlockSpec(w.shape, lambda i: (0, 0)),
                  pl.BlockSpec((nseq, steps, d), lambda i: (0, i, 0)),
                  mod_spec,
                  pl.BlockSpec((1, d), lambda i: (0, 0)),
                  mod_spec, mod_spec,
                  pl.BlockSpec((N_EXPERTS, d), lambda i: (0, 0)),
                  pl.BlockSpec((N_EXPERTS, 1), lambda i: (0, 0))],
        out_specs=[pl.BlockSpec((nseq, steps, d), lambda i: (0, i, 0)),
                   pl.BlockSpec((1, ROUTE_ROWS, TM_ROUTE_ROWS), lambda i: (i, 0, 0))],
        out_shape=[jax.ShapeDtypeStruct((nseq, seq_len, d), F32),
                   jax.ShapeDtypeStruct((n_steps, ROUTE_ROWS, TM_ROUTE_ROWS), F32)],
        compiler_params=_cparams("arbitrary"),
        name="proj_residual_tm_route",
    )(y, w, x.reshape(nseq, seq_len, d), g1, norm2_g.reshape(1, d), sc2, sh2, router_w.T,
      router_b.reshape(N_EXPERTS, 1))
    route = route.reshape(n_steps, ROUTE_ROWS, nseq, steps).transpose(1, 2, 0, 3).reshape(ROUTE_ROWS, t)
    return out.reshape(t, d), route


def _ctx_attn_kernel(q_ref, k_ref, v_ref, o_ref, nk_ref, nv_ref):
    scale = HEAD_DIM ** -0.5
    per_tile = LANES // HEAD_DIM
    lane = lax.broadcasted_iota(jnp.int32, (1, LANES), 1)
    for hp in range(NA_HEADS // per_tile):
        sl = slice(hp * LANES, (hp + 1) * LANES)
        q, k, v = q_ref[:, sl] * scale, k_ref[:, sl], v_ref[:, sl]
        kb, vb = k.astype(BF16), v.astype(BF16)
        out = None
        for j in range(per_tile):
            h = hp * per_tile + j
            nk_ref[0, h] = k[:, j * HEAD_DIM:(j + 1) * HEAD_DIM]
            nv_ref[0, h] = v[:, j * HEAD_DIM:(j + 1) * HEAD_DIM]
            mine = lane // HEAD_DIM == j
            s = _dot_nt(jnp.where(mine, q, 0.0).astype(BF16), kb)
            p = jnp.exp(s - jnp.max(s, axis=-1, keepdims=True))
            o = _dot(p.astype(BF16), vb) / jnp.sum(p, axis=-1, keepdims=True)
            out = o if out is None else jnp.where(mine, o, out)
        o_ref[:, sl] = out.astype(o_ref.dtype)


def context_attention(u, nseq, seq_len):
    t = u.shape[0]
    kv_shape = jax.ShapeDtypeStruct((nseq, NA_HEADS, seq_len, HEAD_DIM), F32)
    kv_spec = pl.BlockSpec((1, NA_HEADS, seq_len, HEAD_DIM), lambda b: (b, 0, 0, 0))
    return pl.pallas_call(
        _ctx_attn_kernel,
        grid=(nseq,),
        in_specs=[pl.BlockSpec((seq_len, D_A), lambda b: (b, 0)),
                  pl.BlockSpec((seq_len, D_A), lambda b: (b, 1)),
                  pl.BlockSpec((seq_len, D_A), lambda b: (b, 2))],
        out_specs=[pl.BlockSpec((seq_len, D_A), lambda b: (b, 0)), kv_spec, kv_spec],
        out_shape=[jax.ShapeDtypeStruct((t, D_A), BF16), kv_shape, kv_shape],
        compiler_params=_cparams("arbitrary"),
        name="context_attention",
    )(u, u, u)


N_DR = 2 * WIN_ROWS - 1
N_DC = 2 * WIN_COLS - 1


def _na_col_tables():
    cols = np.arange(GRID_W)
    col_start = np.clip(cols - WIN_COLS // 2, 0, GRID_W - WIN_COLS)
    col_in = (cols[None, :] >= col_start[:, None]) & (cols[None, :] < col_start[:, None] + WIN_COLS)
    dc = np.clip(cols[None, :] - cols[:, None], 1 - WIN_COLS, WIN_COLS - 1) + WIN_COLS - 1
    onehot = (dc.reshape(1, -1) == np.arange(32)[:, None]).astype(np.float32)
    return onehot, col_in.reshape(1, -1).astype(np.float32)


def _na_bias_kernel(r_ref, e_ref, m_ref, o_ref):
    t = jnp.dot(r_ref[...], e_ref[...], precision=lax.Precision.HIGHEST, preferred_element_type=F32)
    o_ref[...] = jnp.where(m_ref[...] > 0.0, t, NEG_INF)


def na_bias_table(rpb):
    onehot, col_in = _na_col_tables()
    n_rows = NA_HEADS * N_DR
    r = jnp.zeros((LANES, 32), F32).at[:n_rows, :N_DC].set(rpb.reshape(n_rows, N_DC).astype(F32))
    t = pl.pallas_call(
        _na_bias_kernel,
        out_shape=jax.ShapeDtypeStruct((LANES, GRID_W * GRID_W), F32),
        name="na_bias_table",
    )(r, jnp.asarray(onehot), jnp.asarray(col_in))
    t = t[:n_rows].reshape(NA_HEADS, N_DR, GRID_W, GRID_W)
    return jnp.concatenate([t[:, :-1], t[:, 1:]], axis=-1)


def _na_kernel(rows, q_ref, k_ref, v_ref, ck_ref, cv_ref, bias_ref, o_ref,
               q_s, k_s, v_s, ck_s, cv_s, s_s, p_s, den_s, o_s):
    scale = HEAD_DIM ** -0.5
    n_lat = WIN_ROWS * GRID_W
    per_tile = LANES // HEAD_DIM
    n_pairs = NA_HEADS // per_tile
    lane = lax.broadcasted_iota(jnp.int32, (1, LANES), 1)
    for hp in range(n_pairs):
        sl = slice(hp * LANES, (hp + 1) * LANES)
        q_s[hp] = (q_ref[:, sl] * scale).astype(BF16)
        k_s[hp] = k_ref[:, sl].astype(BF16)
        v_s[hp] = v_ref[:, sl].astype(BF16)
        heads = range(hp * per_tile, (hp + 1) * per_tile)
        ck_s[hp] = jnp.concatenate([ck_ref[0, h] for h in heads], axis=1).astype(BF16)
        cv_s[hp] = jnp.concatenate([cv_ref[0, h] for h in heads], axis=1).astype(BF16)

    def window(r):
        start = min(max(r - WIN_ROWS // 2, 0), rows - WIN_ROWS)
        return start, start - r + WIN_ROWS - 1

    def pair_body(hp, carry):
        for j in range(per_tile):
            h = hp * per_tile + j
            mine = lane // HEAD_DIM == j
            for r in range(rows):
                start, off = window(r)
                rs = slice(r * GRID_W, (r + 1) * GRID_W)
                q = jnp.where(mine, q_s[hp, rs, :], 0.0).astype(BF16)
                bias = jnp.concatenate([bias_ref[h, off + 2 * i] for i in range(WIN_ROWS // 2)], axis=1)
                s_s[rs, 0:n_lat] = _dot_nt(q, k_s[hp, start * GRID_W:start * GRID_W + n_lat, :]) + bias
                s_s[rs, n_lat:] = _dot_nt(q, ck_s[hp])
            for r in range(rows):
                rs = slice(r * GRID_W, (r + 1) * GRID_W)
                s = s_s[rs, :]
                p = jnp.exp(s - jnp.max(s, axis=-1, keepdims=True))
                den_s[rs, :] = jnp.sum(p, axis=-1, keepdims=True)
                p_s[rs, :] = p.astype(BF16)
            for r in range(rows):
                start, _ = window(r)
                rs = slice(r * GRID_W, (r + 1) * GRID_W)
                o = (_dot(p_s[rs, 0:n_lat], v_s[hp, start * GRID_W:start * GRID_W + n_lat, :])
                     + _dot(p_s[rs, n_lat:], cv_s[hp])) / den_s[rs, :]
                o_s[hp, rs, :] = o if j == 0 else jnp.where(mine, o, o_s[hp, rs, :])
        return carry

    lax.fori_loop(0, n_pairs, pair_body, 0)
    for hp in range(n_pairs):
        o_ref[:, hp * LANES:(hp + 1) * LANES] = o_s[hp].astype(o_ref.dtype)


def neighbourhood_attention(u, ctx_k, ctx_v, rpb, nseq, seq_len):
    t = u.shape[0]
    rows = seq_len // GRID_W
    assert rows >= WIN_ROWS and WIN_ROWS % 2 == 0
    past = ctx_k.shape[2]
    bias = na_bias_table(rpb)
    ctx_spec = pl.BlockSpec((1, NA_HEADS, past, HEAD_DIM), lambda b: (b, 0, 0, 0))
    return pl.pallas_call(
        functools.partial(_na_kernel, rows),
        grid=(nseq,),
        in_specs=[pl.BlockSpec((seq_len, D_A), lambda b: (b, 0)),
                  pl.BlockSpec((seq_len, D_A), lambda b: (b, 1)),
                  pl.BlockSpec((seq_len, D_A), lambda b: (b, 2)),
                  ctx_spec, ctx_spec,
                  pl.BlockSpec(bias.shape, lambda b: (0, 0, 0, 0))],
        out_specs=pl.BlockSpec((seq_len, D_A), lambda b: (b, 0)),
        out_shape=jax.ShapeDtypeStruct((t, D_A), BF16),
        scratch_shapes=[pltpu.VMEM((D_A // LANES, seq_len, LANES), BF16)] * 3
        + [pltpu.VMEM((D_A // LANES, past, LANES), BF16)] * 2
        + [pltpu.VMEM((seq_len, WIN_ROWS * GRID_W + past), F32),
           pltpu.VMEM((seq_len, WIN_ROWS * GRID_W + past), BF16),
           pltpu.VMEM((seq_len, 1), F32),
           pltpu.VMEM((D_A // LANES, seq_len, LANES), F32)],
        compiler_params=_cparams("arbitrary"),
        name="neighbourhood_attention",
    )(u, u, u, ctx_k, ctx_v, bias)


def _dft_tables(seq_len):
    n = 2 * seq_len
    f = np.arange(seq_len, dtype=np.int64)
    ang = (np.outer(f, f) % n).astype(np.float64) * (math.pi / seq_len)
    cos, sin = np.cos(ang), np.sin(ang)
    alt = np.where(f % 2 == 0, 1.0, -1.0)
    s_fwd = -sin
    s_fwd[0, :] = alt
    fwd = np.concatenate([cos, s_fwd], axis=0)
    wf = np.where(f == 0, 1.0, 2.0) / n
    ci = cos.T * wf[None, :]
    si = -sin.T * wf[None, :]
    si[:, 0] = alt / n
    inv = np.concatenate([ci, si], axis=1)
    return fwd.astype(np.float32), inv.astype(np.float32)


def _hyena_feats(seq_len):
    t = np.linspace(0.0, 1.0, seq_len, dtype=np.float32)[:, None]
    w = (2.0 * math.pi * np.arange(seq_len, dtype=np.float32)[:, None] / seq_len).astype(np.float32)
    f = np.linspace(1e-4, HY_BANDS - 1, HY_BANDS, dtype=np.float32)[None, :]
    z = np.concatenate([t, np.cos(f * w), -np.sin(f * w)], axis=-1).astype(np.float32)
    max_decay = math.log(HY_DECAY_TARGET) / HY_FAST_PCT
    min_decay = math.log(HY_DECAY_TARGET) / HY_SLOW_PCT
    deltas = np.abs(np.linspace(min_decay, max_decay, D_B, dtype=np.float32))[None, :]
    return z, t, deltas


def _hy_filter_kernel(seq_len, z_ref, t_ref, dl_ref, w1_ref, b1_ref, w2_ref, b2_ref, w3_ref, fr_ref,
                      d_ref, fwd_ref, g_ref):
    hp = lax.Precision.HIGHEST
    h = jnp.sin(fr_ref[0:1, :] * (jnp.dot(z_ref[...], w1_ref[...], precision=hp) + b1_ref[...]))
    h = jnp.sin(fr_ref[1:2, :] * (jnp.dot(h, w2_ref[...], precision=hp) + b2_ref[...]))
    h = jnp.dot(h, w3_ref[...], precision=hp)
    decay = jnp.exp(-t_ref[...] * dl_ref[...])
    row0 = lax.broadcasted_iota(jnp.int32, (seq_len, D_B), 0) == 0
    sums, diffs = [], []
    for n in range(HY_ORDER):
        hf = h[:, (2 * n) * D_B:(2 * n + 1) * D_B] * decay
        hb = h[:, (2 * n + 1) * D_B:(2 * n + 2) * D_B] * decay
        gp = jnp.where(row0, hf + hb + d_ref[n:n + 1, :], hf)
        gm = jnp.where(row0, 0.0, hb)
        sums.append(gp + gm)
        diffs.append(gp - gm)
    rhs = jnp.concatenate(sums + diffs, axis=1).astype(BF16)
    spec = _dot(fwd_ref[...], rhs)
    for n in range(HY_ORDER):
        a = spec[:, n * D_B:(n + 1) * D_B]
        b = spec[:, (HY_ORDER + n) * D_B:(HY_ORDER + n + 1) * D_B]
        g_ref[n, 0:seq_len, :] = a[0:seq_len]
        g_ref[n, seq_len:, :] = jnp.where(row0, a[seq_len:], b[seq_len:])


def hyena_spectrum(seq_len, w1, b1, w2, b2, w3, freq, d, fwd):
    z, t, deltas = _hyena_feats(seq_len)
    return pl.pallas_call(
        functools.partial(_hy_filter_kernel, seq_len),
        out_shape=jax.ShapeDtypeStruct((HY_ORDER, 2 * seq_len, D_B), F32),
        compiler_params=pltpu.CompilerParams(vmem_limit_bytes=VMEM_LIMIT),
        name="hyena_spectrum",
    )(jnp.asarray(z), jnp.asarray(t), jnp.asarray(deltas), w1, b1.reshape(1, -1), w2, b2.reshape(1, -1),
      w3, freq, d, fwd)


def _hyena_kernel(seq_len, u_ref, sw_ref, sb_ref, g_ref, fwd_ref, inv_ref, o_ref):
    u = u_ref[...]
    t_idx = lax.broadcasted_iota(jnp.int32, u.shape, 0)
    prev = jnp.where(t_idx == 0, 0.0, pltpu.roll(u, 1, axis=0))
    nxt = jnp.where(t_idx == seq_len - 1, 0.0, pltpu.roll(u, seq_len - 1, axis=0))
    u = prev * sw_ref[0:1, :] + u * sw_ref[1:2, :] + nxt * sw_ref[2:3, :] + sb_ref[...]
    row0 = lax.broadcasted_iota(jnp.int32, (seq_len, D_B), 0) == 0
    z = u[:, 0:D_B]
    for n in range(HY_ORDER):
        spec = _dot(fwd_ref[...], z.astype(BF16))
        ure, uim = spec[0:seq_len], spec[seq_len:]
        gre, gim = g_ref[n, 0:seq_len, :], g_ref[n, seq_len:, :]
        pim = uim * gim
        yre = ure * gre - jnp.where(row0, 0.0, pim)
        yim = jnp.where(row0, pim, ure * gim + uim * gre)
        y = jnp.concatenate([yre, yim], axis=0).astype(BF16)
        z = u[:, (n + 1) * D_B:(n + 2) * D_B] * _dot(inv_ref[...], y)
    o_ref[...] = z.astype(o_ref.dtype)


def hyena_mixer(u, nseq, seq_len, short_w, short_b, spectrum, fwd, inv):
    t = u.shape[0]
    width = (HY_ORDER + 1) * D_B
    col_block = (3 * D_A) // width
    assert col_block * width == 3 * D_A
    return pl.pallas_call(
        functools.partial(_hyena_kernel, seq_len),
        grid=(nseq,),
        in_specs=[pl.BlockSpec((seq_len, width), lambda b: (b, col_block)),
                  pl.BlockSpec(short_w.shape, lambda b: (0, 0)),
                  pl.BlockSpec((1, width), lambda b: (0, 0)),
                  pl.BlockSpec(spectrum.shape, lambda b: (0, 0, 0)),
                  pl.BlockSpec(fwd.shape, lambda b: (0, 0)),
                  pl.BlockSpec(inv.shape, lambda b: (0, 0))],
        out_specs=pl.BlockSpec((seq_len, D_B), lambda b: (b, 0)),
        out_shape=jax.ShapeDtypeStruct((t, D_B), BF16),
        compiler_params=_cparams("arbitrary"),
        name="hyena_mixer",
    )(u, short_w, short_b.reshape(1, width), spectrum, fwd, inv)


RG_CB = LANES
RG_CHUNK = 512
TM_ROWS = 256
TM_ROUTE_ROWS = 512


def _rglru_kernel(nseq, seq_len, gate_ref, xr_ref, cw_ref, cb_ref, wg_ref, bg_ref, lam_ref, h0_ref,
                  y_ref, fin_ref, xp_ref, a_f, b_f, a_b, b_b):
    t_tot = nseq * seq_len
    c = RG_CB
    pad = 2 * nseq
    xp_ref[0:pad, :] = jnp.zeros((pad, c), F32)
    xp_ref[pad + t_tot:, :] = jnp.zeros((pad, c), F32)
    xp_ref[pad:pad + t_tot, :] = xr_ref[...]
    nl = -lam_ref[...]
    sp = jnp.maximum(nl, 0.0) + jnp.log1p(jnp.exp(-jnp.abs(nl)))
    k2 = (-0.5 * RG_C * math.log2(math.e)) * sp

    def gate_chunk(ci, carry):
        r0 = pl.multiple_of(ci * RG_CHUNK, RG_CHUNK)
        xc = xp_ref[pl.ds(r0, RG_CHUNK), :] * cw_ref[0:1, :]
        for j in range(1, cw_ref.shape[0]):
            xc = xc + xp_ref[pl.ds(r0 + j * nseq, RG_CHUNK), :] * cw_ref[j:j + 1, :]
        xc = xc + cb_ref[...]
        gts = _dot(xc.astype(BF16), wg_ref[0]) + bg_ref[...]
        x_half = 0.5 * xc
        for d, (a_ref, b_ref) in enumerate(((a_f, b_f), (a_b, b_b))):
            t_r = jnp.tanh(gts[:, (2 * d) * c:(2 * d + 1) * c])
            t_i = jnp.tanh(gts[:, (2 * d + 1) * c:(2 * d + 2) * c])
            a = jnp.exp2(t_r * k2[d:d + 1, :] + k2[d:d + 1, :])
            a_ref[pl.ds(r0, RG_CHUNK), :] = a
            y = 1.0 - a * a
            root = jnp.where(y > 0.0, y * lax.rsqrt(y), 0.0)
            b_ref[pl.ds(r0, RG_CHUNK), :] = root * ((t_i + 1.0) * x_half)
        return carry

    lax.fori_loop(0, t_tot // RG_CHUNK, gate_chunk, 0)

    def scan_step(t, carry):
        hf, hb = carry
        rows_f = pl.ds(pl.multiple_of(t * nseq, nseq), nseq)
        rows_b = pl.ds(pl.multiple_of((seq_len - 1 - t) * nseq, nseq), nseq)
        hf = a_f[rows_f, :] * hf + b_f[rows_f, :]
        hb = a_b[rows_b, :] * hb + b_b[rows_b, :]
        b_f[rows_f, :] = hf
        b_b[rows_b, :] = hb
        return hf, hb

    hf, hb = lax.fori_loop(0, seq_len, scan_step, (h0_ref[0], h0_ref[1]), unroll=8)
    fin_ref[0] = hf
    fin_ref[1] = hb

    def out_chunk(ci, carry):
        rs = pl.ds(pl.multiple_of(ci * RG_CHUNK, RG_CHUNK), RG_CHUNK)
        y_ref[rs, :] = ((b_f[rs, :] + b_b[rs, :]) * jax.nn.gelu(gate_ref[rs, :])).astype(y_ref.dtype)
        return carry

    lax.fori_loop(0, t_tot // RG_CHUNK, out_chunk, 0)


def _rg_gate_weights(wa, wx):
    per_step = RG_CB // RG_BLOCK
    steps = D_RNN // RG_CB
    mats = []
    for d in range(2):
        for w in (wa[d], wx[d]):
            w = w.reshape(steps, per_step, RG_BLOCK, RG_BLOCK)
            eye = jnp.eye(per_step, dtype=w.dtype)
            m = jnp.einsum('spde,pq->spdqe', w, eye).reshape(steps, RG_CB, RG_CB)
            mats.append(m)
    return (0.5 * jnp.concatenate(mats, axis=-1)).astype(BF16)


def rglru_block(u, nseq, seq_len, conv_w, conv_b, wa, ba, wx, bx, lam, h0):
    t = u.shape[0]
    c = RG_CB
    steps = D_RNN // c
    wg = _rg_gate_weights(wa, wx)
    bg = jnp.stack([ba[0], bx[0], ba[1], bx[1]], axis=0).reshape(4, steps, c)
    bg = 0.5 * bg.transpose(1, 0, 2).reshape(steps, 1, 4 * c)
    y, fin = pl.pallas_call(
        functools.partial(_rglru_kernel, nseq, seq_len),
        grid=(steps,),
        in_specs=[pl.BlockSpec((t, c), lambda j: (0, j)),
                  pl.BlockSpec((t, c), lambda j: (0, steps + j)),
                  pl.BlockSpec((conv_w.shape[0], c), lambda j: (0, j)),
                  pl.BlockSpec((1, c), lambda j: (0, j)),
                  pl.BlockSpec((1, c, 4 * c), lambda j: (j, 0, 0)),
                  pl.BlockSpec((None, 1, 4 * c), lambda j: (j, 0, 0)),
                  pl.BlockSpec((2, c), lambda j: (0, j)),
                  pl.BlockSpec((2, nseq, c), lambda j: (0, 0, j))],
        out_specs=[pl.BlockSpec((t, c), lambda j: (0, j)),
                   pl.BlockSpec((2, nseq, c), lambda j: (0, 0, j))],
        out_shape=[jax.ShapeDtypeStruct((t, D_RNN), BF16),
                   jax.ShapeDtypeStruct((2, nseq, D_RNN), F32)],
        scratch_shapes=[pltpu.VMEM((t + 4 * nseq, c), F32)] + [pltpu.VMEM((t, c), F32)] * 4,
        compiler_params=_cparams("arbitrary"),
        name="rglru_block",
    )(u, u, conv_w, conv_b.reshape(1, -1), wg, bg, lam, h0)
    return y, fin


def _route_record(h, w, rb):
    h_hi = h.astype(BF16)
    h_lo = (h - h_hi.astype(F32)).astype(BF16)
    w_hi = w.astype(BF16)
    w_lo = (w - w_hi.astype(F32)).astype(BF16)
    logits = _dot_nt(w_hi, h_hi) + (_dot_nt(w_lo, h_hi) + _dot_nt(w_hi, h_lo))
    scores = jax.nn.sigmoid(logits)
    sel = scores + rb
    row = [sel[e:e + 1, :] for e in range(N_EXPERTS)]
    gs = []
    for g in range(N_GROUPS):
        r = row[g * EXPERTS_PER_GROUP:(g + 1) * EXPERTS_PER_GROUP]
        best_pair = None
        for i in range(EXPERTS_PER_GROUP):
            for j in range(i + 1, EXPERTS_PER_GROUP):
                s = r[i] + r[j]
                best_pair = s if best_pair is None else jnp.maximum(best_pair, s)
        gs.append(best_pair)
    best = jnp.zeros_like(gs[0], dtype=jnp.int32)
    top = gs[0]
    for g in range(1, N_GROUPS):
        better = gs[g] > top
        best = jnp.where(better, g, best)
        top = jnp.where(better, gs[g], top)
    picked = []
    for e in range(N_EXPERTS):
        g = e // EXPERTS_PER_GROUP
        rank = jnp.zeros_like(best)
        for o in range(g * EXPERTS_PER_GROUP, (g + 1) * EXPERTS_PER_GROUP):
            if o == e:
                continue
            ahead = (row[o] > row[e]) | ((row[o] == row[e]) & (o < e))
            rank = rank + ahead.astype(jnp.int32)
        picked.append((best == g) & (rank < 2))
    den = jnp.zeros_like(gs[0])
    for e in range(N_EXPERTS):
        den = den + jnp.where(picked[e], scores[e:e + 1, :], 0.0)
    gate = [jnp.where(picked[e], scores[e:e + 1, :] / den, 0.0) for e in range(N_EXPERTS)]
    cls = jnp.zeros_like(den)
    w_a = jnp.zeros_like(den)
    w_b = jnp.zeros_like(den)
    for g in range(N_GROUPS):
        for pi, (a, b) in enumerate(MOE_PAIRS):
            ea, eb = g * EXPERTS_PER_GROUP + a, g * EXPERTS_PER_GROUP + b
            both = picked[ea] & picked[eb]
            cls = jnp.where(both, float(g * len(MOE_PAIRS) + pi), cls)
            w_a = jnp.where(both, gate[ea], w_a)
            w_b = jnp.where(both, gate[eb], w_b)
    return jnp.concatenate([cls, w_a, w_b, jnp.zeros((ROUTE_ROWS - 3, cls.shape[1]), F32)], axis=0)


MOE_PAIRS = ((0, 1), (0, 2), (0, 3), (1, 3), (1, 2), (2, 3))
N_CLS = N_GROUPS * len(MOE_PAIRS)
CLS_PAD = 32
ROUTE_ROWS = 8
MOE_TS = 256
MOE_TM = 256
MOE_STEP_TILES = 4
SLOT_BLK = 512
ROW_W = D_MODEL + LANES


def _slots_kernel(n_blk, route_ref, slot_ref, off_ref, cnt_ref):
    cid = lax.broadcasted_iota(jnp.int32, (CLS_PAD, SLOT_BLK), 0).astype(F32)

    def members(j):
        cls = route_ref[0:1, pl.ds(pl.multiple_of(j * SLOT_BLK, SLOT_BLK), SLOT_BLK)]
        return (cid == cls).astype(F32)

    def count(j, cnt):
        return cnt + jnp.sum(members(j), axis=1, keepdims=True)

    cnt = lax.fori_loop(0, n_blk, count, jnp.zeros((CLS_PAD, 1), F32))
    cnt = jnp.broadcast_to(cnt, (CLS_PAD, LANES))
    padded = jnp.ceil(cnt * (1.0 / MOE_TS)) * MOE_TS
    r = lax.broadcasted_iota(jnp.int32, (CLS_PAD, CLS_PAD), 0)
    c = lax.broadcasted_iota(jnp.int32, (CLS_PAD, CLS_PAD), 1)
    off = jnp.dot((c < r).astype(F32), padded, precision=lax.Precision.HIGHEST, preferred_element_type=F32)
    off_ref[...] = off
    cnt_ref[...] = cnt
    tr = lax.broadcasted_iota(jnp.int32, (SLOT_BLK, SLOT_BLK), 0)
    tc = lax.broadcasted_iota(jnp.int32, (SLOT_BLK, SLOT_BLK), 1)
    earlier = (tr < tc).astype(BF16)

    def assign(j, base):
        member = members(j)
        rank = _dot(member.astype(BF16), earlier)
        slot = jnp.sum(member * (rank + base), axis=0, keepdims=True)
        slot_ref[0:1, pl.ds(pl.multiple_of(j * SLOT_BLK, SLOT_BLK), SLOT_BLK)] = slot.astype(jnp.int32)
        return base + jnp.sum(member, axis=1, keepdims=True)

    lax.fori_loop(0, n_blk, assign, off[:, 0:1])


def moe_slots(route):
    t = route.shape[1]
    stat = jax.ShapeDtypeStruct((CLS_PAD, LANES), F32)
    return pl.pallas_call(
        functools.partial(_slots_kernel, t // SLOT_BLK),
        out_shape=[jax.ShapeDtypeStruct((1, t), jnp.int32), stat, stat],
        compiler_params=pltpu.CompilerParams(vmem_limit_bytes=VMEM_LIMIT),
        name="moe_slots",
    )(route)


def _tile_maps(off, cnt, n_tiles):
    off = off[:N_CLS, 0].astype(jnp.int32)
    cnt = cnt[:N_CLS, 0].astype(jnp.int32)
    ends = off + ((cnt + MOE_TS - 1) // MOE_TS) * MOE_TS
    n_used = ends[-1] // MOE_TS
    k = jnp.arange(n_tiles, dtype=jnp.int32)
    tix = jnp.minimum(k, n_used - 1)
    cls = jnp.sum((tix[:, None] * MOE_TS >= ends[None, :]).astype(jnp.int32), axis=1)
    pair = jnp.asarray(MOE_PAIRS, jnp.int32)
    grp = (cls // len(MOE_PAIRS)) * EXPERTS_PER_GROUP
    ea = grp + pair[cls % len(MOE_PAIRS), 0]
    eb = grp + pair[cls % len(MOE_PAIRS), 1]
    n = jnp.int32(n_tiles)

    def slot_plan(e):
        chg = jnp.concatenate([jnp.ones((1,), jnp.int32), (e[1:] != e[:-1]).astype(jnp.int32)])
        at = jnp.where(chg == 1, k, n)
        nxt_at = jnp.concatenate([lax.cummin(at[::-1])[::-1][1:], n.reshape(1)])
        more = (nxt_at < n).astype(jnp.int32)
        nxt = e[jnp.minimum(nxt_at, n - 1)]
        par = (jnp.cumsum(chg) - 1) % 2
        return chg, nxt, more, par.astype(jnp.int32)

    plan_a, plan_b = slot_plan(ea), slot_plan(eb)
    chg, nxt, more, par = (jnp.stack([pa, pb]) for pa, pb in zip(plan_a, plan_b))
    return ea, eb, chg, nxt, more, par, n_used.reshape(1)


def _dispatch_kernel(n_steps, slots_ref, x_ref, g_ref, sc_ref, sh_ref, rt_ref, hs_in, hs_out, rowbuf, sem):
    del hs_in
    i = pl.program_id(0)
    cur = i % 2

    def wait_rows(s):
        pltpu.make_async_copy(rowbuf.at[s], rowbuf.at[s], sem.at[s]).wait()

    @pl.when(i >= 2)
    def _():
        wait_rows(cur)

    rowbuf[cur, :, 0:D_MODEL] = _normmod(x_ref[...], g_ref[...], sc_ref[0], sh_ref[0])
    rowbuf[cur, :, D_MODEL:ROW_W] = jnp.concatenate(
        [rt_ref[...], jnp.zeros((MOE_TM, LANES - ROUTE_ROWS), F32)], axis=1)

    for s in range(2):
        @pl.when(cur == s)
        def _():
            for r in range(MOE_TM):
                dst = slots_ref[i * MOE_TM + r]
                pltpu.make_async_copy(rowbuf.at[s, r], hs_out.at[dst], sem.at[s]).start()

    @pl.when(i == n_steps - 1)
    def _():
        wait_rows(cur)
        if n_steps >= 2:
            wait_rows(1 - cur)


def moe_dispatch(x, g, mod, route_t, slots, hs, seq_len, per_seq):
    t, d = x.shape
    n_steps = t // MOE_TM
    grid_spec = pltpu.PrefetchScalarGridSpec(
        num_scalar_prefetch=1,
        grid=(n_steps,),
        in_specs=[pl.BlockSpec((MOE_TM, d), lambda i, s: (i, 0)),
                  pl.BlockSpec((1, d), lambda i, s: (0, 0)),
                  _mod_spec(4, MOE_TM, seq_len, per_seq),
                  _mod_spec(3, MOE_TM, seq_len, per_seq),
                  pl.BlockSpec((MOE_TM, ROUTE_ROWS), lambda i, s: (i, 0)),
                  pl.BlockSpec(memory_space=pl.ANY)],
        out_specs=pl.BlockSpec(memory_space=pl.ANY),
        scratch_shapes=[pltpu.VMEM((2, MOE_TM, ROW_W), F32), pltpu.SemaphoreType.DMA((2,))],
    )
    return pl.pallas_call(
        functools.partial(_dispatch_kernel, n_steps),
        grid_spec=grid_spec,
        out_shape=jax.ShapeDtypeStruct(hs.shape, F32),
        input_output_aliases={6: 0},
        compiler_params=_cparams("arbitrary"),
        name="moe_dispatch",
    )(slots, x, g.reshape(1, d), mod, mod, route_t, hs)


def _experts_kernel(layer, ea_ref, eb_ref, chg_ref, nxt_ref, more_ref, par_ref, nused_ref,
                    hs_ref, wg_hbm, wu_hbm, wd_hbm, ys_ref, fg, fu, fd, bg, bu, bd, sem):
    def weight_copies(slot, expert, par):
        return [pltpu.make_async_copy(src.at[layer, expert], dst.at[slot, par], sem.at[slot, par])
                for src, dst in ((wg_hbm, fg), (wu_hbm, fu), (wd_hbm, fd))]

    def tile(k, rows):
        @pl.when(k < nused_ref[0])
        def _():
            for slot, e_ref in enumerate((ea_ref, eb_ref)):
                @pl.when(chg_ref[slot, k] == 1)
                def _():
                    par = par_ref[slot, k]

                    @pl.when(k == 0)
                    def _():
                        for cp in weight_copies(slot, e_ref[0], par):
                            cp.start()

                    for cp in weight_copies(slot, e_ref[k], par):
                        cp.wait()
                    bg[slot] = fg[slot, par].astype(BF16)
                    bu[slot] = fu[slot, par].astype(BF16)
                    bd[slot] = fd[slot, par].astype(BF16)

                    @pl.when(more_ref[slot, k] == 1)
                    def _():
                        for cp in weight_copies(slot, nxt_ref[slot, k], 1 - par):
                            cp.start()

            h = hs_ref[rows, 0:D_MODEL].astype(BF16)

            def ffn(slot):
                hid = _dot(h, bg[slot])
                up = _dot(h, bu[slot])
                w = hs_ref[rows, D_MODEL + 1 + slot:D_MODEL + 2 + slot]
                act = (hid * _sigmoid(hid)) * up * w
                return _dot(act.astype(BF16), bd[slot])

            ys_ref[rows, :] = ffn(0) + ffn(1)

        @pl.when(k >= nused_ref[0])
        def _():
            ys_ref[rows, :] = jnp.zeros((MOE_TS, ys_ref.shape[1]), ys_ref.dtype)

    for j in range(MOE_STEP_TILES):
        tile(pl.program_id(0) * MOE_STEP_TILES + j, slice(j * MOE_TS, (j + 1) * MOE_TS))


def moe_experts(hs, maps, layer, w_gate, w_up, w_down):
    n_tiles = hs.shape[0] // MOE_TS
    d = D_MODEL

    grid_spec = pltpu.PrefetchScalarGridSpec(
        num_scalar_prefetch=7,
        grid=(n_tiles // MOE_STEP_TILES,),
        in_specs=[pl.BlockSpec((MOE_STEP_TILES * MOE_TS, ROW_W), lambda k, *_: (k, 0)),
                  pl.BlockSpec(memory_space=pl.ANY), pl.BlockSpec(memory_space=pl.ANY),
                  pl.BlockSpec(memory_space=pl.ANY)],
        out_specs=pl.BlockSpec((MOE_STEP_TILES * MOE_TS, d), lambda k, *_: (k, 0)),
        scratch_shapes=[pltpu.VMEM((2, 2, d, D_EXPERT), F32), pltpu.VMEM((2, 2, d, D_EXPERT), F32),
                        pltpu.VMEM((2, 2, D_EXPERT, d), F32),
                        pltpu.VMEM((2, d, D_EXPERT), BF16), pltpu.VMEM((2, d, D_EXPERT), BF16),
                        pltpu.VMEM((2, D_EXPERT, d), BF16),
                        pltpu.SemaphoreType.DMA((2, 2))],
    )
    return pl.pallas_call(
        functools.partial(_experts_kernel, layer),
        grid_spec=grid_spec,
        out_shape=jax.ShapeDtypeStruct((hs.shape[0], d), F32),
        compiler_params=_cparams("arbitrary"),
        name="moe_experts",
    )(*maps, hs, w_gate, w_up, w_down)


def _combine_kernel(final, n_steps, slots_ref, x_ref, g2_ref, fg_ref, ys_hbm, o_ref, gbuf, sem):
    i = pl.program_id(0)
    cur = i % 2

    def issue_tile(tile, s):
        for r in range(MOE_TM):
            src = slots_ref[tile * MOE_TM + r]
            pltpu.make_async_copy(ys_hbm.at[src], gbuf.at[s, r], sem.at[s]).start()

    @pl.when(i == 0)
    def _():
        issue_tile(0, 0)

    for s in range(2):
        @pl.when((i + 1 < n_steps) & (1 - cur == s))
        def _():
            issue_tile(i + 1, s)

    pltpu.make_async_copy(gbuf.at[cur], gbuf.at[cur], sem.at[cur]).wait()
    y = x_ref[...] + g2_ref[0] * gbuf[cur]
    if final:
        ms = jnp.mean(y * y, axis=-1, keepdims=True)
        y = y * lax.rsqrt(ms + EPS) * fg_ref[...]
    o_ref[...] = y


def moe_combine(x, mod, slots, ys, final_g, final, seq_len, per_seq):
    t, d = x.shape
    n_steps = t // MOE_TM
    grid_spec = pltpu.PrefetchScalarGridSpec(
        num_scalar_prefetch=1,
        grid=(n_steps,),
        in_specs=[pl.BlockSpec((MOE_TM, d), lambda i, s: (i, 0)),
                  _mod_spec(5, MOE_TM, seq_len, per_seq),
                  pl.BlockSpec((1, d), lambda i, s: (0, 0)),
                  pl.BlockSpec(memory_space=pl.ANY)],
        out_specs=pl.BlockSpec((MOE_TM, d), lambda i, s: (i, 0)),
        scratch_shapes=[pltpu.VMEM((2, MOE_TM, d), F32), pltpu.SemaphoreType.DMA((2,))],
    )
    return pl.pallas_call(
        functools.partial(_combine_kernel, final, n_steps),
        grid_spec=grid_spec,
        out_shape=jax.ShapeDtypeStruct((t, d), F32),
        compiler_params=_cparams("arbitrary"),
        name="moe_combine",
    )(slots, x, mod, final_g.reshape(1, d), ys)


def _combine_nm_tm_kernel(nseq, steps, seq_len, n_steps, slots_ref, x_ref, g2_ref, ys_hbm, g_ref, sc_ref,
                          sh_ref, w_ref, x2_ref, u_ref, gbuf, sem):
    i = pl.program_id(0)
    cur = i % 2

    def issue_tile(tile, s):
        for sq in range(nseq):
            for t in range(steps):
                src = slots_ref[sq * seq_len + tile * steps + t]
                pltpu.make_async_copy(ys_hbm.at[src], gbuf.at[s, sq * steps + t], sem.at[s]).start()

    @pl.when(i == 0)
    def _():
        issue_tile(0, 0)

    for s in range(2):
        @pl.when((i + 1 < n_steps) & (1 - cur == s))
        def _():
            issue_tile(i + 1, s)

    pltpu.make_async_copy(gbuf.at[cur], gbuf.at[cur], sem.at[cur]).wait()
    d = x_ref.shape[-1]
    y = x_ref[...] + g2_ref[...] * gbuf[cur].reshape(nseq, steps, d)
    x2_ref[...] = y
    h = _normmod(y, g_ref[...], sc_ref[...], sh_ref[...])
    h = h.reshape(nseq * steps, d).astype(BF16)
    h = _dot(_row_permutation(nseq, steps, True), h).astype(BF16)
    u_ref[...] = _dot(h, w_ref[...])


def combine_nm_matmul_tm(x, mod_prev, slots, ys, g, mod, w, nseq, seq_len, per_seq):
    t, d = x.shape
    n = w.shape[1]
    steps = TM_ROWS // nseq
    n_steps = seq_len // steps
    g2 = _group_mod(mod_prev, 5, nseq, per_seq)
    sc = _group_mod(mod, 1, nseq, per_seq)
    sh = _group_mod(mod, 0, nseq, per_seq)
    mod_spec = pl.BlockSpec(sc.shape, lambda i, s: (0, 0, 0))
    x_spec = pl.BlockSpec((nseq, steps, d), lambda i, s: (0, i, 0))
    grid_spec = pltpu.PrefetchScalarGridSpec(
        num_scalar_prefetch=1,
        grid=(n_steps,),
        in_specs=[x_spec, mod_spec, pl.BlockSpec(memory_space=pl.ANY),
                  pl.BlockSpec((1, d), lambda i, s: (0, 0)), mod_spec, mod_spec,
                  pl.BlockSpec((d, n), lambda i, s: (0, 0))],
        out_specs=[x_spec, pl.BlockSpec((TM_ROWS, n), lambda i, s: (i, 0))],
        scratch_shapes=[pltpu.VMEM((2, TM_ROWS, d), F32), pltpu.SemaphoreType.DMA((2,))],
    )
    x2, u = pl.pallas_call(
        functools.partial(_combine_nm_tm_kernel, nseq, steps, seq_len, n_steps),
        grid_spec=grid_spec,
        out_shape=[jax.ShapeDtypeStruct((nseq, seq_len, d), F32), jax.ShapeDtypeStruct((t, n), F32)],
        compiler_params=_cparams("arbitrary"),
        name="moe_combine_norm_mod_proj_tm",
    )(slots, x.reshape(nseq, seq_len, d), g2, ys, g.reshape(1, d), sc, sh, w)
    return x2.reshape(t, d), u


def moe_block(xs, routes, mods_l, per_seqs, seq_lens, layer, p, final, defer_combine, hs):
    g = p['norm_g'][layer, 1]
    slots, off, cnt = moe_slots(jnp.concatenate(routes, axis=1))
    t_all = slots.shape[1]
    n_tiles = -(-(t_all // MOE_TS + N_CLS) // MOE_STEP_TILES) * MOE_STEP_TILES
    maps = _tile_maps(off, cnt, n_tiles)
    if hs is None:
        hs = jnp.zeros((n_tiles * MOE_TS, ROW_W), F32)
    bounds = np.cumsum([0] + [x.shape[0] for x in xs])
    group_slots = [slots[0, bounds[i]:bounds[i + 1]] for i in range(len(xs))]
    for x, r, s, sl, ps in zip(xs, routes, group_slots, seq_lens, per_seqs):
        hs = moe_dispatch(x, g, mods_l, r.T, s, hs, sl, ps)
    ys = moe_experts(hs, maps, layer, p['moe_w_gate'], p['moe_w_up'], p['moe_w_down'])
    if defer_combine:
        return [(x, s, ys) for x, s in zip(xs, group_slots)], hs
    return [moe_combine(x, mods_l, s, ys, p['final_g'], final, sl, ps)
            for x, s, sl, ps in zip(xs, group_slots, seq_lens, per_seqs)], hs


def _mixer(x, group, l, mod, p, hy_tables, mod_prev):
    per_seq, nseq, seq_len = group['per_seq'], group['nseq'], group['seq_len']
    extras = None
    deferred = isinstance(x, tuple)
    if deferred and l % 2 == 0:
        x = moe_combine(x[0], mod_prev, x[1], x[2], p['final_g'], False, seq_len, per_seq)
        deferred = False
    if l % 2 == 0:
        e = l // 2
        u = nm_matmul(x, p['norm_g'][l, 0], mod, p['a_in_w'][e].astype(BF16), seq_len, per_seq)
        if group['ctx_k'] is None:
            attn, nk, nv = context_attention(u, nseq, seq_len)
            extras = (nk, nv)
        else:
            attn = neighbourhood_attention(u, group['ctx_k'][:, e], group['ctx_v'][:, e], p['na_rpb'][e],
                                           nseq, seq_len)
        fwd, inv = hy_tables[seq_len]
        spectrum = hyena_spectrum(seq_len, p['hy_w1'][e], p['hy_b1'][e], p['hy_w2'][e], p['hy_b2'][e],
                                  p['hy_w3'][e], p['hy_freq'][e], p['hy_d'][e], fwd)
        hy = hyena_mixer(u, nseq, seq_len, p['hy_short_w'][e], p['hy_short_b'][e], spectrum, fwd, inv)
        w_out = p['a_out_w'][e].astype(BF16)
        x, route = proj_residual([attn, hy], [w_out[:D_A], w_out[D_A:]], x, mod, p['norm_g'][l, 1],
                                 p['router_w'], p['router_b'], seq_len, per_seq)
    else:
        o = l // 2
        w_in = p['c_in_w'][o].astype(BF16)
        if deferred:
            x, u = combine_nm_matmul_tm(x[0], mod_prev, x[1], x[2], p['norm_g'][l, 0], mod, w_in,
                                        nseq, seq_len, per_seq)
        else:
            u = nm_matmul_tm(x, p['norm_g'][l, 0], mod, w_in, nseq, seq_len, per_seq)
        y, extras = rglru_block(u, nseq, seq_len, p['rg_conv_w'][o], p['rg_conv_b'][o], p['rg_wa'][o],
                                p['rg_ba'][o], p['rg_wx'][o], p['rg_bx'][o], p['rg_lam'][o], group['h0'][o])
        x, route = proj_residual_tm(y, p['c_out_w'][o].astype(BF16), x, mod, p['norm_g'][l, 1],
                                    p['router_w'], p['router_b'], nseq, seq_len, per_seq)
    return x, extras, route


def kernel(x_prompt, x_sample, cache_k, cache_v, state_h, c, c_ctx, norm_g, ada_w, ada_b, final_g, a_in_w, a_out_w, na_rpb, hy_short_w, hy_short_b, hy_w1, hy_b1, hy_w2, hy_b2, hy_w3, hy_freq, hy_d, c_in_w, c_out_w, rg_conv_w, rg_conv_b, rg_wa, rg_ba, rg_wx, rg_bx, rg_lam, router_w, router_b, moe_w_gate, moe_w_up, moe_w_down):
    p = dict(norm_g=norm_g, final_g=final_g, a_in_w=a_in_w, a_out_w=a_out_w, na_rpb=na_rpb,
             hy_short_w=hy_short_w, hy_short_b=hy_short_b, hy_w1=hy_w1, hy_b1=hy_b1, hy_w2=hy_w2,
             hy_b2=hy_b2, hy_w3=hy_w3, hy_freq=hy_freq, hy_d=hy_d, c_in_w=c_in_w, c_out_w=c_out_w,
             rg_conv_w=rg_conv_w, rg_conv_b=rg_conv_b, rg_wa=rg_wa, rg_ba=rg_ba, rg_wx=rg_wx, rg_bx=rg_bx,
             rg_lam=rg_lam, router_w=router_w, router_b=router_b, moe_w_gate=moe_w_gate,
             moe_w_up=moe_w_up, moe_w_down=moe_w_down)
    batch, seq, d = x_prompt.shape
    dec_batch, dec_seq, _ = x_sample.shape
    n_odd = DEPTH // 2
    assert 1 + dec_batch <= MOD_ROWS

    cond = jnp.concatenate([c_ctx[None, :], c, jnp.zeros((MOD_ROWS - 1 - dec_batch, d), F32)], axis=0)
    m = modulation(cond, ada_w, ada_b)
    mods = [m[l].reshape(MOD_ROWS * N_MOD, 1, d) for l in range(DEPTH)]

    tables = {}
    for sl in (seq, dec_seq):
        fwd, inv = _dft_tables(sl)
        tables[sl] = (jnp.asarray(fwd).astype(BF16), jnp.asarray(inv).astype(BF16))

    groups = [
        dict(per_seq=False, nseq=batch, seq_len=seq, ctx_k=None, ctx_v=None,
             h0=[jnp.zeros((2, batch, D_RNN), F32)] * n_odd),
        dict(per_seq=True, nseq=dec_batch, seq_len=dec_seq, ctx_k=cache_k, ctx_v=cache_v,
             h0=[state_h[:, o].transpose(1, 0, 2) for o in range(n_odd)]),
    ]
    xs = [x_prompt.reshape(batch * seq, d), x_sample.reshape(dec_batch * dec_seq, d)]
    k_list, v_list, h_list = [], [], []
    hs = None
    for l in range(DEPTH):
        mixed = [_mixer(x, grp, l, mods[l], p, tables, mods[l - 1] if l else None)
                 for x, grp in zip(xs, groups)]
        if l % 2 == 0:
            k_list.append(mixed[0][1][0])
            v_list.append(mixed[0][1][1])
        else:
            h_list.append(mixed[0][1].transpose(1, 0, 2))
        xs, hs = moe_block([mx[0] for mx in mixed], [mx[2] for mx in mixed], mods[l],
                           [grp['per_seq'] for grp in groups], [grp['seq_len'] for grp in groups],
                           l, p, l == DEPTH - 1, l < DEPTH - 1, hs)
    new_k = jnp.stack(k_list, axis=1)
    new_v = jnp.stack(v_list, axis=1)
    new_h = jnp.stack(h_list, axis=1)
    return (xs[0].reshape(batch, seq, d), xs[1].reshape(dec_batch, dec_seq, d), new_k, new_v, new_h)
```

```python
import functools
import math

import numpy as np
import jax
import jax.numpy as jnp
from jax import lax
from jax.experimental import pallas as pl
from jax.experimental.pallas import tpu as pltpu

F32 = jnp.float32
BF16 = jnp.bfloat16

D_MODEL = 1024
DEPTH = 2
GRID_W = 64
EPS = 1e-6
NEG_INF = -1e30
NA_HEADS = 8
HEAD_DIM = 64
D_A = NA_HEADS * HEAD_DIM
WIN_ROWS = 8
WIN_COLS = 16
D_B = D_MODEL - D_A
HY_ORDER = 2
HY_EMB = 33
HY_BANDS = (HY_EMB - 1) // 2
HY_FFN = 64
HY_DECAY_TARGET = 1e-2
HY_FAST_PCT = 0.3
HY_SLOW_PCT = 1.5
D_RNN = D_MODEL
RG_BLOCK = 64
RG_C = 8.0
N_EXPERTS = 16
N_GROUPS = 4
EXPERTS_PER_GROUP = N_EXPERTS // N_GROUPS
D_EXPERT = 512

LANES = 128
SUBLANES = 8
VMEM_LIMIT = 56 * 1024 * 1024
N_MOD = 6
MOD_ROWS = 16


def _cparams(*sem):
    return pltpu.CompilerParams(dimension_semantics=sem, vmem_limit_bytes=VMEM_LIMIT)


def _dot(a, b):
    return jnp.dot(a, b, preferred_element_type=F32)


def _dot_nt(a, b):
    return lax.dot_general(a, b, (((1,), (1,)), ((), ())), preferred_element_type=F32)


def _sigmoid(x):
    return 0.5 * jnp.tanh(0.5 * x) + 0.5


def _normmod(x, g, sc, sh):
    ms = jnp.mean(x * x, axis=-1, keepdims=True)
    return (x * lax.rsqrt(ms + EPS) * g) * (1.0 + sc) + sh


def _mod_spec(chunk, tm, seq_len, per_seq):
    if per_seq:
        return pl.BlockSpec((1, 1, D_MODEL), lambda i, *_: ((1 + (i * tm) // seq_len) * N_MOD + chunk, 0, 0))
    return pl.BlockSpec((1, 1, D_MODEL), lambda i, *_: (chunk, 0, 0))


def _mod_kernel(c_ref, w_ref, b_ref, o_ref):
    s = c_ref[...]
    s = s * jax.nn.sigmoid(s)
    o_ref[0] = _dot(s.astype(BF16), w_ref[0].astype(BF16)) + b_ref[0]


def modulation(cond, ada_w, ada_b):
    tn = 1536
    n = ada_w.shape[-1]
    return pl.pallas_call(
        _mod_kernel,
        grid=(DEPTH, n // tn),
        in_specs=[pl.BlockSpec((MOD_ROWS, D_MODEL), lambda l, j: (0, 0)),
                  pl.BlockSpec((1, D_MODEL, tn), lambda l, j: (l, 0, j)),
                  pl.BlockSpec((1, 1, tn), lambda l, j: (l, 0, j))],
        out_specs=pl.BlockSpec((1, MOD_ROWS, tn), lambda l, j: (l, 0, j)),
        out_shape=jax.ShapeDtypeStruct((DEPTH, MOD_ROWS, n), F32),
        compiler_params=_cparams("arbitrary", "arbitrary"),
        name="modulation",
    )(cond, ada_w, ada_b.reshape(DEPTH, 1, n))


def _nm_matmul_kernel(x_ref, g_ref, sc_ref, sh_ref, w_ref, o_ref):
    h = _normmod(x_ref[...], g_ref[...], sc_ref[0], sh_ref[0])
    o_ref[...] = _dot(h.astype(BF16), w_ref[...])


def nm_matmul(x, g, mod, w, seq_len, per_seq, tm=512):
    t, d = x.shape
    n = w.shape[1]
    return pl.pallas_call(
        _nm_matmul_kernel,
        grid=(t // tm,),
        in_specs=[pl.BlockSpec((tm, d), lambda i: (i, 0)),
                  pl.BlockSpec((1, d), lambda i: (0, 0)),
                  _mod_spec(1, tm, seq_len, per_seq),
                  _mod_spec(0, tm, seq_len, per_seq),
                  pl.BlockSpec((d, n), lambda i: (0, 0))],
        out_specs=pl.BlockSpec((tm, n), lambda i: (i, 0)),
        out_shape=jax.ShapeDtypeStruct((t, n), F32),
        compiler_params=_cparams("arbitrary"),
        name="norm_mod_proj",
    )(x, g.reshape(1, d), mod, mod, w)


def _proj_res_kernel(n_act, *refs):
    acts = refs[:n_act]
    ws = refs[n_act:2 * n_act]
    x_ref, g_ref, ng_ref, sc2_ref, sh2_ref, rw_ref, rb_ref, o_ref, route_ref = refs[2 * n_act:]
    acc = _dot(acts[0][...].astype(BF16), ws[0][...])
    for a, w in zip(acts[1:], ws[1:]):
        acc += _dot(a[...].astype(BF16), w[...])
    x = x_ref[...] + g_ref[0] * acc
    o_ref[...] = x
    route_ref[...] = _route_record(_normmod(x, ng_ref[...], sc2_ref[0], sh2_ref[0]), rw_ref[...], rb_ref[...])


def proj_residual(acts, ws, x, mod, norm2_g, router_w, router_b, seq_len, per_seq, tm=512):
    t, d = x.shape
    in_specs = [pl.BlockSpec((tm, a.shape[1]), lambda i: (i, 0)) for a in acts]
    in_specs += [pl.BlockSpec(w.shape, lambda i: (0, 0)) for w in ws]
    in_specs += [pl.BlockSpec((tm, d), lambda i: (i, 0)), _mod_spec(2, tm, seq_len, per_seq),
                 pl.BlockSpec((1, d), lambda i: (0, 0)),
                 _mod_spec(4, tm, seq_len, per_seq), _mod_spec(3, tm, seq_len, per_seq),
                 pl.BlockSpec((N_EXPERTS, d), lambda i: (0, 0)),
                 pl.BlockSpec((N_EXPERTS, 1), lambda i: (0, 0))]
    return pl.pallas_call(
        functools.partial(_proj_res_kernel, len(acts)),
        grid=(t // tm,),
        in_specs=in_specs,
        out_specs=[pl.BlockSpec((tm, d), lambda i: (i, 0)), pl.BlockSpec((ROUTE_ROWS, tm), lambda i: (0, i))],
        out_shape=[jax.ShapeDtypeStruct((t, d), F32), jax.ShapeDtypeStruct((ROUTE_ROWS, t), F32)],
        compiler_params=_cparams("arbitrary"),
        name="proj_residual_route",
    )(*acts, *ws, x, mod, norm2_g.reshape(1, d), mod, mod, router_w.T, router_b.reshape(N_EXPERTS, 1))


def _row_permutation(nseq, steps, to_time_major):
    n = nseq * steps
    i = lax.broadcasted_iota(jnp.int32, (n, n), 0)
    j = lax.broadcasted_iota(jnp.int32, (n, n), 1)
    if to_time_major:
        src = (i % nseq) * steps + i // nseq
    else:
        src = (i % steps) * nseq + i // steps
    return (j == src).astype(BF16)


def _nm_matmul_tm_kernel(nseq, steps, x_ref, g_ref, sc_ref, sh_ref, w_ref, o_ref):
    h = _normmod(x_ref[...], g_ref[...], sc_ref[...], sh_ref[...])
    h = h.reshape(nseq * steps, h.shape[-1]).astype(BF16)
    h = _dot(_row_permutation(nseq, steps, True), h).astype(BF16)
    o_ref[...] = _dot(h, w_ref[...])


def _group_mod(mod, chunk, nseq, per_seq):
    rows = mod.reshape(MOD_ROWS, N_MOD, 1, D_MODEL)
    return rows[1:1 + nseq, chunk] if per_seq else rows[0:1, chunk]


def nm_matmul_tm(x, g, mod, w, nseq, seq_len, per_seq):
    t, d = x.shape
    n = w.shape[1]
    steps = TM_ROWS // nseq
    sc = _group_mod(mod, 1, nseq, per_seq)
    sh = _group_mod(mod, 0, nseq, per_seq)
    mod_spec = pl.BlockSpec(sc.shape, lambda i: (0, 0, 0))
    return pl.pallas_call(
        functools.partial(_nm_matmul_tm_kernel, nseq, steps),
        grid=(seq_len // steps,),
        in_specs=[pl.BlockSpec((nseq, steps, d), lambda i: (0, i, 0)),
                  pl.BlockSpec((1, d), lambda i: (0, 0)),
                  mod_spec, mod_spec,
                  pl.BlockSpec((d, n), lambda i: (0, 0))],
        out_specs=pl.BlockSpec((TM_ROWS, n), lambda i: (i, 0)),
        out_shape=jax.ShapeDtypeStruct((t, n), F32),
        compiler_params=_cparams("arbitrary"),
        name="norm_mod_proj_tm",
    )(x.reshape(nseq, seq_len, d), g.reshape(1, d), sc, sh, w)


def _proj_res_tm_kernel(nseq, steps, y_ref, w_ref, x_ref, g_ref, ng_ref, sc2_ref, sh2_ref, rw_ref, rb_ref,
                        o_ref, route_ref):
    y = _dot(_row_permutation(nseq, steps, False), y_ref[...].astype(BF16)).astype(BF16)
    acc = _dot(y, w_ref[...])
    x = x_ref[...] + g_ref[...] * acc.reshape(nseq, steps, acc.shape[-1])
    o_ref[...] = x
    h = _normmod(x, ng_ref[...], sc2_ref[...], sh2_ref[...]).reshape(nseq * steps, x.shape[-1])
    route_ref[0] = _route_record(h, rw_ref[...], rb_ref[...])


def proj_residual_tm(y, w, x, mod, norm2_g, router_w, router_b, nseq, seq_len, per_seq):
    t, d = x.shape
    steps = TM_ROWS // nseq
    n_steps = seq_len // steps
    g1 = _group_mod(mod, 2, nseq, per_seq)
    sc2 = _group_mod(mod, 4, nseq, per_seq)
    sh2 = _group_mod(mod, 3, nseq, per_seq)
    mod_spec = pl.BlockSpec(g1.shape, lambda i: (0, 0, 0))
    out, route = pl.pallas_call(
        functools.partial(_proj_res_tm_kernel, nseq, steps),
        grid=(n_steps,),
        in_specs=[pl.BlockSpec((TM_ROWS, y.shape[1]), lambda i: (i, 0)),
                  pl.BlockSpec(w.shape, lambda i: (0, 0)),
                  pl.BlockSpec((nseq, steps, d), lambda i: (0, i, 0)),
                  mod_spec,
                  pl.BlockSpec((1, d), lambda i: (0, 0)),
                  mod_spec, mod_spec,
                  pl.BlockSpec((N_EXPERTS, d), lambda i: (0, 0)),
                  pl.BlockSpec((N_EXPERTS, 1), lambda i: (0, 0))],
        out_specs=[pl.BlockSpec((nseq, steps, d), lambda i: (0, i, 0)),
                   pl.BlockSpec((1, ROUTE_ROWS, TM_ROWS), lambda i: (i, 0, 0))],
        out_shape=[jax.ShapeDtypeStruct((nseq, seq_len, d), F32),
                   jax.ShapeDtypeStruct((n_steps, ROUTE_ROWS, TM_ROWS), F32)],
        compiler_params=_cparams("arbitrary"),
        name="proj_residual_tm_route",
    )(y, w, x.reshape(nseq, seq_len, d), g1, norm2_g.reshape(1, d), sc2, sh2, router_w.T,
      router_b.reshape(N_EXPERTS, 1))
    route = route.reshape(n_steps, ROUTE_ROWS, nseq, steps).transpose(1, 2, 0, 3).reshape(ROUTE_ROWS, t)
    return out.reshape(t, d), route


def _ctx_attn_kernel(q_ref, k_ref, v_ref, o_ref, nk_ref, nv_ref):
    scale = HEAD_DIM ** -0.5
    per_tile = LANES // HEAD_DIM
    lane = lax.broadcasted_iota(jnp.int32, (1, LANES), 1)
    for hp in range(NA_HEADS // per_tile):
        sl = slice(hp * LANES, (hp + 1) * LANES)
        q, k, v = q_ref[:, sl] * scale, k_ref[:, sl], v_ref[:, sl]
        kb, vb = k.astype(BF16), v.astype(BF16)
        out = None
        for j in range(per_tile):
            h = hp * per_tile + j
            nk_ref[0, h] = k[:, j * HEAD_DIM:(j + 1) * HEAD_DIM]
            nv_ref[0, h] = v[:, j * HEAD_DIM:(j + 1) * HEAD_DIM]
            mine = lane // HEAD_DIM == j
            s = _dot_nt(jnp.where(mine, q, 0.0).astype(BF16), kb)
            p = jnp.exp(s - jnp.max(s, axis=-1, keepdims=True))
            o = _dot(p.astype(BF16), vb) / jnp.sum(p, axis=-1, keepdims=True)
            out = o if out is None else jnp.where(mine, o, out)
        o_ref[:, sl] = out.astype(o_ref.dtype)


def context_attention(u, nseq, seq_len):
    t = u.shape[0]
    kv_shape = jax.ShapeDtypeStruct((nseq, NA_HEADS, seq_len, HEAD_DIM), F32)
    kv_spec = pl.BlockSpec((1, NA_HEADS, seq_len, HEAD_DIM), lambda b: (b, 0, 0, 0))
    return pl.pallas_call(
        _ctx_attn_kernel,
        grid=(nseq,),
        in_specs=[pl.BlockSpec((seq_len, D_A), lambda b: (b, 0)),
                  pl.BlockSpec((seq_len, D_A), lambda b: (b, 1)),
                  pl.BlockSpec((seq_len, D_A), lambda b: (b, 2))],
        out_specs=[pl.BlockSpec((seq_len, D_A), lambda b: (b, 0)), kv_spec, kv_spec],
        out_shape=[jax.ShapeDtypeStruct((t, D_A), BF16), kv_shape, kv_shape],
        compiler_params=_cparams("arbitrary"),
        name="context_attention",
    )(u, u, u)


N_DR = 2 * WIN_ROWS - 1
N_DC = 2 * WIN_COLS - 1


def _na_col_tables():
    cols = np.arange(GRID_W)
    col_start = np.clip(cols - WIN_COLS // 2, 0, GRID_W - WIN_COLS)
    col_in = (cols[None, :] >= col_start[:, None]) & (cols[None, :] < col_start[:, None] + WIN_COLS)
    dc = np.clip(cols[None, :] - cols[:, None], 1 - WIN_COLS, WIN_COLS - 1) + WIN_COLS - 1
    onehot = (dc.reshape(1, -1) == np.arange(32)[:, None]).astype(np.float32)
    return onehot, col_in.reshape(1, -1).astype(np.float32)


def _na_bias_kernel(r_ref, e_ref, m_ref, o_ref):
    t = jnp.dot(r_ref[...], e_ref[...], precision=lax.Precision.HIGHEST, preferred_element_type=F32)
    o_ref[...] = jnp.where(m_ref[...] > 0.0, t, NEG_INF)


def na_bias_table(rpb):
    onehot, col_in = _na_col_tables()
    n_rows = NA_HEADS * N_DR
    r = jnp.zeros((LANES, 32), F32).at[:n_rows, :N_DC].set(rpb.reshape(n_rows, N_DC).astype(F32))
    t = pl.pallas_call(
        _na_bias_kernel,
        out_shape=jax.ShapeDtypeStruct((LANES, GRID_W * GRID_W), F32),
        name="na_bias_table",
    )(r, jnp.asarray(onehot), jnp.asarray(col_in))
    t = t[:n_rows].reshape(NA_HEADS, N_DR, GRID_W, GRID_W)
    return jnp.concatenate([t[:, :-1], t[:, 1:]], axis=-1)


def _na_kernel(rows, q_ref, k_ref, v_ref, ck_ref, cv_ref, bias_ref, o_ref,
               q_s, k_s, v_s, ck_s, cv_s, s_s, p_s, den_s, o_s):
    scale = HEAD_DIM ** -0.5
    n_lat = WIN_ROWS * GRID_W
    per_tile = LANES // HEAD_DIM
    n_pairs = NA_HEADS // per_tile
    lane = lax.broadcasted_iota(jnp.int32, (1, LANES), 1)
    for hp in range(n_pairs):
        sl = slice(hp * LANES, (hp + 1) * LANES)
        q_s[hp] = (q_ref[:, sl] * scale).astype(BF16)
        k_s[hp] = k_ref[:, sl].astype(BF16)
        v_s[hp] = v_ref[:, sl].astype(BF16)
        heads = range(hp * per_tile, (hp + 1) * per_tile)
        ck_s[hp] = jnp.concatenate([ck_ref[0, h] for h in heads], axis=1).astype(BF16)
        cv_s[hp] = jnp.concatenate([cv_ref[0, h] for h in heads], axis=1).astype(BF16)

    def window(r):
        start = min(max(r - WIN_ROWS // 2, 0), rows - WIN_ROWS)
        return start, start - r + WIN_ROWS - 1

    def pair_body(hp, carry):
        for j in range(per_tile):
            h = hp * per_tile + j
            mine = lane // HEAD_DIM == j
            for r in range(rows):
                start, off = window(r)
                rs = slice(r * GRID_W, (r + 1) * GRID_W)
                q = jnp.where(mine, q_s[hp, rs, :], 0.0).astype(BF16)
                bias = jnp.concatenate([bias_ref[h, off + 2 * i] for i in range(WIN_ROWS // 2)], axis=1)
                s_s[rs, 0:n_lat] = _dot_nt(q, k_s[hp, start * GRID_W:start * GRID_W + n_lat, :]) + bias
                s_s[rs, n_lat:] = _dot_nt(q, ck_s[hp])
            for r in range(rows):
                rs = slice(r * GRID_W, (r + 1) * GRID_W)
                s = s_s[rs, :]
                p = jnp.exp(s - jnp.max(s, axis=-1, keepdims=True))
                den_s[rs, :] = jnp.sum(p, axis=-1, keepdims=True)
                p_s[rs, :] = p.astype(BF16)
            for r in range(rows):
                start, _ = window(r)
                rs = slice(r * GRID_W, (r + 1) * GRID_W)
                o = (_dot(p_s[rs, 0:n_lat], v_s[hp, start * GRID_W:start * GRID_W + n_lat, :])
                     + _dot(p_s[rs, n_lat:], cv_s[hp])) / den_s[rs, :]
                o_s[hp, rs, :] = o if j == 0 else jnp.where(mine, o, o_s[hp, rs, :])
        return carry

    lax.fori_loop(0, n_pairs, pair_body, 0)
    for hp in range(n_pairs):
        o_ref[:, hp * LANES:(hp + 1) * LANES] = o_s[hp].astype(o_ref.dtype)


def neighbourhood_attention(u, ctx_k, ctx_v, rpb, nseq, seq_len):
    t = u.shape[0]
    rows = seq_len // GRID_W
    assert rows >= WIN_ROWS and WIN_ROWS % 2 == 0
    past = ctx_k.shape[2]
    bias = na_bias_table(rpb)
    ctx_spec = pl.BlockSpec((1, NA_HEADS, past, HEAD_DIM), lambda b: (b, 0, 0, 0))
    return pl.pallas_call(
        functools.partial(_na_kernel, rows),
        grid=(nseq,),
        in_specs=[pl.BlockSpec((seq_len, D_A), lambda b: (b, 0)),
                  pl.BlockSpec((seq_len, D_A), lambda b: (b, 1)),
                  pl.BlockSpec((seq_len, D_A), lambda b: (b, 2)),
                  ctx_spec, ctx_spec,
                  pl.BlockSpec(bias.shape, lambda b: (0, 0, 0, 0))],
        out_specs=pl.BlockSpec((seq_len, D_A), lambda b: (b, 0)),
        out_shape=jax.ShapeDtypeStruct((t, D_A), BF16),
        scratch_shapes=[pltpu.VMEM((D_A // LANES, seq_len, LANES), BF16)] * 3
        + [pltpu.VMEM((D_A // LANES, past, LANES), BF16)] * 2
        + [pltpu.VMEM((seq_len, WIN_ROWS * GRID_W + past), F32),
           pltpu.VMEM((seq_len, WIN_ROWS * GRID_W + past), BF16),
           pltpu.VMEM((seq_len, 1), F32),
           pltpu.VMEM((D_A // LANES, seq_len, LANES), F32)],
        compiler_params=_cparams("arbitrary"),
        name="neighbourhood_attention",
    )(u, u, u, ctx_k, ctx_v, bias)


def _dft_tables(seq_len):
    n = 2 * seq_len
    f = np.arange(seq_len, dtype=np.int64)
    ang = (np.outer(f, f) % n).astype(np.float64) * (math.pi / seq_len)
    cos, sin = np.cos(ang), np.sin(ang)
    alt = np.where(f % 2 == 0, 1.0, -1.0)
    s_fwd = -sin
    s_fwd[0, :] = alt
    fwd = np.concatenate([cos, s_fwd], axis=0)
    wf = np.where(f == 0, 1.0, 2.0) / n
    ci = cos.T * wf[None, :]
    si = -sin.T * wf[None, :]
    si[:, 0] = alt / n
    inv = np.concatenate([ci, si], axis=1)
    return fwd.astype(np.float32), inv.astype(np.float32)


def _hyena_feats(seq_len):
    t = np.linspace(0.0, 1.0, seq_len, dtype=np.float32)[:, None]
    w = (2.0 * math.pi * np.arange(seq_len, dtype=np.float32)[:, None] / seq_len).astype(np.float32)
    f = np.linspace(1e-4, HY_BANDS - 1, HY_BANDS, dtype=np.float32)[None, :]
    z = np.concatenate([t, np.cos(f * w), -np.sin(f * w)], axis=-1).astype(np.float32)
    max_decay = math.log(HY_DECAY_TARGET) / HY_FAST_PCT
    min_decay = math.log(HY_DECAY_TARGET) / HY_SLOW_PCT
    deltas = np.abs(np.linspace(min_decay, max_decay, D_B, dtype=np.float32))[None, :]
    return z, t, deltas


def _hy_filter_kernel(seq_len, z_ref, t_ref, dl_ref, w1_ref, b1_ref, w2_ref, b2_ref, w3_ref, fr_ref,
                      d_ref, fwd_ref, g_ref):
    hp = lax.Precision.HIGHEST
    h = jnp.sin(fr_ref[0:1, :] * (jnp.dot(z_ref[...], w1_ref[...], precision=hp) + b1_ref[...]))
    h = jnp.sin(fr_ref[1:2, :] * (jnp.dot(h, w2_ref[...], precision=hp) + b2_ref[...]))
    h = jnp.dot(h, w3_ref[...], precision=hp)
    decay = jnp.exp(-t_ref[...] * dl_ref[...])
    row0 = lax.broadcasted_iota(jnp.int32, (seq_len, D_B), 0) == 0
    sums, diffs = [], []
    for n in range(HY_ORDER):
        hf = h[:, (2 * n) * D_B:(2 * n + 1) * D_B] * decay
        hb = h[:, (2 * n + 1) * D_B:(2 * n + 2) * D_B] * decay
        gp = jnp.where(row0, hf + hb + d_ref[n:n + 1, :], hf)
        gm = jnp.where(row0, 0.0, hb)
        sums.append(gp + gm)
        diffs.append(gp - gm)
    rhs = jnp.concatenate(sums + diffs, axis=1).astype(BF16)
    spec = _dot(fwd_ref[...], rhs)
    for n in range(HY_ORDER):
        a = spec[:, n * D_B:(n + 1) * D_B]
        b = spec[:, (HY_ORDER + n) * D_B:(HY_ORDER + n + 1) * D_B]
        g_ref[n, 0:seq_len, :] = a[0:seq_len]
        g_ref[n, seq_len:, :] = jnp.where(row0, a[seq_len:], b[seq_len:])


def hyena_spectrum(seq_len, w1, b1, w2, b2, w3, freq, d, fwd):
    z, t, deltas = _hyena_feats(seq_len)
    return pl.pallas_call(
        functools.partial(_hy_filter_kernel, seq_len),
        out_shape=jax.ShapeDtypeStruct((HY_ORDER, 2 * seq_len, D_B), F32),
        compiler_params=pltpu.CompilerParams(vmem_limit_bytes=VMEM_LIMIT),
        name="hyena_spectrum",
    )(jnp.asarray(z), jnp.asarray(t), jnp.asarray(deltas), w1, b1.reshape(1, -1), w2, b2.reshape(1, -1),
      w3, freq, d, fwd)


def _hyena_kernel(seq_len, u_ref, sw_ref, sb_ref, g_ref, fwd_ref, inv_ref, o_ref):
    u = u_ref[...]
    t_idx = lax.broadcasted_iota(jnp.int32, u.shape, 0)
    prev = jnp.where(t_idx == 0, 0.0, pltpu.roll(u, 1, axis=0))
    nxt = jnp.where(t_idx == seq_len - 1, 0.0, pltpu.roll(u, seq_len - 1, axis=0))
    u = prev * sw_ref[0:1, :] + u * sw_ref[1:2, :] + nxt * sw_ref[2:3, :] + sb_ref[...]
    row0 = lax.broadcasted_iota(jnp.int32, (seq_len, D_B), 0) == 0
    z = u[:, 0:D_B]
    for n in range(HY_ORDER):
        spec = _dot(fwd_ref[...], z.astype(BF16))
        ure, uim = spec[0:seq_len], spec[seq_len:]
        gre, gim = g_ref[n, 0:seq_len, :], g_ref[n, seq_len:, :]
        pim = uim * gim
        yre = ure * gre - jnp.where(row0, 0.0, pim)
        yim = jnp.where(row0, pim, ure * gim + uim * gre)
        y = jnp.concatenate([yre, yim], axis=0).astype(BF16)
        z = u[:, (n + 1) * D_B:(n + 2) * D_B] * _dot(inv_ref[...], y)
    o_ref[...] = z.astype(o_ref.dtype)


def hyena_mixer(u, nseq, seq_len, short_w, short_b, spectrum, fwd, inv):
    t = u.shape[0]
    width = (HY_ORDER + 1) * D_B
    col_block = (3 * D_A) // width
    assert col_block * width == 3 * D_A
    return pl.pallas_call(
        functools.partial(_hyena_kernel, seq_len),
        grid=(nseq,),
        in_specs=[pl.BlockSpec((seq_len, width), lambda b: (b, col_block)),
                  pl.BlockSpec(short_w.shape, lambda b: (0, 0)),
                  pl.BlockSpec((1, width), lambda b: (0, 0)),
                  pl.BlockSpec(spectrum.shape, lambda b: (0, 0, 0)),
                  pl.BlockSpec(fwd.shape, lambda b: (0, 0)),
                  pl.BlockSpec(inv.shape, lambda b: (0, 0))],
        out_specs=pl.BlockSpec((seq_len, D_B), lambda b: (b, 0)),
        out_shape=jax.ShapeDtypeStruct((t, D_B), BF16),
        compiler_params=_cparams("arbitrary"),
        name="hyena_mixer",
    )(u, short_w, short_b.reshape(1, width), spectrum, fwd, inv)


RG_CB = LANES
RG_CHUNK = 512
TM_ROWS = 512


def _rglru_kernel(nseq, seq_len, gate_ref, xr_ref, cw_ref, cb_ref, wg_ref, bg_ref, lam_ref, h0_ref,
                  y_ref, fin_ref, xp_ref, a_f, b_f, a_b, b_b):
    t_tot = nseq * seq_len
    c = RG_CB
    pad = 2 * nseq
    xp_ref[0:pad, :] = jnp.zeros((pad, c), F32)
    xp_ref[pad + t_tot:, :] = jnp.zeros((pad, c), F32)
    xp_ref[pad:pad + t_tot, :] = xr_ref[...]
    nl = -lam_ref[...]
    sp = jnp.maximum(nl, 0.0) + jnp.log1p(jnp.exp(-jnp.abs(nl)))
    k2 = (-0.5 * RG_C * math.log2(math.e)) * sp

    def gate_chunk(ci, carry):
        r0 = pl.multiple_of(ci * RG_CHUNK, RG_CHUNK)
        xc = xp_ref[pl.ds(r0, RG_CHUNK), :] * cw_ref[0:1, :]
        for j in range(1, cw_ref.shape[0]):
            xc = xc + xp_ref[pl.ds(r0 + j * nseq, RG_CHUNK), :] * cw_ref[j:j + 1, :]
        xc = xc + cb_ref[...]
        gts = _dot(xc.astype(BF16), wg_ref[0]) + bg_ref[...]
        x_half = 0.5 * xc
        for d, (a_ref, b_ref) in enumerate(((a_f, b_f), (a_b, b_b))):
            t_r = jnp.tanh(gts[:, (2 * d) * c:(2 * d + 1) * c])
            t_i = jnp.tanh(gts[:, (2 * d + 1) * c:(2 * d + 2) * c])
            a = jnp.exp2(t_r * k2[d:d + 1, :] + k2[d:d + 1, :])
            a_ref[pl.ds(r0, RG_CHUNK), :] = a
            y = 1.0 - a * a
            root = jnp.where(y > 0.0, y * lax.rsqrt(y), 0.0)
            b_ref[pl.ds(r0, RG_CHUNK), :] = root * ((t_i + 1.0) * x_half)
        return carry

    lax.fori_loop(0, t_tot // RG_CHUNK, gate_chunk, 0)

    def scan_step(t, carry):
        hf, hb = carry
        rows_f = pl.ds(pl.multiple_of(t * nseq, nseq), nseq)
        rows_b = pl.ds(pl.multiple_of((seq_len - 1 - t) * nseq, nseq), nseq)
        hf = a_f[rows_f, :] * hf + b_f[rows_f, :]
        hb = a_b[rows_b, :] * hb + b_b[rows_b, :]
        b_f[rows_f, :] = hf
        b_b[rows_b, :] = hb
        return hf, hb

    def scan_two_steps(i, carry):
        hf, hb = carry
        f0 = pl.ds(pl.multiple_of(2 * i * nseq, nseq), nseq)
        f1 = pl.ds(pl.multiple_of((2 * i + 1) * nseq, nseq), nseq)
        r0 = pl.ds(pl.multiple_of((seq_len - 1 - 2 * i) * nseq, nseq), nseq)
        r1 = pl.ds(pl.multiple_of((seq_len - 2 - 2 * i) * nseq, nseq), nseq)
        outs = []
        for a_ref, b_ref, s0, s1, h in ((a_f, b_f, f0, f1, hf), (a_b, b_b, r0, r1, hb)):
            a0, b0, a1, b1 = a_ref[s0, :], b_ref[s0, :], a_ref[s1, :], b_ref[s1, :]
            b_ref[s0, :] = a0 * h + b0
            h = (a1 * a0) * h + (a1 * b0 + b1)
            b_ref[s1, :] = h
            outs.append(h)
        return tuple(outs)

    if nseq * c <= SUBLANES * LANES and seq_len % 2 == 0:
        hf, hb = lax.fori_loop(0, seq_len // 2, scan_two_steps, (h0_ref[0], h0_ref[1]), unroll=4)
    else:
        hf, hb = lax.fori_loop(0, seq_len, scan_step, (h0_ref[0], h0_ref[1]), unroll=8)
    fin_ref[0] = hf
    fin_ref[1] = hb

    def out_chunk(ci, carry):
        rs = pl.ds(pl.multiple_of(ci * RG_CHUNK, RG_CHUNK), RG_CHUNK)
        y_ref[rs, :] = ((b_f[rs, :] + b_b[rs, :]) * jax.nn.gelu(gate_ref[rs, :])).astype(y_ref.dtype)
        return carry

    lax.fori_loop(0, t_tot // RG_CHUNK, out_chunk, 0)


def _rg_gate_weights(wa, wx):
    per_step = RG_CB // RG_BLOCK
    steps = D_RNN // RG_CB
    mats = []
    for d in range(2):
        for w in (wa[d], wx[d]):
            w = w.reshape(steps, per_step, RG_BLOCK, RG_BLOCK)
            eye = jnp.eye(per_step, dtype=w.dtype)
            m = jnp.einsum('spde,pq->spdqe', w, eye).reshape(steps, RG_CB, RG_CB)
            mats.append(m)
    return (0.5 * jnp.concatenate(mats, axis=-1)).astype(BF16)


def rglru_block(u, nseq, seq_len, conv_w, conv_b, wa, ba, wx, bx, lam, h0):
    t = u.shape[0]
    c = RG_CB
    steps = D_RNN // c
    wg = _rg_gate_weights(wa, wx)
    bg = jnp.stack([ba[0], bx[0], ba[1], bx[1]], axis=0).reshape(4, steps, c)
    bg = 0.5 * bg.transpose(1, 0, 2).reshape(steps, 1, 4 * c)
    y, fin = pl.pallas_call(
        functools.partial(_rglru_kernel, nseq, seq_len),
        grid=(steps,),
        in_specs=[pl.BlockSpec((t, c), lambda j: (0, j)),
                  pl.BlockSpec((t, c), lambda j: (0, steps + j)),
                  pl.BlockSpec((conv_w.shape[0], c), lambda j: (0, j)),
                  pl.BlockSpec((1, c), lambda j: (0, j)),
                  pl.BlockSpec((1, c, 4 * c), lambda j: (j, 0, 0)),
                  pl.BlockSpec((None, 1, 4 * c), lambda j: (j, 0, 0)),
                  pl.BlockSpec((2, c), lambda j: (0, j)),
                  pl.BlockSpec((2, nseq, c), lambda j: (0, 0, j))],
        out_specs=[pl.BlockSpec((t, c), lambda j: (0, j)),
                   pl.BlockSpec((2, nseq, c), lambda j: (0, 0, j))],
        out_shape=[jax.ShapeDtypeStruct((t, D_RNN), BF16),
                   jax.ShapeDtypeStruct((2, nseq, D_RNN), F32)],
        scratch_shapes=[pltpu.VMEM((t + 4 * nseq, c), F32)] + [pltpu.VMEM((t, c), F32)] * 4,
        compiler_params=_cparams("arbitrary"),
        name="rglru_block",
    )(u, u, conv_w, conv_b.reshape(1, -1), wg, bg, lam, h0)
    return y, fin


def _route_record(h, w, rb):
    h_hi = h.astype(BF16)
    h_lo = (h - h_hi.astype(F32)).astype(BF16)
    w_hi = w.astype(BF16)
    w_lo = (w - w_hi.astype(F32)).astype(BF16)
    logits = _dot_nt(w_hi, h_hi) + (_dot_nt(w_lo, h_hi) + _dot_nt(w_hi, h_lo))
    scores = jax.nn.sigmoid(logits)
    sel = scores + rb
    row = [sel[e:e + 1, :] for e in range(N_EXPERTS)]
    gs = []
    for g in range(N_GROUPS):
        r = row[g * EXPERTS_PER_GROUP:(g + 1) * EXPERTS_PER_GROUP]
        best_pair = None
        for i in range(EXPERTS_PER_GROUP):
            for j in range(i + 1, EXPERTS_PER_GROUP):
                s = r[i] + r[j]
                best_pair = s if best_pair is None else jnp.maximum(best_pair, s)
        gs.append(best_pair)
    best = jnp.zeros_like(gs[0], dtype=jnp.int32)
    top = gs[0]
    for g in range(1, N_GROUPS):
        better = gs[g] > top
        best = jnp.where(better, g, best)
        top = jnp.where(better, gs[g], top)
    picked = []
    for e in range(N_EXPERTS):
        g = e // EXPERTS_PER_GROUP
        rank = jnp.zeros_like(best)
        for o in range(g * EXPERTS_PER_GROUP, (g + 1) * EXPERTS_PER_GROUP):
            if o == e:
                continue
            ahead = (row[o] > row[e]) | ((row[o] == row[e]) & (o < e))
            rank = rank + ahead.astype(jnp.int32)
        picked.append((best == g) & (rank < 2))
    den = jnp.zeros_like(gs[0])
    for e in range(N_EXPERTS):
        den = den + jnp.where(picked[e], scores[e:e + 1, :], 0.0)
    gate = [jnp.where(picked[e], scores[e:e + 1, :] / den, 0.0) for e in range(N_EXPERTS)]
    cls = jnp.zeros_like(den)
    w_a = jnp.zeros_like(den)
    w_b = jnp.zeros_like(den)
    for g in range(N_GROUPS):
        for pi, (a, b) in enumerate(MOE_PAIRS):
            ea, eb = g * EXPERTS_PER_GROUP + a, g * EXPERTS_PER_GROUP + b
            both = picked[ea] & picked[eb]
            cls = jnp.where(both, float(g * len(MOE_PAIRS) + pi), cls)
            w_a = jnp.where(both, gate[ea], w_a)
            w_b = jnp.where(both, gate[eb], w_b)
    return jnp.concatenate([cls, w_a, w_b, jnp.zeros((ROUTE_ROWS - 3, cls.shape[1]), F32)], axis=0)


MOE_PAIRS = ((0, 1), (0, 2), (0, 3), (1, 3), (1, 2), (2, 3))
N_CLS = N_GROUPS * len(MOE_PAIRS)
CLS_PAD = 32
ROUTE_ROWS = 8
MOE_TS = 256
MOE_TM = 256
MOE_STEP_TILES = 4
SLOT_BLK = 512
ROW_W = D_MODEL + LANES


def _slots_kernel(n_blk, route_ref, slot_ref, off_ref, cnt_ref):
    cid = lax.broadcasted_iota(jnp.int32, (CLS_PAD, SLOT_BLK), 0).astype(F32)

    def members(j):
        cls = route_ref[0:1, pl.ds(pl.multiple_of(j * SLOT_BLK, SLOT_BLK), SLOT_BLK)]
        return (cid == cls).astype(F32)

    def count(j, cnt):
        return cnt + jnp.sum(members(j), axis=1, keepdims=True)

    cnt = lax.fori_loop(0, n_blk, count, jnp.zeros((CLS_PAD, 1), F32))
    cnt = jnp.broadcast_to(cnt, (CLS_PAD, LANES))
    padded = jnp.ceil(cnt * (1.0 / MOE_TS)) * MOE_TS
    r = lax.broadcasted_iota(jnp.int32, (CLS_PAD, CLS_PAD), 0)
    c = lax.broadcasted_iota(jnp.int32, (CLS_PAD, CLS_PAD), 1)
    off = jnp.dot((c < r).astype(F32), padded, precision=lax.Precision.HIGHEST, preferred_element_type=F32)
    off_ref[...] = off
    cnt_ref[...] = cnt
    tr = lax.broadcasted_iota(jnp.int32, (SLOT_BLK, SLOT_BLK), 0)
    tc = lax.broadcasted_iota(jnp.int32, (SLOT_BLK, SLOT_BLK), 1)
    earlier = (tr < tc).astype(BF16)

    def assign(j, base):
        member = members(j)
        rank = _dot(member.astype(BF16), earlier)
        slot = jnp.sum(member * (rank + base), axis=0, keepdims=True)
        slot_ref[0:1, pl.ds(pl.multiple_of(j * SLOT_BLK, SLOT_BLK), SLOT_BLK)] = slot.astype(jnp.int32)
        return base + jnp.sum(member, axis=1, keepdims=True)

    lax.fori_loop(0, n_blk, assign, off[:, 0:1])


def moe_slots(route):
    t = route.shape[1]
    stat = jax.ShapeDtypeStruct((CLS_PAD, LANES), F32)
    return pl.pallas_call(
        functools.partial(_slots_kernel, t // SLOT_BLK),
        out_shape=[jax.ShapeDtypeStruct((1, t), jnp.int32), stat, stat],
        compiler_params=pltpu.CompilerParams(vmem_limit_bytes=VMEM_LIMIT),
        name="moe_slots",
    )(route)


def _tile_maps(off, cnt, n_tiles):
    off = off[:N_CLS, 0].astype(jnp.int32)
    cnt = cnt[:N_CLS, 0].astype(jnp.int32)
    ends = off + ((cnt + MOE_TS - 1) // MOE_TS) * MOE_TS
    n_used = ends[-1] // MOE_TS
    k = jnp.arange(n_tiles, dtype=jnp.int32)
    tix = jnp.minimum(k, n_used - 1)
    cls = jnp.sum((tix[:, None] * MOE_TS >= ends[None, :]).astype(jnp.int32), axis=1)
    pair = jnp.asarray(MOE_PAIRS, jnp.int32)
    grp = (cls // len(MOE_PAIRS)) * EXPERTS_PER_GROUP
    ea = grp + pair[cls % len(MOE_PAIRS), 0]
    eb = grp + pair[cls % len(MOE_PAIRS), 1]
    n = jnp.int32(n_tiles)

    def slot_plan(e):
        chg = jnp.concatenate([jnp.ones((1,), jnp.int32), (e[1:] != e[:-1]).astype(jnp.int32)])
        at = jnp.where(chg == 1, k, n)
        nxt_at = jnp.concatenate([lax.cummin(at[::-1])[::-1][1:], n.reshape(1)])
        more = (nxt_at < n).astype(jnp.int32)
        nxt = e[jnp.minimum(nxt_at, n - 1)]
        par = (jnp.cumsum(chg) - 1) % 2
        return chg, nxt, more, par.astype(jnp.int32)

    plan_a, plan_b = slot_plan(ea), slot_plan(eb)
    chg, nxt, more, par = (jnp.stack([pa, pb]) for pa, pb in zip(plan_a, plan_b))
    return ea, eb, chg, nxt, more, par, n_used.reshape(1)


def _dispatch_kernel(n_steps, slots_ref, x_ref, g_ref, sc_ref, sh_ref, rt_ref, hs_in, hs_out, rowbuf, sem):
    del hs_in
    i = pl.program_id(0)
    cur = i % 2

    def wait_rows(s):
        pltpu.make_async_copy(rowbuf.at[s], rowbuf.at[s], sem.at[s]).wait()

    @pl.when(i >= 2)
    def _():
        wait_rows(cur)

    rowbuf[cur, :, 0:D_MODEL] = _normmod(x_ref[...], g_ref[...], sc_ref[0], sh_ref[0])
    rowbuf[cur, :, D_MODEL:ROW_W] = jnp.concatenate(
        [rt_ref[...], jnp.zeros((MOE_TM, LANES - ROUTE_ROWS), F32)], axis=1)

    for s in range(2):
        @pl.when(cur == s)
        def _():
            for r in range(MOE_TM):
                dst = slots_ref[i * MOE_TM + r]
                pltpu.make_async_copy(rowbuf.at[s, r], hs_out.at[dst], sem.at[s]).start()

    @pl.when(i == n_steps - 1)
    def _():
        wait_rows(cur)
        if n_steps >= 2:
            wait_rows(1 - cur)


def moe_dispatch(x, g, mod, route_t, slots, hs, seq_len, per_seq):
    t, d = x.shape
    n_steps = t // MOE_TM
    grid_spec = pltpu.PrefetchScalarGridSpec(
        num_scalar_prefetch=1,
        grid=(n_steps,),
        in_specs=[pl.BlockSpec((MOE_TM, d), lambda i, s: (i, 0)),
                  pl.BlockSpec((1, d), lambda i, s: (0, 0)),
                  _mod_spec(4, MOE_TM, seq_len, per_seq),
                  _mod_spec(3, MOE_TM, seq_len, per_seq),
                  pl.BlockSpec((MOE_TM, ROUTE_ROWS), lambda i, s: (i, 0)),
                  pl.BlockSpec(memory_space=pl.ANY)],
        out_specs=pl.BlockSpec(memory_space=pl.ANY),
        scratch_shapes=[pltpu.VMEM((2, MOE_TM, ROW_W), F32), pltpu.SemaphoreType.DMA((2,))],
    )
    return pl.pallas_call(
        functools.partial(_dispatch_kernel, n_steps),
        grid_spec=grid_spec,
        out_shape=jax.ShapeDtypeStruct(hs.shape, F32),
        input_output_aliases={6: 0},
        compiler_params=_cparams("arbitrary"),
        name="moe_dispatch",
    )(slots, x, g.reshape(1, d), mod, mod, route_t, hs)


def _experts_kernel(layer, ea_ref, eb_ref, chg_ref, nxt_ref, more_ref, par_ref, nused_ref,
                    hs_ref, wg_hbm, wu_hbm, wd_hbm, ys_ref, fg, fu, fd, bgu, bd, sem):
    def weight_copies(slot, expert, par):
        return [pltpu.make_async_copy(src.at[layer, expert], dst.at[slot, par], sem.at[slot, par])
                for src, dst in ((wg_hbm, fg), (wu_hbm, fu), (wd_hbm, fd))]

    def tile(k, rows):
        @pl.when(k < nused_ref[0])
        def _():
            for slot, e_ref in enumerate((ea_ref, eb_ref)):
                @pl.when(chg_ref[slot, k] == 1)
                def _():
                    par = par_ref[slot, k]

                    @pl.when(k == 0)
                    def _():
                        for cp in weight_copies(slot, e_ref[0], par):
                            cp.start()

                    for cp in weight_copies(slot, e_ref[k], par):
                        cp.wait()
                    bgu[slot, :, 0:D_EXPERT] = fg[slot, par].astype(BF16)
                    bgu[slot, :, D_EXPERT:] = fu[slot, par].astype(BF16)
                    bd[slot] = fd[slot, par].astype(BF16)

                    @pl.when(more_ref[slot, k] == 1)
                    def _():
                        for cp in weight_copies(slot, nxt_ref[slot, k], 1 - par):
                            cp.start()

            h = hs_ref[rows, 0:D_MODEL].astype(BF16)

            def ffn(slot):
                hid_up = _dot(h, bgu[slot])
                hid, up = hid_up[:, 0:D_EXPERT], hid_up[:, D_EXPERT:]
                w = hs_ref[rows, D_MODEL + 1 + slot:D_MODEL + 2 + slot]
                act = (hid * _sigmoid(hid)) * up * w
                return _dot(act.astype(BF16), bd[slot])

            ys_ref[rows, :] = ffn(0) + ffn(1)

        @pl.when(k >= nused_ref[0])
        def _():
            ys_ref[rows, :] = jnp.zeros((MOE_TS, ys_ref.shape[1]), ys_ref.dtype)

    for j in range(MOE_STEP_TILES):
        tile(pl.program_id(0) * MOE_STEP_TILES + j, slice(j * MOE_TS, (j + 1) * MOE_TS))


def moe_experts(hs, maps, layer, w_gate, w_up, w_down):
    n_tiles = hs.shape[0] // MOE_TS
    d = D_MODEL

    grid_spec = pltpu.PrefetchScalarGridSpec(
        num_scalar_prefetch=7,
        grid=(n_tiles // MOE_STEP_TILES,),
        in_specs=[pl.BlockSpec((MOE_STEP_TILES * MOE_TS, ROW_W), lambda k, *_: (k, 0)),
                  pl.BlockSpec(memory_space=pl.ANY), pl.BlockSpec(memory_space=pl.ANY),
                  pl.BlockSpec(memory_space=pl.ANY)],
        out_specs=pl.BlockSpec((MOE_STEP_TILES * MOE_TS, d), lambda k, *_: (k, 0)),
        scratch_shapes=[pltpu.VMEM((2, 2, d, D_EXPERT), F32), pltpu.VMEM((2, 2, d, D_EXPERT), F32),
                        pltpu.VMEM((2, 2, D_EXPERT, d), F32),
                        pltpu.VMEM((2, d, 2 * D_EXPERT), BF16), pltpu.VMEM((2, D_EXPERT, d), BF16),
                        pltpu.SemaphoreType.DMA((2, 2))],
    )
    return pl.pallas_call(
        functools.partial(_experts_kernel, layer),
        grid_spec=grid_spec,
        out_shape=jax.ShapeDtypeStruct((hs.shape[0], d), F32),
        compiler_params=_cparams("arbitrary"),
        name="moe_experts",
    )(*maps, hs, w_gate, w_up, w_down)


def _combine_kernel(final, n_steps, slots_ref, x_ref, g2_ref, fg_ref, ys_hbm, o_ref, gbuf, sem):
    i = pl.program_id(0)
    cur = i % 2

    def issue_tile(tile, s):
        for r in range(MOE_TM):
            src = slots_ref[tile * MOE_TM + r]
            pltpu.make_async_copy(ys_hbm.at[src], gbuf.at[s, r], sem.at[s]).start()

    @pl.when(i == 0)
    def _():
        issue_tile(0, 0)

    for s in range(2):
        @pl.when((i + 1 < n_steps) & (1 - cur == s))
        def _():
            issue_tile(i + 1, s)

    pltpu.make_async_copy(gbuf.at[cur], gbuf.at[cur], sem.at[cur]).wait()
    y = x_ref[...] + g2_ref[0] * gbuf[cur]
    if final:
        ms = jnp.mean(y * y, axis=-1, keepdims=True)
        y = y * lax.rsqrt(ms + EPS) * fg_ref[...]
    o_ref[...] = y


def moe_combine(x, mod, slots, ys, final_g, final, seq_len, per_seq):
    t, d = x.shape
    n_steps = t // MOE_TM
    grid_spec = pltpu.PrefetchScalarGridSpec(
        num_scalar_prefetch=1,
        grid=(n_steps,),
        in_specs=[pl.BlockSpec((MOE_TM, d), lambda i, s: (i, 0)),
                  _mod_spec(5, MOE_TM, seq_len, per_seq),
                  pl.BlockSpec((1, d), lambda i, s: (0, 0)),
                  pl.BlockSpec(memory_space=pl.ANY)],
        out_specs=pl.BlockSpec((MOE_TM, d), lambda i, s: (i, 0)),
        scratch_shapes=[pltpu.VMEM((2, MOE_TM, d), F32), pltpu.SemaphoreType.DMA((2,))],
    )
    return pl.pallas_call(
        functools.partial(_combine_kernel, final, n_steps),
        grid_spec=grid_spec,
        out_shape=jax.ShapeDtypeStruct((t, d), F32),
        compiler_params=_cparams("arbitrary"),
        name="moe_combine",
    )(slots, x, mod, final_g.reshape(1, d), ys)


def _combine_nm_tm_kernel(nseq, steps, seq_len, n_steps, slots_ref, x_ref, g2_ref, ys_hbm, g_ref, sc_ref,
                          sh_ref, w_ref, x2_ref, u_ref, gbuf, sem):
    i = pl.program_id(0)
    cur = i % 2

    def issue_tile(tile, s):
        for sq in range(nseq):
            for t in range(steps):
                src = slots_ref[sq * seq_len + tile * steps + t]
                pltpu.make_async_copy(ys_hbm.at[src], gbuf.at[s, sq * steps + t], sem.at[s]).start()

    @pl.when(i == 0)
    def _():
        issue_tile(0, 0)

    for s in range(2):
        @pl.when((i + 1 < n_steps) & (1 - cur == s))
        def _():
            issue_tile(i + 1, s)

    pltpu.make_async_copy(gbuf.at[cur], gbuf.at[cur], sem.at[cur]).wait()
    d = x_ref.shape[-1]
    y = x_ref[...] + g2_ref[...] * gbuf[cur].reshape(nseq, steps, d)
    x2_ref[...] = y
    h = _normmod(y, g_ref[...], sc_ref[...], sh_ref[...])
    h = h.reshape(nseq * steps, d).astype(BF16)
    h = _dot(_row_permutation(nseq, steps, True), h).astype(BF16)
    u_ref[...] = _dot(h, w_ref[...])


def combine_nm_matmul_tm(x, mod_prev, slots, ys, g, mod, w, nseq, seq_len, per_seq):
    t, d = x.shape
    n = w.shape[1]
    steps = TM_ROWS // nseq
    n_steps = seq_len // steps
    g2 = _group_mod(mod_prev, 5, nseq, per_seq)
    sc = _group_mod(mod, 1, nseq, per_seq)
    sh = _group_mod(mod, 0, nseq, per_seq)
    mod_spec = pl.BlockSpec(sc.shape, lambda i, s: (0, 0, 0))
    x_spec = pl.BlockSpec((nseq, steps, d), lambda i, s: (0, i, 0))
    grid_spec = pltpu.PrefetchScalarGridSpec(
        num_scalar_prefetch=1,
        grid=(n_steps,),
        in_specs=[x_spec, mod_spec, pl.BlockSpec(memory_space=pl.ANY),
                  pl.BlockSpec((1, d), lambda i, s: (0, 0)), mod_spec, mod_spec,
                  pl.BlockSpec((d, n), lambda i, s: (0, 0))],
        out_specs=[x_spec, pl.BlockSpec((TM_ROWS, n), lambda i, s: (i, 0))],
        scratch_shapes=[pltpu.VMEM((2, TM_ROWS, d), F32), pltpu.SemaphoreType.DMA((2,))],
    )
    x2, u = pl.pallas_call(
        functools.partial(_combine_nm_tm_kernel, nseq, steps, seq_len, n_steps),
        grid_spec=grid_spec,
        out_shape=[jax.ShapeDtypeStruct((nseq, seq_len, d), F32), jax.ShapeDtypeStruct((t, n), F32)],
        compiler_params=_cparams("arbitrary"),
        name="moe_combine_norm_mod_proj_tm",
    )(slots, x.reshape(nseq, seq_len, d), g2, ys, g.reshape(1, d), sc, sh, w)
    return x2.reshape(t, d), u


def moe_block(xs, routes, mods_l, per_seqs, seq_lens, layer, p, final, defer_combine, hs):
    g = p['norm_g'][layer, 1]
    slots, off, cnt = moe_slots(jnp.concatenate(routes, axis=1))
    t_all = slots.shape[1]
    n_tiles = -(-(t_all // MOE_TS + N_CLS) // MOE_STEP_TILES) * MOE_STEP_TILES
    maps = _tile_maps(off, cnt, n_tiles)
    if hs is None:
        hs = jnp.zeros((n_tiles * MOE_TS, ROW_W), F32)
    bounds = np.cumsum([0] + [x.shape[0] for x in xs])
    group_slots = [slots[0, bounds[i]:bounds[i + 1]] for i in range(len(xs))]
    for x, r, s, sl, ps in zip(xs, routes, group_slots, seq_lens, per_seqs):
        hs = moe_dispatch(x, g, mods_l, r.T, s, hs, sl, ps)
    ys = moe_experts(hs, maps, layer, p['moe_w_gate'], p['moe_w_up'], p['moe_w_down'])
    if defer_combine:
        return [(x, s, ys) for x, s in zip(xs, group_slots)], hs
    return [moe_combine(x, mods_l, s, ys, p['final_g'], final, sl, ps)
            for x, s, sl, ps in zip(xs, group_slots, seq_lens, per_seqs)], hs


def _mixer(x, group, l, mod, p, hy_tables, mod_prev):
    per_seq, nseq, seq_len = group['per_seq'], group['nseq'], group['seq_len']
    extras = None
    deferred = isinstance(x, tuple)
    if deferred and l % 2 == 0:
        x = moe_combine(x[0], mod_prev, x[1], x[2], p['final_g'], False, seq_len, per_seq)
        deferred = False
    if l % 2 == 0:
        e = l // 2
        u = nm_matmul(x, p['norm_g'][l, 0], mod, p['a_in_w'][e].astype(BF16), seq_len, per_seq)
        if group['ctx_k'] is None:
            attn, nk, nv = context_attention(u, nseq, seq_len)
            extras = (nk, nv)
        else:
            attn = neighbourhood_attention(u, group['ctx_k'][:, e], group['ctx_v'][:, e], p['na_rpb'][e],
                                           nseq, seq_len)
        fwd, inv = hy_tables[seq_len]
        spectrum = hyena_spectrum(seq_len, p['hy_w1'][e], p['hy_b1'][e], p['hy_w2'][e], p['hy_b2'][e],
                                  p['hy_w3'][e], p['hy_freq'][e], p['hy_d'][e], fwd)
        hy = hyena_mixer(u, nseq, seq_len, p['hy_short_w'][e], p['hy_short_b'][e], spectrum, fwd, inv)
        w_out = p['a_out_w'][e].astype(BF16)
        x, route = proj_residual([attn, hy], [w_out[:D_A], w_out[D_A:]], x, mod, p['norm_g'][l, 1],
                                 p['router_w'], p['router_b'], seq_len, per_seq)
    else:
        o = l // 2
        w_in = p['c_in_w'][o].astype(BF16)
        if deferred:
            x, u = combine_nm_matmul_tm(x[0], mod_prev, x[1], x[2], p['norm_g'][l, 0], mod, w_in,
                                        nseq, seq_len, per_seq)
        else:
            u = nm_matmul_tm(x, p['norm_g'][l, 0], mod, w_in, nseq, seq_len, per_seq)
        y, extras = rglru_block(u, nseq, seq_len, p['rg_conv_w'][o], p['rg_conv_b'][o], p['rg_wa'][o],
                                p['rg_ba'][o], p['rg_wx'][o], p['rg_bx'][o], p['rg_lam'][o], group['h0'][o])
        x, route = proj_residual_tm(y, p['c_out_w'][o].astype(BF16), x, mod, p['norm_g'][l, 1],
                                    p['router_w'], p['router_b'], nseq, seq_len, per_seq)
    return x, extras, route


def kernel(x_prompt, x_sample, cache_k, cache_v, state_h, c, c_ctx, norm_g, ada_w, ada_b, final_g, a_in_w, a_out_w, na_rpb, hy_short_w, hy_short_b, hy_w1, hy_b1, hy_w2, hy_b2, hy_w3, hy_freq, hy_d, c_in_w, c_out_w, rg_conv_w, rg_conv_b, rg_wa, rg_ba, rg_wx, rg_bx, rg_lam, router_w, router_b, moe_w_gate, moe_w_up, moe_w_down):
    p = dict(norm_g=norm_g, final_g=final_g, a_in_w=a_in_w, a_out_w=a_out_w, na_rpb=na_rpb,
             hy_short_w=hy_short_w, hy_short_b=hy_short_b, hy_w1=hy_w1, hy_b1=hy_b1, hy_w2=hy_w2,
             hy_b2=hy_b2, hy_w3=hy_w3, hy_freq=hy_freq, hy_d=hy_d, c_in_w=c_in_w, c_out_w=c_out_w,
             rg_conv_w=rg_conv_w, rg_conv_b=rg_conv_b, rg_wa=rg_wa, rg_ba=rg_ba, rg_wx=rg_wx, rg_bx=rg_bx,
             rg_lam=rg_lam, router_w=router_w, router_b=router_b, moe_w_gate=moe_w_gate,
             moe_w_up=moe_w_up, moe_w_down=moe_w_down)
    batch, seq, d = x_prompt.shape
    dec_batch, dec_seq, _ = x_sample.shape
    n_odd = DEPTH // 2
    assert 1 + dec_batch <= MOD_ROWS

    cond = jnp.concatenate([c_ctx[None, :], c, jnp.zeros((MOD_ROWS - 1 - dec_batch, d), F32)], axis=0)
    m = modulation(cond, ada_w, ada_b)
    mods = [m[l].reshape(MOD_ROWS * N_MOD, 1, d) for l in range(DEPTH)]

    tables = {}
    for sl in (seq, dec_seq):
        fwd, inv = _dft_tables(sl)
        tables[sl] = (jnp.asarray(fwd).astype(BF16), jnp.asarray(inv).astype(BF16))

    groups = [
        dict(per_seq=False, nseq=batch, seq_len=seq, ctx_k=None, ctx_v=None,
             h0=[jnp.zeros((2, batch, D_RNN), F32)] * n_odd),
        dict(per_seq=True, nseq=dec_batch, seq_len=dec_seq, ctx_k=cache_k, ctx_v=cache_v,
             h0=[state_h[:, o].transpose(1, 0, 2) for o in range(n_odd)]),
    ]
    xs = [x_prompt.reshape(batch * seq, d), x_sample.reshape(dec_batch * dec_seq, d)]
    k_list, v_list, h_list = [], [], []
    hs = None
    for l in range(DEPTH):
        mixed = [_mixer(x, grp, l, mods[l], p, tables, mods[l - 1] if l else None)
                 for x, grp in zip(xs, groups)]
        if l % 2 == 0:
            k_list.append(mixed[0][1][0])
            v_list.append(mixed[0][1][1])
        else:
            h_list.append(mixed[0][1].transpose(1, 0, 2))
        xs, hs = moe_block([mx[0] for mx in mixed], [mx[2] for mx in mixed], mods[l],
                           [grp['per_seq'] for grp in groups], [grp['seq_len'] for grp in groups],
                           l, p, l == DEPTH - 1, l < DEPTH - 1, hs)
    new_k = jnp.stack(k_list, axis=1)
    new_v = jnp.stack(v_list, axis=1)
    new_h = jnp.stack(h_list, axis=1)
    return (xs[0].reshape(batch, seq, d), xs[1].reshape(dec_batch, dec_seq, d), new_k, new_v, new_h)
```

```python
import functools
import math

import numpy as np
import jax
import jax.numpy as jnp
from jax import lax
from jax.experimental import pallas as pl
from jax.experimental.pallas import tpu as pltpu

F32 = jnp.float32
BF16 = jnp.bfloat16

D_MODEL = 1024
DEPTH = 2
GRID_W = 64
EPS = 1e-6
NEG_INF = -1e30
NA_HEADS = 8
HEAD_DIM = 64
D_A = NA_HEADS * HEAD_DIM
WIN_ROWS = 8
WIN_COLS = 16
D_B = D_MODEL - D_A
HY_ORDER = 2
HY_EMB = 33
HY_BANDS = (HY_EMB - 1) // 2
HY_FFN = 64
HY_DECAY_TARGET = 1e-2
HY_FAST_PCT = 0.3
HY_SLOW_PCT = 1.5
D_RNN = D_MODEL
RG_BLOCK = 64
RG_C = 8.0
N_EXPERTS = 16
N_GROUPS = 4
EXPERTS_PER_GROUP = N_EXPERTS // N_GROUPS
D_EXPERT = 512

LANES = 128
VMEM_LIMIT = 56 * 1024 * 1024
N_MOD = 6
MOD_ROWS = 16


def _cparams(*sem):
    return pltpu.CompilerParams(dimension_semantics=sem, vmem_limit_bytes=VMEM_LIMIT)


def _dot(a, b):
    return jnp.dot(a, b, preferred_element_type=F32)


def _dot_nt(a, b):
    return lax.dot_general(a, b, (((1,), (1,)), ((), ())), preferred_element_type=F32)


def _sigmoid(x):
    return 0.5 * jnp.tanh(0.5 * x) + 0.5


def _normmod(x, g, sc, sh):
    ms = jnp.mean(x * x, axis=-1, keepdims=True)
    return (x * lax.rsqrt(ms + EPS) * g) * (1.0 + sc) + sh


def _mod_spec(chunk, tm, seq_len, per_seq):
    if per_seq:
        return pl.BlockSpec((1, 1, D_MODEL), lambda i, *_: ((1 + (i * tm) // seq_len) * N_MOD + chunk, 0, 0))
    return pl.BlockSpec((1, 1, D_MODEL), lambda i, *_: (chunk, 0, 0))


def _mod_kernel(c_ref, w_ref, b_ref, o_ref):
    s = c_ref[...]
    s = s * jax.nn.sigmoid(s)
    o_ref[0] = _dot(s.astype(BF16), w_ref[0].astype(BF16)) + b_ref[0]


def modulation(cond, ada_w, ada_b):
    tn = 1536
    n = ada_w.shape[-1]
    return pl.pallas_call(
        _mod_kernel,
        grid=(DEPTH, n // tn),
        in_specs=[pl.BlockSpec((MOD_ROWS, D_MODEL), lambda l, j: (0, 0)),
                  pl.BlockSpec((1, D_MODEL, tn), lambda l, j: (l, 0, j)),
                  pl.BlockSpec((1, 1, tn), lambda l, j: (l, 0, j))],
        out_specs=pl.BlockSpec((1, MOD_ROWS, tn), lambda l, j: (l, 0, j)),
        out_shape=jax.ShapeDtypeStruct((DEPTH, MOD_ROWS, n), F32),
        compiler_params=_cparams("arbitrary", "arbitrary"),
        name="modulation",
    )(cond, ada_w, ada_b.reshape(DEPTH, 1, n))


def _nm_matmul_kernel(x_ref, g_ref, sc_ref, sh_ref, w_ref, o_ref):
    h = _normmod(x_ref[...], g_ref[...], sc_ref[0], sh_ref[0])
    o_ref[...] = _dot(h.astype(BF16), w_ref[...])


def nm_matmul(x, g, mod, w, seq_len, per_seq, tm=512):
    t, d = x.shape
    n = w.shape[1]
    return pl.pallas_call(
        _nm_matmul_kernel,
        grid=(t // tm,),
        in_specs=[pl.BlockSpec((tm, d), lambda i: (i, 0)),
                  pl.BlockSpec((1, d), lambda i: (0, 0)),
                  _mod_spec(1, tm, seq_len, per_seq),
                  _mod_spec(0, tm, seq_len, per_seq),
                  pl.BlockSpec((d, n), lambda i: (0, 0))],
        out_specs=pl.BlockSpec((tm, n), lambda i: (i, 0)),
        out_shape=jax.ShapeDtypeStruct((t, n), F32),
        compiler_params=_cparams("arbitrary"),
        name="norm_mod_proj",
    )(x, g.reshape(1, d), mod, mod, w)


def _proj_res_kernel(n_act, *refs):
    acts = refs[:n_act]
    ws = refs[n_act:2 * n_act]
    x_ref, g_ref, ng_ref, sc2_ref, sh2_ref, rw_ref, rb_ref, o_ref, route_ref = refs[2 * n_act:]
    acc = _dot(acts[0][...].astype(BF16), ws[0][...])
    for a, w in zip(acts[1:], ws[1:]):
        acc += _dot(a[...].astype(BF16), w[...])
    x = x_ref[...] + g_ref[0] * acc
    o_ref[...] = x
    route_ref[...] = _route_record(_normmod(x, ng_ref[...], sc2_ref[0], sh2_ref[0]), rw_ref[...], rb_ref[...])


def proj_residual(acts, ws, x, mod, norm2_g, router_w, router_b, seq_len, per_seq, tm=512):
    t, d = x.shape
    in_specs = [pl.BlockSpec((tm, a.shape[1]), lambda i: (i, 0)) for a in acts]
    in_specs += [pl.BlockSpec(w.shape, lambda i: (0, 0)) for w in ws]
    in_specs += [pl.BlockSpec((tm, d), lambda i: (i, 0)), _mod_spec(2, tm, seq_len, per_seq),
                 pl.BlockSpec((1, d), lambda i: (0, 0)),
                 _mod_spec(4, tm, seq_len, per_seq), _mod_spec(3, tm, seq_len, per_seq),
                 pl.BlockSpec((N_EXPERTS, d), lambda i: (0, 0)),
                 pl.BlockSpec((N_EXPERTS, 1), lambda i: (0, 0))]
    return pl.pallas_call(
        functools.partial(_proj_res_kernel, len(acts)),
        grid=(t // tm,),
        in_specs=in_specs,
        out_specs=[pl.BlockSpec((tm, d), lambda i: (i, 0)), pl.BlockSpec((ROUTE_ROWS, tm), lambda i: (0, i))],
        out_shape=[jax.ShapeDtypeStruct((t, d), F32), jax.ShapeDtypeStruct((ROUTE_ROWS, t), F32)],
        compiler_params=_cparams("arbitrary"),
        name="proj_residual_route",
    )(*acts, *ws, x, mod, norm2_g.reshape(1, d), mod, mod, router_w.T, router_b.reshape(N_EXPERTS, 1))


def _row_permutation(nseq, steps, to_time_major):
    n = nseq * steps
    i = lax.broadcasted_iota(jnp.int32, (n, n), 0)
    j = lax.broadcasted_iota(jnp.int32, (n, n), 1)
    if to_time_major:
        src = (i % nseq) * steps + i // nseq
    else:
        src = (i % steps) * nseq + i // steps
    return (j == src).astype(BF16)


def _nm_matmul_tm_kernel(nseq, steps, x_ref, g_ref, sc_ref, sh_ref, w_ref, o_ref):
    h = _normmod(x_ref[...], g_ref[...], sc_ref[...], sh_ref[...])
    h = h.reshape(nseq * steps, h.shape[-1]).astype(BF16)
    h = _dot(_row_permutation(nseq, steps, True), h).astype(BF16)
    o_ref[...] = _dot(h, w_ref[...])


def _group_mod(mod, chunk, nseq, per_seq):
    rows = mod.reshape(MOD_ROWS, N_MOD, 1, D_MODEL)
    return rows[1:1 + nseq, chunk] if per_seq else rows[0:1, chunk]


def nm_matmul_tm(x, g, mod, w, nseq, seq_len, per_seq):
    t, d = x.shape
    n = w.shape[1]
    steps = TM_ROWS // nseq
    sc = _group_mod(mod, 1, nseq, per_seq)
    sh = _group_mod(mod, 0, nseq, per_seq)
    mod_spec = pl.BlockSpec(sc.shape, lambda i: (0, 0, 0))
    return pl.pallas_call(
        functools.partial(_nm_matmul_tm_kernel, nseq, steps),
        grid=(seq_len // steps,),
        in_specs=[pl.BlockSpec((nseq, steps, d), lambda i: (0, i, 0)),
                  pl.BlockSpec((1, d), lambda i: (0, 0)),
                  mod_spec, mod_spec,
                  pl.BlockSpec((d, n), lambda i: (0, 0))],
        out_specs=pl.BlockSpec((TM_ROWS, n), lambda i: (i, 0)),
        out_shape=jax.ShapeDtypeStruct((t, n), F32),
        compiler_params=_cparams("arbitrary"),
        name="norm_mod_proj_tm",
    )(x.reshape(nseq, seq_len, d), g.reshape(1, d), sc, sh, w)


def _proj_res_tm_kernel(nseq, steps, y_ref, w_ref, x_ref, g_ref, ng_ref, sc2_ref, sh2_ref, rw_ref, rb_ref,
                        o_ref, route_ref):
    y = _dot(_row_permutation(nseq, steps, False), y_ref[...].astype(BF16)).astype(BF16)
    acc = _dot(y, w_ref[...])
    x = x_ref[...] + g_ref[...] * acc.reshape(nseq, steps, acc.shape[-1])
    o_ref[...] = x
    h = _normmod(x, ng_ref[...], sc2_ref[...], sh2_ref[...]).reshape(nseq * steps, x.shape[-1])
    route_ref[0] = _route_record(h, rw_ref[...], rb_ref[...])


def proj_residual_tm(y, w, x, mod, norm2_g, router_w, router_b, nseq, seq_len, per_seq):
    t, d = x.shape
    steps = TM_ROWS // nseq
    n_steps = seq_len // steps
    g1 = _group_mod(mod, 2, nseq, per_seq)
    sc2 = _group_mod(mod, 4, nseq, per_seq)
    sh2 = _group_mod(mod, 3, nseq, per_seq)
    mod_spec = pl.BlockSpec(g1.shape, lambda i: (0, 0, 0))
    out, route = pl.pallas_call(
        functools.partial(_proj_res_tm_kernel, nseq, steps),
        grid=(n_steps,),
        in_specs=[pl.BlockSpec((TM_ROWS, y.shape[1]), lambda i: (i, 0)),
                  pl.BlockSpec(w.shape, lambda i: (0, 0)),
                  pl.BlockSpec((nseq, steps, d), lambda i: (0, i, 0)),
                  mod_spec,
                  pl.BlockSpec((1, d), lambda i: (0, 0)),
                  mod_spec, mod_spec,
                  pl.BlockSpec((N_EXPERTS, d), lambda i: (0, 0)),
                  pl.BlockSpec((N_EXPERTS, 1), lambda i: (0, 0))],
        out_specs=[pl.BlockSpec((nseq, steps, d), lambda i: (0, i, 0)),
                   pl.BlockSpec((1, ROUTE_ROWS, TM_ROWS), lambda i: (i, 0, 0))],
        out_shape=[jax.ShapeDtypeStruct((nseq, seq_len, d), F32),
                   jax.ShapeDtypeStruct((n_steps, ROUTE_ROWS, TM_ROWS), F32)],
        compiler_params=_cparams("arbitrary"),
        name="proj_residual_tm_route",
    )(y, w, x.reshape(nseq, seq_len, d), g1, norm2_g.reshape(1, d), sc2, sh2, router_w.T,
      router_b.reshape(N_EXPERTS, 1))
    route = route.reshape(n_steps, ROUTE_ROWS, nseq, steps).transpose(1, 2, 0, 3).reshape(ROUTE_ROWS, t)
    return out.reshape(t, d), route


CTX_STEP_ROWS = 1024


def _ctx_attn_kernel(seq_len, n_seq, q_ref, k_ref, v_ref, o_ref, nk_ref, nv_ref):
    scale = HEAD_DIM ** -0.5
    per_tile = LANES // HEAD_DIM
    lane = lax.broadcasted_iota(jnp.int32, (1, LANES), 1)
    for b in range(n_seq):
        rows = slice(b * seq_len, (b + 1) * seq_len)
        for hp in range(NA_HEADS // per_tile):
            sl = slice(hp * LANES, (hp + 1) * LANES)
            q, k, v = q_ref[rows, sl] * scale, k_ref[rows, sl], v_ref[rows, sl]
            kb, vb = k.astype(BF16), v.astype(BF16)
            out = None
            for j in range(per_tile):
                h = hp * per_tile + j
                nk_ref[b, h] = k[:, j * HEAD_DIM:(j + 1) * HEAD_DIM]
                nv_ref[b, h] = v[:, j * HEAD_DIM:(j + 1) * HEAD_DIM]
                mine = lane // HEAD_DIM == j
                s = _dot_nt(jnp.where(mine, q, 0.0).astype(BF16), kb)
                p = jnp.exp(s - jnp.max(s, axis=-1, keepdims=True))
                o = _dot(p.astype(BF16), vb) / jnp.sum(p, axis=-1, keepdims=True)
                out = o if out is None else jnp.where(mine, o, out)
            o_ref[rows, sl] = out.astype(o_ref.dtype)


def context_attention(u, nseq, seq_len):
    t = u.shape[0]
    per_step = max(1, CTX_STEP_ROWS // seq_len)
    assert nseq % per_step == 0
    rows = per_step * seq_len
    kv_shape = jax.ShapeDtypeStruct((nseq, NA_HEADS, seq_len, HEAD_DIM), F32)
    kv_spec = pl.BlockSpec((per_step, NA_HEADS, seq_len, HEAD_DIM), lambda b: (b, 0, 0, 0))
    return pl.pallas_call(
        functools.partial(_ctx_attn_kernel, seq_len, per_step),
        grid=(nseq // per_step,),
        in_specs=[pl.BlockSpec((rows, D_A), lambda b: (b, 0)),
                  pl.BlockSpec((rows, D_A), lambda b: (b, 1)),
                  pl.BlockSpec((rows, D_A), lambda b: (b, 2))],
        out_specs=[pl.BlockSpec((rows, D_A), lambda b: (b, 0)), kv_spec, kv_spec],
        out_shape=[jax.ShapeDtypeStruct((t, D_A), BF16), kv_shape, kv_shape],
        compiler_params=_cparams("arbitrary"),
        name="context_attention",
    )(u, u, u)


N_DR = 2 * WIN_ROWS - 1
N_DC = 2 * WIN_COLS - 1


def _na_col_tables():
    cols = np.arange(GRID_W)
    col_start = np.clip(cols - WIN_COLS // 2, 0, GRID_W - WIN_COLS)
    col_in = (cols[None, :] >= col_start[:, None]) & (cols[None, :] < col_start[:, None] + WIN_COLS)
    dc = np.clip(cols[None, :] - cols[:, None], 1 - WIN_COLS, WIN_COLS - 1) + WIN_COLS - 1
    onehot = (dc.reshape(1, -1) == np.arange(32)[:, None]).astype(np.float32)
    return onehot, col_in.reshape(1, -1).astype(np.float32)


def _na_bias_kernel(r_ref, e_ref, m_ref, o_ref):
    t = jnp.dot(r_ref[...], e_ref[...], precision=lax.Precision.HIGHEST, preferred_element_type=F32)
    o_ref[...] = jnp.where(m_ref[...] > 0.0, t, NEG_INF)


def na_bias_table(rpb):
    onehot, col_in = _na_col_tables()
    n_rows = NA_HEADS * N_DR
    r = jnp.zeros((LANES, 32), F32).at[:n_rows, :N_DC].set(rpb.reshape(n_rows, N_DC).astype(F32))
    t = pl.pallas_call(
        _na_bias_kernel,
        out_shape=jax.ShapeDtypeStruct((LANES, GRID_W * GRID_W), F32),
        name="na_bias_table",
    )(r, jnp.asarray(onehot), jnp.asarray(col_in))
    t = t[:n_rows].reshape(NA_HEADS, N_DR, GRID_W, GRID_W)
    return jnp.concatenate([t[:, :-1], t[:, 1:]], axis=-1)


def _na_kernel(rows, q_ref, k_ref, v_ref, ck_ref, cv_ref, bias_ref, o_ref,
               q_s, k_s, v_s, ck_s, cv_s, s_s, p_s, den_s, o_s):
    scale = HEAD_DIM ** -0.5
    n_lat = WIN_ROWS * GRID_W
    per_tile = LANES // HEAD_DIM
    n_pairs = NA_HEADS // per_tile
    lane = lax.broadcasted_iota(jnp.int32, (1, LANES), 1)
    for hp in range(n_pairs):
        sl = slice(hp * LANES, (hp + 1) * LANES)
        q_s[hp] = (q_ref[:, sl] * scale).astype(BF16)
        k_s[hp] = k_ref[:, sl].astype(BF16)
        v_s[hp] = v_ref[:, sl].astype(BF16)
        heads = range(hp * per_tile, (hp + 1) * per_tile)
        ck_s[hp] = jnp.concatenate([ck_ref[0, h] for h in heads], axis=1).astype(BF16)
        cv_s[hp] = jnp.concatenate([cv_ref[0, h] for h in heads], axis=1).astype(BF16)

    def window(r):
        start = min(max(r - WIN_ROWS // 2, 0), rows - WIN_ROWS)
        return start, start - r + WIN_ROWS - 1

    def pair_body(hp, carry):
        for j in range(per_tile):
            h = hp * per_tile + j
            mine = lane // HEAD_DIM == j
            for r in range(rows):
                start, off = window(r)
                rs = slice(r * GRID_W, (r + 1) * GRID_W)
                q = jnp.where(mine, q_s[hp, rs, :], 0.0).astype(BF16)
                bias = jnp.concatenate([bias_ref[h, off + 2 * i] for i in range(WIN_ROWS // 2)], axis=1)
                s_s[rs, 0:n_lat] = _dot_nt(q, k_s[hp, start * GRID_W:start * GRID_W + n_lat, :]) + bias
                s_s[rs, n_lat:] = _dot_nt(q, ck_s[hp])
            for r in range(rows):
                rs = slice(r * GRID_W, (r + 1) * GRID_W)
                s = s_s[rs, :]
                p = jnp.exp(s - jnp.max(s, axis=-1, keepdims=True))
                den_s[rs, :] = jnp.sum(p, axis=-1, keepdims=True)
                p_s[rs, :] = p.astype(BF16)
            for r in range(rows):
                start, _ = window(r)
                rs = slice(r * GRID_W, (r + 1) * GRID_W)
                o = (_dot(p_s[rs, 0:n_lat], v_s[hp, start * GRID_W:start * GRID_W + n_lat, :])
                     + _dot(p_s[rs, n_lat:], cv_s[hp])) / den_s[rs, :]
                o_s[hp, rs, :] = o if j == 0 else jnp.where(mine, o, o_s[hp, rs, :])
        return carry

    lax.fori_loop(0, n_pairs, pair_body, 0)
    for hp in range(n_pairs):
        o_ref[:, hp * LANES:(hp + 1) * LANES] = o_s[hp].astype(o_ref.dtype)


def neighbourhood_attention(u, ctx_k, ctx_v, rpb, nseq, seq_len):
    t = u.shape[0]
    rows = seq_len // GRID_W
    assert rows >= WIN_ROWS and WIN_ROWS % 2 == 0
    past = ctx_k.shape[2]
    bias = na_bias_table(rpb)
    ctx_spec = pl.BlockSpec((1, NA_HEADS, past, HEAD_DIM), lambda b: (b, 0, 0, 0))
    return pl.pallas_call(
        functools.partial(_na_kernel, rows),
        grid=(nseq,),
        in_specs=[pl.BlockSpec((seq_len, D_A), lambda b: (b, 0)),
                  pl.BlockSpec((seq_len, D_A), lambda b: (b, 1)),
                  pl.BlockSpec((seq_len, D_A), lambda b: (b, 2)),
                  ctx_spec, ctx_spec,
                  pl.BlockSpec(bias.shape, lambda b: (0, 0, 0, 0))],
        out_specs=pl.BlockSpec((seq_len, D_A), lambda b: (b, 0)),
        out_shape=jax.ShapeDtypeStruct((t, D_A), BF16),
        scratch_shapes=[pltpu.VMEM((D_A // LANES, seq_len, LANES), BF16)] * 3
        + [pltpu.VMEM((D_A // LANES, past, LANES), BF16)] * 2
        + [pltpu.VMEM((seq_len, WIN_ROWS * GRID_W + past), F32),
           pltpu.VMEM((seq_len, WIN_ROWS * GRID_W + past), BF16),
           pltpu.VMEM((seq_len, 1), F32),
           pltpu.VMEM((D_A // LANES, seq_len, LANES), F32)],
        compiler_params=_cparams("arbitrary"),
        name="neighbourhood_attention",
    )(u, u, u, ctx_k, ctx_v, bias)


HY_STEP_ROWS = 1024


def _dft_tables(seq_len):
    n = 2 * seq_len
    f = np.arange(seq_len, dtype=np.int64)
    ang = (np.outer(f, f) % n).astype(np.float64) * (math.pi / seq_len)
    cos, sin = np.cos(ang), np.sin(ang)
    alt = np.where(f % 2 == 0, 1.0, -1.0)
    s_fwd = -sin
    s_fwd[0, :] = alt
    fwd = np.concatenate([cos, s_fwd], axis=0)
    wf = np.where(f == 0, 1.0, 2.0) / n
    ci = cos.T * wf[None, :]
    si = -sin.T * wf[None, :]
    si[:, 0] = alt / n
    inv = np.concatenate([ci, si], axis=1)
    return fwd.astype(np.float32), inv.astype(np.float32)


def _hyena_feats(seq_len):
    t = np.linspace(0.0, 1.0, seq_len, dtype=np.float32)[:, None]
    w = (2.0 * math.pi * np.arange(seq_len, dtype=np.float32)[:, None] / seq_len).astype(np.float32)
    f = np.linspace(1e-4, HY_BANDS - 1, HY_BANDS, dtype=np.float32)[None, :]
    z = np.concatenate([t, np.cos(f * w), -np.sin(f * w)], axis=-1).astype(np.float32)
    max_decay = math.log(HY_DECAY_TARGET) / HY_FAST_PCT
    min_decay = math.log(HY_DECAY_TARGET) / HY_SLOW_PCT
    deltas = np.abs(np.linspace(min_decay, max_decay, D_B, dtype=np.float32))[None, :]
    return z, t, deltas


def _hy_filter_kernel(seq_len, z_ref, t_ref, dl_ref, w1_ref, b1_ref, w2_ref, b2_ref, w3_ref, fr_ref,
                      d_ref, fwd_ref, g_ref):
    hp = lax.Precision.HIGHEST
    h = jnp.sin(fr_ref[0:1, :] * (jnp.dot(z_ref[...], w1_ref[...], precision=hp) + b1_ref[...]))
    h = jnp.sin(fr_ref[1:2, :] * (jnp.dot(h, w2_ref[...], precision=hp) + b2_ref[...]))
    h = jnp.dot(h, w3_ref[...], precision=hp)
    decay = jnp.exp(-t_ref[...] * dl_ref[...])
    row0 = lax.broadcasted_iota(jnp.int32, (seq_len, D_B), 0) == 0
    sums, diffs = [], []
    for n in range(HY_ORDER):
        hf = h[:, (2 * n) * D_B:(2 * n + 1) * D_B] * decay
        hb = h[:, (2 * n + 1) * D_B:(2 * n + 2) * D_B] * decay
        gp = jnp.where(row0, hf + hb + d_ref[n:n + 1, :], hf)
        gm = jnp.where(row0, 0.0, hb)
        sums.append(gp + gm)
        diffs.append(gp - gm)
    rhs = jnp.concatenate(sums + diffs, axis=1).astype(BF16)
    spec = _dot(fwd_ref[...], rhs)
    for n in range(HY_ORDER):
        a = spec[:, n * D_B:(n + 1) * D_B]
        b = spec[:, (HY_ORDER + n) * D_B:(HY_ORDER + n + 1) * D_B]
        g_ref[n, 0:seq_len, :] = a[0:seq_len]
        g_ref[n, seq_len:, :] = jnp.where(row0, a[seq_len:], b[seq_len:])


def hyena_spectrum(seq_len, w1, b1, w2, b2, w3, freq, d, fwd):
    z, t, deltas = _hyena_feats(seq_len)
    return pl.pallas_call(
        functools.partial(_hy_filter_kernel, seq_len),
        out_shape=jax.ShapeDtypeStruct((HY_ORDER, 2 * seq_len, D_B), F32),
        compiler_params=pltpu.CompilerParams(vmem_limit_bytes=VMEM_LIMIT),
        name="hyena_spectrum",
    )(jnp.asarray(z), jnp.asarray(t), jnp.asarray(deltas), w1, b1.reshape(1, -1), w2, b2.reshape(1, -1),
      w3, freq, d, fwd)


def _hyena_kernel(seq_len, n_seq, u_ref, sw_ref, sb_ref, g_ref, fwd_ref, inv_ref, o_ref):
    t_idx = lax.broadcasted_iota(jnp.int32, (seq_len, u_ref.shape[1]), 0)
    row0 = lax.broadcasted_iota(jnp.int32, (seq_len, D_B), 0) == 0
    for s in range(n_seq):
        rows = slice(s * seq_len, (s + 1) * seq_len)
        u = u_ref[rows, :]
        prev = jnp.where(t_idx == 0, 0.0, pltpu.roll(u, 1, axis=0))
        nxt = jnp.where(t_idx == seq_len - 1, 0.0, pltpu.roll(u, seq_len - 1, axis=0))
        u = prev * sw_ref[0:1, :] + u * sw_ref[1:2, :] + nxt * sw_ref[2:3, :] + sb_ref[...]
        z = u[:, 0:D_B]
        for n in range(HY_ORDER):
            spec = _dot(fwd_ref[...], z.astype(BF16))
            ure, uim = spec[0:seq_len], spec[seq_len:]
            gre, gim = g_ref[n, 0:seq_len, :], g_ref[n, seq_len:, :]
            pim = uim * gim
            yre = ure * gre - jnp.where(row0, 0.0, pim)
            yim = jnp.where(row0, pim, ure * gim + uim * gre)
            y = jnp.concatenate([yre, yim], axis=0).astype(BF16)
            z = u[:, (n + 1) * D_B:(n + 2) * D_B] * _dot(inv_ref[...], y)
        o_ref[rows, :] = z.astype(o_ref.dtype)


def hyena_mixer(u, nseq, seq_len, short_w, short_b, spectrum, fwd, inv):
    t = u.shape[0]
    width = (HY_ORDER + 1) * D_B
    col_block = (3 * D_A) // width
    assert col_block * width == 3 * D_A
    per_step = max(1, HY_STEP_ROWS // seq_len)
    assert nseq % per_step == 0
    return pl.pallas_call(
        functools.partial(_hyena_kernel, seq_len, per_step),
        grid=(nseq // per_step,),
        in_specs=[pl.BlockSpec((per_step * seq_len, width), lambda b: (b, col_block)),
                  pl.BlockSpec(short_w.shape, lambda b: (0, 0)),
                  pl.BlockSpec((1, width), lambda b: (0, 0)),
                  pl.BlockSpec(spectrum.shape, lambda b: (0, 0, 0)),
                  pl.BlockSpec(fwd.shape, lambda b: (0, 0)),
                  pl.BlockSpec(inv.shape, lambda b: (0, 0))],
        out_specs=pl.BlockSpec((per_step * seq_len, D_B), lambda b: (b, 0)),
        out_shape=jax.ShapeDtypeStruct((t, D_B), BF16),
        compiler_params=_cparams("arbitrary"),
        name="hyena_mixer",
    )(u, short_w, short_b.reshape(1, width), spectrum, fwd, inv)


RG_CB = LANES
RG_CHUNK = 512
TM_ROWS = 512


def _rglru_kernel(nseq, seq_len, gate_ref, xr_ref, cw_ref, cb_ref, wg_ref, bg_ref, lam_ref, h0_ref,
                  y_ref, fin_ref, xp_ref, a_f, b_f, a_b, b_b):
    t_tot = nseq * seq_len
    c = RG_CB
    pad = 2 * nseq
    xp_ref[0:pad, :] = jnp.zeros((pad, c), F32)
    xp_ref[pad + t_tot:, :] = jnp.zeros((pad, c), F32)
    xp_ref[pad:pad + t_tot, :] = xr_ref[...]
    nl = -lam_ref[...]
    sp = jnp.maximum(nl, 0.0) + jnp.log1p(jnp.exp(-jnp.abs(nl)))
    k2 = (-0.5 * RG_C * math.log2(math.e)) * sp

    def gate_chunk(ci, carry):
        r0 = pl.multiple_of(ci * RG_CHUNK, RG_CHUNK)
        xc = xp_ref[pl.ds(r0, RG_CHUNK), :] * cw_ref[0:1, :]
        for j in range(1, cw_ref.shape[0]):
            xc = xc + xp_ref[pl.ds(r0 + j * nseq, RG_CHUNK), :] * cw_ref[j:j + 1, :]
        xc = xc + cb_ref[...]
        gts = _dot(xc.astype(BF16), wg_ref[0]) + bg_ref[...]
        x_half = 0.5 * xc
        for d, (a_ref, b_ref) in enumerate(((a_f, b_f), (a_b, b_b))):
            t_r = jnp.tanh(gts[:, (2 * d) * c:(2 * d + 1) * c])
            t_i = jnp.tanh(gts[:, (2 * d + 1) * c:(2 * d + 2) * c])
            a = jnp.exp2(t_r * k2[d:d + 1, :] + k2[d:d + 1, :])
            a_ref[pl.ds(r0, RG_CHUNK), :] = a
            y = 1.0 - a * a
            root = jnp.where(y > 0.0, y * lax.rsqrt(y), 0.0)
            b_ref[pl.ds(r0, RG_CHUNK), :] = root * ((t_i + 1.0) * x_half)
        return carry

    lax.fori_loop(0, t_tot // RG_CHUNK, gate_chunk, 0)

    def scan_step(t, carry):
        hf, hb = carry
        rows_f = pl.ds(pl.multiple_of(t * nseq, nseq), nseq)
        rows_b = pl.ds(pl.multiple_of((seq_len - 1 - t) * nseq, nseq), nseq)
        hf = a_f[rows_f, :] * hf + b_f[rows_f, :]
        hb = a_b[rows_b, :] * hb + b_b[rows_b, :]
        b_f[rows_f, :] = hf
        b_b[rows_b, :] = hb
        return hf, hb

    hf, hb = lax.fori_loop(0, seq_len, scan_step, (h0_ref[0], h0_ref[1]), unroll=8)
    fin_ref[0] = hf
    fin_ref[1] = hb

    def out_chunk(ci, carry):
        rs = pl.ds(pl.multiple_of(ci * RG_CHUNK, RG_CHUNK), RG_CHUNK)
        y_ref[rs, :] = ((b_f[rs, :] + b_b[rs, :]) * jax.nn.gelu(gate_ref[rs, :])).astype(y_ref.dtype)
        return carry

    lax.fori_loop(0, t_tot // RG_CHUNK, out_chunk, 0)


def _rg_gate_weights(wa, wx):
    per_step = RG_CB // RG_BLOCK
    steps = D_RNN // RG_CB
    mats = []
    for d in range(2):
        for w in (wa[d], wx[d]):
            w = w.reshape(steps, per_step, RG_BLOCK, RG_BLOCK)
            eye = jnp.eye(per_step, dtype=w.dtype)
            m = jnp.einsum('spde,pq->spdqe', w, eye).reshape(steps, RG_CB, RG_CB)
            mats.append(m)
    return (0.5 * jnp.concatenate(mats, axis=-1)).astype(BF16)


def rglru_block(u, nseq, seq_len, conv_w, conv_b, wa, ba, wx, bx, lam, h0):
    t = u.shape[0]
    c = RG_CB
    steps = D_RNN // c
    wg = _rg_gate_weights(wa, wx)
    bg = jnp.stack([ba[0], bx[0], ba[1], bx[1]], axis=0).reshape(4, steps, c)
    bg = 0.5 * bg.transpose(1, 0, 2).reshape(steps, 1, 4 * c)
    y, fin = pl.pallas_call(
        functools.partial(_rglru_kernel, nseq, seq_len),
        grid=(steps,),
        in_specs=[pl.BlockSpec((t, c), lambda j: (0, j)),
                  pl.BlockSpec((t, c), lambda j: (0, steps + j)),
                  pl.BlockSpec((conv_w.shape[0], c), lambda j: (0, j)),
                  pl.BlockSpec((1, c), lambda j: (0, j)),
                  pl.BlockSpec((1, c, 4 * c), lambda j: (j, 0, 0)),
                  pl.BlockSpec((None, 1, 4 * c), lambda j: (j, 0, 0)),
                  pl.BlockSpec((2, c), lambda j: (0, j)),
                  pl.BlockSpec((2, nseq, c), lambda j: (0, 0, j))],
        out_specs=[pl.BlockSpec((t, c), lambda j: (0, j)),
                   pl.BlockSpec((2, nseq, c), lambda j: (0, 0, j))],
        out_shape=[jax.ShapeDtypeStruct((t, D_RNN), BF16),
                   jax.ShapeDtypeStruct((2, nseq, D_RNN), F32)],
        scratch_shapes=[pltpu.VMEM((t + 4 * nseq, c), F32)] + [pltpu.VMEM((t, c), F32)] * 4,
        compiler_params=_cparams("arbitrary"),
        name="rglru_block",
    )(u, u, conv_w, conv_b.reshape(1, -1), wg, bg, lam, h0)
    return y, fin


def _route_record(h, w, rb):
    h_hi = h.astype(BF16)
    h_lo = (h - h_hi.astype(F32)).astype(BF16)
    w_hi = w.astype(BF16)
    w_lo = (w - w_hi.astype(F32)).astype(BF16)
    logits = _dot_nt(w_hi, h_hi) + (_dot_nt(w_lo, h_hi) + _dot_nt(w_hi, h_lo))
    scores = jax.nn.sigmoid(logits)
    sel = scores + rb
    row = [sel[e:e + 1, :] for e in range(N_EXPERTS)]
    gs = []
    for g in range(N_GROUPS):
        r = row[g * EXPERTS_PER_GROUP:(g + 1) * EXPERTS_PER_GROUP]
        best_pair = None
        for i in range(EXPERTS_PER_GROUP):
            for j in range(i + 1, EXPERTS_PER_GROUP):
                s = r[i] + r[j]
                best_pair = s if best_pair is None else jnp.maximum(best_pair, s)
        gs.append(best_pair)
    best = jnp.zeros_like(gs[0], dtype=jnp.int32)
    top = gs[0]
    for g in range(1, N_GROUPS):
        better = gs[g] > top
        best = jnp.where(better, g, best)
        top = jnp.where(better, gs[g], top)
    picked = []
    for e in range(N_EXPERTS):
        g = e // EXPERTS_PER_GROUP
        rank = jnp.zeros_like(best)
        for o in range(g * EXPERTS_PER_GROUP, (g + 1) * EXPERTS_PER_GROUP):
            if o == e:
                continue
            ahead = (row[o] > row[e]) | ((row[o] == row[e]) & (o < e))
            rank = rank + ahead.astype(jnp.int32)
        picked.append((best == g) & (rank < 2))
    den = jnp.zeros_like(gs[0])
    for e in range(N_EXPERTS):
        den = den + jnp.where(picked[e], scores[e:e + 1, :], 0.0)
    gate = [jnp.where(picked[e], scores[e:e + 1, :] / den, 0.0) for e in range(N_EXPERTS)]
    cls = jnp.zeros_like(den)
    w_a = jnp.zeros_like(den)
    w_b = jnp.zeros_like(den)
    for g in range(N_GROUPS):
        for pi, (a, b) in enumerate(MOE_PAIRS):
            ea, eb = g * EXPERTS_PER_GROUP + a, g * EXPERTS_PER_GROUP + b
            both = picked[ea] & picked[eb]
            cls = jnp.where(both, float(g * len(MOE_PAIRS) + pi), cls)
            w_a = jnp.where(both, gate[ea], w_a)
            w_b = jnp.where(both, gate[eb], w_b)
    return jnp.concatenate([cls, w_a, w_b, jnp.zeros((ROUTE_ROWS - 3, cls.shape[1]), F32)], axis=0)


MOE_PAIRS = ((0, 1), (0, 2), (0, 3), (1, 3), (1, 2), (2, 3))
N_CLS = N_GROUPS * len(MOE_PAIRS)
CLS_PAD = 32
ROUTE_ROWS = 8
MOE_TS = 256
MOE_TM = 256
MOE_STEP_TILES = 4
SLOT_BLK = 512
ROW_W = D_MODEL + LANES


def _slots_kernel(n_blk, route_ref, slot_ref, off_ref, cnt_ref):
    cid = lax.broadcasted_iota(jnp.int32, (CLS_PAD, SLOT_BLK), 0).astype(F32)

    def members(j):
        cls = route_ref[0:1, pl.ds(pl.multiple_of(j * SLOT_BLK, SLOT_BLK), SLOT_BLK)]
        return (cid == cls).astype(F32)

    def count(j, cnt):
        return cnt + jnp.sum(members(j), axis=1, keepdims=True)

    cnt = lax.fori_loop(0, n_blk, count, jnp.zeros((CLS_PAD, 1), F32))
    cnt = jnp.broadcast_to(cnt, (CLS_PAD, LANES))
    padded = jnp.ceil(cnt * (1.0 / MOE_TS)) * MOE_TS
    r = lax.broadcasted_iota(jnp.int32, (CLS_PAD, CLS_PAD), 0)
    c = lax.broadcasted_iota(jnp.int32, (CLS_PAD, CLS_PAD), 1)
    off = jnp.dot((c < r).astype(F32), padded, precision=lax.Precision.HIGHEST, preferred_element_type=F32)
    off_ref[...] = off
    cnt_ref[...] = cnt
    tr = lax.broadcasted_iota(jnp.int32, (SLOT_BLK, SLOT_BLK), 0)
    tc = lax.broadcasted_iota(jnp.int32, (SLOT_BLK, SLOT_BLK), 1)
    earlier = (tr < tc).astype(BF16)

    def assign(j, base):
        member = members(j)
        rank = _dot(member.astype(BF16), earlier)
        slot = jnp.sum(member * (rank + base), axis=0, keepdims=True)
        slot_ref[0:1, pl.ds(pl.multiple_of(j * SLOT_BLK, SLOT_BLK), SLOT_BLK)] = slot.astype(jnp.int32)
        return base + jnp.sum(member, axis=1, keepdims=True)

    lax.fori_loop(0, n_blk, assign, off[:, 0:1])


def moe_slots(route):
    t = route.shape[1]
    stat = jax.ShapeDtypeStruct((CLS_PAD, LANES), F32)
    return pl.pallas_call(
        functools.partial(_slots_kernel, t // SLOT_BLK),
        out_shape=[jax.ShapeDtypeStruct((1, t), jnp.int32), stat, stat],
        compiler_params=pltpu.CompilerParams(vmem_limit_bytes=VMEM_LIMIT),
        name="moe_slots",
    )(route)


def _tile_maps(off, cnt, n_tiles):
    off = off[:N_CLS, 0].astype(jnp.int32)
    cnt = cnt[:N_CLS, 0].astype(jnp.int32)
    ends = off + ((cnt + MOE_TS - 1) // MOE_TS) * MOE_TS
    n_used = ends[-1] // MOE_TS
    k = jnp.arange(n_tiles, dtype=jnp.int32)
    tix = jnp.minimum(k, n_used - 1)
    cls = jnp.sum((tix[:, None] * MOE_TS >= ends[None, :]).astype(jnp.int32), axis=1)
    pair = jnp.asarray(MOE_PAIRS, jnp.int32)
    grp = (cls // len(MOE_PAIRS)) * EXPERTS_PER_GROUP
    ea = grp + pair[cls % len(MOE_PAIRS), 0]
    eb = grp + pair[cls % len(MOE_PAIRS), 1]
    n = jnp.int32(n_tiles)

    def slot_plan(e):
        chg = jnp.concatenate([jnp.ones((1,), jnp.int32), (e[1:] != e[:-1]).astype(jnp.int32)])
        at = jnp.where(chg == 1, k, n)
        nxt_at = jnp.concatenate([lax.cummin(at[::-1])[::-1][1:], n.reshape(1)])
        more = (nxt_at < n).astype(jnp.int32)
        nxt = e[jnp.minimum(nxt_at, n - 1)]
        par = (jnp.cumsum(chg) - 1) % 2
        return chg, nxt, more, par.astype(jnp.int32)

    plan_a, plan_b = slot_plan(ea), slot_plan(eb)
    chg, nxt, more, par = (jnp.stack([pa, pb]) for pa, pb in zip(plan_a, plan_b))
    return ea, eb, chg, nxt, more, par, n_used.reshape(1)


def _dispatch_kernel(n_steps, slots_ref, x_ref, g_ref, sc_ref, sh_ref, rt_ref, hs_in, hs_out, rowbuf, sem):
    del hs_in
    i = pl.program_id(0)
    cur = i % 2

    def wait_rows(s):
        pltpu.make_async_copy(rowbuf.at[s], rowbuf.at[s], sem.at[s]).wait()

    @pl.when(i >= 2)
    def _():
        wait_rows(cur)

    rowbuf[cur, :, 0:D_MODEL] = _normmod(x_ref[...], g_ref[...], sc_ref[0], sh_ref[0])
    rowbuf[cur, :, D_MODEL:ROW_W] = jnp.concatenate(
        [rt_ref[...], jnp.zeros((MOE_TM, LANES - ROUTE_ROWS), F32)], axis=1)

    for s in range(2):
        @pl.when(cur == s)
        def _():
            for r in range(MOE_TM):
                dst = slots_ref[i * MOE_TM + r]
                pltpu.make_async_copy(rowbuf.at[s, r], hs_out.at[dst], sem.at[s]).start()

    @pl.when(i == n_steps - 1)
    def _():
        wait_rows(cur)
        if n_steps >= 2:
            wait_rows(1 - cur)


def moe_dispatch(x, g, mod, route_t, slots, hs, seq_len, per_seq):
    t, d = x.shape
    n_steps = t // MOE_TM
    grid_spec = pltpu.PrefetchScalarGridSpec(
        num_scalar_prefetch=1,
        grid=(n_steps,),
        in_specs=[pl.BlockSpec((MOE_TM, d), lambda i, s: (i, 0)),
                  pl.BlockSpec((1, d), lambda i, s: (0, 0)),
                  _mod_spec(4, MOE_TM, seq_len, per_seq),
                  _mod_spec(3, MOE_TM, seq_len, per_seq),
                  pl.BlockSpec((MOE_TM, ROUTE_ROWS), lambda i, s: (i, 0)),
                  pl.BlockSpec(memory_space=pl.ANY)],
        out_specs=pl.BlockSpec(memory_space=pl.ANY),
        scratch_shapes=[pltpu.VMEM((2, MOE_TM, ROW_W), F32), pltpu.SemaphoreType.DMA((2,))],
    )
    return pl.pallas_call(
        functools.partial(_dispatch_kernel, n_steps),
        grid_spec=grid_spec,
        out_shape=jax.ShapeDtypeStruct(hs.shape, F32),
        input_output_aliases={6: 0},
        compiler_params=_cparams("arbitrary"),
        name="moe_dispatch",
    )(slots, x, g.reshape(1, d), mod, mod, route_t, hs)


def _experts_kernel(layer, ea_ref, eb_ref, chg_ref, nxt_ref, more_ref, par_ref, nused_ref,
                    hs_ref, wg_hbm, wu_hbm, wd_hbm, ys_ref, fg, fu, fd, bg, bu, bd, sem):
    def weight_copies(slot, expert, par):
        return [pltpu.make_async_copy(src.at[layer, expert], dst.at[slot, par], sem.at[slot, par])
                for src, dst in ((wg_hbm, fg), (wu_hbm, fu), (wd_hbm, fd))]

    def tile(k, rows):
        @pl.when(k < nused_ref[0])
        def _():
            for slot, e_ref in enumerate((ea_ref, eb_ref)):
                @pl.when(chg_ref[slot, k] == 1)
                def _():
                    par = par_ref[slot, k]

                    @pl.when(k == 0)
                    def _():
                        for cp in weight_copies(slot, e_ref[0], par):
                            cp.start()

                    for cp in weight_copies(slot, e_ref[k], par):
                        cp.wait()
                    bg[slot] = fg[slot, par].astype(BF16)
                    bu[slot] = fu[slot, par].astype(BF16)
                    bd[slot] = fd[slot, par].astype(BF16)

                    @pl.when(more_ref[slot, k] == 1)
                    def _():
                        for cp in weight_copies(slot, nxt_ref[slot, k], 1 - par):
                            cp.start()

            h = hs_ref[rows, 0:D_MODEL].astype(BF16)

            def ffn(slot):
                hid = _dot(h, bg[slot])
                up = _dot(h, bu[slot])
                w = hs_ref[rows, D_MODEL + 1 + slot:D_MODEL + 2 + slot]
                act = (hid * _sigmoid(hid)) * up * w
                return _dot(act.astype(BF16), bd[slot])

            ys_ref[rows, :] = ffn(0) + ffn(1)

        @pl.when(k >= nused_ref[0])
        def _():
            ys_ref[rows, :] = jnp.zeros((MOE_TS, ys_ref.shape[1]), ys_ref.dtype)

    for j in range(MOE_STEP_TILES):
        tile(pl.program_id(0) * MOE_STEP_TILES + j, slice(j * MOE_TS, (j + 1) * MOE_TS))


def moe_experts(hs, maps, layer, w_gate, w_up, w_down):
    n_tiles = hs.shape[0] // MOE_TS
    d = D_MODEL

    grid_spec = pltpu.PrefetchScalarGridSpec(
        num_scalar_prefetch=7,
        grid=(n_tiles // MOE_STEP_TILES,),
        in_specs=[pl.BlockSpec((MOE_STEP_TILES * MOE_TS, ROW_W), lambda k, *_: (k, 0)),
                  pl.BlockSpec(memory_space=pl.ANY), pl.BlockSpec(memory_space=pl.ANY),
                  pl.BlockSpec(memory_space=pl.ANY)],
        out_specs=pl.BlockSpec((MOE_STEP_TILES * MOE_TS, d), lambda k, *_: (k, 0)),
        scratch_shapes=[pltpu.VMEM((2, 2, d, D_EXPERT), F32), pltpu.VMEM((2, 2, d, D_EXPERT), F32),
                        pltpu.VMEM((2, 2, D_EXPERT, d), F32),
                        pltpu.VMEM((2, d, D_EXPERT), BF16), pltpu.VMEM((2, d, D_EXPERT), BF16),
                        pltpu.VMEM((2, D_EXPERT, d), BF16),
                        pltpu.SemaphoreType.DMA((2, 2))],
    )
    return pl.pallas_call(
        functools.partial(_experts_kernel, layer),
        grid_spec=grid_spec,
        out_shape=jax.ShapeDtypeStruct((hs.shape[0], d), F32),
        compiler_params=_cparams("arbitrary"),
        name="moe_experts",
    )(*maps, hs, w_gate, w_up, w_down)


def _combine_kernel(final, n_steps, slots_ref, x_ref, g2_ref, fg_ref, ys_hbm, o_ref, gbuf, sem):
    i = pl.program_id(0)
    cur = i % 2

    def issue_tile(tile, s):
        for r in range(MOE_TM):
            src = slots_ref[tile * MOE_TM + r]
            pltpu.make_async_copy(ys_hbm.at[src], gbuf.at[s, r], sem.at[s]).start()

    @pl.when(i == 0)
    def _():
        issue_tile(0, 0)

    for s in range(2):
        @pl.when((i + 1 < n_steps) & (1 - cur == s))
        def _():
            issue_tile(i + 1, s)

    pltpu.make_async_copy(gbuf.at[cur], gbuf.at[cur], sem.at[cur]).wait()
    y = x_ref[...] + g2_ref[0] * gbuf[cur]
    if final:
        ms = jnp.mean(y * y, axis=-1, keepdims=True)
        y = y * lax.rsqrt(ms + EPS) * fg_ref[...]
    o_ref[...] = y


def moe_combine(x, mod, slots, ys, final_g, final, seq_len, per_seq):
    t, d = x.shape
    n_steps = t // MOE_TM
    grid_spec = pltpu.PrefetchScalarGridSpec(
        num_scalar_prefetch=1,
        grid=(n_steps,),
        in_specs=[pl.BlockSpec((MOE_TM, d), lambda i, s: (i, 0)),
                  _mod_spec(5, MOE_TM, seq_len, per_seq),
                  pl.BlockSpec((1, d), lambda i, s: (0, 0)),
                  pl.BlockSpec(memory_space=pl.ANY)],
        out_specs=pl.BlockSpec((MOE_TM, d), lambda i, s: (i, 0)),
        scratch_shapes=[pltpu.VMEM((2, MOE_TM, d), F32), pltpu.SemaphoreType.DMA((2,))],
    )
    return pl.pallas_call(
        functools.partial(_combine_kernel, final, n_steps),
        grid_spec=grid_spec,
        out_shape=jax.ShapeDtypeStruct((t, d), F32),
        compiler_params=_cparams("arbitrary"),
        name="moe_combine",
    )(slots, x, mod, final_g.reshape(1, d), ys)


def _combine_nm_tm_kernel(nseq, steps, seq_len, n_steps, slots_ref, x_ref, g2_ref, ys_hbm, g_ref, sc_ref,
                          sh_ref, w_ref, x2_ref, u_ref, gbuf, sem):
    i = pl.program_id(0)
    cur = i % 2

    def issue_tile(tile, s):
        for sq in range(nseq):
            for t in range(steps):
                src = slots_ref[sq * seq_len + tile * steps + t]
                pltpu.make_async_copy(ys_hbm.at[src], gbuf.at[s, sq * steps + t], sem.at[s]).start()

    @pl.when(i == 0)
    def _():
        issue_tile(0, 0)

    for s in range(2):
        @pl.when((i + 1 < n_steps) & (1 - cur == s))
        def _():
            issue_tile(i + 1, s)

    pltpu.make_async_copy(gbuf.at[cur], gbuf.at[cur], sem.at[cur]).wait()
    d = x_ref.shape[-1]
    y = x_ref[...] + g2_ref[...] * gbuf[cur].reshape(nseq, steps, d)
    x2_ref[...] = y
    h = _normmod(y, g_ref[...], sc_ref[...], sh_ref[...])
    h = h.reshape(nseq * steps, d).astype(BF16)
    h = _dot(_row_permutation(nseq, steps, True), h).astype(BF16)
    u_ref[...] = _dot(h, w_ref[...])


def combine_nm_matmul_tm(x, mod_prev, slots, ys, g, mod, w, nseq, seq_len, per_seq):
    t, d = x.shape
    n = w.shape[1]
    steps = TM_ROWS // nseq
    n_steps = seq_len // steps
    g2 = _group_mod(mod_prev, 5, nseq, per_seq)
    sc = _group_mod(mod, 1, nseq, per_seq)
    sh = _group_mod(mod, 0, nseq, per_seq)
    mod_spec = pl.BlockSpec(sc.shape, lambda i, s: (0, 0, 0))
    x_spec = pl.BlockSpec((nseq, steps, d), lambda i, s: (0, i, 0))
    grid_spec = pltpu.PrefetchScalarGridSpec(
        num_scalar_prefetch=1,
        grid=(n_steps,),
        in_specs=[x_spec, mod_spec, pl.BlockSpec(memory_space=pl.ANY),
                  pl.BlockSpec((1, d), lambda i, s: (0, 0)), mod_spec, mod_spec,
                  pl.BlockSpec((d, n), lambda i, s: (0, 0))],
        out_specs=[x_spec, pl.BlockSpec((TM_ROWS, n), lambda i, s: (i, 0))],
        scratch_shapes=[pltpu.VMEM((2, TM_ROWS, d), F32), pltpu.SemaphoreType.DMA((2,))],
    )
    x2, u = pl.pallas_call(
        functools.partial(_combine_nm_tm_kernel, nseq, steps, seq_len, n_steps),
        grid_spec=grid_spec,
        out_shape=[jax.ShapeDtypeStruct((nseq, seq_len, d), F32), jax.ShapeDtypeStruct((t, n), F32)],
        compiler_params=_cparams("arbitrary"),
        name="moe_combine_norm_mod_proj_tm",
    )(slots, x.reshape(nseq, seq_len, d), g2, ys, g.reshape(1, d), sc, sh, w)
    return x2.reshape(t, d), u


def moe_block(xs, routes, mods_l, per_seqs, seq_lens, layer, p, final, defer_combine, hs):
    g = p['norm_g'][layer, 1]
    slots, off, cnt = moe_slots(jnp.concatenate(routes, axis=1))
    t_all = slots.shape[1]
    n_tiles = -(-(t_all // MOE_TS + N_CLS) // MOE_STEP_TILES) * MOE_STEP_TILES
    maps = _tile_maps(off, cnt, n_tiles)
    if hs is None:
        hs = jnp.zeros((n_tiles * MOE_TS, ROW_W), F32)
    bounds = np.cumsum([0] + [x.shape[0] for x in xs])
    group_slots = [slots[0, bounds[i]:bounds[i + 1]] for i in range(len(xs))]
    for x, r, s, sl, ps in zip(xs, routes, group_slots, seq_lens, per_seqs):
        hs = moe_dispatch(x, g, mods_l, r.T, s, hs, sl, ps)
    ys = moe_experts(hs, maps, layer, p['moe_w_gate'], p['moe_w_up'], p['moe_w_down'])
    if defer_combine:
        return [(x, s, ys) for x, s in zip(xs, group_slots)], hs
    return [moe_combine(x, mods_l, s, ys, p['final_g'], final, sl, ps)
            for x, s, sl, ps in zip(xs, group_slots, seq_lens, per_seqs)], hs


def _mixer(x, group, l, mod, p, hy_tables, mod_prev):
    per_seq, nseq, seq_len = group['per_seq'], group['nseq'], group['seq_len']
    extras = None
    deferred = isinstance(x, tuple)
    if deferred and l % 2 == 0:
        x = moe_combine(x[0], mod_prev, x[1], x[2], p['final_g'], False, seq_len, per_seq)
        deferred = False
    if l % 2 == 0:
        e = l // 2
        u = nm_matmul(x, p['norm_g'][l, 0], mod, p['a_in_w'][e].astype(BF16), seq_len, per_seq)
        if group['ctx_k'] is None:
            attn, nk, nv = context_attention(u, nseq, seq_len)
            extras = (nk, nv)
        else:
            attn = neighbourhood_attention(u, group['ctx_k'][:, e], group['ctx_v'][:, e], p['na_rpb'][e],
                                           nseq, seq_len)
        fwd, inv = hy_tables[seq_len]
        spectrum = hyena_spectrum(seq_len, p['hy_w1'][e], p['hy_b1'][e], p['hy_w2'][e], p['hy_b2'][e],
                                  p['hy_w3'][e], p['hy_freq'][e], p['hy_d'][e], fwd)
        hy = hyena_mixer(u, nseq, seq_len, p['hy_short_w'][e], p['hy_short_b'][e], spectrum, fwd, inv)
        w_out = p['a_out_w'][e].astype(BF16)
        x, route = proj_residual([attn, hy], [w_out[:D_A], w_out[D_A:]], x, mod, p['norm_g'][l, 1],
                                 p['router_w'], p['router_b'], seq_len, per_seq)
    else:
        o = l // 2
        w_in = p['c_in_w'][o].astype(BF16)
        if deferred:
            x, u = combine_nm_matmul_tm(x[0], mod_prev, x[1], x[2], p['norm_g'][l, 0], mod, w_in,
                                        nseq, seq_len, per_seq)
        else:
            u = nm_matmul_tm(x, p['norm_g'][l, 0], mod, w_in, nseq, seq_len, per_seq)
        y, extras = rglru_block(u, nseq, seq_len, p['rg_conv_w'][o], p['rg_conv_b'][o], p['rg_wa'][o],
                                p['rg_ba'][o], p['rg_wx'][o], p['rg_bx'][o], p['rg_lam'][o], group['h0'][o])
        x, route = proj_residual_tm(y, p['c_out_w'][o].astype(BF16), x, mod, p['norm_g'][l, 1],
                                    p['router_w'], p['router_b'], nseq, seq_len, per_seq)
    return x, extras, route


def kernel(x_prompt, x_sample, cache_k, cache_v, state_h, c, c_ctx, norm_g, ada_w, ada_b, final_g, a_in_w, a_out_w, na_rpb, hy_short_w, hy_short_b, hy_w1, hy_b1, hy_w2, hy_b2, hy_w3, hy_freq, hy_d, c_in_w, c_out_w, rg_conv_w, rg_conv_b, rg_wa, rg_ba, rg_wx, rg_bx, rg_lam, router_w, router_b, moe_w_gate, moe_w_up, moe_w_down):
    p = dict(norm_g=norm_g, final_g=final_g, a_in_w=a_in_w, a_out_w=a_out_w, na_rpb=na_rpb,
             hy_short_w=hy_short_w, hy_short_b=hy_short_b, hy_w1=hy_w1, hy_b1=hy_b1, hy_w2=hy_w2,
             hy_b2=hy_b2, hy_w3=hy_w3, hy_freq=hy_freq, hy_d=hy_d, c_in_w=c_in_w, c_out_w=c_out_w,
             rg_conv_w=rg_conv_w, rg_conv_b=rg_conv_b, rg_wa=rg_wa, rg_ba=rg_ba, rg_wx=rg_wx, rg_bx=rg_bx,
             rg_lam=rg_lam, router_w=router_w, router_b=router_b, moe_w_gate=moe_w_gate,
             moe_w_up=moe_w_up, moe_w_down=moe_w_down)
    batch, seq, d = x_prompt.shape
    dec_batch, dec_seq, _ = x_sample.shape
    n_odd = DEPTH // 2
    assert 1 + dec_batch <= MOD_ROWS

    cond = jnp.concatenate([c_ctx[None, :], c, jnp.zeros((MOD_ROWS - 1 - dec_batch, d), F32)], axis=0)
    m = modulation(cond, ada_w, ada_b)
    mods = [m[l].reshape(MOD_ROWS * N_MOD, 1, d) for l in range(DEPTH)]

    tables = {}
    for sl in (seq, dec_seq):
        fwd, inv = _dft_tables(sl)
        tables[sl] = (jnp.asarray(fwd).astype(BF16), jnp.asarray(inv).astype(BF16))

    groups = [
        dict(per_seq=False, nseq=batch, seq_len=seq, ctx_k=None, ctx_v=None,
             h0=[jnp.zeros((2, batch, D_RNN), F32)] * n_odd),
        dict(per_seq=True, nseq=dec_batch, seq_len=dec_seq, ctx_k=cache_k, ctx_v=cache_v,
             h0=[state_h[:, o].transpose(1, 0, 2) for o in range(n_odd)]),
    ]
    xs = [x_prompt.reshape(batch * seq, d), x_sample.reshape(dec_batch * dec_seq, d)]
    k_list, v_list, h_list = [], [], []
    hs = None
    for l in range(DEPTH):
        mixed = [_mixer(x, grp, l, mods[l], p, tables, mods[l - 1] if l else None)
                 for x, grp in zip(xs, groups)]
        if l % 2 == 0:
            k_list.append(mixed[0][1][0])
            v_list.append(mixed[0][1][1])
        else:
            h_list.append(mixed[0][1].transpose(1, 0, 2))
        xs, hs = moe_block([mx[0] for mx in mixed], [mx[2] for mx in mixed], mods[l],
                           [grp['per_seq'] for grp in groups], [grp['seq_len'] for grp in groups],
                           l, p, l == DEPTH - 1, l < DEPTH - 1, hs)
    new_k = jnp.stack(k_list, axis=1)
    new_v = jnp.stack(v_list, axis=1)
    new_h = jnp.stack(h_list, axis=1)
    return (xs[0].reshape(batch, seq, d), xs[1].reshape(dec_batch, dec_seq, d), new_k, new_v, new_h)
```

```python
import functools
import math

import numpy as np
import jax
import jax.numpy as jnp
from jax import lax
from jax.experimental import pallas as pl
from jax.experimental.pallas import tpu as pltpu

F32 = jnp.float32
BF16 = jnp.bfloat16

D_MODEL = 1024
DEPTH = 2
GRID_W = 64
EPS = 1e-6
NEG_INF = -1e30
NA_HEADS = 8
HEAD_DIM = 64
D_A = NA_HEADS * HEAD_DIM
WIN_ROWS = 8
WIN_COLS = 16
D_B = D_MODEL - D_A
HY_ORDER = 2
HY_EMB = 33
HY_BANDS = (HY_EMB - 1) // 2
HY_FFN = 64
HY_DECAY_TARGET = 1e-2
HY_FAST_PCT = 0.3
HY_SLOW_PCT = 1.5
D_RNN = D_MODEL
RG_BLOCK = 64
RG_C = 8.0
N_EXPERTS = 16
N_GROUPS = 4
EXPERTS_PER_GROUP = N_EXPERTS // N_GROUPS
D_EXPERT = 512

LANES = 128
VMEM_LIMIT = 56 * 1024 * 1024
N_MOD = 6
MOD_ROWS = 16


def _cparams(*sem):
    return pltpu.CompilerParams(dimension_semantics=sem, vmem_limit_bytes=VMEM_LIMIT)


def _dot(a, b):
    return jnp.dot(a, b, preferred_element_type=F32)


def _dot_nt(a, b):
    return lax.dot_general(a, b, (((1,), (1,)), ((), ())), preferred_element_type=F32)


def _sigmoid(x):
    return 0.5 * jnp.tanh(0.5 * x) + 0.5


def _normmod(x, g, sc, sh):
    ms = jnp.mean(x * x, axis=-1, keepdims=True)
    return (x * lax.rsqrt(ms + EPS) * g) * (1.0 + sc) + sh


def _mod_spec(chunk, tm, seq_len, per_seq):
    if per_seq:
        return pl.BlockSpec((1, 1, D_MODEL), lambda i, *_: ((1 + (i * tm) // seq_len) * N_MOD + chunk, 0, 0))
    return pl.BlockSpec((1, 1, D_MODEL), lambda i, *_: (chunk, 0, 0))


def _mod_kernel(c_ref, w_ref, b_ref, o_ref):
    s = c_ref[...]
    s = s * jax.nn.sigmoid(s)
    o_ref[0] = _dot(s.astype(BF16), w_ref[0].astype(BF16)) + b_ref[0]


def modulation(cond, ada_w, ada_b):
    n = ada_w.shape[-1]
    tn = n // 4
    assert tn % LANES == 0
    return pl.pallas_call(
        _mod_kernel,
        grid=(DEPTH, n // tn),
        in_specs=[pl.BlockSpec((MOD_ROWS, D_MODEL), lambda l, j: (0, 0)),
                  pl.BlockSpec((1, D_MODEL, tn), lambda l, j: (l, 0, j)),
                  pl.BlockSpec((1, 1, tn), lambda l, j: (l, 0, j))],
        out_specs=pl.BlockSpec((1, MOD_ROWS, tn), lambda l, j: (l, 0, j)),
        out_shape=jax.ShapeDtypeStruct((DEPTH, MOD_ROWS, n), F32),
        compiler_params=_cparams("arbitrary", "arbitrary"),
        name="modulation",
    )(cond, ada_w, ada_b.reshape(DEPTH, 1, n))


def _nm_matmul_kernel(x_ref, g_ref, sc_ref, sh_ref, w_ref, o_ref):
    h = _normmod(x_ref[...], g_ref[...], sc_ref[0], sh_ref[0])
    o_ref[...] = _dot(h.astype(BF16), w_ref[...])


def nm_matmul(x, g, mod, w, seq_len, per_seq, tm=512):
    t, d = x.shape
    n = w.shape[1]
    return pl.pallas_call(
        _nm_matmul_kernel,
        grid=(t // tm,),
        in_specs=[pl.BlockSpec((tm, d), lambda i: (i, 0)),
                  pl.BlockSpec((1, d), lambda i: (0, 0)),
                  _mod_spec(1, tm, seq_len, per_seq),
                  _mod_spec(0, tm, seq_len, per_seq),
                  pl.BlockSpec((d, n), lambda i: (0, 0))],
        out_specs=pl.BlockSpec((tm, n), lambda i: (i, 0)),
        out_shape=jax.ShapeDtypeStruct((t, n), F32),
        compiler_params=_cparams("arbitrary"),
        name="norm_mod_proj",
    )(x, g.reshape(1, d), mod, mod, w)


def _proj_res_kernel(n_act, *refs):
    acts = refs[:n_act]
    ws = refs[n_act:2 * n_act]
    x_ref, g_ref, ng_ref, sc2_ref, sh2_ref, rw_ref, rb_ref, o_ref, route_ref = refs[2 * n_act:]
    acc = _dot(acts[0][...].astype(BF16), ws[0][...])
    for a, w in zip(acts[1:], ws[1:]):
        acc += _dot(a[...].astype(BF16), w[...])
    x = x_ref[...] + g_ref[0] * acc
    o_ref[...] = x
    route_ref[...] = _route_record(_normmod(x, ng_ref[...], sc2_ref[0], sh2_ref[0]), rw_ref[...], rb_ref[...])


def proj_residual(acts, ws, x, mod, norm2_g, router_w, router_b, seq_len, per_seq, tm=512):
    t, d = x.shape
    in_specs = [pl.BlockSpec((tm, a.shape[1]), lambda i: (i, 0)) for a in acts]
    in_specs += [pl.BlockSpec(w.shape, lambda i: (0, 0)) for w in ws]
    in_specs += [pl.BlockSpec((tm, d), lambda i: (i, 0)), _mod_spec(2, tm, seq_len, per_seq),
                 pl.BlockSpec((1, d), lambda i: (0, 0)),
                 _mod_spec(4, tm, seq_len, per_seq), _mod_spec(3, tm, seq_len, per_seq),
                 pl.BlockSpec((N_EXPERTS, d), lambda i: (0, 0)),
                 pl.BlockSpec((N_EXPERTS, 1), lambda i: (0, 0))]
    return pl.pallas_call(
        functools.partial(_proj_res_kernel, len(acts)),
        grid=(t // tm,),
        in_specs=in_specs,
        out_specs=[pl.BlockSpec((tm, d), lambda i: (i, 0)), pl.BlockSpec((ROUTE_ROWS, tm), lambda i: (0, i))],
        out_shape=[jax.ShapeDtypeStruct((t, d), F32), jax.ShapeDtypeStruct((ROUTE_ROWS, t), F32)],
        compiler_params=_cparams("arbitrary"),
        name="proj_residual_route",
    )(*acts, *ws, x, mod, norm2_g.reshape(1, d), mod, mod, router_w.T, router_b.reshape(N_EXPERTS, 1))


def _row_permutation(nseq, steps, to_time_major):
    n = nseq * steps
    i = lax.broadcasted_iota(jnp.int32, (n, n), 0)
    j = lax.broadcasted_iota(jnp.int32, (n, n), 1)
    if to_time_major:
        src = (i % nseq) * steps + i // nseq
    else:
        src = (i % steps) * nseq + i // steps
    return (j == src).astype(BF16)


def _nm_matmul_tm_kernel(nseq, steps, x_ref, g_ref, sc_ref, sh_ref, w_ref, o_ref):
    h = _normmod(x_ref[...], g_ref[...], sc_ref[...], sh_ref[...])
    h = h.reshape(nseq * steps, h.shape[-1]).astype(BF16)
    h = _dot(_row_permutation(nseq, steps, True), h).astype(BF16)
    o_ref[...] = _dot(h, w_ref[...])


def _group_mod(mod, chunk, nseq, per_seq):
    rows = mod.reshape(MOD_ROWS, N_MOD, 1, D_MODEL)
    return rows[1:1 + nseq, chunk] if per_seq else rows[0:1, chunk]


def nm_matmul_tm(x, g, mod, w, nseq, seq_len, per_seq):
    t, d = x.shape
    n = w.shape[1]
    steps = TM_ROWS // nseq
    sc = _group_mod(mod, 1, nseq, per_seq)
    sh = _group_mod(mod, 0, nseq, per_seq)
    mod_spec = pl.BlockSpec(sc.shape, lambda i: (0, 0, 0))
    return pl.pallas_call(
        functools.partial(_nm_matmul_tm_kernel, nseq, steps),
        grid=(seq_len // steps,),
        in_specs=[pl.BlockSpec((nseq, steps, d), lambda i: (0, i, 0)),
                  pl.BlockSpec((1, d), lambda i: (0, 0)),
                  mod_spec, mod_spec,
                  pl.BlockSpec((d, n), lambda i: (0, 0))],
        out_specs=pl.BlockSpec((TM_ROWS, n), lambda i: (i, 0)),
        out_shape=jax.ShapeDtypeStruct((t, n), F32),
        compiler_params=_cparams("arbitrary"),
        name="norm_mod_proj_tm",
    )(x.reshape(nseq, seq_len, d), g.reshape(1, d), sc, sh, w)


def _proj_res_tm_kernel(nseq, steps, y_ref, w_ref, x_ref, g_ref, ng_ref, sc2_ref, sh2_ref, rw_ref, rb_ref,
                        o_ref, route_ref):
    y = _dot(_row_permutation(nseq, steps, False), y_ref[...].astype(BF16)).astype(BF16)
    acc = _dot(y, w_ref[...])
    x = x_ref[...] + g_ref[...] * acc.reshape(nseq, steps, acc.shape[-1])
    o_ref[...] = x
    h = _normmod(x, ng_ref[...], sc2_ref[...], sh2_ref[...]).reshape(nseq * steps, x.shape[-1])
    route_ref[0] = _route_record(h, rw_ref[...], rb_ref[...])


def proj_residual_tm(y, w, x, mod, norm2_g, router_w, router_b, nseq, seq_len, per_seq):
    t, d = x.shape
    steps = TM_ROWS // nseq
    n_steps = seq_len // steps
    g1 = _group_mod(mod, 2, nseq, per_seq)
    sc2 = _group_mod(mod, 4, nseq, per_seq)
    sh2 = _group_mod(mod, 3, nseq, per_seq)
    mod_spec = pl.BlockSpec(g1.shape, lambda i: (0, 0, 0))
    out, route = pl.pallas_call(
        functools.partial(_proj_res_tm_kernel, nseq, steps),
        grid=(n_steps,),
        in_specs=[pl.BlockSpec((TM_ROWS, y.shape[1]), lambda i: (i, 0)),
                  pl.BlockSpec(w.shape, lambda i: (0, 0)),
                  pl.BlockSpec((nseq, steps, d), lambda i: (0, i, 0)),
                  mod_spec,
                  pl.BlockSpec((1, d), lambda i: (0, 0)),
                  mod_spec, mod_spec,
                  pl.BlockSpec((N_EXPERTS, d), lambda i: (0, 0)),
                  pl.BlockSpec((N_EXPERTS, 1), lambda i: (0, 0))],
        out_specs=[pl.BlockSpec((nseq, steps, d), lambda i: (0, i, 0)),
                   pl.BlockSpec((1, ROUTE_ROWS, TM_ROWS), lambda i: (i, 0, 0))],
        out_shape=[jax.ShapeDtypeStruct((nseq, seq_len, d), F32),
                   jax.ShapeDtypeStruct((n_steps, ROUTE_ROWS, TM_ROWS), F32)],
        compiler_params=_cparams("arbitrary"),
        name="proj_residual_tm_route",
    )(y, w, x.reshape(nseq, seq_len, d), g1, norm2_g.reshape(1, d), sc2, sh2, router_w.T,
      router_b.reshape(N_EXPERTS, 1))
    route = route.reshape(n_steps, ROUTE_ROWS, nseq, steps).transpose(1, 2, 0, 3).reshape(ROUTE_ROWS, t)
    return out.reshape(t, d), route


CTX_STEP_ROWS = 1024


def _ctx_attn_kernel(seq_len, n_seq, q_ref, k_ref, v_ref, o_ref, nk_ref, nv_ref):
    scale = HEAD_DIM ** -0.5
    per_tile = LANES // HEAD_DIM
    lane = lax.broadcasted_iota(jnp.int32, (1, LANES), 1)
    for b in range(n_seq):
        rows = slice(b * seq_len, (b + 1) * seq_len)
        for hp in range(NA_HEADS // per_tile):
            sl = slice(hp * LANES, (hp + 1) * LANES)
            q, k, v = q_ref[rows, sl] * scale, k_ref[rows, sl], v_ref[rows, sl]
            kb, vb = k.astype(BF16), v.astype(BF16)
            out = None
            for j in range(per_tile):
                h = hp * per_tile + j
                nk_ref[b, h] = k[:, j * HEAD_DIM:(j + 1) * HEAD_DIM]
                nv_ref[b, h] = v[:, j * HEAD_DIM:(j + 1) * HEAD_DIM]
                mine = lane // HEAD_DIM == j
                s = _dot_nt(jnp.where(mine, q, 0.0).astype(BF16), kb)
                p = jnp.exp(s - jnp.max(s, axis=-1, keepdims=True))
                o = _dot(p.astype(BF16), vb) / jnp.sum(p, axis=-1, keepdims=True)
                out = o if out is None else jnp.where(mine, o, out)
            o_ref[rows, sl] = out.astype(o_ref.dtype)


def context_attention(u, nseq, seq_len):
    t = u.shape[0]
    per_step = max(1, CTX_STEP_ROWS // seq_len)
    assert nseq % per_step == 0
    rows = per_step * seq_len
    kv_shape = jax.ShapeDtypeStruct((nseq, NA_HEADS, seq_len, HEAD_DIM), F32)
    kv_spec = pl.BlockSpec((per_step, NA_HEADS, seq_len, HEAD_DIM), lambda b: (b, 0, 0, 0))
    return pl.pallas_call(
        functools.partial(_ctx_attn_kernel, seq_len, per_step),
        grid=(nseq // per_step,),
        in_specs=[pl.BlockSpec((rows, D_A), lambda b: (b, 0)),
                  pl.BlockSpec((rows, D_A), lambda b: (b, 1)),
                  pl.BlockSpec((rows, D_A), lambda b: (b, 2))],
        out_specs=[pl.BlockSpec((rows, D_A), lambda b: (b, 0)), kv_spec, kv_spec],
        out_shape=[jax.ShapeDtypeStruct((t, D_A), BF16), kv_shape, kv_shape],
        compiler_params=_cparams("arbitrary"),
        name="context_attention",
    )(u, u, u)


N_DR = 2 * WIN_ROWS - 1
N_DC = 2 * WIN_COLS - 1
N_DC_PAD = 32


def _na_col_tables():
    cols = np.arange(GRID_W)
    col_start = np.clip(cols - WIN_COLS // 2, 0, GRID_W - WIN_COLS)
    col_in = (cols[None, :] >= col_start[:, None]) & (cols[None, :] < col_start[:, None] + WIN_COLS)
    dc = np.clip(cols[None, :] - cols[:, None], 1 - WIN_COLS, WIN_COLS - 1) + WIN_COLS - 1
    onehot = (dc.reshape(1, -1) == np.arange(N_DC_PAD)[:, None]).astype(np.float32)
    return onehot, col_in.reshape(1, -1).astype(np.float32)


def _na_bias_kernel(r_ref, e_ref, m_ref, o_ref):
    t = jnp.dot(r_ref[...], e_ref[...], precision=lax.Precision.HIGHEST, preferred_element_type=F32)
    o_ref[...] = jnp.where(m_ref[...] > 0.0, t, NEG_INF)


def na_bias_table(rpb):
    onehot, col_in = _na_col_tables()
    n_rows = NA_HEADS * N_DR
    assert n_rows <= LANES
    r = jnp.zeros((LANES, N_DC_PAD), F32).at[:n_rows, :N_DC].set(rpb.reshape(n_rows, N_DC).astype(F32))
    t = pl.pallas_call(
        _na_bias_kernel,
        out_shape=jax.ShapeDtypeStruct((LANES, GRID_W * GRID_W), F32),
        name="na_bias_table",
    )(r, jnp.asarray(onehot), jnp.asarray(col_in))
    t = t[:n_rows].reshape(NA_HEADS, N_DR, GRID_W, GRID_W)
    return jnp.concatenate([t[:, :-1], t[:, 1:]], axis=-1)


def _na_kernel(rows, q_ref, k_ref, v_ref, ck_ref, cv_ref, bias_ref, o_ref,
               q_s, k_s, v_s, ck_s, cv_s, s_s, p_s, den_s, o_s):
    scale = HEAD_DIM ** -0.5
    n_lat = WIN_ROWS * GRID_W
    per_tile = LANES // HEAD_DIM
    n_pairs = NA_HEADS // per_tile
    lane = lax.broadcasted_iota(jnp.int32, (1, LANES), 1)
    for hp in range(n_pairs):
        sl = slice(hp * LANES, (hp + 1) * LANES)
        q_s[hp] = (q_ref[:, sl] * scale).astype(BF16)
        k_s[hp] = k_ref[:, sl].astype(BF16)
        v_s[hp] = v_ref[:, sl].astype(BF16)
        heads = range(hp * per_tile, (hp + 1) * per_tile)
        ck_s[hp] = jnp.concatenate([ck_ref[0, h] for h in heads], axis=1).astype(BF16)
        cv_s[hp] = jnp.concatenate([cv_ref[0, h] for h in heads], axis=1).astype(BF16)

    def window(r):
        start = min(max(r - WIN_ROWS // 2, 0), rows - WIN_ROWS)
        return start, start - r + WIN_ROWS - 1

    def pair_body(hp, carry):
        for j in range(per_tile):
            h = hp * per_tile + j
            mine = lane // HEAD_DIM == j
            for r in range(rows):
                start, off = window(r)
                rs = slice(r * GRID_W, (r + 1) * GRID_W)
                q = jnp.where(mine, q_s[hp, rs, :], 0.0).astype(BF16)
                bias = jnp.concatenate([bias_ref[h, off + 2 * i] for i in range(WIN_ROWS // 2)], axis=1)
                s_s[rs, 0:n_lat] = _dot_nt(q, k_s[hp, start * GRID_W:start * GRID_W + n_lat, :]) + bias
                s_s[rs, n_lat:] = _dot_nt(q, ck_s[hp])
            for r in range(rows):
                rs = slice(r * GRID_W, (r + 1) * GRID_W)
                s = s_s[rs, :]
                p = jnp.exp(s - jnp.max(s, axis=-1, keepdims=True))
                den_s[rs, :] = jnp.sum(p, axis=-1, keepdims=True)
                p_s[rs, :] = p.astype(BF16)
            for r in range(rows):
                start, _ = window(r)
                rs = slice(r * GRID_W, (r + 1) * GRID_W)
                o = (_dot(p_s[rs, 0:n_lat], v_s[hp, start * GRID_W:start * GRID_W + n_lat, :])
                     + _dot(p_s[rs, n_lat:], cv_s[hp])) / den_s[rs, :]
                o_s[hp, rs, :] = o if j == 0 else jnp.where(mine, o, o_s[hp, rs, :])
        return carry

    lax.fori_loop(0, n_pairs, pair_body, 0)
    for hp in range(n_pairs):
        o_ref[:, hp * LANES:(hp + 1) * LANES] = o_s[hp].astype(o_ref.dtype)


def neighbourhood_attention(u, ctx_k, ctx_v, rpb, nseq, seq_len):
    t = u.shape[0]
    rows = seq_len // GRID_W
    assert rows >= WIN_ROWS and WIN_ROWS % 2 == 0
    past = ctx_k.shape[2]
    bias = na_bias_table(rpb)
    ctx_spec = pl.BlockSpec((1, NA_HEADS, past, HEAD_DIM), lambda b: (b, 0, 0, 0))
    return pl.pallas_call(
        functools.partial(_na_kernel, rows),
        grid=(nseq,),
        in_specs=[pl.BlockSpec((seq_len, D_A), lambda b: (b, 0)),
                  pl.BlockSpec((seq_len, D_A), lambda b: (b, 1)),
                  pl.BlockSpec((seq_len, D_A), lambda b: (b, 2)),
                  ctx_spec, ctx_spec,
                  pl.BlockSpec(bias.shape, lambda b: (0, 0, 0, 0))],
        out_specs=pl.BlockSpec((seq_len, D_A), lambda b: (b, 0)),
        out_shape=jax.ShapeDtypeStruct((t, D_A), BF16),
        scratch_shapes=[pltpu.VMEM((D_A // LANES, seq_len, LANES), BF16)] * 3
        + [pltpu.VMEM((D_A // LANES, past, LANES), BF16)] * 2
        + [pltpu.VMEM((seq_len, WIN_ROWS * GRID_W + past), F32),
           pltpu.VMEM((seq_len, WIN_ROWS * GRID_W + past), BF16),
           pltpu.VMEM((seq_len, 1), F32),
           pltpu.VMEM((D_A // LANES, seq_len, LANES), F32)],
        compiler_params=_cparams("arbitrary"),
        name="neighbourhood_attention",
    )(u, u, u, ctx_k, ctx_v, bias)


HY_STEP_ROWS = 1024


def _dft_tables(seq_len):
    n = 2 * seq_len
    f = np.arange(seq_len, dtype=np.int64)
    ang = (np.outer(f, f) % n).astype(np.float64) * (math.pi / seq_len)
    cos, sin = np.cos(ang), np.sin(ang)
    alt = np.where(f % 2 == 0, 1.0, -1.0)
    s_fwd = -sin
    s_fwd[0, :] = alt
    fwd = np.concatenate([cos, s_fwd], axis=0)
    wf = np.where(f == 0, 1.0, 2.0) / n
    ci = cos.T * wf[None, :]
    si = -sin.T * wf[None, :]
    si[:, 0] = alt / n
    inv = np.concatenate([ci, si], axis=1)
    return fwd.astype(np.float32), inv.astype(np.float32)


def _hyena_feats(seq_len):
    t = np.linspace(0.0, 1.0, seq_len, dtype=np.float32)[:, None]
    w = (2.0 * math.pi * np.arange(seq_len, dtype=np.float32)[:, None] / seq_len).astype(np.float32)
    f = np.linspace(1e-4, HY_BANDS - 1, HY_BANDS, dtype=np.float32)[None, :]
    z = np.concatenate([t, np.cos(f * w), -np.sin(f * w)], axis=-1).astype(np.float32)
    max_decay = math.log(HY_DECAY_TARGET) / HY_FAST_PCT
    min_decay = math.log(HY_DECAY_TARGET) / HY_SLOW_PCT
    deltas = np.abs(np.linspace(min_decay, max_decay, D_B, dtype=np.float32))[None, :]
    return z, t, deltas


def _hy_filter_kernel(seq_len, z_ref, t_ref, dl_ref, w1_ref, b1_ref, w2_ref, b2_ref, w3_ref, fr_ref,
                      d_ref, fwd_ref, g_ref):
    hp = lax.Precision.HIGHEST
    h = jnp.sin(fr_ref[0:1, :] * (jnp.dot(z_ref[...], w1_ref[...], precision=hp) + b1_ref[...]))
    h = jnp.sin(fr_ref[1:2, :] * (jnp.dot(h, w2_ref[...], precision=hp) + b2_ref[...]))
    h = jnp.dot(h, w3_ref[...], precision=hp)
    decay = jnp.exp(-t_ref[...] * dl_ref[...])
    row0 = lax.broadcasted_iota(jnp.int32, (seq_len, D_B), 0) == 0
    sums, diffs = [], []
    for n in range(HY_ORDER):
        hf = h[:, (2 * n) * D_B:(2 * n + 1) * D_B] * decay
        hb = h[:, (2 * n + 1) * D_B:(2 * n + 2) * D_B] * decay
        gp = jnp.where(row0, hf + hb + d_ref[n:n + 1, :], hf)
        gm = jnp.where(row0, 0.0, hb)
        sums.append(gp + gm)
        diffs.append(gp - gm)
    rhs = jnp.concatenate(sums + diffs, axis=1).astype(BF16)
    spec = _dot(fwd_ref[...], rhs)
    for n in range(HY_ORDER):
        a = spec[:, n * D_B:(n + 1) * D_B]
        b = spec[:, (HY_ORDER + n) * D_B:(HY_ORDER + n + 1) * D_B]
        g_ref[n, 0:seq_len, :] = a[0:seq_len]
        g_ref[n, seq_len:, :] = jnp.where(row0, a[seq_len:], b[seq_len:])


def hyena_spectrum(seq_len, w1, b1, w2, b2, w3, freq, d, fwd):
    z, t, deltas = _hyena_feats(seq_len)
    return pl.pallas_call(
        functools.partial(_hy_filter_kernel, seq_len),
        out_shape=jax.ShapeDtypeStruct((HY_ORDER, 2 * seq_len, D_B), F32),
        compiler_params=pltpu.CompilerParams(vmem_limit_bytes=VMEM_LIMIT),
        name="hyena_spectrum",
    )(jnp.asarray(z), jnp.asarray(t), jnp.asarray(deltas), w1, b1.reshape(1, -1), w2, b2.reshape(1, -1),
      w3, freq, d, fwd)


def _hyena_kernel(seq_len, n_seq, u_ref, sw_ref, sb_ref, g_ref, fwd_ref, inv_ref, o_ref):
    t_idx = lax.broadcasted_iota(jnp.int32, (seq_len, u_ref.shape[1]), 0)
    row0 = lax.broadcasted_iota(jnp.int32, (seq_len, D_B), 0) == 0
    for s in range(n_seq):
        rows = slice(s * seq_len, (s + 1) * seq_len)
        u = u_ref[rows, :]
        prev = jnp.where(t_idx == 0, 0.0, pltpu.roll(u, 1, axis=0))
        nxt = jnp.where(t_idx == seq_len - 1, 0.0, pltpu.roll(u, seq_len - 1, axis=0))
        u = prev * sw_ref[0:1, :] + u * sw_ref[1:2, :] + nxt * sw_ref[2:3, :] + sb_ref[...]
        z = u[:, 0:D_B]
        for n in range(HY_ORDER):
            spec = _dot(fwd_ref[...], z.astype(BF16))
            ure, uim = spec[0:seq_len], spec[seq_len:]
            gre, gim = g_ref[n, 0:seq_len, :], g_ref[n, seq_len:, :]
            pim = uim * gim
            yre = ure * gre - jnp.where(row0, 0.0, pim)
            yim = jnp.where(row0, pim, ure * gim + uim * gre)
            y = jnp.concatenate([yre, yim], axis=0).astype(BF16)
            z = u[:, (n + 1) * D_B:(n + 2) * D_B] * _dot(inv_ref[...], y)
        o_ref[rows, :] = z.astype(o_ref.dtype)


def hyena_mixer(u, nseq, seq_len, short_w, short_b, spectrum, fwd, inv):
    t = u.shape[0]
    width = (HY_ORDER + 1) * D_B
    col_block = (3 * D_A) // width
    assert col_block * width == 3 * D_A
    per_step = max(1, HY_STEP_ROWS // seq_len)
    assert nseq % per_step == 0
    return pl.pallas_call(
        functools.partial(_hyena_kernel, seq_len, per_step),
        grid=(nseq // per_step,),
        in_specs=[pl.BlockSpec((per_step * seq_len, width), lambda b: (b, col_block)),
                  pl.BlockSpec(short_w.shape, lambda b: (0, 0)),
                  pl.BlockSpec((1, width), lambda b: (0, 0)),
                  pl.BlockSpec(spectrum.shape, lambda b: (0, 0, 0)),
                  pl.BlockSpec(fwd.shape, lambda b: (0, 0)),
                  pl.BlockSpec(inv.shape, lambda b: (0, 0))],
        out_specs=pl.BlockSpec((per_step * seq_len, D_B), lambda b: (b, 0)),
        out_shape=jax.ShapeDtypeStruct((t, D_B), BF16),
        compiler_params=_cparams("arbitrary"),
        name="hyena_mixer",
    )(u, short_w, short_b.reshape(1, width), spectrum, fwd, inv)


RG_CB = LANES
RG_CHUNK = 512
TM_ROWS = 512


def _rglru_kernel(nseq, seq_len, gate_ref, xr_ref, cw_ref, cb_ref, wg_ref, bg_ref, lam_ref, h0_ref,
                  y_ref, fin_ref, xp_ref, a_f, b_f, a_b, b_b):
    t_tot = nseq * seq_len
    c = RG_CB
    pad = 2 * nseq
    xp_ref[0:pad, :] = jnp.zeros((pad, c), F32)
    xp_ref[pad + t_tot:, :] = jnp.zeros((pad, c), F32)
    xp_ref[pad:pad + t_tot, :] = xr_ref[...]
    nl = -lam_ref[...]
    sp = jnp.maximum(nl, 0.0) + jnp.log1p(jnp.exp(-jnp.abs(nl)))
    k2 = (-0.5 * RG_C * math.log2(math.e)) * sp

    def gate_chunk(ci, carry):
        r0 = pl.multiple_of(ci * RG_CHUNK, RG_CHUNK)
        xc = xp_ref[pl.ds(r0, RG_CHUNK), :] * cw_ref[0:1, :]
        for j in range(1, cw_ref.shape[0]):
            xc = xc + xp_ref[pl.ds(r0 + j * nseq, RG_CHUNK), :] * cw_ref[j:j + 1, :]
        xc = xc + cb_ref[...]
        gts = _dot(xc.astype(BF16), wg_ref[0]) + bg_ref[...]
        x_half = 0.5 * xc
        for d, (a_ref, b_ref) in enumerate(((a_f, b_f), (a_b, b_b))):
            t_r = jnp.tanh(gts[:, (2 * d) * c:(2 * d + 1) * c])
            t_i = jnp.tanh(gts[:, (2 * d + 1) * c:(2 * d + 2) * c])
            a = jnp.exp2(t_r * k2[d:d + 1, :] + k2[d:d + 1, :])
            a_ref[pl.ds(r0, RG_CHUNK), :] = a
            y = 1.0 - a * a
            root = jnp.where(y > 0.0, y * lax.rsqrt(y), 0.0)
            b_ref[pl.ds(r0, RG_CHUNK), :] = root * ((t_i + 1.0) * x_half)
        return carry

    lax.fori_loop(0, t_tot // RG_CHUNK, gate_chunk, 0)

    def scan_step(t, carry):
        hf, hb = carry
        rows_f = pl.ds(pl.multiple_of(t * nseq, nseq), nseq)
        rows_b = pl.ds(pl.multiple_of((seq_len - 1 - t) * nseq, nseq), nseq)
        hf = a_f[rows_f, :] * hf + b_f[rows_f, :]
        hb = a_b[rows_b, :] * hb + b_b[rows_b, :]
        b_f[rows_f, :] = hf
        b_b[rows_b, :] = hb
        return hf, hb

    hf, hb = lax.fori_loop(0, seq_len, scan_step, (h0_ref[0], h0_ref[1]), unroll=8)
    fin_ref[0] = hf
    fin_ref[1] = hb

    def out_chunk(ci, carry):
        rs = pl.ds(pl.multiple_of(ci * RG_CHUNK, RG_CHUNK), RG_CHUNK)
        y_ref[rs, :] = ((b_f[rs, :] + b_b[rs, :]) * jax.nn.gelu(gate_ref[rs, :])).astype(y_ref.dtype)
        return carry

    lax.fori_loop(0, t_tot // RG_CHUNK, out_chunk, 0)


def _rg_gate_weights(wa, wx):
    per_step = RG_CB // RG_BLOCK
    steps = D_RNN // RG_CB
    mats = []
    for d in range(2):
        for w in (wa[d], wx[d]):
            w = w.reshape(steps, per_step, RG_BLOCK, RG_BLOCK)
            eye = jnp.eye(per_step, dtype=w.dtype)
            m = jnp.einsum('spde,pq->spdqe', w, eye).reshape(steps, RG_CB, RG_CB)
            mats.append(m)
    return (0.5 * jnp.concatenate(mats, axis=-1)).astype(BF16)


def rglru_block(u, nseq, seq_len, conv_w, conv_b, wa, ba, wx, bx, lam, h0):
    t = u.shape[0]
    c = RG_CB
    steps = D_RNN // c
    wg = _rg_gate_weights(wa, wx)
    bg = jnp.stack([ba[0], bx[0], ba[1], bx[1]], axis=0).reshape(4, steps, c)
    bg = 0.5 * bg.transpose(1, 0, 2).reshape(steps, 1, 4 * c)
    y, fin = pl.pallas_call(
        functools.partial(_rglru_kernel, nseq, seq_len),
        grid=(steps,),
        in_specs=[pl.BlockSpec((t, c), lambda j: (0, j)),
                  pl.BlockSpec((t, c), lambda j: (0, steps + j)),
                  pl.BlockSpec((conv_w.shape[0], c), lambda j: (0, j)),
                  pl.BlockSpec((1, c), lambda j: (0, j)),
                  pl.BlockSpec((1, c, 4 * c), lambda j: (j, 0, 0)),
                  pl.BlockSpec((None, 1, 4 * c), lambda j: (j, 0, 0)),
                  pl.BlockSpec((2, c), lambda j: (0, j)),
                  pl.BlockSpec((2, nseq, c), lambda j: (0, 0, j))],
        out_specs=[pl.BlockSpec((t, c), lambda j: (0, j)),
                   pl.BlockSpec((2, nseq, c), lambda j: (0, 0, j))],
        out_shape=[jax.ShapeDtypeStruct((t, D_RNN), BF16),
                   jax.ShapeDtypeStruct((2, nseq, D_RNN), F32)],
        scratch_shapes=[pltpu.VMEM((t + 4 * nseq, c), F32)] + [pltpu.VMEM((t, c), F32)] * 4,
        compiler_params=_cparams("arbitrary"),
        name="rglru_block",
    )(u, u, conv_w, conv_b.reshape(1, -1), wg, bg, lam, h0)
    return y, fin


def _route_record(h, w, rb):
    h_hi = h.astype(BF16)
    h_lo = (h - h_hi.astype(F32)).astype(BF16)
    w_hi = w.astype(BF16)
    w_lo = (w - w_hi.astype(F32)).astype(BF16)
    logits = _dot_nt(w_hi, h_hi) + (_dot_nt(w_lo, h_hi) + _dot_nt(w_hi, h_lo))
    scores = jax.nn.sigmoid(logits)
    sel = scores + rb
    row = [sel[e:e + 1, :] for e in range(N_EXPERTS)]
    gs = []
    for g in range(N_GROUPS):
        r = row[g * EXPERTS_PER_GROUP:(g + 1) * EXPERTS_PER_GROUP]
        best_pair = None
        for i in range(EXPERTS_PER_GROUP):
            for j in range(i + 1, EXPERTS_PER_GROUP):
                s = r[i] + r[j]
                best_pair = s if best_pair is None else jnp.maximum(best_pair, s)
        gs.append(best_pair)
    best = jnp.zeros_like(gs[0], dtype=jnp.int32)
    top = gs[0]
    for g in range(1, N_GROUPS):
        better = gs[g] > top
        best = jnp.where(better, g, best)
        top = jnp.where(better, gs[g], top)
    picked = []
    for e in range(N_EXPERTS):
        g = e // EXPERTS_PER_GROUP
        rank = jnp.zeros_like(best)
        for o in range(g * EXPERTS_PER_GROUP, (g + 1) * EXPERTS_PER_GROUP):
            if o == e:
                continue
            ahead = (row[o] > row[e]) | ((row[o] == row[e]) & (o < e))
            rank = rank + ahead.astype(jnp.int32)
        picked.append((best == g) & (rank < 2))
    den = jnp.zeros_like(gs[0])
    for e in range(N_EXPERTS):
        den = den + jnp.where(picked[e], scores[e:e + 1, :], 0.0)
    gate = [jnp.where(picked[e], scores[e:e + 1, :] / den, 0.0) for e in range(N_EXPERTS)]
    cls = jnp.zeros_like(den)
    w_a = jnp.zeros_like(den)
    w_b = jnp.zeros_like(den)
    for g in range(N_GROUPS):
        for pi, (a, b) in enumerate(MOE_PAIRS):
            ea, eb = g * EXPERTS_PER_GROUP + a, g * EXPERTS_PER_GROUP + b
            both = picked[ea] & picked[eb]
            cls = jnp.where(both, float(g * len(MOE_PAIRS) + pi), cls)
            w_a = jnp.where(both, gate[ea], w_a)
            w_b = jnp.where(both, gate[eb], w_b)
    return jnp.concatenate([cls, w_a, w_b, jnp.zeros((ROUTE_ROWS - 3, cls.shape[1]), F32)], axis=0)


MOE_PAIRS = ((0, 1), (0, 2), (0, 3), (1, 3), (1, 2), (2, 3))
N_CLS = N_GROUPS * len(MOE_PAIRS)
CLS_PAD = 32
ROUTE_ROWS = 8
MOE_TS = 256
MOE_TM = 256
MOE_STEP_TILES = 4
SLOT_BLK = 512
ROW_W = D_MODEL + LANES


def _slots_kernel(n_blk, route_ref, slot_ref, off_ref, cnt_ref):
    cid = lax.broadcasted_iota(jnp.int32, (CLS_PAD, SLOT_BLK), 0).astype(F32)

    def members(j):
        cls = route_ref[0:1, pl.ds(pl.multiple_of(j * SLOT_BLK, SLOT_BLK), SLOT_BLK)]
        return (cid == cls).astype(F32)

    def count(j, cnt):
        return cnt + jnp.sum(members(j), axis=1, keepdims=True)

    cnt = lax.fori_loop(0, n_blk, count, jnp.zeros((CLS_PAD, 1), F32))
    cnt = jnp.broadcast_to(cnt, (CLS_PAD, LANES))
    padded = jnp.ceil(cnt * (1.0 / MOE_TS)) * MOE_TS
    r = lax.broadcasted_iota(jnp.int32, (CLS_PAD, CLS_PAD), 0)
    c = lax.broadcasted_iota(jnp.int32, (CLS_PAD, CLS_PAD), 1)
    off = jnp.dot((c < r).astype(F32), padded, precision=lax.Precision.HIGHEST, preferred_element_type=F32)
    off_ref[...] = off
    cnt_ref[...] = cnt
    tr = lax.broadcasted_iota(jnp.int32, (SLOT_BLK, SLOT_BLK), 0)
    tc = lax.broadcasted_iota(jnp.int32, (SLOT_BLK, SLOT_BLK), 1)
    earlier = (tr < tc).astype(BF16)

    def assign(j, base):
        member = members(j)
        rank = _dot(member.astype(BF16), earlier)
        slot = jnp.sum(member * (rank + base), axis=0, keepdims=True)
        slot_ref[0:1, pl.ds(pl.multiple_of(j * SLOT_BLK, SLOT_BLK), SLOT_BLK)] = slot.astype(jnp.int32)
        return base + jnp.sum(member, axis=1, keepdims=True)

    lax.fori_loop(0, n_blk, assign, off[:, 0:1])


def moe_slots(route):
    t = route.shape[1]
    stat = jax.ShapeDtypeStruct((CLS_PAD, LANES), F32)
    return pl.pallas_call(
        functools.partial(_slots_kernel, t // SLOT_BLK),
        out_shape=[jax.ShapeDtypeStruct((1, t), jnp.int32), stat, stat],
        compiler_params=pltpu.CompilerParams(vmem_limit_bytes=VMEM_LIMIT),
        name="moe_slots",
    )(route)


def _tile_maps(off, cnt, n_tiles):
    off = off[:N_CLS, 0].astype(jnp.int32)
    cnt = cnt[:N_CLS, 0].astype(jnp.int32)
    ends = off + ((cnt + MOE_TS - 1) // MOE_TS) * MOE_TS
    n_used = ends[-1] // MOE_TS
    k = jnp.arange(n_tiles, dtype=jnp.int32)
    tix = jnp.minimum(k, n_used - 1)
    cls = jnp.sum((tix[:, None] * MOE_TS >= ends[None, :]).astype(jnp.int32), axis=1)
    pair = jnp.asarray(MOE_PAIRS, jnp.int32)
    grp = (cls // len(MOE_PAIRS)) * EXPERTS_PER_GROUP
    ea = grp + pair[cls % len(MOE_PAIRS), 0]
    eb = grp + pair[cls % len(MOE_PAIRS), 1]
    n = jnp.int32(n_tiles)

    def slot_plan(e):
        chg = jnp.concatenate([jnp.ones((1,), jnp.int32), (e[1:] != e[:-1]).astype(jnp.int32)])
        at = jnp.where(chg == 1, k, n)
        nxt_at = jnp.concatenate([lax.cummin(at[::-1])[::-1][1:], n.reshape(1)])
        more = (nxt_at < n).astype(jnp.int32)
        nxt = e[jnp.minimum(nxt_at, n - 1)]
        par = (jnp.cumsum(chg) - 1) % 2
        return chg, nxt, more, par.astype(jnp.int32)

    plan_a, plan_b = slot_plan(ea), slot_plan(eb)
    chg, nxt, more, par = (jnp.stack([pa, pb]) for pa, pb in zip(plan_a, plan_b))
    return ea, eb, chg, nxt, more, par, n_used.reshape(1)


def _dispatch_kernel(n_steps, slots_ref, x_ref, g_ref, sc_ref, sh_ref, rt_ref, hs_in, hs_out, rowbuf, sem):
    del hs_in
    i = pl.program_id(0)
    cur = i % 2

    def wait_rows(s):
        pltpu.make_async_copy(rowbuf.at[s], rowbuf.at[s], sem.at[s]).wait()

    @pl.when(i >= 2)
    def _():
        wait_rows(cur)

    rowbuf[cur, :, 0:D_MODEL] = _normmod(x_ref[...], g_ref[...], sc_ref[0], sh_ref[0])
    rowbuf[cur, :, D_MODEL:ROW_W] = jnp.concatenate(
        [rt_ref[...], jnp.zeros((MOE_TM, LANES - ROUTE_ROWS), F32)], axis=1)

    for s in range(2):
        @pl.when(cur == s)
        def _():
            for r in range(MOE_TM):
                dst = slots_ref[i * MOE_TM + r]
                pltpu.make_async_copy(rowbuf.at[s, r], hs_out.at[dst], sem.at[s]).start()

    @pl.when(i == n_steps - 1)
    def _():
        wait_rows(cur)
        if n_steps >= 2:
            wait_rows(1 - cur)


def moe_dispatch(x, g, mod, route_t, slots, hs, seq_len, per_seq):
    t, d = x.shape
    n_steps = t // MOE_TM
    grid_spec = pltpu.PrefetchScalarGridSpec(
        num_scalar_prefetch=1,
        grid=(n_steps,),
        in_specs=[pl.BlockSpec((MOE_TM, d), lambda i, s: (i, 0)),
                  pl.BlockSpec((1, d), lambda i, s: (0, 0)),
                  _mod_spec(4, MOE_TM, seq_len, per_seq),
                  _mod_spec(3, MOE_TM, seq_len, per_seq),
                  pl.BlockSpec((MOE_TM, ROUTE_ROWS), lambda i, s: (i, 0)),
                  pl.BlockSpec(memory_space=pl.ANY)],
        out_specs=pl.BlockSpec(memory_space=pl.ANY),
        scratch_shapes=[pltpu.VMEM((2, MOE_TM, ROW_W), F32), pltpu.SemaphoreType.DMA((2,))],
    )
    return pl.pallas_call(
        functools.partial(_dispatch_kernel, n_steps),
        grid_spec=grid_spec,
        out_shape=jax.ShapeDtypeStruct(hs.shape, F32),
        input_output_aliases={6: 0},
        compiler_params=_cparams("arbitrary"),
        name="moe_dispatch",
    )(slots, x, g.reshape(1, d), mod, mod, route_t, hs)


def _experts_kernel(layer, ea_ref, eb_ref, chg_ref, nxt_ref, more_ref, par_ref, nused_ref,
                    hs_ref, wg_hbm, wu_hbm, wd_hbm, ys_ref, fg, fu, fd, bg, bu, bd, sem):
    def weight_copies(slot, expert, par):
        return [pltpu.make_async_copy(src.at[layer, expert], dst.at[slot, par], sem.at[slot, par])
                for src, dst in ((wg_hbm, fg), (wu_hbm, fu), (wd_hbm, fd))]

    def tile(k, rows):
        @pl.when(k < nused_ref[0])
        def _():
            for slot, e_ref in enumerate((ea_ref, eb_ref)):
                @pl.when(chg_ref[slot, k] == 1)
                def _():
                    par = par_ref[slot, k]

                    @pl.when(k == 0)
                    def _():
                        for cp in weight_copies(slot, e_ref[0], par):
                            cp.start()

                    for cp in weight_copies(slot, e_ref[k], par):
                        cp.wait()
                    bg[slot] = fg[slot, par].astype(BF16)
                    bu[slot] = fu[slot, par].astype(BF16)
                    bd[slot] = fd[slot, par].astype(BF16)

                    @pl.when(more_ref[slot, k] == 1)
                    def _():
                        for cp in weight_copies(slot, nxt_ref[slot, k], 1 - par):
                            cp.start()

            h = hs_ref[rows, 0:D_MODEL].astype(BF16)

            def ffn(slot):
                hid = _dot(h, bg[slot])
                up = _dot(h, bu[slot])
                w = hs_ref[rows, D_MODEL + 1 + slot:D_MODEL + 2 + slot]
                act = (hid * _sigmoid(hid)) * up * w
                return _dot(act.astype(BF16), bd[slot])

            ys_ref[rows, :] = ffn(0) + ffn(1)

        @pl.when(k >= nused_ref[0])
        def _():
            ys_ref[rows, :] = jnp.zeros((MOE_TS, ys_ref.shape[1]), ys_ref.dtype)

    for j in range(MOE_STEP_TILES):
        tile(pl.program_id(0) * MOE_STEP_TILES + j, slice(j * MOE_TS, (j + 1) * MOE_TS))


def moe_experts(hs, maps, layer, w_gate, w_up, w_down):
    n_tiles = hs.shape[0] // MOE_TS
    d = D_MODEL

    grid_spec = pltpu.PrefetchScalarGridSpec(
        num_scalar_prefetch=7,
        grid=(n_tiles // MOE_STEP_TILES,),
        in_specs=[pl.BlockSpec((MOE_STEP_TILES * MOE_TS, ROW_W), lambda k, *_: (k, 0)),
                  pl.BlockSpec(memory_space=pl.ANY), pl.BlockSpec(memory_space=pl.ANY),
                  pl.BlockSpec(memory_space=pl.ANY)],
        out_specs=pl.BlockSpec((MOE_STEP_TILES * MOE_TS, d), lambda k, *_: (k, 0)),
        scratch_shapes=[pltpu.VMEM((2, 2, d, D_EXPERT), F32), pltpu.VMEM((2, 2, d, D_EXPERT), F32),
                        pltpu.VMEM((2, 2, D_EXPERT, d), F32),
                        pltpu.VMEM((2, d, D_EXPERT), BF16), pltpu.VMEM((2, d, D_EXPERT), BF16),
                        pltpu.VMEM((2, D_EXPERT, d), BF16),
                        pltpu.SemaphoreType.DMA((2, 2))],
    )
    return pl.pallas_call(
        functools.partial(_experts_kernel, layer),
        grid_spec=grid_spec,
        out_shape=jax.ShapeDtypeStruct((hs.shape[0], d), F32),
        compiler_params=_cparams("arbitrary"),
        name="moe_experts",
    )(*maps, hs, w_gate, w_up, w_down)


def _combine_kernel(final, n_steps, slots_ref, x_ref, g2_ref, fg_ref, ys_hbm, o_ref, gbuf, sem):
    i = pl.program_id(0)
    cur = i % 2

    def issue_tile(tile, s):
        for r in range(MOE_TM):
            src = slots_ref[tile * MOE_TM + r]
            pltpu.make_async_copy(ys_hbm.at[src], gbuf.at[s, r], sem.at[s]).start()

    @pl.when(i == 0)
    def _():
        issue_tile(0, 0)

    for s in range(2):
        @pl.when((i + 1 < n_steps) & (1 - cur == s))
        def _():
            issue_tile(i + 1, s)

    pltpu.make_async_copy(gbuf.at[cur], gbuf.at[cur], sem.at[cur]).wait()
    y = x_ref[...] + g2_ref[0] * gbuf[cur]
    if final:
        ms = jnp.mean(y * y, axis=-1, keepdims=True)
        y = y * lax.rsqrt(ms + EPS) * fg_ref[...]
    o_ref[...] = y


def moe_combine(x, mod, slots, ys, final_g, final, seq_len, per_seq):
    t, d = x.shape
    n_steps = t // MOE_TM
    grid_spec = pltpu.PrefetchScalarGridSpec(
        num_scalar_prefetch=1,
        grid=(n_steps,),
        in_specs=[pl.BlockSpec((MOE_TM, d), lambda i, s: (i, 0)),
                  _mod_spec(5, MOE_TM, seq_len, per_seq),
                  pl.BlockSpec((1, d), lambda i, s: (0, 0)),
                  pl.BlockSpec(memory_space=pl.ANY)],
        out_specs=pl.BlockSpec((MOE_TM, d), lambda i, s: (i, 0)),
        scratch_shapes=[pltpu.VMEM((2, MOE_TM, d), F32), pltpu.SemaphoreType.DMA((2,))],
    )
    return pl.pallas_call(
        functools.partial(_combine_kernel, final, n_steps),
        grid_spec=grid_spec,
        out_shape=jax.ShapeDtypeStruct((t, d), F32),
        compiler_params=_cparams("arbitrary"),
        name="moe_combine",
    )(slots, x, mod, final_g.reshape(1, d), ys)


def _combine_nm_tm_kernel(nseq, steps, seq_len, n_steps, slots_ref, x_ref, g2_ref, ys_hbm, g_ref, sc_ref,
                          sh_ref, w_ref, x2_ref, u_ref, gbuf, sem):
    i = pl.program_id(0)
    cur = i % 2

    def issue_tile(tile, s):
        for sq in range(nseq):
            for t in range(steps):
                src = slots_ref[sq * seq_len + tile * steps + t]
                pltpu.make_async_copy(ys_hbm.at[src], gbuf.at[s, sq * steps + t], sem.at[s]).start()

    @pl.when(i == 0)
    def _():
        issue_tile(0, 0)

    for s in range(2):
        @pl.when((i + 1 < n_steps) & (1 - cur == s))
        def _():
            issue_tile(i + 1, s)

    pltpu.make_async_copy(gbuf.at[cur], gbuf.at[cur], sem.at[cur]).wait()
    d = x_ref.shape[-1]
    y = x_ref[...] + g2_ref[...] * gbuf[cur].reshape(nseq, steps, d)
    x2_ref[...] = y
    h = _normmod(y, g_ref[...], sc_ref[...], sh_ref[...])
    h = h.reshape(nseq * steps, d).astype(BF16)
    h = _dot(_row_permutation(nseq, steps, True), h).astype(BF16)
    u_ref[...] = _dot(h, w_ref[...])


def combine_nm_matmul_tm(x, mod_prev, slots, ys, g, mod, w, nseq, seq_len, per_seq):
    t, d = x.shape
    n = w.shape[1]
    steps = TM_ROWS // nseq
    n_steps = seq_len // steps
    g2 = _group_mod(mod_prev, 5, nseq, per_seq)
    sc = _group_mod(mod, 1, nseq, per_seq)
    sh = _group_mod(mod, 0, nseq, per_seq)
    mod_spec = pl.BlockSpec(sc.shape, lambda i, s: (0, 0, 0))
    x_spec = pl.BlockSpec((nseq, steps, d), lambda i, s: (0, i, 0))
    grid_spec = pltpu.PrefetchScalarGridSpec(
        num_scalar_prefetch=1,
        grid=(n_steps,),
        in_specs=[x_spec, mod_spec, pl.BlockSpec(memory_space=pl.ANY),
                  pl.BlockSpec((1, d), lambda i, s: (0, 0)), mod_spec, mod_spec,
                  pl.BlockSpec((d, n), lambda i, s: (0, 0))],
        out_specs=[x_spec, pl.BlockSpec((TM_ROWS, n), lambda i, s: (i, 0))],
        scratch_shapes=[pltpu.VMEM((2, TM_ROWS, d), F32), pltpu.SemaphoreType.DMA((2,))],
    )
    x2, u = pl.pallas_call(
        functools.partial(_combine_nm_tm_kernel, nseq, steps, seq_len, n_steps),
        grid_spec=grid_spec,
        out_shape=[jax.ShapeDtypeStruct((nseq, seq_len, d), F32), jax.ShapeDtypeStruct((t, n), F32)],
        compiler_params=_cparams("arbitrary"),
        name="moe_combine_norm_mod_proj_tm",
    )(slots, x.reshape(nseq, seq_len, d), g2, ys, g.reshape(1, d), sc, sh, w)
    return x2.reshape(t, d), u


def moe_block(xs, routes, mods_l, per_seqs, seq_lens, layer, p, final, defer_combine, hs):
    g = p['norm_g'][layer, 1]
    slots, off, cnt = moe_slots(jnp.concatenate(routes, axis=1))
    t_all = slots.shape[1]
    n_tiles = -(-(t_all // MOE_TS + N_CLS) // MOE_STEP_TILES) * MOE_STEP_TILES
    maps = _tile_maps(off, cnt, n_tiles)
    if hs is None:
        hs = jnp.zeros((n_tiles * MOE_TS, ROW_W), F32)
    bounds = np.cumsum([0] + [x.shape[0] for x in xs])
    group_slots = [slots[0, bounds[i]:bounds[i + 1]] for i in range(len(xs))]
    for x, r, s, sl, ps in zip(xs, routes, group_slots, seq_lens, per_seqs):
        hs = moe_dispatch(x, g, mods_l, r.T, s, hs, sl, ps)
    ys = moe_experts(hs, maps, layer, p['moe_w_gate'], p['moe_w_up'], p['moe_w_down'])
    if defer_combine:
        return [(x, s, ys) for x, s in zip(xs, group_slots)], hs
    return [moe_combine(x, mods_l, s, ys, p['final_g'], final, sl, ps)
            for x, s, sl, ps in zip(xs, group_slots, seq_lens, per_seqs)], hs


def _mixer(x, group, l, mod, p, hy_tables, mod_prev):
    per_seq, nseq, seq_len = group['per_seq'], group['nseq'], group['seq_len']
    extras = None
    deferred = isinstance(x, tuple)
    if deferred and l % 2 == 0:
        x = moe_combine(x[0], mod_prev, x[1], x[2], p['final_g'], False, seq_len, per_seq)
        deferred = False
    if l % 2 == 0:
        e = l // 2
        u = nm_matmul(x, p['norm_g'][l, 0], mod, p['a_in_w'][e].astype(BF16), seq_len, per_seq)
        if group['ctx_k'] is None:
            attn, nk, nv = context_attention(u, nseq, seq_len)
            extras = (nk, nv)
        else:
            attn = neighbourhood_attention(u, group['ctx_k'][:, e], group['ctx_v'][:, e], p['na_rpb'][e],
                                           nseq, seq_len)
        fwd, inv = hy_tables[seq_len]
        spectrum = hyena_spectrum(seq_len, p['hy_w1'][e], p['hy_b1'][e], p['hy_w2'][e], p['hy_b2'][e],
                                  p['hy_w3'][e], p['hy_freq'][e], p['hy_d'][e], fwd)
        hy = hyena_mixer(u, nseq, seq_len, p['hy_short_w'][e], p['hy_short_b'][e], spectrum, fwd, inv)
        w_out = p['a_out_w'][e].astype(BF16)
        x, route = proj_residual([attn, hy], [w_out[:D_A], w_out[D_A:]], x, mod, p['norm_g'][l, 1],
                                 p['router_w'], p['router_b'], seq_len, per_seq)
    else:
        o = l // 2
        w_in = p['c_in_w'][o].astype(BF16)
        if deferred:
            x, u = combine_nm_matmul_tm(x[0], mod_prev, x[1], x[2], p['norm_g'][l, 0], mod, w_in,
                                        nseq, seq_len, per_seq)
        else:
            u = nm_matmul_tm(x, p['norm_g'][l, 0], mod, w_in, nseq, seq_len, per_seq)
        y, extras = rglru_block(u, nseq, seq_len, p['rg_conv_w'][o], p['rg_conv_b'][o], p['rg_wa'][o],
                                p['rg_ba'][o], p['rg_wx'][o], p['rg_bx'][o], p['rg_lam'][o], group['h0'][o])
        x, route = proj_residual_tm(y, p['c_out_w'][o].astype(BF16), x, mod, p['norm_g'][l, 1],
                                    p['router_w'], p['router_b'], nseq, seq_len, per_seq)
    return x, extras, route


def kernel(x_prompt, x_sample, cache_k, cache_v, state_h, c, c_ctx, norm_g, ada_w, ada_b, final_g, a_in_w, a_out_w, na_rpb, hy_short_w, hy_short_b, hy_w1, hy_b1, hy_w2, hy_b2, hy_w3, hy_freq, hy_d, c_in_w, c_out_w, rg_conv_w, rg_conv_b, rg_wa, rg_ba, rg_wx, rg_bx, rg_lam, router_w, router_b, moe_w_gate, moe_w_up, moe_w_down):
    p = dict(norm_g=norm_g, final_g=final_g, a_in_w=a_in_w, a_out_w=a_out_w, na_rpb=na_rpb,
             hy_short_w=hy_short_w, hy_short_b=hy_short_b, hy_w1=hy_w1, hy_b1=hy_b1, hy_w2=hy_w2,
             hy_b2=hy_b2, hy_w3=hy_w3, hy_freq=hy_freq, hy_d=hy_d, c_in_w=c_in_w, c_out_w=c_out_w,
             rg_conv_w=rg_conv_w, rg_conv_b=rg_conv_b, rg_wa=rg_wa, rg_ba=rg_ba, rg_wx=rg_wx, rg_bx=rg_bx,
             rg_lam=rg_lam, router_w=router_w, router_b=router_b, moe_w_gate=moe_w_gate,
             moe_w_up=moe_w_up, moe_w_down=moe_w_down)
    batch, seq, d = x_prompt.shape
    dec_batch, dec_seq, _ = x_sample.shape
    n_odd = DEPTH // 2
    assert 1 + dec_batch <= MOD_ROWS

    cond = jnp.concatenate([c_ctx[None, :], c, jnp.zeros((MOD_ROWS - 1 - dec_batch, d), F32)], axis=0)
    m = modulation(cond, ada_w, ada_b)
    mods = [m[l].reshape(MOD_ROWS * N_MOD, 1, d) for l in range(DEPTH)]

    tables = {}
    for sl in (seq, dec_seq):
        fwd, inv = _dft_tables(sl)
        tables[sl] = (jnp.asarray(fwd).astype(BF16), jnp.asarray(inv).astype(BF16))

    groups = [
        dict(per_seq=False, nseq=batch, seq_len=seq, ctx_k=None, ctx_v=None,
             h0=[jnp.zeros((2, batch, D_RNN), F32)] * n_odd),
        dict(per_seq=True, nseq=dec_batch, seq_len=dec_seq, ctx_k=cache_k, ctx_v=cache_v,
             h0=[state_h[:, o].transpose(1, 0, 2) for o in range(n_odd)]),
    ]
    xs = [x_prompt.reshape(batch * seq, d), x_sample.reshape(dec_batch * dec_seq, d)]
    k_list, v_list, h_list = [], [], []
    hs = None
    for l in range(DEPTH):
        mixed = [_mixer(x, grp, l, mods[l], p, tables, mods[l - 1] if l else None)
                 for x, grp in zip(xs, groups)]
        if l % 2 == 0:
            k_list.append(mixed[0][1][0])
            v_list.append(mixed[0][1][1])
        else:
            h_list.append(mixed[0][1].transpose(1, 0, 2))
        xs, hs = moe_block([mx[0] for mx in mixed], [mx[2] for mx in mixed], mods[l],
                           [grp['per_seq'] for grp in groups], [grp['seq_len'] for grp in groups],
                           l, p, l == DEPTH - 1, l < DEPTH - 1, hs)
    new_k = jnp.stack(k_list, axis=1)
    new_v = jnp.stack(v_list, axis=1)
    new_h = jnp.stack(h_list, axis=1)
    return (xs[0].reshape(batch, seq, d), xs[1].reshape(dec_batch, dec_seq, d), new_k, new_v, new_h)
```

```python
import functools
import math

import numpy as np
import jax
import jax.numpy as jnp
from jax import lax
from jax.experimental import pallas as pl
from jax.experimental.pallas import tpu as pltpu

F32 = jnp.float32
BF16 = jnp.bfloat16

D_MODEL = 1024
DEPTH = 2
GRID_W = 64
EPS = 1e-6
NEG_INF = -1e30
NA_HEADS = 8
HEAD_DIM = 64
D_A = NA_HEADS * HEAD_DIM
WIN_ROWS = 8
WIN_COLS = 16
D_B = D_MODEL - D_A
HY_ORDER = 2
HY_EMB = 33
HY_BANDS = (HY_EMB - 1) // 2
HY_FFN = 64
HY_DECAY_TARGET = 1e-2
HY_FAST_PCT = 0.3
HY_SLOW_PCT = 1.5
D_RNN = D_MODEL
RG_BLOCK = 64
RG_C = 8.0
N_EXPERTS = 16
N_GROUPS = 4
EXPERTS_PER_GROUP = N_EXPERTS // N_GROUPS
D_EXPERT = 512

LANES = 128
VMEM_LIMIT = 56 * 1024 * 1024
N_MOD = 6
MOD_ROWS = 16


def _cparams(*sem):
    return pltpu.CompilerParams(dimension_semantics=sem, vmem_limit_bytes=VMEM_LIMIT)


def _dot(a, b):
    return jnp.dot(a, b, preferred_element_type=F32)


def _dot_nt(a, b):
    return lax.dot_general(a, b, (((1,), (1,)), ((), ())), preferred_element_type=F32)


def _sigmoid(x):
    return 0.5 * jnp.tanh(0.5 * x) + 0.5


def _normmod(x, g, sc, sh):
    ms = jnp.mean(x * x, axis=-1, keepdims=True)
    return (x * lax.rsqrt(ms + EPS) * g) * (1.0 + sc) + sh


def _mod_spec(chunk, tm, seq_len, per_seq):
    if per_seq:
        return pl.BlockSpec((1, 1, D_MODEL), lambda i, *_: ((1 + (i * tm) // seq_len) * N_MOD + chunk, 0, 0))
    return pl.BlockSpec((1, 1, D_MODEL), lambda i, *_: (chunk, 0, 0))


def _mod_kernel(c_ref, w_ref, b_ref, o_ref):
    s = c_ref[...]
    s = s * jax.nn.sigmoid(s)
    o_ref[0] = _dot(s.astype(BF16), w_ref[0].astype(BF16)) + b_ref[0]


def modulation(cond, ada_w, ada_b):
    n = ada_w.shape[-1]
    tn = n // 4
    assert tn % LANES == 0
    return pl.pallas_call(
        _mod_kernel,
        grid=(DEPTH, n // tn),
        in_specs=[pl.BlockSpec((MOD_ROWS, D_MODEL), lambda l, j: (0, 0)),
                  pl.BlockSpec((1, D_MODEL, tn), lambda l, j: (l, 0, j)),
                  pl.BlockSpec((1, 1, tn), lambda l, j: (l, 0, j))],
        out_specs=pl.BlockSpec((1, MOD_ROWS, tn), lambda l, j: (l, 0, j)),
        out_shape=jax.ShapeDtypeStruct((DEPTH, MOD_ROWS, n), F32),
        compiler_params=_cparams("arbitrary", "arbitrary"),
        name="modulation",
    )(cond, ada_w, ada_b.reshape(DEPTH, 1, n))


def _nm_matmul_kernel(x_ref, g_ref, sc_ref, sh_ref, w_ref, o_ref):
    h = _normmod(x_ref[...], g_ref[...], sc_ref[0], sh_ref[0])
    o_ref[...] = _dot(h.astype(BF16), w_ref[...])


def nm_matmul(x, g, mod, w, seq_len, per_seq, tm=512):
    t, d = x.shape
    n = w.shape[1]
    return pl.pallas_call(
        _nm_matmul_kernel,
        grid=(t // tm,),
        in_specs=[pl.BlockSpec((tm, d), lambda i: (i, 0)),
                  pl.BlockSpec((1, d), lambda i: (0, 0)),
                  _mod_spec(1, tm, seq_len, per_seq),
                  _mod_spec(0, tm, seq_len, per_seq),
                  pl.BlockSpec((d, n), lambda i: (0, 0))],
        out_specs=pl.BlockSpec((tm, n), lambda i: (i, 0)),
        out_shape=jax.ShapeDtypeStruct((t, n), F32),
        compiler_params=_cparams("arbitrary"),
        name="norm_mod_proj",
    )(x, g.reshape(1, d), mod, mod, w)


def _proj_res_kernel(n_act, *refs):
    acts = refs[:n_act]
    ws = refs[n_act:2 * n_act]
    x_ref, g_ref, ng_ref, sc2_ref, sh2_ref, rw_ref, rb_ref, o_ref, route_ref = refs[2 * n_act:]
    acc = _dot(acts[0][...].astype(BF16), ws[0][...])
    for a, w in zip(acts[1:], ws[1:]):
        acc += _dot(a[...].astype(BF16), w[...])
    x = x_ref[...] + g_ref[0] * acc
    o_ref[...] = x
    route_ref[...] = _route_record(_normmod(x, ng_ref[...], sc2_ref[0], sh2_ref[0]), rw_ref[...], rb_ref[...])


def proj_residual(acts, ws, x, mod, norm2_g, router_w, router_b, seq_len, per_seq, tm=512):
    t, d = x.shape
    in_specs = [pl.BlockSpec((tm, a.shape[1]), lambda i: (i, 0)) for a in acts]
    in_specs += [pl.BlockSpec(w.shape, lambda i: (0, 0)) for w in ws]
    in_specs += [pl.BlockSpec((tm, d), lambda i: (i, 0)), _mod_spec(2, tm, seq_len, per_seq),
                 pl.BlockSpec((1, d), lambda i: (0, 0)),
                 _mod_spec(4, tm, seq_len, per_seq), _mod_spec(3, tm, seq_len, per_seq),
                 pl.BlockSpec((N_EXPERTS, d), lambda i: (0, 0)),
                 pl.BlockSpec((N_EXPERTS, 1), lambda i: (0, 0))]
    return pl.pallas_call(
        functools.partial(_proj_res_kernel, len(acts)),
        grid=(t // tm,),
        in_specs=in_specs,
        out_specs=[pl.BlockSpec((tm, d), lambda i: (i, 0)), pl.BlockSpec((ROUTE_ROWS, tm), lambda i: (0, i))],
        out_shape=[jax.ShapeDtypeStruct((t, d), F32), jax.ShapeDtypeStruct((ROUTE_ROWS, t), F32)],
        compiler_params=_cparams("arbitrary"),
        name="proj_residual_route",
    )(*acts, *ws, x, mod, norm2_g.reshape(1, d), mod, mod, router_w.T, router_b.reshape(N_EXPERTS, 1))


def _row_permutation(nseq, steps, to_time_major):
    n = nseq * steps
    i = lax.broadcasted_iota(jnp.int32, (n, n), 0)
    j = lax.broadcasted_iota(jnp.int32, (n, n), 1)
    if to_time_major:
        src = (i % nseq) * steps + i // nseq
    else:
        src = (i % steps) * nseq + i // steps
    return (j == src).astype(BF16)


def _nm_matmul_tm_kernel(nseq, steps, x_ref, g_ref, sc_ref, sh_ref, w_ref, o_ref):
    h = _normmod(x_ref[...], g_ref[...], sc_ref[...], sh_ref[...])
    h = h.reshape(nseq * steps, h.shape[-1]).astype(BF16)
    h = _dot(_row_permutation(nseq, steps, True), h).astype(BF16)
    o_ref[...] = _dot(h, w_ref[...])


def _group_mod(mod, chunk, nseq, per_seq):
    rows = mod.reshape(MOD_ROWS, N_MOD, 1, D_MODEL)
    return rows[1:1 + nseq, chunk] if per_seq else rows[0:1, chunk]


def nm_matmul_tm(x, g, mod, w, nseq, seq_len, per_seq):
    t, d = x.shape
    n = w.shape[1]
    steps = TM_ROWS // nseq
    sc = _group_mod(mod, 1, nseq, per_seq)
    sh = _group_mod(mod, 0, nseq, per_seq)
    mod_spec = pl.BlockSpec(sc.shape, lambda i: (0, 0, 0))
    return pl.pallas_call(
        functools.partial(_nm_matmul_tm_kernel, nseq, steps),
        grid=(seq_len // steps,),
        in_specs=[pl.BlockSpec((nseq, steps, d), lambda i: (0, i, 0)),
                  pl.BlockSpec((1, d), lambda i: (0, 0)),
                  mod_spec, mod_spec,
                  pl.BlockSpec((d, n), lambda i: (0, 0))],
        out_specs=pl.BlockSpec((TM_ROWS, n), lambda i: (i, 0)),
        out_shape=jax.ShapeDtypeStruct((t, n), F32),
        compiler_params=_cparams("arbitrary"),
        name="norm_mod_proj_tm",
    )(x.reshape(nseq, seq_len, d), g.reshape(1, d), sc, sh, w)


def _proj_res_tm_kernel(nseq, steps, y_ref, w_ref, x_ref, g_ref, ng_ref, sc2_ref, sh2_ref, rw_ref, rb_ref,
                        o_ref, route_ref):
    y = _dot(_row_permutation(nseq, steps, False), y_ref[...].astype(BF16)).astype(BF16)
    acc = _dot(y, w_ref[...])
    x = x_ref[...] + g_ref[...] * acc.reshape(nseq, steps, acc.shape[-1])
    o_ref[...] = x
    h = _normmod(x, ng_ref[...], sc2_ref[...], sh2_ref[...]).reshape(nseq * steps, x.shape[-1])
    route_ref[0] = _route_record(h, rw_ref[...], rb_ref[...])


def proj_residual_tm(y, w, x, mod, norm2_g, router_w, router_b, nseq, seq_len, per_seq):
    t, d = x.shape
    steps = TM_ROWS // nseq
    n_steps = seq_len // steps
    g1 = _group_mod(mod, 2, nseq, per_seq)
    sc2 = _group_mod(mod, 4, nseq, per_seq)
    sh2 = _group_mod(mod, 3, nseq, per_seq)
    mod_spec = pl.BlockSpec(g1.shape, lambda i: (0, 0, 0))
    out, route = pl.pallas_call(
        functools.partial(_proj_res_tm_kernel, nseq, steps),
        grid=(n_steps,),
        in_specs=[pl.BlockSpec((TM_ROWS, y.shape[1]), lambda i: (i, 0)),
                  pl.BlockSpec(w.shape, lambda i: (0, 0)),
                  pl.BlockSpec((nseq, steps, d), lambda i: (0, i, 0)),
                  mod_spec,
                  pl.BlockSpec((1, d), lambda i: (0, 0)),
                  mod_spec, mod_spec,
                  pl.BlockSpec((N_EXPERTS, d), lambda i: (0, 0)),
                  pl.BlockSpec((N_EXPERTS, 1), lambda i: (0, 0))],
        out_specs=[pl.BlockSpec((nseq, steps, d), lambda i: (0, i, 0)),
                   pl.BlockSpec((1, ROUTE_ROWS, TM_ROWS), lambda i: (i, 0, 0))],
        out_shape=[jax.ShapeDtypeStruct((nseq, seq_len, d), F32),
                   jax.ShapeDtypeStruct((n_steps, ROUTE_ROWS, TM_ROWS), F32)],
        compiler_params=_cparams("arbitrary"),
        name="proj_residual_tm_route",
    )(y, w, x.reshape(nseq, seq_len, d), g1, norm2_g.reshape(1, d), sc2, sh2, router_w.T,
      router_b.reshape(N_EXPERTS, 1))
    route = route.reshape(n_steps, ROUTE_ROWS, nseq, steps).transpose(1, 2, 0, 3).reshape(ROUTE_ROWS, t)
    return out.reshape(t, d), route


CTX_STEP_ROWS = 1024


def _ctx_attn_kernel(seq_len, n_seq, q_ref, k_ref, v_ref, o_ref, nk_ref, nv_ref):
    scale = HEAD_DIM ** -0.5
    per_tile = LANES // HEAD_DIM
    lane = lax.broadcasted_iota(jnp.int32, (1, LANES), 1)
    for b in range(n_seq):
        rows = slice(b * seq_len, (b + 1) * seq_len)
        for hp in range(NA_HEADS // per_tile):
            sl = slice(hp * LANES, (hp + 1) * LANES)
            q, k, v = q_ref[rows, sl] * scale, k_ref[rows, sl], v_ref[rows, sl]
            kb, vb = k.astype(BF16), v.astype(BF16)
            out = None
            for j in range(per_tile):
                h = hp * per_tile + j
                nk_ref[b, h] = k[:, j * HEAD_DIM:(j + 1) * HEAD_DIM]
                nv_ref[b, h] = v[:, j * HEAD_DIM:(j + 1) * HEAD_DIM]
                mine = lane // HEAD_DIM == j
                s = _dot_nt(jnp.where(mine, q, 0.0).astype(BF16), kb)
                p = jnp.exp(s - jnp.max(s, axis=-1, keepdims=True))
                o = _dot(p.astype(BF16), vb) / jnp.sum(p, axis=-1, keepdims=True)
                out = o if out is None else jnp.where(mine, o, out)
            o_ref[rows, sl] = out.astype(o_ref.dtype)


def context_attention(u, nseq, seq_len):
    t = u.shape[0]
    per_step = max(1, CTX_STEP_ROWS // seq_len)
    assert nseq % per_step == 0
    rows = per_step * seq_len
    kv_shape = jax.ShapeDtypeStruct((nseq, NA_HEADS, seq_len, HEAD_DIM), F32)
    kv_spec = pl.BlockSpec((per_step, NA_HEADS, seq_len, HEAD_DIM), lambda b: (b, 0, 0, 0))
    return pl.pallas_call(
        functools.partial(_ctx_attn_kernel, seq_len, per_step),
        grid=(nseq // per_step,),
        in_specs=[pl.BlockSpec((rows, D_A), lambda b: (b, 0)),
                  pl.BlockSpec((rows, D_A), lambda b: (b, 1)),
                  pl.BlockSpec((rows, D_A), lambda b: (b, 2))],
        out_specs=[pl.BlockSpec((rows, D_A), lambda b: (b, 0)), kv_spec, kv_spec],
        out_shape=[jax.ShapeDtypeStruct((t, D_A), BF16), kv_shape, kv_shape],
        compiler_params=_cparams("arbitrary"),
        name="context_attention",
    )(u, u, u)


N_DR = 2 * WIN_ROWS - 1
N_DC = 2 * WIN_COLS - 1
N_DC_PAD = 32


def _na_col_tables():
    cols = np.arange(GRID_W)
    col_start = np.clip(cols - WIN_COLS // 2, 0, GRID_W - WIN_COLS)
    col_in = (cols[None, :] >= col_start[:, None]) & (cols[None, :] < col_start[:, None] + WIN_COLS)
    dc = np.clip(cols[None, :] - cols[:, None], 1 - WIN_COLS, WIN_COLS - 1) + WIN_COLS - 1
    onehot = (dc.reshape(1, -1) == np.arange(N_DC_PAD)[:, None]).astype(np.float32)
    return onehot, col_in.reshape(1, -1).astype(np.float32)


def _na_bias_kernel(r_ref, e_ref, m_ref, o_ref):
    t = jnp.dot(r_ref[...], e_ref[...], precision=lax.Precision.HIGHEST, preferred_element_type=F32)
    o_ref[...] = jnp.where(m_ref[...] > 0.0, t, NEG_INF)


def na_bias_table(rpb):
    onehot, col_in = _na_col_tables()
    n_rows = NA_HEADS * N_DR
    assert n_rows <= LANES
    r = jnp.zeros((LANES, N_DC_PAD), F32).at[:n_rows, :N_DC].set(rpb.reshape(n_rows, N_DC).astype(F32))
    t = pl.pallas_call(
        _na_bias_kernel,
        out_shape=jax.ShapeDtypeStruct((LANES, GRID_W * GRID_W), F32),
        name="na_bias_table",
    )(r, jnp.asarray(onehot), jnp.asarray(col_in))
    t = t[:n_rows].reshape(NA_HEADS, N_DR, GRID_W, GRID_W)
    return jnp.concatenate([t[:, :-1], t[:, 1:]], axis=-1)


def _na_kernel(rows, q_ref, k_ref, v_ref, ck_ref, cv_ref, bias_ref, o_ref,
               q_s, k_s, v_s, ck_s, cv_s, s_s, p_s, den_s, o_s):
    scale = HEAD_DIM ** -0.5
    n_lat = WIN_ROWS * GRID_W
    per_tile = LANES // HEAD_DIM
    n_pairs = NA_HEADS // per_tile
    lane = lax.broadcasted_iota(jnp.int32, (1, LANES), 1)
    for hp in range(n_pairs):
        sl = slice(hp * LANES, (hp + 1) * LANES)
        q_s[hp] = (q_ref[:, sl] * scale).astype(BF16)
        k_s[hp] = k_ref[:, sl].astype(BF16)
        v_s[hp] = v_ref[:, sl].astype(BF16)
        heads = range(hp * per_tile, (hp + 1) * per_tile)
        ck_s[hp] = jnp.concatenate([ck_ref[0, h] for h in heads], axis=1).astype(BF16)
        cv_s[hp] = jnp.concatenate([cv_ref[0, h] for h in heads], axis=1).astype(BF16)

    def window(r):
        start = min(max(r - WIN_ROWS // 2, 0), rows - WIN_ROWS)
        return start, start - r + WIN_ROWS - 1

    def pair_body(hp, carry):
        for j in range(per_tile):
            h = hp * per_tile + j
            mine = lane // HEAD_DIM == j
            for r in range(rows):
                start, off = window(r)
                rs = slice(r * GRID_W, (r + 1) * GRID_W)
                q = jnp.where(mine, q_s[hp, rs, :], 0.0).astype(BF16)
                bias = jnp.concatenate([bias_ref[h, off + 2 * i] for i in range(WIN_ROWS // 2)], axis=1)
                s_s[rs, 0:n_lat] = _dot_nt(q, k_s[hp, start * GRID_W:start * GRID_W + n_lat, :]) + bias
                s_s[rs, n_lat:] = _dot_nt(q, ck_s[hp])
            for r in range(rows):
                rs = slice(r * GRID_W, (r + 1) * GRID_W)
                s = s_s[rs, :]
                p = jnp.exp(s - jnp.max(s, axis=-1, keepdims=True))
                den_s[rs, :] = jnp.sum(p, axis=-1, keepdims=True)
                p_s[rs, :] = p.astype(BF16)
            for r in range(rows):
                start, _ = window(r)
                rs = slice(r * GRID_W, (r + 1) * GRID_W)
                o = (_dot(p_s[rs, 0:n_lat], v_s[hp, start * GRID_W:start * GRID_W + n_lat, :])
                     + _dot(p_s[rs, n_lat:], cv_s[hp])) / den_s[rs, :]
                o_s[hp, rs, :] = o if j == 0 else jnp.where(mine, o, o_s[hp, rs, :])
        return carry

    lax.fori_loop(0, n_pairs, pair_body, 0)
    for hp in range(n_pairs):
        o_ref[:, hp * LANES:(hp + 1) * LANES] = o_s[hp].astype(o_ref.dtype)


def neighbourhood_attention(u, ctx_k, ctx_v, rpb, nseq, seq_len):
    t = u.shape[0]
    rows = seq_len // GRID_W
    assert rows >= WIN_ROWS and WIN_ROWS % 2 == 0
    past = ctx_k.shape[2]
    bias = na_bias_table(rpb)
    ctx_spec = pl.BlockSpec((1, NA_HEADS, past, HEAD_DIM), lambda b: (b, 0, 0, 0))
    return pl.pallas_call(
        functools.partial(_na_kernel, rows),
        grid=(nseq,),
        in_specs=[pl.BlockSpec((seq_len, D_A), lambda b: (b, 0)),
                  pl.BlockSpec((seq_len, D_A), lambda b: (b, 1)),
                  pl.BlockSpec((seq_len, D_A), lambda b: (b, 2)),
                  ctx_spec, ctx_spec,
                  pl.BlockSpec(bias.shape, lambda b: (0, 0, 0, 0))],
        out_specs=pl.BlockSpec((seq_len, D_A), lambda b: (b, 0)),
        out_shape=jax.ShapeDtypeStruct((t, D_A), BF16),
        scratch_shapes=[pltpu.VMEM((D_A // LANES, seq_len, LANES), BF16)] * 3
        + [pltpu.VMEM((D_A // LANES, past, LANES), BF16)] * 2
        + [pltpu.VMEM((seq_len, WIN_ROWS * GRID_W + past), F32),
           pltpu.VMEM((seq_len, WIN_ROWS * GRID_W + past), BF16),
           pltpu.VMEM((seq_len, 1), F32),
           pltpu.VMEM((D_A // LANES, seq_len, LANES), F32)],
        compiler_params=_cparams("arbitrary"),
        name="neighbourhood_attention",
    )(u, u, u, ctx_k, ctx_v, bias)


HY_STEP_ROWS = 1024


def _dft_tables(seq_len):
    n = 2 * seq_len
    f = np.arange(seq_len, dtype=np.int64)
    ang = (np.outer(f, f) % n).astype(np.float64) * (math.pi / seq_len)
    cos, sin = np.cos(ang), np.sin(ang)
    alt = np.where(f % 2 == 0, 1.0, -1.0)
    s_fwd = -sin
    s_fwd[0, :] = alt
    fwd = np.concatenate([cos, s_fwd], axis=0)
    wf = np.where(f == 0, 1.0, 2.0) / n
    ci = cos.T * wf[None, :]
    si = -sin.T * wf[None, :]
    si[:, 0] = alt / n
    inv = np.concatenate([ci, si], axis=1)
    return fwd.astype(np.float32), inv.astype(np.float32)


def _hyena_feats(seq_len):
    t = np.linspace(0.0, 1.0, seq_len, dtype=np.float32)[:, None]
    w = (2.0 * math.pi * np.arange(seq_len, dtype=np.float32)[:, None] / seq_len).astype(np.float32)
    f = np.linspace(1e-4, HY_BANDS - 1, HY_BANDS, dtype=np.float32)[None, :]
    z = np.concatenate([t, np.cos(f * w), -np.sin(f * w)], axis=-1).astype(np.float32)
    max_decay = math.log(HY_DECAY_TARGET) / HY_FAST_PCT
    min_decay = math.log(HY_DECAY_TARGET) / HY_SLOW_PCT
    deltas = np.abs(np.linspace(min_decay, max_decay, D_B, dtype=np.float32))[None, :]
    return z, t, deltas


def _hy_filter_kernel(seq_len, z_ref, t_ref, dl_ref, w1_ref, b1_ref, w2_ref, b2_ref, w3_ref, fr_ref,
                      d_ref, fwd_ref, g_ref):
    hp = lax.Precision.HIGHEST
    h = jnp.sin(fr_ref[0:1, :] * (jnp.dot(z_ref[...], w1_ref[...], precision=hp) + b1_ref[...]))
    h = jnp.sin(fr_ref[1:2, :] * (jnp.dot(h, w2_ref[...], precision=hp) + b2_ref[...]))
    h = jnp.dot(h, w3_ref[...], precision=hp)
    decay = jnp.exp(-t_ref[...] * dl_ref[...])
    row0 = lax.broadcasted_iota(jnp.int32, (seq_len, D_B), 0) == 0
    sums, diffs = [], []
    for n in range(HY_ORDER):
        hf = h[:, (2 * n) * D_B:(2 * n + 1) * D_B] * decay
        hb = h[:, (2 * n + 1) * D_B:(2 * n + 2) * D_B] * decay
        gp = jnp.where(row0, hf + hb + d_ref[n:n + 1, :], hf)
        gm = jnp.where(row0, 0.0, hb)
        sums.append(gp + gm)
        diffs.append(gp - gm)
    rhs = jnp.concatenate(sums + diffs, axis=1).astype(BF16)
    spec = _dot(fwd_ref[...], rhs)
    for n in range(HY_ORDER):
        a = spec[:, n * D_B:(n + 1) * D_B]
        b = spec[:, (HY_ORDER + n) * D_B:(HY_ORDER + n + 1) * D_B]
        g_ref[n, 0:seq_len, :] = a[0:seq_len]
        g_ref[n, seq_len:, :] = jnp.where(row0, a[seq_len:], b[seq_len:])


def hyena_spectrum(seq_len, w1, b1, w2, b2, w3, freq, d, fwd):
    z, t, deltas = _hyena_feats(seq_len)
    return pl.pallas_call(
        functools.partial(_hy_filter_kernel, seq_len),
        out_shape=jax.ShapeDtypeStruct((HY_ORDER, 2 * seq_len, D_B), F32),
        compiler_params=pltpu.CompilerParams(vmem_limit_bytes=VMEM_LIMIT),
        name="hyena_spectrum",
    )(jnp.asarray(z), jnp.asarray(t), jnp.asarray(deltas), w1, b1.reshape(1, -1), w2, b2.reshape(1, -1),
      w3, freq, d, fwd)


def _hyena_kernel(seq_len, n_seq, u_ref, sw_ref, sb_ref, g_ref, fwd_ref, inv_ref, o_ref):
    t_idx = lax.broadcasted_iota(jnp.int32, (seq_len, u_ref.shape[1]), 0)
    row0 = lax.broadcasted_iota(jnp.int32, (seq_len, D_B), 0) == 0
    for s in range(n_seq):
        rows = slice(s * seq_len, (s + 1) * seq_len)
        u = u_ref[rows, :]
        prev = jnp.where(t_idx == 0, 0.0, pltpu.roll(u, 1, axis=0))
        nxt = jnp.where(t_idx == seq_len - 1, 0.0, pltpu.roll(u, seq_len - 1, axis=0))
        u = prev * sw_ref[0:1, :] + u * sw_ref[1:2, :] + nxt * sw_ref[2:3, :] + sb_ref[...]
        z = u[:, 0:D_B]
        for n in range(HY_ORDER):
            spec = _dot(fwd_ref[...], z.astype(BF16))
            ure, uim = spec[0:seq_len], spec[seq_len:]
            gre, gim = g_ref[n, 0:seq_len, :], g_ref[n, seq_len:, :]
            pim = uim * gim
            yre = ure * gre - jnp.where(row0, 0.0, pim)
            yim = jnp.where(row0, pim, ure * gim + uim * gre)
            y = jnp.concatenate([yre, yim], axis=0).astype(BF16)
            z = u[:, (n + 1) * D_B:(n + 2) * D_B] * _dot(inv_ref[...], y)
        o_ref[rows, :] = z.astype(o_ref.dtype)


def hyena_mixer(u, nseq, seq_len, short_w, short_b, spectrum, fwd, inv):
    t = u.shape[0]
    width = (HY_ORDER + 1) * D_B
    col_block = (3 * D_A) // width
    assert col_block * width == 3 * D_A
    per_step = max(1, HY_STEP_ROWS // seq_len)
    assert nseq % per_step == 0
    return pl.pallas_call(
        functools.partial(_hyena_kernel, seq_len, per_step),
        grid=(nseq // per_step,),
        in_specs=[pl.BlockSpec((per_step * seq_len, width), lambda b: (b, col_block)),
                  pl.BlockSpec(short_w.shape, lambda b: (0, 0)),
                  pl.BlockSpec((1, width), lambda b: (0, 0)),
                  pl.BlockSpec(spectrum.shape, lambda b: (0, 0, 0)),
                  pl.BlockSpec(fwd.shape, lambda b: (0, 0)),
                  pl.BlockSpec(inv.shape, lambda b: (0, 0))],
        out_specs=pl.BlockSpec((per_step * seq_len, D_B), lambda b: (b, 0)),
        out_shape=jax.ShapeDtypeStruct((t, D_B), BF16),
        compiler_params=_cparams("arbitrary"),
        name="hyena_mixer",
    )(u, short_w, short_b.reshape(1, width), spectrum, fwd, inv)


RG_CB = LANES
RG_CHUNK = 512
TM_ROWS = 512


def _rglru_kernel(nseq, seq_len, gate_ref, xr_ref, cw_ref, cb_ref, wg_ref, bg_ref, lam_ref, h0_ref,
                  y_ref, fin_ref, xp_ref, a_f, b_f, a_b, b_b):
    t_tot = nseq * seq_len
    c = RG_CB
    pad = 2 * nseq
    xp_ref[0:pad, :] = jnp.zeros((pad, c), F32)
    xp_ref[pad + t_tot:, :] = jnp.zeros((pad, c), F32)
    xp_ref[pad:pad + t_tot, :] = xr_ref[...]
    nl = -lam_ref[...]
    sp = jnp.maximum(nl, 0.0) + jnp.log1p(jnp.exp(-jnp.abs(nl)))
    k2 = (-0.5 * RG_C * math.log2(math.e)) * sp

    def gate_chunk(ci, carry):
        r0 = pl.multiple_of(ci * RG_CHUNK, RG_CHUNK)
        xc = xp_ref[pl.ds(r0, RG_CHUNK), :] * cw_ref[0:1, :]
        for j in range(1, cw_ref.shape[0]):
            xc = xc + xp_ref[pl.ds(r0 + j * nseq, RG_CHUNK), :] * cw_ref[j:j + 1, :]
        xc = xc + cb_ref[...]
        gts = _dot(xc.astype(BF16), wg_ref[0]) + bg_ref[...]
        x_half = 0.5 * xc
        for d, (a_ref, b_ref) in enumerate(((a_f, b_f), (a_b, b_b))):
            t_r = jnp.tanh(gts[:, (2 * d) * c:(2 * d + 1) * c])
            t_i = jnp.tanh(gts[:, (2 * d + 1) * c:(2 * d + 2) * c])
            a = jnp.exp2(t_r * k2[d:d + 1, :] + k2[d:d + 1, :])
            a_ref[pl.ds(r0, RG_CHUNK), :] = a
            y = 1.0 - a * a
            root = jnp.where(y > 0.0, y * lax.rsqrt(y), 0.0)
            b_ref[pl.ds(r0, RG_CHUNK), :] = root * ((t_i + 1.0) * x_half)
        return carry

    lax.fori_loop(0, t_tot // RG_CHUNK, gate_chunk, 0)

    def scan_step(t, carry):
        hf, hb = carry
        rows_f = pl.ds(pl.multiple_of(t * nseq, nseq), nseq)
        rows_b = pl.ds(pl.multiple_of((seq_len - 1 - t) * nseq, nseq), nseq)
        hf = a_f[rows_f, :] * hf + b_f[rows_f, :]
        hb = a_b[rows_b, :] * hb + b_b[rows_b, :]
        b_f[rows_f, :] = hf
        b_b[rows_b, :] = hb
        return hf, hb

    hf, hb = lax.fori_loop(0, seq_len, scan_step, (h0_ref[0], h0_ref[1]), unroll=8)
    fin_ref[0] = hf
    fin_ref[1] = hb

    def out_chunk(ci, carry):
        rs = pl.ds(pl.multiple_of(ci * RG_CHUNK, RG_CHUNK), RG_CHUNK)
        y_ref[rs, :] = ((b_f[rs, :] + b_b[rs, :]) * jax.nn.gelu(gate_ref[rs, :])).astype(y_ref.dtype)
        return carry

    lax.fori_loop(0, t_tot // RG_CHUNK, out_chunk, 0)


def _rg_gate_weights(wa, wx):
    per_step = RG_CB // RG_BLOCK
    steps = D_RNN // RG_CB
    mats = []
    for d in range(2):
        for w in (wa[d], wx[d]):
            w = w.reshape(steps, per_step, RG_BLOCK, RG_BLOCK)
            eye = jnp.eye(per_step, dtype=w.dtype)
            m = jnp.einsum('spde,pq->spdqe', w, eye).reshape(steps, RG_CB, RG_CB)
            mats.append(m)
    return (0.5 * jnp.concatenate(mats, axis=-1)).astype(BF16)


def rglru_block(u, nseq, seq_len, conv_w, conv_b, wa, ba, wx, bx, lam, h0):
    t = u.shape[0]
    c = RG_CB
    steps = D_RNN // c
    wg = _rg_gate_weights(wa, wx)
    bg = jnp.stack([ba[0], bx[0], ba[1], bx[1]], axis=0).reshape(4, steps, c)
    bg = 0.5 * bg.transpose(1, 0, 2).reshape(steps, 1, 4 * c)
    y, fin = pl.pallas_call(
        functools.partial(_rglru_kernel, nseq, seq_len),
        grid=(steps,),
        in_specs=[pl.BlockSpec((t, c), lambda j: (0, j)),
                  pl.BlockSpec((t, c), lambda j: (0, steps + j)),
                  pl.BlockSpec((conv_w.shape[0], c), lambda j: (0, j)),
                  pl.BlockSpec((1, c), lambda j: (0, j)),
                  pl.BlockSpec((1, c, 4 * c), lambda j: (j, 0, 0)),
                  pl.BlockSpec((None, 1, 4 * c), lambda j: (j, 0, 0)),
                  pl.BlockSpec((2, c), lambda j: (0, j)),
                  pl.BlockSpec((2, nseq, c), lambda j: (0, 0, j))],
        out_specs=[pl.BlockSpec((t, c), lambda j: (0, j)),
                   pl.BlockSpec((2, nseq, c), lambda j: (0, 0, j))],
        out_shape=[jax.ShapeDtypeStruct((t, D_RNN), BF16),
                   jax.ShapeDtypeStruct((2, nseq, D_RNN), F32)],
        scratch_shapes=[pltpu.VMEM((t + 4 * nseq, c), F32)] + [pltpu.VMEM((t, c), F32)] * 4,
        compiler_params=_cparams("arbitrary"),
        name="rglru_block",
    )(u, u, conv_w, conv_b.reshape(1, -1), wg, bg, lam, h0)
    return y, fin


def _route_record(h, w, rb):
    h_hi = h.astype(BF16)
    h_lo = (h - h_hi.astype(F32)).astype(BF16)
    w_hi = w.astype(BF16)
    w_lo = (w - w_hi.astype(F32)).astype(BF16)
    logits = _dot_nt(w_hi, h_hi) + (_dot_nt(w_lo, h_hi) + _dot_nt(w_hi, h_lo))
    scores = jax.nn.sigmoid(logits)
    sel = scores + rb
    row = [sel[e:e + 1, :] for e in range(N_EXPERTS)]
    gs = []
    for g in range(N_GROUPS):
        r = row[g * EXPERTS_PER_GROUP:(g + 1) * EXPERTS_PER_GROUP]
        best_pair = None
        for i in range(EXPERTS_PER_GROUP):
            for j in range(i + 1, EXPERTS_PER_GROUP):
                s = r[i] + r[j]
                best_pair = s if best_pair is None else jnp.maximum(best_pair, s)
        gs.append(best_pair)
    best = jnp.zeros_like(gs[0], dtype=jnp.int32)
    top = gs[0]
    for g in range(1, N_GROUPS):
        better = gs[g] > top
        best = jnp.where(better, g, best)
        top = jnp.where(better, gs[g], top)
    picked = []
    for e in range(N_EXPERTS):
        g = e // EXPERTS_PER_GROUP
        rank = jnp.zeros_like(best)
        for o in range(g * EXPERTS_PER_GROUP, (g + 1) * EXPERTS_PER_GROUP):
            if o == e:
                continue
            ahead = (row[o] > row[e]) | ((row[o] == row[e]) & (o < e))
            rank = rank + ahead.astype(jnp.int32)
        picked.append((best == g) & (rank < 2))
    den = jnp.zeros_like(gs[0])
    for e in range(N_EXPERTS):
        den = den + jnp.where(picked[e], scores[e:e + 1, :], 0.0)
    gate = [jnp.where(picked[e], scores[e:e + 1, :] / den, 0.0) for e in range(N_EXPERTS)]
    cls = jnp.zeros_like(den)
    w_a = jnp.zeros_like(den)
    w_b = jnp.zeros_like(den)
    for g in range(N_GROUPS):
        for pi, (a, b) in enumerate(MOE_PAIRS):
            ea, eb = g * EXPERTS_PER_GROUP + a, g * EXPERTS_PER_GROUP + b
            both = picked[ea] & picked[eb]
            cls = jnp.where(both, float(g * len(MOE_PAIRS) + pi), cls)
            w_a = jnp.where(both, gate[ea], w_a)
            w_b = jnp.where(both, gate[eb], w_b)
    return jnp.concatenate([cls, w_a, w_b, jnp.zeros((ROUTE_ROWS - 3, cls.shape[1]), F32)], axis=0)


MOE_PAIRS = ((0, 1), (0, 2), (0, 3), (1, 3), (1, 2), (2, 3))
N_CLS = N_GROUPS * len(MOE_PAIRS)
CLS_PAD = 32
ROUTE_ROWS = 8
MOE_TS = 256
MOE_TM = 256
MOE_STEP_TILES = 4
SLOT_BLK = 512
ROW_W = D_MODEL + LANES
ROW_DMA_PRIORITY = 1


def _slots_kernel(n_blk, route_ref, slot_ref, off_ref, cnt_ref):
    cid = lax.broadcasted_iota(jnp.int32, (CLS_PAD, SLOT_BLK), 0).astype(F32)

    def members(j):
        cls = route_ref[0:1, pl.ds(pl.multiple_of(j * SLOT_BLK, SLOT_BLK), SLOT_BLK)]
        return (cid == cls).astype(F32)

    def count(j, cnt):
        return cnt + jnp.sum(members(j), axis=1, keepdims=True)

    cnt = lax.fori_loop(0, n_blk, count, jnp.zeros((CLS_PAD, 1), F32))
    cnt = jnp.broadcast_to(cnt, (CLS_PAD, LANES))
    padded = jnp.ceil(cnt * (1.0 / MOE_TS)) * MOE_TS
    r = lax.broadcasted_iota(jnp.int32, (CLS_PAD, CLS_PAD), 0)
    c = lax.broadcasted_iota(jnp.int32, (CLS_PAD, CLS_PAD), 1)
    off = jnp.dot((c < r).astype(F32), padded, precision=lax.Precision.HIGHEST, preferred_element_type=F32)
    off_ref[...] = off
    cnt_ref[...] = cnt
    tr = lax.broadcasted_iota(jnp.int32, (SLOT_BLK, SLOT_BLK), 0)
    tc = lax.broadcasted_iota(jnp.int32, (SLOT_BLK, SLOT_BLK), 1)
    earlier = (tr < tc).astype(BF16)

    def assign(j, base):
        member = members(j)
        rank = _dot(member.astype(BF16), earlier)
        slot = jnp.sum(member * (rank + base), axis=0, keepdims=True)
        slot_ref[0:1, pl.ds(pl.multiple_of(j * SLOT_BLK, SLOT_BLK), SLOT_BLK)] = slot.astype(jnp.int32)
        return base + jnp.sum(member, axis=1, keepdims=True)

    lax.fori_loop(0, n_blk, assign, off[:, 0:1])


def moe_slots(route):
    t = route.shape[1]
    stat = jax.ShapeDtypeStruct((CLS_PAD, LANES), F32)
    return pl.pallas_call(
        functools.partial(_slots_kernel, t // SLOT_BLK),
        out_shape=[jax.ShapeDtypeStruct((1, t), jnp.int32), stat, stat],
        compiler_params=pltpu.CompilerParams(vmem_limit_bytes=VMEM_LIMIT),
        name="moe_slots",
    )(route)


def _tile_maps(off, cnt, n_tiles):
    off = off[:N_CLS, 0].astype(jnp.int32)
    cnt = cnt[:N_CLS, 0].astype(jnp.int32)
    ends = off + ((cnt + MOE_TS - 1) // MOE_TS) * MOE_TS
    n_used = ends[-1] // MOE_TS
    k = jnp.arange(n_tiles, dtype=jnp.int32)
    tix = jnp.minimum(k, n_used - 1)
    cls = jnp.sum((tix[:, None] * MOE_TS >= ends[None, :]).astype(jnp.int32), axis=1)
    pair = jnp.asarray(MOE_PAIRS, jnp.int32)
    grp = (cls // len(MOE_PAIRS)) * EXPERTS_PER_GROUP
    ea = grp + pair[cls % len(MOE_PAIRS), 0]
    eb = grp + pair[cls % len(MOE_PAIRS), 1]
    n = jnp.int32(n_tiles)

    def slot_plan(e):
        chg = jnp.concatenate([jnp.ones((1,), jnp.int32), (e[1:] != e[:-1]).astype(jnp.int32)])
        at = jnp.where(chg == 1, k, n)
        nxt_at = jnp.concatenate([lax.cummin(at[::-1])[::-1][1:], n.reshape(1)])
        more = (nxt_at < n).astype(jnp.int32)
        nxt = e[jnp.minimum(nxt_at, n - 1)]
        par = (jnp.cumsum(chg) - 1) % 2
        return chg, nxt, more, par.astype(jnp.int32)

    plan_a, plan_b = slot_plan(ea), slot_plan(eb)
    chg, nxt, more, par = (jnp.stack([pa, pb]) for pa, pb in zip(plan_a, plan_b))
    return ea, eb, chg, nxt, more, par, n_used.reshape(1)


def _dispatch_kernel(n_steps, slots_ref, x_ref, g_ref, sc_ref, sh_ref, rt_ref, hs_in, hs_out, rowbuf, sem):
    del hs_in
    i = pl.program_id(0)
    cur = i % 2

    def wait_rows(s):
        pltpu.make_async_copy(rowbuf.at[s], rowbuf.at[s], sem.at[s]).wait()

    @pl.when(i >= 2)
    def _():
        wait_rows(cur)

    rowbuf[cur, :, 0:D_MODEL] = _normmod(x_ref[...], g_ref[...], sc_ref[0], sh_ref[0])
    rowbuf[cur, :, D_MODEL:ROW_W] = jnp.concatenate(
        [rt_ref[...], jnp.zeros((MOE_TM, LANES - ROUTE_ROWS), F32)], axis=1)

    for s in range(2):
        @pl.when(cur == s)
        def _():
            for r in range(MOE_TM):
                dst = slots_ref[i * MOE_TM + r]
                pltpu.make_async_copy(rowbuf.at[s, r], hs_out.at[dst], sem.at[s]).start(priority=ROW_DMA_PRIORITY)

    @pl.when(i == n_steps - 1)
    def _():
        wait_rows(cur)
        if n_steps >= 2:
            wait_rows(1 - cur)


def moe_dispatch(x, g, mod, route_t, slots, hs, seq_len, per_seq):
    t, d = x.shape
    n_steps = t // MOE_TM
    grid_spec = pltpu.PrefetchScalarGridSpec(
        num_scalar_prefetch=1,
        grid=(n_steps,),
        in_specs=[pl.BlockSpec((MOE_TM, d), lambda i, s: (i, 0)),
                  pl.BlockSpec((1, d), lambda i, s: (0, 0)),
                  _mod_spec(4, MOE_TM, seq_len, per_seq),
                  _mod_spec(3, MOE_TM, seq_len, per_seq),
                  pl.BlockSpec((MOE_TM, ROUTE_ROWS), lambda i, s: (i, 0)),
                  pl.BlockSpec(memory_space=pl.ANY)],
        out_specs=pl.BlockSpec(memory_space=pl.ANY),
        scratch_shapes=[pltpu.VMEM((2, MOE_TM, ROW_W), F32), pltpu.SemaphoreType.DMA((2,))],
    )
    return pl.pallas_call(
        functools.partial(_dispatch_kernel, n_steps),
        grid_spec=grid_spec,
        out_shape=jax.ShapeDtypeStruct(hs.shape, F32),
        input_output_aliases={6: 0},
        compiler_params=_cparams("arbitrary"),
        name="moe_dispatch",
    )(slots, x, g.reshape(1, d), mod, mod, route_t, hs)


def _experts_kernel(layer, ea_ref, eb_ref, chg_ref, nxt_ref, more_ref, par_ref, nused_ref,
                    hs_ref, wg_hbm, wu_hbm, wd_hbm, ys_ref, fg, fu, fd, bg, bu, bd, sem):
    def weight_copies(slot, expert, par):
        return [pltpu.make_async_copy(src.at[layer, expert], dst.at[slot, par], sem.at[slot, par])
                for src, dst in ((wg_hbm, fg), (wu_hbm, fu), (wd_hbm, fd))]

    def tile(k, rows):
        @pl.when(k < nused_ref[0])
        def _():
            for slot, e_ref in enumerate((ea_ref, eb_ref)):
                @pl.when(chg_ref[slot, k] == 1)
                def _():
                    par = par_ref[slot, k]

                    @pl.when(k == 0)
                    def _():
                        for cp in weight_copies(slot, e_ref[0], par):
                            cp.start()

                    for cp in weight_copies(slot, e_ref[k], par):
                        cp.wait()
                    bg[slot] = fg[slot, par].astype(BF16)
                    bu[slot] = fu[slot, par].astype(BF16)
                    bd[slot] = fd[slot, par].astype(BF16)

                    @pl.when(more_ref[slot, k] == 1)
                    def _():
                        for cp in weight_copies(slot, nxt_ref[slot, k], 1 - par):
                            cp.start()

            h = hs_ref[rows, 0:D_MODEL].astype(BF16)

            def ffn(slot):
                hid = _dot(h, bg[slot])
                up = _dot(h, bu[slot])
                w = hs_ref[rows, D_MODEL + 1 + slot:D_MODEL + 2 + slot]
                act = (hid * _sigmoid(hid)) * up * w
                return _dot(act.astype(BF16), bd[slot])

            ys_ref[rows, :] = ffn(0) + ffn(1)

        @pl.when(k >= nused_ref[0])
        def _():
            ys_ref[rows, :] = jnp.zeros((MOE_TS, ys_ref.shape[1]), ys_ref.dtype)

    for j in range(MOE_STEP_TILES):
        tile(pl.program_id(0) * MOE_STEP_TILES + j, slice(j * MOE_TS, (j + 1) * MOE_TS))


def moe_experts(hs, maps, layer, w_gate, w_up, w_down):
    n_tiles = hs.shape[0] // MOE_TS
    d = D_MODEL

    grid_spec = pltpu.PrefetchScalarGridSpec(
        num_scalar_prefetch=7,
        grid=(n_tiles // MOE_STEP_TILES,),
        in_specs=[pl.BlockSpec((MOE_STEP_TILES * MOE_TS, ROW_W), lambda k, *_: (k, 0)),
                  pl.BlockSpec(memory_space=pl.ANY), pl.BlockSpec(memory_space=pl.ANY),
                  pl.BlockSpec(memory_space=pl.ANY)],
        out_specs=pl.BlockSpec((MOE_STEP_TILES * MOE_TS, d), lambda k, *_: (k, 0)),
        scratch_shapes=[pltpu.VMEM((2, 2, d, D_EXPERT), F32), pltpu.VMEM((2, 2, d, D_EXPERT), F32),
                        pltpu.VMEM((2, 2, D_EXPERT, d), F32),
                        pltpu.VMEM((2, d, D_EXPERT), BF16), pltpu.VMEM((2, d, D_EXPERT), BF16),
                        pltpu.VMEM((2, D_EXPERT, d), BF16),
                        pltpu.SemaphoreType.DMA((2, 2))],
    )
    return pl.pallas_call(
        functools.partial(_experts_kernel, layer),
        grid_spec=grid_spec,
        out_shape=jax.ShapeDtypeStruct((hs.shape[0], d), F32),
        compiler_params=_cparams("arbitrary"),
        name="moe_experts",
    )(*maps, hs, w_gate, w_up, w_down)


def _combine_kernel(final, n_steps, slots_ref, x_ref, g2_ref, fg_ref, ys_hbm, o_ref, gbuf, sem):
    i = pl.program_id(0)
    cur = i % 2

    def issue_tile(tile, s):
        for r in range(MOE_TM):
            src = slots_ref[tile * MOE_TM + r]
            pltpu.make_async_copy(ys_hbm.at[src], gbuf.at[s, r], sem.at[s]).start(priority=ROW_DMA_PRIORITY)

    @pl.when(i == 0)
    def _():
        issue_tile(0, 0)

    for s in range(2):
        @pl.when((i + 1 < n_steps) & (1 - cur == s))
        def _():
            issue_tile(i + 1, s)

    pltpu.make_async_copy(gbuf.at[cur], gbuf.at[cur], sem.at[cur]).wait()
    y = x_ref[...] + g2_ref[0] * gbuf[cur]
    if final:
        ms = jnp.mean(y * y, axis=-1, keepdims=True)
        y = y * lax.rsqrt(ms + EPS) * fg_ref[...]
    o_ref[...] = y


def moe_combine(x, mod, slots, ys, final_g, final, seq_len, per_seq):
    t, d = x.shape
    n_steps = t // MOE_TM
    grid_spec = pltpu.PrefetchScalarGridSpec(
        num_scalar_prefetch=1,
        grid=(n_steps,),
        in_specs=[pl.BlockSpec((MOE_TM, d), lambda i, s: (i, 0)),
                  _mod_spec(5, MOE_TM, seq_len, per_seq),
                  pl.BlockSpec((1, d), lambda i, s: (0, 0)),
                  pl.BlockSpec(memory_space=pl.ANY)],
        out_specs=pl.BlockSpec((MOE_TM, d), lambda i, s: (i, 0)),
        scratch_shapes=[pltpu.VMEM((2, MOE_TM, d), F32), pltpu.SemaphoreType.DMA((2,))],
    )
    return pl.pallas_call(
        functools.partial(_combine_kernel, final, n_steps),
        grid_spec=grid_spec,
        out_shape=jax.ShapeDtypeStruct((t, d), F32),
        compiler_params=_cparams("arbitrary"),
        name="moe_combine",
    )(slots, x, mod, final_g.reshape(1, d), ys)


def _combine_nm_tm_kernel(nseq, steps, seq_len, n_steps, slots_ref, x_ref, g2_ref, ys_hbm, g_ref, sc_ref,
                          sh_ref, w_ref, x2_ref, u_ref, gbuf, sem):
    i = pl.program_id(0)
    cur = i % 2

    def issue_tile(tile, s):
        for sq in range(nseq):
            for t in range(steps):
                src = slots_ref[sq * seq_len + tile * steps + t]
                pltpu.make_async_copy(ys_hbm.at[src], gbuf.at[s, sq * steps + t], sem.at[s]).start(
                    priority=ROW_DMA_PRIORITY)

    @pl.when(i == 0)
    def _():
        issue_tile(0, 0)

    for s in range(2):
        @pl.when((i + 1 < n_steps) & (1 - cur == s))
        def _():
            issue_tile(i + 1, s)

    pltpu.make_async_copy(gbuf.at[cur], gbuf.at[cur], sem.at[cur]).wait()
    d = x_ref.shape[-1]
    y = x_ref[...] + g2_ref[...] * gbuf[cur].reshape(nseq, steps, d)
    x2_ref[...] = y
    h = _normmod(y, g_ref[...], sc_ref[...], sh_ref[...])
    h = h.reshape(nseq * steps, d).astype(BF16)
    h = _dot(_row_permutation(nseq, steps, True), h).astype(BF16)
    u_ref[...] = _dot(h, w_ref[...])


def combine_nm_matmul_tm(x, mod_prev, slots, ys, g, mod, w, nseq, seq_len, per_seq):
    t, d = x.shape
    n = w.shape[1]
    steps = TM_ROWS // nseq
    n_steps = seq_len // steps
    g2 = _group_mod(mod_prev, 5, nseq, per_seq)
    sc = _group_mod(mod, 1, nseq, per_seq)
    sh = _group_mod(mod, 0, nseq, per_seq)
    mod_spec = pl.BlockSpec(sc.shape, lambda i, s: (0, 0, 0))
    x_spec = pl.BlockSpec((nseq, steps, d), lambda i, s: (0, i, 0))
    grid_spec = pltpu.PrefetchScalarGridSpec(
        num_scalar_prefetch=1,
        grid=(n_steps,),
        in_specs=[x_spec, mod_spec, pl.BlockSpec(memory_space=pl.ANY),
                  pl.BlockSpec((1, d), lambda i, s: (0, 0)), mod_spec, mod_spec,
                  pl.BlockSpec((d, n), lambda i, s: (0, 0))],
        out_specs=[x_spec, pl.BlockSpec((TM_ROWS, n), lambda i, s: (i, 0))],
        scratch_shapes=[pltpu.VMEM((2, TM_ROWS, d), F32), pltpu.SemaphoreType.DMA((2,))],
    )
    x2, u = pl.pallas_call(
        functools.partial(_combine_nm_tm_kernel, nseq, steps, seq_len, n_steps),
        grid_spec=grid_spec,
        out_shape=[jax.ShapeDtypeStruct((nseq, seq_len, d), F32), jax.ShapeDtypeStruct((t, n), F32)],
        compiler_params=_cparams("arbitrary"),
        name="moe_combine_norm_mod_proj_tm",
    )(slots, x.reshape(nseq, seq_len, d), g2, ys, g.reshape(1, d), sc, sh, w)
    return x2.reshape(t, d), u


def moe_block(xs, routes, mods_l, per_seqs, seq_lens, layer, p, final, defer_combine, hs):
    g = p['norm_g'][layer, 1]
    slots, off, cnt = moe_slots(jnp.concatenate(routes, axis=1))
    t_all = slots.shape[1]
    n_tiles = -(-(t_all // MOE_TS + N_CLS) // MOE_STEP_TILES) * MOE_STEP_TILES
    maps = _tile_maps(off, cnt, n_tiles)
    if hs is None:
        hs = jnp.zeros((n_tiles * MOE_TS, ROW_W), F32)
    bounds = np.cumsum([0] + [x.shape[0] for x in xs])
    group_slots = [slots[0, bounds[i]:bounds[i + 1]] for i in range(len(xs))]
    for x, r, s, sl, ps in zip(xs, routes, group_slots, seq_lens, per_seqs):
        hs = moe_dispatch(x, g, mods_l, r.T, s, hs, sl, ps)
    ys = moe_experts(hs, maps, layer, p['moe_w_gate'], p['moe_w_up'], p['moe_w_down'])
    if defer_combine:
        return [(x, s, ys) for x, s in zip(xs, group_slots)], hs
    return [moe_combine(x, mods_l, s, ys, p['final_g'], final, sl, ps)
            for x, s, sl, ps in zip(xs, group_slots, seq_lens, per_seqs)], hs


def _mixer(x, group, l, mod, p, hy_tables, mod_prev):
    per_seq, nseq, seq_len = group['per_seq'], group['nseq'], group['seq_len']
    extras = None
    deferred = isinstance(x, tuple)
    if deferred and l % 2 == 0:
        x = moe_combine(x[0], mod_prev, x[1], x[2], p['final_g'], False, seq_len, per_seq)
        deferred = False
    if l % 2 == 0:
        e = l // 2
        u = nm_matmul(x, p['norm_g'][l, 0], mod, p['a_in_w'][e].astype(BF16), seq_len, per_seq)
        if group['ctx_k'] is None:
            attn, nk, nv = context_attention(u, nseq, seq_len)
            extras = (nk, nv)
        else:
            attn = neighbourhood_attention(u, group['ctx_k'][:, e], group['ctx_v'][:, e], p['na_rpb'][e],
                                           nseq, seq_len)
        fwd, inv = hy_tables[seq_len]
        spectrum = hyena_spectrum(seq_len, p['hy_w1'][e], p['hy_b1'][e], p['hy_w2'][e], p['hy_b2'][e],
                                  p['hy_w3'][e], p['hy_freq'][e], p['hy_d'][e], fwd)
        hy = hyena_mixer(u, nseq, seq_len, p['hy_short_w'][e], p['hy_short_b'][e], spectrum, fwd, inv)
        w_out = p['a_out_w'][e].astype(BF16)
        x, route = proj_residual([attn, hy], [w_out[:D_A], w_out[D_A:]], x, mod, p['norm_g'][l, 1],
                                 p['router_w'], p['router_b'], seq_len, per_seq)
    else:
        o = l // 2
        w_in = p['c_in_w'][o].astype(BF16)
        if deferred:
            x, u = combine_nm_matmul_tm(x[0], mod_prev, x[1], x[2], p['norm_g'][l, 0], mod, w_in,
                                        nseq, seq_len, per_seq)
        else:
            u = nm_matmul_tm(x, p['norm_g'][l, 0], mod, w_in, nseq, seq_len, per_seq)
        y, extras = rglru_block(u, nseq, seq_len, p['rg_conv_w'][o], p['rg_conv_b'][o], p['rg_wa'][o],
                                p['rg_ba'][o], p['rg_wx'][o], p['rg_bx'][o], p['rg_lam'][o], group['h0'][o])
        x, route = proj_residual_tm(y, p['c_out_w'][o].astype(BF16), x, mod, p['norm_g'][l, 1],
                                    p['router_w'], p['router_b'], nseq, seq_len, per_seq)
    return x, extras, route


def kernel(x_prompt, x_sample, cache_k, cache_v, state_h, c, c_ctx, norm_g, ada_w, ada_b, final_g, a_in_w, a_out_w, na_rpb, hy_short_w, hy_short_b, hy_w1, hy_b1, hy_w2, hy_b2, hy_w3, hy_freq, hy_d, c_in_w, c_out_w, rg_conv_w, rg_conv_b, rg_wa, rg_ba, rg_wx, rg_bx, rg_lam, router_w, router_b, moe_w_gate, moe_w_up, moe_w_down):
    p = dict(norm_g=norm_g, final_g=final_g, a_in_w=a_in_w, a_out_w=a_out_w, na_rpb=na_rpb,
             hy_short_w=hy_short_w, hy_short_b=hy_short_b, hy_w1=hy_w1, hy_b1=hy_b1, hy_w2=hy_w2,
             hy_b2=hy_b2, hy_w3=hy_w3, hy_freq=hy_freq, hy_d=hy_d, c_in_w=c_in_w, c_out_w=c_out_w,
             rg_conv_w=rg_conv_w, rg_conv_b=rg_conv_b, rg_wa=rg_wa, rg_ba=rg_ba, rg_wx=rg_wx, rg_bx=rg_bx,
             rg_lam=rg_lam, router_w=router_w, router_b=router_b, moe_w_gate=moe_w_gate,
             moe_w_up=moe_w_up, moe_w_down=moe_w_down)
    batch, seq, d = x_prompt.shape
    dec_batch, dec_seq, _ = x_sample.shape
    n_odd = DEPTH // 2
    assert 1 + dec_batch <= MOD_ROWS

    cond = jnp.concatenate([c_ctx[None, :], c, jnp.zeros((MOD_ROWS - 1 - dec_batch, d), F32)], axis=0)
    m = modulation(cond, ada_w, ada_b)
    mods = [m[l].reshape(MOD_ROWS * N_MOD, 1, d) for l in range(DEPTH)]

    tables = {}
    for sl in (seq, dec_seq):
        fwd, inv = _dft_tables(sl)
        tables[sl] = (jnp.asarray(fwd).astype(BF16), jnp.asarray(inv).astype(BF16))

    groups = [
        dict(per_seq=False, nseq=batch, seq_len=seq, ctx_k=None, ctx_v=None,
             h0=[jnp.zeros((2, batch, D_RNN), F32)] * n_odd),
        dict(per_seq=True, nseq=dec_batch, seq_len=dec_seq, ctx_k=cache_k, ctx_v=cache_v,
             h0=[state_h[:, o].transpose(1, 0, 2) for o in range(n_odd)]),
    ]
    xs = [x_prompt.reshape(batch * seq, d), x_sample.reshape(dec_batch * dec_seq, d)]
    k_list, v_list, h_list = [], [], []
    hs = None
    for l in range(DEPTH):
        mixed = [_mixer(x, grp, l, mods[l], p, tables, mods[l - 1] if l else None)
                 for x, grp in zip(xs, groups)]
        if l % 2 == 0:
            k_list.append(mixed[0][1][0])
            v_list.append(mixed[0][1][1])
        else:
            h_list.append(mixed[0][1].transpose(1, 0, 2))
        xs, hs = moe_block([mx[0] for mx in mixed], [mx[2] for mx in mixed], mods[l],
                           [grp['per_seq'] for grp in groups], [grp['seq_len'] for grp in groups],
                           l, p, l == DEPTH - 1, l < DEPTH - 1, hs)
    new_k = jnp.stack(k_list, axis=1)
    new_v = jnp.stack(v_list, axis=1)
    new_h = jnp.stack(h_list, axis=1)
    return (xs[0].reshape(batch, seq, d), xs[1].reshape(dec_batch, dec_seq, d), new_k, new_v, new_h)
```

```python
import functools
import math

import numpy as np
import jax
import jax.numpy as jnp
from jax import lax
from jax.experimental import pallas as pl
from jax.experimental.pallas import tpu as pltpu
from jax.experimental.pallas import tpu_sc as plsc

F32 = jnp.float32
BF16 = jnp.bfloat16

D_MODEL = 1024
DEPTH = 2
GRID_W = 64
EPS = 1e-6
NEG_INF = -1e30
NA_HEADS = 8
HEAD_DIM = 64
D_A = NA_HEADS * HEAD_DIM
WIN_ROWS = 8
WIN_COLS = 16
D_B = D_MODEL - D_A
HY_ORDER = 2
HY_EMB = 33
HY_BANDS = (HY_EMB - 1) // 2
HY_FFN = 64
HY_DECAY_TARGET = 1e-2
HY_FAST_PCT = 0.3
HY_SLOW_PCT = 1.5
D_RNN = D_MODEL
RG_BLOCK = 64
RG_C = 8.0
N_EXPERTS = 16
N_GROUPS = 4
EXPERTS_PER_GROUP = N_EXPERTS // N_GROUPS
D_EXPERT = 512

LANES = 128
VMEM_LIMIT = 56 * 1024 * 1024
N_MOD = 6
MOD_ROWS = 16


def _cparams(*sem):
    return pltpu.CompilerParams(dimension_semantics=sem, vmem_limit_bytes=VMEM_LIMIT)


def _dot(a, b):
    return jnp.dot(a, b, preferred_element_type=F32)


def _dot_nt(a, b):
    return lax.dot_general(a, b, (((1,), (1,)), ((), ())), preferred_element_type=F32)


def _sigmoid(x):
    return 0.5 * jnp.tanh(0.5 * x) + 0.5


def _normmod(x, g, sc, sh):
    ms = jnp.mean(x * x, axis=-1, keepdims=True)
    return (x * lax.rsqrt(ms + EPS) * g) * (1.0 + sc) + sh


def _mod_spec(chunk, tm, seq_len, per_seq):
    if per_seq:
        return pl.BlockSpec((1, 1, D_MODEL), lambda i, *_: ((1 + (i * tm) // seq_len) * N_MOD + chunk, 0, 0))
    return pl.BlockSpec((1, 1, D_MODEL), lambda i, *_: (chunk, 0, 0))


def _mod_kernel(c_ref, w_ref, b_ref, o_ref):
    s = c_ref[...]
    s = s * jax.nn.sigmoid(s)
    o_ref[0] = _dot(s.astype(BF16), w_ref[0].astype(BF16)) + b_ref[0]


def modulation(cond, ada_w, ada_b):
    n = ada_w.shape[-1]
    tn = n // 4
    assert tn % LANES == 0
    return pl.pallas_call(
        _mod_kernel,
        grid=(DEPTH, n // tn),
        in_specs=[pl.BlockSpec((MOD_ROWS, D_MODEL), lambda l, j: (0, 0)),
                  pl.BlockSpec((1, D_MODEL, tn), lambda l, j: (l, 0, j)),
                  pl.BlockSpec((1, 1, tn), lambda l, j: (l, 0, j))],
        out_specs=pl.BlockSpec((1, MOD_ROWS, tn), lambda l, j: (l, 0, j)),
        out_shape=jax.ShapeDtypeStruct((DEPTH, MOD_ROWS, n), F32),
        compiler_params=_cparams("arbitrary", "arbitrary"),
        name="modulation",
    )(cond, ada_w, ada_b.reshape(DEPTH, 1, n))


def _nm_matmul_kernel(x_ref, g_ref, sc_ref, sh_ref, w_ref, o_ref):
    h = _normmod(x_ref[...], g_ref[...], sc_ref[0], sh_ref[0])
    o_ref[...] = _dot(h.astype(BF16), w_ref[...])


def nm_matmul(x, g, mod, w, seq_len, per_seq, tm=512):
    t, d = x.shape
    n = w.shape[1]
    return pl.pallas_call(
        _nm_matmul_kernel,
        grid=(t // tm,),
        in_specs=[pl.BlockSpec((tm, d), lambda i: (i, 0)),
                  pl.BlockSpec((1, d), lambda i: (0, 0)),
                  _mod_spec(1, tm, seq_len, per_seq),
                  _mod_spec(0, tm, seq_len, per_seq),
                  pl.BlockSpec((d, n), lambda i: (0, 0))],
        out_specs=pl.BlockSpec((tm, n), lambda i: (i, 0)),
        out_shape=jax.ShapeDtypeStruct((t, n), F32),
        compiler_params=_cparams("arbitrary"),
        name="norm_mod_proj",
    )(x, g.reshape(1, d), mod, mod, w)


def _proj_res_kernel(n_act, *refs):
    acts = refs[:n_act]
    ws = refs[n_act:2 * n_act]
    x_ref, g_ref, ng_ref, sc2_ref, sh2_ref, rw_ref, rb_ref, o_ref, route_ref = refs[2 * n_act:]
    acc = _dot(acts[0][...].astype(BF16), ws[0][...])
    for a, w in zip(acts[1:], ws[1:]):
        acc += _dot(a[...].astype(BF16), w[...])
    x = x_ref[...] + g_ref[0] * acc
    o_ref[...] = x
    route_ref[...] = _route_record(_normmod(x, ng_ref[...], sc2_ref[0], sh2_ref[0]), rw_ref[...], rb_ref[...])


def proj_residual(acts, ws, x, mod, norm2_g, router_w, router_b, seq_len, per_seq, tm=512):
    t, d = x.shape
    in_specs = [pl.BlockSpec((tm, a.shape[1]), lambda i: (i, 0)) for a in acts]
    in_specs += [pl.BlockSpec(w.shape, lambda i: (0, 0)) for w in ws]
    in_specs += [pl.BlockSpec((tm, d), lambda i: (i, 0)), _mod_spec(2, tm, seq_len, per_seq),
                 pl.BlockSpec((1, d), lambda i: (0, 0)),
                 _mod_spec(4, tm, seq_len, per_seq), _mod_spec(3, tm, seq_len, per_seq),
                 pl.BlockSpec((N_EXPERTS, d), lambda i: (0, 0)),
                 pl.BlockSpec((N_EXPERTS, 1), lambda i: (0, 0))]
    return pl.pallas_call(
        functools.partial(_proj_res_kernel, len(acts)),
        grid=(t // tm,),
        in_specs=in_specs,
        out_specs=[pl.BlockSpec((tm, d), lambda i: (i, 0)), pl.BlockSpec((ROUTE_ROWS, tm), lambda i: (0, i))],
        out_shape=[jax.ShapeDtypeStruct((t, d), F32), jax.ShapeDtypeStruct((ROUTE_ROWS, t), F32)],
        compiler_params=_cparams("arbitrary"),
        name="proj_residual_route",
    )(*acts, *ws, x, mod, norm2_g.reshape(1, d), mod, mod, router_w.T, router_b.reshape(N_EXPERTS, 1))


def _row_permutation(nseq, steps, to_time_major):
    n = nseq * steps
    i = lax.broadcasted_iota(jnp.int32, (n, n), 0)
    j = lax.broadcasted_iota(jnp.int32, (n, n), 1)
    if to_time_major:
        src = (i % nseq) * steps + i // nseq
    else:
        src = (i % steps) * nseq + i // steps
    return (j == src).astype(BF16)


def _nm_matmul_tm_kernel(nseq, steps, x_ref, g_ref, sc_ref, sh_ref, w_ref, o_ref):
    h = _normmod(x_ref[...], g_ref[...], sc_ref[...], sh_ref[...])
    h = h.reshape(nseq * steps, h.shape[-1]).astype(BF16)
    h = _dot(_row_permutation(nseq, steps, True), h).astype(BF16)
    o_ref[...] = _dot(h, w_ref[...])


def _group_mod(mod, chunk, nseq, per_seq):
    rows = mod.reshape(MOD_ROWS, N_MOD, 1, D_MODEL)
    return rows[1:1 + nseq, chunk] if per_seq else rows[0:1, chunk]


def nm_matmul_tm(x, g, mod, w, nseq, seq_len, per_seq):
    t, d = x.shape
    n = w.shape[1]
    steps = TM_ROWS // nseq
    sc = _group_mod(mod, 1, nseq, per_seq)
    sh = _group_mod(mod, 0, nseq, per_seq)
    mod_spec = pl.BlockSpec(sc.shape, lambda i: (0, 0, 0))
    return pl.pallas_call(
        functools.partial(_nm_matmul_tm_kernel, nseq, steps),
        grid=(seq_len // steps,),
        in_specs=[pl.BlockSpec((nseq, steps, d), lambda i: (0, i, 0)),
                  pl.BlockSpec((1, d), lambda i: (0, 0)),
                  mod_spec, mod_spec,
                  pl.BlockSpec((d, n), lambda i: (0, 0))],
        out_specs=pl.BlockSpec((TM_ROWS, n), lambda i: (i, 0)),
        out_shape=jax.ShapeDtypeStruct((t, n), F32),
        compiler_params=_cparams("arbitrary"),
        name="norm_mod_proj_tm",
    )(x.reshape(nseq, seq_len, d), g.reshape(1, d), sc, sh, w)


def _proj_res_tm_kernel(nseq, steps, y_ref, w_ref, x_ref, g_ref, ng_ref, sc2_ref, sh2_ref, rw_ref, rb_ref,
                        o_ref, route_ref):
    y = _dot(_row_permutation(nseq, steps, False), y_ref[...].astype(BF16)).astype(BF16)
    acc = _dot(y, w_ref[...])
    x = x_ref[...] + g_ref[...] * acc.reshape(nseq, steps, acc.shape[-1])
    o_ref[...] = x
    h = _normmod(x, ng_ref[...], sc2_ref[...], sh2_ref[...]).reshape(nseq * steps, x.shape[-1])
    route_ref[0] = _route_record(h, rw_ref[...], rb_ref[...])


def proj_residual_tm(y, w, x, mod, norm2_g, router_w, router_b, nseq, seq_len, per_seq):
    t, d = x.shape
    steps = TM_ROWS // nseq
    n_steps = seq_len // steps
    g1 = _group_mod(mod, 2, nseq, per_seq)
    sc2 = _group_mod(mod, 4, nseq, per_seq)
    sh2 = _group_mod(mod, 3, nseq, per_seq)
    mod_spec = pl.BlockSpec(g1.shape, lambda i: (0, 0, 0))
    out, route = pl.pallas_call(
        functools.partial(_proj_res_tm_kernel, nseq, steps),
        grid=(n_steps,),
        in_specs=[pl.BlockSpec((TM_ROWS, y.shape[1]), lambda i: (i, 0)),
                  pl.BlockSpec(w.shape, lambda i: (0, 0)),
                  pl.BlockSpec((nseq, steps, d), lambda i: (0, i, 0)),
                  mod_spec,
                  pl.BlockSpec((1, d), lambda i: (0, 0)),
                  mod_spec, mod_spec,
                  pl.BlockSpec((N_EXPERTS, d), lambda i: (0, 0)),
                  pl.BlockSpec((N_EXPERTS, 1), lambda i: (0, 0))],
        out_specs=[pl.BlockSpec((nseq, steps, d), lambda i: (0, i, 0)),
                   pl.BlockSpec((1, ROUTE_ROWS, TM_ROWS), lambda i: (i, 0, 0))],
        out_shape=[jax.ShapeDtypeStruct((nseq, seq_len, d), F32),
                   jax.ShapeDtypeStruct((n_steps, ROUTE_ROWS, TM_ROWS), F32)],
        compiler_params=_cparams("arbitrary"),
        name="proj_residual_tm_route",
    )(y, w, x.reshape(nseq, seq_len, d), g1, norm2_g.reshape(1, d), sc2, sh2, router_w.T,
      router_b.reshape(N_EXPERTS, 1))
    route = route.reshape(n_steps, ROUTE_ROWS, nseq, steps).transpose(1, 2, 0, 3).reshape(ROUTE_ROWS, t)
    return out.reshape(t, d), route


CTX_STEP_ROWS = 1024


def _ctx_attn_kernel(seq_len, n_seq, q_ref, k_ref, v_ref, o_ref, nk_ref, nv_ref):
    scale = HEAD_DIM ** -0.5
    per_tile = LANES // HEAD_DIM
    lane = lax.broadcasted_iota(jnp.int32, (1, LANES), 1)
    for b in range(n_seq):
        rows = slice(b * seq_len, (b + 1) * seq_len)
        for hp in range(NA_HEADS // per_tile):
            sl = slice(hp * LANES, (hp + 1) * LANES)
            q, k, v = q_ref[rows, sl] * scale, k_ref[rows, sl], v_ref[rows, sl]
            kb, vb = k.astype(BF16), v.astype(BF16)
            out = None
            for j in range(per_tile):
                h = hp * per_tile + j
                nk_ref[b, h] = k[:, j * HEAD_DIM:(j + 1) * HEAD_DIM]
                nv_ref[b, h] = v[:, j * HEAD_DIM:(j + 1) * HEAD_DIM]
                mine = lane // HEAD_DIM == j
                s = _dot_nt(jnp.where(mine, q, 0.0).astype(BF16), kb)
                p = jnp.exp(s - jnp.max(s, axis=-1, keepdims=True))
                o = _dot(p.astype(BF16), vb) / jnp.sum(p, axis=-1, keepdims=True)
                out = o if out is None else jnp.where(mine, o, out)
            o_ref[rows, sl] = out.astype(o_ref.dtype)


def context_attention(u, nseq, seq_len):
    t = u.shape[0]
    per_step = max(1, CTX_STEP_ROWS // seq_len)
    assert nseq % per_step == 0
    rows = per_step * seq_len
    kv_shape = jax.ShapeDtypeStruct((nseq, NA_HEADS, seq_len, HEAD_DIM), F32)
    kv_spec = pl.BlockSpec((per_step, NA_HEADS, seq_len, HEAD_DIM), lambda b: (b, 0, 0, 0))
    return pl.pallas_call(
        functools.partial(_ctx_attn_kernel, seq_len, per_step),
        grid=(nseq // per_step,),
        in_specs=[pl.BlockSpec((rows, D_A), lambda b: (b, 0)),
                  pl.BlockSpec((rows, D_A), lambda b: (b, 1)),
                  pl.BlockSpec((rows, D_A), lambda b: (b, 2))],
        out_specs=[pl.BlockSpec((rows, D_A), lambda b: (b, 0)), kv_spec, kv_spec],
        out_shape=[jax.ShapeDtypeStruct((t, D_A), BF16), kv_shape, kv_shape],
        compiler_params=_cparams("arbitrary"),
        name="context_attention",
    )(u, u, u)


N_DR = 2 * WIN_ROWS - 1
N_DC = 2 * WIN_COLS - 1
N_DC_PAD = 32


def _na_col_tables():
    cols = np.arange(GRID_W)
    col_start = np.clip(cols - WIN_COLS // 2, 0, GRID_W - WIN_COLS)
    col_in = (cols[None, :] >= col_start[:, None]) & (cols[None, :] < col_start[:, None] + WIN_COLS)
    dc = np.clip(cols[None, :] - cols[:, None], 1 - WIN_COLS, WIN_COLS - 1) + WIN_COLS - 1
    onehot = (dc.reshape(1, -1) == np.arange(N_DC_PAD)[:, None]).astype(np.float32)
    return onehot, col_in.reshape(1, -1).astype(np.float32)


def _na_bias_kernel(r_ref, e_ref, m_ref, o_ref):
    t = jnp.dot(r_ref[...], e_ref[...], precision=lax.Precision.HIGHEST, preferred_element_type=F32)
    o_ref[...] = jnp.where(m_ref[...] > 0.0, t, NEG_INF)


def na_bias_table(rpb):
    onehot, col_in = _na_col_tables()
    n_rows = NA_HEADS * N_DR
    assert n_rows <= LANES
    r = jnp.zeros((LANES, N_DC_PAD), F32).at[:n_rows, :N_DC].set(rpb.reshape(n_rows, N_DC).astype(F32))
    t = pl.pallas_call(
        _na_bias_kernel,
        out_shape=jax.ShapeDtypeStruct((LANES, GRID_W * GRID_W), F32),
        name="na_bias_table",
    )(r, jnp.asarray(onehot), jnp.asarray(col_in))
    t = t[:n_rows].reshape(NA_HEADS, N_DR, GRID_W, GRID_W)
    return jnp.concatenate([t[:, :-1], t[:, 1:]], axis=-1)


def _na_kernel(rows, q_ref, k_ref, v_ref, ck_ref, cv_ref, bias_ref, o_ref,
               q_s, k_s, v_s, ck_s, cv_s, s_s, p_s, den_s, o_s):
    scale = HEAD_DIM ** -0.5
    n_lat = WIN_ROWS * GRID_W
    per_tile = LANES // HEAD_DIM
    n_pairs = NA_HEADS // per_tile
    lane = lax.broadcasted_iota(jnp.int32, (1, LANES), 1)
    for hp in range(n_pairs):
        sl = slice(hp * LANES, (hp + 1) * LANES)
        q_s[hp] = (q_ref[:, sl] * scale).astype(BF16)
        k_s[hp] = k_ref[:, sl].astype(BF16)
        v_s[hp] = v_ref[:, sl].astype(BF16)
        heads = range(hp * per_tile, (hp + 1) * per_tile)
        ck_s[hp] = jnp.concatenate([ck_ref[0, h] for h in heads], axis=1).astype(BF16)
        cv_s[hp] = jnp.concatenate([cv_ref[0, h] for h in heads], axis=1).astype(BF16)

    def window(r):
        start = min(max(r - WIN_ROWS // 2, 0), rows - WIN_ROWS)
        return start, start - r + WIN_ROWS - 1

    def pair_body(hp, carry):
        for j in range(per_tile):
            h = hp * per_tile + j
            mine = lane // HEAD_DIM == j
            for r in range(rows):
                start, off = window(r)
                rs = slice(r * GRID_W, (r + 1) * GRID_W)
                q = jnp.where(mine, q_s[hp, rs, :], 0.0).astype(BF16)
                bias = jnp.concatenate([bias_ref[h, off + 2 * i] for i in range(WIN_ROWS // 2)], axis=1)
                s_s[rs, 0:n_lat] = _dot_nt(q, k_s[hp, start * GRID_W:start * GRID_W + n_lat, :]) + bias
                s_s[rs, n_lat:] = _dot_nt(q, ck_s[hp])
            for r in range(rows):
                rs = slice(r * GRID_W, (r + 1) * GRID_W)
                s = s_s[rs, :]
                p = jnp.exp(s - jnp.max(s, axis=-1, keepdims=True))
                den_s[rs, :] = jnp.sum(p, axis=-1, keepdims=True)
                p_s[rs, :] = p.astype(BF16)
            for r in range(rows):
                start, _ = window(r)
                rs = slice(r * GRID_W, (r + 1) * GRID_W)
                o = (_dot(p_s[rs, 0:n_lat], v_s[hp, start * GRID_W:start * GRID_W + n_lat, :])
                     + _dot(p_s[rs, n_lat:], cv_s[hp])) / den_s[rs, :]
                o_s[hp, rs, :] = o if j == 0 else jnp.where(mine, o, o_s[hp, rs, :])
        return carry

    lax.fori_loop(0, n_pairs, pair_body, 0)
    for hp in range(n_pairs):
        o_ref[:, hp * LANES:(hp + 1) * LANES] = o_s[hp].astype(o_ref.dtype)


def neighbourhood_attention(u, ctx_k, ctx_v, rpb, nseq, seq_len):
    t = u.shape[0]
    rows = seq_len // GRID_W
    assert rows >= WIN_ROWS and WIN_ROWS % 2 == 0
    past = ctx_k.shape[2]
    bias = na_bias_table(rpb)
    ctx_spec = pl.BlockSpec((1, NA_HEADS, past, HEAD_DIM), lambda b: (b, 0, 0, 0))
    return pl.pallas_call(
        functools.partial(_na_kernel, rows),
        grid=(nseq,),
        in_specs=[pl.BlockSpec((seq_len, D_A), lambda b: (b, 0)),
                  pl.BlockSpec((seq_len, D_A), lambda b: (b, 1)),
                  pl.BlockSpec((seq_len, D_A), lambda b: (b, 2)),
                  ctx_spec, ctx_spec,
                  pl.BlockSpec(bias.shape, lambda b: (0, 0, 0, 0))],
        out_specs=pl.BlockSpec((seq_len, D_A), lambda b: (b, 0)),
        out_shape=jax.ShapeDtypeStruct((t, D_A), BF16),
        scratch_shapes=[pltpu.VMEM((D_A // LANES, seq_len, LANES), BF16)] * 3
        + [pltpu.VMEM((D_A // LANES, past, LANES), BF16)] * 2
        + [pltpu.VMEM((seq_len, WIN_ROWS * GRID_W + past), F32),
           pltpu.VMEM((seq_len, WIN_ROWS * GRID_W + past), BF16),
           pltpu.VMEM((seq_len, 1), F32),
           pltpu.VMEM((D_A // LANES, seq_len, LANES), F32)],
        compiler_params=_cparams("arbitrary"),
        name="neighbourhood_attention",
    )(u, u, u, ctx_k, ctx_v, bias)


HY_STEP_ROWS = 1024


def _dft_tables(seq_len):
    n = 2 * seq_len
    f = np.arange(seq_len, dtype=np.int64)
    ang = (np.outer(f, f) % n).astype(np.float64) * (math.pi / seq_len)
    cos, sin = np.cos(ang), np.sin(ang)
    alt = np.where(f % 2 == 0, 1.0, -1.0)
    s_fwd = -sin
    s_fwd[0, :] = alt
    fwd = np.concatenate([cos, s_fwd], axis=0)
    wf = np.where(f == 0, 1.0, 2.0) / n
    ci = cos.T * wf[None, :]
    si = -sin.T * wf[None, :]
    si[:, 0] = alt / n
    inv = np.concatenate([ci, si], axis=1)
    return fwd.astype(np.float32), inv.astype(np.float32)


def _hyena_feats(seq_len):
    t = np.linspace(0.0, 1.0, seq_len, dtype=np.float32)[:, None]
    w = (2.0 * math.pi * np.arange(seq_len, dtype=np.float32)[:, None] / seq_len).astype(np.float32)
    f = np.linspace(1e-4, HY_BANDS - 1, HY_BANDS, dtype=np.float32)[None, :]
    z = np.concatenate([t, np.cos(f * w), -np.sin(f * w)], axis=-1).astype(np.float32)
    max_decay = math.log(HY_DECAY_TARGET) / HY_FAST_PCT
    min_decay = math.log(HY_DECAY_TARGET) / HY_SLOW_PCT
    deltas = np.abs(np.linspace(min_decay, max_decay, D_B, dtype=np.float32))[None, :]
    return z, t, deltas


def _hy_filter_kernel(seq_len, z_ref, t_ref, dl_ref, w1_ref, b1_ref, w2_ref, b2_ref, w3_ref, fr_ref,
                      d_ref, fwd_ref, g_ref):
    hp = lax.Precision.HIGHEST
    h = jnp.sin(fr_ref[0:1, :] * (jnp.dot(z_ref[...], w1_ref[...], precision=hp) + b1_ref[...]))
    h = jnp.sin(fr_ref[1:2, :] * (jnp.dot(h, w2_ref[...], precision=hp) + b2_ref[...]))
    h = jnp.dot(h, w3_ref[...], precision=hp)
    decay = jnp.exp(-t_ref[...] * dl_ref[...])
    row0 = lax.broadcasted_iota(jnp.int32, (seq_len, D_B), 0) == 0
    sums, diffs = [], []
    for n in range(HY_ORDER):
        hf = h[:, (2 * n) * D_B:(2 * n + 1) * D_B] * decay
        hb = h[:, (2 * n + 1) * D_B:(2 * n + 2) * D_B] * decay
        gp = jnp.where(row0, hf + hb + d_ref[n:n + 1, :], hf)
        gm = jnp.where(row0, 0.0, hb)
        sums.append(gp + gm)
        diffs.append(gp - gm)
    rhs = jnp.concatenate(sums + diffs, axis=1).astype(BF16)
    spec = _dot(fwd_ref[...], rhs)
    for n in range(HY_ORDER):
        a = spec[:, n * D_B:(n + 1) * D_B]
        b = spec[:, (HY_ORDER + n) * D_B:(HY_ORDER + n + 1) * D_B]
        g_ref[n, 0:seq_len, :] = a[0:seq_len]
        g_ref[n, seq_len:, :] = jnp.where(row0, a[seq_len:], b[seq_len:])


def hyena_spectrum(seq_len, w1, b1, w2, b2, w3, freq, d, fwd):
    z, t, deltas = _hyena_feats(seq_len)
    return pl.pallas_call(
        functools.partial(_hy_filter_kernel, seq_len),
        out_shape=jax.ShapeDtypeStruct((HY_ORDER, 2 * seq_len, D_B), F32),
        compiler_params=pltpu.CompilerParams(vmem_limit_bytes=VMEM_LIMIT),
        name="hyena_spectrum",
    )(jnp.asarray(z), jnp.asarray(t), jnp.asarray(deltas), w1, b1.reshape(1, -1), w2, b2.reshape(1, -1),
      w3, freq, d, fwd)


def _hyena_kernel(seq_len, n_seq, u_ref, sw_ref, sb_ref, g_ref, fwd_ref, inv_ref, o_ref):
    t_idx = lax.broadcasted_iota(jnp.int32, (seq_len, u_ref.shape[1]), 0)
    row0 = lax.broadcasted_iota(jnp.int32, (seq_len, D_B), 0) == 0
    for s in range(n_seq):
        rows = slice(s * seq_len, (s + 1) * seq_len)
        u = u_ref[rows, :]
        prev = jnp.where(t_idx == 0, 0.0, pltpu.roll(u, 1, axis=0))
        nxt = jnp.where(t_idx == seq_len - 1, 0.0, pltpu.roll(u, seq_len - 1, axis=0))
        u = prev * sw_ref[0:1, :] + u * sw_ref[1:2, :] + nxt * sw_ref[2:3, :] + sb_ref[...]
        z = u[:, 0:D_B]
        for n in range(HY_ORDER):
            spec = _dot(fwd_ref[...], z.astype(BF16))
            ure, uim = spec[0:seq_len], spec[seq_len:]
            gre, gim = g_ref[n, 0:seq_len, :], g_ref[n, seq_len:, :]
            pim = uim * gim
            yre = ure * gre - jnp.where(row0, 0.0, pim)
            yim = jnp.where(row0, pim, ure * gim + uim * gre)
            y = jnp.concatenate([yre, yim], axis=0).astype(BF16)
            z = u[:, (n + 1) * D_B:(n + 2) * D_B] * _dot(inv_ref[...], y)
        o_ref[rows, :] = z.astype(o_ref.dtype)


def hyena_mixer(u, nseq, seq_len, short_w, short_b, spectrum, fwd, inv):
    t = u.shape[0]
    width = (HY_ORDER + 1) * D_B
    col_block = (3 * D_A) // width
    assert col_block * width == 3 * D_A
    per_step = max(1, HY_STEP_ROWS // seq_len)
    assert nseq % per_step == 0
    return pl.pallas_call(
        functools.partial(_hyena_kernel, seq_len, per_step),
        grid=(nseq // per_step,),
        in_specs=[pl.BlockSpec((per_step * seq_len, width), lambda b: (b, col_block)),
                  pl.BlockSpec(short_w.shape, lambda b: (0, 0)),
                  pl.BlockSpec((1, width), lambda b: (0, 0)),
                  pl.BlockSpec(spectrum.shape, lambda b: (0, 0, 0)),
                  pl.BlockSpec(fwd.shape, lambda b: (0, 0)),
                  pl.BlockSpec(inv.shape, lambda b: (0, 0))],
        out_specs=pl.BlockSpec((per_step * seq_len, D_B), lambda b: (b, 0)),
        out_shape=jax.ShapeDtypeStruct((t, D_B), BF16),
        compiler_params=_cparams("arbitrary"),
        name="hyena_mixer",
    )(u, short_w, short_b.reshape(1, width), spectrum, fwd, inv)


RG_CB = LANES
RG_CHUNK = 512
TM_ROWS = 512


def _rglru_kernel(nseq, seq_len, gate_ref, xr_ref, cw_ref, cb_ref, wg_ref, bg_ref, lam_ref, h0_ref,
                  y_ref, fin_ref, xp_ref, a_f, b_f, a_b, b_b):
    t_tot = nseq * seq_len
    c = RG_CB
    pad = 2 * nseq
    xp_ref[0:pad, :] = jnp.zeros((pad, c), F32)
    xp_ref[pad + t_tot:, :] = jnp.zeros((pad, c), F32)
    xp_ref[pad:pad + t_tot, :] = xr_ref[...]
    nl = -lam_ref[...]
    sp = jnp.maximum(nl, 0.0) + jnp.log1p(jnp.exp(-jnp.abs(nl)))
    k2 = (-0.5 * RG_C * math.log2(math.e)) * sp

    def gate_chunk(ci, carry):
        r0 = pl.multiple_of(ci * RG_CHUNK, RG_CHUNK)
        xc = xp_ref[pl.ds(r0, RG_CHUNK), :] * cw_ref[0:1, :]
        for j in range(1, cw_ref.shape[0]):
            xc = xc + xp_ref[pl.ds(r0 + j * nseq, RG_CHUNK), :] * cw_ref[j:j + 1, :]
        xc = xc + cb_ref[...]
        gts = _dot(xc.astype(BF16), wg_ref[0]) + bg_ref[...]
        x_half = 0.5 * xc
        for d, (a_ref, b_ref) in enumerate(((a_f, b_f), (a_b, b_b))):
            t_r = jnp.tanh(gts[:, (2 * d) * c:(2 * d + 1) * c])
            t_i = jnp.tanh(gts[:, (2 * d + 1) * c:(2 * d + 2) * c])
            a = jnp.exp2(t_r * k2[d:d + 1, :] + k2[d:d + 1, :])
            a_ref[pl.ds(r0, RG_CHUNK), :] = a
            y = 1.0 - a * a
            root = jnp.where(y > 0.0, y * lax.rsqrt(y), 0.0)
            b_ref[pl.ds(r0, RG_CHUNK), :] = root * ((t_i + 1.0) * x_half)
        return carry

    lax.fori_loop(0, t_tot // RG_CHUNK, gate_chunk, 0)

    def scan_step(t, carry):
        hf, hb = carry
        rows_f = pl.ds(pl.multiple_of(t * nseq, nseq), nseq)
        rows_b = pl.ds(pl.multiple_of((seq_len - 1 - t) * nseq, nseq), nseq)
        hf = a_f[rows_f, :] * hf + b_f[rows_f, :]
        hb = a_b[rows_b, :] * hb + b_b[rows_b, :]
        b_f[rows_f, :] = hf
        b_b[rows_b, :] = hb
        return hf, hb

    hf, hb = lax.fori_loop(0, seq_len, scan_step, (h0_ref[0], h0_ref[1]), unroll=8)
    fin_ref[0] = hf
    fin_ref[1] = hb

    def out_chunk(ci, carry):
        rs = pl.ds(pl.multiple_of(ci * RG_CHUNK, RG_CHUNK), RG_CHUNK)
        y_ref[rs, :] = ((b_f[rs, :] + b_b[rs, :]) * jax.nn.gelu(gate_ref[rs, :])).astype(y_ref.dtype)
        return carry

    lax.fori_loop(0, t_tot // RG_CHUNK, out_chunk, 0)


def _rg_gate_weights(wa, wx):
    per_step = RG_CB // RG_BLOCK
    steps = D_RNN // RG_CB
    mats = []
    for d in range(2):
        for w in (wa[d], wx[d]):
            w = w.reshape(steps, per_step, RG_BLOCK, RG_BLOCK)
            eye = jnp.eye(per_step, dtype=w.dtype)
            m = jnp.einsum('spde,pq->spdqe', w, eye).reshape(steps, RG_CB, RG_CB)
            mats.append(m)
    return (0.5 * jnp.concatenate(mats, axis=-1)).astype(BF16)


def rglru_block(u, nseq, seq_len, conv_w, conv_b, wa, ba, wx, bx, lam, h0):
    t = u.shape[0]
    c = RG_CB
    steps = D_RNN // c
    wg = _rg_gate_weights(wa, wx)
    bg = jnp.stack([ba[0], bx[0], ba[1], bx[1]], axis=0).reshape(4, steps, c)
    bg = 0.5 * bg.transpose(1, 0, 2).reshape(steps, 1, 4 * c)
    y, fin = pl.pallas_call(
        functools.partial(_rglru_kernel, nseq, seq_len),
        grid=(steps,),
        in_specs=[pl.BlockSpec((t, c), lambda j: (0, j)),
                  pl.BlockSpec((t, c), lambda j: (0, steps + j)),
                  pl.BlockSpec((conv_w.shape[0], c), lambda j: (0, j)),
                  pl.BlockSpec((1, c), lambda j: (0, j)),
                  pl.BlockSpec((1, c, 4 * c), lambda j: (j, 0, 0)),
                  pl.BlockSpec((None, 1, 4 * c), lambda j: (j, 0, 0)),
                  pl.BlockSpec((2, c), lambda j: (0, j)),
                  pl.BlockSpec((2, nseq, c), lambda j: (0, 0, j))],
        out_specs=[pl.BlockSpec((t, c), lambda j: (0, j)),
                   pl.BlockSpec((2, nseq, c), lambda j: (0, 0, j))],
        out_shape=[jax.ShapeDtypeStruct((t, D_RNN), BF16),
                   jax.ShapeDtypeStruct((2, nseq, D_RNN), F32)],
        scratch_shapes=[pltpu.VMEM((t + 4 * nseq, c), F32)] + [pltpu.VMEM((t, c), F32)] * 4,
        compiler_params=_cparams("arbitrary"),
        name="rglru_block",
    )(u, u, conv_w, conv_b.reshape(1, -1), wg, bg, lam, h0)
    return y, fin


def _route_record(h, w, rb):
    h_hi = h.astype(BF16)
    h_lo = (h - h_hi.astype(F32)).astype(BF16)
    w_hi = w.astype(BF16)
    w_lo = (w - w_hi.astype(F32)).astype(BF16)
    logits = _dot_nt(w_hi, h_hi) + (_dot_nt(w_lo, h_hi) + _dot_nt(w_hi, h_lo))
    scores = jax.nn.sigmoid(logits)
    sel = scores + rb
    row = [sel[e:e + 1, :] for e in range(N_EXPERTS)]
    gs = []
    for g in range(N_GROUPS):
        r = row[g * EXPERTS_PER_GROUP:(g + 1) * EXPERTS_PER_GROUP]
        best_pair = None
        for i in range(EXPERTS_PER_GROUP):
            for j in range(i + 1, EXPERTS_PER_GROUP):
                s = r[i] + r[j]
                best_pair = s if best_pair is None else jnp.maximum(best_pair, s)
        gs.append(best_pair)
    best = jnp.zeros_like(gs[0], dtype=jnp.int32)
    top = gs[0]
    for g in range(1, N_GROUPS):
        better = gs[g] > top
        best = jnp.where(better, g, best)
        top = jnp.where(better, gs[g], top)
    picked = []
    for e in range(N_EXPERTS):
        g = e // EXPERTS_PER_GROUP
        rank = jnp.zeros_like(best)
        for o in range(g * EXPERTS_PER_GROUP, (g + 1) * EXPERTS_PER_GROUP):
            if o == e:
                continue
            ahead = (row[o] > row[e]) | ((row[o] == row[e]) & (o < e))
            rank = rank + ahead.astype(jnp.int32)
        picked.append((best == g) & (rank < 2))
    den = jnp.zeros_like(gs[0])
    for e in range(N_EXPERTS):
        den = den + jnp.where(picked[e], scores[e:e + 1, :], 0.0)
    gate = [jnp.where(picked[e], scores[e:e + 1, :] / den, 0.0) for e in range(N_EXPERTS)]
    cls = jnp.zeros_like(den)
    w_a = jnp.zeros_like(den)
    w_b = jnp.zeros_like(den)
    for g in range(N_GROUPS):
        for pi, (a, b) in enumerate(MOE_PAIRS):
            ea, eb = g * EXPERTS_PER_GROUP + a, g * EXPERTS_PER_GROUP + b
            both = picked[ea] & picked[eb]
            cls = jnp.where(both, float(g * len(MOE_PAIRS) + pi), cls)
            w_a = jnp.where(both, gate[ea], w_a)
            w_b = jnp.where(both, gate[eb], w_b)
    return jnp.concatenate([cls, w_a, w_b, jnp.zeros((ROUTE_ROWS - 3, cls.shape[1]), F32)], axis=0)


MOE_PAIRS = ((0, 1), (0, 2), (0, 3), (1, 3), (1, 2), (2, 3))
N_CLS = N_GROUPS * len(MOE_PAIRS)
CLS_PAD = 32
ROUTE_ROWS = 8
MOE_TS = 256
MOE_TM = 256
MOE_STEP_TILES = 4
SLOT_BLK = 512
ROW_W = D_MODEL + LANES


def _slots_kernel(n_blk, route_ref, slot_ref, off_ref, cnt_ref):
    cid = lax.broadcasted_iota(jnp.int32, (CLS_PAD, SLOT_BLK), 0).astype(F32)

    def members(j):
        cls = route_ref[0:1, pl.ds(pl.multiple_of(j * SLOT_BLK, SLOT_BLK), SLOT_BLK)]
        return (cid == cls).astype(F32)

    def count(j, cnt):
        return cnt + jnp.sum(members(j), axis=1, keepdims=True)

    cnt = lax.fori_loop(0, n_blk, count, jnp.zeros((CLS_PAD, 1), F32))
    cnt = jnp.broadcast_to(cnt, (CLS_PAD, LANES))
    padded = jnp.ceil(cnt * (1.0 / MOE_TS)) * MOE_TS
    r = lax.broadcasted_iota(jnp.int32, (CLS_PAD, CLS_PAD), 0)
    c = lax.broadcasted_iota(jnp.int32, (CLS_PAD, CLS_PAD), 1)
    off = jnp.dot((c < r).astype(F32), padded, precision=lax.Precision.HIGHEST, preferred_element_type=F32)
    off_ref[...] = off
    cnt_ref[...] = cnt
    tr = lax.broadcasted_iota(jnp.int32, (SLOT_BLK, SLOT_BLK), 0)
    tc = lax.broadcasted_iota(jnp.int32, (SLOT_BLK, SLOT_BLK), 1)
    earlier = (tr < tc).astype(BF16)

    def assign(j, base):
        member = members(j)
        rank = _dot(member.astype(BF16), earlier)
        slot = jnp.sum(member * (rank + base), axis=0, keepdims=True)
        slot_ref[0:1, pl.ds(pl.multiple_of(j * SLOT_BLK, SLOT_BLK), SLOT_BLK)] = slot.astype(jnp.int32)
        return base + jnp.sum(member, axis=1, keepdims=True)

    lax.fori_loop(0, n_blk, assign, off[:, 0:1])


def moe_slots(route):
    t = route.shape[1]
    stat = jax.ShapeDtypeStruct((CLS_PAD, LANES), F32)
    return pl.pallas_call(
        functools.partial(_slots_kernel, t // SLOT_BLK),
        out_shape=[jax.ShapeDtypeStruct((1, t), jnp.int32), stat, stat],
        compiler_params=pltpu.CompilerParams(vmem_limit_bytes=VMEM_LIMIT),
        name="moe_slots",
    )(route)


def _tile_maps(off, cnt, n_tiles):
    off = off[:N_CLS, 0].astype(jnp.int32)
    cnt = cnt[:N_CLS, 0].astype(jnp.int32)
    ends = off + ((cnt + MOE_TS - 1) // MOE_TS) * MOE_TS
    n_used = ends[-1] // MOE_TS
    k = jnp.arange(n_tiles, dtype=jnp.int32)
    tix = jnp.minimum(k, n_used - 1)
    cls = jnp.sum((tix[:, None] * MOE_TS >= ends[None, :]).astype(jnp.int32), axis=1)
    pair = jnp.asarray(MOE_PAIRS, jnp.int32)
    grp = (cls // len(MOE_PAIRS)) * EXPERTS_PER_GROUP
    ea = grp + pair[cls % len(MOE_PAIRS), 0]
    eb = grp + pair[cls % len(MOE_PAIRS), 1]
    n = jnp.int32(n_tiles)

    def slot_plan(e):
        chg = jnp.concatenate([jnp.ones((1,), jnp.int32), (e[1:] != e[:-1]).astype(jnp.int32)])
        at = jnp.where(chg == 1, k, n)
        nxt_at = jnp.concatenate([lax.cummin(at[::-1])[::-1][1:], n.reshape(1)])
        more = (nxt_at < n).astype(jnp.int32)
        nxt = e[jnp.minimum(nxt_at, n - 1)]
        par = (jnp.cumsum(chg) - 1) % 2
        return chg, nxt, more, par.astype(jnp.int32)

    plan_a, plan_b = slot_plan(ea), slot_plan(eb)
    chg, nxt, more, par = (jnp.stack([pa, pb]) for pa, pb in zip(plan_a, plan_b))
    return ea, eb, chg, nxt, more, par, n_used.reshape(1)


def _dispatch_kernel(n_steps, slots_ref, x_ref, g_ref, sc_ref, sh_ref, rt_ref, hs_in, hs_out, rowbuf, sem):
    del hs_in
    i = pl.program_id(0)
    cur = i % 2

    def wait_rows(s):
        pltpu.make_async_copy(rowbuf.at[s], rowbuf.at[s], sem.at[s]).wait()

    @pl.when(i >= 2)
    def _():
        wait_rows(cur)

    rowbuf[cur, :, 0:D_MODEL] = _normmod(x_ref[...], g_ref[...], sc_ref[0], sh_ref[0])
    rowbuf[cur, :, D_MODEL:ROW_W] = jnp.concatenate(
        [rt_ref[...], jnp.zeros((MOE_TM, LANES - ROUTE_ROWS), F32)], axis=1)

    for s in range(2):
        @pl.when(cur == s)
        def _():
            for r in range(MOE_TM):
                dst = slots_ref[i * MOE_TM + r]
                pltpu.make_async_copy(rowbuf.at[s, r], hs_out.at[dst], sem.at[s]).start()

    @pl.when(i == n_steps - 1)
    def _():
        wait_rows(cur)
        if n_steps >= 2:
            wait_rows(1 - cur)


def moe_dispatch(x, g, mod, route_t, slots, hs, seq_len, per_seq):
    t, d = x.shape
    n_steps = t // MOE_TM
    grid_spec = pltpu.PrefetchScalarGridSpec(
        num_scalar_prefetch=1,
        grid=(n_steps,),
        in_specs=[pl.BlockSpec((MOE_TM, d), lambda i, s: (i, 0)),
                  pl.BlockSpec((1, d), lambda i, s: (0, 0)),
                  _mod_spec(4, MOE_TM, seq_len, per_seq),
                  _mod_spec(3, MOE_TM, seq_len, per_seq),
                  pl.BlockSpec((MOE_TM, ROUTE_ROWS), lambda i, s: (i, 0)),
                  pl.BlockSpec(memory_space=pl.ANY)],
        out_specs=pl.BlockSpec(memory_space=pl.ANY),
        scratch_shapes=[pltpu.VMEM((2, MOE_TM, ROW_W), F32), pltpu.SemaphoreType.DMA((2,))],
    )
    return pl.pallas_call(
        functools.partial(_dispatch_kernel, n_steps),
        grid_spec=grid_spec,
        out_shape=jax.ShapeDtypeStruct(hs.shape, F32),
        input_output_aliases={6: 0},
        compiler_params=_cparams("arbitrary"),
        name="moe_dispatch",
    )(slots, x, g.reshape(1, d), mod, mod, route_t, hs)


def _experts_kernel(layer, ea_ref, eb_ref, chg_ref, nxt_ref, more_ref, par_ref, nused_ref,
                    hs_ref, wg_hbm, wu_hbm, wd_hbm, ys_ref, fg, fu, fd, bg, bu, bd, sem):
    def weight_copies(slot, expert, par):
        return [pltpu.make_async_copy(src.at[layer, expert], dst.at[slot, par], sem.at[slot, par])
                for src, dst in ((wg_hbm, fg), (wu_hbm, fu), (wd_hbm, fd))]

    def tile(k, rows):
        @pl.when(k < nused_ref[0])
        def _():
            for slot, e_ref in enumerate((ea_ref, eb_ref)):
                @pl.when(chg_ref[slot, k] == 1)
                def _():
                    par = par_ref[slot, k]

                    @pl.when(k == 0)
                    def _():
                        for cp in weight_copies(slot, e_ref[0], par):
                            cp.start()

                    for cp in weight_copies(slot, e_ref[k], par):
                        cp.wait()
                    bg[slot] = fg[slot, par].astype(BF16)
                    bu[slot] = fu[slot, par].astype(BF16)
                    bd[slot] = fd[slot, par].astype(BF16)

                    @pl.when(more_ref[slot, k] == 1)
                    def _():
                        for cp in weight_copies(slot, nxt_ref[slot, k], 1 - par):
                            cp.start()

            h = hs_ref[rows, 0:D_MODEL].astype(BF16)

            def ffn(slot):
                hid = _dot(h, bg[slot])
                up = _dot(h, bu[slot])
                w = hs_ref[rows, D_MODEL + 1 + slot:D_MODEL + 2 + slot]
                act = (hid * _sigmoid(hid)) * up * w
                return _dot(act.astype(BF16), bd[slot])

            ys_ref[rows, :] = ffn(0) + ffn(1)

        @pl.when(k >= nused_ref[0])
        def _():
            ys_ref[rows, :] = jnp.zeros((MOE_TS, ys_ref.shape[1]), ys_ref.dtype)

    for j in range(MOE_STEP_TILES):
        tile(pl.program_id(0) * MOE_STEP_TILES + j, slice(j * MOE_TS, (j + 1) * MOE_TS))


def moe_experts(hs, maps, layer, w_gate, w_up, w_down):
    n_tiles = hs.shape[0] // MOE_TS
    d = D_MODEL

    grid_spec = pltpu.PrefetchScalarGridSpec(
        num_scalar_prefetch=7,
        grid=(n_tiles // MOE_STEP_TILES,),
        in_specs=[pl.BlockSpec((MOE_STEP_TILES * MOE_TS, ROW_W), lambda k, *_: (k, 0)),
                  pl.BlockSpec(memory_space=pl.ANY), pl.BlockSpec(memory_space=pl.ANY),
                  pl.BlockSpec(memory_space=pl.ANY)],
        out_specs=pl.BlockSpec((MOE_STEP_TILES * MOE_TS, d), lambda k, *_: (k, 0)),
        scratch_shapes=[pltpu.VMEM((2, 2, d, D_EXPERT), F32), pltpu.VMEM((2, 2, d, D_EXPERT), F32),
                        pltpu.VMEM((2, 2, D_EXPERT, d), F32),
                        pltpu.VMEM((2, d, D_EXPERT), BF16), pltpu.VMEM((2, d, D_EXPERT), BF16),
                        pltpu.VMEM((2, D_EXPERT, d), BF16),
                        pltpu.SemaphoreType.DMA((2, 2))],
    )
    return pl.pallas_call(
        functools.partial(_experts_kernel, layer),
        grid_spec=grid_spec,
        out_shape=jax.ShapeDtypeStruct((hs.shape[0], d), F32),
        compiler_params=_cparams("arbitrary"),
        name="moe_experts",
    )(*maps, hs, w_gate, w_up, w_down)


def _combine_kernel(final, n_steps, slots_ref, x_ref, g2_ref, fg_ref, ys_hbm, o_ref, gbuf, sem):
    i = pl.program_id(0)
    cur = i % 2

    def issue_tile(tile, s):
        for r in range(MOE_TM):
            src = slots_ref[tile * MOE_TM + r]
            pltpu.make_async_copy(ys_hbm.at[src], gbuf.at[s, r], sem.at[s]).start()

    @pl.when(i == 0)
    def _():
        issue_tile(0, 0)

    for s in range(2):
        @pl.when((i + 1 < n_steps) & (1 - cur == s))
        def _():
            issue_tile(i + 1, s)

    pltpu.make_async_copy(gbuf.at[cur], gbuf.at[cur], sem.at[cur]).wait()
    y = x_ref[...] + g2_ref[0] * gbuf[cur]
    if final:
        ms = jnp.mean(y * y, axis=-1, keepdims=True)
        y = y * lax.rsqrt(ms + EPS) * fg_ref[...]
    o_ref[...] = y


def moe_combine(x, mod, slots, ys, final_g, final, seq_len, per_seq):
    t, d = x.shape
    n_steps = t // MOE_TM
    grid_spec = pltpu.PrefetchScalarGridSpec(
        num_scalar_prefetch=1,
        grid=(n_steps,),
        in_specs=[pl.BlockSpec((MOE_TM, d), lambda i, s: (i, 0)),
                  _mod_spec(5, MOE_TM, seq_len, per_seq),
                  pl.BlockSpec((1, d), lambda i, s: (0, 0)),
                  pl.BlockSpec(memory_space=pl.ANY)],
        out_specs=pl.BlockSpec((MOE_TM, d), lambda i, s: (i, 0)),
        scratch_shapes=[pltpu.VMEM((2, MOE_TM, d), F32), pltpu.SemaphoreType.DMA((2,))],
    )
    return pl.pallas_call(
        functools.partial(_combine_kernel, final, n_steps),
        grid_spec=grid_spec,
        out_shape=jax.ShapeDtypeStruct((t, d), F32),
        compiler_params=_cparams("arbitrary"),
        name="moe_combine",
    )(slots, x, mod, final_g.reshape(1, d), ys)


SC_WINDOW = 128
SC_COLS = 256


def sc_gather_rows(table, idx):
    n = idx.shape[0]
    d = table.shape[1]
    mesh = plsc.VectorSubcoreMesh(core_axis_name="core", subcore_axis_name="subcore")

    @pl.kernel(out_type=jax.ShapeDtypeStruct((n, d), table.dtype), mesh=mesh)
    def gather(x_hbm, i_hbm, o_hbm):
        for j in range(d // SC_COLS):
            def body(i_vmem, o_vmem, j=j):
                pltpu.sync_copy(x_hbm.at[i_vmem.at[0], pl.ds(j * SC_COLS, SC_COLS)], o_vmem)

            pltpu.emit_pipeline(
                body,
                grid=(n // SC_WINDOW,),
                in_specs=[pl.BlockSpec((1, SC_WINDOW), lambda i: (0, i))],
                out_specs=[pl.BlockSpec((SC_WINDOW, SC_COLS), lambda i, j=j: (i, j))],
                core_axis_name="subcore",
                dimension_semantics=(pltpu.PARALLEL,),
            )(i_hbm, o_hbm)

    return gather(table, idx.reshape(1, n))


def _residual_norm_kernel(final, x_ref, y_ref, g2_ref, fg_ref, o_ref):
    y = x_ref[...] + g2_ref[0] * y_ref[...]
    if final:
        ms = jnp.mean(y * y, axis=-1, keepdims=True)
        y = y * lax.rsqrt(ms + EPS) * fg_ref[...]
    o_ref[...] = y


def moe_combine_sc(x, mod, slots, ys, final_g, final, seq_len, per_seq, tm=512):
    t, d = x.shape
    y = sc_gather_rows(ys, slots)
    return pl.pallas_call(
        functools.partial(_residual_norm_kernel, final),
        grid=(t // tm,),
        in_specs=[pl.BlockSpec((tm, d), lambda i: (i, 0)),
                  pl.BlockSpec((tm, d), lambda i: (i, 0)),
                  _mod_spec(5, tm, seq_len, per_seq),
                  pl.BlockSpec((1, d), lambda i: (0, 0))],
        out_specs=pl.BlockSpec((tm, d), lambda i: (i, 0)),
        out_shape=jax.ShapeDtypeStruct((t, d), F32),
        compiler_params=_cparams("arbitrary"),
        name="moe_residual_norm",
    )(x, y, mod, final_g.reshape(1, d))


def _combine_nm_tm_kernel(nseq, steps, seq_len, n_steps, slots_ref, x_ref, g2_ref, ys_hbm, g_ref, sc_ref,
                          sh_ref, w_ref, x2_ref, u_ref, gbuf, sem):
    i = pl.program_id(0)
    cur = i % 2

    def issue_tile(tile, s):
        for sq in range(nseq):
            for t in range(steps):
                src = slots_ref[sq * seq_len + tile * steps + t]
                pltpu.make_async_copy(ys_hbm.at[src], gbuf.at[s, sq * steps + t], sem.at[s]).start()

    @pl.when(i == 0)
    def _():
        issue_tile(0, 0)

    for s in range(2):
        @pl.when((i + 1 < n_steps) & (1 - cur == s))
        def _():
            issue_tile(i + 1, s)

    pltpu.make_async_copy(gbuf.at[cur], gbuf.at[cur], sem.at[cur]).wait()
    d = x_ref.shape[-1]
    y = x_ref[...] + g2_ref[...] * gbuf[cur].reshape(nseq, steps, d)
    x2_ref[...] = y
    h = _normmod(y, g_ref[...], sc_ref[...], sh_ref[...])
    h = h.reshape(nseq * steps, d).astype(BF16)
    h = _dot(_row_permutation(nseq, steps, True), h).astype(BF16)
    u_ref[...] = _dot(h, w_ref[...])


def combine_nm_matmul_tm(x, mod_prev, slots, ys, g, mod, w, nseq, seq_len, per_seq):
    t, d = x.shape
    n = w.shape[1]
    steps = TM_ROWS // nseq
    n_steps = seq_len // steps
    g2 = _group_mod(mod_prev, 5, nseq, per_seq)
    sc = _group_mod(mod, 1, nseq, per_seq)
    sh = _group_mod(mod, 0, nseq, per_seq)
    mod_spec = pl.BlockSpec(sc.shape, lambda i, s: (0, 0, 0))
    x_spec = pl.BlockSpec((nseq, steps, d), lambda i, s: (0, i, 0))
    grid_spec = pltpu.PrefetchScalarGridSpec(
        num_scalar_prefetch=1,
        grid=(n_steps,),
        in_specs=[x_spec, mod_spec, pl.BlockSpec(memory_space=pl.ANY),
                  pl.BlockSpec((1, d), lambda i, s: (0, 0)), mod_spec, mod_spec,
                  pl.BlockSpec((d, n), lambda i, s: (0, 0))],
        out_specs=[x_spec, pl.BlockSpec((TM_ROWS, n), lambda i, s: (i, 0))],
        scratch_shapes=[pltpu.VMEM((2, TM_ROWS, d), F32), pltpu.SemaphoreType.DMA((2,))],
    )
    x2, u = pl.pallas_call(
        functools.partial(_combine_nm_tm_kernel, nseq, steps, seq_len, n_steps),
        grid_spec=grid_spec,
        out_shape=[jax.ShapeDtypeStruct((nseq, seq_len, d), F32), jax.ShapeDtypeStruct((t, n), F32)],
        compiler_params=_cparams("arbitrary"),
        name="moe_combine_norm_mod_proj_tm",
    )(slots, x.reshape(nseq, seq_len, d), g2, ys, g.reshape(1, d), sc, sh, w)
    return x2.reshape(t, d), u


def moe_block(xs, routes, mods_l, per_seqs, seq_lens, layer, p, final, defer_combine, hs):
    g = p['norm_g'][layer, 1]
    slots, off, cnt = moe_slots(jnp.concatenate(routes, axis=1))
    t_all = slots.shape[1]
    n_tiles = -(-(t_all // MOE_TS + N_CLS) // MOE_STEP_TILES) * MOE_STEP_TILES
    maps = _tile_maps(off, cnt, n_tiles)
    if hs is None:
        hs = jnp.zeros((n_tiles * MOE_TS, ROW_W), F32)
    bounds = np.cumsum([0] + [x.shape[0] for x in xs])
    group_slots = [slots[0, bounds[i]:bounds[i + 1]] for i in range(len(xs))]
    for x, r, s, sl, ps in zip(xs, routes, group_slots, seq_lens, per_seqs):
        hs = moe_dispatch(x, g, mods_l, r.T, s, hs, sl, ps)
    ys = moe_experts(hs, maps, layer, p['moe_w_gate'], p['moe_w_up'], p['moe_w_down'])
    if defer_combine:
        return [(x, s, ys) for x, s in zip(xs, group_slots)], hs
    return [moe_combine_sc(x, mods_l, s, ys, p['final_g'], final, sl, ps)
            for x, s, sl, ps in zip(xs, group_slots, seq_lens, per_seqs)], hs


def _mixer(x, group, l, mod, p, hy_tables, mod_prev):
    per_seq, nseq, seq_len = group['per_seq'], group['nseq'], group['seq_len']
    extras = None
    deferred = isinstance(x, tuple)
    if deferred and l % 2 == 0:
        x = moe_combine(x[0], mod_prev, x[1], x[2], p['final_g'], False, seq_len, per_seq)
        deferred = False
    if l % 2 == 0:
        e = l // 2
        u = nm_matmul(x, p['norm_g'][l, 0], mod, p['a_in_w'][e].astype(BF16), seq_len, per_seq)
        if group['ctx_k'] is None:
            attn, nk, nv = context_attention(u, nseq, seq_len)
            extras = (nk, nv)
        else:
            attn = neighbourhood_attention(u, group['ctx_k'][:, e], group['ctx_v'][:, e], p['na_rpb'][e],
                                           nseq, seq_len)
        fwd, inv = hy_tables[seq_len]
        spectrum = hyena_spectrum(seq_len, p['hy_w1'][e], p['hy_b1'][e], p['hy_w2'][e], p['hy_b2'][e],
                                  p['hy_w3'][e], p['hy_freq'][e], p['hy_d'][e], fwd)
        hy = hyena_mixer(u, nseq, seq_len, p['hy_short_w'][e], p['hy_short_b'][e], spectrum, fwd, inv)
        w_out = p['a_out_w'][e].astype(BF16)
        x, route = proj_residual([attn, hy], [w_out[:D_A], w_out[D_A:]], x, mod, p['norm_g'][l, 1],
                                 p['router_w'], p['router_b'], seq_len, per_seq)
    else:
        o = l // 2
        w_in = p['c_in_w'][o].astype(BF16)
        if deferred:
            x, u = combine_nm_matmul_tm(x[0], mod_prev, x[1], x[2], p['norm_g'][l, 0], mod, w_in,
                                        nseq, seq_len, per_seq)
        else:
            u = nm_matmul_tm(x, p['norm_g'][l, 0], mod, w_in, nseq, seq_len, per_seq)
        y, extras = rglru_block(u, nseq, seq_len, p['rg_conv_w'][o], p['rg_conv_b'][o], p['rg_wa'][o],
                                p['rg_ba'][o], p['rg_wx'][o], p['rg_bx'][o], p['rg_lam'][o], group['h0'][o])
        x, route = proj_residual_tm(y, p['c_out_w'][o].astype(BF16), x, mod, p['norm_g'][l, 1],
                                    p['router_w'], p['router_b'], nseq, seq_len, per_seq)
    return x, extras, route


def kernel(x_prompt, x_sample, cache_k, cache_v, state_h, c, c_ctx, norm_g, ada_w, ada_b, final_g, a_in_w, a_out_w, na_rpb, hy_short_w, hy_short_b, hy_w1, hy_b1, hy_w2, hy_b2, hy_w3, hy_freq, hy_d, c_in_w, c_out_w, rg_conv_w, rg_conv_b, rg_wa, rg_ba, rg_wx, rg_bx, rg_lam, router_w, router_b, moe_w_gate, moe_w_up, moe_w_down):
    p = dict(norm_g=norm_g, final_g=final_g, a_in_w=a_in_w, a_out_w=a_out_w, na_rpb=na_rpb,
             hy_short_w=hy_short_w, hy_short_b=hy_short_b, hy_w1=hy_w1, hy_b1=hy_b1, hy_w2=hy_w2,
             hy_b2=hy_b2, hy_w3=hy_w3, hy_freq=hy_freq, hy_d=hy_d, c_in_w=c_in_w, c_out_w=c_out_w,
             rg_conv_w=rg_conv_w, rg_conv_b=rg_conv_b, rg_wa=rg_wa, rg_ba=rg_ba, rg_wx=rg_wx, rg_bx=rg_bx,
             rg_lam=rg_lam, router_w=router_w, router_b=router_b, moe_w_gate=moe_w_gate,
             moe_w_up=moe_w_up, moe_w_down=moe_w_down)
    batch, seq, d = x_prompt.shape
    dec_batch, dec_seq, _ = x_sample.shape
    n_odd = DEPTH // 2
    assert 1 + dec_batch <= MOD_ROWS

    cond = jnp.concatenate([c_ctx[None, :], c, jnp.zeros((MOD_ROWS - 1 - dec_batch, d), F32)], axis=0)
    m = modulation(cond, ada_w, ada_b)
    mods = [m[l].reshape(MOD_ROWS * N_MOD, 1, d) for l in range(DEPTH)]

    tables = {}
    for sl in (seq, dec_seq):
        fwd, inv = _dft_tables(sl)
        tables[sl] = (jnp.asarray(fwd).astype(BF16), jnp.asarray(inv).astype(BF16))

    groups = [
        dict(per_seq=False, nseq=batch, seq_len=seq, ctx_k=None, ctx_v=None,
             h0=[jnp.zeros((2, batch, D_RNN), F32)] * n_odd),
        dict(per_seq=True, nseq=dec_batch, seq_len=dec_seq, ctx_k=cache_k, ctx_v=cache_v,
             h0=[state_h[:, o].transpose(1, 0, 2) for o in range(n_odd)]),
    ]
    xs = [x_prompt.reshape(batch * seq, d), x_sample.reshape(dec_batch * dec_seq, d)]
    k_list, v_list, h_list = [], [], []
    hs = None
    for l in range(DEPTH):
        mixed = [_mixer(x, grp, l, mods[l], p, tables, mods[l - 1] if l else None)
                 for x, grp in zip(xs, groups)]
        if l % 2 == 0:
            k_list.append(mixed[0][1][0])
            v_list.append(mixed[0][1][1])
        else:
            h_list.append(mixed[0][1].transpose(1, 0, 2))
        xs, hs = moe_block([mx[0] for mx in mixed], [mx[2] for mx in mixed], mods[l],
                           [grp['per_seq'] for grp in groups], [grp['seq_len'] for grp in groups],
                           l, p, l == DEPTH - 1, l < DEPTH - 1, hs)
    new_k = jnp.stack(k_list, axis=1)
    new_v = jnp.stack(v_list, axis=1)
    new_h = jnp.stack(h_list, axis=1)
    return (xs[0].reshape(batch, seq, d), xs[1].reshape(dec_batch, dec_seq, d), new_k, new_v, new_h)
```

```python
import functools
import math

import numpy as np
import jax
import jax.numpy as jnp
from jax import lax
from jax.experimental import pallas as pl
from jax.experimental.pallas import tpu as pltpu
from jax.experimental.pallas import tpu_sc as plsc

F32 = jnp.float32
BF16 = jnp.bfloat16

D_MODEL = 1024
DEPTH = 2
GRID_W = 64
EPS = 1e-6
NEG_INF = -1e30
NA_HEADS = 8
HEAD_DIM = 64
D_A = NA_HEADS * HEAD_DIM
WIN_ROWS = 8
WIN_COLS = 16
D_B = D_MODEL - D_A
HY_ORDER = 2
HY_EMB = 33
HY_BANDS = (HY_EMB - 1) // 2
HY_FFN = 64
HY_DECAY_TARGET = 1e-2
HY_FAST_PCT = 0.3
HY_SLOW_PCT = 1.5
D_RNN = D_MODEL
RG_BLOCK = 64
RG_C = 8.0
N_EXPERTS = 16
N_GROUPS = 4
EXPERTS_PER_GROUP = N_EXPERTS // N_GROUPS
D_EXPERT = 512

LANES = 128
VMEM_LIMIT = 56 * 1024 * 1024
N_MOD = 6
MOD_ROWS = 16


def _cparams(*sem):
    return pltpu.CompilerParams(dimension_semantics=sem, vmem_limit_bytes=VMEM_LIMIT)


def _dot(a, b):
    return jnp.dot(a, b, preferred_element_type=F32)


def _dot_nt(a, b):
    return lax.dot_general(a, b, (((1,), (1,)), ((), ())), preferred_element_type=F32)


def _sigmoid(x):
    return 0.5 * jnp.tanh(0.5 * x) + 0.5


def _normmod(x, g, sc, sh):
    ms = jnp.mean(x * x, axis=-1, keepdims=True)
    return (x * lax.rsqrt(ms + EPS) * g) * (1.0 + sc) + sh


def _mod_spec(chunk, tm, seq_len, per_seq):
    if per_seq:
        return pl.BlockSpec((1, 1, D_MODEL), lambda i, *_: ((1 + (i * tm) // seq_len) * N_MOD + chunk, 0, 0))
    return pl.BlockSpec((1, 1, D_MODEL), lambda i, *_: (chunk, 0, 0))


def _mod_kernel(c_ref, w_ref, b_ref, o_ref):
    s = c_ref[...]
    s = s * jax.nn.sigmoid(s)
    o_ref[0] = _dot(s.astype(BF16), w_ref[0].astype(BF16)) + b_ref[0]


def modulation(cond, ada_w, ada_b):
    n = ada_w.shape[-1]
    tn = n // 4
    assert tn % LANES == 0
    return pl.pallas_call(
        _mod_kernel,
        grid=(DEPTH, n // tn),
        in_specs=[pl.BlockSpec((MOD_ROWS, D_MODEL), lambda l, j: (0, 0)),
                  pl.BlockSpec((1, D_MODEL, tn), lambda l, j: (l, 0, j)),
                  pl.BlockSpec((1, 1, tn), lambda l, j: (l, 0, j))],
        out_specs=pl.BlockSpec((1, MOD_ROWS, tn), lambda l, j: (l, 0, j)),
        out_shape=jax.ShapeDtypeStruct((DEPTH, MOD_ROWS, n), F32),
        compiler_params=_cparams("arbitrary", "arbitrary"),
        name="modulation",
    )(cond, ada_w, ada_b.reshape(DEPTH, 1, n))


def _nm_matmul_kernel(x_ref, g_ref, sc_ref, sh_ref, w_ref, o_ref):
    h = _normmod(x_ref[...], g_ref[...], sc_ref[0], sh_ref[0])
    o_ref[...] = _dot(h.astype(BF16), w_ref[...])


def nm_matmul(x, g, mod, w, seq_len, per_seq, tm=512):
    t, d = x.shape
    n = w.shape[1]
    return pl.pallas_call(
        _nm_matmul_kernel,
        grid=(t // tm,),
        in_specs=[pl.BlockSpec((tm, d), lambda i: (i, 0)),
                  pl.BlockSpec((1, d), lambda i: (0, 0)),
                  _mod_spec(1, tm, seq_len, per_seq),
                  _mod_spec(0, tm, seq_len, per_seq),
                  pl.BlockSpec((d, n), lambda i: (0, 0))],
        out_specs=pl.BlockSpec((tm, n), lambda i: (i, 0)),
        out_shape=jax.ShapeDtypeStruct((t, n), F32),
        compiler_params=_cparams("arbitrary"),
        name="norm_mod_proj",
    )(x, g.reshape(1, d), mod, mod, w)


def _proj_res_kernel(n_act, *refs):
    acts = refs[:n_act]
    ws = refs[n_act:2 * n_act]
    x_ref, g_ref, ng_ref, sc2_ref, sh2_ref, rw_ref, rb_ref, o_ref, route_ref = refs[2 * n_act:]
    acc = _dot(acts[0][...].astype(BF16), ws[0][...])
    for a, w in zip(acts[1:], ws[1:]):
        acc += _dot(a[...].astype(BF16), w[...])
    x = x_ref[...] + g_ref[0] * acc
    o_ref[...] = x
    route_ref[...] = _route_record(_normmod(x, ng_ref[...], sc2_ref[0], sh2_ref[0]), rw_ref[...], rb_ref[...])


def proj_residual(acts, ws, x, mod, norm2_g, router_w, router_b, seq_len, per_seq, tm=512):
    t, d = x.shape
    in_specs = [pl.BlockSpec((tm, a.shape[1]), lambda i: (i, 0)) for a in acts]
    in_specs += [pl.BlockSpec(w.shape, lambda i: (0, 0)) for w in ws]
    in_specs += [pl.BlockSpec((tm, d), lambda i: (i, 0)), _mod_spec(2, tm, seq_len, per_seq),
                 pl.BlockSpec((1, d), lambda i: (0, 0)),
                 _mod_spec(4, tm, seq_len, per_seq), _mod_spec(3, tm, seq_len, per_seq),
                 pl.BlockSpec((N_EXPERTS, d), lambda i: (0, 0)),
                 pl.BlockSpec((N_EXPERTS, 1), lambda i: (0, 0))]
    return pl.pallas_call(
        functools.partial(_proj_res_kernel, len(acts)),
        grid=(t // tm,),
        in_specs=in_specs,
        out_specs=[pl.BlockSpec((tm, d), lambda i: (i, 0)), pl.BlockSpec((ROUTE_ROWS, tm), lambda i: (0, i))],
        out_shape=[jax.ShapeDtypeStruct((t, d), F32), jax.ShapeDtypeStruct((ROUTE_ROWS, t), F32)],
        compiler_params=_cparams("arbitrary"),
        name="proj_residual_route",
    )(*acts, *ws, x, mod, norm2_g.reshape(1, d), mod, mod, router_w.T, router_b.reshape(N_EXPERTS, 1))


def _row_permutation(nseq, steps, to_time_major):
    n = nseq * steps
    i = lax.broadcasted_iota(jnp.int32, (n, n), 0)
    j = lax.broadcasted_iota(jnp.int32, (n, n), 1)
    if to_time_major:
        src = (i % nseq) * steps + i // nseq
    else:
        src = (i % steps) * nseq + i // steps
    return (j == src).astype(BF16)


def _nm_matmul_tm_kernel(nseq, steps, x_ref, g_ref, sc_ref, sh_ref, w_ref, o_ref):
    h = _normmod(x_ref[...], g_ref[...], sc_ref[...], sh_ref[...])
    h = h.reshape(nseq * steps, h.shape[-1]).astype(BF16)
    h = _dot(_row_permutation(nseq, steps, True), h).astype(BF16)
    o_ref[...] = _dot(h, w_ref[...])


def _group_mod(mod, chunk, nseq, per_seq):
    rows = mod.reshape(MOD_ROWS, N_MOD, 1, D_MODEL)
    return rows[1:1 + nseq, chunk] if per_seq else rows[0:1, chunk]


def nm_matmul_tm(x, g, mod, w, nseq, seq_len, per_seq):
    t, d = x.shape
    n = w.shape[1]
    steps = TM_ROWS // nseq
    sc = _group_mod(mod, 1, nseq, per_seq)
    sh = _group_mod(mod, 0, nseq, per_seq)
    mod_spec = pl.BlockSpec(sc.shape, lambda i: (0, 0, 0))
    return pl.pallas_call(
        functools.partial(_nm_matmul_tm_kernel, nseq, steps),
        grid=(seq_len // steps,),
        in_specs=[pl.BlockSpec((nseq, steps, d), lambda i: (0, i, 0)),
                  pl.BlockSpec((1, d), lambda i: (0, 0)),
                  mod_spec, mod_spec,
                  pl.BlockSpec((d, n), lambda i: (0, 0))],
        out_specs=pl.BlockSpec((TM_ROWS, n), lambda i: (i, 0)),
        out_shape=jax.ShapeDtypeStruct((t, n), F32),
        compiler_params=_cparams("arbitrary"),
        name="norm_mod_proj_tm",
    )(x.reshape(nseq, seq_len, d), g.reshape(1, d), sc, sh, w)


def _proj_res_tm_kernel(nseq, steps, y_ref, w_ref, x_ref, g_ref, ng_ref, sc2_ref, sh2_ref, rw_ref, rb_ref,
                        o_ref, route_ref):
    y = _dot(_row_permutation(nseq, steps, False), y_ref[...].astype(BF16)).astype(BF16)
    acc = _dot(y, w_ref[...])
    x = x_ref[...] + g_ref[...] * acc.reshape(nseq, steps, acc.shape[-1])
    o_ref[...] = x
    h = _normmod(x, ng_ref[...], sc2_ref[...], sh2_ref[...]).reshape(nseq * steps, x.shape[-1])
    route_ref[0] = _route_record(h, rw_ref[...], rb_ref[...])


def proj_residual_tm(y, w, x, mod, norm2_g, router_w, router_b, nseq, seq_len, per_seq):
    t, d = x.shape
    steps = TM_ROWS // nseq
    n_steps = seq_len // steps
    g1 = _group_mod(mod, 2, nseq, per_seq)
    sc2 = _group_mod(mod, 4, nseq, per_seq)
    sh2 = _group_mod(mod, 3, nseq, per_seq)
    mod_spec = pl.BlockSpec(g1.shape, lambda i: (0, 0, 0))
    out, route = pl.pallas_call(
        functools.partial(_proj_res_tm_kernel, nseq, steps),
        grid=(n_steps,),
        in_specs=[pl.BlockSpec((TM_ROWS, y.shape[1]), lambda i: (i, 0)),
                  pl.BlockSpec(w.shape, lambda i: (0, 0)),
                  pl.BlockSpec((nseq, steps, d), lambda i: (0, i, 0)),
                  mod_spec,
                  pl.BlockSpec((1, d), lambda i: (0, 0)),
                  mod_spec, mod_spec,
                  pl.BlockSpec((N_EXPERTS, d), lambda i: (0, 0)),
                  pl.BlockSpec((N_EXPERTS, 1), lambda i: (0, 0))],
        out_specs=[pl.BlockSpec((nseq, steps, d), lambda i: (0, i, 0)),
                   pl.BlockSpec((1, ROUTE_ROWS, TM_ROWS), lambda i: (i, 0, 0))],
        out_shape=[jax.ShapeDtypeStruct((nseq, seq_len, d), F32),
                   jax.ShapeDtypeStruct((n_steps, ROUTE_ROWS, TM_ROWS), F32)],
        compiler_params=_cparams("arbitrary"),
        name="proj_residual_tm_route",
    )(y, w, x.reshape(nseq, seq_len, d), g1, norm2_g.reshape(1, d), sc2, sh2, router_w.T,
      router_b.reshape(N_EXPERTS, 1))
    route = route.reshape(n_steps, ROUTE_ROWS, nseq, steps).transpose(1, 2, 0, 3).reshape(ROUTE_ROWS, t)
    return out.reshape(t, d), route


CTX_STEP_ROWS = 1024


def _ctx_attn_kernel(seq_len, n_seq, q_ref, k_ref, v_ref, o_ref, nk_ref, nv_ref):
    scale = HEAD_DIM ** -0.5
    per_tile = LANES // HEAD_DIM
    lane = lax.broadcasted_iota(jnp.int32, (1, LANES), 1)
    for b in range(n_seq):
        rows = slice(b * seq_len, (b + 1) * seq_len)
        for hp in range(NA_HEADS // per_tile):
            sl = slice(hp * LANES, (hp + 1) * LANES)
            q, k, v = q_ref[rows, sl] * scale, k_ref[rows, sl], v_ref[rows, sl]
            kb, vb = k.astype(BF16), v.astype(BF16)
            out = None
            for j in range(per_tile):
                h = hp * per_tile + j
                nk_ref[b, h] = k[:, j * HEAD_DIM:(j + 1) * HEAD_DIM]
                nv_ref[b, h] = v[:, j * HEAD_DIM:(j + 1) * HEAD_DIM]
                mine = lane // HEAD_DIM == j
                s = _dot_nt(jnp.where(mine, q, 0.0).astype(BF16), kb)
                p = jnp.exp(s - jnp.max(s, axis=-1, keepdims=True))
                o = _dot(p.astype(BF16), vb) / jnp.sum(p, axis=-1, keepdims=True)
                out = o if out is None else jnp.where(mine, o, out)
            o_ref[rows, sl] = out.astype(o_ref.dtype)


def context_attention(u, nseq, seq_len):
    t = u.shape[0]
    per_step = max(1, CTX_STEP_ROWS // seq_len)
    assert nseq % per_step == 0
    rows = per_step * seq_len
    kv_shape = jax.ShapeDtypeStruct((nseq, NA_HEADS, seq_len, HEAD_DIM), F32)
    kv_spec = pl.BlockSpec((per_step, NA_HEADS, seq_len, HEAD_DIM), lambda b: (b, 0, 0, 0))
    return pl.pallas_call(
        functools.partial(_ctx_attn_kernel, seq_len, per_step),
        grid=(nseq // per_step,),
        in_specs=[pl.BlockSpec((rows, D_A), lambda b: (b, 0)),
                  pl.BlockSpec((rows, D_A), lambda b: (b, 1)),
                  pl.BlockSpec((rows, D_A), lambda b: (b, 2))],
        out_specs=[pl.BlockSpec((rows, D_A), lambda b: (b, 0)), kv_spec, kv_spec],
        out_shape=[jax.ShapeDtypeStruct((t, D_A), BF16), kv_shape, kv_shape],
        compiler_params=_cparams("arbitrary"),
        name="context_attention",
    )(u, u, u)


N_DR = 2 * WIN_ROWS - 1
N_DC = 2 * WIN_COLS - 1
N_DC_PAD = 32


def _na_col_tables():
    cols = np.arange(GRID_W)
    col_start = np.clip(cols - WIN_COLS // 2, 0, GRID_W - WIN_COLS)
    col_in = (cols[None, :] >= col_start[:, None]) & (cols[None, :] < col_start[:, None] + WIN_COLS)
    dc = np.clip(cols[None, :] - cols[:, None], 1 - WIN_COLS, WIN_COLS - 1) + WIN_COLS - 1
    onehot = (dc.reshape(1, -1) == np.arange(N_DC_PAD)[:, None]).astype(np.float32)
    return onehot, col_in.reshape(1, -1).astype(np.float32)


def _na_bias_kernel(r_ref, e_ref, m_ref, o_ref):
    t = jnp.dot(r_ref[...], e_ref[...], precision=lax.Precision.HIGHEST, preferred_element_type=F32)
    o_ref[...] = jnp.where(m_ref[...] > 0.0, t, NEG_INF)


def na_bias_table(rpb):
    onehot, col_in = _na_col_tables()
    n_rows = NA_HEADS * N_DR
    assert n_rows <= LANES
    r = jnp.zeros((LANES, N_DC_PAD), F32).at[:n_rows, :N_DC].set(rpb.reshape(n_rows, N_DC).astype(F32))
    t = pl.pallas_call(
        _na_bias_kernel,
        out_shape=jax.ShapeDtypeStruct((LANES, GRID_W * GRID_W), F32),
        name="na_bias_table",
    )(r, jnp.asarray(onehot), jnp.asarray(col_in))
    t = t[:n_rows].reshape(NA_HEADS, N_DR, GRID_W, GRID_W)
    return jnp.concatenate([t[:, :-1], t[:, 1:]], axis=-1)


def _na_kernel(rows, q_ref, k_ref, v_ref, ck_ref, cv_ref, bias_ref, o_ref,
               q_s, k_s, v_s, ck_s, cv_s, s_s, p_s, den_s, o_s):
    scale = HEAD_DIM ** -0.5
    n_lat = WIN_ROWS * GRID_W
    per_tile = LANES // HEAD_DIM
    n_pairs = NA_HEADS // per_tile
    lane = lax.broadcasted_iota(jnp.int32, (1, LANES), 1)
    for hp in range(n_pairs):
        sl = slice(hp * LANES, (hp + 1) * LANES)
        q_s[hp] = (q_ref[:, sl] * scale).astype(BF16)
        k_s[hp] = k_ref[:, sl].astype(BF16)
        v_s[hp] = v_ref[:, sl].astype(BF16)
        heads = range(hp * per_tile, (hp + 1) * per_tile)
        ck_s[hp] = jnp.concatenate([ck_ref[0, h] for h in heads], axis=1).astype(BF16)
        cv_s[hp] = jnp.concatenate([cv_ref[0, h] for h in heads], axis=1).astype(BF16)

    def window(r):
        start = min(max(r - WIN_ROWS // 2, 0), rows - WIN_ROWS)
        return start, start - r + WIN_ROWS - 1

    def pair_body(hp, carry):
        for j in range(per_tile):
            h = hp * per_tile + j
            mine = lane // HEAD_DIM == j
            for r in range(rows):
                start, off = window(r)
                rs = slice(r * GRID_W, (r + 1) * GRID_W)
                q = jnp.where(mine, q_s[hp, rs, :], 0.0).astype(BF16)
                bias = jnp.concatenate([bias_ref[h, off + 2 * i] for i in range(WIN_ROWS // 2)], axis=1)
                s_s[rs, 0:n_lat] = _dot_nt(q, k_s[hp, start * GRID_W:start * GRID_W + n_lat, :]) + bias
                s_s[rs, n_lat:] = _dot_nt(q, ck_s[hp])
            for r in range(rows):
                rs = slice(r * GRID_W, (r + 1) * GRID_W)
                s = s_s[rs, :]
                p = jnp.exp(s - jnp.max(s, axis=-1, keepdims=True))
                den_s[rs, :] = jnp.sum(p, axis=-1, keepdims=True)
                p_s[rs, :] = p.astype(BF16)
            for r in range(rows):
                start, _ = window(r)
                rs = slice(r * GRID_W, (r + 1) * GRID_W)
                o = (_dot(p_s[rs, 0:n_lat], v_s[hp, start * GRID_W:start * GRID_W + n_lat, :])
                     + _dot(p_s[rs, n_lat:], cv_s[hp])) / den_s[rs, :]
                o_s[hp, rs, :] = o if j == 0 else jnp.where(mine, o, o_s[hp, rs, :])
        return carry

    lax.fori_loop(0, n_pairs, pair_body, 0)
    for hp in range(n_pairs):
        o_ref[:, hp * LANES:(hp + 1) * LANES] = o_s[hp].astype(o_ref.dtype)


def neighbourhood_attention(u, ctx_k, ctx_v, rpb, nseq, seq_len):
    t = u.shape[0]
    rows = seq_len // GRID_W
    assert rows >= WIN_ROWS and WIN_ROWS % 2 == 0
    past = ctx_k.shape[2]
    bias = na_bias_table(rpb)
    ctx_spec = pl.BlockSpec((1, NA_HEADS, past, HEAD_DIM), lambda b: (b, 0, 0, 0))
    return pl.pallas_call(
        functools.partial(_na_kernel, rows),
        grid=(nseq,),
        in_specs=[pl.BlockSpec((seq_len, D_A), lambda b: (b, 0)),
                  pl.BlockSpec((seq_len, D_A), lambda b: (b, 1)),
                  pl.BlockSpec((seq_len, D_A), lambda b: (b, 2)),
                  ctx_spec, ctx_spec,
                  pl.BlockSpec(bias.shape, lambda b: (0, 0, 0, 0))],
        out_specs=pl.BlockSpec((seq_len, D_A), lambda b: (b, 0)),
        out_shape=jax.ShapeDtypeStruct((t, D_A), BF16),
        scratch_shapes=[pltpu.VMEM((D_A // LANES, seq_len, LANES), BF16)] * 3
        + [pltpu.VMEM((D_A // LANES, past, LANES), BF16)] * 2
        + [pltpu.VMEM((seq_len, WIN_ROWS * GRID_W + past), F32),
           pltpu.VMEM((seq_len, WIN_ROWS * GRID_W + past), BF16),
           pltpu.VMEM((seq_len, 1), F32),
           pltpu.VMEM((D_A // LANES, seq_len, LANES), F32)],
        compiler_params=_cparams("arbitrary"),
        name="neighbourhood_attention",
    )(u, u, u, ctx_k, ctx_v, bias)


HY_STEP_ROWS = 1024


def _dft_tables(seq_len):
    n = 2 * seq_len
    f = np.arange(seq_len, dtype=np.int64)
    ang = (np.outer(f, f) % n).astype(np.float64) * (math.pi / seq_len)
    cos, sin = np.cos(ang), np.sin(ang)
    alt = np.where(f % 2 == 0, 1.0, -1.0)
    s_fwd = -sin
    s_fwd[0, :] = alt
    fwd = np.concatenate([cos, s_fwd], axis=0)
    wf = np.where(f == 0, 1.0, 2.0) / n
    ci = cos.T * wf[None, :]
    si = -sin.T * wf[None, :]
    si[:, 0] = alt / n
    inv = np.concatenate([ci, si], axis=1)
    return fwd.astype(np.float32), inv.astype(np.float32)


def _hyena_feats(seq_len):
    t = np.linspace(0.0, 1.0, seq_len, dtype=np.float32)[:, None]
    w = (2.0 * math.pi * np.arange(seq_len, dtype=np.float32)[:, None] / seq_len).astype(np.float32)
    f = np.linspace(1e-4, HY_BANDS - 1, HY_BANDS, dtype=np.float32)[None, :]
    z = np.concatenate([t, np.cos(f * w), -np.sin(f * w)], axis=-1).astype(np.float32)
    max_decay = math.log(HY_DECAY_TARGET) / HY_FAST_PCT
    min_decay = math.log(HY_DECAY_TARGET) / HY_SLOW_PCT
    deltas = np.abs(np.linspace(min_decay, max_decay, D_B, dtype=np.float32))[None, :]
    return z, t, deltas


def _hy_filter_kernel(seq_len, z_ref, t_ref, dl_ref, w1_ref, b1_ref, w2_ref, b2_ref, w3_ref, fr_ref,
                      d_ref, fwd_ref, g_ref):
    hp = lax.Precision.HIGHEST
    h = jnp.sin(fr_ref[0:1, :] * (jnp.dot(z_ref[...], w1_ref[...], precision=hp) + b1_ref[...]))
    h = jnp.sin(fr_ref[1:2, :] * (jnp.dot(h, w2_ref[...], precision=hp) + b2_ref[...]))
    h = jnp.dot(h, w3_ref[...], precision=hp)
    decay = jnp.exp(-t_ref[...] * dl_ref[...])
    row0 = lax.broadcasted_iota(jnp.int32, (seq_len, D_B), 0) == 0
    sums, diffs = [], []
    for n in range(HY_ORDER):
        hf = h[:, (2 * n) * D_B:(2 * n + 1) * D_B] * decay
        hb = h[:, (2 * n + 1) * D_B:(2 * n + 2) * D_B] * decay
        gp = jnp.where(row0, hf + hb + d_ref[n:n + 1, :], hf)
        gm = jnp.where(row0, 0.0, hb)
        sums.append(gp + gm)
        diffs.append(gp - gm)
    rhs = jnp.concatenate(sums + diffs, axis=1).astype(BF16)
    spec = _dot(fwd_ref[...], rhs)
    for n in range(HY_ORDER):
        a = spec[:, n * D_B:(n + 1) * D_B]
        b = spec[:, (HY_ORDER + n) * D_B:(HY_ORDER + n + 1) * D_B]
        g_ref[n, 0:seq_len, :] = a[0:seq_len]
        g_ref[n, seq_len:, :] = jnp.where(row0, a[seq_len:], b[seq_len:])


def hyena_spectrum(seq_len, w1, b1, w2, b2, w3, freq, d, fwd):
    z, t, deltas = _hyena_feats(seq_len)
    return pl.pallas_call(
        functools.partial(_hy_filter_kernel, seq_len),
        out_shape=jax.ShapeDtypeStruct((HY_ORDER, 2 * seq_len, D_B), F32),
        compiler_params=pltpu.CompilerParams(vmem_limit_bytes=VMEM_LIMIT),
        name="hyena_spectrum",
    )(jnp.asarray(z), jnp.asarray(t), jnp.asarray(deltas), w1, b1.reshape(1, -1), w2, b2.reshape(1, -1),
      w3, freq, d, fwd)


def _hyena_kernel(seq_len, n_seq, u_ref, sw_ref, sb_ref, g_ref, fwd_ref, inv_ref, o_ref):
    t_idx = lax.broadcasted_iota(jnp.int32, (seq_len, u_ref.shape[1]), 0)
    row0 = lax.broadcasted_iota(jnp.int32, (seq_len, D_B), 0) == 0
    for s in range(n_seq):
        rows = slice(s * seq_len, (s + 1) * seq_len)
        u = u_ref[rows, :]
        prev = jnp.where(t_idx == 0, 0.0, pltpu.roll(u, 1, axis=0))
        nxt = jnp.where(t_idx == seq_len - 1, 0.0, pltpu.roll(u, seq_len - 1, axis=0))
        u = prev * sw_ref[0:1, :] + u * sw_ref[1:2, :] + nxt * sw_ref[2:3, :] + sb_ref[...]
        z = u[:, 0:D_B]
        for n in range(HY_ORDER):
            spec = _dot(fwd_ref[...], z.astype(BF16))
            ure, uim = spec[0:seq_len], spec[seq_len:]
            gre, gim = g_ref[n, 0:seq_len, :], g_ref[n, seq_len:, :]
            pim = uim * gim
            yre = ure * gre - jnp.where(row0, 0.0, pim)
            yim = jnp.where(row0, pim, ure * gim + uim * gre)
            y = jnp.concatenate([yre, yim], axis=0).astype(BF16)
            z = u[:, (n + 1) * D_B:(n + 2) * D_B] * _dot(inv_ref[...], y)
        o_ref[rows, :] = z.astype(o_ref.dtype)


def hyena_mixer(u, nseq, seq_len, short_w, short_b, spectrum, fwd, inv):
    t = u.shape[0]
    width = (HY_ORDER + 1) * D_B
    col_block = (3 * D_A) // width
    assert col_block * width == 3 * D_A
    per_step = max(1, HY_STEP_ROWS // seq_len)
    assert nseq % per_step == 0
    return pl.pallas_call(
        functools.partial(_hyena_kernel, seq_len, per_step),
        grid=(nseq // per_step,),
        in_specs=[pl.BlockSpec((per_step * seq_len, width), lambda b: (b, col_block)),
                  pl.BlockSpec(short_w.shape, lambda b: (0, 0)),
                  pl.BlockSpec((1, width), lambda b: (0, 0)),
                  pl.BlockSpec(spectrum.shape, lambda b: (0, 0, 0)),
                  pl.BlockSpec(fwd.shape, lambda b: (0, 0)),
                  pl.BlockSpec(inv.shape, lambda b: (0, 0))],
        out_specs=pl.BlockSpec((per_step * seq_len, D_B), lambda b: (b, 0)),
        out_shape=jax.ShapeDtypeStruct((t, D_B), BF16),
        compiler_params=_cparams("arbitrary"),
        name="hyena_mixer",
    )(u, short_w, short_b.reshape(1, width), spectrum, fwd, inv)


RG_CB = LANES
RG_CHUNK = 512
TM_ROWS = 512


def _rglru_kernel(nseq, seq_len, gate_ref, xr_ref, cw_ref, cb_ref, wg_ref, bg_ref, lam_ref, h0_ref,
                  y_ref, fin_ref, xp_ref, a_f, b_f, a_b, b_b):
    t_tot = nseq * seq_len
    c = RG_CB
    pad = 2 * nseq
    xp_ref[0:pad, :] = jnp.zeros((pad, c), F32)
    xp_ref[pad + t_tot:, :] = jnp.zeros((pad, c), F32)
    xp_ref[pad:pad + t_tot, :] = xr_ref[...]
    nl = -lam_ref[...]
    sp = jnp.maximum(nl, 0.0) + jnp.log1p(jnp.exp(-jnp.abs(nl)))
    k2 = (-0.5 * RG_C * math.log2(math.e)) * sp

    def gate_chunk(ci, carry):
        r0 = pl.multiple_of(ci * RG_CHUNK, RG_CHUNK)
        xc = xp_ref[pl.ds(r0, RG_CHUNK), :] * cw_ref[0:1, :]
        for j in range(1, cw_ref.shape[0]):
            xc = xc + xp_ref[pl.ds(r0 + j * nseq, RG_CHUNK), :] * cw_ref[j:j + 1, :]
        xc = xc + cb_ref[...]
        gts = _dot(xc.astype(BF16), wg_ref[0]) + bg_ref[...]
        x_half = 0.5 * xc
        for d, (a_ref, b_ref) in enumerate(((a_f, b_f), (a_b, b_b))):
            t_r = jnp.tanh(gts[:, (2 * d) * c:(2 * d + 1) * c])
            t_i = jnp.tanh(gts[:, (2 * d + 1) * c:(2 * d + 2) * c])
            a = jnp.exp2(t_r * k2[d:d + 1, :] + k2[d:d + 1, :])
            a_ref[pl.ds(r0, RG_CHUNK), :] = a
            y = 1.0 - a * a
            root = jnp.where(y > 0.0, y * lax.rsqrt(y), 0.0)
            b_ref[pl.ds(r0, RG_CHUNK), :] = root * ((t_i + 1.0) * x_half)
        return carry

    lax.fori_loop(0, t_tot // RG_CHUNK, gate_chunk, 0)

    def scan_step(t, carry):
        hf, hb = carry
        rows_f = pl.ds(pl.multiple_of(t * nseq, nseq), nseq)
        rows_b = pl.ds(pl.multiple_of((seq_len - 1 - t) * nseq, nseq), nseq)
        hf = a_f[rows_f, :] * hf + b_f[rows_f, :]
        hb = a_b[rows_b, :] * hb + b_b[rows_b, :]
        b_f[rows_f, :] = hf
        b_b[rows_b, :] = hb
        return hf, hb

    hf, hb = lax.fori_loop(0, seq_len, scan_step, (h0_ref[0], h0_ref[1]), unroll=8)
    fin_ref[0] = hf
    fin_ref[1] = hb

    def out_chunk(ci, carry):
        rs = pl.ds(pl.multiple_of(ci * RG_CHUNK, RG_CHUNK), RG_CHUNK)
        y_ref[rs, :] = ((b_f[rs, :] + b_b[rs, :]) * jax.nn.gelu(gate_ref[rs, :])).astype(y_ref.dtype)
        return carry

    lax.fori_loop(0, t_tot // RG_CHUNK, out_chunk, 0)


def _rg_gate_weights(wa, wx):
    per_step = RG_CB // RG_BLOCK
    steps = D_RNN // RG_CB
    mats = []
    for d in range(2):
        for w in (wa[d], wx[d]):
            w = w.reshape(steps, per_step, RG_BLOCK, RG_BLOCK)
            eye = jnp.eye(per_step, dtype=w.dtype)
            m = jnp.einsum('spde,pq->spdqe', w, eye).reshape(steps, RG_CB, RG_CB)
            mats.append(m)
    return (0.5 * jnp.concatenate(mats, axis=-1)).astype(BF16)


def rglru_block(u, nseq, seq_len, conv_w, conv_b, wa, ba, wx, bx, lam, h0):
    t = u.shape[0]
    c = RG_CB
    steps = D_RNN // c
    wg = _rg_gate_weights(wa, wx)
    bg = jnp.stack([ba[0], bx[0], ba[1], bx[1]], axis=0).reshape(4, steps, c)
    bg = 0.5 * bg.transpose(1, 0, 2).reshape(steps, 1, 4 * c)
    y, fin = pl.pallas_call(
        functools.partial(_rglru_kernel, nseq, seq_len),
        grid=(steps,),
        in_specs=[pl.BlockSpec((t, c), lambda j: (0, j)),
                  pl.BlockSpec((t, c), lambda j: (0, steps + j)),
                  pl.BlockSpec((conv_w.shape[0], c), lambda j: (0, j)),
                  pl.BlockSpec((1, c), lambda j: (0, j)),
                  pl.BlockSpec((1, c, 4 * c), lambda j: (j, 0, 0)),
                  pl.BlockSpec((None, 1, 4 * c), lambda j: (j, 0, 0)),
                  pl.BlockSpec((2, c), lambda j: (0, j)),
                  pl.BlockSpec((2, nseq, c), lambda j: (0, 0, j))],
        out_specs=[pl.BlockSpec((t, c), lambda j: (0, j)),
                   pl.BlockSpec((2, nseq, c), lambda j: (0, 0, j))],
        out_shape=[jax.ShapeDtypeStruct((t, D_RNN), BF16),
                   jax.ShapeDtypeStruct((2, nseq, D_RNN), F32)],
        scratch_shapes=[pltpu.VMEM((t + 4 * nseq, c), F32)] + [pltpu.VMEM((t, c), F32)] * 4,
        compiler_params=_cparams("arbitrary"),
        name="rglru_block",
    )(u, u, conv_w, conv_b.reshape(1, -1), wg, bg, lam, h0)
    return y, fin


def _route_record(h, w, rb):
    h_hi = h.astype(BF16)
    h_lo = (h - h_hi.astype(F32)).astype(BF16)
    w_hi = w.astype(BF16)
    w_lo = (w - w_hi.astype(F32)).astype(BF16)
    logits = _dot_nt(w_hi, h_hi) + (_dot_nt(w_lo, h_hi) + _dot_nt(w_hi, h_lo))
    scores = jax.nn.sigmoid(logits)
    sel = scores + rb
    row = [sel[e:e + 1, :] for e in range(N_EXPERTS)]
    gs = []
    for g in range(N_GROUPS):
        r = row[g * EXPERTS_PER_GROUP:(g + 1) * EXPERTS_PER_GROUP]
        best_pair = None
        for i in range(EXPERTS_PER_GROUP):
            for j in range(i + 1, EXPERTS_PER_GROUP):
                s = r[i] + r[j]
                best_pair = s if best_pair is None else jnp.maximum(best_pair, s)
        gs.append(best_pair)
    best = jnp.zeros_like(gs[0], dtype=jnp.int32)
    top = gs[0]
    for g in range(1, N_GROUPS):
        better = gs[g] > top
        best = jnp.where(better, g, best)
        top = jnp.where(better, gs[g], top)
    picked = []
    for e in range(N_EXPERTS):
        g = e // EXPERTS_PER_GROUP
        rank = jnp.zeros_like(best)
        for o in range(g * EXPERTS_PER_GROUP, (g + 1) * EXPERTS_PER_GROUP):
            if o == e:
                continue
            ahead = (row[o] > row[e]) | ((row[o] == row[e]) & (o < e))
            rank = rank + ahead.astype(jnp.int32)
        picked.append((best == g) & (rank < 2))
    den = jnp.zeros_like(gs[0])
    for e in range(N_EXPERTS):
        den = den + jnp.where(picked[e], scores[e:e + 1, :], 0.0)
    gate = [jnp.where(picked[e], scores[e:e + 1, :] / den, 0.0) for e in range(N_EXPERTS)]
    cls = jnp.zeros_like(den)
    w_a = jnp.zeros_like(den)
    w_b = jnp.zeros_like(den)
    for g in range(N_GROUPS):
        for pi, (a, b) in enumerate(MOE_PAIRS):
            ea, eb = g * EXPERTS_PER_GROUP + a, g * EXPERTS_PER_GROUP + b
            both = picked[ea] & picked[eb]
            cls = jnp.where(both, float(g * len(MOE_PAIRS) + pi), cls)
            w_a = jnp.where(both, gate[ea], w_a)
            w_b = jnp.where(both, gate[eb], w_b)
    return jnp.concatenate([cls, w_a, w_b, jnp.zeros((ROUTE_ROWS - 3, cls.shape[1]), F32)], axis=0)


MOE_PAIRS = ((0, 1), (0, 2), (0, 3), (1, 3), (1, 2), (2, 3))
N_CLS = N_GROUPS * len(MOE_PAIRS)
CLS_PAD = 32
ROUTE_ROWS = 8
MOE_TS = 256
MOE_TM = 256
MOE_STEP_TILES = 4
SLOT_BLK = 512
ROW_W = D_MODEL + LANES


def _slots_kernel(n_blk, route_ref, slot_ref, off_ref, cnt_ref):
    cid = lax.broadcasted_iota(jnp.int32, (CLS_PAD, SLOT_BLK), 0).astype(F32)

    def members(j):
        cls = route_ref[0:1, pl.ds(pl.multiple_of(j * SLOT_BLK, SLOT_BLK), SLOT_BLK)]
        return (cid == cls).astype(F32)

    def count(j, cnt):
        return cnt + jnp.sum(members(j), axis=1, keepdims=True)

    cnt = lax.fori_loop(0, n_blk, count, jnp.zeros((CLS_PAD, 1), F32))
    cnt = jnp.broadcast_to(cnt, (CLS_PAD, LANES))
    padded = jnp.ceil(cnt * (1.0 / MOE_TS)) * MOE_TS
    r = lax.broadcasted_iota(jnp.int32, (CLS_PAD, CLS_PAD), 0)
    c = lax.broadcasted_iota(jnp.int32, (CLS_PAD, CLS_PAD), 1)
    off = jnp.dot((c < r).astype(F32), padded, precision=lax.Precision.HIGHEST, preferred_element_type=F32)
    off_ref[...] = off
    cnt_ref[...] = cnt
    tr = lax.broadcasted_iota(jnp.int32, (SLOT_BLK, SLOT_BLK), 0)
    tc = lax.broadcasted_iota(jnp.int32, (SLOT_BLK, SLOT_BLK), 1)
    earlier = (tr < tc).astype(BF16)

    def assign(j, base):
        member = members(j)
        rank = _dot(member.astype(BF16), earlier)
        slot = jnp.sum(member * (rank + base), axis=0, keepdims=True)
        slot_ref[0:1, pl.ds(pl.multiple_of(j * SLOT_BLK, SLOT_BLK), SLOT_BLK)] = slot.astype(jnp.int32)
        return base + jnp.sum(member, axis=1, keepdims=True)

    lax.fori_loop(0, n_blk, assign, off[:, 0:1])


def moe_slots(route):
    t = route.shape[1]
    stat = jax.ShapeDtypeStruct((CLS_PAD, LANES), F32)
    return pl.pallas_call(
        functools.partial(_slots_kernel, t // SLOT_BLK),
        out_shape=[jax.ShapeDtypeStruct((1, t), jnp.int32), stat, stat],
        compiler_params=pltpu.CompilerParams(vmem_limit_bytes=VMEM_LIMIT),
        name="moe_slots",
    )(route)


def _tile_maps(off, cnt, n_tiles):
    off = off[:N_CLS, 0].astype(jnp.int32)
    cnt = cnt[:N_CLS, 0].astype(jnp.int32)
    ends = off + ((cnt + MOE_TS - 1) // MOE_TS) * MOE_TS
    n_used = ends[-1] // MOE_TS
    k = jnp.arange(n_tiles, dtype=jnp.int32)
    tix = jnp.minimum(k, n_used - 1)
    cls = jnp.sum((tix[:, None] * MOE_TS >= ends[None, :]).astype(jnp.int32), axis=1)
    pair = jnp.asarray(MOE_PAIRS, jnp.int32)
    grp = (cls // len(MOE_PAIRS)) * EXPERTS_PER_GROUP
    ea = grp + pair[cls % len(MOE_PAIRS), 0]
    eb = grp + pair[cls % len(MOE_PAIRS), 1]
    n = jnp.int32(n_tiles)

    def slot_plan(e):
        chg = jnp.concatenate([jnp.ones((1,), jnp.int32), (e[1:] != e[:-1]).astype(jnp.int32)])
        at = jnp.where(chg == 1, k, n)
        nxt_at = jnp.concatenate([lax.cummin(at[::-1])[::-1][1:], n.reshape(1)])
        more = (nxt_at < n).astype(jnp.int32)
        nxt = e[jnp.minimum(nxt_at, n - 1)]
        par = (jnp.cumsum(chg) - 1) % 2
        return chg, nxt, more, par.astype(jnp.int32)

    plan_a, plan_b = slot_plan(ea), slot_plan(eb)
    chg, nxt, more, par = (jnp.stack([pa, pb]) for pa, pb in zip(plan_a, plan_b))
    return ea, eb, chg, nxt, more, par, n_used.reshape(1)


def _dispatch_kernel(n_steps, slots_ref, x_ref, g_ref, sc_ref, sh_ref, rt_ref, hs_in, hs_out, rowbuf, sem):
    del hs_in
    i = pl.program_id(0)
    cur = i % 2

    def wait_rows(s):
        pltpu.make_async_copy(rowbuf.at[s], rowbuf.at[s], sem.at[s]).wait()

    @pl.when(i >= 2)
    def _():
        wait_rows(cur)

    rowbuf[cur, :, 0:D_MODEL] = _normmod(x_ref[...], g_ref[...], sc_ref[0], sh_ref[0])
    rowbuf[cur, :, D_MODEL:ROW_W] = jnp.concatenate(
        [rt_ref[...], jnp.zeros((MOE_TM, LANES - ROUTE_ROWS), F32)], axis=1)

    for s in range(2):
        @pl.when(cur == s)
        def _():
            for r in range(MOE_TM):
                dst = slots_ref[i * MOE_TM + r]
                pltpu.make_async_copy(rowbuf.at[s, r], hs_out.at[dst], sem.at[s]).start()

    @pl.when(i == n_steps - 1)
    def _():
        wait_rows(cur)
        if n_steps >= 2:
            wait_rows(1 - cur)


def moe_dispatch(x, g, mod, route_t, slots, hs, seq_len, per_seq):
    t, d = x.shape
    n_steps = t // MOE_TM
    grid_spec = pltpu.PrefetchScalarGridSpec(
        num_scalar_prefetch=1,
        grid=(n_steps,),
        in_specs=[pl.BlockSpec((MOE_TM, d), lambda i, s: (i, 0)),
                  pl.BlockSpec((1, d), lambda i, s: (0, 0)),
                  _mod_spec(4, MOE_TM, seq_len, per_seq),
                  _mod_spec(3, MOE_TM, seq_len, per_seq),
                  pl.BlockSpec((MOE_TM, ROUTE_ROWS), lambda i, s: (i, 0)),
                  pl.BlockSpec(memory_space=pl.ANY)],
        out_specs=pl.BlockSpec(memory_space=pl.ANY),
        scratch_shapes=[pltpu.VMEM((2, MOE_TM, ROW_W), F32), pltpu.SemaphoreType.DMA((2,))],
    )
    return pl.pallas_call(
        functools.partial(_dispatch_kernel, n_steps),
        grid_spec=grid_spec,
        out_shape=jax.ShapeDtypeStruct(hs.shape, F32),
        input_output_aliases={6: 0},
        compiler_params=_cparams("arbitrary"),
        name="moe_dispatch",
    )(slots, x, g.reshape(1, d), mod, mod, route_t, hs)


def _experts_kernel(layer, ea_ref, eb_ref, chg_ref, nxt_ref, more_ref, par_ref, nused_ref,
                    hs_ref, wg_hbm, wu_hbm, wd_hbm, ys_ref, fg, fu, fd, bg, bu, bd, sem):
    def weight_copies(slot, expert, par):
        return [pltpu.make_async_copy(src.at[layer, expert], dst.at[slot, par], sem.at[slot, par])
                for src, dst in ((wg_hbm, fg), (wu_hbm, fu), (wd_hbm, fd))]

    def tile(k, rows):
        @pl.when(k < nused_ref[0])
        def _():
            for slot, e_ref in enumerate((ea_ref, eb_ref)):
                @pl.when(chg_ref[slot, k] == 1)
                def _():
                    par = par_ref[slot, k]

                    @pl.when(k == 0)
                    def _():
                        for cp in weight_copies(slot, e_ref[0], par):
                            cp.start()

                    for cp in weight_copies(slot, e_ref[k], par):
                        cp.wait()
                    bg[slot] = fg[slot, par].astype(BF16)
                    bu[slot] = fu[slot, par].astype(BF16)
                    bd[slot] = fd[slot, par].astype(BF16)

                    @pl.when(more_ref[slot, k] == 1)
                    def _():
                        for cp in weight_copies(slot, nxt_ref[slot, k], 1 - par):
                            cp.start()

            h = hs_ref[rows, 0:D_MODEL].astype(BF16)

            def ffn(slot):
                hid = _dot(h, bg[slot])
                up = _dot(h, bu[slot])
                w = hs_ref[rows, D_MODEL + 1 + slot:D_MODEL + 2 + slot]
                act = (hid * _sigmoid(hid)) * up * w
                return _dot(act.astype(BF16), bd[slot])

            ys_ref[rows, :] = ffn(0) + ffn(1)

        @pl.when(k >= nused_ref[0])
        def _():
            ys_ref[rows, :] = jnp.zeros((MOE_TS, ys_ref.shape[1]), ys_ref.dtype)

    for j in range(MOE_STEP_TILES):
        tile(pl.program_id(0) * MOE_STEP_TILES + j, slice(j * MOE_TS, (j + 1) * MOE_TS))


def moe_experts(hs, maps, layer, w_gate, w_up, w_down):
    n_tiles = hs.shape[0] // MOE_TS
    d = D_MODEL

    grid_spec = pltpu.PrefetchScalarGridSpec(
        num_scalar_prefetch=7,
        grid=(n_tiles // MOE_STEP_TILES,),
        in_specs=[pl.BlockSpec((MOE_STEP_TILES * MOE_TS, ROW_W), lambda k, *_: (k, 0)),
                  pl.BlockSpec(memory_space=pl.ANY), pl.BlockSpec(memory_space=pl.ANY),
                  pl.BlockSpec(memory_space=pl.ANY)],
        out_specs=pl.BlockSpec((MOE_STEP_TILES * MOE_TS, d), lambda k, *_: (k, 0)),
        scratch_shapes=[pltpu.VMEM((2, 2, d, D_EXPERT), F32), pltpu.VMEM((2, 2, d, D_EXPERT), F32),
                        pltpu.VMEM((2, 2, D_EXPERT, d), F32),
                        pltpu.VMEM((2, d, D_EXPERT), BF16), pltpu.VMEM((2, d, D_EXPERT), BF16),
                        pltpu.VMEM((2, D_EXPERT, d), BF16),
                        pltpu.SemaphoreType.DMA((2, 2))],
    )
    return pl.pallas_call(
        functools.partial(_experts_kernel, layer),
        grid_spec=grid_spec,
        out_shape=jax.ShapeDtypeStruct((hs.shape[0], d), F32),
        compiler_params=_cparams("arbitrary"),
        name="moe_experts",
    )(*maps, hs, w_gate, w_up, w_down)


def _combine_kernel(final, n_steps, slots_ref, x_ref, g2_ref, fg_ref, ys_hbm, o_ref, gbuf, sem):
    i = pl.program_id(0)
    cur = i % 2

    def issue_tile(tile, s):
        for r in range(MOE_TM):
            src = slots_ref[tile * MOE_TM + r]
            pltpu.make_async_copy(ys_hbm.at[src], gbuf.at[s, r], sem.at[s]).start()

    @pl.when(i == 0)
    def _():
        issue_tile(0, 0)

    for s in range(2):
        @pl.when((i + 1 < n_steps) & (1 - cur == s))
        def _():
            issue_tile(i + 1, s)

    pltpu.make_async_copy(gbuf.at[cur], gbuf.at[cur], sem.at[cur]).wait()
    y = x_ref[...] + g2_ref[0] * gbuf[cur]
    if final:
        ms = jnp.mean(y * y, axis=-1, keepdims=True)
        y = y * lax.rsqrt(ms + EPS) * fg_ref[...]
    o_ref[...] = y


def moe_combine(x, mod, slots, ys, final_g, final, seq_len, per_seq):
    t, d = x.shape
    n_steps = t // MOE_TM
    grid_spec = pltpu.PrefetchScalarGridSpec(
        num_scalar_prefetch=1,
        grid=(n_steps,),
        in_specs=[pl.BlockSpec((MOE_TM, d), lambda i, s: (i, 0)),
                  _mod_spec(5, MOE_TM, seq_len, per_seq),
                  pl.BlockSpec((1, d), lambda i, s: (0, 0)),
                  pl.BlockSpec(memory_space=pl.ANY)],
        out_specs=pl.BlockSpec((MOE_TM, d), lambda i, s: (i, 0)),
        scratch_shapes=[pltpu.VMEM((2, MOE_TM, d), F32), pltpu.SemaphoreType.DMA((2,))],
    )
    return pl.pallas_call(
        functools.partial(_combine_kernel, final, n_steps),
        grid_spec=grid_spec,
        out_shape=jax.ShapeDtypeStruct((t, d), F32),
        compiler_params=_cparams("arbitrary"),
        name="moe_combine",
    )(slots, x, mod, final_g.reshape(1, d), ys)


SC_WINDOW = 128
SC_COLS = 256


def sc_gather_rows(table, idx):
    n = idx.shape[0]
    d = table.shape[1]
    mesh = plsc.VectorSubcoreMesh(core_axis_name="core", subcore_axis_name="subcore")

    @pl.kernel(out_type=jax.ShapeDtypeStruct((n, d), table.dtype), mesh=mesh)
    def gather(x_hbm, i_hbm, o_hbm):
        for j in range(d // SC_COLS):
            def body(i_vmem, o_vmem, j=j):
                pltpu.sync_copy(x_hbm.at[i_vmem.at[0], pl.ds(j * SC_COLS, SC_COLS)], o_vmem)

            per_core = n // SC_WINDOW // mesh.num_cores
            pltpu.emit_pipeline(
                body,
                grid=(mesh.num_cores, per_core),
                in_specs=[pl.BlockSpec((1, SC_WINDOW), lambda c, i: (0, c * per_core + i))],
                out_specs=[pl.BlockSpec((SC_WINDOW, SC_COLS), lambda c, i, j=j: (c * per_core + i, j))],
                core_axis_name=("core", "subcore"),
                dimension_semantics=(pltpu.PARALLEL, pltpu.PARALLEL),
            )(i_hbm, o_hbm)

    return gather(table, idx.reshape(1, n))


def _residual_norm_kernel(final, x_ref, y_ref, g2_ref, fg_ref, o_ref):
    y = x_ref[...] + g2_ref[0] * y_ref[...]
    if final:
        ms = jnp.mean(y * y, axis=-1, keepdims=True)
        y = y * lax.rsqrt(ms + EPS) * fg_ref[...]
    o_ref[...] = y


def moe_combine_sc(x, mod, slots, ys, final_g, final, seq_len, per_seq, tm=512):
    t, d = x.shape
    y = sc_gather_rows(ys, slots)
    return pl.pallas_call(
        functools.partial(_residual_norm_kernel, final),
        grid=(t // tm,),
        in_specs=[pl.BlockSpec((tm, d), lambda i: (i, 0)),
                  pl.BlockSpec((tm, d), lambda i: (i, 0)),
                  _mod_spec(5, tm, seq_len, per_seq),
                  pl.BlockSpec((1, d), lambda i: (0, 0))],
        out_specs=pl.BlockSpec((tm, d), lambda i: (i, 0)),
        out_shape=jax.ShapeDtypeStruct((t, d), F32),
        compiler_params=_cparams("arbitrary"),
        name="moe_residual_norm",
    )(x, y, mod, final_g.reshape(1, d))


def _combine_nm_tm_kernel(nseq, steps, seq_len, n_steps, slots_ref, x_ref, g2_ref, ys_hbm, g_ref, sc_ref,
                          sh_ref, w_ref, x2_ref, u_ref, gbuf, sem):
    i = pl.program_id(0)
    cur = i % 2

    def issue_tile(tile, s):
        for sq in range(nseq):
            for t in range(steps):
                src = slots_ref[sq * seq_len + tile * steps + t]
                pltpu.make_async_copy(ys_hbm.at[src], gbuf.at[s, sq * steps + t], sem.at[s]).start()

    @pl.when(i == 0)
    def _():
        issue_tile(0, 0)

    for s in range(2):
        @pl.when((i + 1 < n_steps) & (1 - cur == s))
        def _():
            issue_tile(i + 1, s)

    pltpu.make_async_copy(gbuf.at[cur], gbuf.at[cur], sem.at[cur]).wait()
    d = x_ref.shape[-1]
    y = x_ref[...] + g2_ref[...] * gbuf[cur].reshape(nseq, steps, d)
    x2_ref[...] = y
    h = _normmod(y, g_ref[...], sc_ref[...], sh_ref[...])
    h = h.reshape(nseq * steps, d).astype(BF16)
    h = _dot(_row_permutation(nseq, steps, True), h).astype(BF16)
    u_ref[...] = _dot(h, w_ref[...])


def combine_nm_matmul_tm(x, mod_prev, slots, ys, g, mod, w, nseq, seq_len, per_seq):
    t, d = x.shape
    n = w.shape[1]
    steps = TM_ROWS // nseq
    n_steps = seq_len // steps
    g2 = _group_mod(mod_prev, 5, nseq, per_seq)
    sc = _group_mod(mod, 1, nseq, per_seq)
    sh = _group_mod(mod, 0, nseq, per_seq)
    mod_spec = pl.BlockSpec(sc.shape, lambda i, s: (0, 0, 0))
    x_spec = pl.BlockSpec((nseq, steps, d), lambda i, s: (0, i, 0))
    grid_spec = pltpu.PrefetchScalarGridSpec(
        num_scalar_prefetch=1,
        grid=(n_steps,),
        in_specs=[x_spec, mod_spec, pl.BlockSpec(memory_space=pl.ANY),
                  pl.BlockSpec((1, d), lambda i, s: (0, 0)), mod_spec, mod_spec,
                  pl.BlockSpec((d, n), lambda i, s: (0, 0))],
        out_specs=[x_spec, pl.BlockSpec((TM_ROWS, n), lambda i, s: (i, 0))],
        scratch_shapes=[pltpu.VMEM((2, TM_ROWS, d), F32), pltpu.SemaphoreType.DMA((2,))],
    )
    x2, u = pl.pallas_call(
        functools.partial(_combine_nm_tm_kernel, nseq, steps, seq_len, n_steps),
        grid_spec=grid_spec,
        out_shape=[jax.ShapeDtypeStruct((nseq, seq_len, d), F32), jax.ShapeDtypeStruct((t, n), F32)],
        compiler_params=_cparams("arbitrary"),
        name="moe_combine_norm_mod_proj_tm",
    )(slots, x.reshape(nseq, seq_len, d), g2, ys, g.reshape(1, d), sc, sh, w)
    return x2.reshape(t, d), u


def moe_block(xs, routes, mods_l, per_seqs, seq_lens, layer, p, final, defer_combine, hs):
    g = p['norm_g'][layer, 1]
    slots, off, cnt = moe_slots(jnp.concatenate(routes, axis=1))
    t_all = slots.shape[1]
    n_tiles = -(-(t_all // MOE_TS + N_CLS) // MOE_STEP_TILES) * MOE_STEP_TILES
    maps = _tile_maps(off, cnt, n_tiles)
    if hs is None:
        hs = jnp.zeros((n_tiles * MOE_TS, ROW_W), F32)
    bounds = np.cumsum([0] + [x.shape[0] for x in xs])
    group_slots = [slots[0, bounds[i]:bounds[i + 1]] for i in range(len(xs))]
    for x, r, s, sl, ps in zip(xs, routes, group_slots, seq_lens, per_seqs):
        hs = moe_dispatch(x, g, mods_l, r.T, s, hs, sl, ps)
    ys = moe_experts(hs, maps, layer, p['moe_w_gate'], p['moe_w_up'], p['moe_w_down'])
    if defer_combine:
        return [(x, s, ys) for x, s in zip(xs, group_slots)], hs
    return [(moe_combine_sc if i == 0 else moe_combine)(x, mods_l, s, ys, p['final_g'], final, sl, ps)
            for i, (x, s, sl, ps) in enumerate(zip(xs, group_slots, seq_lens, per_seqs))], hs


def _mixer(x, group, l, mod, p, hy_tables, mod_prev):
    per_seq, nseq, seq_len = group['per_seq'], group['nseq'], group['seq_len']
    extras = None
    deferred = isinstance(x, tuple)
    if deferred and l % 2 == 0:
        x = moe_combine(x[0], mod_prev, x[1], x[2], p['final_g'], False, seq_len, per_seq)
        deferred = False
    if l % 2 == 0:
        e = l // 2
        u = nm_matmul(x, p['norm_g'][l, 0], mod, p['a_in_w'][e].astype(BF16), seq_len, per_seq)
        if group['ctx_k'] is None:
            attn, nk, nv = context_attention(u, nseq, seq_len)
            extras = (nk, nv)
        else:
            attn = neighbourhood_attention(u, group['ctx_k'][:, e], group['ctx_v'][:, e], p['na_rpb'][e],
                                           nseq, seq_len)
        fwd, inv = hy_tables[seq_len]
        spectrum = hyena_spectrum(seq_len, p['hy_w1'][e], p['hy_b1'][e], p['hy_w2'][e], p['hy_b2'][e],
                                  p['hy_w3'][e], p['hy_freq'][e], p['hy_d'][e], fwd)
        hy = hyena_mixer(u, nseq, seq_len, p['hy_short_w'][e], p['hy_short_b'][e], spectrum, fwd, inv)
        w_out = p['a_out_w'][e].astype(BF16)
        x, route = proj_residual([attn, hy], [w_out[:D_A], w_out[D_A:]], x, mod, p['norm_g'][l, 1],
                                 p['router_w'], p['router_b'], seq_len, per_seq)
    else:
        o = l // 2
        w_in = p['c_in_w'][o].astype(BF16)
        if deferred:
            x, u = combine_nm_matmul_tm(x[0], mod_prev, x[1], x[2], p['norm_g'][l, 0], mod, w_in,
                                        nseq, seq_len, per_seq)
        else:
            u = nm_matmul_tm(x, p['norm_g'][l, 0], mod, w_in, nseq, seq_len, per_seq)
        y, extras = rglru_block(u, nseq, seq_len, p['rg_conv_w'][o], p['rg_conv_b'][o], p['rg_wa'][o],
                                p['rg_ba'][o], p['rg_wx'][o], p['rg_bx'][o], p['rg_lam'][o], group['h0'][o])
        x, route = proj_residual_tm(y, p['c_out_w'][o].astype(BF16), x, mod, p['norm_g'][l, 1],
                                    p['router_w'], p['router_b'], nseq, seq_len, per_seq)
    return x, extras, route


def kernel(x_prompt, x_sample, cache_k, cache_v, state_h, c, c_ctx, norm_g, ada_w, ada_b, final_g, a_in_w, a_out_w, na_rpb, hy_short_w, hy_short_b, hy_w1, hy_b1, hy_w2, hy_b2, hy_w3, hy_freq, hy_d, c_in_w, c_out_w, rg_conv_w, rg_conv_b, rg_wa, rg_ba, rg_wx, rg_bx, rg_lam, router_w, router_b, moe_w_gate, moe_w_up, moe_w_down):
    p = dict(norm_g=norm_g, final_g=final_g, a_in_w=a_in_w, a_out_w=a_out_w, na_rpb=na_rpb,
             hy_short_w=hy_short_w, hy_short_b=hy_short_b, hy_w1=hy_w1, hy_b1=hy_b1, hy_w2=hy_w2,
             hy_b2=hy_b2, hy_w3=hy_w3, hy_freq=hy_freq, hy_d=hy_d, c_in_w=c_in_w, c_out_w=c_out_w,
             rg_conv_w=rg_conv_w, rg_conv_b=rg_conv_b, rg_wa=rg_wa, rg_ba=rg_ba, rg_wx=rg_wx, rg_bx=rg_bx,
             rg_lam=rg_lam, router_w=router_w, router_b=router_b, moe_w_gate=moe_w_gate,
             moe_w_up=moe_w_up, moe_w_down=moe_w_down)
    batch, seq, d = x_prompt.shape
    dec_batch, dec_seq, _ = x_sample.shape
    n_odd = DEPTH // 2
    assert 1 + dec_batch <= MOD_ROWS

    cond = jnp.concatenate([c_ctx[None, :], c, jnp.zeros((MOD_ROWS - 1 - dec_batch, d), F32)], axis=0)
    m = modulation(cond, ada_w, ada_b)
    mods = [m[l].reshape(MOD_ROWS * N_MOD, 1, d) for l in range(DEPTH)]

    tables = {}
    for sl in (seq, dec_seq):
        fwd, inv = _dft_tables(sl)
        tables[sl] = (jnp.asarray(fwd).astype(BF16), jnp.asarray(inv).astype(BF16))

    groups = [
        dict(per_seq=False, nseq=batch, seq_len=seq, ctx_k=None, ctx_v=None,
             h0=[jnp.zeros((2, batch, D_RNN), F32)] * n_odd),
        dict(per_seq=True, nseq=dec_batch, seq_len=dec_seq, ctx_k=cache_k, ctx_v=cache_v,
             h0=[state_h[:, o].transpose(1, 0, 2) for o in range(n_odd)]),
    ]
    xs = [x_prompt.reshape(batch * seq, d), x_sample.reshape(dec_batch * dec_seq, d)]
    k_list, v_list, h_list = [], [], []
    hs = None
    for l in range(DEPTH):
        mixed = [_mixer(x, grp, l, mods[l], p, tables, mods[l - 1] if l else None)
                 for x, grp in zip(xs, groups)]
        if l % 2 == 0:
            k_list.append(mixed[0][1][0])
            v_list.append(mixed[0][1][1])
        else:
            h_list.append(mixed[0][1].transpose(1, 0, 2))
        xs, hs = moe_block([mx[0] for mx in mixed], [mx[2] for mx in mixed], mods[l],
                           [grp['per_seq'] for grp in groups], [grp['seq_len'] for grp in groups],
                           l, p, l == DEPTH - 1, l < DEPTH - 1, hs)
    new_k = jnp.stack(k_list, axis=1)
    new_v = jnp.stack(v_list, axis=1)
    new_h = jnp.stack(h_list, axis=1)
    return (xs[0].reshape(batch, seq, d), xs[1].reshape(dec_batch, dec_seq, d), new_k, new_v, new_h)
```

```python
import functools
import math

import numpy as np
import jax
import jax.numpy as jnp
from jax import lax
from jax.experimental import pallas as pl
from jax.experimental.pallas import tpu as pltpu
from jax.experimental.pallas import tpu_sc as plsc

F32 = jnp.float32
BF16 = jnp.bfloat16

D_MODEL = 1024
DEPTH = 2
GRID_W = 64
EPS = 1e-6
NEG_INF = -1e30
NA_HEADS = 8
HEAD_DIM = 64
D_A = NA_HEADS * HEAD_DIM
WIN_ROWS = 8
WIN_COLS = 16
D_B = D_MODEL - D_A
HY_ORDER = 2
HY_EMB = 33
HY_BANDS = (HY_EMB - 1) // 2
HY_FFN = 64
HY_DECAY_TARGET = 1e-2
HY_FAST_PCT = 0.3
HY_SLOW_PCT = 1.5
D_RNN = D_MODEL
RG_BLOCK = 64
RG_C = 8.0
N_EXPERTS = 16
N_GROUPS = 4
EXPERTS_PER_GROUP = N_EXPERTS // N_GROUPS
D_EXPERT = 512

LANES = 128
VMEM_LIMIT = 56 * 1024 * 1024
N_MOD = 6
MOD_ROWS = 16


def _cparams(*sem):
    return pltpu.CompilerParams(dimension_semantics=sem, vmem_limit_bytes=VMEM_LIMIT)


def _dot(a, b):
    return jnp.dot(a, b, preferred_element_type=F32)


def _dot_nt(a, b):
    return lax.dot_general(a, b, (((1,), (1,)), ((), ())), preferred_element_type=F32)


def _sigmoid(x):
    return 0.5 * jnp.tanh(0.5 * x) + 0.5


def _normmod(x, g, sc, sh):
    ms = jnp.mean(x * x, axis=-1, keepdims=True)
    return (x * lax.rsqrt(ms + EPS) * g) * (1.0 + sc) + sh


def _mod_spec(chunk, tm, seq_len, per_seq):
    if per_seq:
        return pl.BlockSpec((1, 1, D_MODEL), lambda i, *_: ((1 + (i * tm) // seq_len) * N_MOD + chunk, 0, 0))
    return pl.BlockSpec((1, 1, D_MODEL), lambda i, *_: (chunk, 0, 0))


def _mod_kernel(c_ref, w_ref, b_ref, o_ref):
    s = c_ref[...]
    s = s * jax.nn.sigmoid(s)
    o_ref[0] = _dot(s.astype(BF16), w_ref[0].astype(BF16)) + b_ref[0]


def modulation(cond, ada_w, ada_b):
    n = ada_w.shape[-1]
    tn = n // 4
    assert tn % LANES == 0
    return pl.pallas_call(
        _mod_kernel,
        grid=(DEPTH, n // tn),
        in_specs=[pl.BlockSpec((MOD_ROWS, D_MODEL), lambda l, j: (0, 0)),
                  pl.BlockSpec((1, D_MODEL, tn), lambda l, j: (l, 0, j)),
                  pl.BlockSpec((1, 1, tn), lambda l, j: (l, 0, j))],
        out_specs=pl.BlockSpec((1, MOD_ROWS, tn), lambda l, j: (l, 0, j)),
        out_shape=jax.ShapeDtypeStruct((DEPTH, MOD_ROWS, n), F32),
        compiler_params=_cparams("arbitrary", "arbitrary"),
        name="modulation",
    )(cond, ada_w, ada_b.reshape(DEPTH, 1, n))


def _nm_matmul_kernel(x_ref, g_ref, sc_ref, sh_ref, w_ref, o_ref):
    h = _normmod(x_ref[...], g_ref[...], sc_ref[0], sh_ref[0])
    o_ref[...] = _dot(h.astype(BF16), w_ref[...])


def nm_matmul(x, g, mod, w, seq_len, per_seq, tm=512):
    t, d = x.shape
    n = w.shape[1]
    return pl.pallas_call(
        _nm_matmul_kernel,
        grid=(t // tm,),
        in_specs=[pl.BlockSpec((tm, d), lambda i: (i, 0)),
                  pl.BlockSpec((1, d), lambda i: (0, 0)),
                  _mod_spec(1, tm, seq_len, per_seq),
                  _mod_spec(0, tm, seq_len, per_seq),
                  pl.BlockSpec((d, n), lambda i: (0, 0))],
        out_specs=pl.BlockSpec((tm, n), lambda i: (i, 0)),
        out_shape=jax.ShapeDtypeStruct((t, n), F32),
        compiler_params=_cparams("arbitrary"),
        name="norm_mod_proj",
    )(x, g.reshape(1, d), mod, mod, w)


def _proj_res_kernel(n_act, *refs):
    acts = refs[:n_act]
    ws = refs[n_act:2 * n_act]
    x_ref, g_ref, ng_ref, sc2_ref, sh2_ref, rw_ref, rb_ref, o_ref, route_ref = refs[2 * n_act:]
    acc = _dot(acts[0][...].astype(BF16), ws[0][...])
    for a, w in zip(acts[1:], ws[1:]):
        acc += _dot(a[...].astype(BF16), w[...])
    x = x_ref[...] + g_ref[0] * acc
    o_ref[...] = x
    route_ref[...] = _route_record(_normmod(x, ng_ref[...], sc2_ref[0], sh2_ref[0]), rw_ref[...], rb_ref[...])


def proj_residual(acts, ws, x, mod, norm2_g, router_w, router_b, seq_len, per_seq, tm=512):
    t, d = x.shape
    in_specs = [pl.BlockSpec((tm, a.shape[1]), lambda i: (i, 0)) for a in acts]
    in_specs += [pl.BlockSpec(w.shape, lambda i: (0, 0)) for w in ws]
    in_specs += [pl.BlockSpec((tm, d), lambda i: (i, 0)), _mod_spec(2, tm, seq_len, per_seq),
                 pl.BlockSpec((1, d), lambda i: (0, 0)),
                 _mod_spec(4, tm, seq_len, per_seq), _mod_spec(3, tm, seq_len, per_seq),
                 pl.BlockSpec((N_EXPERTS, d), lambda i: (0, 0)),
                 pl.BlockSpec((N_EXPERTS, 1), lambda i: (0, 0))]
    return pl.pallas_call(
        functools.partial(_proj_res_kernel, len(acts)),
        grid=(t // tm,),
        in_specs=in_specs,
        out_specs=[pl.BlockSpec((tm, d), lambda i: (i, 0)), pl.BlockSpec((ROUTE_ROWS, tm), lambda i: (0, i))],
        out_shape=[jax.ShapeDtypeStruct((t, d), F32), jax.ShapeDtypeStruct((ROUTE_ROWS, t), F32)],
        compiler_params=_cparams("arbitrary"),
        name="proj_residual_route",
    )(*acts, *ws, x, mod, norm2_g.reshape(1, d), mod, mod, router_w.T, router_b.reshape(N_EXPERTS, 1))


def _row_permutation(nseq, steps, to_time_major):
    n = nseq * steps
    i = lax.broadcasted_iota(jnp.int32, (n, n), 0)
    j = lax.broadcasted_iota(jnp.int32, (n, n), 1)
    if to_time_major:
        src = (i % nseq) * steps + i // nseq
    else:
        src = (i % steps) * nseq + i // steps
    return (j == src).astype(BF16)


def _nm_matmul_tm_kernel(nseq, steps, x_ref, g_ref, sc_ref, sh_ref, w_ref, o_ref):
    h = _normmod(x_ref[...], g_ref[...], sc_ref[...], sh_ref[...])
    h = h.reshape(nseq * steps, h.shape[-1]).astype(BF16)
    h = _dot(_row_permutation(nseq, steps, True), h).astype(BF16)
    o_ref[...] = _dot(h, w_ref[...])


def _group_mod(mod, chunk, nseq, per_seq):
    rows = mod.reshape(MOD_ROWS, N_MOD, 1, D_MODEL)
    return rows[1:1 + nseq, chunk] if per_seq else rows[0:1, chunk]


def nm_matmul_tm(x, g, mod, w, nseq, seq_len, per_seq):
    t, d = x.shape
    n = w.shape[1]
    steps = TM_ROWS // nseq
    sc = _group_mod(mod, 1, nseq, per_seq)
    sh = _group_mod(mod, 0, nseq, per_seq)
    mod_spec = pl.BlockSpec(sc.shape, lambda i: (0, 0, 0))
    return pl.pallas_call(
        functools.partial(_nm_matmul_tm_kernel, nseq, steps),
        grid=(seq_len // steps,),
        in_specs=[pl.BlockSpec((nseq, steps, d), lambda i: (0, i, 0)),
                  pl.BlockSpec((1, d), lambda i: (0, 0)),
                  mod_spec, mod_spec,
                  pl.BlockSpec((d, n), lambda i: (0, 0))],
        out_specs=pl.BlockSpec((TM_ROWS, n), lambda i: (i, 0)),
        out_shape=jax.ShapeDtypeStruct((t, n), F32),
        compiler_params=_cparams("arbitrary"),
        name="norm_mod_proj_tm",
    )(x.reshape(nseq, seq_len, d), g.reshape(1, d), sc, sh, w)


def _proj_res_tm_kernel(nseq, steps, y_ref, w_ref, x_ref, g_ref, ng_ref, sc2_ref, sh2_ref, rw_ref, rb_ref,
                        o_ref, route_ref):
    y = _dot(_row_permutation(nseq, steps, False), y_ref[...].astype(BF16)).astype(BF16)
    acc = _dot(y, w_ref[...])
    x = x_ref[...] + g_ref[...] * acc.reshape(nseq, steps, acc.shape[-1])
    o_ref[...] = x
    h = _normmod(x, ng_ref[...], sc2_ref[...], sh2_ref[...]).reshape(nseq * steps, x.shape[-1])
    route_ref[0] = _route_record(h, rw_ref[...], rb_ref[...])


def proj_residual_tm(y, w, x, mod, norm2_g, router_w, router_b, nseq, seq_len, per_seq):
    t, d = x.shape
    steps = TM_ROWS // nseq
    n_steps = seq_len // steps
    g1 = _group_mod(mod, 2, nseq, per_seq)
    sc2 = _group_mod(mod, 4, nseq, per_seq)
    sh2 = _group_mod(mod, 3, nseq, per_seq)
    mod_spec = pl.BlockSpec(g1.shape, lambda i: (0, 0, 0))
    out, route = pl.pallas_call(
        functools.partial(_proj_res_tm_kernel, nseq, steps),
        grid=(n_steps,),
        in_specs=[pl.BlockSpec((TM_ROWS, y.shape[1]), lambda i: (i, 0)),
                  pl.BlockSpec(w.shape, lambda i: (0, 0)),
                  pl.BlockSpec((nseq, steps, d), lambda i: (0, i, 0)),
                  mod_spec,
                  pl.BlockSpec((1, d), lambda i: (0, 0)),
                  mod_spec, mod_spec,
                  pl.BlockSpec((N_EXPERTS, d), lambda i: (0, 0)),
                  pl.BlockSpec((N_EXPERTS, 1), lambda i: (0, 0))],
        out_specs=[pl.BlockSpec((nseq, steps, d), lambda i: (0, i, 0)),
                   pl.BlockSpec((1, ROUTE_ROWS, TM_ROWS), lambda i: (i, 0, 0))],
        out_shape=[jax.ShapeDtypeStruct((nseq, seq_len, d), F32),
                   jax.ShapeDtypeStruct((n_steps, ROUTE_ROWS, TM_ROWS), F32)],
        compiler_params=_cparams("arbitrary"),
        name="proj_residual_tm_route",
    )(y, w, x.reshape(nseq, seq_len, d), g1, norm2_g.reshape(1, d), sc2, sh2, router_w.T,
      router_b.reshape(N_EXPERTS, 1))
    route = route.reshape(n_steps, ROUTE_ROWS, nseq, steps).transpose(1, 2, 0, 3).reshape(ROUTE_ROWS, t)
    return out.reshape(t, d), route


CTX_STEP_ROWS = 1024


def _ctx_attn_kernel(seq_len, n_seq, q_ref, k_ref, v_ref, o_ref, nk_ref, nv_ref):
    scale = HEAD_DIM ** -0.5
    per_tile = LANES // HEAD_DIM
    lane = lax.broadcasted_iota(jnp.int32, (1, LANES), 1)
    for b in range(n_seq):
        rows = slice(b * seq_len, (b + 1) * seq_len)
        for hp in range(NA_HEADS // per_tile):
            sl = slice(hp * LANES, (hp + 1) * LANES)
            q, k, v = q_ref[rows, sl] * scale, k_ref[rows, sl], v_ref[rows, sl]
            kb, vb = k.astype(BF16), v.astype(BF16)
            out = None
            for j in range(per_tile):
                h = hp * per_tile + j
                nk_ref[b, h] = k[:, j * HEAD_DIM:(j + 1) * HEAD_DIM]
                nv_ref[b, h] = v[:, j * HEAD_DIM:(j + 1) * HEAD_DIM]
                mine = lane // HEAD_DIM == j
                s = _dot_nt(jnp.where(mine, q, 0.0).astype(BF16), kb)
                p = jnp.exp(s - jnp.max(s, axis=-1, keepdims=True))
                o = _dot(p.astype(BF16), vb) / jnp.sum(p, axis=-1, keepdims=True)
                out = o if out is None else jnp.where(mine, o, out)
            o_ref[rows, sl] = out.astype(o_ref.dtype)


def context_attention(u, nseq, seq_len):
    t = u.shape[0]
    per_step = max(1, CTX_STEP_ROWS // seq_len)
    assert nseq % per_step == 0
    rows = per_step * seq_len
    kv_shape = jax.ShapeDtypeStruct((nseq, NA_HEADS, seq_len, HEAD_DIM), F32)
    kv_spec = pl.BlockSpec((per_step, NA_HEADS, seq_len, HEAD_DIM), lambda b: (b, 0, 0, 0))
    return pl.pallas_call(
        functools.partial(_ctx_attn_kernel, seq_len, per_step),
        grid=(nseq // per_step,),
        in_specs=[pl.BlockSpec((rows, D_A), lambda b: (b, 0)),
                  pl.BlockSpec((rows, D_A), lambda b: (b, 1)),
                  pl.BlockSpec((rows, D_A), lambda b: (b, 2))],
        out_specs=[pl.BlockSpec((rows, D_A), lambda b: (b, 0)), kv_spec, kv_spec],
        out_shape=[jax.ShapeDtypeStruct((t, D_A), BF16), kv_shape, kv_shape],
        compiler_params=_cparams("arbitrary"),
        name="context_attention",
    )(u, u, u)


N_DR = 2 * WIN_ROWS - 1
N_DC = 2 * WIN_COLS - 1
N_DC_PAD = 32


def _na_col_tables():
    cols = np.arange(GRID_W)
    col_start = np.clip(cols - WIN_COLS // 2, 0, GRID_W - WIN_COLS)
    col_in = (cols[None, :] >= col_start[:, None]) & (cols[None, :] < col_start[:, None] + WIN_COLS)
    dc = np.clip(cols[None, :] - cols[:, None], 1 - WIN_COLS, WIN_COLS - 1) + WIN_COLS - 1
    onehot = (dc.reshape(1, -1) == np.arange(N_DC_PAD)[:, None]).astype(np.float32)
    return onehot, col_in.reshape(1, -1).astype(np.float32)


def _na_bias_kernel(r_ref, e_ref, m_ref, o_ref):
    t = jnp.dot(r_ref[...], e_ref[...], precision=lax.Precision.HIGHEST, preferred_element_type=F32)
    o_ref[...] = jnp.where(m_ref[...] > 0.0, t, NEG_INF)


def na_bias_table(rpb):
    onehot, col_in = _na_col_tables()
    n_rows = NA_HEADS * N_DR
    assert n_rows <= LANES
    r = jnp.zeros((LANES, N_DC_PAD), F32).at[:n_rows, :N_DC].set(rpb.reshape(n_rows, N_DC).astype(F32))
    t = pl.pallas_call(
        _na_bias_kernel,
        out_shape=jax.ShapeDtypeStruct((LANES, GRID_W * GRID_W), F32),
        name="na_bias_table",
    )(r, jnp.asarray(onehot), jnp.asarray(col_in))
    t = t[:n_rows].reshape(NA_HEADS, N_DR, GRID_W, GRID_W)
    return jnp.concatenate([t[:, :-1], t[:, 1:]], axis=-1)


def _na_kernel(rows, q_ref, k_ref, v_ref, ck_ref, cv_ref, bias_ref, o_ref,
               q_s, k_s, v_s, ck_s, cv_s, s_s, p_s, den_s, o_s):
    scale = HEAD_DIM ** -0.5
    n_lat = WIN_ROWS * GRID_W
    per_tile = LANES // HEAD_DIM
    n_pairs = NA_HEADS // per_tile
    lane = lax.broadcasted_iota(jnp.int32, (1, LANES), 1)
    for hp in range(n_pairs):
        sl = slice(hp * LANES, (hp + 1) * LANES)
        q_s[hp] = (q_ref[:, sl] * scale).astype(BF16)
        k_s[hp] = k_ref[:, sl].astype(BF16)
        v_s[hp] = v_ref[:, sl].astype(BF16)
        heads = range(hp * per_tile, (hp + 1) * per_tile)
        ck_s[hp] = jnp.concatenate([ck_ref[0, h] for h in heads], axis=1).astype(BF16)
        cv_s[hp] = jnp.concatenate([cv_ref[0, h] for h in heads], axis=1).astype(BF16)

    def window(r):
        start = min(max(r - WIN_ROWS // 2, 0), rows - WIN_ROWS)
        return start, start - r + WIN_ROWS - 1

    def pair_body(hp, carry):
        for j in range(per_tile):
            h = hp * per_tile + j
            mine = lane // HEAD_DIM == j
            for r in range(rows):
                start, off = window(r)
                rs = slice(r * GRID_W, (r + 1) * GRID_W)
                q = jnp.where(mine, q_s[hp, rs, :], 0.0).astype(BF16)
                bias = jnp.concatenate([bias_ref[h, off + 2 * i] for i in range(WIN_ROWS // 2)], axis=1)
                s_s[rs, 0:n_lat] = _dot_nt(q, k_s[hp, start * GRID_W:start * GRID_W + n_lat, :]) + bias
                s_s[rs, n_lat:] = _dot_nt(q, ck_s[hp])
            for r in range(rows):
                rs = slice(r * GRID_W, (r + 1) * GRID_W)
                s = s_s[rs, :]
                p = jnp.exp(s - jnp.max(s, axis=-1, keepdims=True))
                den_s[rs, :] = jnp.sum(p, axis=-1, keepdims=True)
                p_s[rs, :] = p.astype(BF16)
            for r in range(rows):
                start, _ = window(r)
                rs = slice(r * GRID_W, (r + 1) * GRID_W)
                o = (_dot(p_s[rs, 0:n_lat], v_s[hp, start * GRID_W:start * GRID_W + n_lat, :])
                     + _dot(p_s[rs, n_lat:], cv_s[hp])) / den_s[rs, :]
                o_s[hp, rs, :] = o if j == 0 else jnp.where(mine, o, o_s[hp, rs, :])
        return carry

    lax.fori_loop(0, n_pairs, pair_body, 0)
    for hp in range(n_pairs):
        o_ref[:, hp * LANES:(hp + 1) * LANES] = o_s[hp].astype(o_ref.dtype)


def neighbourhood_attention(u, ctx_k, ctx_v, rpb, nseq, seq_len):
    t = u.shape[0]
    rows = seq_len // GRID_W
    assert rows >= WIN_ROWS and WIN_ROWS % 2 == 0
    past = ctx_k.shape[2]
    bias = na_bias_table(rpb)
    ctx_spec = pl.BlockSpec((1, NA_HEADS, past, HEAD_DIM), lambda b: (b, 0, 0, 0))
    return pl.pallas_call(
        functools.partial(_na_kernel, rows),
        grid=(nseq,),
        in_specs=[pl.BlockSpec((seq_len, D_A), lambda b: (b, 0)),
                  pl.BlockSpec((seq_len, D_A), lambda b: (b, 1)),
                  pl.BlockSpec((seq_len, D_A), lambda b: (b, 2)),
                  ctx_spec, ctx_spec,
                  pl.BlockSpec(bias.shape, lambda b: (0, 0, 0, 0))],
        out_specs=pl.BlockSpec((seq_len, D_A), lambda b: (b, 0)),
        out_shape=jax.ShapeDtypeStruct((t, D_A), BF16),
        scratch_shapes=[pltpu.VMEM((D_A // LANES, seq_len, LANES), BF16)] * 3
        + [pltpu.VMEM((D_A // LANES, past, LANES), BF16)] * 2
        + [pltpu.VMEM((seq_len, WIN_ROWS * GRID_W + past), F32),
           pltpu.VMEM((seq_len, WIN_ROWS * GRID_W + past), BF16),
           pltpu.VMEM((seq_len, 1), F32),
           pltpu.VMEM((D_A // LANES, seq_len, LANES), F32)],
        compiler_params=_cparams("arbitrary"),
        name="neighbourhood_attention",
    )(u, u, u, ctx_k, ctx_v, bias)


HY_STEP_ROWS = 1024


def _dft_tables(seq_len):
    n = 2 * seq_len
    f = np.arange(seq_len, dtype=np.int64)
    ang = (np.outer(f, f) % n).astype(np.float64) * (math.pi / seq_len)
    cos, sin = np.cos(ang), np.sin(ang)
    alt = np.where(f % 2 == 0, 1.0, -1.0)
    s_fwd = -sin
    s_fwd[0, :] = alt
    fwd = np.concatenate([cos, s_fwd], axis=0)
    wf = np.where(f == 0, 1.0, 2.0) / n
    ci = cos.T * wf[None, :]
    si = -sin.T * wf[None, :]
    si[:, 0] = alt / n
    inv = np.concatenate([ci, si], axis=1)
    return fwd.astype(np.float32), inv.astype(np.float32)


def _hyena_feats(seq_len):
    t = np.linspace(0.0, 1.0, seq_len, dtype=np.float32)[:, None]
    w = (2.0 * math.pi * np.arange(seq_len, dtype=np.float32)[:, None] / seq_len).astype(np.float32)
    f = np.linspace(1e-4, HY_BANDS - 1, HY_BANDS, dtype=np.float32)[None, :]
    z = np.concatenate([t, np.cos(f * w), -np.sin(f * w)], axis=-1).astype(np.float32)
    max_decay = math.log(HY_DECAY_TARGET) / HY_FAST_PCT
    min_decay = math.log(HY_DECAY_TARGET) / HY_SLOW_PCT
    deltas = np.abs(np.linspace(min_decay, max_decay, D_B, dtype=np.float32))[None, :]
    return z, t, deltas


def _hy_filter_kernel(seq_len, z_ref, t_ref, dl_ref, w1_ref, b1_ref, w2_ref, b2_ref, w3_ref, fr_ref,
                      d_ref, fwd_ref, g_ref):
    hp = lax.Precision.HIGHEST
    h = jnp.sin(fr_ref[0:1, :] * (jnp.dot(z_ref[...], w1_ref[...], precision=hp) + b1_ref[...]))
    h = jnp.sin(fr_ref[1:2, :] * (jnp.dot(h, w2_ref[...], precision=hp) + b2_ref[...]))
    h = jnp.dot(h, w3_ref[...], precision=hp)
    decay = jnp.exp(-t_ref[...] * dl_ref[...])
    row0 = lax.broadcasted_iota(jnp.int32, (seq_len, D_B), 0) == 0
    sums, diffs = [], []
    for n in range(HY_ORDER):
        hf = h[:, (2 * n) * D_B:(2 * n + 1) * D_B] * decay
        hb = h[:, (2 * n + 1) * D_B:(2 * n + 2) * D_B] * decay
        gp = jnp.where(row0, hf + hb + d_ref[n:n + 1, :], hf)
        gm = jnp.where(row0, 0.0, hb)
        sums.append(gp + gm)
        diffs.append(gp - gm)
    rhs = jnp.concatenate(sums + diffs, axis=1).astype(BF16)
    spec = _dot(fwd_ref[...], rhs)
    for n in range(HY_ORDER):
        a = spec[:, n * D_B:(n + 1) * D_B]
        b = spec[:, (HY_ORDER + n) * D_B:(HY_ORDER + n + 1) * D_B]
        g_ref[n, 0:seq_len, :] = a[0:seq_len]
        g_ref[n, seq_len:, :] = jnp.where(row0, a[seq_len:], b[seq_len:])


def hyena_spectrum(seq_len, w1, b1, w2, b2, w3, freq, d, fwd):
    z, t, deltas = _hyena_feats(seq_len)
    return pl.pallas_call(
        functools.partial(_hy_filter_kernel, seq_len),
        out_shape=jax.ShapeDtypeStruct((HY_ORDER, 2 * seq_len, D_B), F32),
        compiler_params=pltpu.CompilerParams(vmem_limit_bytes=VMEM_LIMIT),
        name="hyena_spectrum",
    )(jnp.asarray(z), jnp.asarray(t), jnp.asarray(deltas), w1, b1.reshape(1, -1), w2, b2.reshape(1, -1),
      w3, freq, d, fwd)


def _hyena_kernel(seq_len, n_seq, u_ref, sw_ref, sb_ref, g_ref, fwd_ref, inv_ref, o_ref):
    t_idx = lax.broadcasted_iota(jnp.int32, (seq_len, u_ref.shape[1]), 0)
    row0 = lax.broadcasted_iota(jnp.int32, (seq_len, D_B), 0) == 0
    for s in range(n_seq):
        rows = slice(s * seq_len, (s + 1) * seq_len)
        u = u_ref[rows, :]
        prev = jnp.where(t_idx == 0, 0.0, pltpu.roll(u, 1, axis=0))
        nxt = jnp.where(t_idx == seq_len - 1, 0.0, pltpu.roll(u, seq_len - 1, axis=0))
        u = prev * sw_ref[0:1, :] + u * sw_ref[1:2, :] + nxt * sw_ref[2:3, :] + sb_ref[...]
        z = u[:, 0:D_B]
        for n in range(HY_ORDER):
            spec = _dot(fwd_ref[...], z.astype(BF16))
            ure, uim = spec[0:seq_len], spec[seq_len:]
            gre, gim = g_ref[n, 0:seq_len, :], g_ref[n, seq_len:, :]
            pim = uim * gim
            yre = ure * gre - jnp.where(row0, 0.0, pim)
            yim = jnp.where(row0, pim, ure * gim + uim * gre)
            y = jnp.concatenate([yre, yim], axis=0).astype(BF16)
            z = u[:, (n + 1) * D_B:(n + 2) * D_B] * _dot(inv_ref[...], y)
        o_ref[rows, :] = z.astype(o_ref.dtype)


def hyena_mixer(u, nseq, seq_len, short_w, short_b, spectrum, fwd, inv):
    t = u.shape[0]
    width = (HY_ORDER + 1) * D_B
    col_block = (3 * D_A) // width
    assert col_block * width == 3 * D_A
    per_step = max(1, HY_STEP_ROWS // seq_len)
    assert nseq % per_step == 0
    return pl.pallas_call(
        functools.partial(_hyena_kernel, seq_len, per_step),
        grid=(nseq // per_step,),
        in_specs=[pl.BlockSpec((per_step * seq_len, width), lambda b: (b, col_block)),
                  pl.BlockSpec(short_w.shape, lambda b: (0, 0)),
                  pl.BlockSpec((1, width), lambda b: (0, 0)),
                  pl.BlockSpec(spectrum.shape, lambda b: (0, 0, 0)),
                  pl.BlockSpec(fwd.shape, lambda b: (0, 0)),
                  pl.BlockSpec(inv.shape, lambda b: (0, 0))],
        out_specs=pl.BlockSpec((per_step * seq_len, D_B), lambda b: (b, 0)),
        out_shape=jax.ShapeDtypeStruct((t, D_B), BF16),
        compiler_params=_cparams("arbitrary"),
        name="hyena_mixer",
    )(u, short_w, short_b.reshape(1, width), spectrum, fwd, inv)


RG_CB = LANES
RG_CHUNK = 512
TM_ROWS = 512


def _rglru_kernel(nseq, seq_len, gate_ref, xr_ref, cw_ref, cb_ref, wg_ref, bg_ref, lam_ref, h0_ref,
                  y_ref, fin_ref, xp_ref, a_f, b_f, a_b, b_b):
    t_tot = nseq * seq_len
    c = RG_CB
    pad = 2 * nseq
    xp_ref[0:pad, :] = jnp.zeros((pad, c), F32)
    xp_ref[pad + t_tot:, :] = jnp.zeros((pad, c), F32)
    xp_ref[pad:pad + t_tot, :] = xr_ref[...]
    nl = -lam_ref[...]
    sp = jnp.maximum(nl, 0.0) + jnp.log1p(jnp.exp(-jnp.abs(nl)))
    k2 = (-0.5 * RG_C * math.log2(math.e)) * sp

    def gate_chunk(ci, carry):
        r0 = pl.multiple_of(ci * RG_CHUNK, RG_CHUNK)
        xc = xp_ref[pl.ds(r0, RG_CHUNK), :] * cw_ref[0:1, :]
        for j in range(1, cw_ref.shape[0]):
            xc = xc + xp_ref[pl.ds(r0 + j * nseq, RG_CHUNK), :] * cw_ref[j:j + 1, :]
        xc = xc + cb_ref[...]
        gts = _dot(xc.astype(BF16), wg_ref[0]) + bg_ref[...]
        x_half = 0.5 * xc
        for d, (a_ref, b_ref) in enumerate(((a_f, b_f), (a_b, b_b))):
            t_r = jnp.tanh(gts[:, (2 * d) * c:(2 * d + 1) * c])
            t_i = jnp.tanh(gts[:, (2 * d + 1) * c:(2 * d + 2) * c])
            a = jnp.exp2(t_r * k2[d:d + 1, :] + k2[d:d + 1, :])
            a_ref[pl.ds(r0, RG_CHUNK), :] = a
            y = 1.0 - a * a
            root = jnp.where(y > 0.0, y * lax.rsqrt(y), 0.0)
            b_ref[pl.ds(r0, RG_CHUNK), :] = root * ((t_i + 1.0) * x_half)
        return carry

    lax.fori_loop(0, t_tot // RG_CHUNK, gate_chunk, 0)

    def scan_step(t, carry):
        hf, hb = carry
        rows_f = pl.ds(pl.multiple_of(t * nseq, nseq), nseq)
        rows_b = pl.ds(pl.multiple_of((seq_len - 1 - t) * nseq, nseq), nseq)
        hf = a_f[rows_f, :] * hf + b_f[rows_f, :]
        hb = a_b[rows_b, :] * hb + b_b[rows_b, :]
        b_f[rows_f, :] = hf
        b_b[rows_b, :] = hb
        return hf, hb

    hf, hb = lax.fori_loop(0, seq_len, scan_step, (h0_ref[0], h0_ref[1]), unroll=8)
    fin_ref[0] = hf
    fin_ref[1] = hb

    def out_chunk(ci, carry):
        rs = pl.ds(pl.multiple_of(ci * RG_CHUNK, RG_CHUNK), RG_CHUNK)
        y_ref[rs, :] = ((b_f[rs, :] + b_b[rs, :]) * jax.nn.gelu(gate_ref[rs, :])).astype(y_ref.dtype)
        return carry

    lax.fori_loop(0, t_tot // RG_CHUNK, out_chunk, 0)


def _rg_gate_weights(wa, wx):
    per_step = RG_CB // RG_BLOCK
    steps = D_RNN // RG_CB
    mats = []
    for d in range(2):
        for w in (wa[d], wx[d]):
            w = w.reshape(steps, per_step, RG_BLOCK, RG_BLOCK)
            eye = jnp.eye(per_step, dtype=w.dtype)
            m = jnp.einsum('spde,pq->spdqe', w, eye).reshape(steps, RG_CB, RG_CB)
            mats.append(m)
    return (0.5 * jnp.concatenate(mats, axis=-1)).astype(BF16)


def rglru_block(u, nseq, seq_len, conv_w, conv_b, wa, ba, wx, bx, lam, h0):
    t = u.shape[0]
    c = RG_CB
    steps = D_RNN // c
    wg = _rg_gate_weights(wa, wx)
    bg = jnp.stack([ba[0], bx[0], ba[1], bx[1]], axis=0).reshape(4, steps, c)
    bg = 0.5 * bg.transpose(1, 0, 2).reshape(steps, 1, 4 * c)
    y, fin = pl.pallas_call(
        functools.partial(_rglru_kernel, nseq, seq_len),
        grid=(steps,),
        in_specs=[pl.BlockSpec((t, c), lambda j: (0, j)),
                  pl.BlockSpec((t, c), lambda j: (0, steps + j)),
                  pl.BlockSpec((conv_w.shape[0], c), lambda j: (0, j)),
                  pl.BlockSpec((1, c), lambda j: (0, j)),
                  pl.BlockSpec((1, c, 4 * c), lambda j: (j, 0, 0)),
                  pl.BlockSpec((None, 1, 4 * c), lambda j: (j, 0, 0)),
                  pl.BlockSpec((2, c), lambda j: (0, j)),
                  pl.BlockSpec((2, nseq, c), lambda j: (0, 0, j))],
        out_specs=[pl.BlockSpec((t, c), lambda j: (0, j)),
                   pl.BlockSpec((2, nseq, c), lambda j: (0, 0, j))],
        out_shape=[jax.ShapeDtypeStruct((t, D_RNN), BF16),
                   jax.ShapeDtypeStruct((2, nseq, D_RNN), F32)],
        scratch_shapes=[pltpu.VMEM((t + 4 * nseq, c), F32)] + [pltpu.VMEM((t, c), F32)] * 4,
        compiler_params=_cparams("arbitrary"),
        name="rglru_block",
    )(u, u, conv_w, conv_b.reshape(1, -1), wg, bg, lam, h0)
    return y, fin


def _route_record(h, w, rb):
    h_hi = h.astype(BF16)
    h_lo = (h - h_hi.astype(F32)).astype(BF16)
    w_hi = w.astype(BF16)
    w_lo = (w - w_hi.astype(F32)).astype(BF16)
    logits = _dot_nt(w_hi, h_hi) + (_dot_nt(w_lo, h_hi) + _dot_nt(w_hi, h_lo))
    scores = jax.nn.sigmoid(logits)
    sel = scores + rb
    row = [sel[e:e + 1, :] for e in range(N_EXPERTS)]
    gs = []
    for g in range(N_GROUPS):
        r = row[g * EXPERTS_PER_GROUP:(g + 1) * EXPERTS_PER_GROUP]
        best_pair = None
        for i in range(EXPERTS_PER_GROUP):
            for j in range(i + 1, EXPERTS_PER_GROUP):
                s = r[i] + r[j]
                best_pair = s if best_pair is None else jnp.maximum(best_pair, s)
        gs.append(best_pair)
    best = jnp.zeros_like(gs[0], dtype=jnp.int32)
    top = gs[0]
    for g in range(1, N_GROUPS):
        better = gs[g] > top
        best = jnp.where(better, g, best)
        top = jnp.where(better, gs[g], top)
    picked = []
    for e in range(N_EXPERTS):
        g = e // EXPERTS_PER_GROUP
        rank = jnp.zeros_like(best)
        for o in range(g * EXPERTS_PER_GROUP, (g + 1) * EXPERTS_PER_GROUP):
            if o == e:
                continue
            ahead = (row[o] > row[e]) | ((row[o] == row[e]) & (o < e))
            rank = rank + ahead.astype(jnp.int32)
        picked.append((best == g) & (rank < 2))
    den = jnp.zeros_like(gs[0])
    for e in range(N_EXPERTS):
        den = den + jnp.where(picked[e], scores[e:e + 1, :], 0.0)
    gate = [jnp.where(picked[e], scores[e:e + 1, :] / den, 0.0) for e in range(N_EXPERTS)]
    cls = jnp.zeros_like(den)
    w_a = jnp.zeros_like(den)
    w_b = jnp.zeros_like(den)
    for g in range(N_GROUPS):
        for pi, (a, b) in enumerate(MOE_PAIRS):
            ea, eb = g * EXPERTS_PER_GROUP + a, g * EXPERTS_PER_GROUP + b
            both = picked[ea] & picked[eb]
            cls = jnp.where(both, float(g * len(MOE_PAIRS) + pi), cls)
            w_a = jnp.where(both, gate[ea], w_a)
            w_b = jnp.where(both, gate[eb], w_b)
    return jnp.concatenate([cls, w_a, w_b, jnp.zeros((ROUTE_ROWS - 3, cls.shape[1]), F32)], axis=0)


MOE_PAIRS = ((0, 1), (0, 2), (0, 3), (1, 3), (1, 2), (2, 3))
N_CLS = N_GROUPS * len(MOE_PAIRS)
CLS_PAD = 32
ROUTE_ROWS = 8
MOE_TS = 256
MOE_TM = 256
MOE_STEP_TILES = 4
SLOT_BLK = 512
ROW_W = D_MODEL + LANES


def _slots_kernel(n_blk, route_ref, slot_ref, off_ref, cnt_ref):
    cid = lax.broadcasted_iota(jnp.int32, (CLS_PAD, SLOT_BLK), 0).astype(F32)

    def members(j):
        cls = route_ref[0:1, pl.ds(pl.multiple_of(j * SLOT_BLK, SLOT_BLK), SLOT_BLK)]
        return (cid == cls).astype(F32)

    def count(j, cnt):
        return cnt + jnp.sum(members(j), axis=1, keepdims=True)

    cnt = lax.fori_loop(0, n_blk, count, jnp.zeros((CLS_PAD, 1), F32))
    cnt = jnp.broadcast_to(cnt, (CLS_PAD, LANES))
    padded = jnp.ceil(cnt * (1.0 / MOE_TS)) * MOE_TS
    r = lax.broadcasted_iota(jnp.int32, (CLS_PAD, CLS_PAD), 0)
    c = lax.broadcasted_iota(jnp.int32, (CLS_PAD, CLS_PAD), 1)
    off = jnp.dot((c < r).astype(F32), padded, precision=lax.Precision.HIGHEST, preferred_element_type=F32)
    off_ref[...] = off
    cnt_ref[...] = cnt
    tr = lax.broadcasted_iota(jnp.int32, (SLOT_BLK, SLOT_BLK), 0)
    tc = lax.broadcasted_iota(jnp.int32, (SLOT_BLK, SLOT_BLK), 1)
    earlier = (tr < tc).astype(BF16)

    def assign(j, base):
        member = members(j)
        rank = _dot(member.astype(BF16), earlier)
        slot = jnp.sum(member * (rank + base), axis=0, keepdims=True)
        slot_ref[0:1, pl.ds(pl.multiple_of(j * SLOT_BLK, SLOT_BLK), SLOT_BLK)] = slot.astype(jnp.int32)
        return base + jnp.sum(member, axis=1, keepdims=True)

    lax.fori_loop(0, n_blk, assign, off[:, 0:1])


def moe_slots(route):
    t = route.shape[1]
    stat = jax.ShapeDtypeStruct((CLS_PAD, LANES), F32)
    return pl.pallas_call(
        functools.partial(_slots_kernel, t // SLOT_BLK),
        out_shape=[jax.ShapeDtypeStruct((1, t), jnp.int32), stat, stat],
        compiler_params=pltpu.CompilerParams(vmem_limit_bytes=VMEM_LIMIT),
        name="moe_slots",
    )(route)


def _tile_maps(off, cnt, n_tiles):
    off = off[:N_CLS, 0].astype(jnp.int32)
    cnt = cnt[:N_CLS, 0].astype(jnp.int32)
    ends = off + ((cnt + MOE_TS - 1) // MOE_TS) * MOE_TS
    n_used = ends[-1] // MOE_TS
    k = jnp.arange(n_tiles, dtype=jnp.int32)
    tix = jnp.minimum(k, n_used - 1)
    cls = jnp.sum((tix[:, None] * MOE_TS >= ends[None, :]).astype(jnp.int32), axis=1)
    pair = jnp.asarray(MOE_PAIRS, jnp.int32)
    grp = (cls // len(MOE_PAIRS)) * EXPERTS_PER_GROUP
    ea = grp + pair[cls % len(MOE_PAIRS), 0]
    eb = grp + pair[cls % len(MOE_PAIRS), 1]
    n = jnp.int32(n_tiles)

    def slot_plan(e):
        chg = jnp.concatenate([jnp.ones((1,), jnp.int32), (e[1:] != e[:-1]).astype(jnp.int32)])
        at = jnp.where(chg == 1, k, n)
        nxt_at = jnp.concatenate([lax.cummin(at[::-1])[::-1][1:], n.reshape(1)])
        more = (nxt_at < n).astype(jnp.int32)
        nxt = e[jnp.minimum(nxt_at, n - 1)]
        par = (jnp.cumsum(chg) - 1) % 2
        return chg, nxt, more, par.astype(jnp.int32)

    plan_a, plan_b = slot_plan(ea), slot_plan(eb)
    chg, nxt, more, par = (jnp.stack([pa, pb]) for pa, pb in zip(plan_a, plan_b))
    return ea, eb, chg, nxt, more, par, n_used.reshape(1)


def _dispatch_kernel(n_steps, slots_ref, x_ref, g_ref, sc_ref, sh_ref, rt_ref, hs_in, hs_out, rowbuf, sem):
    del hs_in
    i = pl.program_id(0)
    cur = i % 2

    def wait_rows(s):
        pltpu.make_async_copy(rowbuf.at[s], rowbuf.at[s], sem.at[s]).wait()

    @pl.when(i >= 2)
    def _():
        wait_rows(cur)

    rowbuf[cur, :, 0:D_MODEL] = _normmod(x_ref[...], g_ref[...], sc_ref[0], sh_ref[0])
    rowbuf[cur, :, D_MODEL:ROW_W] = jnp.concatenate(
        [rt_ref[...], jnp.zeros((MOE_TM, LANES - ROUTE_ROWS), F32)], axis=1)

    for s in range(2):
        @pl.when(cur == s)
        def _():
            for r in range(MOE_TM):
                dst = slots_ref[i * MOE_TM + r]
                pltpu.make_async_copy(rowbuf.at[s, r], hs_out.at[dst], sem.at[s]).start()

    @pl.when(i == n_steps - 1)
    def _():
        wait_rows(cur)
        if n_steps >= 2:
            wait_rows(1 - cur)


def moe_dispatch(x, g, mod, route_t, slots, hs, seq_len, per_seq):
    t, d = x.shape
    n_steps = t // MOE_TM
    grid_spec = pltpu.PrefetchScalarGridSpec(
        num_scalar_prefetch=1,
        grid=(n_steps,),
        in_specs=[pl.BlockSpec((MOE_TM, d), lambda i, s: (i, 0)),
                  pl.BlockSpec((1, d), lambda i, s: (0, 0)),
                  _mod_spec(4, MOE_TM, seq_len, per_seq),
                  _mod_spec(3, MOE_TM, seq_len, per_seq),
                  pl.BlockSpec((MOE_TM, ROUTE_ROWS), lambda i, s: (i, 0)),
                  pl.BlockSpec(memory_space=pl.ANY)],
        out_specs=pl.BlockSpec(memory_space=pl.ANY),
        scratch_shapes=[pltpu.VMEM((2, MOE_TM, ROW_W), F32), pltpu.SemaphoreType.DMA((2,))],
    )
    return pl.pallas_call(
        functools.partial(_dispatch_kernel, n_steps),
        grid_spec=grid_spec,
        out_shape=jax.ShapeDtypeStruct(hs.shape, F32),
        input_output_aliases={6: 0},
        compiler_params=_cparams("arbitrary"),
        name="moe_dispatch",
    )(slots, x, g.reshape(1, d), mod, mod, route_t, hs)


def _experts_kernel(layer, ea_ref, eb_ref, chg_ref, nxt_ref, more_ref, par_ref, nused_ref,
                    hs_ref, wg_hbm, wu_hbm, wd_hbm, ys_ref, fg, fu, fd, bg, bu, bd, sem):
    def weight_copies(slot, expert, par):
        return [pltpu.make_async_copy(src.at[layer, expert], dst.at[slot, par], sem.at[slot, par])
                for src, dst in ((wg_hbm, fg), (wu_hbm, fu), (wd_hbm, fd))]

    def tile(k, rows):
        @pl.when(k < nused_ref[0])
        def _():
            for slot, e_ref in enumerate((ea_ref, eb_ref)):
                @pl.when(chg_ref[slot, k] == 1)
                def _():
                    par = par_ref[slot, k]

                    @pl.when(k == 0)
                    def _():
                        for cp in weight_copies(slot, e_ref[0], par):
                            cp.start()

                    for cp in weight_copies(slot, e_ref[k], par):
                        cp.wait()
                    bg[slot] = fg[slot, par].astype(BF16)
                    bu[slot] = fu[slot, par].astype(BF16)
                    bd[slot] = fd[slot, par].astype(BF16)

                    @pl.when(more_ref[slot, k] == 1)
                    def _():
                        for cp in weight_copies(slot, nxt_ref[slot, k], 1 - par):
                            cp.start()

            h = hs_ref[rows, 0:D_MODEL].astype(BF16)

            def ffn(slot):
                hid = _dot(h, bg[slot])
                up = _dot(h, bu[slot])
                w = hs_ref[rows, D_MODEL + 1 + slot:D_MODEL + 2 + slot]
                act = (hid * _sigmoid(hid)) * up * w
                return _dot(act.astype(BF16), bd[slot])

            ys_ref[rows, :] = ffn(0) + ffn(1)

        @pl.when(k >= nused_ref[0])
        def _():
            ys_ref[rows, :] = jnp.zeros((MOE_TS, ys_ref.shape[1]), ys_ref.dtype)

    for j in range(MOE_STEP_TILES):
        tile(pl.program_id(0) * MOE_STEP_TILES + j, slice(j * MOE_TS, (j + 1) * MOE_TS))


def moe_experts(hs, maps, layer, w_gate, w_up, w_down):
    n_tiles = hs.shape[0] // MOE_TS
    d = D_MODEL

    grid_spec = pltpu.PrefetchScalarGridSpec(
        num_scalar_prefetch=7,
        grid=(n_tiles // MOE_STEP_TILES,),
        in_specs=[pl.BlockSpec((MOE_STEP_TILES * MOE_TS, ROW_W), lambda k, *_: (k, 0)),
                  pl.BlockSpec(memory_space=pl.ANY), pl.BlockSpec(memory_space=pl.ANY),
                  pl.BlockSpec(memory_space=pl.ANY)],
        out_specs=pl.BlockSpec((MOE_STEP_TILES * MOE_TS, d), lambda k, *_: (k, 0)),
        scratch_shapes=[pltpu.VMEM((2, 2, d, D_EXPERT), F32), pltpu.VMEM((2, 2, d, D_EXPERT), F32),
                        pltpu.VMEM((2, 2, D_EXPERT, d), F32),
                        pltpu.VMEM((2, d, D_EXPERT), BF16), pltpu.VMEM((2, d, D_EXPERT), BF16),
                        pltpu.VMEM((2, D_EXPERT, d), BF16),
                        pltpu.SemaphoreType.DMA((2, 2))],
    )
    return pl.pallas_call(
        functools.partial(_experts_kernel, layer),
        grid_spec=grid_spec,
        out_shape=jax.ShapeDtypeStruct((hs.shape[0], d), F32),
        compiler_params=_cparams("arbitrary"),
        name="moe_experts",
    )(*maps, hs, w_gate, w_up, w_down)


def _combine_kernel(final, n_steps, slots_ref, x_ref, g2_ref, fg_ref, ys_hbm, o_ref, gbuf, sem):
    i = pl.program_id(0)
    cur = i % 2

    def issue_tile(tile, s):
        for r in range(MOE_TM):
            src = slots_ref[tile * MOE_TM + r]
            pltpu.make_async_copy(ys_hbm.at[src], gbuf.at[s, r], sem.at[s]).start()

    @pl.when(i == 0)
    def _():
        issue_tile(0, 0)

    for s in range(2):
        @pl.when((i + 1 < n_steps) & (1 - cur == s))
        def _():
            issue_tile(i + 1, s)

    pltpu.make_async_copy(gbuf.at[cur], gbuf.at[cur], sem.at[cur]).wait()
    y = x_ref[...] + g2_ref[0] * gbuf[cur]
    if final:
        ms = jnp.mean(y * y, axis=-1, keepdims=True)
        y = y * lax.rsqrt(ms + EPS) * fg_ref[...]
    o_ref[...] = y


def moe_combine(x, mod, slots, ys, final_g, final, seq_len, per_seq):
    t, d = x.shape
    n_steps = t // MOE_TM
    grid_spec = pltpu.PrefetchScalarGridSpec(
        num_scalar_prefetch=1,
        grid=(n_steps,),
        in_specs=[pl.BlockSpec((MOE_TM, d), lambda i, s: (i, 0)),
                  _mod_spec(5, MOE_TM, seq_len, per_seq),
                  pl.BlockSpec((1, d), lambda i, s: (0, 0)),
                  pl.BlockSpec(memory_space=pl.ANY)],
        out_specs=pl.BlockSpec((MOE_TM, d), lambda i, s: (i, 0)),
        scratch_shapes=[pltpu.VMEM((2, MOE_TM, d), F32), pltpu.SemaphoreType.DMA((2,))],
    )
    return pl.pallas_call(
        functools.partial(_combine_kernel, final, n_steps),
        grid_spec=grid_spec,
        out_shape=jax.ShapeDtypeStruct((t, d), F32),
        compiler_params=_cparams("arbitrary"),
        name="moe_combine",
    )(slots, x, mod, final_g.reshape(1, d), ys)


SC_WINDOW = 128
SC_COLS = 256


def sc_gather_rows(table, idx):
    n = idx.shape[0]
    d = table.shape[1]
    mesh = plsc.VectorSubcoreMesh(core_axis_name="core", subcore_axis_name="subcore")

    @pl.kernel(out_type=jax.ShapeDtypeStruct((n, d), table.dtype), mesh=mesh)
    def gather(x_hbm, i_hbm, o_hbm):
        for j in range(d // SC_COLS):
            def body(i_vmem, o_vmem, j=j):
                pltpu.sync_copy(x_hbm.at[i_vmem.at[0], pl.ds(j * SC_COLS, SC_COLS)], o_vmem)

            per_core = n // SC_WINDOW // mesh.num_cores
            pltpu.emit_pipeline(
                body,
                grid=(mesh.num_cores, per_core),
                in_specs=[pl.BlockSpec((1, SC_WINDOW), lambda c, i: (0, c * per_core + i))],
                out_specs=[pl.BlockSpec((SC_WINDOW, SC_COLS), lambda c, i, j=j: (c * per_core + i, j))],
                core_axis_name=("core", "subcore"),
                dimension_semantics=(pltpu.PARALLEL, pltpu.PARALLEL),
            )(i_hbm, o_hbm)

    return gather(table, idx.reshape(1, n))


def _res_nm_tm_kernel(nseq, steps, x_ref, y_ref, g2_ref, g_ref, sc_ref, sh_ref, w_ref, x2_ref, u_ref):
    d = x_ref.shape[-1]
    y = x_ref[...] + g2_ref[...] * y_ref[...]
    x2_ref[...] = y
    h = _normmod(y, g_ref[...], sc_ref[...], sh_ref[...])
    h = h.reshape(nseq * steps, d).astype(BF16)
    h = _dot(_row_permutation(nseq, steps, True), h).astype(BF16)
    u_ref[...] = _dot(h, w_ref[...])


def residual_nm_matmul_tm(x, mod_prev, y, g, mod, w, nseq, seq_len, per_seq):
    t, d = x.shape
    n = w.shape[1]
    steps = TM_ROWS // nseq
    g2 = _group_mod(mod_prev, 5, nseq, per_seq)
    sc = _group_mod(mod, 1, nseq, per_seq)
    sh = _group_mod(mod, 0, nseq, per_seq)
    mod_spec = pl.BlockSpec(sc.shape, lambda i: (0, 0, 0))
    x_spec = pl.BlockSpec((nseq, steps, d), lambda i: (0, i, 0))
    x2, u = pl.pallas_call(
        functools.partial(_res_nm_tm_kernel, nseq, steps),
        grid=(seq_len // steps,),
        in_specs=[x_spec, x_spec, mod_spec, pl.BlockSpec((1, d), lambda i: (0, 0)), mod_spec, mod_spec,
                  pl.BlockSpec((d, n), lambda i: (0, 0))],
        out_specs=[x_spec, pl.BlockSpec((TM_ROWS, n), lambda i: (i, 0))],
        out_shape=[jax.ShapeDtypeStruct((nseq, seq_len, d), F32), jax.ShapeDtypeStruct((t, n), F32)],
        compiler_params=_cparams("arbitrary"),
        name="moe_residual_norm_mod_proj_tm",
    )(x.reshape(nseq, seq_len, d), y.reshape(nseq, seq_len, d), g2, g.reshape(1, d), sc, sh, w)
    return x2.reshape(t, d), u


def _combine_nm_tm_kernel(nseq, steps, seq_len, n_steps, slots_ref, x_ref, g2_ref, ys_hbm, g_ref, sc_ref,
                          sh_ref, w_ref, x2_ref, u_ref, gbuf, sem):
    i = pl.program_id(0)
    cur = i % 2

    def issue_tile(tile, s):
        for sq in range(nseq):
            for t in range(steps):
                src = slots_ref[sq * seq_len + tile * steps + t]
                pltpu.make_async_copy(ys_hbm.at[src], gbuf.at[s, sq * steps + t], sem.at[s]).start()

    @pl.when(i == 0)
    def _():
        issue_tile(0, 0)

    for s in range(2):
        @pl.when((i + 1 < n_steps) & (1 - cur == s))
        def _():
            issue_tile(i + 1, s)

    pltpu.make_async_copy(gbuf.at[cur], gbuf.at[cur], sem.at[cur]).wait()
    d = x_ref.shape[-1]
    y = x_ref[...] + g2_ref[...] * gbuf[cur].reshape(nseq, steps, d)
    x2_ref[...] = y
    h = _normmod(y, g_ref[...], sc_ref[...], sh_ref[...])
    h = h.reshape(nseq * steps, d).astype(BF16)
    h = _dot(_row_permutation(nseq, steps, True), h).astype(BF16)
    u_ref[...] = _dot(h, w_ref[...])


def combine_nm_matmul_tm(x, mod_prev, slots, ys, g, mod, w, nseq, seq_len, per_seq):
    t, d = x.shape
    n = w.shape[1]
    steps = TM_ROWS // nseq
    n_steps = seq_len // steps
    g2 = _group_mod(mod_prev, 5, nseq, per_seq)
    sc = _group_mod(mod, 1, nseq, per_seq)
    sh = _group_mod(mod, 0, nseq, per_seq)
    mod_spec = pl.BlockSpec(sc.shape, lambda i, s: (0, 0, 0))
    x_spec = pl.BlockSpec((nseq, steps, d), lambda i, s: (0, i, 0))
    grid_spec = pltpu.PrefetchScalarGridSpec(
        num_scalar_prefetch=1,
        grid=(n_steps,),
        in_specs=[x_spec, mod_spec, pl.BlockSpec(memory_space=pl.ANY),
                  pl.BlockSpec((1, d), lambda i, s: (0, 0)), mod_spec, mod_spec,
                  pl.BlockSpec((d, n), lambda i, s: (0, 0))],
        out_specs=[x_spec, pl.BlockSpec((TM_ROWS, n), lambda i, s: (i, 0))],
        scratch_shapes=[pltpu.VMEM((2, TM_ROWS, d), F32), pltpu.SemaphoreType.DMA((2,))],
    )
    x2, u = pl.pallas_call(
        functools.partial(_combine_nm_tm_kernel, nseq, steps, seq_len, n_steps),
        grid_spec=grid_spec,
        out_shape=[jax.ShapeDtypeStruct((nseq, seq_len, d), F32), jax.ShapeDtypeStruct((t, n), F32)],
        compiler_params=_cparams("arbitrary"),
        name="moe_combine_norm_mod_proj_tm",
    )(slots, x.reshape(nseq, seq_len, d), g2, ys, g.reshape(1, d), sc, sh, w)
    return x2.reshape(t, d), u


def moe_block(xs, routes, mods_l, per_seqs, seq_lens, layer, p, final, defer_combine, hs):
    g = p['norm_g'][layer, 1]
    slots, off, cnt = moe_slots(jnp.concatenate(routes, axis=1))
    t_all = slots.shape[1]
    n_tiles = -(-(t_all // MOE_TS + N_CLS) // MOE_STEP_TILES) * MOE_STEP_TILES
    maps = _tile_maps(off, cnt, n_tiles)
    if hs is None:
        hs = jnp.zeros((n_tiles * MOE_TS, ROW_W), F32)
    bounds = np.cumsum([0] + [x.shape[0] for x in xs])
    group_slots = [slots[0, bounds[i]:bounds[i + 1]] for i in range(len(xs))]
    for x, r, s, sl, ps in zip(xs, routes, group_slots, seq_lens, per_seqs):
        hs = moe_dispatch(x, g, mods_l, r.T, s, hs, sl, ps)
    ys = moe_experts(hs, maps, layer, p['moe_w_gate'], p['moe_w_up'], p['moe_w_down'])
    if defer_combine:
        return [(x, s, ys) for x, s in zip(xs, group_slots)], hs
    return [moe_combine(x, mods_l, s, ys, p['final_g'], final, sl, ps)
            for x, s, sl, ps in zip(xs, group_slots, seq_lens, per_seqs)], hs


def _mixer(x, group, l, mod, p, hy_tables, mod_prev):
    per_seq, nseq, seq_len = group['per_seq'], group['nseq'], group['seq_len']
    extras = None
    deferred = isinstance(x, tuple)
    if deferred and l % 2 == 0:
        x = moe_combine(x[0], mod_prev, x[1], x[2], p['final_g'], False, seq_len, per_seq)
        deferred = False
    if l % 2 == 0:
        e = l // 2
        u = nm_matmul(x, p['norm_g'][l, 0], mod, p['a_in_w'][e].astype(BF16), seq_len, per_seq)
        if group['ctx_k'] is None:
            attn, nk, nv = context_attention(u, nseq, seq_len)
            extras = (nk, nv)
        else:
            attn = neighbourhood_attention(u, group['ctx_k'][:, e], group['ctx_v'][:, e], p['na_rpb'][e],
                                           nseq, seq_len)
        fwd, inv = hy_tables[seq_len]
        spectrum = hyena_spectrum(seq_len, p['hy_w1'][e], p['hy_b1'][e], p['hy_w2'][e], p['hy_b2'][e],
                                  p['hy_w3'][e], p['hy_freq'][e], p['hy_d'][e], fwd)
        hy = hyena_mixer(u, nseq, seq_len, p['hy_short_w'][e], p['hy_short_b'][e], spectrum, fwd, inv)
        w_out = p['a_out_w'][e].astype(BF16)
        x, route = proj_residual([attn, hy], [w_out[:D_A], w_out[D_A:]], x, mod, p['norm_g'][l, 1],
                                 p['router_w'], p['router_b'], seq_len, per_seq)
    else:
        o = l // 2
        w_in = p['c_in_w'][o].astype(BF16)
        if deferred and group['sc_combine']:
            x, u = residual_nm_matmul_tm(x[0], mod_prev, sc_gather_rows(x[2], x[1]), p['norm_g'][l, 0], mod,
                                         w_in, nseq, seq_len, per_seq)
        elif deferred:
            x, u = combine_nm_matmul_tm(x[0], mod_prev, x[1], x[2], p['norm_g'][l, 0], mod, w_in,
                                        nseq, seq_len, per_seq)
        else:
            u = nm_matmul_tm(x, p['norm_g'][l, 0], mod, w_in, nseq, seq_len, per_seq)
        y, extras = rglru_block(u, nseq, seq_len, p['rg_conv_w'][o], p['rg_conv_b'][o], p['rg_wa'][o],
                                p['rg_ba'][o], p['rg_wx'][o], p['rg_bx'][o], p['rg_lam'][o], group['h0'][o])
        x, route = proj_residual_tm(y, p['c_out_w'][o].astype(BF16), x, mod, p['norm_g'][l, 1],
                                    p['router_w'], p['router_b'], nseq, seq_len, per_seq)
    return x, extras, route


def kernel(x_prompt, x_sample, cache_k, cache_v, state_h, c, c_ctx, norm_g, ada_w, ada_b, final_g, a_in_w, a_out_w, na_rpb, hy_short_w, hy_short_b, hy_w1, hy_b1, hy_w2, hy_b2, hy_w3, hy_freq, hy_d, c_in_w, c_out_w, rg_conv_w, rg_conv_b, rg_wa, rg_ba, rg_wx, rg_bx, rg_lam, router_w, router_b, moe_w_gate, moe_w_up, moe_w_down):
    p = dict(norm_g=norm_g, final_g=final_g, a_in_w=a_in_w, a_out_w=a_out_w, na_rpb=na_rpb,
             hy_short_w=hy_short_w, hy_short_b=hy_short_b, hy_w1=hy_w1, hy_b1=hy_b1, hy_w2=hy_w2,
             hy_b2=hy_b2, hy_w3=hy_w3, hy_freq=hy_freq, hy_d=hy_d, c_in_w=c_in_w, c_out_w=c_out_w,
             rg_conv_w=rg_conv_w, rg_conv_b=rg_conv_b, rg_wa=rg_wa, rg_ba=rg_ba, rg_wx=rg_wx, rg_bx=rg_bx,
             rg_lam=rg_lam, router_w=router_w, router_b=router_b, moe_w_gate=moe_w_gate,
             moe_w_up=moe_w_up, moe_w_down=moe_w_down)
    batch, seq, d = x_prompt.shape
    dec_batch, dec_seq, _ = x_sample.shape
    n_odd = DEPTH // 2
    assert 1 + dec_batch <= MOD_ROWS

    cond = jnp.concatenate([c_ctx[None, :], c, jnp.zeros((MOD_ROWS - 1 - dec_batch, d), F32)], axis=0)
    m = modulation(cond, ada_w, ada_b)
    mods = [m[l].reshape(MOD_ROWS * N_MOD, 1, d) for l in range(DEPTH)]

    tables = {}
    for sl in (seq, dec_seq):
        fwd, inv = _dft_tables(sl)
        tables[sl] = (jnp.asarray(fwd).astype(BF16), jnp.asarray(inv).astype(BF16))

    groups = [
        dict(per_seq=False, nseq=batch, seq_len=seq, ctx_k=None, ctx_v=None, sc_combine=False,
             h0=[jnp.zeros((2, batch, D_RNN), F32)] * n_odd),
        dict(per_seq=True, nseq=dec_batch, seq_len=dec_seq, ctx_k=cache_k, ctx_v=cache_v, sc_combine=True,
             h0=[state_h[:, o].transpose(1, 0, 2) for o in range(n_odd)]),
    ]
    xs = [x_prompt.reshape(batch * seq, d), x_sample.reshape(dec_batch * dec_seq, d)]
    k_list, v_list, h_list = [], [], []
    hs = None
    for l in range(DEPTH):
        mixed = [_mixer(x, grp, l, mods[l], p, tables, mods[l - 1] if l else None)
                 for x, grp in zip(xs, groups)]
        if l % 2 == 0:
            k_list.append(mixed[0][1][0])
            v_list.append(mixed[0][1][1])
        else:
            h_list.append(mixed[0][1].transpose(1, 0, 2))
        xs, hs = moe_block([mx[0] for mx in mixed], [mx[2] for mx in mixed], mods[l],
                           [grp['per_seq'] for grp in groups], [grp['seq_len'] for grp in groups],
                           l, p, l == DEPTH - 1, l < DEPTH - 1, hs)
    new_k = jnp.stack(k_list, axis=1)
    new_v = jnp.stack(v_list, axis=1)
    new_h = jnp.stack(h_list, axis=1)
    return (xs[0].reshape(batch, seq, d), xs[1].reshape(dec_batch, dec_seq, d), new_k, new_v, new_h)
```

```python
import functools
import math

import numpy as np
import jax
import jax.numpy as jnp
from jax import lax
from jax.experimental import pallas as pl
from jax.experimental.pallas import tpu as pltpu
from jax.experimental.pallas import tpu_sc as plsc

F32 = jnp.float32
BF16 = jnp.bfloat16

D_MODEL = 1024
DEPTH = 2
GRID_W = 64
EPS = 1e-6
NEG_INF = -1e30
NA_HEADS = 8
HEAD_DIM = 64
D_A = NA_HEADS * HEAD_DIM
WIN_ROWS = 8
WIN_COLS = 16
D_B = D_MODEL - D_A
HY_ORDER = 2
HY_EMB = 33
HY_BANDS = (HY_EMB - 1) // 2
HY_FFN = 64
HY_DECAY_TARGET = 1e-2
HY_FAST_PCT = 0.3
HY_SLOW_PCT = 1.5
D_RNN = D_MODEL
RG_BLOCK = 64
RG_C = 8.0
N_EXPERTS = 16
N_GROUPS = 4
EXPERTS_PER_GROUP = N_EXPERTS // N_GROUPS
D_EXPERT = 512

LANES = 128
VMEM_LIMIT = 56 * 1024 * 1024
N_MOD = 6
MOD_ROWS = 16


def _cparams(*sem):
    return pltpu.CompilerParams(dimension_semantics=sem, vmem_limit_bytes=VMEM_LIMIT)


def _dot(a, b):
    return jnp.dot(a, b, preferred_element_type=F32)


def _dot_nt(a, b):
    return lax.dot_general(a, b, (((1,), (1,)), ((), ())), preferred_element_type=F32)


def _sigmoid(x):
    return 0.5 * jnp.tanh(0.5 * x) + 0.5


def _normmod(x, g, sc, sh):
    ms = jnp.mean(x * x, axis=-1, keepdims=True)
    return (x * lax.rsqrt(ms + EPS) * g) * (1.0 + sc) + sh


def _mod_spec(chunk, tm, seq_len, per_seq):
    if per_seq:
        return pl.BlockSpec((1, 1, D_MODEL), lambda i, *_: ((1 + (i * tm) // seq_len) * N_MOD + chunk, 0, 0))
    return pl.BlockSpec((1, 1, D_MODEL), lambda i, *_: (chunk, 0, 0))


def _mod_kernel(c_ref, w_ref, b_ref, o_ref):
    s = c_ref[...]
    s = s * jax.nn.sigmoid(s)
    o_ref[0] = _dot(s.astype(BF16), w_ref[0].astype(BF16)) + b_ref[0]


def modulation(cond, ada_w, ada_b):
    n = ada_w.shape[-1]
    tn = n // 4
    assert tn % LANES == 0
    return pl.pallas_call(
        _mod_kernel,
        grid=(DEPTH, n // tn),
        in_specs=[pl.BlockSpec((MOD_ROWS, D_MODEL), lambda l, j: (0, 0)),
                  pl.BlockSpec((1, D_MODEL, tn), lambda l, j: (l, 0, j)),
                  pl.BlockSpec((1, 1, tn), lambda l, j: (l, 0, j))],
        out_specs=pl.BlockSpec((1, MOD_ROWS, tn), lambda l, j: (l, 0, j)),
        out_shape=jax.ShapeDtypeStruct((DEPTH, MOD_ROWS, n), F32),
        compiler_params=_cparams("arbitrary", "arbitrary"),
        name="modulation",
    )(cond, ada_w, ada_b.reshape(DEPTH, 1, n))


def _nm_matmul_kernel(x_ref, g_ref, sc_ref, sh_ref, w_ref, o_ref):
    h = _normmod(x_ref[...], g_ref[...], sc_ref[0], sh_ref[0])
    o_ref[...] = _dot(h.astype(BF16), w_ref[...])


def nm_matmul(x, g, mod, w, seq_len, per_seq, tm=512):
    t, d = x.shape
    n = w.shape[1]
    return pl.pallas_call(
        _nm_matmul_kernel,
        grid=(t // tm,),
        in_specs=[pl.BlockSpec((tm, d), lambda i: (i, 0)),
                  pl.BlockSpec((1, d), lambda i: (0, 0)),
                  _mod_spec(1, tm, seq_len, per_seq),
                  _mod_spec(0, tm, seq_len, per_seq),
                  pl.BlockSpec((d, n), lambda i: (0, 0))],
        out_specs=pl.BlockSpec((tm, n), lambda i: (i, 0)),
        out_shape=jax.ShapeDtypeStruct((t, n), F32),
        compiler_params=_cparams("arbitrary"),
        name="norm_mod_proj",
    )(x, g.reshape(1, d), mod, mod, w)


def _proj_res_kernel(n_act, *refs):
    acts = refs[:n_act]
    ws = refs[n_act:2 * n_act]
    x_ref, g_ref, ng_ref, sc2_ref, sh2_ref, rw_ref, rb_ref, o_ref, route_ref = refs[2 * n_act:]
    acc = _dot(acts[0][...].astype(BF16), ws[0][...])
    for a, w in zip(acts[1:], ws[1:]):
        acc += _dot(a[...].astype(BF16), w[...])
    x = x_ref[...] + g_ref[0] * acc
    o_ref[...] = x
    route_ref[...] = _route_record(_normmod(x, ng_ref[...], sc2_ref[0], sh2_ref[0]), rw_ref[...], rb_ref[...])


def proj_residual(acts, ws, x, mod, norm2_g, router_w, router_b, seq_len, per_seq, tm=512):
    t, d = x.shape
    in_specs = [pl.BlockSpec((tm, a.shape[1]), lambda i: (i, 0)) for a in acts]
    in_specs += [pl.BlockSpec(w.shape, lambda i: (0, 0)) for w in ws]
    in_specs += [pl.BlockSpec((tm, d), lambda i: (i, 0)), _mod_spec(2, tm, seq_len, per_seq),
                 pl.BlockSpec((1, d), lambda i: (0, 0)),
                 _mod_spec(4, tm, seq_len, per_seq), _mod_spec(3, tm, seq_len, per_seq),
                 pl.BlockSpec((N_EXPERTS, d), lambda i: (0, 0)),
                 pl.BlockSpec((N_EXPERTS, 1), lambda i: (0, 0))]
    return pl.pallas_call(
        functools.partial(_proj_res_kernel, len(acts)),
        grid=(t // tm,),
        in_specs=in_specs,
        out_specs=[pl.BlockSpec((tm, d), lambda i: (i, 0)), pl.BlockSpec((ROUTE_ROWS, tm), lambda i: (0, i))],
        out_shape=[jax.ShapeDtypeStruct((t, d), F32), jax.ShapeDtypeStruct((ROUTE_ROWS, t), F32)],
        compiler_params=_cparams("arbitrary"),
        name="proj_residual_route",
    )(*acts, *ws, x, mod, norm2_g.reshape(1, d), mod, mod, router_w.T, router_b.reshape(N_EXPERTS, 1))


def _row_permutation(nseq, steps, to_time_major):
    n = nseq * steps
    i = lax.broadcasted_iota(jnp.int32, (n, n), 0)
    j = lax.broadcasted_iota(jnp.int32, (n, n), 1)
    if to_time_major:
        src = (i % nseq) * steps + i // nseq
    else:
        src = (i % steps) * nseq + i // steps
    return (j == src).astype(BF16)


def _nm_matmul_tm_kernel(nseq, steps, x_ref, g_ref, sc_ref, sh_ref, w_ref, o_ref):
    h = _normmod(x_ref[...], g_ref[...], sc_ref[...], sh_ref[...])
    h = h.reshape(nseq * steps, h.shape[-1]).astype(BF16)
    h = _dot(_row_permutation(nseq, steps, True), h).astype(BF16)
    o_ref[...] = _dot(h, w_ref[...])


def _group_mod(mod, chunk, nseq, per_seq):
    rows = mod.reshape(MOD_ROWS, N_MOD, 1, D_MODEL)
    return rows[1:1 + nseq, chunk] if per_seq else rows[0:1, chunk]


def nm_matmul_tm(x, g, mod, w, nseq, seq_len, per_seq):
    t, d = x.shape
    n = w.shape[1]
    steps = TM_ROWS // nseq
    sc = _group_mod(mod, 1, nseq, per_seq)
    sh = _group_mod(mod, 0, nseq, per_seq)
    mod_spec = pl.BlockSpec(sc.shape, lambda i: (0, 0, 0))
    return pl.pallas_call(
        functools.partial(_nm_matmul_tm_kernel, nseq, steps),
        grid=(seq_len // steps,),
        in_specs=[pl.BlockSpec((nseq, steps, d), lambda i: (0, i, 0)),
                  pl.BlockSpec((1, d), lambda i: (0, 0)),
                  mod_spec, mod_spec,
                  pl.BlockSpec((d, n), lambda i: (0, 0))],
        out_specs=pl.BlockSpec((TM_ROWS, n), lambda i: (i, 0)),
        out_shape=jax.ShapeDtypeStruct((t, n), F32),
        compiler_params=_cparams("arbitrary"),
        name="norm_mod_proj_tm",
    )(x.reshape(nseq, seq_len, d), g.reshape(1, d), sc, sh, w)


def _proj_res_tm_kernel(nseq, steps, y_ref, w_ref, x_ref, g_ref, ng_ref, sc2_ref, sh2_ref, rw_ref, rb_ref,
                        o_ref, route_ref):
    y = _dot(_row_permutation(nseq, steps, False), y_ref[...].astype(BF16)).astype(BF16)
    acc = _dot(y, w_ref[...])
    x = x_ref[...] + g_ref[...] * acc.reshape(nseq, steps, acc.shape[-1])
    o_ref[...] = x
    h = _normmod(x, ng_ref[...], sc2_ref[...], sh2_ref[...]).reshape(nseq * steps, x.shape[-1])
    route_ref[0] = _route_record(h, rw_ref[...], rb_ref[...])


def proj_residual_tm(y, w, x, mod, norm2_g, router_w, router_b, nseq, seq_len, per_seq):
    t, d = x.shape
    steps = TM_ROWS // nseq
    n_steps = seq_len // steps
    g1 = _group_mod(mod, 2, nseq, per_seq)
    sc2 = _group_mod(mod, 4, nseq, per_seq)
    sh2 = _group_mod(mod, 3, nseq, per_seq)
    mod_spec = pl.BlockSpec(g1.shape, lambda i: (0, 0, 0))
    out, route = pl.pallas_call(
        functools.partial(_proj_res_tm_kernel, nseq, steps),
        grid=(n_steps,),
        in_specs=[pl.BlockSpec((TM_ROWS, y.shape[1]), lambda i: (i, 0)),
                  pl.BlockSpec(w.shape, lambda i: (0, 0)),
                  pl.BlockSpec((nseq, steps, d), lambda i: (0, i, 0)),
                  mod_spec,
                  pl.BlockSpec((1, d), lambda i: (0, 0)),
                  mod_spec, mod_spec,
                  pl.BlockSpec((N_EXPERTS, d), lambda i: (0, 0)),
                  pl.BlockSpec((N_EXPERTS, 1), lambda i: (0, 0))],
        out_specs=[pl.BlockSpec((nseq, steps, d), lambda i: (0, i, 0)),
                   pl.BlockSpec((1, ROUTE_ROWS, TM_ROWS), lambda i: (i, 0, 0))],
        out_shape=[jax.ShapeDtypeStruct((nseq, seq_len, d), F32),
                   jax.ShapeDtypeStruct((n_steps, ROUTE_ROWS, TM_ROWS), F32)],
        compiler_params=_cparams("arbitrary"),
        name="proj_residual_tm_route",
    )(y, w, x.reshape(nseq, seq_len, d), g1, norm2_g.reshape(1, d), sc2, sh2, router_w.T,
      router_b.reshape(N_EXPERTS, 1))
    route = route.reshape(n_steps, ROUTE_ROWS, nseq, steps).transpose(1, 2, 0, 3).reshape(ROUTE_ROWS, t)
    return out.reshape(t, d), route


CTX_STEP_ROWS = 1024


def _ctx_attn_kernel(seq_len, n_seq, q_ref, k_ref, v_ref, o_ref, nk_ref, nv_ref):
    scale = HEAD_DIM ** -0.5
    per_tile = LANES // HEAD_DIM
    lane = lax.broadcasted_iota(jnp.int32, (1, LANES), 1)
    for b in range(n_seq):
        rows = slice(b * seq_len, (b + 1) * seq_len)
        for hp in range(NA_HEADS // per_tile):
            sl = slice(hp * LANES, (hp + 1) * LANES)
            q, k, v = q_ref[rows, sl] * scale, k_ref[rows, sl], v_ref[rows, sl]
            kb, vb = k.astype(BF16), v.astype(BF16)
            out = None
            for j in range(per_tile):
                h = hp * per_tile + j
                nk_ref[b, h] = k[:, j * HEAD_DIM:(j + 1) * HEAD_DIM]
                nv_ref[b, h] = v[:, j * HEAD_DIM:(j + 1) * HEAD_DIM]
                mine = lane // HEAD_DIM == j
                s = _dot_nt(jnp.where(mine, q, 0.0).astype(BF16), kb)
                p = jnp.exp(s - jnp.max(s, axis=-1, keepdims=True))
                o = _dot(p.astype(BF16), vb) / jnp.sum(p, axis=-1, keepdims=True)
                out = o if out is None else jnp.where(mine, o, out)
            o_ref[rows, sl] = out.astype(o_ref.dtype)


def context_attention(u, nseq, seq_len):
    t = u.shape[0]
    per_step = max(1, CTX_STEP_ROWS // seq_len)
    assert nseq % per_step == 0
    rows = per_step * seq_len
    kv_shape = jax.ShapeDtypeStruct((nseq, NA_HEADS, seq_len, HEAD_DIM), F32)
    kv_spec = pl.BlockSpec((per_step, NA_HEADS, seq_len, HEAD_DIM), lambda b: (b, 0, 0, 0))
    return pl.pallas_call(
        functools.partial(_ctx_attn_kernel, seq_len, per_step),
        grid=(nseq // per_step,),
        in_specs=[pl.BlockSpec((rows, D_A), lambda b: (b, 0)),
                  pl.BlockSpec((rows, D_A), lambda b: (b, 1)),
                  pl.BlockSpec((rows, D_A), lambda b: (b, 2))],
        out_specs=[pl.BlockSpec((rows, D_A), lambda b: (b, 0)), kv_spec, kv_spec],
        out_shape=[jax.ShapeDtypeStruct((t, D_A), BF16), kv_shape, kv_shape],
        compiler_params=_cparams("arbitrary"),
        name="context_attention",
    )(u, u, u)


N_DR = 2 * WIN_ROWS - 1
N_DC = 2 * WIN_COLS - 1
N_DC_PAD = 32


def _na_col_tables():
    cols = np.arange(GRID_W)
    col_start = np.clip(cols - WIN_COLS // 2, 0, GRID_W - WIN_COLS)
    col_in = (cols[None, :] >= col_start[:, None]) & (cols[None, :] < col_start[:, None] + WIN_COLS)
    dc = np.clip(cols[None, :] - cols[:, None], 1 - WIN_COLS, WIN_COLS - 1) + WIN_COLS - 1
    onehot = (dc.reshape(1, -1) == np.arange(N_DC_PAD)[:, None]).astype(np.float32)
    return onehot, col_in.reshape(1, -1).astype(np.float32)


def _na_bias_kernel(r_ref, e_ref, m_ref, o_ref):
    t = jnp.dot(r_ref[...], e_ref[...], precision=lax.Precision.HIGHEST, preferred_element_type=F32)
    o_ref[...] = jnp.where(m_ref[...] > 0.0, t, NEG_INF)


def na_bias_table(rpb):
    onehot, col_in = _na_col_tables()
    n_rows = NA_HEADS * N_DR
    assert n_rows <= LANES
    r = jnp.zeros((LANES, N_DC_PAD), F32).at[:n_rows, :N_DC].set(rpb.reshape(n_rows, N_DC).astype(F32))
    t = pl.pallas_call(
        _na_bias_kernel,
        out_shape=jax.ShapeDtypeStruct((LANES, GRID_W * GRID_W), F32),
        name="na_bias_table",
    )(r, jnp.asarray(onehot), jnp.asarray(col_in))
    t = t[:n_rows].reshape(NA_HEADS, N_DR, GRID_W, GRID_W)
    return jnp.concatenate([t[:, :-1], t[:, 1:]], axis=-1)


def _na_kernel(rows, q_ref, k_ref, v_ref, ck_ref, cv_ref, bias_ref, o_ref,
               q_s, k_s, v_s, ck_s, cv_s, s_s, p_s, den_s, o_s):
    scale = HEAD_DIM ** -0.5
    n_lat = WIN_ROWS * GRID_W
    per_tile = LANES // HEAD_DIM
    n_pairs = NA_HEADS // per_tile
    lane = lax.broadcasted_iota(jnp.int32, (1, LANES), 1)
    for hp in range(n_pairs):
        sl = slice(hp * LANES, (hp + 1) * LANES)
        q_s[hp] = (q_ref[:, sl] * scale).astype(BF16)
        k_s[hp] = k_ref[:, sl].astype(BF16)
        v_s[hp] = v_ref[:, sl].astype(BF16)
        heads = range(hp * per_tile, (hp + 1) * per_tile)
        ck_s[hp] = jnp.concatenate([ck_ref[0, h] for h in heads], axis=1).astype(BF16)
        cv_s[hp] = jnp.concatenate([cv_ref[0, h] for h in heads], axis=1).astype(BF16)

    def window(r):
        start = min(max(r - WIN_ROWS // 2, 0), rows - WIN_ROWS)
        return start, start - r + WIN_ROWS - 1

    def pair_body(hp, carry):
        for j in range(per_tile):
            h = hp * per_tile + j
            mine = lane // HEAD_DIM == j
            for r in range(rows):
                start, off = window(r)
                rs = slice(r * GRID_W, (r + 1) * GRID_W)
                q = jnp.where(mine, q_s[hp, rs, :], 0.0).astype(BF16)
                bias = jnp.concatenate([bias_ref[h, off + 2 * i] for i in range(WIN_ROWS // 2)], axis=1)
                s_s[rs, 0:n_lat] = _dot_nt(q, k_s[hp, start * GRID_W:start * GRID_W + n_lat, :]) + bias
                s_s[rs, n_lat:] = _dot_nt(q, ck_s[hp])
            for r in range(rows):
                rs = slice(r * GRID_W, (r + 1) * GRID_W)
                s = s_s[rs, :]
                p = jnp.exp(s - jnp.max(s, axis=-1, keepdims=True))
                den_s[rs, :] = jnp.sum(p, axis=-1, keepdims=True)
                p_s[rs, :] = p.astype(BF16)
            for r in range(rows):
                start, _ = window(r)
                rs = slice(r * GRID_W, (r + 1) * GRID_W)
                o = (_dot(p_s[rs, 0:n_lat], v_s[hp, start * GRID_W:start * GRID_W + n_lat, :])
                     + _dot(p_s[rs, n_lat:], cv_s[hp])) / den_s[rs, :]
                o_s[hp, rs, :] = o if j == 0 else jnp.where(mine, o, o_s[hp, rs, :])
        return carry

    lax.fori_loop(0, n_pairs, pair_body, 0)
    for hp in range(n_pairs):
        o_ref[:, hp * LANES:(hp + 1) * LANES] = o_s[hp].astype(o_ref.dtype)


def neighbourhood_attention(u, ctx_k, ctx_v, rpb, nseq, seq_len):
    t = u.shape[0]
    rows = seq_len // GRID_W
    assert rows >= WIN_ROWS and WIN_ROWS % 2 == 0
    past = ctx_k.shape[2]
    bias = na_bias_table(rpb)
    ctx_spec = pl.BlockSpec((1, NA_HEADS, past, HEAD_DIM), lambda b: (b, 0, 0, 0))
    return pl.pallas_call(
        functools.partial(_na_kernel, rows),
        grid=(nseq,),
        in_specs=[pl.BlockSpec((seq_len, D_A), lambda b: (b, 0)),
                  pl.BlockSpec((seq_len, D_A), lambda b: (b, 1)),
                  pl.BlockSpec((seq_len, D_A), lambda b: (b, 2)),
                  ctx_spec, ctx_spec,
                  pl.BlockSpec(bias.shape, lambda b: (0, 0, 0, 0))],
        out_specs=pl.BlockSpec((seq_len, D_A), lambda b: (b, 0)),
        out_shape=jax.ShapeDtypeStruct((t, D_A), BF16),
        scratch_shapes=[pltpu.VMEM((D_A // LANES, seq_len, LANES), BF16)] * 3
        + [pltpu.VMEM((D_A // LANES, past, LANES), BF16)] * 2
        + [pltpu.VMEM((seq_len, WIN_ROWS * GRID_W + past), F32),
           pltpu.VMEM((seq_len, WIN_ROWS * GRID_W + past), BF16),
           pltpu.VMEM((seq_len, 1), F32),
           pltpu.VMEM((D_A // LANES, seq_len, LANES), F32)],
        compiler_params=_cparams("arbitrary"),
        name="neighbourhood_attention",
    )(u, u, u, ctx_k, ctx_v, bias)


HY_STEP_ROWS = 1024


def _dft_tables(seq_len):
    n = 2 * seq_len
    f = np.arange(seq_len, dtype=np.int64)
    ang = (np.outer(f, f) % n).astype(np.float64) * (math.pi / seq_len)
    cos, sin = np.cos(ang), np.sin(ang)
    alt = np.where(f % 2 == 0, 1.0, -1.0)
    s_fwd = -sin
    s_fwd[0, :] = alt
    fwd = np.concatenate([cos, s_fwd], axis=0)
    wf = np.where(f == 0, 1.0, 2.0) / n
    ci = cos.T * wf[None, :]
    si = -sin.T * wf[None, :]
    si[:, 0] = alt / n
    inv = np.concatenate([ci, si], axis=1)
    return fwd.astype(np.float32), inv.astype(np.float32)


def _hyena_feats(seq_len):
    t = np.linspace(0.0, 1.0, seq_len, dtype=np.float32)[:, None]
    w = (2.0 * math.pi * np.arange(seq_len, dtype=np.float32)[:, None] / seq_len).astype(np.float32)
    f = np.linspace(1e-4, HY_BANDS - 1, HY_BANDS, dtype=np.float32)[None, :]
    z = np.concatenate([t, np.cos(f * w), -np.sin(f * w)], axis=-1).astype(np.float32)
    max_decay = math.log(HY_DECAY_TARGET) / HY_FAST_PCT
    min_decay = math.log(HY_DECAY_TARGET) / HY_SLOW_PCT
    deltas = np.abs(np.linspace(min_decay, max_decay, D_B, dtype=np.float32))[None, :]
    return z, t, deltas


def _hy_filter_kernel(seq_len, z_ref, t_ref, dl_ref, w1_ref, b1_ref, w2_ref, b2_ref, w3_ref, fr_ref,
                      d_ref, fwd_ref, g_ref):
    hp = lax.Precision.HIGHEST
    h = jnp.sin(fr_ref[0:1, :] * (jnp.dot(z_ref[...], w1_ref[...], precision=hp) + b1_ref[...]))
    h = jnp.sin(fr_ref[1:2, :] * (jnp.dot(h, w2_ref[...], precision=hp) + b2_ref[...]))
    h = jnp.dot(h, w3_ref[...], precision=hp)
    decay = jnp.exp(-t_ref[...] * dl_ref[...])
    row0 = lax.broadcasted_iota(jnp.int32, (seq_len, D_B), 0) == 0
    sums, diffs = [], []
    for n in range(HY_ORDER):
        hf = h[:, (2 * n) * D_B:(2 * n + 1) * D_B] * decay
        hb = h[:, (2 * n + 1) * D_B:(2 * n + 2) * D_B] * decay
        gp = jnp.where(row0, hf + hb + d_ref[n:n + 1, :], hf)
        gm = jnp.where(row0, 0.0, hb)
        sums.append(gp + gm)
        diffs.append(gp - gm)
    rhs = jnp.concatenate(sums + diffs, axis=1).astype(BF16)
    spec = _dot(fwd_ref[...], rhs)
    for n in range(HY_ORDER):
        a = spec[:, n * D_B:(n + 1) * D_B]
        b = spec[:, (HY_ORDER + n) * D_B:(HY_ORDER + n + 1) * D_B]
        g_ref[n, 0:seq_len, :] = a[0:seq_len]
        g_ref[n, seq_len:, :] = jnp.where(row0, a[seq_len:], b[seq_len:])


def hyena_spectrum(seq_len, w1, b1, w2, b2, w3, freq, d, fwd):
    z, t, deltas = _hyena_feats(seq_len)
    return pl.pallas_call(
        functools.partial(_hy_filter_kernel, seq_len),
        out_shape=jax.ShapeDtypeStruct((HY_ORDER, 2 * seq_len, D_B), F32),
        compiler_params=pltpu.CompilerParams(vmem_limit_bytes=VMEM_LIMIT),
        name="hyena_spectrum",
    )(jnp.asarray(z), jnp.asarray(t), jnp.asarray(deltas), w1, b1.reshape(1, -1), w2, b2.reshape(1, -1),
      w3, freq, d, fwd)


def _hyena_kernel(seq_len, n_seq, u_ref, sw_ref, sb_ref, g_ref, fwd_ref, inv_ref, o_ref):
    t_idx = lax.broadcasted_iota(jnp.int32, (seq_len, u_ref.shape[1]), 0)
    row0 = lax.broadcasted_iota(jnp.int32, (seq_len, D_B), 0) == 0
    for s in range(n_seq):
        rows = slice(s * seq_len, (s + 1) * seq_len)
        u = u_ref[rows, :]
        prev = jnp.where(t_idx == 0, 0.0, pltpu.roll(u, 1, axis=0))
        nxt = jnp.where(t_idx == seq_len - 1, 0.0, pltpu.roll(u, seq_len - 1, axis=0))
        u = prev * sw_ref[0:1, :] + u * sw_ref[1:2, :] + nxt * sw_ref[2:3, :] + sb_ref[...]
        z = u[:, 0:D_B]
        for n in range(HY_ORDER):
            spec = _dot(fwd_ref[...], z.astype(BF16))
            ure, uim = spec[0:seq_len], spec[seq_len:]
            gre, gim = g_ref[n, 0:seq_len, :], g_ref[n, seq_len:, :]
            pim = uim * gim
            yre = ure * gre - jnp.where(row0, 0.0, pim)
            yim = jnp.where(row0, pim, ure * gim + uim * gre)
            y = jnp.concatenate([yre, yim], axis=0).astype(BF16)
            z = u[:, (n + 1) * D_B:(n + 2) * D_B] * _dot(inv_ref[...], y)
        o_ref[rows, :] = z.astype(o_ref.dtype)


def hyena_mixer(u, nseq, seq_len, short_w, short_b, spectrum, fwd, inv):
    t = u.shape[0]
    width = (HY_ORDER + 1) * D_B
    col_block = (3 * D_A) // width
    assert col_block * width == 3 * D_A
    per_step = max(1, HY_STEP_ROWS // seq_len)
    assert nseq % per_step == 0
    return pl.pallas_call(
        functools.partial(_hyena_kernel, seq_len, per_step),
        grid=(nseq // per_step,),
        in_specs=[pl.BlockSpec((per_step * seq_len, width), lambda b: (b, col_block)),
                  pl.BlockSpec(short_w.shape, lambda b: (0, 0)),
                  pl.BlockSpec((1, width), lambda b: (0, 0)),
                  pl.BlockSpec(spectrum.shape, lambda b: (0, 0, 0)),
                  pl.BlockSpec(fwd.shape, lambda b: (0, 0)),
                  pl.BlockSpec(inv.shape, lambda b: (0, 0))],
        out_specs=pl.BlockSpec((per_step * seq_len, D_B), lambda b: (b, 0)),
        out_shape=jax.ShapeDtypeStruct((t, D_B), BF16),
        compiler_params=_cparams("arbitrary"),
        name="hyena_mixer",
    )(u, short_w, short_b.reshape(1, width), spectrum, fwd, inv)


RG_CB = LANES
RG_CHUNK = 512
TM_ROWS = 512


def _rglru_kernel(nseq, seq_len, gate_ref, xr_ref, cw_ref, cb_ref, wg_ref, bg_ref, lam_ref, h0_ref,
                  y_ref, fin_ref, xp_ref, a_f, b_f, a_b, b_b):
    t_tot = nseq * seq_len
    c = RG_CB
    pad = 2 * nseq
    xp_ref[0:pad, :] = jnp.zeros((pad, c), F32)
    xp_ref[pad + t_tot:, :] = jnp.zeros((pad, c), F32)
    xp_ref[pad:pad + t_tot, :] = xr_ref[...]
    nl = -lam_ref[...]
    sp = jnp.maximum(nl, 0.0) + jnp.log1p(jnp.exp(-jnp.abs(nl)))
    k2 = (-0.5 * RG_C * math.log2(math.e)) * sp

    def gate_chunk(ci, carry):
        r0 = pl.multiple_of(ci * RG_CHUNK, RG_CHUNK)
        xc = xp_ref[pl.ds(r0, RG_CHUNK), :] * cw_ref[0:1, :]
        for j in range(1, cw_ref.shape[0]):
            xc = xc + xp_ref[pl.ds(r0 + j * nseq, RG_CHUNK), :] * cw_ref[j:j + 1, :]
        xc = xc + cb_ref[...]
        gts = _dot(xc.astype(BF16), wg_ref[0]) + bg_ref[...]
        x_half = 0.5 * xc
        for d, (a_ref, b_ref) in enumerate(((a_f, b_f), (a_b, b_b))):
            t_r = jnp.tanh(gts[:, (2 * d) * c:(2 * d + 1) * c])
            t_i = jnp.tanh(gts[:, (2 * d + 1) * c:(2 * d + 2) * c])
            a = jnp.exp2(t_r * k2[d:d + 1, :] + k2[d:d + 1, :])
            a_ref[pl.ds(r0, RG_CHUNK), :] = a
            y = 1.0 - a * a
            root = jnp.where(y > 0.0, y * lax.rsqrt(y), 0.0)
            b_ref[pl.ds(r0, RG_CHUNK), :] = root * ((t_i + 1.0) * x_half)
        return carry

    lax.fori_loop(0, t_tot // RG_CHUNK, gate_chunk, 0)

    def scan_step(t, carry):
        hf, hb = carry
        rows_f = pl.ds(pl.multiple_of(t * nseq, nseq), nseq)
        rows_b = pl.ds(pl.multiple_of((seq_len - 1 - t) * nseq, nseq), nseq)
        hf = a_f[rows_f, :] * hf + b_f[rows_f, :]
        hb = a_b[rows_b, :] * hb + b_b[rows_b, :]
        b_f[rows_f, :] = hf
        b_b[rows_b, :] = hb
        return hf, hb

    hf, hb = lax.fori_loop(0, seq_len, scan_step, (h0_ref[0], h0_ref[1]), unroll=8)
    fin_ref[0] = hf
    fin_ref[1] = hb

    def out_chunk(ci, carry):
        rs = pl.ds(pl.multiple_of(ci * RG_CHUNK, RG_CHUNK), RG_CHUNK)
        y_ref[rs, :] = ((b_f[rs, :] + b_b[rs, :]) * jax.nn.gelu(gate_ref[rs, :])).astype(y_ref.dtype)
        return carry

    lax.fori_loop(0, t_tot // RG_CHUNK, out_chunk, 0)


def _rg_gate_weights(wa, wx):
    per_step = RG_CB // RG_BLOCK
    steps = D_RNN // RG_CB
    mats = []
    for d in range(2):
        for w in (wa[d], wx[d]):
            w = w.reshape(steps, per_step, RG_BLOCK, RG_BLOCK)
            eye = jnp.eye(per_step, dtype=w.dtype)
            m = jnp.einsum('spde,pq->spdqe', w, eye).reshape(steps, RG_CB, RG_CB)
            mats.append(m)
    return (0.5 * jnp.concatenate(mats, axis=-1)).astype(BF16)


def rglru_block(u, nseq, seq_len, conv_w, conv_b, wa, ba, wx, bx, lam, h0):
    t = u.shape[0]
    c = RG_CB
    steps = D_RNN // c
    wg = _rg_gate_weights(wa, wx)
    bg = jnp.stack([ba[0], bx[0], ba[1], bx[1]], axis=0).reshape(4, steps, c)
    bg = 0.5 * bg.transpose(1, 0, 2).reshape(steps, 1, 4 * c)
    y, fin = pl.pallas_call(
        functools.partial(_rglru_kernel, nseq, seq_len),
        grid=(steps,),
        in_specs=[pl.BlockSpec((t, c), lambda j: (0, j)),
                  pl.BlockSpec((t, c), lambda j: (0, steps + j)),
                  pl.BlockSpec((conv_w.shape[0], c), lambda j: (0, j)),
                  pl.BlockSpec((1, c), lambda j: (0, j)),
                  pl.BlockSpec((1, c, 4 * c), lambda j: (j, 0, 0)),
                  pl.BlockSpec((None, 1, 4 * c), lambda j: (j, 0, 0)),
                  pl.BlockSpec((2, c), lambda j: (0, j)),
                  pl.BlockSpec((2, nseq, c), lambda j: (0, 0, j))],
        out_specs=[pl.BlockSpec((t, c), lambda j: (0, j)),
                   pl.BlockSpec((2, nseq, c), lambda j: (0, 0, j))],
        out_shape=[jax.ShapeDtypeStruct((t, D_RNN), BF16),
                   jax.ShapeDtypeStruct((2, nseq, D_RNN), F32)],
        scratch_shapes=[pltpu.VMEM((t + 4 * nseq, c), F32)] + [pltpu.VMEM((t, c), F32)] * 4,
        compiler_params=_cparams("arbitrary"),
        name="rglru_block",
    )(u, u, conv_w, conv_b.reshape(1, -1), wg, bg, lam, h0)
    return y, fin


def _route_record(h, w, rb):
    h_hi = h.astype(BF16)
    h_lo = (h - h_hi.astype(F32)).astype(BF16)
    w_hi = w.astype(BF16)
    w_lo = (w - w_hi.astype(F32)).astype(BF16)
    logits = _dot_nt(w_hi, h_hi) + (_dot_nt(w_lo, h_hi) + _dot_nt(w_hi, h_lo))
    scores = jax.nn.sigmoid(logits)
    sel = scores + rb
    row = [sel[e:e + 1, :] for e in range(N_EXPERTS)]
    gs = []
    for g in range(N_GROUPS):
        r = row[g * EXPERTS_PER_GROUP:(g + 1) * EXPERTS_PER_GROUP]
        best_pair = None
        for i in range(EXPERTS_PER_GROUP):
            for j in range(i + 1, EXPERTS_PER_GROUP):
                s = r[i] + r[j]
                best_pair = s if best_pair is None else jnp.maximum(best_pair, s)
        gs.append(best_pair)
    best = jnp.zeros_like(gs[0], dtype=jnp.int32)
    top = gs[0]
    for g in range(1, N_GROUPS):
        better = gs[g] > top
        best = jnp.where(better, g, best)
        top = jnp.where(better, gs[g], top)
    picked = []
    for e in range(N_EXPERTS):
        g = e // EXPERTS_PER_GROUP
        rank = jnp.zeros_like(best)
        for o in range(g * EXPERTS_PER_GROUP, (g + 1) * EXPERTS_PER_GROUP):
            if o == e:
                continue
            ahead = (row[o] > row[e]) | ((row[o] == row[e]) & (o < e))
            rank = rank + ahead.astype(jnp.int32)
        picked.append((best == g) & (rank < 2))
    den = jnp.zeros_like(gs[0])
    for e in range(N_EXPERTS):
        den = den + jnp.where(picked[e], scores[e:e + 1, :], 0.0)
    gate = [jnp.where(picked[e], scores[e:e + 1, :] / den, 0.0) for e in range(N_EXPERTS)]
    cls = jnp.zeros_like(den)
    w_a = jnp.zeros_like(den)
    w_b = jnp.zeros_like(den)
    for g in range(N_GROUPS):
        for pi, (a, b) in enumerate(MOE_PAIRS):
            ea, eb = g * EXPERTS_PER_GROUP + a, g * EXPERTS_PER_GROUP + b
            both = picked[ea] & picked[eb]
            cls = jnp.where(both, float(g * len(MOE_PAIRS) + pi), cls)
            w_a = jnp.where(both, gate[ea], w_a)
            w_b = jnp.where(both, gate[eb], w_b)
    return jnp.concatenate([cls, w_a, w_b, jnp.zeros((ROUTE_ROWS - 3, cls.shape[1]), F32)], axis=0)


MOE_PAIRS = ((0, 1), (0, 2), (0, 3), (1, 3), (1, 2), (2, 3))
N_CLS = N_GROUPS * len(MOE_PAIRS)
CLS_PAD = 32
ROUTE_ROWS = 8
MOE_TS = 256
MOE_TM = 256
MOE_STEP_TILES = 4
SLOT_BLK = 512
ROW_W = D_MODEL + LANES


def _slots_kernel(n_blk, route_ref, slot_ref, off_ref, cnt_ref):
    cid = lax.broadcasted_iota(jnp.int32, (CLS_PAD, SLOT_BLK), 0).astype(F32)

    def members(j):
        cls = route_ref[0:1, pl.ds(pl.multiple_of(j * SLOT_BLK, SLOT_BLK), SLOT_BLK)]
        return (cid == cls).astype(F32)

    def count(j, cnt):
        return cnt + jnp.sum(members(j), axis=1, keepdims=True)

    cnt = lax.fori_loop(0, n_blk, count, jnp.zeros((CLS_PAD, 1), F32))
    cnt = jnp.broadcast_to(cnt, (CLS_PAD, LANES))
    padded = jnp.ceil(cnt * (1.0 / MOE_TS)) * MOE_TS
    r = lax.broadcasted_iota(jnp.int32, (CLS_PAD, CLS_PAD), 0)
    c = lax.broadcasted_iota(jnp.int32, (CLS_PAD, CLS_PAD), 1)
    off = jnp.dot((c < r).astype(F32), padded, precision=lax.Precision.HIGHEST, preferred_element_type=F32)
    off_ref[...] = off
    cnt_ref[...] = cnt
    tr = lax.broadcasted_iota(jnp.int32, (SLOT_BLK, SLOT_BLK), 0)
    tc = lax.broadcasted_iota(jnp.int32, (SLOT_BLK, SLOT_BLK), 1)
    earlier = (tr < tc).astype(BF16)

    def assign(j, base):
        member = members(j)
        rank = _dot(member.astype(BF16), earlier)
        slot = jnp.sum(member * (rank + base), axis=0, keepdims=True)
        slot_ref[0:1, pl.ds(pl.multiple_of(j * SLOT_BLK, SLOT_BLK), SLOT_BLK)] = slot.astype(jnp.int32)
        return base + jnp.sum(member, axis=1, keepdims=True)

    lax.fori_loop(0, n_blk, assign, off[:, 0:1])


def moe_slots(route):
    t = route.shape[1]
    stat = jax.ShapeDtypeStruct((CLS_PAD, LANES), F32)
    return pl.pallas_call(
        functools.partial(_slots_kernel, t // SLOT_BLK),
        out_shape=[jax.ShapeDtypeStruct((1, t), jnp.int32), stat, stat],
        compiler_params=pltpu.CompilerParams(vmem_limit_bytes=VMEM_LIMIT),
        name="moe_slots",
    )(route)


def _tile_maps(off, cnt, n_tiles):
    off = off[:N_CLS, 0].astype(jnp.int32)
    cnt = cnt[:N_CLS, 0].astype(jnp.int32)
    ends = off + ((cnt + MOE_TS - 1) // MOE_TS) * MOE_TS
    n_used = ends[-1] // MOE_TS
    k = jnp.arange(n_tiles, dtype=jnp.int32)
    tix = jnp.minimum(k, n_used - 1)
    cls = jnp.sum((tix[:, None] * MOE_TS >= ends[None, :]).astype(jnp.int32), axis=1)
    pair = jnp.asarray(MOE_PAIRS, jnp.int32)
    grp = (cls // len(MOE_PAIRS)) * EXPERTS_PER_GROUP
    ea = grp + pair[cls % len(MOE_PAIRS), 0]
    eb = grp + pair[cls % len(MOE_PAIRS), 1]
    n = jnp.int32(n_tiles)

    def slot_plan(e):
        chg = jnp.concatenate([jnp.ones((1,), jnp.int32), (e[1:] != e[:-1]).astype(jnp.int32)])
        at = jnp.where(chg == 1, k, n)
        nxt_at = jnp.concatenate([lax.cummin(at[::-1])[::-1][1:], n.reshape(1)])
        more = (nxt_at < n).astype(jnp.int32)
        nxt = e[jnp.minimum(nxt_at, n - 1)]
        par = (jnp.cumsum(chg) - 1) % 2
        return chg, nxt, more, par.astype(jnp.int32)

    plan_a, plan_b = slot_plan(ea), slot_plan(eb)
    chg, nxt, more, par = (jnp.stack([pa, pb]) for pa, pb in zip(plan_a, plan_b))
    return ea, eb, chg, nxt, more, par, n_used.reshape(1)


def _dispatch_kernel(n_steps, slots_ref, x_ref, g_ref, sc_ref, sh_ref, rt_ref, hs_in, hs_out, rowbuf, sem):
    del hs_in
    i = pl.program_id(0)
    cur = i % 2

    def wait_rows(s):
        pltpu.make_async_copy(rowbuf.at[s], rowbuf.at[s], sem.at[s]).wait()

    @pl.when(i >= 2)
    def _():
        wait_rows(cur)

    rowbuf[cur, :, 0:D_MODEL] = _normmod(x_ref[...], g_ref[...], sc_ref[0], sh_ref[0])
    rowbuf[cur, :, D_MODEL:ROW_W] = jnp.concatenate(
        [rt_ref[...], jnp.zeros((MOE_TM, LANES - ROUTE_ROWS), F32)], axis=1)

    for s in range(2):
        @pl.when(cur == s)
        def _():
            for r in range(MOE_TM):
                dst = slots_ref[i * MOE_TM + r]
                pltpu.make_async_copy(rowbuf.at[s, r], hs_out.at[dst], sem.at[s]).start()

    @pl.when(i == n_steps - 1)
    def _():
        wait_rows(cur)
        if n_steps >= 2:
            wait_rows(1 - cur)


def moe_dispatch(x, g, mod, route_t, slots, hs, seq_len, per_seq):
    t, d = x.shape
    n_steps = t // MOE_TM
    grid_spec = pltpu.PrefetchScalarGridSpec(
        num_scalar_prefetch=1,
        grid=(n_steps,),
        in_specs=[pl.BlockSpec((MOE_TM, d), lambda i, s: (i, 0)),
                  pl.BlockSpec((1, d), lambda i, s: (0, 0)),
                  _mod_spec(4, MOE_TM, seq_len, per_seq),
                  _mod_spec(3, MOE_TM, seq_len, per_seq),
                  pl.BlockSpec((MOE_TM, ROUTE_ROWS), lambda i, s: (i, 0)),
                  pl.BlockSpec(memory_space=pl.ANY)],
        out_specs=pl.BlockSpec(memory_space=pl.ANY),
        scratch_shapes=[pltpu.VMEM((2, MOE_TM, ROW_W), F32), pltpu.SemaphoreType.DMA((2,))],
    )
    return pl.pallas_call(
        functools.partial(_dispatch_kernel, n_steps),
        grid_spec=grid_spec,
        out_shape=jax.ShapeDtypeStruct(hs.shape, F32),
        input_output_aliases={6: 0},
        compiler_params=_cparams("arbitrary"),
        name="moe_dispatch",
    )(slots, x, g.reshape(1, d), mod, mod, route_t, hs)


def _experts_kernel(layer, ea_ref, eb_ref, chg_ref, nxt_ref, more_ref, par_ref, nused_ref,
                    hs_ref, wg_hbm, wu_hbm, wd_hbm, ys_ref, fg, fu, fd, bg, bu, bd, sem):
    def weight_copies(slot, expert, par):
        return [pltpu.make_async_copy(src.at[layer, expert], dst.at[slot, par], sem.at[slot, par])
                for src, dst in ((wg_hbm, fg), (wu_hbm, fu), (wd_hbm, fd))]

    def tile(k, rows):
        @pl.when(k < nused_ref[0])
        def _():
            for slot, e_ref in enumerate((ea_ref, eb_ref)):
                @pl.when(chg_ref[slot, k] == 1)
                def _():
                    par = par_ref[slot, k]

                    @pl.when(k == 0)
                    def _():
                        for cp in weight_copies(slot, e_ref[0], par):
                            cp.start()

                    for cp in weight_copies(slot, e_ref[k], par):
                        cp.wait()
                    bg[slot] = fg[slot, par].astype(BF16)
                    bu[slot] = fu[slot, par].astype(BF16)
                    bd[slot] = fd[slot, par].astype(BF16)

                    @pl.when(more_ref[slot, k] == 1)
                    def _():
                        for cp in weight_copies(slot, nxt_ref[slot, k], 1 - par):
                            cp.start()

            h = hs_ref[rows, 0:D_MODEL].astype(BF16)

            def ffn(slot):
                hid = _dot(h, bg[slot])
                up = _dot(h, bu[slot])
                w = hs_ref[rows, D_MODEL + 1 + slot:D_MODEL + 2 + slot]
                act = (hid * _sigmoid(hid)) * up * w
                return _dot(act.astype(BF16), bd[slot])

            ys_ref[rows, :] = ffn(0) + ffn(1)

        @pl.when(k >= nused_ref[0])
        def _():
            ys_ref[rows, :] = jnp.zeros((MOE_TS, ys_ref.shape[1]), ys_ref.dtype)

    for j in range(MOE_STEP_TILES):
        tile(pl.program_id(0) * MOE_STEP_TILES + j, slice(j * MOE_TS, (j + 1) * MOE_TS))


def moe_experts(hs, maps, layer, w_gate, w_up, w_down):
    n_tiles = hs.shape[0] // MOE_TS
    d = D_MODEL

    grid_spec = pltpu.PrefetchScalarGridSpec(
        num_scalar_prefetch=7,
        grid=(n_tiles // MOE_STEP_TILES,),
        in_specs=[pl.BlockSpec((MOE_STEP_TILES * MOE_TS, ROW_W), lambda k, *_: (k, 0)),
                  pl.BlockSpec(memory_space=pl.ANY), pl.BlockSpec(memory_space=pl.ANY),
                  pl.BlockSpec(memory_space=pl.ANY)],
        out_specs=pl.BlockSpec((MOE_STEP_TILES * MOE_TS, d), lambda k, *_: (k, 0)),
        scratch_shapes=[pltpu.VMEM((2, 2, d, D_EXPERT), F32), pltpu.VMEM((2, 2, d, D_EXPERT), F32),
                        pltpu.VMEM((2, 2, D_EXPERT, d), F32),
                        pltpu.VMEM((2, d, D_EXPERT), BF16), pltpu.VMEM((2, d, D_EXPERT), BF16),
                        pltpu.VMEM((2, D_EXPERT, d), BF16),
                        pltpu.SemaphoreType.DMA((2, 2))],
    )
    return pl.pallas_call(
        functools.partial(_experts_kernel, layer),
        grid_spec=grid_spec,
        out_shape=jax.ShapeDtypeStruct((hs.shape[0], d), F32),
        compiler_params=_cparams("arbitrary"),
        name="moe_experts",
    )(*maps, hs, w_gate, w_up, w_down)


def _combine_kernel(final, n_steps, slots_ref, x_ref, g2_ref, fg_ref, ys_hbm, o_ref, gbuf, sem):
    i = pl.program_id(0)
    cur = i % 2

    def issue_tile(tile, s):
        for r in range(MOE_TM):
            src = slots_ref[tile * MOE_TM + r]
            pltpu.make_async_copy(ys_hbm.at[src], gbuf.at[s, r], sem.at[s]).start()

    @pl.when(i == 0)
    def _():
        issue_tile(0, 0)

    for s in range(2):
        @pl.when((i + 1 < n_steps) & (1 - cur == s))
        def _():
            issue_tile(i + 1, s)

    pltpu.make_async_copy(gbuf.at[cur], gbuf.at[cur], sem.at[cur]).wait()
    y = x_ref[...] + g2_ref[0] * gbuf[cur]
    if final:
        ms = jnp.mean(y * y, axis=-1, keepdims=True)
        y = y * lax.rsqrt(ms + EPS) * fg_ref[...]
    o_ref[...] = y


def moe_combine(x, mod, slots, ys, final_g, final, seq_len, per_seq):
    t, d = x.shape
    n_steps = t // MOE_TM
    grid_spec = pltpu.PrefetchScalarGridSpec(
        num_scalar_prefetch=1,
        grid=(n_steps,),
        in_specs=[pl.BlockSpec((MOE_TM, d), lambda i, s: (i, 0)),
                  _mod_spec(5, MOE_TM, seq_len, per_seq),
                  pl.BlockSpec((1, d), lambda i, s: (0, 0)),
                  pl.BlockSpec(memory_space=pl.ANY)],
        out_specs=pl.BlockSpec((MOE_TM, d), lambda i, s: (i, 0)),
        scratch_shapes=[pltpu.VMEM((2, MOE_TM, d), F32), pltpu.SemaphoreType.DMA((2,))],
    )
    return pl.pallas_call(
        functools.partial(_combine_kernel, final, n_steps),
        grid_spec=grid_spec,
        out_shape=jax.ShapeDtypeStruct((t, d), F32),
        compiler_params=_cparams("arbitrary"),
        name="moe_combine",
    )(slots, x, mod, final_g.reshape(1, d), ys)


SC_WINDOW = 128
SC_COLS = 256


def sc_gather_rows(table, idx):
    n = idx.shape[0]
    d = table.shape[1]
    mesh = plsc.VectorSubcoreMesh(core_axis_name="core", subcore_axis_name="subcore")

    @pl.kernel(out_type=jax.ShapeDtypeStruct((n, d), table.dtype), mesh=mesh)
    def gather(x_hbm, i_hbm, o_hbm):
        for j in range(d // SC_COLS):
            def body(i_vmem, o_vmem, j=j):
                pltpu.sync_copy(x_hbm.at[i_vmem.at[0], pl.ds(j * SC_COLS, SC_COLS)], o_vmem)

            per_core = n // SC_WINDOW // mesh.num_cores
            pltpu.emit_pipeline(
                body,
                grid=(mesh.num_cores, per_core),
                in_specs=[pl.BlockSpec((1, SC_WINDOW), lambda c, i: (0, c * per_core + i))],
                out_specs=[pl.BlockSpec((SC_WINDOW, SC_COLS), lambda c, i, j=j: (c * per_core + i, j))],
                core_axis_name=("core", "subcore"),
                dimension_semantics=(pltpu.PARALLEL, pltpu.PARALLEL),
            )(i_hbm, o_hbm)

    return gather(table, idx.reshape(1, n))


def _res_nm_tm_kernel(nseq, steps, x_ref, y_ref, g2_ref, g_ref, sc_ref, sh_ref, w_ref, x2_ref, u_ref):
    d = x_ref.shape[-1]
    y = x_ref[...] + g2_ref[...] * y_ref[...]
    x2_ref[...] = y
    h = _normmod(y, g_ref[...], sc_ref[...], sh_ref[...])
    h = h.reshape(nseq * steps, d).astype(BF16)
    h = _dot(_row_permutation(nseq, steps, True), h).astype(BF16)
    u_ref[...] = _dot(h, w_ref[...])


def residual_nm_matmul_tm(x, mod_prev, y, g, mod, w, nseq, seq_len, per_seq):
    t, d = x.shape
    n = w.shape[1]
    steps = TM_ROWS // nseq
    g2 = _group_mod(mod_prev, 5, nseq, per_seq)
    sc = _group_mod(mod, 1, nseq, per_seq)
    sh = _group_mod(mod, 0, nseq, per_seq)
    mod_spec = pl.BlockSpec(sc.shape, lambda i: (0, 0, 0))
    x_spec = pl.BlockSpec((nseq, steps, d), lambda i: (0, i, 0))
    x2, u = pl.pallas_call(
        functools.partial(_res_nm_tm_kernel, nseq, steps),
        grid=(seq_len // steps,),
        in_specs=[x_spec, x_spec, mod_spec, pl.BlockSpec((1, d), lambda i: (0, 0)), mod_spec, mod_spec,
                  pl.BlockSpec((d, n), lambda i: (0, 0))],
        out_specs=[x_spec, pl.BlockSpec((TM_ROWS, n), lambda i: (i, 0))],
        out_shape=[jax.ShapeDtypeStruct((nseq, seq_len, d), F32), jax.ShapeDtypeStruct((t, n), F32)],
        compiler_params=_cparams("arbitrary"),
        name="moe_residual_norm_mod_proj_tm",
    )(x.reshape(nseq, seq_len, d), y.reshape(nseq, seq_len, d), g2, g.reshape(1, d), sc, sh, w)
    return x2.reshape(t, d), u


def _combine_nm_tm_kernel(nseq, steps, seq_len, n_steps, slots_ref, x_ref, g2_ref, ys_hbm, g_ref, sc_ref,
                          sh_ref, w_ref, x2_ref, u_ref, gbuf, sem):
    i = pl.program_id(0)
    cur = i % 2

    def issue_tile(tile, s):
        for sq in range(nseq):
            for t in range(steps):
                src = slots_ref[sq * seq_len + tile * steps + t]
                pltpu.make_async_copy(ys_hbm.at[src], gbuf.at[s, sq * steps + t], sem.at[s]).start()

    @pl.when(i == 0)
    def _():
        issue_tile(0, 0)

    for s in range(2):
        @pl.when((i + 1 < n_steps) & (1 - cur == s))
        def _():
            issue_tile(i + 1, s)

    pltpu.make_async_copy(gbuf.at[cur], gbuf.at[cur], sem.at[cur]).wait()
    d = x_ref.shape[-1]
    y = x_ref[...] + g2_ref[...] * gbuf[cur].reshape(nseq, steps, d)
    x2_ref[...] = y
    h = _normmod(y, g_ref[...], sc_ref[...], sh_ref[...])
    h = h.reshape(nseq * steps, d).astype(BF16)
    h = _dot(_row_permutation(nseq, steps, True), h).astype(BF16)
    u_ref[...] = _dot(h, w_ref[...])


def combine_nm_matmul_tm(x, mod_prev, slots, ys, g, mod, w, nseq, seq_len, per_seq):
    t, d = x.shape
    n = w.shape[1]
    steps = TM_ROWS // nseq
    n_steps = seq_len // steps
    g2 = _group_mod(mod_prev, 5, nseq, per_seq)
    sc = _group_mod(mod, 1, nseq, per_seq)
    sh = _group_mod(mod, 0, nseq, per_seq)
    mod_spec = pl.BlockSpec(sc.shape, lambda i, s: (0, 0, 0))
    x_spec = pl.BlockSpec((nseq, steps, d), lambda i, s: (0, i, 0))
    grid_spec = pltpu.PrefetchScalarGridSpec(
        num_scalar_prefetch=1,
        grid=(n_steps,),
        in_specs=[x_spec, mod_spec, pl.BlockSpec(memory_space=pl.ANY),
                  pl.BlockSpec((1, d), lambda i, s: (0, 0)), mod_spec, mod_spec,
                  pl.BlockSpec((d, n), lambda i, s: (0, 0))],
        out_specs=[x_spec, pl.BlockSpec((TM_ROWS, n), lambda i, s: (i, 0))],
        scratch_shapes=[pltpu.VMEM((2, TM_ROWS, d), F32), pltpu.SemaphoreType.DMA((2,))],
    )
    x2, u = pl.pallas_call(
        functools.partial(_combine_nm_tm_kernel, nseq, steps, seq_len, n_steps),
        grid_spec=grid_spec,
        out_shape=[jax.ShapeDtypeStruct((nseq, seq_len, d), F32), jax.ShapeDtypeStruct((t, n), F32)],
        compiler_params=_cparams("arbitrary"),
        name="moe_combine_norm_mod_proj_tm",
    )(slots, x.reshape(nseq, seq_len, d), g2, ys, g.reshape(1, d), sc, sh, w)
    return x2.reshape(t, d), u


def moe_block(xs, routes, mods_l, per_seqs, seq_lens, layer, p, final, defer_combine, hs):
    g = p['norm_g'][layer, 1]
    slots, off, cnt = moe_slots(jnp.concatenate(routes, axis=1))
    t_all = slots.shape[1]
    n_tiles = -(-(t_all // MOE_TS + N_CLS) // MOE_STEP_TILES) * MOE_STEP_TILES
    maps = _tile_maps(off, cnt, n_tiles)
    if hs is None:
        hs = jnp.zeros((n_tiles * MOE_TS, ROW_W), F32)
    bounds = np.cumsum([0] + [x.shape[0] for x in xs])
    group_slots = [slots[0, bounds[i]:bounds[i + 1]] for i in range(len(xs))]
    for x, r, s, sl, ps in zip(xs, routes, group_slots, seq_lens, per_seqs):
        hs = moe_dispatch(x, g, mods_l, r.T, s, hs, sl, ps)
    ys = moe_experts(hs, maps, layer, p['moe_w_gate'], p['moe_w_up'], p['moe_w_down'])
    if defer_combine:
        return [(x, s, ys) for x, s in zip(xs, group_slots)], hs
    return [moe_combine(x, mods_l, s, ys, p['final_g'], final, sl, ps)
            for x, s, sl, ps in zip(xs, group_slots, seq_lens, per_seqs)], hs


def _mixer(x, group, l, mod, p, hy_tables, mod_prev):
    per_seq, nseq, seq_len = group['per_seq'], group['nseq'], group['seq_len']
    extras = None
    deferred = isinstance(x, tuple)
    if deferred and l % 2 == 0:
        x = moe_combine(x[0], mod_prev, x[1], x[2], p['final_g'], False, seq_len, per_seq)
        deferred = False
    if l % 2 == 0:
        e = l // 2
        u = nm_matmul(x, p['norm_g'][l, 0], mod, p['a_in_w'][e].astype(BF16), seq_len, per_seq)
        if group['ctx_k'] is None:
            attn, nk, nv = context_attention(u, nseq, seq_len)
            extras = (nk, nv)
        else:
            attn = neighbourhood_attention(u, group['ctx_k'][:, e], group['ctx_v'][:, e], p['na_rpb'][e],
                                           nseq, seq_len)
        fwd, inv = hy_tables[seq_len]
        spectrum = hyena_spectrum(seq_len, p['hy_w1'][e], p['hy_b1'][e], p['hy_w2'][e], p['hy_b2'][e],
                                  p['hy_w3'][e], p['hy_freq'][e], p['hy_d'][e], fwd)
        hy = hyena_mixer(u, nseq, seq_len, p['hy_short_w'][e], p['hy_short_b'][e], spectrum, fwd, inv)
        w_out = p['a_out_w'][e].astype(BF16)
        x, route = proj_residual([attn, hy], [w_out[:D_A], w_out[D_A:]], x, mod, p['norm_g'][l, 1],
                                 p['router_w'], p['router_b'], seq_len, per_seq)
    else:
        o = l // 2
        w_in = p['c_in_w'][o].astype(BF16)
        if deferred and group['sc_combine']:
            idx = x[1]
            if p.get('sc_after') is not None:
                idx, _ = lax.optimization_barrier((idx, p['sc_after']))
            x, u = residual_nm_matmul_tm(x[0], mod_prev, sc_gather_rows(x[2], idx), p['norm_g'][l, 0], mod,
                                         w_in, nseq, seq_len, per_seq)
        elif deferred:
            x, u = combine_nm_matmul_tm(x[0], mod_prev, x[1], x[2], p['norm_g'][l, 0], mod, w_in,
                                        nseq, seq_len, per_seq)
            p['sc_after'] = u
        else:
            u = nm_matmul_tm(x, p['norm_g'][l, 0], mod, w_in, nseq, seq_len, per_seq)
        y, extras = rglru_block(u, nseq, seq_len, p['rg_conv_w'][o], p['rg_conv_b'][o], p['rg_wa'][o],
                                p['rg_ba'][o], p['rg_wx'][o], p['rg_bx'][o], p['rg_lam'][o], group['h0'][o])
        x, route = proj_residual_tm(y, p['c_out_w'][o].astype(BF16), x, mod, p['norm_g'][l, 1],
                                    p['router_w'], p['router_b'], nseq, seq_len, per_seq)
    return x, extras, route


def kernel(x_prompt, x_sample, cache_k, cache_v, state_h, c, c_ctx, norm_g, ada_w, ada_b, final_g, a_in_w, a_out_w, na_rpb, hy_short_w, hy_short_b, hy_w1, hy_b1, hy_w2, hy_b2, hy_w3, hy_freq, hy_d, c_in_w, c_out_w, rg_conv_w, rg_conv_b, rg_wa, rg_ba, rg_wx, rg_bx, rg_lam, router_w, router_b, moe_w_gate, moe_w_up, moe_w_down):
    p = dict(norm_g=norm_g, final_g=final_g, a_in_w=a_in_w, a_out_w=a_out_w, na_rpb=na_rpb,
             hy_short_w=hy_short_w, hy_short_b=hy_short_b, hy_w1=hy_w1, hy_b1=hy_b1, hy_w2=hy_w2,
             hy_b2=hy_b2, hy_w3=hy_w3, hy_freq=hy_freq, hy_d=hy_d, c_in_w=c_in_w, c_out_w=c_out_w,
             rg_conv_w=rg_conv_w, rg_conv_b=rg_conv_b, rg_wa=rg_wa, rg_ba=rg_ba, rg_wx=rg_wx, rg_bx=rg_bx,
             rg_lam=rg_lam, router_w=router_w, router_b=router_b, moe_w_gate=moe_w_gate,
             moe_w_up=moe_w_up, moe_w_down=moe_w_down)
    batch, seq, d = x_prompt.shape
    dec_batch, dec_seq, _ = x_sample.shape
    n_odd = DEPTH // 2
    assert 1 + dec_batch <= MOD_ROWS

    cond = jnp.concatenate([c_ctx[None, :], c, jnp.zeros((MOD_ROWS - 1 - dec_batch, d), F32)], axis=0)
    m = modulation(cond, ada_w, ada_b)
    mods = [m[l].reshape(MOD_ROWS * N_MOD, 1, d) for l in range(DEPTH)]

    tables = {}
    for sl in (seq, dec_seq):
        fwd, inv = _dft_tables(sl)
        tables[sl] = (jnp.asarray(fwd).astype(BF16), jnp.asarray(inv).astype(BF16))

    groups = [
        dict(per_seq=False, nseq=batch, seq_len=seq, ctx_k=None, ctx_v=None, sc_combine=False,
             h0=[jnp.zeros((2, batch, D_RNN), F32)] * n_odd),
        dict(per_seq=True, nseq=dec_batch, seq_len=dec_seq, ctx_k=cache_k, ctx_v=cache_v, sc_combine=True,
             h0=[state_h[:, o].transpose(1, 0, 2) for o in range(n_odd)]),
    ]
    xs = [x_prompt.reshape(batch * seq, d), x_sample.reshape(dec_batch * dec_seq, d)]
    k_list, v_list, h_list = [], [], []
    hs = None
    for l in range(DEPTH):
        mixed = [_mixer(x, grp, l, mods[l], p, tables, mods[l - 1] if l else None)
                 for x, grp in zip(xs, groups)]
        if l % 2 == 0:
            k_list.append(mixed[0][1][0])
            v_list.append(mixed[0][1][1])
        else:
            h_list.append(mixed[0][1].transpose(1, 0, 2))
        xs, hs = moe_block([mx[0] for mx in mixed], [mx[2] for mx in mixed], mods[l],
                           [grp['per_seq'] for grp in groups], [grp['seq_len'] for grp in groups],
                           l, p, l == DEPTH - 1, l < DEPTH - 1, hs)
    new_k = jnp.stack(k_list, axis=1)
    new_v = jnp.stack(v_list, axis=1)
    new_h = jnp.stack(h_list, axis=1)
    return (xs[0].reshape(batch, seq, d), xs[1].reshape(dec_batch, dec_seq, d), new_k, new_v, new_h)
```

```python
import functools
import math

import numpy as np
import jax
import jax.numpy as jnp
from jax import lax
from jax.experimental import pallas as pl
from jax.experimental.pallas import tpu as pltpu
from jax.experimental.pallas import tpu_sc as plsc

F32 = jnp.float32
BF16 = jnp.bfloat16

D_MODEL = 1024
DEPTH = 2
GRID_W = 64
EPS = 1e-6
NEG_INF = -1e30
NA_HEADS = 8
HEAD_DIM = 64
D_A = NA_HEADS * HEAD_DIM
WIN_ROWS = 8
WIN_COLS = 16
D_B = D_MODEL - D_A
HY_ORDER = 2
HY_EMB = 33
HY_BANDS = (HY_EMB - 1) // 2
HY_FFN = 64
HY_DECAY_TARGET = 1e-2
HY_FAST_PCT = 0.3
HY_SLOW_PCT = 1.5
D_RNN = D_MODEL
RG_BLOCK = 64
RG_C = 8.0
N_EXPERTS = 16
N_GROUPS = 4
EXPERTS_PER_GROUP = N_EXPERTS // N_GROUPS
D_EXPERT = 512

LANES = 128
VMEM_LIMIT = 56 * 1024 * 1024
N_MOD = 6
MOD_ROWS = 16


def _cparams(*sem):
    return pltpu.CompilerParams(dimension_semantics=sem, vmem_limit_bytes=VMEM_LIMIT)


def _dot(a, b):
    return jnp.dot(a, b, preferred_element_type=F32)


def _dot_nt(a, b):
    return lax.dot_general(a, b, (((1,), (1,)), ((), ())), preferred_element_type=F32)


def _sigmoid(x):
    return 0.5 * jnp.tanh(0.5 * x) + 0.5


def _normmod(x, g, sc, sh):
    ms = jnp.mean(x * x, axis=-1, keepdims=True)
    return (x * lax.rsqrt(ms + EPS) * g) * (1.0 + sc) + sh


def _mod_spec(chunk, tm, seq_len, per_seq):
    if per_seq:
        return pl.BlockSpec((1, 1, D_MODEL), lambda i, *_: ((1 + (i * tm) // seq_len) * N_MOD + chunk, 0, 0))
    return pl.BlockSpec((1, 1, D_MODEL), lambda i, *_: (chunk, 0, 0))


def _mod_kernel(c_ref, w_ref, b_ref, o_ref):
    s = c_ref[...]
    s = s * jax.nn.sigmoid(s)
    o_ref[0] = _dot(s.astype(BF16), w_ref[0].astype(BF16)) + b_ref[0]


def modulation(cond, ada_w, ada_b):
    n = ada_w.shape[-1]
    tn = n // 4
    assert tn % LANES == 0
    return pl.pallas_call(
        _mod_kernel,
        grid=(DEPTH, n // tn),
        in_specs=[pl.BlockSpec((MOD_ROWS, D_MODEL), lambda l, j: (0, 0)),
                  pl.BlockSpec((1, D_MODEL, tn), lambda l, j: (l, 0, j)),
                  pl.BlockSpec((1, 1, tn), lambda l, j: (l, 0, j))],
        out_specs=pl.BlockSpec((1, MOD_ROWS, tn), lambda l, j: (l, 0, j)),
        out_shape=jax.ShapeDtypeStruct((DEPTH, MOD_ROWS, n), F32),
        compiler_params=_cparams("arbitrary", "arbitrary"),
        name="modulation",
    )(cond, ada_w, ada_b.reshape(DEPTH, 1, n))


def _nm_matmul_kernel(x_ref, g_ref, sc_ref, sh_ref, w_ref, o_ref):
    h = _normmod(x_ref[...], g_ref[...], sc_ref[0], sh_ref[0])
    o_ref[...] = _dot(h.astype(BF16), w_ref[...])


def nm_matmul(x, g, mod, w, seq_len, per_seq, tm=512):
    t, d = x.shape
    n = w.shape[1]
    return pl.pallas_call(
        _nm_matmul_kernel,
        grid=(t // tm,),
        in_specs=[pl.BlockSpec((tm, d), lambda i: (i, 0)),
                  pl.BlockSpec((1, d), lambda i: (0, 0)),
                  _mod_spec(1, tm, seq_len, per_seq),
                  _mod_spec(0, tm, seq_len, per_seq),
                  pl.BlockSpec((d, n), lambda i: (0, 0))],
        out_specs=pl.BlockSpec((tm, n), lambda i: (i, 0)),
        out_shape=jax.ShapeDtypeStruct((t, n), F32),
        compiler_params=_cparams("arbitrary"),
        name="norm_mod_proj",
    )(x, g.reshape(1, d), mod, mod, w)


def _proj_res_kernel(n_act, *refs):
    acts = refs[:n_act]
    ws = refs[n_act:2 * n_act]
    x_ref, g_ref, ng_ref, sc2_ref, sh2_ref, rw_ref, rb_ref, o_ref, route_ref = refs[2 * n_act:]
    acc = _dot(acts[0][...].astype(BF16), ws[0][...])
    for a, w in zip(acts[1:], ws[1:]):
        acc += _dot(a[...].astype(BF16), w[...])
    x = x_ref[...] + g_ref[0] * acc
    o_ref[...] = x
    route_ref[...] = _route_record(_normmod(x, ng_ref[...], sc2_ref[0], sh2_ref[0]), rw_ref[...], rb_ref[...])


def proj_residual(acts, ws, x, mod, norm2_g, router_w, router_b, seq_len, per_seq, tm=512):
    t, d = x.shape
    in_specs = [pl.BlockSpec((tm, a.shape[1]), lambda i: (i, 0)) for a in acts]
    in_specs += [pl.BlockSpec(w.shape, lambda i: (0, 0)) for w in ws]
    in_specs += [pl.BlockSpec((tm, d), lambda i: (i, 0)), _mod_spec(2, tm, seq_len, per_seq),
                 pl.BlockSpec((1, d), lambda i: (0, 0)),
                 _mod_spec(4, tm, seq_len, per_seq), _mod_spec(3, tm, seq_len, per_seq),
                 pl.BlockSpec((N_EXPERTS, d), lambda i: (0, 0)),
                 pl.BlockSpec((N_EXPERTS, 1), lambda i: (0, 0))]
    return pl.pallas_call(
        functools.partial(_proj_res_kernel, len(acts)),
        grid=(t // tm,),
        in_specs=in_specs,
        out_specs=[pl.BlockSpec((tm, d), lambda i: (i, 0)), pl.BlockSpec((ROUTE_ROWS, tm), lambda i: (0, i))],
        out_shape=[jax.ShapeDtypeStruct((t, d), F32), jax.ShapeDtypeStruct((ROUTE_ROWS, t), F32)],
        compiler_params=_cparams("arbitrary"),
        name="proj_residual_route",
    )(*acts, *ws, x, mod, norm2_g.reshape(1, d), mod, mod, router_w.T, router_b.reshape(N_EXPERTS, 1))


def _row_permutation(nseq, steps, to_time_major):
    n = nseq * steps
    i = lax.broadcasted_iota(jnp.int32, (n, n), 0)
    j = lax.broadcasted_iota(jnp.int32, (n, n), 1)
    if to_time_major:
        src = (i % nseq) * steps + i // nseq
    else:
        src = (i % steps) * nseq + i // steps
    return (j == src).astype(BF16)


def _nm_matmul_tm_kernel(nseq, steps, x_ref, g_ref, sc_ref, sh_ref, w_ref, o_ref):
    h = _normmod(x_ref[...], g_ref[...], sc_ref[...], sh_ref[...])
    h = h.reshape(nseq * steps, h.shape[-1]).astype(BF16)
    h = _dot(_row_permutation(nseq, steps, True), h).astype(BF16)
    o_ref[...] = _dot(h, w_ref[...])


def _group_mod(mod, chunk, nseq, per_seq):
    rows = mod.reshape(MOD_ROWS, N_MOD, 1, D_MODEL)
    return rows[1:1 + nseq, chunk] if per_seq else rows[0:1, chunk]


def nm_matmul_tm(x, g, mod, w, nseq, seq_len, per_seq):
    t, d = x.shape
    n = w.shape[1]
    steps = TM_ROWS // nseq
    sc = _group_mod(mod, 1, nseq, per_seq)
    sh = _group_mod(mod, 0, nseq, per_seq)
    mod_spec = pl.BlockSpec(sc.shape, lambda i: (0, 0, 0))
    return pl.pallas_call(
        functools.partial(_nm_matmul_tm_kernel, nseq, steps),
        grid=(seq_len // steps,),
        in_specs=[pl.BlockSpec((nseq, steps, d), lambda i: (0, i, 0)),
                  pl.BlockSpec((1, d), lambda i: (0, 0)),
                  mod_spec, mod_spec,
                  pl.BlockSpec((d, n), lambda i: (0, 0))],
        out_specs=pl.BlockSpec((TM_ROWS, n), lambda i: (i, 0)),
        out_shape=jax.ShapeDtypeStruct((t, n), F32),
        compiler_params=_cparams("arbitrary"),
        name="norm_mod_proj_tm",
    )(x.reshape(nseq, seq_len, d), g.reshape(1, d), sc, sh, w)


def _proj_res_tm_kernel(nseq, steps, y_ref, w_ref, x_ref, g_ref, ng_ref, sc2_ref, sh2_ref, rw_ref, rb_ref,
                        o_ref, route_ref):
    y = _dot(_row_permutation(nseq, steps, False), y_ref[...].astype(BF16)).astype(BF16)
    acc = _dot(y, w_ref[...])
    x = x_ref[...] + g_ref[...] * acc.reshape(nseq, steps, acc.shape[-1])
    o_ref[...] = x
    h = _normmod(x, ng_ref[...], sc2_ref[...], sh2_ref[...]).reshape(nseq * steps, x.shape[-1])
    route_ref[0] = _route_record(h, rw_ref[...], rb_ref[...])


def proj_residual_tm(y, w, x, mod, norm2_g, router_w, router_b, nseq, seq_len, per_seq):
    t, d = x.shape
    steps = TM_ROWS // nseq
    n_steps = seq_len // steps
    g1 = _group_mod(mod, 2, nseq, per_seq)
    sc2 = _group_mod(mod, 4, nseq, per_seq)
    sh2 = _group_mod(mod, 3, nseq, per_seq)
    mod_spec = pl.BlockSpec(g1.shape, lambda i: (0, 0, 0))
    out, route = pl.pallas_call(
        functools.partial(_proj_res_tm_kernel, nseq, steps),
        grid=(n_steps,),
        in_specs=[pl.BlockSpec((TM_ROWS, y.shape[1]), lambda i: (i, 0)),
                  pl.BlockSpec(w.shape, lambda i: (0, 0)),
                  pl.BlockSpec((nseq, steps, d), lambda i: (0, i, 0)),
                  mod_spec,
                  pl.BlockSpec((1, d), lambda i: (0, 0)),
                  mod_spec, mod_spec,
                  pl.BlockSpec((N_EXPERTS, d), lambda i: (0, 0)),
                  pl.BlockSpec((N_EXPERTS, 1), lambda i: (0, 0))],
        out_specs=[pl.BlockSpec((nseq, steps, d), lambda i: (0, i, 0)),
                   pl.BlockSpec((1, ROUTE_ROWS, TM_ROWS), lambda i: (i, 0, 0))],
        out_shape=[jax.ShapeDtypeStruct((nseq, seq_len, d), F32),
                   jax.ShapeDtypeStruct((n_steps, ROUTE_ROWS, TM_ROWS), F32)],
        compiler_params=_cparams("arbitrary"),
        name="proj_residual_tm_route",
    )(y, w, x.reshape(nseq, seq_len, d), g1, norm2_g.reshape(1, d), sc2, sh2, router_w.T,
      router_b.reshape(N_EXPERTS, 1))
    route = route.reshape(n_steps, ROUTE_ROWS, nseq, steps).transpose(1, 2, 0, 3).reshape(ROUTE_ROWS, t)
    return out.reshape(t, d), route


CTX_STEP_ROWS = 1024


def _ctx_attn_kernel(seq_len, n_seq, q_ref, k_ref, v_ref, o_ref, nk_ref, nv_ref):
    scale = HEAD_DIM ** -0.5
    per_tile = LANES // HEAD_DIM
    lane = lax.broadcasted_iota(jnp.int32, (1, LANES), 1)
    for b in range(n_seq):
        rows = slice(b * seq_len, (b + 1) * seq_len)
        for hp in range(NA_HEADS // per_tile):
            sl = slice(hp * LANES, (hp + 1) * LANES)
            q, k, v = q_ref[rows, sl] * scale, k_ref[rows, sl], v_ref[rows, sl]
            kb, vb = k.astype(BF16), v.astype(BF16)
            out = None
            for j in range(per_tile):
                h = hp * per_tile + j
                nk_ref[b, h] = k[:, j * HEAD_DIM:(j + 1) * HEAD_DIM]
                nv_ref[b, h] = v[:, j * HEAD_DIM:(j + 1) * HEAD_DIM]
                mine = lane // HEAD_DIM == j
                s = _dot_nt(jnp.where(mine, q, 0.0).astype(BF16), kb)
                p = jnp.exp(s - jnp.max(s, axis=-1, keepdims=True))
                o = _dot(p.astype(BF16), vb) / jnp.sum(p, axis=-1, keepdims=True)
                out = o if out is None else jnp.where(mine, o, out)
            o_ref[rows, sl] = out.astype(o_ref.dtype)


def context_attention(u, nseq, seq_len):
    t = u.shape[0]
    per_step = max(1, CTX_STEP_ROWS // seq_len)
    assert nseq % per_step == 0
    rows = per_step * seq_len
    kv_shape = jax.ShapeDtypeStruct((nseq, NA_HEADS, seq_len, HEAD_DIM), F32)
    kv_spec = pl.BlockSpec((per_step, NA_HEADS, seq_len, HEAD_DIM), lambda b: (b, 0, 0, 0))
    return pl.pallas_call(
        functools.partial(_ctx_attn_kernel, seq_len, per_step),
        grid=(nseq // per_step,),
        in_specs=[pl.BlockSpec((rows, D_A), lambda b: (b, 0)),
                  pl.BlockSpec((rows, D_A), lambda b: (b, 1)),
                  pl.BlockSpec((rows, D_A), lambda b: (b, 2))],
        out_specs=[pl.BlockSpec((rows, D_A), lambda b: (b, 0)), kv_spec, kv_spec],
        out_shape=[jax.ShapeDtypeStruct((t, D_A), BF16), kv_shape, kv_shape],
        compiler_params=_cparams("arbitrary"),
        name="context_attention",
    )(u, u, u)


N_DR = 2 * WIN_ROWS - 1
N_DC = 2 * WIN_COLS - 1
N_DC_PAD = 32


def _na_col_tables():
    cols = np.arange(GRID_W)
    col_start = np.clip(cols - WIN_COLS // 2, 0, GRID_W - WIN_COLS)
    col_in = (cols[None, :] >= col_start[:, None]) & (cols[None, :] < col_start[:, None] + WIN_COLS)
    dc = np.clip(cols[None, :] - cols[:, None], 1 - WIN_COLS, WIN_COLS - 1) + WIN_COLS - 1
    onehot = (dc.reshape(1, -1) == np.arange(N_DC_PAD)[:, None]).astype(np.float32)
    return onehot, col_in.reshape(1, -1).astype(np.float32)


def _na_bias_kernel(r_ref, e_ref, m_ref, o_ref):
    t = jnp.dot(r_ref[...], e_ref[...], precision=lax.Precision.HIGHEST, preferred_element_type=F32)
    o_ref[...] = jnp.where(m_ref[...] > 0.0, t, NEG_INF)


def na_bias_table(rpb):
    onehot, col_in = _na_col_tables()
    n_rows = NA_HEADS * N_DR
    assert n_rows <= LANES
    r = jnp.zeros((LANES, N_DC_PAD), F32).at[:n_rows, :N_DC].set(rpb.reshape(n_rows, N_DC).astype(F32))
    t = pl.pallas_call(
        _na_bias_kernel,
        out_shape=jax.ShapeDtypeStruct((LANES, GRID_W * GRID_W), F32),
        name="na_bias_table",
    )(r, jnp.asarray(onehot), jnp.asarray(col_in))
    t = t[:n_rows].reshape(NA_HEADS, N_DR, GRID_W, GRID_W)
    return jnp.concatenate([t[:, :-1], t[:, 1:]], axis=-1)


def _na_kernel(rows, q_ref, k_ref, v_ref, ck_ref, cv_ref, bias_ref, o_ref,
               q_s, k_s, v_s, ck_s, cv_s, s_s, p_s, den_s, o_s):
    scale = HEAD_DIM ** -0.5
    n_lat = WIN_ROWS * GRID_W
    per_tile = LANES // HEAD_DIM
    n_pairs = NA_HEADS // per_tile
    lane = lax.broadcasted_iota(jnp.int32, (1, LANES), 1)
    for hp in range(n_pairs):
        sl = slice(hp * LANES, (hp + 1) * LANES)
        q_s[hp] = (q_ref[:, sl] * scale).astype(BF16)
        k_s[hp] = k_ref[:, sl].astype(BF16)
        v_s[hp] = v_ref[:, sl].astype(BF16)
        heads = range(hp * per_tile, (hp + 1) * per_tile)
        ck_s[hp] = jnp.concatenate([ck_ref[0, h] for h in heads], axis=1).astype(BF16)
        cv_s[hp] = jnp.concatenate([cv_ref[0, h] for h in heads], axis=1).astype(BF16)

    def window(r):
        start = min(max(r - WIN_ROWS // 2, 0), rows - WIN_ROWS)
        return start, start - r + WIN_ROWS - 1

    def pair_body(hp, carry):
        for j in range(per_tile):
            h = hp * per_tile + j
            mine = lane // HEAD_DIM == j
            for r in range(rows):
                start, off = window(r)
                rs = slice(r * GRID_W, (r + 1) * GRID_W)
                q = jnp.where(mine, q_s[hp, rs, :], 0.0).astype(BF16)
                bias = jnp.concatenate([bias_ref[h, off + 2 * i] for i in range(WIN_ROWS // 2)], axis=1)
                s_s[rs, 0:n_lat] = _dot_nt(q, k_s[hp, start * GRID_W:start * GRID_W + n_lat, :]) + bias
                s_s[rs, n_lat:] = _dot_nt(q, ck_s[hp])
            for r in range(rows):
                rs = slice(r * GRID_W, (r + 1) * GRID_W)
                s = s_s[rs, :]
                p = jnp.exp(s - jnp.max(s, axis=-1, keepdims=True))
                den_s[rs, :] = jnp.sum(p, axis=-1, keepdims=True)
                p_s[rs, :] = p.astype(BF16)
            for r in range(rows):
                start, _ = window(r)
                rs = slice(r * GRID_W, (r + 1) * GRID_W)
                o = (_dot(p_s[rs, 0:n_lat], v_s[hp, start * GRID_W:start * GRID_W + n_lat, :])
                     + _dot(p_s[rs, n_lat:], cv_s[hp])) / den_s[rs, :]
                o_s[hp, rs, :] = o if j == 0 else jnp.where(mine, o, o_s[hp, rs, :])
        return carry

    lax.fori_loop(0, n_pairs, pair_body, 0)
    for hp in range(n_pairs):
        o_ref[:, hp * LANES:(hp + 1) * LANES] = o_s[hp].astype(o_ref.dtype)


def neighbourhood_attention(u, ctx_k, ctx_v, rpb, nseq, seq_len):
    t = u.shape[0]
    rows = seq_len // GRID_W
    assert rows >= WIN_ROWS and WIN_ROWS % 2 == 0
    past = ctx_k.shape[2]
    bias = na_bias_table(rpb)
    ctx_spec = pl.BlockSpec((1, NA_HEADS, past, HEAD_DIM), lambda b: (b, 0, 0, 0))
    return pl.pallas_call(
        functools.partial(_na_kernel, rows),
        grid=(nseq,),
        in_specs=[pl.BlockSpec((seq_len, D_A), lambda b: (b, 0)),
                  pl.BlockSpec((seq_len, D_A), lambda b: (b, 1)),
                  pl.BlockSpec((seq_len, D_A), lambda b: (b, 2)),
                  ctx_spec, ctx_spec,
                  pl.BlockSpec(bias.shape, lambda b: (0, 0, 0, 0))],
        out_specs=pl.BlockSpec((seq_len, D_A), lambda b: (b, 0)),
        out_shape=jax.ShapeDtypeStruct((t, D_A), BF16),
        scratch_shapes=[pltpu.VMEM((D_A // LANES, seq_len, LANES), BF16)] * 3
        + [pltpu.VMEM((D_A // LANES, past, LANES), BF16)] * 2
        + [pltpu.VMEM((seq_len, WIN_ROWS * GRID_W + past), F32),
           pltpu.VMEM((seq_len, WIN_ROWS * GRID_W + past), BF16),
           pltpu.VMEM((seq_len, 1), F32),
           pltpu.VMEM((D_A // LANES, seq_len, LANES), F32)],
        compiler_params=_cparams("arbitrary"),
        name="neighbourhood_attention",
    )(u, u, u, ctx_k, ctx_v, bias)


HY_STEP_ROWS = 1024


def _dft_tables(seq_len):
    n = 2 * seq_len
    f = np.arange(seq_len, dtype=np.int64)
    ang = (np.outer(f, f) % n).astype(np.float64) * (math.pi / seq_len)
    cos, sin = np.cos(ang), np.sin(ang)
    alt = np.where(f % 2 == 0, 1.0, -1.0)
    s_fwd = -sin
    s_fwd[0, :] = alt
    fwd = np.concatenate([cos, s_fwd], axis=0)
    wf = np.where(f == 0, 1.0, 2.0) / n
    ci = cos.T * wf[None, :]
    si = -sin.T * wf[None, :]
    si[:, 0] = alt / n
    inv = np.concatenate([ci, si], axis=1)
    return fwd.astype(np.float32), inv.astype(np.float32)


def _hyena_feats(seq_len):
    t = np.linspace(0.0, 1.0, seq_len, dtype=np.float32)[:, None]
    w = (2.0 * math.pi * np.arange(seq_len, dtype=np.float32)[:, None] / seq_len).astype(np.float32)
    f = np.linspace(1e-4, HY_BANDS - 1, HY_BANDS, dtype=np.float32)[None, :]
    z = np.concatenate([t, np.cos(f * w), -np.sin(f * w)], axis=-1).astype(np.float32)
    max_decay = math.log(HY_DECAY_TARGET) / HY_FAST_PCT
    min_decay = math.log(HY_DECAY_TARGET) / HY_SLOW_PCT
    deltas = np.abs(np.linspace(min_decay, max_decay, D_B, dtype=np.float32))[None, :]
    return z, t, deltas


def _hy_filter_kernel(seq_len, z_ref, t_ref, dl_ref, w1_ref, b1_ref, w2_ref, b2_ref, w3_ref, fr_ref,
                      d_ref, fwd_ref, g_ref):
    hp = lax.Precision.HIGHEST
    h = jnp.sin(fr_ref[0:1, :] * (jnp.dot(z_ref[...], w1_ref[...], precision=hp) + b1_ref[...]))
    h = jnp.sin(fr_ref[1:2, :] * (jnp.dot(h, w2_ref[...], precision=hp) + b2_ref[...]))
    h = jnp.dot(h, w3_ref[...], precision=hp)
    decay = jnp.exp(-t_ref[...] * dl_ref[...])
    row0 = lax.broadcasted_iota(jnp.int32, (seq_len, D_B), 0) == 0
    sums, diffs = [], []
    for n in range(HY_ORDER):
        hf = h[:, (2 * n) * D_B:(2 * n + 1) * D_B] * decay
        hb = h[:, (2 * n + 1) * D_B:(2 * n + 2) * D_B] * decay
        gp = jnp.where(row0, hf + hb + d_ref[n:n + 1, :], hf)
        gm = jnp.where(row0, 0.0, hb)
        sums.append(gp + gm)
        diffs.append(gp - gm)
    rhs = jnp.concatenate(sums + diffs, axis=1).astype(BF16)
    spec = _dot(fwd_ref[...], rhs)
    for n in range(HY_ORDER):
        a = spec[:, n * D_B:(n + 1) * D_B]
        b = spec[:, (HY_ORDER + n) * D_B:(HY_ORDER + n + 1) * D_B]
        g_ref[n, 0:seq_len, :] = a[0:seq_len]
        g_ref[n, seq_len:, :] = jnp.where(row0, a[seq_len:], b[seq_len:])


def hyena_spectrum(seq_len, w1, b1, w2, b2, w3, freq, d, fwd):
    z, t, deltas = _hyena_feats(seq_len)
    return pl.pallas_call(
        functools.partial(_hy_filter_kernel, seq_len),
        out_shape=jax.ShapeDtypeStruct((HY_ORDER, 2 * seq_len, D_B), F32),
        compiler_params=pltpu.CompilerParams(vmem_limit_bytes=VMEM_LIMIT),
        name="hyena_spectrum",
    )(jnp.asarray(z), jnp.asarray(t), jnp.asarray(deltas), w1, b1.reshape(1, -1), w2, b2.reshape(1, -1),
      w3, freq, d, fwd)


def _hyena_kernel(seq_len, n_seq, u_ref, sw_ref, sb_ref, g_ref, fwd_ref, inv_ref, o_ref):
    t_idx = lax.broadcasted_iota(jnp.int32, (seq_len, u_ref.shape[1]), 0)
    row0 = lax.broadcasted_iota(jnp.int32, (seq_len, D_B), 0) == 0
    for s in range(n_seq):
        rows = slice(s * seq_len, (s + 1) * seq_len)
        u = u_ref[rows, :]
        prev = jnp.where(t_idx == 0, 0.0, pltpu.roll(u, 1, axis=0))
        nxt = jnp.where(t_idx == seq_len - 1, 0.0, pltpu.roll(u, seq_len - 1, axis=0))
        u = prev * sw_ref[0:1, :] + u * sw_ref[1:2, :] + nxt * sw_ref[2:3, :] + sb_ref[...]
        z = u[:, 0:D_B]
        for n in range(HY_ORDER):
            spec = _dot(fwd_ref[...], z.astype(BF16))
            ure, uim = spec[0:seq_len], spec[seq_len:]
            gre, gim = g_ref[n, 0:seq_len, :], g_ref[n, seq_len:, :]
            pim = uim * gim
            yre = ure * gre - jnp.where(row0, 0.0, pim)
            yim = jnp.where(row0, pim, ure * gim + uim * gre)
            y = jnp.concatenate([yre, yim], axis=0).astype(BF16)
            z = u[:, (n + 1) * D_B:(n + 2) * D_B] * _dot(inv_ref[...], y)
        o_ref[rows, :] = z.astype(o_ref.dtype)


def hyena_mixer(u, nseq, seq_len, short_w, short_b, spectrum, fwd, inv):
    t = u.shape[0]
    width = (HY_ORDER + 1) * D_B
    col_block = (3 * D_A) // width
    assert col_block * width == 3 * D_A
    per_step = max(1, HY_STEP_ROWS // seq_len)
    assert nseq % per_step == 0
    return pl.pallas_call(
        functools.partial(_hyena_kernel, seq_len, per_step),
        grid=(nseq // per_step,),
        in_specs=[pl.BlockSpec((per_step * seq_len, width), lambda b: (b, col_block)),
                  pl.BlockSpec(short_w.shape, lambda b: (0, 0)),
                  pl.BlockSpec((1, width), lambda b: (0, 0)),
                  pl.BlockSpec(spectrum.shape, lambda b: (0, 0, 0)),
                  pl.BlockSpec(fwd.shape, lambda b: (0, 0)),
                  pl.BlockSpec(inv.shape, lambda b: (0, 0))],
        out_specs=pl.BlockSpec((per_step * seq_len, D_B), lambda b: (b, 0)),
        out_shape=jax.ShapeDtypeStruct((t, D_B), BF16),
        compiler_params=_cparams("arbitrary"),
        name="hyena_mixer",
    )(u, short_w, short_b.reshape(1, width), spectrum, fwd, inv)


RG_CB = LANES
RG_CHUNK = 512
TM_ROWS = 512


def _rglru_kernel(nseq, seq_len, gate_ref, xr_ref, cw_ref, cb_ref, wg_ref, bg_ref, lam_ref, h0_ref,
                  y_ref, fin_ref, xp_ref, a_f, b_f, a_b, b_b):
    t_tot = nseq * seq_len
    c = RG_CB
    pad = 2 * nseq
    xp_ref[0:pad, :] = jnp.zeros((pad, c), F32)
    xp_ref[pad + t_tot:, :] = jnp.zeros((pad, c), F32)
    xp_ref[pad:pad + t_tot, :] = xr_ref[...]
    nl = -lam_ref[...]
    sp = jnp.maximum(nl, 0.0) + jnp.log1p(jnp.exp(-jnp.abs(nl)))
    k2 = (-0.5 * RG_C * math.log2(math.e)) * sp

    def gate_chunk(ci, carry):
        r0 = pl.multiple_of(ci * RG_CHUNK, RG_CHUNK)
        xc = xp_ref[pl.ds(r0, RG_CHUNK), :] * cw_ref[0:1, :]
        for j in range(1, cw_ref.shape[0]):
            xc = xc + xp_ref[pl.ds(r0 + j * nseq, RG_CHUNK), :] * cw_ref[j:j + 1, :]
        xc = xc + cb_ref[...]
        gts = _dot(xc.astype(BF16), wg_ref[0]) + bg_ref[...]
        x_half = 0.5 * xc
        for d, (a_ref, b_ref) in enumerate(((a_f, b_f), (a_b, b_b))):
            t_r = jnp.tanh(gts[:, (2 * d) * c:(2 * d + 1) * c])
            t_i = jnp.tanh(gts[:, (2 * d + 1) * c:(2 * d + 2) * c])
            a = jnp.exp2(t_r * k2[d:d + 1, :] + k2[d:d + 1, :])
            a_ref[pl.ds(r0, RG_CHUNK), :] = a
            y = 1.0 - a * a
            root = jnp.where(y > 0.0, y * lax.rsqrt(y), 0.0)
            b_ref[pl.ds(r0, RG_CHUNK), :] = root * ((t_i + 1.0) * x_half)
        return carry

    lax.fori_loop(0, t_tot // RG_CHUNK, gate_chunk, 0)

    def scan_step(t, carry):
        hf, hb = carry
        rows_f = pl.ds(pl.multiple_of(t * nseq, nseq), nseq)
        rows_b = pl.ds(pl.multiple_of((seq_len - 1 - t) * nseq, nseq), nseq)
        hf = a_f[rows_f, :] * hf + b_f[rows_f, :]
        hb = a_b[rows_b, :] * hb + b_b[rows_b, :]
        b_f[rows_f, :] = hf
        b_b[rows_b, :] = hb
        return hf, hb

    hf, hb = lax.fori_loop(0, seq_len, scan_step, (h0_ref[0], h0_ref[1]), unroll=8)
    fin_ref[0] = hf
    fin_ref[1] = hb

    def out_chunk(ci, carry):
        rs = pl.ds(pl.multiple_of(ci * RG_CHUNK, RG_CHUNK), RG_CHUNK)
        y_ref[rs, :] = ((b_f[rs, :] + b_b[rs, :]) * jax.nn.gelu(gate_ref[rs, :])).astype(y_ref.dtype)
        return carry

    lax.fori_loop(0, t_tot // RG_CHUNK, out_chunk, 0)


def _rg_gate_weights(wa, wx):
    per_step = RG_CB // RG_BLOCK
    steps = D_RNN // RG_CB
    mats = []
    for d in range(2):
        for w in (wa[d], wx[d]):
            w = w.reshape(steps, per_step, RG_BLOCK, RG_BLOCK)
            eye = jnp.eye(per_step, dtype=w.dtype)
            m = jnp.einsum('spde,pq->spdqe', w, eye).reshape(steps, RG_CB, RG_CB)
            mats.append(m)
    return (0.5 * jnp.concatenate(mats, axis=-1)).astype(BF16)


def rglru_block(u, nseq, seq_len, conv_w, conv_b, wa, ba, wx, bx, lam, h0):
    t = u.shape[0]
    c = RG_CB
    steps = D_RNN // c
    wg = _rg_gate_weights(wa, wx)
    bg = jnp.stack([ba[0], bx[0], ba[1], bx[1]], axis=0).reshape(4, steps, c)
    bg = 0.5 * bg.transpose(1, 0, 2).reshape(steps, 1, 4 * c)
    y, fin = pl.pallas_call(
        functools.partial(_rglru_kernel, nseq, seq_len),
        grid=(steps,),
        in_specs=[pl.BlockSpec((t, c), lambda j: (0, j)),
                  pl.BlockSpec((t, c), lambda j: (0, steps + j)),
                  pl.BlockSpec((conv_w.shape[0], c), lambda j: (0, j)),
                  pl.BlockSpec((1, c), lambda j: (0, j)),
                  pl.BlockSpec((1, c, 4 * c), lambda j: (j, 0, 0)),
                  pl.BlockSpec((None, 1, 4 * c), lambda j: (j, 0, 0)),
                  pl.BlockSpec((2, c), lambda j: (0, j)),
                  pl.BlockSpec((2, nseq, c), lambda j: (0, 0, j))],
        out_specs=[pl.BlockSpec((t, c), lambda j: (0, j)),
                   pl.BlockSpec((2, nseq, c), lambda j: (0, 0, j))],
        out_shape=[jax.ShapeDtypeStruct((t, D_RNN), BF16),
                   jax.ShapeDtypeStruct((2, nseq, D_RNN), F32)],
        scratch_shapes=[pltpu.VMEM((t + 4 * nseq, c), F32)] + [pltpu.VMEM((t, c), F32)] * 4,
        compiler_params=_cparams("arbitrary"),
        name="rglru_block",
    )(u, u, conv_w, conv_b.reshape(1, -1), wg, bg, lam, h0)
    return y, fin


def _route_record(h, w, rb):
    h_hi = h.astype(BF16)
    h_lo = (h - h_hi.astype(F32)).astype(BF16)
    w_hi = w.astype(BF16)
    w_lo = (w - w_hi.astype(F32)).astype(BF16)
    logits = _dot_nt(w_hi, h_hi) + (_dot_nt(w_lo, h_hi) + _dot_nt(w_hi, h_lo))
    scores = jax.nn.sigmoid(logits)
    sel = scores + rb
    row = [sel[e:e + 1, :] for e in range(N_EXPERTS)]
    gs = []
    for g in range(N_GROUPS):
        r = row[g * EXPERTS_PER_GROUP:(g + 1) * EXPERTS_PER_GROUP]
        best_pair = None
        for i in range(EXPERTS_PER_GROUP):
            for j in range(i + 1, EXPERTS_PER_GROUP):
                s = r[i] + r[j]
                best_pair = s if best_pair is None else jnp.maximum(best_pair, s)
        gs.append(best_pair)
    best = jnp.zeros_like(gs[0], dtype=jnp.int32)
    top = gs[0]
    for g in range(1, N_GROUPS):
        better = gs[g] > top
        best = jnp.where(better, g, best)
        top = jnp.where(better, gs[g], top)
    picked = []
    for e in range(N_EXPERTS):
        g = e // EXPERTS_PER_GROUP
        rank = jnp.zeros_like(best)
        for o in range(g * EXPERTS_PER_GROUP, (g + 1) * EXPERTS_PER_GROUP):
            if o == e:
                continue
            ahead = (row[o] > row[e]) | ((row[o] == row[e]) & (o < e))
            rank = rank + ahead.astype(jnp.int32)
        picked.append((best == g) & (rank < 2))
    den = jnp.zeros_like(gs[0])
    for e in range(N_EXPERTS):
        den = den + jnp.where(picked[e], scores[e:e + 1, :], 0.0)
    gate = [jnp.where(picked[e], scores[e:e + 1, :] / den, 0.0) for e in range(N_EXPERTS)]
    cls = jnp.zeros_like(den)
    w_a = jnp.zeros_like(den)
    w_b = jnp.zeros_like(den)
    for g in range(N_GROUPS):
        for pi, (a, b) in enumerate(MOE_PAIRS):
            ea, eb = g * EXPERTS_PER_GROUP + a, g * EXPERTS_PER_GROUP + b
            both = picked[ea] & picked[eb]
            cls = jnp.where(both, float(g * len(MOE_PAIRS) + pi), cls)
            w_a = jnp.where(both, gate[ea], w_a)
            w_b = jnp.where(both, gate[eb], w_b)
    return jnp.concatenate([cls, w_a, w_b, jnp.zeros((ROUTE_ROWS - 3, cls.shape[1]), F32)], axis=0)


MOE_PAIRS = ((0, 1), (0, 2), (0, 3), (1, 3), (1, 2), (2, 3))
N_CLS = N_GROUPS * len(MOE_PAIRS)
CLS_PAD = 32
ROUTE_ROWS = 8
MOE_TS = 256
MOE_TM = 256
MOE_STEP_TILES = 4
SLOT_BLK = 512
ROW_W = D_MODEL + LANES


def _slots_kernel(n_blk, route_ref, slot_ref, off_ref, cnt_ref):
    cid = lax.broadcasted_iota(jnp.int32, (CLS_PAD, SLOT_BLK), 0).astype(F32)

    def members(j):
        cls = route_ref[0:1, pl.ds(pl.multiple_of(j * SLOT_BLK, SLOT_BLK), SLOT_BLK)]
        return (cid == cls).astype(F32)

    def count(j, cnt):
        return cnt + jnp.sum(members(j), axis=1, keepdims=True)

    cnt = lax.fori_loop(0, n_blk, count, jnp.zeros((CLS_PAD, 1), F32))
    cnt = jnp.broadcast_to(cnt, (CLS_PAD, LANES))
    padded = jnp.ceil(cnt * (1.0 / MOE_TS)) * MOE_TS
    r = lax.broadcasted_iota(jnp.int32, (CLS_PAD, CLS_PAD), 0)
    c = lax.broadcasted_iota(jnp.int32, (CLS_PAD, CLS_PAD), 1)
    off = jnp.dot((c < r).astype(F32), padded, precision=lax.Precision.HIGHEST, preferred_element_type=F32)
    off_ref[...] = off
    cnt_ref[...] = cnt
    tr = lax.broadcasted_iota(jnp.int32, (SLOT_BLK, SLOT_BLK), 0)
    tc = lax.broadcasted_iota(jnp.int32, (SLOT_BLK, SLOT_BLK), 1)
    earlier = (tr < tc).astype(BF16)

    def assign(j, base):
        member = members(j)
        rank = _dot(member.astype(BF16), earlier)
        slot = jnp.sum(member * (rank + base), axis=0, keepdims=True)
        slot_ref[0:1, pl.ds(pl.multiple_of(j * SLOT_BLK, SLOT_BLK), SLOT_BLK)] = slot.astype(jnp.int32)
        return base + jnp.sum(member, axis=1, keepdims=True)

    lax.fori_loop(0, n_blk, assign, off[:, 0:1])


def moe_slots(route):
    t = route.shape[1]
    stat = jax.ShapeDtypeStruct((CLS_PAD, LANES), F32)
    return pl.pallas_call(
        functools.partial(_slots_kernel, t // SLOT_BLK),
        out_shape=[jax.ShapeDtypeStruct((1, t), jnp.int32), stat, stat],
        compiler_params=pltpu.CompilerParams(vmem_limit_bytes=VMEM_LIMIT),
        name="moe_slots",
    )(route)


def _tile_maps(off, cnt, n_tiles):
    off = off[:N_CLS, 0].astype(jnp.int32)
    cnt = cnt[:N_CLS, 0].astype(jnp.int32)
    ends = off + ((cnt + MOE_TS - 1) // MOE_TS) * MOE_TS
    n_used = ends[-1] // MOE_TS
    k = jnp.arange(n_tiles, dtype=jnp.int32)
    tix = jnp.minimum(k, n_used - 1)
    cls = jnp.sum((tix[:, None] * MOE_TS >= ends[None, :]).astype(jnp.int32), axis=1)
    pair = jnp.asarray(MOE_PAIRS, jnp.int32)
    grp = (cls // len(MOE_PAIRS)) * EXPERTS_PER_GROUP
    ea = grp + pair[cls % len(MOE_PAIRS), 0]
    eb = grp + pair[cls % len(MOE_PAIRS), 1]
    n = jnp.int32(n_tiles)

    def slot_plan(e):
        chg = jnp.concatenate([jnp.ones((1,), jnp.int32), (e[1:] != e[:-1]).astype(jnp.int32)])
        at = jnp.where(chg == 1, k, n)
        nxt_at = jnp.concatenate([lax.cummin(at[::-1])[::-1][1:], n.reshape(1)])
        more = (nxt_at < n).astype(jnp.int32)
        nxt = e[jnp.minimum(nxt_at, n - 1)]
        par = (jnp.cumsum(chg) - 1) % 2
        return chg, nxt, more, par.astype(jnp.int32)

    plan_a, plan_b = slot_plan(ea), slot_plan(eb)
    chg, nxt, more, par = (jnp.stack([pa, pb]) for pa, pb in zip(plan_a, plan_b))
    return ea, eb, chg, nxt, more, par, n_used.reshape(1)


def _dispatch_kernel(n_steps, slots_ref, x_ref, g_ref, sc_ref, sh_ref, rt_ref, hs_in, hs_out, rowbuf, sem):
    del hs_in
    i = pl.program_id(0)
    cur = i % 2

    def wait_rows(s):
        pltpu.make_async_copy(rowbuf.at[s], rowbuf.at[s], sem.at[s]).wait()

    @pl.when(i >= 2)
    def _():
        wait_rows(cur)

    rowbuf[cur, :, 0:D_MODEL] = _normmod(x_ref[...], g_ref[...], sc_ref[0], sh_ref[0])
    rowbuf[cur, :, D_MODEL:ROW_W] = jnp.concatenate(
        [rt_ref[...], jnp.zeros((MOE_TM, LANES - ROUTE_ROWS), F32)], axis=1)

    for s in range(2):
        @pl.when(cur == s)
        def _():
            for r in range(MOE_TM):
                dst = slots_ref[i * MOE_TM + r]
                pltpu.make_async_copy(rowbuf.at[s, r], hs_out.at[dst], sem.at[s]).start()

    @pl.when(i == n_steps - 1)
    def _():
        wait_rows(cur)
        if n_steps >= 2:
            wait_rows(1 - cur)


def moe_dispatch(x, g, mod, route_t, slots, hs, seq_len, per_seq):
    t, d = x.shape
    n_steps = t // MOE_TM
    grid_spec = pltpu.PrefetchScalarGridSpec(
        num_scalar_prefetch=1,
        grid=(n_steps,),
        in_specs=[pl.BlockSpec((MOE_TM, d), lambda i, s: (i, 0)),
                  pl.BlockSpec((1, d), lambda i, s: (0, 0)),
                  _mod_spec(4, MOE_TM, seq_len, per_seq),
                  _mod_spec(3, MOE_TM, seq_len, per_seq),
                  pl.BlockSpec((MOE_TM, ROUTE_ROWS), lambda i, s: (i, 0)),
                  pl.BlockSpec(memory_space=pl.ANY)],
        out_specs=pl.BlockSpec(memory_space=pl.ANY),
        scratch_shapes=[pltpu.VMEM((2, MOE_TM, ROW_W), F32), pltpu.SemaphoreType.DMA((2,))],
    )
    return pl.pallas_call(
        functools.partial(_dispatch_kernel, n_steps),
        grid_spec=grid_spec,
        out_shape=jax.ShapeDtypeStruct(hs.shape, F32),
        input_output_aliases={6: 0},
        compiler_params=_cparams("arbitrary"),
        name="moe_dispatch",
    )(slots, x, g.reshape(1, d), mod, mod, route_t, hs)


def _experts_kernel(layer, ea_ref, eb_ref, chg_ref, nxt_ref, more_ref, par_ref, nused_ref,
                    hs_ref, wg_hbm, wu_hbm, wd_hbm, ys_ref, fg, fu, fd, bg, bu, bd, sem):
    def weight_copies(slot, expert, par):
        return [pltpu.make_async_copy(src.at[layer, expert], dst.at[slot, par], sem.at[slot, par])
                for src, dst in ((wg_hbm, fg), (wu_hbm, fu), (wd_hbm, fd))]

    def tile(k, rows):
        @pl.when(k < nused_ref[0])
        def _():
            for slot, e_ref in enumerate((ea_ref, eb_ref)):
                @pl.when(chg_ref[slot, k] == 1)
                def _():
                    par = par_ref[slot, k]

                    @pl.when(k == 0)
                    def _():
                        for cp in weight_copies(slot, e_ref[0], par):
                            cp.start()

                    for cp in weight_copies(slot, e_ref[k], par):
                        cp.wait()
                    bg[slot] = fg[slot, par].astype(BF16)
                    bu[slot] = fu[slot, par].astype(BF16)
                    bd[slot] = fd[slot, par].astype(BF16)

                    @pl.when(more_ref[slot, k] == 1)
                    def _():
                        for cp in weight_copies(slot, nxt_ref[slot, k], 1 - par):
                            cp.start()

            h = hs_ref[rows, 0:D_MODEL].astype(BF16)

            def ffn(slot):
                hid = _dot(h, bg[slot])
                up = _dot(h, bu[slot])
                w = hs_ref[rows, D_MODEL + 1 + slot:D_MODEL + 2 + slot]
                act = (hid * _sigmoid(hid)) * up * w
                return _dot(act.astype(BF16), bd[slot])

            ys_ref[rows, :] = ffn(0) + ffn(1)

        @pl.when(k >= nused_ref[0])
        def _():
            ys_ref[rows, :] = jnp.zeros((MOE_TS, ys_ref.shape[1]), ys_ref.dtype)

    for j in range(MOE_STEP_TILES):
        tile(pl.program_id(0) * MOE_STEP_TILES + j, slice(j * MOE_TS, (j + 1) * MOE_TS))


def moe_experts(hs, maps, layer, w_gate, w_up, w_down):
    n_tiles = hs.shape[0] // MOE_TS
    d = D_MODEL

    grid_spec = pltpu.PrefetchScalarGridSpec(
        num_scalar_prefetch=7,
        grid=(n_tiles // MOE_STEP_TILES,),
        in_specs=[pl.BlockSpec((MOE_STEP_TILES * MOE_TS, ROW_W), lambda k, *_: (k, 0)),
                  pl.BlockSpec(memory_space=pl.ANY), pl.BlockSpec(memory_space=pl.ANY),
                  pl.BlockSpec(memory_space=pl.ANY)],
        out_specs=pl.BlockSpec((MOE_STEP_TILES * MOE_TS, d), lambda k, *_: (k, 0)),
        scratch_shapes=[pltpu.VMEM((2, 2, d, D_EXPERT), F32), pltpu.VMEM((2, 2, d, D_EXPERT), F32),
                        pltpu.VMEM((2, 2, D_EXPERT, d), F32),
                        pltpu.VMEM((2, d, D_EXPERT), BF16), pltpu.VMEM((2, d, D_EXPERT), BF16),
                        pltpu.VMEM((2, D_EXPERT, d), BF16),
                        pltpu.SemaphoreType.DMA((2, 2))],
    )
    return pl.pallas_call(
        functools.partial(_experts_kernel, layer),
        grid_spec=grid_spec,
        out_shape=jax.ShapeDtypeStruct((hs.shape[0], d), F32),
        compiler_params=_cparams("arbitrary"),
        name="moe_experts",
    )(*maps, hs, w_gate, w_up, w_down)


def _combine_kernel(final, n_steps, slots_ref, x_ref, g2_ref, fg_ref, ys_hbm, o_ref, gbuf, sem):
    i = pl.program_id(0)
    cur = i % 2

    def issue_tile(tile, s):
        for r in range(MOE_TM):
            src = slots_ref[tile * MOE_TM + r]
            pltpu.make_async_copy(ys_hbm.at[src], gbuf.at[s, r], sem.at[s]).start()

    @pl.when(i == 0)
    def _():
        issue_tile(0, 0)

    for s in range(2):
        @pl.when((i + 1 < n_steps) & (1 - cur == s))
        def _():
            issue_tile(i + 1, s)

    pltpu.make_async_copy(gbuf.at[cur], gbuf.at[cur], sem.at[cur]).wait()
    y = x_ref[...] + g2_ref[0] * gbuf[cur]
    if final:
        ms = jnp.mean(y * y, axis=-1, keepdims=True)
        y = y * lax.rsqrt(ms + EPS) * fg_ref[...]
    o_ref[...] = y


def moe_combine(x, mod, slots, ys, final_g, final, seq_len, per_seq):
    t, d = x.shape
    n_steps = t // MOE_TM
    grid_spec = pltpu.PrefetchScalarGridSpec(
        num_scalar_prefetch=1,
        grid=(n_steps,),
        in_specs=[pl.BlockSpec((MOE_TM, d), lambda i, s: (i, 0)),
                  _mod_spec(5, MOE_TM, seq_len, per_seq),
                  pl.BlockSpec((1, d), lambda i, s: (0, 0)),
                  pl.BlockSpec(memory_space=pl.ANY)],
        out_specs=pl.BlockSpec((MOE_TM, d), lambda i, s: (i, 0)),
        scratch_shapes=[pltpu.VMEM((2, MOE_TM, d), F32), pltpu.SemaphoreType.DMA((2,))],
    )
    return pl.pallas_call(
        functools.partial(_combine_kernel, final, n_steps),
        grid_spec=grid_spec,
        out_shape=jax.ShapeDtypeStruct((t, d), F32),
        compiler_params=_cparams("arbitrary"),
        name="moe_combine",
    )(slots, x, mod, final_g.reshape(1, d), ys)


SC_WINDOW = 128
SC_COLS = 256


def sc_gather_rows(table, idx):
    n = idx.shape[0]
    d = table.shape[1]
    mesh = plsc.VectorSubcoreMesh(core_axis_name="core", subcore_axis_name="subcore")

    @pl.kernel(out_type=jax.ShapeDtypeStruct((n, d), table.dtype), mesh=mesh)
    def gather(x_hbm, i_hbm, o_hbm):
        for j in range(d // SC_COLS):
            def body(i_vmem, o_vmem, j=j):
                pltpu.sync_copy(x_hbm.at[i_vmem.at[0], pl.ds(j * SC_COLS, SC_COLS)], o_vmem)

            per_core = n // SC_WINDOW // mesh.num_cores
            pltpu.emit_pipeline(
                body,
                grid=(mesh.num_cores, per_core),
                in_specs=[pl.BlockSpec((1, SC_WINDOW), lambda c, i: (0, c * per_core + i))],
                out_specs=[pl.BlockSpec((SC_WINDOW, SC_COLS), lambda c, i, j=j: (c * per_core + i, j))],
                core_axis_name=("core", "subcore"),
                dimension_semantics=(pltpu.PARALLEL, pltpu.PARALLEL),
            )(i_hbm, o_hbm)

    return gather(table, idx.reshape(1, n))


def sc_zeros(rows, width):
    mesh = plsc.VectorSubcoreMesh(core_axis_name="core", subcore_axis_name="subcore")
    per = rows // (mesh.num_cores * mesh.num_subcores)
    assert per * mesh.num_cores * mesh.num_subcores == rows

    @pl.kernel(out_type=jax.ShapeDtypeStruct((rows, width), F32), mesh=mesh)
    def fill(z_hbm, o_hbm):
        w = lax.axis_index("core") * mesh.num_subcores + lax.axis_index("subcore")
        pltpu.sync_copy(z_hbm, o_hbm.at[pl.ds(w * per, per)])

    return fill(jnp.zeros((per, width), F32))


def _res_nm_tm_kernel(nseq, steps, x_ref, y_ref, g2_ref, g_ref, sc_ref, sh_ref, w_ref, x2_ref, u_ref):
    d = x_ref.shape[-1]
    y = x_ref[...] + g2_ref[...] * y_ref[...]
    x2_ref[...] = y
    h = _normmod(y, g_ref[...], sc_ref[...], sh_ref[...])
    h = h.reshape(nseq * steps, d).astype(BF16)
    h = _dot(_row_permutation(nseq, steps, True), h).astype(BF16)
    u_ref[...] = _dot(h, w_ref[...])


def residual_nm_matmul_tm(x, mod_prev, y, g, mod, w, nseq, seq_len, per_seq):
    t, d = x.shape
    n = w.shape[1]
    steps = TM_ROWS // nseq
    g2 = _group_mod(mod_prev, 5, nseq, per_seq)
    sc = _group_mod(mod, 1, nseq, per_seq)
    sh = _group_mod(mod, 0, nseq, per_seq)
    mod_spec = pl.BlockSpec(sc.shape, lambda i: (0, 0, 0))
    x_spec = pl.BlockSpec((nseq, steps, d), lambda i: (0, i, 0))
    x2, u = pl.pallas_call(
        functools.partial(_res_nm_tm_kernel, nseq, steps),
        grid=(seq_len // steps,),
        in_specs=[x_spec, x_spec, mod_spec, pl.BlockSpec((1, d), lambda i: (0, 0)), mod_spec, mod_spec,
                  pl.BlockSpec((d, n), lambda i: (0, 0))],
        out_specs=[x_spec, pl.BlockSpec((TM_ROWS, n), lambda i: (i, 0))],
        out_shape=[jax.ShapeDtypeStruct((nseq, seq_len, d), F32), jax.ShapeDtypeStruct((t, n), F32)],
        compiler_params=_cparams("arbitrary"),
        name="moe_residual_norm_mod_proj_tm",
    )(x.reshape(nseq, seq_len, d), y.reshape(nseq, seq_len, d), g2, g.reshape(1, d), sc, sh, w)
    return x2.reshape(t, d), u


def _combine_nm_tm_kernel(nseq, steps, seq_len, n_steps, slots_ref, x_ref, g2_ref, ys_hbm, g_ref, sc_ref,
                          sh_ref, w_ref, x2_ref, u_ref, gbuf, sem):
    i = pl.program_id(0)
    cur = i % 2

    def issue_tile(tile, s):
        for sq in range(nseq):
            for t in range(steps):
                src = slots_ref[sq * seq_len + tile * steps + t]
                pltpu.make_async_copy(ys_hbm.at[src], gbuf.at[s, sq * steps + t], sem.at[s]).start()

    @pl.when(i == 0)
    def _():
        issue_tile(0, 0)

    for s in range(2):
        @pl.when((i + 1 < n_steps) & (1 - cur == s))
        def _():
            issue_tile(i + 1, s)

    pltpu.make_async_copy(gbuf.at[cur], gbuf.at[cur], sem.at[cur]).wait()
    d = x_ref.shape[-1]
    y = x_ref[...] + g2_ref[...] * gbuf[cur].reshape(nseq, steps, d)
    x2_ref[...] = y
    h = _normmod(y, g_ref[...], sc_ref[...], sh_ref[...])
    h = h.reshape(nseq * steps, d).astype(BF16)
    h = _dot(_row_permutation(nseq, steps, True), h).astype(BF16)
    u_ref[...] = _dot(h, w_ref[...])


def combine_nm_matmul_tm(x, mod_prev, slots, ys, g, mod, w, nseq, seq_len, per_seq):
    t, d = x.shape
    n = w.shape[1]
    steps = TM_ROWS // nseq
    n_steps = seq_len // steps
    g2 = _group_mod(mod_prev, 5, nseq, per_seq)
    sc = _group_mod(mod, 1, nseq, per_seq)
    sh = _group_mod(mod, 0, nseq, per_seq)
    mod_spec = pl.BlockSpec(sc.shape, lambda i, s: (0, 0, 0))
    x_spec = pl.BlockSpec((nseq, steps, d), lambda i, s: (0, i, 0))
    grid_spec = pltpu.PrefetchScalarGridSpec(
        num_scalar_prefetch=1,
        grid=(n_steps,),
        in_specs=[x_spec, mod_spec, pl.BlockSpec(memory_space=pl.ANY),
                  pl.BlockSpec((1, d), lambda i, s: (0, 0)), mod_spec, mod_spec,
                  pl.BlockSpec((d, n), lambda i, s: (0, 0))],
        out_specs=[x_spec, pl.BlockSpec((TM_ROWS, n), lambda i, s: (i, 0))],
        scratch_shapes=[pltpu.VMEM((2, TM_ROWS, d), F32), pltpu.SemaphoreType.DMA((2,))],
    )
    x2, u = pl.pallas_call(
        functools.partial(_combine_nm_tm_kernel, nseq, steps, seq_len, n_steps),
        grid_spec=grid_spec,
        out_shape=[jax.ShapeDtypeStruct((nseq, seq_len, d), F32), jax.ShapeDtypeStruct((t, n), F32)],
        compiler_params=_cparams("arbitrary"),
        name="moe_combine_norm_mod_proj_tm",
    )(slots, x.reshape(nseq, seq_len, d), g2, ys, g.reshape(1, d), sc, sh, w)
    return x2.reshape(t, d), u


def moe_block(xs, routes, mods_l, per_seqs, seq_lens, layer, p, final, defer_combine, hs):
    g = p['norm_g'][layer, 1]
    slots, off, cnt = moe_slots(jnp.concatenate(routes, axis=1))
    t_all = slots.shape[1]
    n_tiles = -(-(t_all // MOE_TS + N_CLS) // MOE_STEP_TILES) * MOE_STEP_TILES
    maps = _tile_maps(off, cnt, n_tiles)
    if hs is None:
        hs = sc_zeros(n_tiles * MOE_TS, ROW_W)
    bounds = np.cumsum([0] + [x.shape[0] for x in xs])
    group_slots = [slots[0, bounds[i]:bounds[i + 1]] for i in range(len(xs))]
    for x, r, s, sl, ps in zip(xs, routes, group_slots, seq_lens, per_seqs):
        hs = moe_dispatch(x, g, mods_l, r.T, s, hs, sl, ps)
    ys = moe_experts(hs, maps, layer, p['moe_w_gate'], p['moe_w_up'], p['moe_w_down'])
    if defer_combine:
        return [(x, s, ys) for x, s in zip(xs, group_slots)], hs
    return [moe_combine(x, mods_l, s, ys, p['final_g'], final, sl, ps)
            for x, s, sl, ps in zip(xs, group_slots, seq_lens, per_seqs)], hs


def _mixer(x, group, l, mod, p, hy_tables, mod_prev):
    per_seq, nseq, seq_len = group['per_seq'], group['nseq'], group['seq_len']
    extras = None
    deferred = isinstance(x, tuple)
    if deferred and l % 2 == 0:
        x = moe_combine(x[0], mod_prev, x[1], x[2], p['final_g'], False, seq_len, per_seq)
        deferred = False
    if l % 2 == 0:
        e = l // 2
        u = nm_matmul(x, p['norm_g'][l, 0], mod, p['a_in_w'][e].astype(BF16), seq_len, per_seq)
        if group['ctx_k'] is None:
            attn, nk, nv = context_attention(u, nseq, seq_len)
            extras = (nk, nv)
        else:
            attn = neighbourhood_attention(u, group['ctx_k'][:, e], group['ctx_v'][:, e], p['na_rpb'][e],
                                           nseq, seq_len)
        fwd, inv = hy_tables[seq_len]
        spectrum = hyena_spectrum(seq_len, p['hy_w1'][e], p['hy_b1'][e], p['hy_w2'][e], p['hy_b2'][e],
                                  p['hy_w3'][e], p['hy_freq'][e], p['hy_d'][e], fwd)
        hy = hyena_mixer(u, nseq, seq_len, p['hy_short_w'][e], p['hy_short_b'][e], spectrum, fwd, inv)
        w_out = p['a_out_w'][e].astype(BF16)
        x, route = proj_residual([attn, hy], [w_out[:D_A], w_out[D_A:]], x, mod, p['norm_g'][l, 1],
                                 p['router_w'], p['router_b'], seq_len, per_seq)
    else:
        o = l // 2
        w_in = p['c_in_w'][o].astype(BF16)
        if deferred and group['sc_combine']:
            idx = x[1]
            if p.get('sc_after') is not None:
                idx, _ = lax.optimization_barrier((idx, p['sc_after']))
            x, u = residual_nm_matmul_tm(x[0], mod_prev, sc_gather_rows(x[2], idx), p['norm_g'][l, 0], mod,
                                         w_in, nseq, seq_len, per_seq)
        elif deferred:
            x, u = combine_nm_matmul_tm(x[0], mod_prev, x[1], x[2], p['norm_g'][l, 0], mod, w_in,
                                        nseq, seq_len, per_seq)
            p['sc_after'] = u
        else:
            u = nm_matmul_tm(x, p['norm_g'][l, 0], mod, w_in, nseq, seq_len, per_seq)
        y, extras = rglru_block(u, nseq, seq_len, p['rg_conv_w'][o], p['rg_conv_b'][o], p['rg_wa'][o],
                                p['rg_ba'][o], p['rg_wx'][o], p['rg_bx'][o], p['rg_lam'][o], group['h0'][o])
        x, route = proj_residual_tm(y, p['c_out_w'][o].astype(BF16), x, mod, p['norm_g'][l, 1],
                                    p['router_w'], p['router_b'], nseq, seq_len, per_seq)
    return x, extras, route


def kernel(x_prompt, x_sample, cache_k, cache_v, state_h, c, c_ctx, norm_g, ada_w, ada_b, final_g, a_in_w, a_out_w, na_rpb, hy_short_w, hy_short_b, hy_w1, hy_b1, hy_w2, hy_b2, hy_w3, hy_freq, hy_d, c_in_w, c_out_w, rg_conv_w, rg_conv_b, rg_wa, rg_ba, rg_wx, rg_bx, rg_lam, router_w, router_b, moe_w_gate, moe_w_up, moe_w_down):
    p = dict(norm_g=norm_g, final_g=final_g, a_in_w=a_in_w, a_out_w=a_out_w, na_rpb=na_rpb,
             hy_short_w=hy_short_w, hy_short_b=hy_short_b, hy_w1=hy_w1, hy_b1=hy_b1, hy_w2=hy_w2,
             hy_b2=hy_b2, hy_w3=hy_w3, hy_freq=hy_freq, hy_d=hy_d, c_in_w=c_in_w, c_out_w=c_out_w,
             rg_conv_w=rg_conv_w, rg_conv_b=rg_conv_b, rg_wa=rg_wa, rg_ba=rg_ba, rg_wx=rg_wx, rg_bx=rg_bx,
             rg_lam=rg_lam, router_w=router_w, router_b=router_b, moe_w_gate=moe_w_gate,
             moe_w_up=moe_w_up, moe_w_down=moe_w_down)
    batch, seq, d = x_prompt.shape
    dec_batch, dec_seq, _ = x_sample.shape
    n_odd = DEPTH // 2
    assert 1 + dec_batch <= MOD_ROWS

    cond = jnp.concatenate([c_ctx[None, :], c, jnp.zeros((MOD_ROWS - 1 - dec_batch, d), F32)], axis=0)
    m = modulation(cond, ada_w, ada_b)
    mods = [m[l].reshape(MOD_ROWS * N_MOD, 1, d) for l in range(DEPTH)]

    tables = {}
    for sl in (seq, dec_seq):
        fwd, inv = _dft_tables(sl)
        tables[sl] = (jnp.asarray(fwd).astype(BF16), jnp.asarray(inv).astype(BF16))

    groups = [
        dict(per_seq=False, nseq=batch, seq_len=seq, ctx_k=None, ctx_v=None, sc_combine=False,
             h0=[jnp.zeros((2, batch, D_RNN), F32)] * n_odd),
        dict(per_seq=True, nseq=dec_batch, seq_len=dec_seq, ctx_k=cache_k, ctx_v=cache_v, sc_combine=True,
             h0=[state_h[:, o].transpose(1, 0, 2) for o in range(n_odd)]),
    ]
    xs = [x_prompt.reshape(batch * seq, d), x_sample.reshape(dec_batch * dec_seq, d)]
    k_list, v_list, h_list = [], [], []
    hs = None
    for l in range(DEPTH):
        mixed = [_mixer(x, grp, l, mods[l], p, tables, mods[l - 1] if l else None)
                 for x, grp in zip(xs, groups)]
        if l % 2 == 0:
            k_list.append(mixed[0][1][0])
            v_list.append(mixed[0][1][1])
        else:
            h_list.append(mixed[0][1].transpose(1, 0, 2))
        xs, hs = moe_block([mx[0] for mx in mixed], [mx[2] for mx in mixed], mods[l],
                           [grp['per_seq'] for grp in groups], [grp['seq_len'] for grp in groups],
                           l, p, l == DEPTH - 1, l < DEPTH - 1, hs)
    new_k = jnp.stack(k_list, axis=1)
    new_v = jnp.stack(v_list, axis=1)
    new_h = jnp.stack(h_list, axis=1)
    return (xs[0].reshape(batch, seq, d), xs[1].reshape(dec_batch, dec_seq, d), new_k, new_v, new_h)
```
